```python
import math
import jax, jax.numpy as jnp
from jax import lax
import numpy as np

D_MODEL = 4096
BATCH = 8
SEQ = 4096
DEPTH = 1

N_META = 16
D_FF = 11008
D_RGLRU = D_MODEL // 2
RG_HEADS = 8
RG_HEAD_DIM = D_RGLRU // RG_HEADS
CONV_WIDTH = 4
RG_C = 8.0
D_S5 = D_MODEL - D_RGLRU
S5_GROUP = 16
S5_GROUPS = D_S5 // S5_GROUP
S5_STATE = 64
DT_MIN = 0.001
DT_MAX = 0.1
D_MIX = D_RGLRU + D_S5
D_IN_PROJ = 2 * D_RGLRU + D_S5
EPS = 1e-6

kernel_name = "hymba_rglru_s5_macaron_layer"


def rmsnorm(x, g):
    xf = x.astype(jnp.float32)
    y = xf * lax.rsqrt(jnp.mean(xf * xf, axis=-1, keepdims=True) + EPS)
    return (y * g.astype(jnp.float32)).astype(x.dtype)


def swiglu(h, w_gate, w_up, w_down):
    return (jax.nn.silu(h @ w_gate) * (h @ w_up)) @ w_down


def rg_lru_mixer(u, gate, conv_w, conv_b, w_a, b_a, w_x, b_x, lam):
    bsz, t_len, _ = u.shape
    up = jnp.pad(u, ((0, 0), (CONV_WIDTH - 1, 0), (0, 0)))
    xc = conv_b + sum(up[:, k:k + t_len] * conv_w[k] for k in range(CONV_WIDTH))
    xh = xc.reshape(bsz, t_len, RG_HEADS, RG_HEAD_DIM)
    r = jax.nn.sigmoid(jnp.einsum('bthi,hij->bthj', xh, w_a) + b_a).reshape(bsz, t_len, D_RGLRU)
    i = jax.nn.sigmoid(jnp.einsum('bthi,hij->bthj', xh, w_x) + b_x).reshape(bsz, t_len, D_RGLRU)
    log_a = -RG_C * r.astype(jnp.float32) * jax.nn.softplus(-lam.astype(jnp.float32))
    a = jnp.exp(log_a)
    mult = jnp.sqrt(-jnp.expm1(2.0 * log_a))
    bx = mult * i.astype(jnp.float32) * xc.astype(jnp.float32)

    def step(h, ab):
        a_t, b_t = ab
        h = a_t * h + b_t
        return h, h

    _, hs = lax.scan(step, jnp.zeros((bsz, D_RGLRU), jnp.float32),
                     (jnp.swapaxes(a, 0, 1), jnp.swapaxes(bx, 0, 1)))
    h = jnp.swapaxes(hs, 0, 1)
    return (h * jax.nn.gelu(gate.astype(jnp.float32))).astype(u.dtype)


def s5_mixer(u, lam_re, lam_im, log_dt, b_re, b_im, c_re, c_im, d, glu_w, glu_b):
    bsz, t_len, _ = u.shape
    f32 = jnp.float32
    dt = jnp.exp(log_dt.astype(f32))[:, None]
    lam = lax.complex(lam_re.astype(f32), lam_im.astype(f32))
    lam_bar = jnp.exp(lam * dt)
    b = lax.complex(b_re.astype(f32), b_im.astype(f32))
    b_bar = ((lam_bar - 1.0) / lam)[..., None] * b
    ug = u.astype(f32).reshape(bsz, t_len, S5_GROUPS, S5_GROUP)
    bu = jnp.einsum('btgc,gnc->btgn', ug, b_bar)
    a_elems = jnp.broadcast_to(lam_bar[None, None], (1, t_len, S5_GROUPS, S5_STATE))

    def combine(e_i, e_j):
        a_i, b_i = e_i
        a_j, b_j = e_j
        return (a_j * a_i, a_j * b_i + b_j)

    _, states = lax.associative_scan(combine, (a_elems, bu), axis=1)
    c = lax.complex(c_re.astype(f32), c_im.astype(f32))
    y = jnp.einsum('btgn,gcn->btgc', states, c).real
    y = y + d.astype(f32).reshape(S5_GROUPS, S5_GROUP) * ug
    y = y.reshape(bsz, t_len, D_S5).astype(u.dtype)
    z = jax.nn.gelu(y)
    return z * jax.nn.sigmoid(z @ glu_w + glu_b)


def _fwd_setup_inputs(seed: int = 0) -> dict:
    key = jax.random.key(seed)
    ks = jax.random.split(key, 40)
    f32 = jnp.float32
    nrm = lambda k, shape, s: jax.random.normal(k, shape, f32) * s
    L = DEPTH
    u = jax.random.uniform(ks[30], (L, D_RGLRU), f32, 0.9, 0.999)
    rg_lambda = jnp.log(u ** (1.0 / RG_C)) - jnp.log1p(-(u ** (1.0 / RG_C)))
    n_idx = jnp.arange(S5_STATE, dtype=f32)
    s5_lambda_re = -0.5 + nrm(ks[31], (L, S5_GROUPS, S5_STATE), 0.01)
    s5_lambda_im = math.pi * n_idx + nrm(ks[32], (L, S5_GROUPS, S5_STATE), 0.01)
    s5_log_dt = jax.random.uniform(ks[33], (L, S5_GROUPS), f32, math.log(DT_MIN), math.log(DT_MAX))
    return {
        "x": nrm(ks[0], (BATCH, SEQ, D_MODEL), 1.0),
        "meta_tokens": nrm(ks[1], (N_META, D_MODEL), 1.0),
        "ffn1_norm": 1.0 + nrm(ks[2], (L, D_MODEL), 0.02),
        "ffn1_w_gate": nrm(ks[3], (L, D_MODEL, D_FF), D_MODEL ** -0.5),
        "ffn1_w_up": nrm(ks[4], (L, D_MODEL, D_FF), D_MODEL ** -0.5),
        "ffn1_w_down": nrm(ks[5], (L, D_FF, D_MODEL), D_FF ** -0.5),
        "mix_norm": 1.0 + nrm(ks[6], (L, D_MODEL), 0.02),
        "w_in": nrm(ks[7], (L, D_MODEL, D_IN_PROJ), D_MODEL ** -0.5),
        "rg_conv_w": nrm(ks[8], (L, CONV_WIDTH, D_RGLRU), CONV_WIDTH ** -0.5),
        "rg_conv_b": nrm(ks[9], (L, D_RGLRU), 0.01),
        "rg_w_a": nrm(ks[10], (L, RG_HEADS, RG_HEAD_DIM, RG_HEAD_DIM), RG_HEAD_DIM ** -0.5),
        "rg_b_a": nrm(ks[11], (L, RG_HEADS, RG_HEAD_DIM), 0.01),
        "rg_w_x": nrm(ks[12], (L, RG_HEADS, RG_HEAD_DIM, RG_HEAD_DIM), RG_HEAD_DIM ** -0.5),
        "rg_b_x": nrm(ks[13], (L, RG_HEADS, RG_HEAD_DIM), 0.01),
        "rg_lambda": rg_lambda,
        "s5_lambda_re": s5_lambda_re,
        "s5_lambda_im": s5_lambda_im,
        "s5_log_dt": s5_log_dt,
        "s5_b_re": nrm(ks[14], (L, S5_GROUPS, S5_STATE, S5_GROUP), (2.0 * S5_GROUP) ** -0.5),
        "s5_b_im": nrm(ks[15], (L, S5_GROUPS, S5_STATE, S5_GROUP), (2.0 * S5_GROUP) ** -0.5),
        "s5_c_re": nrm(ks[16], (L, S5_GROUPS, S5_GROUP, S5_STATE), 1.0),
        "s5_c_im": nrm(ks[17], (L, S5_GROUPS, S5_GROUP, S5_STATE), 1.0),
        "s5_d": nrm(ks[18], (L, D_S5), 0.5),
        "s5_glu_w": nrm(ks[19], (L, D_S5, D_S5), D_S5 ** -0.5),
        "s5_glu_b": nrm(ks[20], (L, D_S5), 0.01),
        "rg_out_norm": 1.0 + nrm(ks[21], (L, D_RGLRU), 0.02),
        "s5_out_norm": 1.0 + nrm(ks[22], (L, D_S5), 0.02),
        "w_out": nrm(ks[23], (L, D_MIX, D_MODEL), D_MIX ** -0.5),
        "ffn2_norm": 1.0 + nrm(ks[24], (L, D_MODEL), 0.02),
        "ffn2_w_gate": nrm(ks[25], (L, D_MODEL, D_FF), D_MODEL ** -0.5),
        "ffn2_w_up": nrm(ks[26], (L, D_MODEL, D_FF), D_MODEL ** -0.5),
        "ffn2_w_down": nrm(ks[27], (L, D_FF, D_MODEL), D_FF ** -0.5),
        "final_norm": 1.0 + nrm(ks[28], (D_MODEL,), 0.02),
    }


def _fwd_reference(x, meta_tokens, ffn1_norm, ffn1_w_gate, ffn1_w_up, ffn1_w_down, mix_norm, w_in,
              rg_conv_w, rg_conv_b, rg_w_a, rg_b_a, rg_w_x, rg_b_x, rg_lambda,
              s5_lambda_re, s5_lambda_im, s5_log_dt, s5_b_re, s5_b_im, s5_c_re, s5_c_im, s5_d,
              s5_glu_w, s5_glu_b, rg_out_norm, s5_out_norm, w_out,
              ffn2_norm, ffn2_w_gate, ffn2_w_up, ffn2_w_down, final_norm):
    bsz = x.shape[0]
    meta = jnp.broadcast_to(meta_tokens[None].astype(x.dtype), (bsz, N_META, D_MODEL))
    h = jnp.concatenate([meta, x], axis=1)
    for l in range(DEPTH):
        h = h + 0.5 * swiglu(rmsnorm(h, ffn1_norm[l]), ffn1_w_gate[l], ffn1_w_up[l], ffn1_w_down[l])
        proj = rmsnorm(h, mix_norm[l]) @ w_in[l]
        u_rg, g_rg, u_s5 = jnp.split(proj, [D_RGLRU, 2 * D_RGLRU], axis=-1)
        y_rg = rg_lru_mixer(u_rg, g_rg, rg_conv_w[l], rg_conv_b[l], rg_w_a[l], rg_b_a[l],
                            rg_w_x[l], rg_b_x[l], rg_lambda[l])
        y_s5 = s5_mixer(u_s5, s5_lambda_re[l], s5_lambda_im[l], s5_log_dt[l], s5_b_re[l], s5_b_im[l],
                        s5_c_re[l], s5_c_im[l], s5_d[l], s5_glu_w[l], s5_glu_b[l])
        y = jnp.concatenate([rmsnorm(y_rg, rg_out_norm[l]), rmsnorm(y_s5, s5_out_norm[l])], axis=-1)
        h = h + y @ w_out[l]
        h = h + 0.5 * swiglu(rmsnorm(h, ffn2_norm[l]), ffn2_w_gate[l], ffn2_w_up[l], ffn2_w_down[l])
    out = rmsnorm(h, final_norm)
    return out[:, N_META:]


import jax as _jax
import jax.numpy as _jnp

TWIN_FORMAT = 'train_step'
FWD_PARAMS = ['x', 'meta_tokens', 'ffn1_norm', 'ffn1_w_gate', 'ffn1_w_up', 'ffn1_w_down', 'mix_norm', 'w_in', 'rg_conv_w', 'rg_conv_b', 'rg_w_a', 'rg_b_a', 'rg_w_x', 'rg_b_x', 'rg_lambda', 's5_lambda_re', 's5_lambda_im', 's5_log_dt', 's5_b_re', 's5_b_im', 's5_c_re', 's5_c_im', 's5_d', 's5_glu_w', 's5_glu_b', 'rg_out_norm', 's5_out_norm', 'w_out', 'ffn2_norm', 'ffn2_w_gate', 'ffn2_w_up', 'ffn2_w_down', 'final_norm']
TWIN_WEIGHTS = ['meta_tokens', 'ffn1_norm', 'ffn1_w_gate', 'ffn1_w_up', 'ffn1_w_down', 'mix_norm', 'w_in', 'rg_conv_w', 'rg_conv_b', 'rg_w_a', 'rg_b_a', 'rg_w_x', 'rg_b_x', 'rg_lambda', 's5_lambda_re', 's5_lambda_im', 's5_log_dt', 's5_b_re', 's5_b_im', 's5_c_re', 's5_c_im', 's5_d', 's5_glu_w', 's5_glu_b', 'rg_out_norm', 's5_out_norm', 'w_out', 'ffn2_norm', 'ffn2_w_gate', 'ffn2_w_up', 'ffn2_w_down', 'final_norm']
TWIN_DIFF_INPUT = 'x'
TWIN_INPUTS = ['x', 'meta_tokens', 'ffn1_norm', 'ffn1_w_gate', 'ffn1_w_up', 'ffn1_w_down', 'mix_norm', 'w_in', 'rg_conv_w', 'rg_conv_b', 'rg_w_a', 'rg_b_a', 'rg_w_x', 'rg_b_x', 'rg_lambda', 's5_lambda_re', 's5_lambda_im', 's5_log_dt', 's5_b_re', 's5_b_im', 's5_c_re', 's5_c_im', 's5_d', 's5_glu_w', 's5_glu_b', 'rg_out_norm', 's5_out_norm', 'w_out', 'ffn2_norm', 'ffn2_w_gate', 'ffn2_w_up', 'ffn2_w_down', 'final_norm', 'loss_target', 'm_meta_tokens', 'm_ffn1_norm', 'm_ffn1_w_gate', 'm_ffn1_w_up', 'm_ffn1_w_down', 'm_mix_norm', 'm_w_in', 'm_rg_conv_w', 'm_rg_conv_b', 'm_rg_w_a', 'm_rg_b_a', 'm_rg_w_x', 'm_rg_b_x', 'm_rg_lambda', 'm_s5_lambda_re', 'm_s5_lambda_im', 'm_s5_log_dt', 'm_s5_b_re', 'm_s5_b_im', 'm_s5_c_re', 'm_s5_c_im', 'm_s5_d', 'm_s5_glu_w', 'm_s5_glu_b', 'm_rg_out_norm', 'm_s5_out_norm', 'm_w_out', 'm_ffn2_norm', 'm_ffn2_w_gate', 'm_ffn2_w_up', 'm_ffn2_w_down', 'm_final_norm', 'v_meta_tokens', 'v_ffn1_norm', 'v_ffn1_w_gate', 'v_ffn1_w_up', 'v_ffn1_w_down', 'v_mix_norm', 'v_w_in', 'v_rg_conv_w', 'v_rg_conv_b', 'v_rg_w_a', 'v_rg_b_a', 'v_rg_w_x', 'v_rg_b_x', 'v_rg_lambda', 'v_s5_lambda_re', 'v_s5_lambda_im', 'v_s5_log_dt', 'v_s5_b_re', 'v_s5_b_im', 'v_s5_c_re', 'v_s5_c_im', 'v_s5_d', 'v_s5_glu_w', 'v_s5_glu_b', 'v_rg_out_norm', 'v_s5_out_norm', 'v_w_out', 'v_ffn2_norm', 'v_ffn2_w_gate', 'v_ffn2_w_up', 'v_ffn2_w_down', 'v_final_norm']
TWIN_OUTPUTS = ['loss', 'grad_x', 'grad_meta_tokens', 'grad_ffn1_norm', 'grad_ffn1_w_gate', 'grad_ffn1_w_up', 'grad_ffn1_w_down', 'grad_mix_norm', 'grad_w_in', 'grad_rg_conv_w', 'grad_rg_conv_b', 'grad_rg_w_a', 'grad_rg_b_a', 'grad_rg_w_x', 'grad_rg_b_x', 'grad_rg_lambda', 'grad_s5_lambda_re', 'grad_s5_lambda_im', 'grad_s5_log_dt', 'grad_s5_b_re', 'grad_s5_b_im', 'grad_s5_c_re', 'grad_s5_c_im', 'grad_s5_d', 'grad_s5_glu_w', 'grad_s5_glu_b', 'grad_rg_out_norm', 'grad_s5_out_norm', 'grad_w_out', 'grad_ffn2_norm', 'grad_ffn2_w_gate', 'grad_ffn2_w_up', 'grad_ffn2_w_down', 'grad_final_norm', 'delta_meta_tokens', 'delta_ffn1_norm', 'delta_ffn1_w_gate', 'delta_ffn1_w_up', 'delta_ffn1_w_down', 'delta_mix_norm', 'delta_w_in', 'delta_rg_conv_w', 'delta_rg_conv_b', 'delta_rg_w_a', 'delta_rg_b_a', 'delta_rg_w_x', 'delta_rg_b_x', 'delta_rg_lambda', 'delta_s5_lambda_re', 'delta_s5_lambda_im', 'delta_s5_log_dt', 'delta_s5_b_re', 'delta_s5_b_im', 'delta_s5_c_re', 'delta_s5_c_im', 'delta_s5_d', 'delta_s5_glu_w', 'delta_s5_glu_b', 'delta_rg_out_norm', 'delta_s5_out_norm', 'delta_w_out', 'delta_ffn2_norm', 'delta_ffn2_w_gate', 'delta_ffn2_w_up', 'delta_ffn2_w_down', 'delta_final_norm', 'new_m_meta_tokens', 'new_m_ffn1_norm', 'new_m_ffn1_w_gate', 'new_m_ffn1_w_up', 'new_m_ffn1_w_down', 'new_m_mix_norm', 'new_m_w_in', 'new_m_rg_conv_w', 'new_m_rg_conv_b', 'new_m_rg_w_a', 'new_m_rg_b_a', 'new_m_rg_w_x', 'new_m_rg_b_x', 'new_m_rg_lambda', 'new_m_s5_lambda_re', 'new_m_s5_lambda_im', 'new_m_s5_log_dt', 'new_m_s5_b_re', 'new_m_s5_b_im', 'new_m_s5_c_re', 'new_m_s5_c_im', 'new_m_s5_d', 'new_m_s5_glu_w', 'new_m_s5_glu_b', 'new_m_rg_out_norm', 'new_m_s5_out_norm', 'new_m_w_out', 'new_m_ffn2_norm', 'new_m_ffn2_w_gate', 'new_m_ffn2_w_up', 'new_m_ffn2_w_down', 'new_m_final_norm', 'new_v_meta_tokens', 'new_v_ffn1_norm', 'new_v_ffn1_w_gate', 'new_v_ffn1_w_up', 'new_v_ffn1_w_down', 'new_v_mix_norm', 'new_v_w_in', 'new_v_rg_conv_w', 'new_v_rg_conv_b', 'new_v_rg_w_a', 'new_v_rg_b_a', 'new_v_rg_w_x', 'new_v_rg_b_x', 'new_v_rg_lambda', 'new_v_s5_lambda_re', 'new_v_s5_lambda_im', 'new_v_s5_log_dt', 'new_v_s5_b_re', 'new_v_s5_b_im', 'new_v_s5_c_re', 'new_v_s5_c_im', 'new_v_s5_d', 'new_v_s5_glu_w', 'new_v_s5_glu_b', 'new_v_rg_out_norm', 'new_v_s5_out_norm', 'new_v_w_out', 'new_v_ffn2_norm', 'new_v_ffn2_w_gate', 'new_v_ffn2_w_up', 'new_v_ffn2_w_down', 'new_v_final_norm']
TWIN_LEAF_KINDS = {'loss': 'loss', 'grad_x': 'grad_x', 'grad_meta_tokens': 'grad_w', 'grad_ffn1_norm': 'grad_w', 'grad_ffn1_w_gate': 'grad_w', 'grad_ffn1_w_up': 'grad_w', 'grad_ffn1_w_down': 'grad_w', 'grad_mix_norm': 'grad_w', 'grad_w_in': 'grad_w', 'grad_rg_conv_w': 'grad_w', 'grad_rg_conv_b': 'grad_w', 'grad_rg_w_a': 'grad_w', 'grad_rg_b_a': 'grad_w', 'grad_rg_w_x': 'grad_w', 'grad_rg_b_x': 'grad_w', 'grad_rg_lambda': 'grad_w', 'grad_s5_lambda_re': 'grad_w', 'grad_s5_lambda_im': 'grad_w', 'grad_s5_log_dt': 'grad_w', 'grad_s5_b_re': 'grad_w', 'grad_s5_b_im': 'grad_w', 'grad_s5_c_re': 'grad_w', 'grad_s5_c_im': 'grad_w', 'grad_s5_d': 'grad_w', 'grad_s5_glu_w': 'grad_w', 'grad_s5_glu_b': 'grad_w', 'grad_rg_out_norm': 'grad_w', 'grad_s5_out_norm': 'grad_w', 'grad_w_out': 'grad_w', 'grad_ffn2_norm': 'grad_w', 'grad_ffn2_w_gate': 'grad_w', 'grad_ffn2_w_up': 'grad_w', 'grad_ffn2_w_down': 'grad_w', 'grad_final_norm': 'grad_w', 'delta_meta_tokens': 'delta_w', 'delta_ffn1_norm': 'delta_w', 'delta_ffn1_w_gate': 'delta_w', 'delta_ffn1_w_up': 'delta_w', 'delta_ffn1_w_down': 'delta_w', 'delta_mix_norm': 'delta_w', 'delta_w_in': 'delta_w', 'delta_rg_conv_w': 'delta_w', 'delta_rg_conv_b': 'delta_w', 'delta_rg_w_a': 'delta_w', 'delta_rg_b_a': 'delta_w', 'delta_rg_w_x': 'delta_w', 'delta_rg_b_x': 'delta_w', 'delta_rg_lambda': 'delta_w', 'delta_s5_lambda_re': 'delta_w', 'delta_s5_lambda_im': 'delta_w', 'delta_s5_log_dt': 'delta_w', 'delta_s5_b_re': 'delta_w', 'delta_s5_b_im': 'delta_w', 'delta_s5_c_re': 'delta_w', 'delta_s5_c_im': 'delta_w', 'delta_s5_d': 'delta_w', 'delta_s5_glu_w': 'delta_w', 'delta_s5_glu_b': 'delta_w', 'delta_rg_out_norm': 'delta_w', 'delta_s5_out_norm': 'delta_w', 'delta_w_out': 'delta_w', 'delta_ffn2_norm': 'delta_w', 'delta_ffn2_w_gate': 'delta_w', 'delta_ffn2_w_up': 'delta_w', 'delta_ffn2_w_down': 'delta_w', 'delta_final_norm': 'delta_w', 'new_m_meta_tokens': 'new_m', 'new_m_ffn1_norm': 'new_m', 'new_m_ffn1_w_gate': 'new_m', 'new_m_ffn1_w_up': 'new_m', 'new_m_ffn1_w_down': 'new_m', 'new_m_mix_norm': 'new_m', 'new_m_w_in': 'new_m', 'new_m_rg_conv_w': 'new_m', 'new_m_rg_conv_b': 'new_m', 'new_m_rg_w_a': 'new_m', 'new_m_rg_b_a': 'new_m', 'new_m_rg_w_x': 'new_m', 'new_m_rg_b_x': 'new_m', 'new_m_rg_lambda': 'new_m', 'new_m_s5_lambda_re': 'new_m', 'new_m_s5_lambda_im': 'new_m', 'new_m_s5_log_dt': 'new_m', 'new_m_s5_b_re': 'new_m', 'new_m_s5_b_im': 'new_m', 'new_m_s5_c_re': 'new_m', 'new_m_s5_c_im': 'new_m', 'new_m_s5_d': 'new_m', 'new_m_s5_glu_w': 'new_m', 'new_m_s5_glu_b': 'new_m', 'new_m_rg_out_norm': 'new_m', 'new_m_s5_out_norm': 'new_m', 'new_m_w_out': 'new_m', 'new_m_ffn2_norm': 'new_m', 'new_m_ffn2_w_gate': 'new_m', 'new_m_ffn2_w_up': 'new_m', 'new_m_ffn2_w_down': 'new_m', 'new_m_final_norm': 'new_m', 'new_v_meta_tokens': 'new_v', 'new_v_ffn1_norm': 'new_v', 'new_v_ffn1_w_gate': 'new_v', 'new_v_ffn1_w_up': 'new_v', 'new_v_ffn1_w_down': 'new_v', 'new_v_mix_norm': 'new_v', 'new_v_w_in': 'new_v', 'new_v_rg_conv_w': 'new_v', 'new_v_rg_conv_b': 'new_v', 'new_v_rg_w_a': 'new_v', 'new_v_rg_b_a': 'new_v', 'new_v_rg_w_x': 'new_v', 'new_v_rg_b_x': 'new_v', 'new_v_rg_lambda': 'new_v', 'new_v_s5_lambda_re': 'new_v', 'new_v_s5_lambda_im': 'new_v', 'new_v_s5_log_dt': 'new_v', 'new_v_s5_b_re': 'new_v', 'new_v_s5_b_im': 'new_v', 'new_v_s5_c_re': 'new_v', 'new_v_s5_c_im': 'new_v', 'new_v_s5_d': 'new_v', 'new_v_s5_glu_w': 'new_v', 'new_v_s5_glu_b': 'new_v', 'new_v_rg_out_norm': 'new_v', 'new_v_s5_out_norm': 'new_v', 'new_v_w_out': 'new_v', 'new_v_ffn2_norm': 'new_v', 'new_v_ffn2_w_gate': 'new_v', 'new_v_ffn2_w_up': 'new_v', 'new_v_ffn2_w_down': 'new_v', 'new_v_final_norm': 'new_v'}


def _forward(args):
    return _fwd_reference(*[args[k] for k in FWD_PARAMS])


def _output_shape():
    out = _jax.eval_shape(lambda: _forward(_fwd_setup_inputs(0)))
    return out.shape, out.dtype

N_MICROBATCH = 1
ADAM_LR = 0.001
ADAM_B1 = 0.9
ADAM_B2 = 0.999
ADAM_EPS = 1e-08
ADAM_WD = 0.01
ADAM_STEP = 10
PER_EXAMPLE_BATCH_AXIS = {'x': 0, 'loss_target': 0}
SHARED_INPUTS = []
_WEIGHT_DTYPES = {'meta_tokens': _jnp.float32, 'ffn1_norm': _jnp.float32, 'ffn1_w_gate': _jnp.float32, 'ffn1_w_up': _jnp.float32, 'ffn1_w_down': _jnp.float32, 'mix_norm': _jnp.float32, 'w_in': _jnp.float32, 'rg_conv_w': _jnp.float32, 'rg_conv_b': _jnp.float32, 'rg_w_a': _jnp.float32, 'rg_b_a': _jnp.float32, 'rg_w_x': _jnp.float32, 'rg_b_x': _jnp.float32, 'rg_lambda': _jnp.float32, 's5_lambda_re': _jnp.float32, 's5_lambda_im': _jnp.float32, 's5_log_dt': _jnp.float32, 's5_b_re': _jnp.float32, 's5_b_im': _jnp.float32, 's5_c_re': _jnp.float32, 's5_c_im': _jnp.float32, 's5_d': _jnp.float32, 's5_glu_w': _jnp.float32, 's5_glu_b': _jnp.float32, 'rg_out_norm': _jnp.float32, 's5_out_norm': _jnp.float32, 'w_out': _jnp.float32, 'ffn2_norm': _jnp.float32, 'ffn2_w_gate': _jnp.float32, 'ffn2_w_up': _jnp.float32, 'ffn2_w_down': _jnp.float32, 'final_norm': _jnp.float32}
MOMENT_SCALE = {'meta_tokens': 2.002019e-03, 'ffn1_norm': 2.191802e-02, 'ffn1_w_gate': 9.398701e-03, 'ffn1_w_up': 9.090815e-03, 'ffn1_w_down': 1.489432e-02, 'mix_norm': 4.074807e-02, 'w_in': 3.309112e-02, 'rg_conv_w': 3.534421e-02, 'rg_conv_b': 4.141914e-01, 'rg_w_a': 9.044195e-03, 'rg_b_a': 8.190603e-03, 'rg_w_x': 1.620462e-02, 'rg_b_x': 1.220466e-02, 'rg_lambda': 1.719414e-02, 's5_lambda_re': 2.225295e-02, 's5_lambda_im': 2.328324e-02, 's5_log_dt': 1.876391e+01, 's5_b_re': 1.411414e-02, 's5_b_im': 1.440434e-02, 's5_c_re': 2.450888e-03, 's5_c_im': 2.480462e-03, 's5_d': 4.200115e-02, 's5_glu_w': 8.359157e-03, 's5_glu_b': 1.465033e-02, 'rg_out_norm': 3.305504e-02, 's5_out_norm': 3.241788e-02, 'w_out': 3.231611e-02, 'ffn2_norm': 1.287097e-02, 'ffn2_w_gate': 5.636647e-03, 'ffn2_w_up': 5.464296e-03, 'ffn2_w_down': 8.960406e-03, 'final_norm': 8.006904e+00}


def _to_microbatches(a, axis):
    t = _jnp.moveaxis(a, axis, 0)
    t = t.reshape((N_MICROBATCH, t.shape[0] // N_MICROBATCH) + t.shape[1:])
    return _jnp.moveaxis(t, 1, axis + 1)


def setup_inputs(seed: int = 0) -> dict:
    inp = _fwd_setup_inputs(seed)
    key = _jax.random.fold_in(_jax.random.key(seed), 7919)
    shape, _ = _output_shape()
    out = dict(inp)
    out["loss_target"] = _jax.random.normal(_jax.random.fold_in(key, 0), shape, _jnp.float32)
    for i, name in enumerate(TWIN_WEIGHTS):
        w = inp[name].astype(_jnp.float32)
        if MOMENT_SCALE is None:
            s = _jnp.sqrt(_jnp.mean(_jnp.square(w)) + 1e-30)
        else:
            s = MOMENT_SCALE[name]
        km, kv = _jax.random.split(_jax.random.fold_in(key, i + 1))
        out[name] = w
        out["m_" + name] = s * _jax.random.normal(km, w.shape, _jnp.float32)
        out["v_" + name] = (s * s) * _jax.random.uniform(kv, w.shape, _jnp.float32, 0.5, 1.5)
    if N_MICROBATCH > 1:
        for name, axis in PER_EXAMPLE_BATCH_AXIS.items():
            out[name] = _to_microbatches(out[name], axis)
    return {'x': out['x'], 'meta_tokens': out['meta_tokens'], 'ffn1_norm': out['ffn1_norm'], 'ffn1_w_gate': out['ffn1_w_gate'], 'ffn1_w_up': out['ffn1_w_up'], 'ffn1_w_down': out['ffn1_w_down'], 'mix_norm': out['mix_norm'], 'w_in': out['w_in'], 'rg_conv_w': out['rg_conv_w'], 'rg_conv_b': out['rg_conv_b'], 'rg_w_a': out['rg_w_a'], 'rg_b_a': out['rg_b_a'], 'rg_w_x': out['rg_w_x'], 'rg_b_x': out['rg_b_x'], 'rg_lambda': out['rg_lambda'], 's5_lambda_re': out['s5_lambda_re'], 's5_lambda_im': out['s5_lambda_im'], 's5_log_dt': out['s5_log_dt'], 's5_b_re': out['s5_b_re'], 's5_b_im': out['s5_b_im'], 's5_c_re': out['s5_c_re'], 's5_c_im': out['s5_c_im'], 's5_d': out['s5_d'], 's5_glu_w': out['s5_glu_w'], 's5_glu_b': out['s5_glu_b'], 'rg_out_norm': out['rg_out_norm'], 's5_out_norm': out['s5_out_norm'], 'w_out': out['w_out'], 'ffn2_norm': out['ffn2_norm'], 'ffn2_w_gate': out['ffn2_w_gate'], 'ffn2_w_up': out['ffn2_w_up'], 'ffn2_w_down': out['ffn2_w_down'], 'final_norm': out['final_norm'], 'loss_target': out['loss_target'], 'm_meta_tokens': out['m_meta_tokens'], 'm_ffn1_norm': out['m_ffn1_norm'], 'm_ffn1_w_gate': out['m_ffn1_w_gate'], 'm_ffn1_w_up': out['m_ffn1_w_up'], 'm_ffn1_w_down': out['m_ffn1_w_down'], 'm_mix_norm': out['m_mix_norm'], 'm_w_in': out['m_w_in'], 'm_rg_conv_w': out['m_rg_conv_w'], 'm_rg_conv_b': out['m_rg_conv_b'], 'm_rg_w_a': out['m_rg_w_a'], 'm_rg_b_a': out['m_rg_b_a'], 'm_rg_w_x': out['m_rg_w_x'], 'm_rg_b_x': out['m_rg_b_x'], 'm_rg_lambda': out['m_rg_lambda'], 'm_s5_lambda_re': out['m_s5_lambda_re'], 'm_s5_lambda_im': out['m_s5_lambda_im'], 'm_s5_log_dt': out['m_s5_log_dt'], 'm_s5_b_re': out['m_s5_b_re'], 'm_s5_b_im': out['m_s5_b_im'], 'm_s5_c_re': out['m_s5_c_re'], 'm_s5_c_im': out['m_s5_c_im'], 'm_s5_d': out['m_s5_d'], 'm_s5_glu_w': out['m_s5_glu_w'], 'm_s5_glu_b': out['m_s5_glu_b'], 'm_rg_out_norm': out['m_rg_out_norm'], 'm_s5_out_norm': out['m_s5_out_norm'], 'm_w_out': out['m_w_out'], 'm_ffn2_norm': out['m_ffn2_norm'], 'm_ffn2_w_gate': out['m_ffn2_w_gate'], 'm_ffn2_w_up': out['m_ffn2_w_up'], 'm_ffn2_w_down': out['m_ffn2_w_down'], 'm_final_norm': out['m_final_norm'], 'v_meta_tokens': out['v_meta_tokens'], 'v_ffn1_norm': out['v_ffn1_norm'], 'v_ffn1_w_gate': out['v_ffn1_w_gate'], 'v_ffn1_w_up': out['v_ffn1_w_up'], 'v_ffn1_w_down': out['v_ffn1_w_down'], 'v_mix_norm': out['v_mix_norm'], 'v_w_in': out['v_w_in'], 'v_rg_conv_w': out['v_rg_conv_w'], 'v_rg_conv_b': out['v_rg_conv_b'], 'v_rg_w_a': out['v_rg_w_a'], 'v_rg_b_a': out['v_rg_b_a'], 'v_rg_w_x': out['v_rg_w_x'], 'v_rg_b_x': out['v_rg_b_x'], 'v_rg_lambda': out['v_rg_lambda'], 'v_s5_lambda_re': out['v_s5_lambda_re'], 'v_s5_lambda_im': out['v_s5_lambda_im'], 'v_s5_log_dt': out['v_s5_log_dt'], 'v_s5_b_re': out['v_s5_b_re'], 'v_s5_b_im': out['v_s5_b_im'], 'v_s5_c_re': out['v_s5_c_re'], 'v_s5_c_im': out['v_s5_c_im'], 'v_s5_d': out['v_s5_d'], 'v_s5_glu_w': out['v_s5_glu_w'], 'v_s5_glu_b': out['v_s5_glu_b'], 'v_rg_out_norm': out['v_rg_out_norm'], 'v_s5_out_norm': out['v_s5_out_norm'], 'v_w_out': out['v_w_out'], 'v_ffn2_norm': out['v_ffn2_norm'], 'v_ffn2_w_gate': out['v_ffn2_w_gate'], 'v_ffn2_w_up': out['v_ffn2_w_up'], 'v_ffn2_w_down': out['v_ffn2_w_down'], 'v_final_norm': out['v_final_norm']}


def _loss(weights, diff, rest, loss_target):
    with _jax.named_scope("forward"):
        args = {**rest, TWIN_DIFF_INPUT: diff, **{k: w.astype(_WEIGHT_DTYPES[k]) for k, w in weights.items()}}
        y = _forward(args)
    with _jax.named_scope("loss_head"):
        err = _jnp.square(y.astype(_jnp.float32) - loss_target)
        return 0.5 * _jnp.sum(_jnp.mean(err, axis=-1)) if err.ndim else 0.5 * err


def _adamw(w, g, m, v):
    m = ADAM_B1 * m + (1.0 - ADAM_B1) * g
    v = ADAM_B2 * v + (1.0 - ADAM_B2) * _jnp.square(g)
    m_hat = m / (1.0 - ADAM_B1 ** ADAM_STEP)
    v_hat = v / (1.0 - ADAM_B2 ** ADAM_STEP)
    delta = -ADAM_LR * (m_hat / (_jnp.sqrt(v_hat) + ADAM_EPS) + ADAM_WD * w)
    return delta, m, v


def reference(x, meta_tokens, ffn1_norm, ffn1_w_gate, ffn1_w_up, ffn1_w_down, mix_norm, w_in, rg_conv_w, rg_conv_b, rg_w_a, rg_b_a, rg_w_x, rg_b_x, rg_lambda, s5_lambda_re, s5_lambda_im, s5_log_dt, s5_b_re, s5_b_im, s5_c_re, s5_c_im, s5_d, s5_glu_w, s5_glu_b, rg_out_norm, s5_out_norm, w_out, ffn2_norm, ffn2_w_gate, ffn2_w_up, ffn2_w_down, final_norm, loss_target, m_meta_tokens, m_ffn1_norm, m_ffn1_w_gate, m_ffn1_w_up, m_ffn1_w_down, m_mix_norm, m_w_in, m_rg_conv_w, m_rg_conv_b, m_rg_w_a, m_rg_b_a, m_rg_w_x, m_rg_b_x, m_rg_lambda, m_s5_lambda_re, m_s5_lambda_im, m_s5_log_dt, m_s5_b_re, m_s5_b_im, m_s5_c_re, m_s5_c_im, m_s5_d, m_s5_glu_w, m_s5_glu_b, m_rg_out_norm, m_s5_out_norm, m_w_out, m_ffn2_norm, m_ffn2_w_gate, m_ffn2_w_up, m_ffn2_w_down, m_final_norm, v_meta_tokens, v_ffn1_norm, v_ffn1_w_gate, v_ffn1_w_up, v_ffn1_w_down, v_mix_norm, v_w_in, v_rg_conv_w, v_rg_conv_b, v_rg_w_a, v_rg_b_a, v_rg_w_x, v_rg_b_x, v_rg_lambda, v_s5_lambda_re, v_s5_lambda_im, v_s5_log_dt, v_s5_b_re, v_s5_b_im, v_s5_c_re, v_s5_c_im, v_s5_d, v_s5_glu_w, v_s5_glu_b, v_rg_out_norm, v_s5_out_norm, v_w_out, v_ffn2_norm, v_ffn2_w_gate, v_ffn2_w_up, v_ffn2_w_down, v_final_norm):
    given = dict(x=x, meta_tokens=meta_tokens, ffn1_norm=ffn1_norm, ffn1_w_gate=ffn1_w_gate, ffn1_w_up=ffn1_w_up, ffn1_w_down=ffn1_w_down, mix_norm=mix_norm, w_in=w_in, rg_conv_w=rg_conv_w, rg_conv_b=rg_conv_b, rg_w_a=rg_w_a, rg_b_a=rg_b_a, rg_w_x=rg_w_x, rg_b_x=rg_b_x, rg_lambda=rg_lambda, s5_lambda_re=s5_lambda_re, s5_lambda_im=s5_lambda_im, s5_log_dt=s5_log_dt, s5_b_re=s5_b_re, s5_b_im=s5_b_im, s5_c_re=s5_c_re, s5_c_im=s5_c_im, s5_d=s5_d, s5_glu_w=s5_glu_w, s5_glu_b=s5_glu_b, rg_out_norm=rg_out_norm, s5_out_norm=s5_out_norm, w_out=w_out, ffn2_norm=ffn2_norm, ffn2_w_gate=ffn2_w_gate, ffn2_w_up=ffn2_w_up, ffn2_w_down=ffn2_w_down, final_norm=final_norm, loss_target=loss_target, m_meta_tokens=m_meta_tokens, m_ffn1_norm=m_ffn1_norm, m_ffn1_w_gate=m_ffn1_w_gate, m_ffn1_w_up=m_ffn1_w_up, m_ffn1_w_down=m_ffn1_w_down, m_mix_norm=m_mix_norm, m_w_in=m_w_in, m_rg_conv_w=m_rg_conv_w, m_rg_conv_b=m_rg_conv_b, m_rg_w_a=m_rg_w_a, m_rg_b_a=m_rg_b_a, m_rg_w_x=m_rg_w_x, m_rg_b_x=m_rg_b_x, m_rg_lambda=m_rg_lambda, m_s5_lambda_re=m_s5_lambda_re, m_s5_lambda_im=m_s5_lambda_im, m_s5_log_dt=m_s5_log_dt, m_s5_b_re=m_s5_b_re, m_s5_b_im=m_s5_b_im, m_s5_c_re=m_s5_c_re, m_s5_c_im=m_s5_c_im, m_s5_d=m_s5_d, m_s5_glu_w=m_s5_glu_w, m_s5_glu_b=m_s5_glu_b, m_rg_out_norm=m_rg_out_norm, m_s5_out_norm=m_s5_out_norm, m_w_out=m_w_out, m_ffn2_norm=m_ffn2_norm, m_ffn2_w_gate=m_ffn2_w_gate, m_ffn2_w_up=m_ffn2_w_up, m_ffn2_w_down=m_ffn2_w_down, m_final_norm=m_final_norm, v_meta_tokens=v_meta_tokens, v_ffn1_norm=v_ffn1_norm, v_ffn1_w_gate=v_ffn1_w_gate, v_ffn1_w_up=v_ffn1_w_up, v_ffn1_w_down=v_ffn1_w_down, v_mix_norm=v_mix_norm, v_w_in=v_w_in, v_rg_conv_w=v_rg_conv_w, v_rg_conv_b=v_rg_conv_b, v_rg_w_a=v_rg_w_a, v_rg_b_a=v_rg_b_a, v_rg_w_x=v_rg_w_x, v_rg_b_x=v_rg_b_x, v_rg_lambda=v_rg_lambda, v_s5_lambda_re=v_s5_lambda_re, v_s5_lambda_im=v_s5_lambda_im, v_s5_log_dt=v_s5_log_dt, v_s5_b_re=v_s5_b_re, v_s5_b_im=v_s5_b_im, v_s5_c_re=v_s5_c_re, v_s5_c_im=v_s5_c_im, v_s5_d=v_s5_d, v_s5_glu_w=v_s5_glu_w, v_s5_glu_b=v_s5_glu_b, v_rg_out_norm=v_rg_out_norm, v_s5_out_norm=v_s5_out_norm, v_w_out=v_w_out, v_ffn2_norm=v_ffn2_norm, v_ffn2_w_gate=v_ffn2_w_gate, v_ffn2_w_up=v_ffn2_w_up, v_ffn2_w_down=v_ffn2_w_down, v_final_norm=v_final_norm)
    weights = {n: given[n] for n in TWIN_WEIGHTS}
    shared = {n: given[n] for n in SHARED_INPUTS}
    per_example = {n: given[n] for n in ['x']}
    grad_fn = _jax.value_and_grad(_loss, argnums=(0, 1))

    def one_microbatch(ex, loss_target):
        ex = dict(ex)
        diff = ex.pop(TWIN_DIFF_INPUT)
        return grad_fn(weights, diff, {**shared, **ex}, loss_target)

    if N_MICROBATCH == 1:
        loss, (grad_w, grad_x) = one_microbatch(per_example, given["loss_target"])
    else:
        def body(carry, xs):
            loss_sum, grad_sum = carry
            l_k, (gw_k, gx_k) = one_microbatch(xs[0], xs[1])
            with _jax.named_scope("update"):
                return (loss_sum + l_k, _jax.tree.map(_jnp.add, grad_sum, gw_k)), gx_k

        init = (_jnp.zeros((), _jnp.float32), _jax.tree.map(_jnp.zeros_like, weights))
        (loss, grad_w), grad_x = _jax.lax.scan(body, init, (per_example, given["loss_target"]))
    with _jax.named_scope("update"):
        delta_w, new_m, new_v = {}, {}, {}
        for n in TWIN_WEIGHTS:
            delta_w[n], new_m[n], new_v[n] = _adamw(weights[n], grad_w[n], given["m_" + n], given["v_" + n])
    return (loss, grad_x, *[grad_w[n] for n in TWIN_WEIGHTS], *[delta_w[n] for n in TWIN_WEIGHTS],
            *[new_m[n] for n in TWIN_WEIGHTS], *[new_v[n] for n in TWIN_WEIGHTS])
```

```python
import functools
import math

import jax
import jax.numpy as jnp
from jax import lax
from jax.experimental import pallas as pl
from jax.experimental.pallas import tpu as pltpu

F32 = jnp.float32
BF16 = jnp.bfloat16
MESH = pl.DeviceIdType.MESH

N_META = 16
RG_HEADS = 8
CONV_WIDTH = 4
RG_C = 8.0
S5_GROUP = 16
S5_STATE = 64
GROUPS_PER_BLOCK = 8
EPS = 1e-6
N_CHIPS = 4
LANES = 128
VMEM_LIMIT = 56 * 1024 * 1024

ADAM_LR = 0.001
ADAM_B1 = 0.9
ADAM_B2 = 0.999
ADAM_EPS = 1e-08
ADAM_WD = 0.01
ADAM_STEP = 10

WEIGHTS = ['meta_tokens', 'ffn1_norm', 'ffn1_w_gate', 'ffn1_w_up', 'ffn1_w_down', 'mix_norm', 'w_in', 'rg_conv_w',
           'rg_conv_b', 'rg_w_a', 'rg_b_a', 'rg_w_x', 'rg_b_x', 'rg_lambda', 's5_lambda_re', 's5_lambda_im',
           's5_log_dt', 's5_b_re', 's5_b_im', 's5_c_re', 's5_c_im', 's5_d', 's5_glu_w', 's5_glu_b', 'rg_out_norm',
           's5_out_norm', 'w_out', 'ffn2_norm', 'ffn2_w_gate', 'ffn2_w_up', 'ffn2_w_down', 'final_norm']
BIG = ('ffn1_w_gate', 'ffn1_w_up', 'ffn1_w_down', 'w_in', 's5_glu_w', 'w_out', 'ffn2_w_gate', 'ffn2_w_up',
       'ffn2_w_down')

_DN = {'nn': (((1,), (0,)), ((), ())), 'nt': (((1,), (1,)), ((), ())), 'tn': (((0,), (0,)), ((), ()))}


def _round_up(n, m):
    return (n + m - 1) // m * m


def _tile(n, pref, unit=LANES):
    best = None
    for t in range(unit, min(n, pref) + 1, unit):
        if n % t == 0:
            best = t
    return best if best is not None else n


def _params(sem=None):
    return pltpu.CompilerParams(dimension_semantics=sem, vmem_limit_bytes=VMEM_LIMIT)


def _rms(x, g):
    return x * lax.rsqrt(jnp.mean(x * x, axis=-1, keepdims=True) + EPS) * g


def _sigmoid(x):
    return 1.0 / (1.0 + jnp.exp(-x))


def _gelu(x):
    return 0.5 * x * (1.0 + jnp.tanh(math.sqrt(2.0 / math.pi) * (x + 0.044715 * (x * x * x))))


def _silu(x):
    return x * _sigmoid(x)


def _expm1(x):
    series = x * (1.0 + x * (1.0 / 2) * (1.0 + x * (1.0 / 3) * (1.0 + x * (1.0 / 4) * (1.0 + x * (1.0 / 5) * (1.0 + x * (1.0 / 6))))))
    return jnp.where(jnp.abs(x) < 0.3, series, jnp.exp(x) - 1.0)


def _softplus_neg(lam):
    m = jnp.maximum(-lam, 0.0)
    e = jnp.exp(-jnp.abs(lam))
    w = 1.0 + e
    log1p = jnp.where(w == 1.0, e, jnp.log(w) * (e / jnp.where(w == 1.0, 1.0, w - 1.0)))
    return m + log1p


def _rg_gate_math(xc, pre_r, pre_i, sp):
    r = _sigmoid(pre_r)
    i = _sigmoid(pre_i)
    log_a = -RG_C * r * sp
    a = jnp.exp(log_a)
    mult = jnp.sqrt(-_expm1(2.0 * log_a))
    return a, mult * i * xc


def _matmul(name, mode, a, a_blk, a_map, b, b_blk, b_map, grid, outs, epilogue, extras=()):
    ni, nj, nk = grid
    ne, no = len(extras), len(outs)
    sq = lambda blk: tuple(d for d in blk if d is not None)
    ab, bb = sq(a_blk), sq(b_blk)
    acc_shape = {'nn': (ab[0], bb[1]), 'nt': (ab[0], bb[0]), 'tn': (ab[1], bb[1])}[mode]

    def body(*refs):
        a_ref, b_ref = refs[0], refs[1]
        ex = refs[2:2 + ne]
        out = refs[2 + ne:2 + ne + no]
        part = lax.dot_general(a_ref[...], b_ref[...], _DN[mode], preferred_element_type=F32)

        def finish(acc):
            res = epilogue(acc, *[e[...] for e in ex])
            if not isinstance(res, tuple):
                res = (res,)
            for o, r in zip(out, res):
                o[...] = r.astype(o.dtype)

        if nk == 1:
            finish(part)
        else:
            acc_ref = refs[2 + ne + no]
            k = pl.program_id(2)

            @pl.when(k == 0)
            def _():
                acc_ref[...] = part

            @pl.when(k > 0)
            def _():
                acc_ref[...] += part

            @pl.when(k == nk - 1)
            def _():
                finish(acc_ref[...])

    in_specs = [pl.BlockSpec(a_blk, a_map), pl.BlockSpec(b_blk, b_map)]
    in_specs += [pl.BlockSpec(blk, functools.partial(lambda m, i, j, k: m(i, j), m)) for _, blk, m in extras]
    out_specs = [pl.BlockSpec(blk, functools.partial(lambda m, i, j, k: m(i, j), m)) for _, _, blk, m in outs]
    res = pl.pallas_call(
        body, name=name, grid=grid, in_specs=in_specs, out_specs=out_specs,
        out_shape=[jax.ShapeDtypeStruct(s, d) for s, d, _, _ in outs],
        scratch_shapes=[pltpu.VMEM(acc_shape, F32)] if nk > 1 else [],
        compiler_params=_params(("parallel", "parallel", "arbitrary")),
    )(a, b, *[e for e, _, _ in extras])
    return res


def _rows(arr, tr, tc=None, col0=0):
    if tc is None:
        return (arr, (tr, arr.shape[1]), lambda j, i: (i, 0))
    return (arr, (tr, tc), lambda j, i: (i, col0 + j))


def _full(arr):
    nd = arr.ndim
    return (arr, arr.shape, lambda j, i: (0,) * nd)


def _cols(arr, tc):
    return (arr, (arr.shape[0], tc), lambda j, i: (0, j))


def _rowmap(name, fn, grid, ins, outs, accs=(), scratch=()):
    nj, ni = grid
    n_in, n_out, n_acc = len(ins), len(outs), len(accs)

    def body(*refs):
        vals = [r[...] for r in refs[:n_in]]
        o_refs = refs[n_in:n_in + n_out]
        a_refs = refs[n_in + n_out:n_in + n_out + n_acc]
        s_refs = refs[n_in + n_out + n_acc:]
        res = fn(*vals, *s_refs)
        if not isinstance(res, tuple):
            res = (res,)
        for o, r in zip(o_refs, res[:n_out]):
            o[...] = r.astype(o.dtype)
        i = pl.program_id(1)
        for a_ref, r in zip(a_refs, res[n_out:]):
            @pl.when(i == 0)
            def _(a_ref=a_ref, r=r):
                a_ref[...] = r.astype(a_ref.dtype)

            @pl.when(i > 0)
            def _(a_ref=a_ref, r=r):
                a_ref[...] += r.astype(a_ref.dtype)

    res = pl.pallas_call(
        body, name=name, grid=grid,
        in_specs=[pl.BlockSpec(blk, m) for _, blk, m in ins],
        out_specs=[pl.BlockSpec(blk, m) for _, _, blk, m in list(outs) + list(accs)],
        out_shape=[jax.ShapeDtypeStruct(s, d) for s, d, _, _ in list(outs) + list(accs)],
        scratch_shapes=list(scratch),
        compiler_params=_params(("parallel", "arbitrary")),
    )(*[a for a, _, _ in ins])
    return res


def _o(shape, dtype, tr, tc=None):
    if tc is None:
        return (shape, dtype, (tr, shape[1]), lambda j, i: (i, 0))
    return (shape, dtype, (tr, tc), lambda j, i: (i, j))


def _acc(shape, tc=None):
    if tc is None:
        nd = len(shape)
        return (shape, F32, shape, lambda j, i: (0,) * nd)
    return (shape, F32, (shape[0], tc), lambda j, i: (0, j))


def _position():
    x, y, c = lax.axis_index("x"), lax.axis_index("y"), lax.axis_index("c")
    return x, y, c


def _all_gather_chips(name, w):
    rows, cols = w.shape
    rh = rows // 2

    def body(w_ref, out_ref, send_sems, recv_sems, fsend_sems, frecv_sems, local_sem):
        x, y, c = _position()
        s = 2 * x + y
        sibling = (x, y, 1 - c)
        chips = [(1 - x, y), (x, 1 - y), (1 - x, 1 - y)]
        mine = pl.ds(c * rh, rh)
        other = pl.ds((1 - c) * rh, rh)
        own = pltpu.make_async_copy(w_ref, out_ref.at[s], local_sem)
        own.start()

        def ici(j, chip_index, to):
            return pltpu.make_async_remote_copy(
                src_ref=w_ref.at[mine], dst_ref=out_ref.at[chip_index, mine],
                send_sem=send_sems.at[j], recv_sem=recv_sems.at[j], device_id=to, device_id_type=MESH)

        def d2d(j, chip_index, half):
            return pltpu.make_async_remote_copy(
                src_ref=out_ref.at[chip_index, half], dst_ref=out_ref.at[chip_index, half],
                send_sem=fsend_sems.at[j], recv_sem=frecv_sems.at[j], device_id=sibling, device_id_type=MESH)

        sends = [ici(j, s, (px, py, c)) for j, (px, py) in enumerate(chips)]
        for cp in sends:
            cp.start()
        passed = []
        for j, (px, py) in enumerate(chips):
            ici(j, 2 * px + py, (px, py, c)).wait_recv()
            fwd = d2d(j, 2 * px + py, mine)
            fwd.start()
            passed.append(fwd)
        for j, (px, py) in enumerate(chips):
            d2d(j, 2 * px + py, other).wait_recv()
        for cp in sends + passed:
            cp.wait_send()
        own.wait()

    hbm = pl.BlockSpec(memory_space=pltpu.HBM)
    return pl.pallas_call(
        body, name=name, in_specs=[hbm], out_specs=hbm,
        out_shape=jax.ShapeDtypeStruct((N_CHIPS, rows, cols), w.dtype),
        scratch_shapes=[pltpu.SemaphoreType.DMA((3,)), pltpu.SemaphoreType.DMA((3,)),
                        pltpu.SemaphoreType.DMA((3,)), pltpu.SemaphoreType.DMA((3,)), pltpu.SemaphoreType.DMA],
    )(w)


def _scatter_partials(name, g):
    _, rows, cols = g.shape
    rh = rows // 2

    def body(g_ref, land_ref, send_sems, recv_sems):
        x, y, c = _position()
        s = 2 * x + y
        chips = [(1 - x, y), (x, 1 - y), (1 - x, 1 - y)]

        def copy(send_slot, recv_slot, chip_index, half, to):
            return pltpu.make_async_remote_copy(
                src_ref=g_ref.at[chip_index, pl.ds(half * rh, rh)], dst_ref=land_ref.at[recv_slot],
                send_sem=send_sems.at[send_slot], recv_sem=recv_sems.at[recv_slot], device_id=to, device_id_type=MESH)

        sends = [copy(0, 0, s, 1 - c, (x, y, 1 - c))]
        for j, (px, py) in enumerate(chips):
            for cc in range(2):
                sends.append(copy(1 + 2 * j + cc, 1 + 2 * j + c, 2 * px + py, cc, (px, py, cc)))
        for cp in sends:
            cp.start()
        for k in range(7):
            copy(k, k, s, c, (x, y, c)).wait_recv()
        for cp in sends:
            cp.wait_send()

    hbm = pl.BlockSpec(memory_space=pltpu.HBM)
    return pl.pallas_call(
        body, name=name, in_specs=[hbm], out_specs=hbm,
        out_shape=jax.ShapeDtypeStruct((7, rh, cols), g.dtype),
        scratch_shapes=[pltpu.SemaphoreType.DMA((7,)), pltpu.SemaphoreType.DMA((7,))],
    )(g)


def _reduce_partials(name, g, land, pos):
    _, rows, cols = g.shape
    rh = rows // 2
    tr = _tile(rh, 256, 16)
    nt = rh // tr

    def body(pos_ref, g_ref, l_ref, o_ref):
        acc = g_ref[...].astype(F32)
        for k in range(7):
            acc = acc + l_ref[k].astype(F32)
        o_ref[...] = acc

    grid_spec = pltpu.PrefetchScalarGridSpec(
        num_scalar_prefetch=1, grid=(nt,),
        in_specs=[pl.BlockSpec((None, tr, cols), lambda i, p: (p[0], p[1] * nt + i, 0)),
                  pl.BlockSpec((7, tr, cols), lambda i, p: (0, i, 0))],
        out_specs=pl.BlockSpec((tr, cols), lambda i, p: (i, 0)))
    return pl.pallas_call(
        body, name=name, grid_spec=grid_spec, out_shape=jax.ShapeDtypeStruct((rh, cols), F32),
        compiler_params=_params(("parallel",)),
    )(pos, g, land)


def _sibling_exchange(name, halves):
    n = len(halves)

    def body(*refs):
        ins, outs = refs[:n], refs[n:2 * n]
        send_sems, recv_sems, local_sems = refs[2 * n:]
        x, y, c = _position()
        sibling = (x, y, 1 - c)
        copies = []
        for k in range(n):
            lc = pltpu.make_async_copy(ins[k], outs[k].at[c], local_sems.at[k])
            lc.start()
            rc = pltpu.make_async_remote_copy(
                src_ref=ins[k], dst_ref=outs[k].at[c], send_sem=send_sems.at[k], recv_sem=recv_sems.at[k],
                device_id=sibling, device_id_type=MESH)
            rc.start()
            copies.append((lc, rc))
        for k in range(n):
            pltpu.make_async_remote_copy(
                src_ref=ins[k], dst_ref=outs[k].at[1 - c], send_sem=send_sems.at[k], recv_sem=recv_sems.at[k],
                device_id=sibling, device_id_type=MESH).wait_recv()
        for lc, rc in copies:
            rc.wait_send()
            lc.wait()

    hbm = pl.BlockSpec(memory_space=pltpu.HBM)
    return pl.pallas_call(
        body, name=name, in_specs=[hbm] * n, out_specs=[hbm] * n,
        out_shape=[jax.ShapeDtypeStruct((2,) + h.shape, h.dtype) for h in halves],
        scratch_shapes=[pltpu.SemaphoreType.DMA((n,)), pltpu.SemaphoreType.DMA((n,)), pltpu.SemaphoreType.DMA((n,))],
    )(*halves)


def _all_reduce_small(name, buf):
    rows = buf.shape[0]
    p = rows // 8

    def body(buf_ref, out_ref, land_ref, red_ref, s1, r1, s2, r2):
        x, y, c = _position()
        me = 4 * x + 2 * y + c

        def peer(r):
            px = 1 - x if (r >> 2) & 1 else x
            py = 1 - y if (r >> 1) & 1 else y
            pc = 1 - c if r & 1 else c
            return (px, py, pc), 4 * px + 2 * py + pc

        firsts = []
        for r in range(1, 8):
            to, d = peer(r)
            cp = pltpu.make_async_remote_copy(
                src_ref=buf_ref.at[pl.ds(d * p, p)], dst_ref=land_ref.at[r - 1],
                send_sem=s1.at[r - 1], recv_sem=r1.at[r - 1], device_id=to, device_id_type=MESH)
            cp.start()
            firsts.append(cp)
        acc = buf_ref[pl.ds(me * p, p), :]
        for r in range(1, 8):
            firsts[r - 1].wait_recv()
            acc = acc + land_ref[r - 1]
        red_ref[...] = acc
        out_ref[pl.ds(me * p, p), :] = acc
        seconds = []
        for r in range(1, 8):
            to, d = peer(r)
            cp = pltpu.make_async_remote_copy(
                src_ref=red_ref, dst_ref=out_ref.at[pl.ds(me * p, p)],
                send_sem=s2.at[r - 1], recv_sem=r2.at[r - 1], device_id=to, device_id_type=MESH)
            cp.start()
            seconds.append(cp)
        for r in range(1, 8):
            to, d = peer(r)
            pltpu.make_async_remote_copy(
                src_ref=red_ref, dst_ref=out_ref.at[pl.ds(d * p, p)],
                send_sem=s2.at[r - 1], recv_sem=r2.at[r - 1], device_id=to, device_id_type=MESH).wait_recv()
        for cp in firsts + seconds:
            cp.wait_send()

    vmem = pl.BlockSpec(memory_space=pltpu.VMEM)
    return pl.pallas_call(
        body, name=name, in_specs=[vmem], out_specs=vmem,
        out_shape=jax.ShapeDtypeStruct(buf.shape, F32),
        scratch_shapes=[pltpu.VMEM((7, p, LANES), F32), pltpu.VMEM((p, LANES), F32),
                        pltpu.SemaphoreType.DMA((7,)), pltpu.SemaphoreType.DMA((7,)),
                        pltpu.SemaphoreType.DMA((7,)), pltpu.SemaphoreType.DMA((7,))],
        compiler_params=pltpu.CompilerParams(vmem_limit_bytes=VMEM_LIMIT),
    )(buf)


def _cast_pad(name, w, rows_to, cols_to):
    rows, cols = w.shape
    tr = _tile(math.gcd(rows, rows_to), 256, 16)
    assert rows % tr == 0 and rows_to % tr == 0, (rows, rows_to, tr)
    n_src = rows // tr

    def body(w_ref, o_ref):
        i = pl.program_id(0)
        if cols_to > cols:
            o_ref[:, cols:] = jnp.zeros((tr, cols_to - cols), BF16)

        @pl.when(i < n_src)
        def _():
            o_ref[:, :cols] = w_ref[...].astype(BF16)

        if rows_to > rows:
            @pl.when(i >= n_src)
            def _():
                o_ref[:, :cols] = jnp.zeros((tr, cols), BF16)

    return pl.pallas_call(
        body, name=name, grid=(rows_to // tr,),
        in_specs=[pl.BlockSpec((tr, cols), lambda i: (jnp.minimum(i, n_src - 1), 0))],
        out_specs=pl.BlockSpec((tr, cols_to), lambda i: (i, 0)),
        out_shape=jax.ShapeDtypeStruct((rows_to, cols_to), BF16),
        compiler_params=_params(("parallel",)),
    )(w)


def _scan_rows(a, b, row, tr, reverse):
    sh = 1
    while sh < tr:
        if reverse:
            valid = row < tr - sh
            a_s = pltpu.roll(a, tr - sh, 0)
            b_s = pltpu.roll(b, tr - sh, 0)
        else:
            valid = row >= sh
            a_s = pltpu.roll(a, sh, 0)
            b_s = pltpu.roll(b, sh, 0)
        b = b + jnp.where(valid, a * b_s, 0.0)
        a = jnp.where(valid, a * a_s, a)
        sh *= 2
    return a, b


def _linear_scan(name, a, b, tr, reverse=False, shift_a=False):
    t_rows, cols = a.shape
    tc = _tile(cols, 512)
    ni, nj = t_rows // tr, cols // tc
    hb = tr // 8

    def rmap(j, i):
        return ((ni - 1 - i) if reverse else i, j)

    def halo_map(j, i):
        ri = ni - 1 - i
        return (jnp.minimum((ri + 1) * hb, t_rows // 8 - 1), j)

    def body(*refs):
        if shift_a:
            a_ref, halo_ref, b_ref, h_ref, carry_ref = refs
        else:
            a_ref, b_ref, h_ref, carry_ref = refs
        i = pl.program_id(1)
        row = lax.broadcasted_iota(jnp.int32, (tr, tc), 0)

        @pl.when(i == 0)
        def _():
            carry_ref[...] = jnp.zeros_like(carry_ref)

        av = a_ref[...]
        if shift_a:
            nxt = jnp.where(i > 0, halo_ref[0:1, :], 0.0)
            av = jnp.where(row == tr - 1, jnp.broadcast_to(nxt, (tr, tc)), pltpu.roll(av, tr - 1, 0))
        pa, hb_ = _scan_rows(av, b_ref[...], row, tr, reverse)
        h = hb_ + pa * carry_ref[0:1, :]
        h_ref[...] = h
        last = h[0:1, :] if reverse else h[tr - 1:tr, :]
        carry_ref[...] = jnp.broadcast_to(last, carry_ref.shape)

    in_specs = [pl.BlockSpec((tr, tc), rmap)]
    args = [a]
    if shift_a:
        in_specs.append(pl.BlockSpec((8, tc), halo_map))
        args.append(a)
    in_specs.append(pl.BlockSpec((tr, tc), rmap))
    args.append(b)
    return pl.pallas_call(
        body, name=name, grid=(nj, ni), in_specs=in_specs, out_specs=pl.BlockSpec((tr, tc), rmap),
        out_shape=jax.ShapeDtypeStruct((t_rows, cols), F32), scratch_shapes=[pltpu.VMEM((8, tc), F32)],
        compiler_params=_params(("parallel", "arbitrary")),
    )(*args)


def _conv_fwd(name, proj, conv_w, conv_b, width, tr):
    t_rows = proj.shape[0]
    ni = t_rows // tr
    hb = tr // 8
    tc = _tile(width, 512)

    def body(u_ref, halo_ref, w_ref, b_ref, o_ref, ext_ref):
        i = pl.program_id(1)
        ext_ref[0:8, :] = jnp.where(i > 0, halo_ref[...], 0.0)
        ext_ref[8:8 + tr, :] = u_ref[...]
        acc = b_ref[...] + w_ref[CONV_WIDTH - 1:CONV_WIDTH, :] * u_ref[...]
        for k in range(CONV_WIDTH - 1):
            acc = acc + w_ref[k:k + 1, :] * ext_ref[pl.ds(8 - (CONV_WIDTH - 1) + k, tr), :]
        o_ref[...] = acc

    return pl.pallas_call(
        body, name=name, grid=(width // tc, ni),
        in_specs=[pl.BlockSpec((tr, tc), lambda j, i: (i, j)),
                  pl.BlockSpec((8, tc), lambda j, i: (jnp.maximum(i * hb - 1, 0), j)),
                  pl.BlockSpec((CONV_WIDTH, tc), lambda j, i: (0, j)),
                  pl.BlockSpec((1, tc), lambda j, i: (0, j))],
        out_specs=pl.BlockSpec((tr, tc), lambda j, i: (i, j)),
        out_shape=jax.ShapeDtypeStruct((t_rows, width), F32),
        scratch_shapes=[pltpu.VMEM((tr + 8, tc), F32)],
        compiler_params=_params(("parallel", "parallel")),
    )(proj, proj, conv_w, conv_b)


def _conv_bwd(name, dxc, proj, conv_w, width, tr):
    t_rows = dxc.shape[0]
    ni = t_rows // tr
    hb = tr // 8
    tc = _tile(width, 512)

    def body(d_ref, dnext_ref, u_ref, uprev_ref, w_ref, du_ref, dw_ref, db_ref, dext_ref, uext_ref):
        i = pl.program_id(1)
        dext_ref[0:tr, :] = d_ref[...]
        dext_ref[tr:tr + 8, :] = jnp.where(i < ni - 1, dnext_ref[...], 0.0)
        uext_ref[0:8, :] = jnp.where(i > 0, uprev_ref[...], 0.0)
        uext_ref[8:8 + tr, :] = u_ref[...]
        d = d_ref[...]
        du = w_ref[CONV_WIDTH - 1:CONV_WIDTH, :] * d
        dws = []
        for k in range(CONV_WIDTH - 1):
            du = du + w_ref[k:k + 1, :] * dext_ref[pl.ds(CONV_WIDTH - 1 - k, tr), :]
            dws.append(jnp.sum(d * uext_ref[pl.ds(8 - (CONV_WIDTH - 1) + k, tr), :], axis=0, keepdims=True))
        dws.append(jnp.sum(d * u_ref[...], axis=0, keepdims=True))
        du_ref[...] = du
        dw = jnp.concatenate(dws, axis=0)
        db = jnp.sum(d, axis=0, keepdims=True)

        @pl.when(i == 0)
        def _():
            dw_ref[...] = dw
            db_ref[...] = db

        @pl.when(i > 0)
        def _():
            dw_ref[...] += dw
            db_ref[...] += db

    return pl.pallas_call(
        body, name=name, grid=(width // tc, ni),
        in_specs=[pl.BlockSpec((tr, tc), lambda j, i: (i, j)),
                  pl.BlockSpec((8, tc), lambda j, i: (jnp.minimum((i + 1) * hb, t_rows // 8 - 1), j)),
                  pl.BlockSpec((tr, tc), lambda j, i: (i, j)),
                  pl.BlockSpec((8, tc), lambda j, i: (jnp.maximum(i * hb - 1, 0), j)),
                  pl.BlockSpec((CONV_WIDTH, tc), lambda j, i: (0, j))],
        out_specs=[pl.BlockSpec((tr, tc), lambda j, i: (i, j)),
                   pl.BlockSpec((CONV_WIDTH, tc), lambda j, i: (0, j)),
                   pl.BlockSpec((1, tc), lambda j, i: (0, j))],
        out_shape=[jax.ShapeDtypeStruct((t_rows, width), F32), jax.ShapeDtypeStruct((CONV_WIDTH, width), F32),
                   jax.ShapeDtypeStruct((1, width), F32)],
        scratch_shapes=[pltpu.VMEM((tr + 8, tc), F32), pltpu.VMEM((tr + 8, tc), F32)],
        compiler_params=_params(("parallel", "arbitrary")),
    )(dxc, dxc, proj, proj, conv_w)


def _s5_disc_math(lam_re, lam_im, log_dt, b_re, b_im, expand):
    dt = jnp.exp(log_dt)
    zr, zi = lam_re * dt, lam_im * dt
    mag = jnp.exp(zr)
    lbr, lbi = mag * jnp.cos(zi), mag * jnp.sin(zi)
    ar, ai = lbr - 1.0, lbi
    den = lam_re * lam_re + lam_im * lam_im
    cr = (ar * lam_re + ai * lam_im) / den
    ci = (ai * lam_re - ar * lam_im) / den
    cre = jnp.dot(cr, expand, precision=lax.Precision.HIGHEST, preferred_element_type=F32)
    cie = jnp.dot(ci, expand, precision=lax.Precision.HIGHEST, preferred_element_type=F32)
    return lbr, lbi, cre * b_re - cie * b_im, cre * b_im + cie * b_re


def _s5_disc(name, lam_re, lam_im, log_dt, b_re, b_im, expand, cots=None):
    ins = [lam_re, lam_im, log_dt, b_re, b_im, expand]
    n_in = len(ins) + (len(cots) if cots else 0)

    def body(*refs):
        vals = [r[...] for r in refs[:6]]
        outs = refs[n_in:]
        if cots is None:
            res = _s5_disc_math(*vals)
        else:
            cv = tuple(r[...] for r in refs[6:n_in])
            _, vjp = jax.vjp(lambda a, b, c, d, e: _s5_disc_math(a, b, c, d, e, vals[5]), *vals[:5])
            res = vjp(cv)
        for o, r in zip(outs, res):
            o[...] = r

    if cots is None:
        shapes = [lam_re.shape, lam_re.shape, b_re.shape, b_re.shape]
    else:
        shapes = [lam_re.shape, lam_re.shape, log_dt.shape, b_re.shape, b_re.shape]
    vmem = pl.BlockSpec(memory_space=pltpu.VMEM)
    return pl.pallas_call(
        body, name=name, in_specs=[vmem] * n_in, out_specs=[vmem] * len(shapes),
        out_shape=[jax.ShapeDtypeStruct(s, F32) for s in shapes],
        compiler_params=pltpu.CompilerParams(vmem_limit_bytes=VMEM_LIMIT),
    )(*ins, *(cots or ()))


def _cmul(ar, ai, br, bi):
    return ar * br - ai * bi, ar * bi + ai * br


def _cscan(br, bi, lr, li, row, tr, reverse):
    pr, pi = lr, li
    sh = 1
    while sh < tr:
        if reverse:
            valid = row < tr - sh
            rs, is_ = pltpu.roll(br, tr - sh, 0), pltpu.roll(bi, tr - sh, 0)
        else:
            valid = row >= sh
            rs, is_ = pltpu.roll(br, sh, 0), pltpu.roll(bi, sh, 0)
        mr, mi = _cmul(pr, pi, rs, is_)
        br = br + jnp.where(valid, mr, 0.0)
        bi = bi + jnp.where(valid, mi, 0.0)
        pr, pi = _cmul(pr, pi, pr, pi)
        sh *= 2
    return br, bi


def _s5_fwd(name, proj, col0, bblk, cblk, lamblk, dvec, width, tr):
    t_rows = proj.shape[0]
    ni = t_rows // tr
    nb = width // LANES
    sw = bblk.shape[-1]
    hw = sw // 2

    def body(u_ref, b_ref, c_ref, lam_ref, d_ref, y_ref, s_ref, carry_ref, pow_ref):
        i = pl.program_id(1)
        row = lax.broadcasted_iota(jnp.int32, (tr, hw), 0)
        lam = lam_ref[...]
        lr, li = lam[:, :hw], lam[:, hw:]

        @pl.when(i == 0)
        def _():
            carry_ref[...] = jnp.zeros_like(carry_ref)
            zr, zi = _cscan(jnp.broadcast_to(lr, (tr, hw)) * jnp.where(row == 0, 1.0, 0.0),
                            jnp.broadcast_to(li, (tr, hw)) * jnp.where(row == 0, 1.0, 0.0), lr, li, row, tr, False)
            pow_ref[:, :hw] = zr
            pow_ref[:, hw:] = zi

        u = u_ref[...]
        bu = jnp.dot(u.astype(BF16), b_ref[...], preferred_element_type=F32)
        sr, si = _cscan(bu[:, :hw], bu[:, hw:], lr, li, row, tr, False)
        cr, ci = carry_ref[0:1, :hw], carry_ref[0:1, hw:]
        kr, ki = _cmul(pow_ref[:, :hw], pow_ref[:, hw:], cr, ci)
        sr, si = sr + kr, si + ki
        s = jnp.concatenate([sr, si], axis=1)
        carry_ref[...] = jnp.broadcast_to(s[tr - 1:tr, :], carry_ref.shape)
        s16 = s.astype(BF16)
        s_ref[...] = s16
        y_ref[...] = jnp.dot(s16, c_ref[...], preferred_element_type=F32) + d_ref[...] * u

    return pl.pallas_call(
        body, name=name, grid=(nb, ni),
        in_specs=[pl.BlockSpec((tr, LANES), lambda k, i: (i, col0 + k)),
                  pl.BlockSpec((None, LANES, sw), lambda k, i: (k, 0, 0)),
                  pl.BlockSpec((None, sw, LANES), lambda k, i: (k, 0, 0)),
                  pl.BlockSpec((None, 1, sw), lambda k, i: (k, 0, 0)),
                  pl.BlockSpec((1, LANES), lambda k, i: (0, k))],
        out_specs=[pl.BlockSpec((tr, LANES), lambda k, i: (i, k)),
                   pl.BlockSpec((tr, sw), lambda k, i: (i, k))],
        out_shape=[jax.ShapeDtypeStruct((t_rows, width), F32), jax.ShapeDtypeStruct((t_rows, nb * sw), BF16)],
        scratch_shapes=[pltpu.VMEM((8, sw), F32), pltpu.VMEM((tr, sw), F32)],
        compiler_params=_params(("parallel", "arbitrary")),
    )(proj, bblk, cblk, lamblk, dvec)


def _s5_bwd(name, dy, proj, col0, states, bblk, cblk, lamblk, dvec, width, tr):
    t_rows = dy.shape[0]
    ni = t_rows // tr
    nb = width // LANES
    sw = bblk.shape[-1]
    hw = sw // 2
    hb16 = tr // 16

    def rmap(k, i):
        return (ni - 1 - i, k)

    def body(dy_ref, u_ref, s_ref, sprev_ref, b_ref, c_ref, lam_ref, d_ref,
             du_ref, db_ref, dc_ref, dlam_ref, dd_ref, carry_ref, pow_ref):
        i = pl.program_id(1)
        row = lax.broadcasted_iota(jnp.int32, (tr, hw), 0)
        row_w = lax.broadcasted_iota(jnp.int32, (tr, sw), 0)
        lam = lam_ref[...]
        lr, li = lam[:, :hw], -lam[:, hw:]

        @pl.when(i == 0)
        def _():
            carry_ref[...] = jnp.zeros_like(carry_ref)
            zr, zi = _cscan(jnp.broadcast_to(lr, (tr, hw)) * jnp.where(row == tr - 1, 1.0, 0.0),
                            jnp.broadcast_to(li, (tr, hw)) * jnp.where(row == tr - 1, 1.0, 0.0), lr, li, row, tr, True)
            pow_ref[:, :hw] = zr
            pow_ref[:, hw:] = zi

        dyv = dy_ref[...]
        dy16 = dyv.astype(BF16)
        u = u_ref[...]
        gd = lax.dot_general(dy16, c_ref[...], _DN['nt'], preferred_element_type=F32)
        gr, gi = _cscan(gd[:, :hw], gd[:, hw:], lr, li, row, tr, True)
        cr, ci = carry_ref[0:1, :hw], carry_ref[0:1, hw:]
        kr, ki = _cmul(pow_ref[:, :hw], pow_ref[:, hw:], cr, ci)
        gr, gi = gr + kr, gi + ki
        g = jnp.concatenate([gr, gi], axis=1)
        carry_ref[...] = jnp.broadcast_to(g[0:1, :], carry_ref.shape)
        g16 = g.astype(BF16)
        du_ref[...] = lax.dot_general(g16, b_ref[...], _DN['nt'], preferred_element_type=F32) + d_ref[...] * dyv
        s16 = s_ref[...]
        dbv = lax.dot_general(g16, u.astype(BF16), _DN['tn'], preferred_element_type=F32)
        dcv = lax.dot_general(s16, dy16, _DN['tn'], preferred_element_type=F32)
        s32 = s16.astype(F32)
        first = jnp.where(i < ni - 1, sprev_ref[15:16, :].astype(F32), 0.0)
        sp = jnp.where(row_w == 0, jnp.broadcast_to(first, (tr, sw)), pltpu.roll(s32, 1, 0))
        spr, spi = sp[:, :hw], sp[:, hw:]
        dlr = jnp.sum(gr * spr + gi * spi, axis=0, keepdims=True)
        dli = jnp.sum(gi * spr - gr * spi, axis=0, keepdims=True)
        dlam = jnp.concatenate([dlr, dli], axis=1)
        ddv = jnp.sum(dyv * u, axis=0, keepdims=True)

        @pl.when(i == 0)
        def _():
            db_ref[...] = dbv
            dc_ref[...] = dcv
            dlam_ref[...] = dlam
            dd_ref[...] = ddv

        @pl.when(i > 0)
        def _():
            db_ref[...] += dbv
            dc_ref[...] += dcv
            dlam_ref[...] += dlam
            dd_ref[...] += ddv

    return pl.pallas_call(
        body, name=name, grid=(nb, ni),
        in_specs=[pl.BlockSpec((tr, LANES), rmap),
                  pl.BlockSpec((tr, LANES), lambda k, i: (ni - 1 - i, col0 + k)),
                  pl.BlockSpec((tr, sw), rmap),
                  pl.BlockSpec((16, sw), lambda k, i: (jnp.maximum((ni - 1 - i) * hb16 - 1, 0), k)),
                  pl.BlockSpec((None, LANES, sw), lambda k, i: (k, 0, 0)),
                  pl.BlockSpec((None, sw, LANES), lambda k, i: (k, 0, 0)),
                  pl.BlockSpec((None, 1, sw), lambda k, i: (k, 0, 0)),
                  pl.BlockSpec((1, LANES), lambda k, i: (0, k))],
        out_specs=[pl.BlockSpec((tr, LANES), rmap),
                   pl.BlockSpec((None, sw, LANES), lambda k, i: (k, 0, 0)),
                   pl.BlockSpec((None, sw, LANES), lambda k, i: (k, 0, 0)),
                   pl.BlockSpec((None, 1, sw), lambda k, i: (k, 0, 0)),
                   pl.BlockSpec((1, LANES), lambda k, i: (0, k))],
        out_shape=[jax.ShapeDtypeStruct((t_rows, width), F32), jax.ShapeDtypeStruct((nb, sw, LANES), F32),
                   jax.ShapeDtypeStruct((nb, sw, LANES), F32), jax.ShapeDtypeStruct((nb, 1, sw), F32),
                   jax.ShapeDtypeStruct((1, width), F32)],
        scratch_shapes=[pltpu.VMEM((8, sw), F32), pltpu.VMEM((tr, sw), F32)],
        compiler_params=_params(("parallel", "arbitrary")),
    )(dy, proj, states, states, bblk, cblk, lamblk, dvec)


def _blockdiag(m):
    nb, g, p, q = m.shape
    eye = jnp.eye(g, dtype=m.dtype)
    return (m[:, :, :, None, :] * eye[None, :, None, :, None]).reshape(nb, g * p, g * q)


def _blockdiag_take(m, p, q):
    nb = m.shape[0]
    g = GROUPS_PER_BLOCK
    m = m.reshape(nb, g, p, g, q)
    return jnp.stack([m[:, k, :, k, :] for k in range(g)], axis=1)


def _adamw(name, w, g, m, v, tr):
    rows, cols = w.shape
    gcols = g.shape[1]
    c1 = 1.0 - ADAM_B1 ** ADAM_STEP
    c2 = 1.0 - ADAM_B2 ** ADAM_STEP

    def body(w_ref, g_ref, m_ref, v_ref, go_ref, d_ref, mo_ref, vo_ref):
        gv = g_ref[...] if gcols == cols else g_ref[:, :cols]
        mn = ADAM_B1 * m_ref[...] + (1.0 - ADAM_B1) * gv
        vn = ADAM_B2 * v_ref[...] + (1.0 - ADAM_B2) * (gv * gv)
        go_ref[...] = gv
        mo_ref[...] = mn
        vo_ref[...] = vn
        d_ref[...] = -ADAM_LR * ((mn / c1) / (jnp.sqrt(vn / c2) + ADAM_EPS) + ADAM_WD * w_ref[...])

    spec = pl.BlockSpec((tr, cols), lambda i: (i, 0))
    return pl.pallas_call(
        body, name=name, grid=(rows // tr,),
        in_specs=[spec, pl.BlockSpec((tr, gcols), lambda i: (i, 0)), spec, spec],
        out_specs=[spec] * 4, out_shape=[jax.ShapeDtypeStruct((rows, cols), F32)] * 4,
        compiler_params=_params(("parallel",)),
    )(w, g, m, v)


def _ffn_fwd(tag, h, norm, wg, wu, wd, dims):
    t_rows, d, fp, tm, tr, trn = dims['T'], dims['D'], dims['FP'], dims['TM'], dims['TR'], dims['TRN']
    ft = N_CHIPS * fp
    tn = _tile(fp, 1408)
    npb = fp // tn
    n16, = _rowmap(f"{tag}_norm", lambda x, g: _rms(x, g).astype(BF16), (1, t_rows // trn),
                   [_rows(h, trn), _full(norm)], [_o((t_rows, d), BF16, trn)])

    def up(nm, w):
        return _matmul(f"{tag}_{nm}", 'nn', n16, (tm, d // 2), lambda i, j, k: (i, k),
                       w, (None, d // 2, tn), lambda i, j, k: (j // npb, k, j % npb),
                       (t_rows // tm, ft // tn, 2), [((t_rows, ft), BF16, (tm, tn), lambda i, j: (i, j))],
                       lambda acc: acc)[0]

    gate, upv = up("gate", wg), up("up", wu)
    tc = _tile(ft, 1408)
    act, = _rowmap(f"{tag}_act", lambda g, u: _silu(g.astype(F32)) * u.astype(F32), (ft // tc, t_rows // tr),
                   [_rows(gate, tr, tc), _rows(upv, tr, tc)], [_o((t_rows, ft), BF16, tr, tc)])
    tnd = _tile(d, 1024)
    hn, = _matmul(f"{tag}_down", 'nn', act, (tm, fp), lambda i, j, k: (i, k),
                  wd, (None, fp, tnd), lambda i, j, k: (k, 0, j),
                  (t_rows // tm, d // tnd, N_CHIPS), [((t_rows, d), F32, (tm, tnd), lambda i, j: (i, j))],
                  lambda acc, hin: hin + 0.5 * acc, extras=[(h, (tm, tnd), lambda i, j: (i, j))])
    return hn, dict(h=h, n16=n16, gate=gate, up=upv, act=act)


def _ffn_bwd(tag, dh, dh16, saved, norm, wg, wu, wd, dims):
    t_rows, d, fp, tm, trn = dims['T'], dims['D'], dims['FP'], dims['TM'], dims['TRN']
    ft = N_CHIPS * fp
    tn = _tile(fp, 1408)
    npb = fp // tn
    tm2 = tm // 2
    tk = t_rows // 2

    def act_bwd(acc, g, u):
        da = 0.5 * acc
        g32, u32 = g.astype(F32), u.astype(F32)
        sg = _sigmoid(g32)
        return da * u32 * (sg * (1.0 + g32 * (1.0 - sg))), da * (g32 * sg)

    dgate, dup = _matmul(
        f"{tag}_dact", 'nt', dh16, (tm2, d // 2), lambda i, j, k: (i, k),
        wd, (None, tn, d // 2), lambda i, j, k: (j // npb, j % npb, k),
        (t_rows // tm2, ft // tn, 2),
        [((t_rows, ft), BF16, (tm2, tn), lambda i, j: (i, j))] * 2, act_bwd,
        extras=[(saved['gate'], (tm2, tn), lambda i, j: (i, j)), (saved['up'], (tm2, tn), lambda i, j: (i, j))])
    tnd = _tile(d, 1024)
    dwd, = _matmul(f"{tag}_dwd", 'tn', saved['act'], (tk, tn), lambda i, j, k: (k, i),
                   dh16, (tk, tnd), lambda i, j, k: (k, j),
                   (ft // tn, d // tnd, 2),
                   [((N_CHIPS, fp, d), BF16, (None, tn, tnd), lambda i, j: (i // npb, i % npb, j))],
                   lambda acc: 0.5 * acc)

    def dn_part(nm, dact, w, prev):
        extras = [] if prev is None else [(prev, (tm, tnd), lambda i, j: (i, j))]
        return _matmul(f"{tag}_{nm}", 'nt', dact, (tm, fp), lambda i, j, k: (i, k),
                       w, (None, tnd, fp), lambda i, j, k: (k, j, 0),
                       (t_rows // tm, d // tnd, N_CHIPS), [((t_rows, d), F32, (tm, tnd), lambda i, j: (i, j))],
                       (lambda acc: acc) if prev is None else (lambda acc, p: acc + p), extras=extras)[0]

    dn = dn_part("dn_up", dup, wu, dn_part("dn_gate", dgate, wg, None))

    def dw_up(nm, dact):
        return _matmul(f"{tag}_{nm}", 'tn', saved['n16'], (tk, tnd), lambda i, j, k: (k, i),
                       dact, (tk, tn), lambda i, j, k: (k, j),
                       (d // tnd, ft // tn, 2),
                       [((N_CHIPS, d, fp), BF16, (None, tnd, tn), lambda i, j: (j // npb, i, j % npb))],
                       lambda acc: acc)[0]

    dwg, dwu = dw_up("dwg", dgate), dw_up("dwu", dup)

    def norm_bwd(x, dnv, dhv, g):
        _, vjp = jax.vjp(_rms, x, g)
        dx, dg = vjp(dnv)
        res = dhv + dx
        return res, res.astype(BF16), dg

    dh_in, dh_in16, dnorm = _rowmap(
        f"{tag}_dnorm", norm_bwd, (1, t_rows // trn),
        [_rows(saved['h'], trn), _rows(dn, trn), _rows(dh, trn), _full(norm)],
        [_o((t_rows, d), F32, trn), _o((t_rows, d), BF16, trn)], [_acc((1, d))])
    return dh_in, dh_in16, dnorm, dwg, dwu, dwd


def kernel(x, meta_tokens, ffn1_norm, ffn1_w_gate, ffn1_w_up, ffn1_w_down, mix_norm, w_in, rg_conv_w, rg_conv_b, rg_w_a, rg_b_a, rg_w_x, rg_b_x, rg_lambda, s5_lambda_re, s5_lambda_im, s5_log_dt, s5_b_re, s5_b_im, s5_c_re, s5_c_im, s5_d, s5_glu_w, s5_glu_b, rg_out_norm, s5_out_norm, w_out, ffn2_norm, ffn2_w_gate, ffn2_w_up, ffn2_w_down, final_norm, loss_target, m_meta_tokens, m_ffn1_norm, m_ffn1_w_gate, m_ffn1_w_up, m_ffn1_w_down, m_mix_norm, m_w_in, m_rg_conv_w, m_rg_conv_b, m_rg_w_a, m_rg_b_a, m_rg_w_x, m_rg_b_x, m_rg_lambda, m_s5_lambda_re, m_s5_lambda_im, m_s5_log_dt, m_s5_b_re, m_s5_b_im, m_s5_c_re, m_s5_c_im, m_s5_d, m_s5_glu_w, m_s5_glu_b, m_rg_out_norm, m_s5_out_norm, m_w_out, m_ffn2_norm, m_ffn2_w_gate, m_ffn2_w_up, m_ffn2_w_down, m_final_norm, v_meta_tokens, v_ffn1_norm, v_ffn1_w_gate, v_ffn1_w_up, v_ffn1_w_down, v_mix_norm, v_w_in, v_rg_conv_w, v_rg_conv_b, v_rg_w_a, v_rg_b_a, v_rg_w_x, v_rg_b_x, v_rg_lambda, v_s5_lambda_re, v_s5_lambda_im, v_s5_log_dt, v_s5_b_re, v_s5_b_im, v_s5_c_re, v_s5_c_im, v_s5_d, v_s5_glu_w, v_s5_glu_b, v_rg_out_norm, v_s5_out_norm, v_w_out, v_ffn2_norm, v_ffn2_w_gate, v_ffn2_w_up, v_ffn2_w_down, v_final_norm):
    args = locals()
    w = {n: args[n] for n in WEIGHTS}
    mom = {n: args["m_" + n] for n in WEIGHTS}
    var = {n: args["v_" + n] for n in WEIGHTS}

    seq, d = x.shape[1], x.shape[2]
    f_shard = ffn1_w_gate.shape[2]
    fp = _round_up(f_shard, LANES)
    r = rg_conv_b.shape[1]
    s5w = s5_d.shape[1]
    hd = r // RG_HEADS
    groups = s5w // S5_GROUP
    nb = groups // GROUPS_PER_BLOCK
    t_rows = _round_up(N_META + seq, LANES)
    tm, tr, tk = t_rows // 4, t_rows // 8, t_rows // 2
    trn = _tile(t_rows, max(t_rows // 32, 16), 16)
    dims = dict(T=t_rows, D=d, FP=fp, TM=tm, TR=tr, TRN=trn)
    cx, cy, cc = lax.axis_index("x"), lax.axis_index("y"), lax.axis_index("c")
    chip = 2 * cx + cy
    pos = jnp.stack([chip, cc]).astype(jnp.int32)

    def gather(nm, local):
        return _all_gather_chips(f"ag_{nm}", local)

    wfull = {}
    for nm in ('ffn1_w_gate', 'ffn1_w_up', 'ffn2_w_gate', 'ffn2_w_up'):
        wfull[nm] = gather(nm, _cast_pad(f"cast_{nm}", w[nm][0], d, fp))
    for nm in ('ffn1_w_down', 'ffn2_w_down'):
        wfull[nm] = gather(nm, _cast_pad(f"cast_{nm}", w[nm][0], fp, d))
    wfull['w_in'] = gather('w_in', _cast_pad("cast_w_in", w_in[0], d, w_in.shape[2]))
    wfull['w_out'] = gather('w_out', _cast_pad("cast_w_out", w_out[0], w_out.shape[1], d)).reshape(d, d)
    wfull['s5_glu_w'] = gather('glu', _cast_pad("cast_glu", s5_glu_w[0], s5_glu_w.shape[1], s5w)).reshape(s5w, s5w)
    rgw_local = jnp.concatenate([rg_w_a[0], rg_w_x[0]], axis=0).reshape(2 * RG_HEADS * (hd // N_CHIPS), hd)
    rgw = gather('rgw', _cast_pad("cast_rgw", rgw_local, rgw_local.shape[0], hd))
    rgw = rgw.reshape(N_CHIPS, 2, RG_HEADS, hd // N_CHIPS, hd).transpose(1, 2, 0, 3, 4).reshape(2, RG_HEADS, hd, hd)
    w_a16, w_x16 = rgw[0], rgw[1]
    meta_full = gather('meta', meta_tokens).transpose(1, 0, 2).reshape(N_META, d)

    pad_rows = t_rows - N_META - seq
    h0 = jnp.concatenate([meta_full, x[0], jnp.zeros((pad_rows, d), F32)], axis=0)
    tgt = jnp.concatenate([jnp.zeros((N_META, d), F32), loss_target[0], jnp.zeros((pad_rows, d), F32)], axis=0)

    h1, sv1 = _ffn_fwd("ffn1", h0, ffn1_norm, wfull['ffn1_w_gate'], wfull['ffn1_w_up'], wfull['ffn1_w_down'], dims)

    n2, = _rowmap("mix_norm", lambda xv, g: _rms(xv, g).astype(BF16), (1, t_rows // trn),
                  [_rows(h1, trn), _full(mix_norm)], [_o((t_rows, d), BF16, trn)])
    pw = w_in.shape[2]
    tnp = _tile(pw, 1536)
    nppb = pw // tnp
    proj, = _matmul("in_proj", 'nn', n2, (tm, d // 2), lambda i, j, k: (i, k),
                    wfull['w_in'], (None, d // 2, tnp), lambda i, j, k: (j // nppb, k, j % nppb),
                    (t_rows // tm, N_CHIPS * nppb, 2), [((t_rows, 3 * r), F32, (tm, tnp), lambda i, j: (i, j))],
                    lambda acc: acc)

    conv_w2, b_a2, b_x2 = _gather_small_shards(rg_conv_w[0], rg_b_a[0], rg_b_x[0], gather)
    xc = _conv_fwd("rg_conv", proj, conv_w2, rg_conv_b, r, tr)

    def head_cols(arr):
        return (arr, (tr, hd), lambda j, i: (i, j))

    def head_w(arr):
        return (arr, (None, hd, hd), lambda j, i: (j, 0, 0))

    def head_vec(arr):
        return (arr, (1, hd), lambda j, i: (0, j))

    def gates_fwd(xcv, wa, wx, ba, bx, lam):
        x16 = xcv.astype(BF16)
        pre_r = jnp.dot(x16, wa, preferred_element_type=F32) + ba
        pre_i = jnp.dot(x16, wx, preferred_element_type=F32) + bx
        return _rg_gate_math(xcv, pre_r, pre_i, _softplus_neg(lam))

    gate_params = [head_w(w_a16), head_w(w_x16), head_vec(b_a2), head_vec(b_x2), head_vec(rg_lambda)]
    head_out = ((t_rows, r), F32, (tr, hd), lambda j, i: (i, j))
    a_dec, bxv = _rowmap("rg_gates", gates_fwd, (RG_HEADS, t_rows // tr), [head_cols(xc)] + gate_params,
                         [head_out, head_out])
    h_rg = _linear_scan("rg_scan", a_dec, bxv, tr)

    expand = jnp.repeat(jnp.eye(S5_STATE, dtype=F32), S5_GROUP, axis=1)
    b_re2 = s5_b_re[0].reshape(groups, S5_STATE * S5_GROUP)
    b_im2 = s5_b_im[0].reshape(groups, S5_STATE * S5_GROUP)
    log_dt2 = s5_log_dt[0].reshape(groups, 1)
    lbr, lbi, bbr, bbi = _s5_disc("s5_disc", s5_lambda_re[0], s5_lambda_im[0], log_dt2, b_re2, b_im2, expand)

    def to_bblk(m):
        return _blockdiag(m.reshape(nb, GROUPS_PER_BLOCK, S5_STATE, S5_GROUP).transpose(0, 1, 3, 2))

    def to_cblk(m):
        return _blockdiag(m.reshape(nb, GROUPS_PER_BLOCK, S5_GROUP, S5_STATE).transpose(0, 1, 3, 2))

    bblk = jnp.concatenate([to_bblk(bbr), to_bblk(bbi)], axis=-1).astype(BF16)
    cblk = jnp.concatenate([to_cblk(s5_c_re[0]), -to_cblk(s5_c_im[0])], axis=-2).astype(BF16)
    hw = GROUPS_PER_BLOCK * S5_STATE
    lamblk = jnp.concatenate([lbr.reshape(nb, 1, hw), lbi.reshape(nb, 1, hw)], axis=-1)
    col0 = 2 * r // LANES
    y_pre, states = _s5_fwd("s5_fwd", proj, col0, bblk, cblk, lamblk, s5_d, s5w, tr)

    z16, = _rowmap("s5_gelu", lambda yv: _gelu(yv).astype(BF16), (1, t_rows // trn),
                   [_rows(y_pre, trn)], [_o((t_rows, s5w), BF16, trn)])
    tns = _tile(s5w, 1024)

    def glu_fwd(acc, yv, b):
        gl = acc + b
        return _gelu(yv) * _sigmoid(gl), gl

    y_s5, gl = _matmul("s5_glu", 'nn', z16, (tm, s5w), lambda i, j, k: (i, 0),
                       wfull['s5_glu_w'], (s5w, tns), lambda i, j, k: (0, j),
                       (t_rows // tm, s5w // tns, 1), [((t_rows, s5w), F32, (tm, tns), lambda i, j: (i, j))] * 2,
                       glu_fwd, extras=[(y_pre, (tm, tns), lambda i, j: (i, j)), (s5_glu_b, (1, tns), lambda i, j: (0, j))])

    def mix_out(hv, gv, yv, g1, g2):
        return jnp.concatenate([_rms(hv * _gelu(gv), g1), _rms(yv, g2)], axis=1)

    gate_in = (proj, (trn, r), lambda j, i: (i, 1))
    ycat16, = _rowmap("mix_out", lambda *a: mix_out(*a).astype(BF16), (1, t_rows // trn),
                      [_rows(h_rg, trn), gate_in, _rows(y_s5, trn), _full(rg_out_norm), _full(s5_out_norm)],
                      [_o((t_rows, r + s5w), BF16, trn)])
    tnd = _tile(d, 1024)
    h2, = _matmul("out_proj", 'nn', ycat16, (tm, (r + s5w) // 2), lambda i, j, k: (i, k),
                  wfull['w_out'], ((r + s5w) // 2, tnd), lambda i, j, k: (k, j),
                  (t_rows // tm, d // tnd, 2), [((t_rows, d), F32, (tm, tnd), lambda i, j: (i, j))],
                  lambda acc, hin: hin + acc, extras=[(h1, (tm, tnd), lambda i, j: (i, j))])

    h3, sv2 = _ffn_fwd("ffn2", h2, ffn2_norm, wfull['ffn2_w_gate'], wfull['ffn2_w_up'], wfull['ffn2_w_down'], dims)

    fnorm2 = final_norm.reshape(1, d)

    def head(xv, tv, g):
        i = pl.program_id(1)
        rowi = lax.broadcasted_iota(jnp.int32, (trn, 1), 0) + i * trn
        mask = jnp.where((rowi >= N_META) & (rowi < N_META + seq), 1.0, 0.0)
        out, vjp = jax.vjp(_rms, xv, g)
        err = (out - tv) * mask
        dx, dg = vjp(err * (1.0 / d))
        return dx, dx.astype(BF16), jnp.sum(err * err, axis=0, keepdims=True), dg

    dh3, dh3_16, loss_cols, d_final = _rowmap(
        "loss_head", head, (1, t_rows // trn), [_rows(h3, trn), _rows(tgt, trn), _full(fnorm2)],
        [_o((t_rows, d), F32, trn), _o((t_rows, d), BF16, trn)], [_acc((1, d)), _acc((1, d))])
    loss = lax.psum(0.5 * jnp.sum(loss_cols) / d, ("x", "y", "c"))

    small = {}
    big = {}
    dh2, dh2_16, small['ffn2_norm'], big['ffn2_w_gate'], big['ffn2_w_up'], big['ffn2_w_down'] = _ffn_bwd(
        "ffn2b", dh3, dh3_16, sv2, ffn2_norm, wfull['ffn2_w_gate'], wfull['ffn2_w_up'], wfull['ffn2_w_down'], dims)

    dycat, = _matmul("d_out_proj", 'nt', dh2_16, (tm, d // 2), lambda i, j, k: (i, k),
                     wfull['w_out'], (tnd, d // 2), lambda i, j, k: (j, k),
                     (t_rows // tm, (r + s5w) // tnd, 2), [((t_rows, r + s5w), F32, (tm, tnd), lambda i, j: (i, j))],
                     lambda acc: acc)
    wo_rows = (r + s5w) // N_CHIPS
    tno = _tile(wo_rows, 1024)
    npo = wo_rows // tno
    big['w_out'], = _matmul("dw_out", 'tn', ycat16, (tk, tno), lambda i, j, k: (k, i),
                            dh2_16, (tk, tnd), lambda i, j, k: (k, j),
                            ((r + s5w) // tno, d // tnd, 2),
                            [((N_CHIPS, wo_rows, d), BF16, (None, tno, tnd), lambda i, j: (i // npo, i % npo, j))],
                            lambda acc: acc)

    def mix_out_bwd(hv, gv, yv, dyc, g1, g2):
        _, vjp = jax.vjp(mix_out, hv, gv, yv, g1, g2)
        return vjp(dyc)

    dh_out, dgate_rg, dy_s5, small['rg_out_norm'], small['s5_out_norm'] = _rowmap(
        "mix_out_bwd", mix_out_bwd, (1, t_rows // trn),
        [_rows(h_rg, trn), gate_in, _rows(y_s5, trn), _rows(dycat, trn), _full(rg_out_norm), _full(s5_out_norm)],
        [_o((t_rows, r), F32, trn), _o((t_rows, r), F32, trn), _o((t_rows, s5w), F32, trn)],
        [_acc((1, r)), _acc((1, s5w))])

    def glu_bwd(dyv, yv, glv):
        zv = _gelu(yv)
        sg = _sigmoid(glv)
        dgl = dyv * zv * sg * (1.0 - sg)
        return dyv * sg, dgl.astype(BF16), jnp.sum(dgl, axis=0, keepdims=True)

    dz_dir, dgl16, small['s5_glu_b'] = _rowmap(
        "s5_glu_bwd", glu_bwd, (1, t_rows // trn), [_rows(dy_s5, trn), _rows(y_pre, trn), _rows(gl, trn)],
        [_o((t_rows, s5w), F32, trn), _o((t_rows, s5w), BF16, trn)], [_acc((1, s5w))])

    def dgelu(acc, dzd, yv):
        _, vjp = jax.vjp(_gelu, yv)
        return vjp(acc + dzd)[0]

    dy_pre, = _matmul("s5_dz", 'nt', dgl16, (tm, s5w), lambda i, j, k: (i, 0),
                      wfull['s5_glu_w'], (tns, s5w), lambda i, j, k: (j, 0),
                      (t_rows // tm, s5w // tns, 1), [((t_rows, s5w), F32, (tm, tns), lambda i, j: (i, j))],
                      dgelu, extras=[(dz_dir, (tm, tns), lambda i, j: (i, j)), (y_pre, (tm, tns), lambda i, j: (i, j))])
    gl_rows = s5w // N_CHIPS
    tng = _tile(gl_rows, 1024)
    npg = gl_rows // tng
    big['s5_glu_w'], = _matmul("dw_glu", 'tn', z16, (tk, tng), lambda i, j, k: (k, i),
                               dgl16, (tk, tns), lambda i, j, k: (k, j),
                               (s5w // tng, s5w // tns, 2),
                               [((N_CHIPS, gl_rows, s5w), BF16, (None, tng, tns), lambda i, j: (i // npg, i % npg, j))],
                               lambda acc: acc)
    du_s5, dbblk, dcblk, dlamblk, small['s5_d'] = _s5_bwd(
        "s5_bwd", dy_pre, proj, col0, states, bblk, cblk, lamblk, s5_d, s5w, tr)

    def from_bblk(m):
        return _blockdiag_take(m, S5_STATE, S5_GROUP).reshape(groups, S5_STATE * S5_GROUP)

    dbbr, dbbi = from_bblk(dbblk[:, :hw]), from_bblk(dbblk[:, hw:])
    dcs = lambda m: _blockdiag_take(m, S5_STATE, S5_GROUP).transpose(0, 1, 3, 2).reshape(1, groups, S5_GROUP, S5_STATE)
    small['s5_c_re'], small['s5_c_im'] = dcs(dcblk[:, :hw]), -dcs(dcblk[:, hw:])
    dlbr, dlbi = dlamblk[:, 0, :hw].reshape(groups, S5_STATE), dlamblk[:, 0, hw:].reshape(groups, S5_STATE)
    g_lre, g_lim, g_ldt, g_bre, g_bim = _s5_disc(
        "s5_disc_bwd", s5_lambda_re[0], s5_lambda_im[0], log_dt2, b_re2, b_im2, expand, cots=(dlbr, dlbi, dbbr, dbbi))
    small['s5_lambda_re'], small['s5_lambda_im'], small['s5_log_dt'] = g_lre[None], g_lim[None], g_ldt.reshape(1, groups)
    small['s5_b_re'] = g_bre.reshape(s5_b_re.shape)
    small['s5_b_im'] = g_bim.reshape(s5_b_im.shape)

    dh_rg = _linear_scan("rg_scan_bwd", a_dec, dh_out, tr, reverse=True, shift_a=True)

    hb = tr // 8

    def gates_bwd(xcv, dhv, hv, hprev, wa, wx, ba, bx, lam):
        i = pl.program_id(1)
        row = lax.broadcasted_iota(jnp.int32, (tr, hd), 0)
        first = jnp.where(i > 0, hprev[7:8, :], 0.0)
        h_prev = jnp.where(row == 0, jnp.broadcast_to(first, (tr, hd)), pltpu.roll(hv, 1, 0))
        x16 = xcv.astype(BF16)
        pre_r = jnp.dot(x16, wa, preferred_element_type=F32) + ba
        pre_i = jnp.dot(x16, wx, preferred_element_type=F32) + bx
        _, vjp = jax.vjp(_rg_gate_math, xcv, pre_r, pre_i, _softplus_neg(lam))
        dxc_, dpr, dpi, dsp = vjp((dhv * h_prev, dhv))
        dpr16, dpi16 = dpr.astype(BF16), dpi.astype(BF16)
        dxc_ = (dxc_ + lax.dot_general(dpr16, wa, _DN['nt'], preferred_element_type=F32)
                + lax.dot_general(dpi16, wx, _DN['nt'], preferred_element_type=F32))
        dwa = lax.dot_general(x16, dpr16, _DN['tn'], preferred_element_type=F32)
        dwx = lax.dot_general(x16, dpi16, _DN['tn'], preferred_element_type=F32)
        dlam = dsp * (-_sigmoid(-lam))
        return (dxc_, dwa, dwx, jnp.sum(dpr, axis=0, keepdims=True), jnp.sum(dpi, axis=0, keepdims=True), dlam)

    head_acc_w = ((RG_HEADS, hd, hd), F32, (None, hd, hd), lambda j, i: (j, 0, 0))
    head_acc_v = ((1, r), F32, (1, hd), lambda j, i: (0, j))
    h_halo = (h_rg, (8, hd), lambda j, i: (jnp.maximum(i * hb - 1, 0), j))
    dxc, d_wa, d_wx, d_ba, d_bx, d_lam = _rowmap(
        "rg_gates_bwd", gates_bwd, (RG_HEADS, t_rows // tr),
        [head_cols(xc), head_cols(dh_rg), head_cols(h_rg), h_halo] + gate_params,
        [head_out], [head_acc_w, head_acc_w, head_acc_v, head_acc_v, head_acc_v])
    small['rg_w_a'], small['rg_w_x'] = d_wa[None], d_wx[None]
    small['rg_b_a'], small['rg_b_x'] = d_ba.reshape(1, RG_HEADS, hd), d_bx.reshape(1, RG_HEADS, hd)
    small['rg_lambda'] = d_lam
    du_rg, d_convw, small['rg_conv_b'] = _conv_bwd("rg_conv_bwd", dxc, proj, conv_w2, r, tr)
    small['rg_conv_w'] = d_convw[None]

    dproj16, = _rowmap("dproj", lambda a, b, c: jnp.concatenate([a, b, c], axis=1).astype(BF16), (1, t_rows // trn),
                       [_rows(du_rg, trn), _rows(dgate_rg, trn), _rows(du_s5, trn)], [_o((t_rows, 3 * r), BF16, trn)])
    dn2, = _matmul("d_in_proj", 'nt', dproj16, (tm, pw), lambda i, j, k: (i, k),
                   wfull['w_in'], (None, tnd, pw), lambda i, j, k: (k, j, 0),
                   (t_rows // tm, d // tnd, N_CHIPS), [((t_rows, d), F32, (tm, tnd), lambda i, j: (i, j))],
                   lambda acc: acc)
    big['w_in'], = _matmul("dw_in", 'tn', n2, (tk, tnd), lambda i, j, k: (k, i),
                           dproj16, (tk, tnp), lambda i, j, k: (k, j),
                           (d // tnd, N_CHIPS * nppb, 2),
                           [((N_CHIPS, d, pw), BF16, (None, tnd, tnp), lambda i, j: (j // nppb, i, j % nppb))],
                           lambda acc: acc)

    def norm_bwd(xv, dnv, dhv, g):
        _, vjp = jax.vjp(_rms, xv, g)
        dx, dg = vjp(dnv)
        res = dhv + dx
        return res, res.astype(BF16), dg

    dh1, dh1_16, small['mix_norm'] = _rowmap(
        "mix_dnorm", norm_bwd, (1, t_rows // trn), [_rows(h1, trn), _rows(dn2, trn), _rows(dh2, trn), _full(mix_norm)],
        [_o((t_rows, d), F32, trn), _o((t_rows, d), BF16, trn)], [_acc((1, d))])

    dh0, _, small['ffn1_norm'], big['ffn1_w_gate'], big['ffn1_w_up'], big['ffn1_w_down'] = _ffn_bwd(
        "ffn1b", dh1, dh1_16, sv1, ffn1_norm, wfull['ffn1_w_gate'], wfull['ffn1_w_up'], wfull['ffn1_w_down'], dims)
    grad_x = dh0[N_META:N_META + seq][None]
    small['meta_tokens'] = dh0[:N_META]
    small['final_norm'] = d_final.reshape(d)

    halves = []
    for nm in BIG:
        land = _scatter_partials(f"rs_{nm}", big[nm])
        halves.append(_reduce_partials(f"red_{nm}", big[nm], land, pos))
    wholes = _sibling_exchange("exchange", halves)
    gbig = {nm: wh.reshape(wh.shape[0] * wh.shape[1], wh.shape[2]) for nm, wh in zip(BIG, wholes)}

    small_names = [n for n in WEIGHTS if n not in BIG]
    full_shape = {n: small[n].shape for n in small_names}
    flat = jnp.concatenate([small[n].reshape(-1) for n in small_names])
    unit = 8 * 8 * LANES
    total = _round_up(flat.shape[0], unit)
    flat = jnp.concatenate([flat, jnp.zeros((total - flat.shape[0],), F32)])
    red = _all_reduce_small("ar_small", flat.reshape(total // LANES, LANES)).reshape(-1)
    gsmall = {}
    off = 0
    for n in small_names:
        size = math.prod(full_shape[n])
        gsmall[n] = _own_shard(n, red[off:off + size].reshape(full_shape[n]), w[n].shape, chip)
        off += size

    out_g, out_d, out_m, out_v = {}, {}, {}, {}
    for nm in BIG:
        shp = w[nm].shape
        w2 = w[nm].reshape(shp[1], shp[2])
        trw = _tile(shp[1], 128 if shp[2] <= 4096 else 64, 8)
        res = _adamw(f"adamw_{nm}", w2, gbig[nm], mom[nm].reshape(w2.shape), var[nm].reshape(w2.shape), trw)
        out_g[nm], out_d[nm], out_m[nm], out_v[nm] = [a.reshape(shp) for a in res]

    def pack(tree):
        fl = jnp.concatenate([tree[n].reshape(-1) for n in small_names])
        tot = _round_up(fl.shape[0], 8 * LANES)
        return jnp.concatenate([fl, jnp.zeros((tot - fl.shape[0],), F32)]).reshape(tot // LANES, LANES)

    wp, gp, mp, vp = pack(w), pack(gsmall), pack(mom), pack(var)
    res = _adamw("adamw_small", wp, gp, mp, vp, _tile(wp.shape[0], 512, 8))
    off = 0
    for n in small_names:
        size = math.prod(w[n].shape)
        for dst, src in zip((out_g, out_d, out_m, out_v), res):
            dst[n] = src.reshape(-1)[off:off + size].reshape(w[n].shape)
        off += size

    return (loss, grad_x, *[out_g[n] for n in WEIGHTS], *[out_d[n] for n in WEIGHTS],
            *[out_m[n] for n in WEIGHTS], *[out_v[n] for n in WEIGHTS])


def _gather_small_shards(conv_w, b_a, b_x, gather):
    cw = conv_w.shape[1]
    part = b_a.shape[1]

    def rows8(a):
        a = jnp.concatenate([a, jnp.zeros((a.shape[0], cw - a.shape[1]), F32)], axis=1)
        return jnp.concatenate([a, jnp.zeros((8 - a.shape[0], cw), F32)], axis=0) if a.shape[0] < 8 else a

    local = jnp.concatenate([rows8(conv_w), rows8(b_a), rows8(b_x), jnp.zeros((8, cw), F32)], axis=0)
    full = gather("rg_small", local)
    conv_full = full[:, :CONV_WIDTH].transpose(1, 0, 2).reshape(CONV_WIDTH, N_CHIPS * cw)
    bias = lambda k: full[:, 8 * k:8 * k + RG_HEADS, :part].transpose(1, 0, 2).reshape(1, RG_HEADS * N_CHIPS * part)
    return conv_full, bias(1), bias(2)


def _own_shard(name, g, local_shape, chip):
    if tuple(g.shape) == tuple(local_shape):
        return g
    axis = [k for k, (a, b) in enumerate(zip(g.shape, local_shape)) if a != b][0]
    size = local_shape[axis]
    return lax.dynamic_slice_in_dim(g, chip * size, size, axis=axis)
```

```python
import functools
import math

import jax
import jax.numpy as jnp
from jax import lax
from jax.experimental import pallas as pl
from jax.experimental.pallas import tpu as pltpu

F32 = jnp.float32
BF16 = jnp.bfloat16
MESH = pl.DeviceIdType.MESH

N_META = 16
RG_HEADS = 8
CONV_WIDTH = 4
RG_C = 8.0
S5_GROUP = 16
S5_STATE = 64
GROUPS_PER_BLOCK = 8
EPS = 1e-6
N_CHIPS = 4
LANES = 128
VMEM_LIMIT = 56 * 1024 * 1024

ADAM_LR = 0.001
ADAM_B1 = 0.9
ADAM_B2 = 0.999
ADAM_EPS = 1e-08
ADAM_WD = 0.01
ADAM_STEP = 10

WEIGHTS = ['meta_tokens', 'ffn1_norm', 'ffn1_w_gate', 'ffn1_w_up', 'ffn1_w_down', 'mix_norm', 'w_in', 'rg_conv_w',
           'rg_conv_b', 'rg_w_a', 'rg_b_a', 'rg_w_x', 'rg_b_x', 'rg_lambda', 's5_lambda_re', 's5_lambda_im',
           's5_log_dt', 's5_b_re', 's5_b_im', 's5_c_re', 's5_c_im', 's5_d', 's5_glu_w', 's5_glu_b', 'rg_out_norm',
           's5_out_norm', 'w_out', 'ffn2_norm', 'ffn2_w_gate', 'ffn2_w_up', 'ffn2_w_down', 'final_norm']
BIG = ('ffn1_w_gate', 'ffn1_w_up', 'ffn1_w_down', 'w_in', 's5_glu_w', 'w_out', 'ffn2_w_gate', 'ffn2_w_up',
       'ffn2_w_down')

_DN = {'nn': (((1,), (0,)), ((), ())), 'nt': (((1,), (1,)), ((), ())), 'tn': (((0,), (0,)), ((), ()))}


def _round_up(n, m):
    return (n + m - 1) // m * m


def _tile(n, pref, unit=LANES):
    best = None
    for t in range(unit, min(n, pref) + 1, unit):
        if n % t == 0:
            best = t
    return best if best is not None else n


def _params(sem=None):
    return pltpu.CompilerParams(dimension_semantics=sem, vmem_limit_bytes=VMEM_LIMIT)


def _rms(x, g):
    return x * lax.rsqrt(jnp.mean(x * x, axis=-1, keepdims=True) + EPS) * g


def _sigmoid(x):
    return 1.0 / (1.0 + jnp.exp(-x))


def _gelu(x):
    return 0.5 * x * (1.0 + jnp.tanh(math.sqrt(2.0 / math.pi) * (x + 0.044715 * (x * x * x))))


def _silu(x):
    return x * _sigmoid(x)


def _expm1(x):
    series = x * (1.0 + x * (1.0 / 2) * (1.0 + x * (1.0 / 3) * (1.0 + x * (1.0 / 4) * (1.0 + x * (1.0 / 5) * (1.0 + x * (1.0 / 6))))))
    return jnp.where(jnp.abs(x) < 0.3, series, jnp.exp(x) - 1.0)


def _softplus_neg(lam):
    m = jnp.maximum(-lam, 0.0)
    e = jnp.exp(-jnp.abs(lam))
    w = 1.0 + e
    log1p = jnp.where(w == 1.0, e, jnp.log(w) * (e / jnp.where(w == 1.0, 1.0, w - 1.0)))
    return m + log1p


def _rg_gate_math(xc, pre_r, pre_i, sp):
    r = _sigmoid(pre_r)
    i = _sigmoid(pre_i)
    log_a = -RG_C * r * sp
    a = jnp.exp(log_a)
    mult = jnp.sqrt(-_expm1(2.0 * log_a))
    return a, mult * i * xc


def _matmul(name, mode, a, a_blk, a_map, b, b_blk, b_map, grid, outs, epilogue, extras=()):
    ni, nj, nk = grid
    ne, no = len(extras), len(outs)
    sq = lambda blk: tuple(d for d in blk if d is not None)
    ab, bb = sq(a_blk), sq(b_blk)
    acc_shape = {'nn': (ab[0], bb[1]), 'nt': (ab[0], bb[0]), 'tn': (ab[1], bb[1])}[mode]

    def body(*refs):
        a_ref, b_ref = refs[0], refs[1]
        ex = refs[2:2 + ne]
        out = refs[2 + ne:2 + ne + no]
        part = lax.dot_general(a_ref[...], b_ref[...], _DN[mode], preferred_element_type=F32)

        def finish(acc):
            res = epilogue(acc, *[e[...] for e in ex])
            if not isinstance(res, tuple):
                res = (res,)
            for o, r in zip(out, res):
                o[...] = r.astype(o.dtype)

        if nk == 1:
            finish(part)
        else:
            acc_ref = refs[2 + ne + no]
            k = pl.program_id(2)

            @pl.when(k == 0)
            def _():
                acc_ref[...] = part

            @pl.when(k > 0)
            def _():
                acc_ref[...] += part

            @pl.when(k == nk - 1)
            def _():
                finish(acc_ref[...])

    in_specs = [pl.BlockSpec(a_blk, a_map), pl.BlockSpec(b_blk, b_map)]
    in_specs += [pl.BlockSpec(blk, functools.partial(lambda m, i, j, k: m(i, j), m)) for _, blk, m in extras]
    out_specs = [pl.BlockSpec(blk, functools.partial(lambda m, i, j, k: m(i, j), m)) for _, _, blk, m in outs]
    res = pl.pallas_call(
        body, name=name, grid=grid, in_specs=in_specs, out_specs=out_specs,
        out_shape=[jax.ShapeDtypeStruct(s, d) for s, d, _, _ in outs],
        scratch_shapes=[pltpu.VMEM(acc_shape, F32)] if nk > 1 else [],
        compiler_params=_params(("parallel", "parallel", "arbitrary")),
    )(a, b, *[e for e, _, _ in extras])
    return res


def _rows(arr, tr, tc=None, col0=0):
    if tc is None:
        return (arr, (tr, arr.shape[1]), lambda j, i: (i, 0))
    return (arr, (tr, tc), lambda j, i: (i, col0 + j))


def _full(arr):
    nd = arr.ndim
    return (arr, arr.shape, lambda j, i: (0,) * nd)


def _cols(arr, tc):
    return (arr, (arr.shape[0], tc), lambda j, i: (0, j))


def _rowmap(name, fn, grid, ins, outs, accs=(), scratch=()):
    nj, ni = grid
    n_in, n_out, n_acc = len(ins), len(outs), len(accs)

    def body(*refs):
        vals = [r[...] for r in refs[:n_in]]
        o_refs = refs[n_in:n_in + n_out]
        a_refs = refs[n_in + n_out:n_in + n_out + n_acc]
        s_refs = refs[n_in + n_out + n_acc:]
        res = fn(*vals, *s_refs)
        if not isinstance(res, tuple):
            res = (res,)
        for o, r in zip(o_refs, res[:n_out]):
            o[...] = r.astype(o.dtype)
        i = pl.program_id(1)
        for a_ref, r in zip(a_refs, res[n_out:]):
            @pl.when(i == 0)
            def _(a_ref=a_ref, r=r):
                a_ref[...] = r.astype(a_ref.dtype)

            @pl.when(i > 0)
            def _(a_ref=a_ref, r=r):
                a_ref[...] += r.astype(a_ref.dtype)

    res = pl.pallas_call(
        body, name=name, grid=grid,
        in_specs=[pl.BlockSpec(blk, m) for _, blk, m in ins],
        out_specs=[pl.BlockSpec(blk, m) for _, _, blk, m in list(outs) + list(accs)],
        out_shape=[jax.ShapeDtypeStruct(s, d) for s, d, _, _ in list(outs) + list(accs)],
        scratch_shapes=list(scratch),
        compiler_params=_params(("parallel", "arbitrary")),
    )(*[a for a, _, _ in ins])
    return res


def _o(shape, dtype, tr, tc=None):
    if tc is None:
        return (shape, dtype, (tr, shape[1]), lambda j, i: (i, 0))
    return (shape, dtype, (tr, tc), lambda j, i: (i, j))


def _acc(shape, tc=None):
    if tc is None:
        nd = len(shape)
        return (shape, F32, shape, lambda j, i: (0,) * nd)
    return (shape, F32, (shape[0], tc), lambda j, i: (0, j))


def _position():
    x, y, c = lax.axis_index("x"), lax.axis_index("y"), lax.axis_index("c")
    return x, y, c


def _all_gather_chips(name, buf):
    _, rows, cols = buf.shape
    rh = rows // 2

    def body(in_ref, out_ref, send_sems, recv_sems, fsend_sems, frecv_sems):
        del in_ref
        x, y, c = _position()
        s = 2 * x + y
        sibling = (x, y, 1 - c)
        chips = [(1 - x, y), (x, 1 - y), (1 - x, 1 - y)]
        mine = pl.ds(c * rh, rh)
        other = pl.ds((1 - c) * rh, rh)

        def copy(sems, j, chip_index, half, to):
            return pltpu.make_async_remote_copy(
                src_ref=out_ref.at[chip_index, half], dst_ref=out_ref.at[chip_index, half],
                send_sem=sems[0].at[j], recv_sem=sems[1].at[j], device_id=to, device_id_type=MESH)

        ici, d2d = (send_sems, recv_sems), (fsend_sems, frecv_sems)
        sends = [copy(ici, j, s, mine, (px, py, c)) for j, (px, py) in enumerate(chips)]
        for cp in sends:
            cp.start()
        passed = []
        for j, (px, py) in enumerate(chips):
            copy(ici, j, 2 * px + py, mine, (px, py, c)).wait_recv()
            fwd = copy(d2d, j, 2 * px + py, mine, sibling)
            fwd.start()
            passed.append(fwd)
        for j, (px, py) in enumerate(chips):
            copy(d2d, j, 2 * px + py, other, sibling).wait_recv()
        for cp in sends + passed:
            cp.wait_send()

    hbm = pl.BlockSpec(memory_space=pltpu.HBM)
    return pl.pallas_call(
        body, name=name, in_specs=[hbm], out_specs=hbm, input_output_aliases={0: 0},
        out_shape=jax.ShapeDtypeStruct(buf.shape, buf.dtype),
        scratch_shapes=[pltpu.SemaphoreType.DMA((3,)), pltpu.SemaphoreType.DMA((3,)),
                        pltpu.SemaphoreType.DMA((3,)), pltpu.SemaphoreType.DMA((3,))],
    )(buf)


def _scatter_partials(name, g):
    _, rows, cols = g.shape
    rh = rows // 2

    def body(g_ref, land_ref, send_sems, recv_sems):
        x, y, c = _position()
        s = 2 * x + y
        chips = [(1 - x, y), (x, 1 - y), (1 - x, 1 - y)]

        def copy(send_slot, recv_slot, chip_index, half, to):
            return pltpu.make_async_remote_copy(
                src_ref=g_ref.at[chip_index, pl.ds(half * rh, rh)], dst_ref=land_ref.at[recv_slot],
                send_sem=send_sems.at[send_slot], recv_sem=recv_sems.at[recv_slot], device_id=to, device_id_type=MESH)

        sends = [copy(0, 0, s, 1 - c, (x, y, 1 - c))]
        for j, (px, py) in enumerate(chips):
            for cc in range(2):
                sends.append(copy(1 + 2 * j + cc, 1 + 2 * j + c, 2 * px + py, cc, (px, py, cc)))
        for cp in sends:
            cp.start()
        for k in range(7):
            copy(k, k, s, c, (x, y, c)).wait_recv()
        for cp in sends:
            cp.wait_send()

    hbm = pl.BlockSpec(memory_space=pltpu.HBM)
    return pl.pallas_call(
        body, name=name, in_specs=[hbm], out_specs=hbm,
        out_shape=jax.ShapeDtypeStruct((7, rh, cols), g.dtype),
        scratch_shapes=[pltpu.SemaphoreType.DMA((7,)), pltpu.SemaphoreType.DMA((7,))],
    )(g)


def _reduce_partials(name, g, land, pos):
    _, rows, cols = g.shape
    rh = rows // 2
    tr = _tile(rh, 256, 16)
    nt = rh // tr

    def body(pos_ref, g_ref, l_ref, o_ref):
        acc = g_ref[...].astype(F32)
        for k in range(7):
            acc = acc + l_ref[k].astype(F32)
        o_ref[...] = acc

    grid_spec = pltpu.PrefetchScalarGridSpec(
        num_scalar_prefetch=1, grid=(nt,),
        in_specs=[pl.BlockSpec((None, tr, cols), lambda i, p: (p[0], p[1] * nt + i, 0)),
                  pl.BlockSpec((7, tr, cols), lambda i, p: (0, i, 0))],
        out_specs=pl.BlockSpec((None, tr, cols), lambda i, p: (p[1], i, 0)))
    return pl.pallas_call(
        body, name=name, grid_spec=grid_spec, out_shape=jax.ShapeDtypeStruct((2, rh, cols), F32),
        compiler_params=_params(("parallel",)),
    )(pos, g, land)


def _sibling_exchange(name, bufs):
    n = len(bufs)

    def body(*refs):
        outs = refs[n:2 * n]
        send_sems, recv_sems = refs[2 * n:]
        x, y, c = _position()
        sibling = (x, y, 1 - c)

        def copy(k, half):
            return pltpu.make_async_remote_copy(
                src_ref=outs[k].at[half], dst_ref=outs[k].at[half], send_sem=send_sems.at[k], recv_sem=recv_sems.at[k],
                device_id=sibling, device_id_type=MESH)

        sends = [copy(k, c) for k in range(n)]
        for cp in sends:
            cp.start()
        for k in range(n):
            copy(k, 1 - c).wait_recv()
        for cp in sends:
            cp.wait_send()

    hbm = pl.BlockSpec(memory_space=pltpu.HBM)
    return pl.pallas_call(
        body, name=name, in_specs=[hbm] * n, out_specs=[hbm] * n, input_output_aliases={k: k for k in range(n)},
        out_shape=[jax.ShapeDtypeStruct(b.shape, b.dtype) for b in bufs],
        scratch_shapes=[pltpu.SemaphoreType.DMA((n,)), pltpu.SemaphoreType.DMA((n,))],
    )(*bufs)


def _all_reduce_small(name, buf):
    rows = buf.shape[0]
    p = rows // 8

    def body(buf_ref, out_ref, land_ref, red_ref, s1, r1, s2, r2):
        x, y, c = _position()
        me = 4 * x + 2 * y + c

        def peer(r):
            px = 1 - x if (r >> 2) & 1 else x
            py = 1 - y if (r >> 1) & 1 else y
            pc = 1 - c if r & 1 else c
            return (px, py, pc), 4 * px + 2 * py + pc

        firsts = []
        for r in range(1, 8):
            to, d = peer(r)
            cp = pltpu.make_async_remote_copy(
                src_ref=buf_ref.at[pl.ds(d * p, p)], dst_ref=land_ref.at[r - 1],
                send_sem=s1.at[r - 1], recv_sem=r1.at[r - 1], device_id=to, device_id_type=MESH)
            cp.start()
            firsts.append(cp)
        acc = buf_ref[pl.ds(me * p, p), :]
        for r in range(1, 8):
            firsts[r - 1].wait_recv()
            acc = acc + land_ref[r - 1]
        red_ref[...] = acc
        out_ref[pl.ds(me * p, p), :] = acc
        seconds = []
        for r in range(1, 8):
            to, d = peer(r)
            cp = pltpu.make_async_remote_copy(
                src_ref=red_ref, dst_ref=out_ref.at[pl.ds(me * p, p)],
                send_sem=s2.at[r - 1], recv_sem=r2.at[r - 1], device_id=to, device_id_type=MESH)
            cp.start()
            seconds.append(cp)
        for r in range(1, 8):
            to, d = peer(r)
            pltpu.make_async_remote_copy(
                src_ref=red_ref, dst_ref=out_ref.at[pl.ds(d * p, p)],
                send_sem=s2.at[r - 1], recv_sem=r2.at[r - 1], device_id=to, device_id_type=MESH).wait_recv()
        for cp in firsts + seconds:
            cp.wait_send()

    vmem = pl.BlockSpec(memory_space=pltpu.VMEM)
    return pl.pallas_call(
        body, name=name, in_specs=[vmem], out_specs=vmem,
        out_shape=jax.ShapeDtypeStruct(buf.shape, F32),
        scratch_shapes=[pltpu.VMEM((7, p, LANES), F32), pltpu.VMEM((p, LANES), F32),
                        pltpu.SemaphoreType.DMA((7,)), pltpu.SemaphoreType.DMA((7,)),
                        pltpu.SemaphoreType.DMA((7,)), pltpu.SemaphoreType.DMA((7,))],
        compiler_params=pltpu.CompilerParams(vmem_limit_bytes=VMEM_LIMIT),
    )(buf)


def _cast_pad(name, w, rows_to, cols_to, pos):
    rows, cols = w.shape
    tr = _tile(math.gcd(rows, rows_to), 256, 16)
    assert rows % tr == 0 and rows_to % tr == 0, (rows, rows_to, tr)
    n_src = rows // tr

    def body(pos_ref, w_ref, o_ref):
        i = pl.program_id(0)
        if cols_to > cols:
            o_ref[:, cols:] = jnp.zeros((tr, cols_to - cols), BF16)

        @pl.when(i < n_src)
        def _():
            o_ref[:, :cols] = w_ref[...].astype(BF16)

        if rows_to > rows:
            @pl.when(i >= n_src)
            def _():
                o_ref[:, :cols] = jnp.zeros((tr, cols), BF16)

    grid_spec = pltpu.PrefetchScalarGridSpec(
        num_scalar_prefetch=1, grid=(rows_to // tr,),
        in_specs=[pl.BlockSpec((tr, cols), lambda i, p: (jnp.minimum(i, n_src - 1), 0))],
        out_specs=pl.BlockSpec((None, tr, cols_to), lambda i, p: (p[0], i, 0)))
    return pl.pallas_call(
        body, name=name, grid_spec=grid_spec,
        out_shape=jax.ShapeDtypeStruct((N_CHIPS, rows_to, cols_to), BF16),
        compiler_params=_params(("parallel",)),
    )(pos, w)


def _scan_rows(a, b, row, tr, reverse):
    sh = 1
    while sh < tr:
        if reverse:
            valid = row < tr - sh
            a_s = pltpu.roll(a, tr - sh, 0)
            b_s = pltpu.roll(b, tr - sh, 0)
        else:
            valid = row >= sh
            a_s = pltpu.roll(a, sh, 0)
            b_s = pltpu.roll(b, sh, 0)
        b = b + jnp.where(valid, a * b_s, 0.0)
        a = jnp.where(valid, a * a_s, a)
        sh *= 2
    return a, b


def _linear_scan(name, a, b, tr, reverse=False, shift_a=False):
    t_rows, cols = a.shape
    tc = _tile(cols, 512)
    ni, nj = t_rows // tr, cols // tc
    hb = tr // 8

    def rmap(j, i):
        return ((ni - 1 - i) if reverse else i, j)

    def halo_map(j, i):
        ri = ni - 1 - i
        return (jnp.minimum((ri + 1) * hb, t_rows // 8 - 1), j)

    def body(*refs):
        if shift_a:
            a_ref, halo_ref, b_ref, h_ref, carry_ref = refs
        else:
            a_ref, b_ref, h_ref, carry_ref = refs
        i = pl.program_id(1)
        row = lax.broadcasted_iota(jnp.int32, (tr, tc), 0)

        @pl.when(i == 0)
        def _():
            carry_ref[...] = jnp.zeros_like(carry_ref)

        av = a_ref[...]
        if shift_a:
            nxt = jnp.where(i > 0, halo_ref[0:1, :], 0.0)
            av = jnp.where(row == tr - 1, jnp.broadcast_to(nxt, (tr, tc)), pltpu.roll(av, tr - 1, 0))
        pa, hb_ = _scan_rows(av, b_ref[...], row, tr, reverse)
        h = hb_ + pa * carry_ref[0:1, :]
        h_ref[...] = h
        last = h[0:1, :] if reverse else h[tr - 1:tr, :]
        carry_ref[...] = jnp.broadcast_to(last, carry_ref.shape)

    in_specs = [pl.BlockSpec((tr, tc), rmap)]
    args = [a]
    if shift_a:
        in_specs.append(pl.BlockSpec((8, tc), halo_map))
        args.append(a)
    in_specs.append(pl.BlockSpec((tr, tc), rmap))
    args.append(b)
    return pl.pallas_call(
        body, name=name, grid=(nj, ni), in_specs=in_specs, out_specs=pl.BlockSpec((tr, tc), rmap),
        out_shape=jax.ShapeDtypeStruct((t_rows, cols), F32), scratch_shapes=[pltpu.VMEM((8, tc), F32)],
        compiler_params=_params(("parallel", "arbitrary")),
    )(*args)


def _conv_fwd(name, proj, conv_w, conv_b, width, tr):
    t_rows = proj.shape[0]
    ni = t_rows // tr
    hb = tr // 8
    tc = _tile(width, 512)

    def body(u_ref, halo_ref, w_ref, b_ref, o_ref, ext_ref):
        i = pl.program_id(1)
        ext_ref[0:8, :] = jnp.where(i > 0, halo_ref[...], 0.0)
        ext_ref[8:8 + tr, :] = u_ref[...]
        acc = b_ref[...] + w_ref[CONV_WIDTH - 1:CONV_WIDTH, :] * u_ref[...]
        for k in range(CONV_WIDTH - 1):
            acc = acc + w_ref[k:k + 1, :] * ext_ref[pl.ds(8 - (CONV_WIDTH - 1) + k, tr), :]
        o_ref[...] = acc

    return pl.pallas_call(
        body, name=name, grid=(width // tc, ni),
        in_specs=[pl.BlockSpec((tr, tc), lambda j, i: (i, j)),
                  pl.BlockSpec((8, tc), lambda j, i: (jnp.maximum(i * hb - 1, 0), j)),
                  pl.BlockSpec((CONV_WIDTH, tc), lambda j, i: (0, j)),
                  pl.BlockSpec((1, tc), lambda j, i: (0, j))],
        out_specs=pl.BlockSpec((tr, tc), lambda j, i: (i, j)),
        out_shape=jax.ShapeDtypeStruct((t_rows, width), F32),
        scratch_shapes=[pltpu.VMEM((tr + 8, tc), F32)],
        compiler_params=_params(("parallel", "parallel")),
    )(proj, proj, conv_w, conv_b)


def _conv_bwd(name, dxc, proj, conv_w, width, tr):
    t_rows = dxc.shape[0]
    ni = t_rows // tr
    hb = tr // 8
    tc = _tile(width, 512)

    def body(d_ref, dnext_ref, u_ref, uprev_ref, w_ref, du_ref, dw_ref, db_ref, dext_ref, uext_ref):
        i = pl.program_id(1)
        dext_ref[0:tr, :] = d_ref[...]
        dext_ref[tr:tr + 8, :] = jnp.where(i < ni - 1, dnext_ref[...], 0.0)
        uext_ref[0:8, :] = jnp.where(i > 0, uprev_ref[...], 0.0)
        uext_ref[8:8 + tr, :] = u_ref[...]
        d = d_ref[...]
        du = w_ref[CONV_WIDTH - 1:CONV_WIDTH, :] * d
        dws = []
        for k in range(CONV_WIDTH - 1):
            du = du + w_ref[k:k + 1, :] * dext_ref[pl.ds(CONV_WIDTH - 1 - k, tr), :]
            dws.append(jnp.sum(d * uext_ref[pl.ds(8 - (CONV_WIDTH - 1) + k, tr), :], axis=0, keepdims=True))
        dws.append(jnp.sum(d * u_ref[...], axis=0, keepdims=True))
        du_ref[...] = du
        dw = jnp.concatenate(dws, axis=0)
        db = jnp.sum(d, axis=0, keepdims=True)

        @pl.when(i == 0)
        def _():
            dw_ref[...] = dw
            db_ref[...] = db

        @pl.when(i > 0)
        def _():
            dw_ref[...] += dw
            db_ref[...] += db

    return pl.pallas_call(
        body, name=name, grid=(width // tc, ni),
        in_specs=[pl.BlockSpec((tr, tc), lambda j, i: (i, j)),
                  pl.BlockSpec((8, tc), lambda j, i: (jnp.minimum((i + 1) * hb, t_rows // 8 - 1), j)),
                  pl.BlockSpec((tr, tc), lambda j, i: (i, j)),
                  pl.BlockSpec((8, tc), lambda j, i: (jnp.maximum(i * hb - 1, 0), j)),
                  pl.BlockSpec((CONV_WIDTH, tc), lambda j, i: (0, j))],
        out_specs=[pl.BlockSpec((tr, tc), lambda j, i: (i, j)),
                   pl.BlockSpec((CONV_WIDTH, tc), lambda j, i: (0, j)),
                   pl.BlockSpec((1, tc), lambda j, i: (0, j))],
        out_shape=[jax.ShapeDtypeStruct((t_rows, width), F32), jax.ShapeDtypeStruct((CONV_WIDTH, width), F32),
                   jax.ShapeDtypeStruct((1, width), F32)],
        scratch_shapes=[pltpu.VMEM((tr + 8, tc), F32), pltpu.VMEM((tr + 8, tc), F32)],
        compiler_params=_params(("parallel", "arbitrary")),
    )(dxc, dxc, proj, proj, conv_w)


def _s5_disc_math(lam_re, lam_im, log_dt, b_re, b_im, expand):
    dt = jnp.exp(log_dt)
    zr, zi = lam_re * dt, lam_im * dt
    mag = jnp.exp(zr)
    lbr, lbi = mag * jnp.cos(zi), mag * jnp.sin(zi)
    ar, ai = lbr - 1.0, lbi
    den = lam_re * lam_re + lam_im * lam_im
    cr = (ar * lam_re + ai * lam_im) / den
    ci = (ai * lam_re - ar * lam_im) / den
    cre = jnp.dot(cr, expand, precision=lax.Precision.HIGHEST, preferred_element_type=F32)
    cie = jnp.dot(ci, expand, precision=lax.Precision.HIGHEST, preferred_element_type=F32)
    return lbr, lbi, cre * b_re - cie * b_im, cre * b_im + cie * b_re


def _s5_disc(name, lam_re, lam_im, log_dt, b_re, b_im, expand, cots=None):
    ins = [lam_re, lam_im, log_dt, b_re, b_im, expand]
    n_in = len(ins) + (len(cots) if cots else 0)

    def body(*refs):
        vals = [r[...] for r in refs[:6]]
        outs = refs[n_in:]
        if cots is None:
            res = _s5_disc_math(*vals)
        else:
            cv = tuple(r[...] for r in refs[6:n_in])
            _, vjp = jax.vjp(lambda a, b, c, d, e: _s5_disc_math(a, b, c, d, e, vals[5]), *vals[:5])
            res = vjp(cv)
        for o, r in zip(outs, res):
            o[...] = r

    if cots is None:
        shapes = [lam_re.shape, lam_re.shape, b_re.shape, b_re.shape]
    else:
        shapes = [lam_re.shape, lam_re.shape, log_dt.shape, b_re.shape, b_re.shape]
    vmem = pl.BlockSpec(memory_space=pltpu.VMEM)
    return pl.pallas_call(
        body, name=name, in_specs=[vmem] * n_in, out_specs=[vmem] * len(shapes),
        out_shape=[jax.ShapeDtypeStruct(s, F32) for s in shapes],
        compiler_params=pltpu.CompilerParams(vmem_limit_bytes=VMEM_LIMIT),
    )(*ins, *(cots or ()))


def _cmul(ar, ai, br, bi):
    return ar * br - ai * bi, ar * bi + ai * br


def _cscan(br, bi, lr, li, row, tr, reverse):
    pr, pi = lr, li
    sh = 1
    while sh < tr:
        if reverse:
            valid = row < tr - sh
            rs, is_ = pltpu.roll(br, tr - sh, 0), pltpu.roll(bi, tr - sh, 0)
        else:
            valid = row >= sh
            rs, is_ = pltpu.roll(br, sh, 0), pltpu.roll(bi, sh, 0)
        mr, mi = _cmul(pr, pi, rs, is_)
        br = br + jnp.where(valid, mr, 0.0)
        bi = bi + jnp.where(valid, mi, 0.0)
        pr, pi = _cmul(pr, pi, pr, pi)
        sh *= 2
    return br, bi


def _s5_fwd(name, proj, col0, bblk, cblk, lamblk, dvec, width, tr):
    t_rows = proj.shape[0]
    ni = t_rows // tr
    nb = width // LANES
    sw = bblk.shape[-1]
    hw = sw // 2

    def body(u_ref, b_ref, c_ref, lam_ref, d_ref, y_ref, s_ref, carry_ref, pow_ref):
        i = pl.program_id(1)
        row = lax.broadcasted_iota(jnp.int32, (tr, hw), 0)
        lam = lam_ref[...]
        lr, li = lam[:, :hw], lam[:, hw:]

        @pl.when(i == 0)
        def _():
            carry_ref[...] = jnp.zeros_like(carry_ref)
            zr, zi = _cscan(jnp.broadcast_to(lr, (tr, hw)) * jnp.where(row == 0, 1.0, 0.0),
                            jnp.broadcast_to(li, (tr, hw)) * jnp.where(row == 0, 1.0, 0.0), lr, li, row, tr, False)
            pow_ref[:, :hw] = zr
            pow_ref[:, hw:] = zi

        u = u_ref[...]
        bu = jnp.dot(u.astype(BF16), b_ref[...], preferred_element_type=F32)
        sr, si = _cscan(bu[:, :hw], bu[:, hw:], lr, li, row, tr, False)
        cr, ci = carry_ref[0:1, :hw], carry_ref[0:1, hw:]
        kr, ki = _cmul(pow_ref[:, :hw], pow_ref[:, hw:], cr, ci)
        sr, si = sr + kr, si + ki
        s = jnp.concatenate([sr, si], axis=1)
        carry_ref[...] = jnp.broadcast_to(s[tr - 1:tr, :], carry_ref.shape)
        s16 = s.astype(BF16)
        s_ref[...] = s16
        y_ref[...] = jnp.dot(s16, c_ref[...], preferred_element_type=F32) + d_ref[...] * u

    return pl.pallas_call(
        body, name=name, grid=(nb, ni),
        in_specs=[pl.BlockSpec((tr, LANES), lambda k, i: (i, col0 + k)),
                  pl.BlockSpec((None, LANES, sw), lambda k, i: (k, 0, 0)),
                  pl.BlockSpec((None, sw, LANES), lambda k, i: (k, 0, 0)),
                  pl.BlockSpec((None, 1, sw), lambda k, i: (k, 0, 0)),
                  pl.BlockSpec((1, LANES), lambda k, i: (0, k))],
        out_specs=[pl.BlockSpec((tr, LANES), lambda k, i: (i, k)),
                   pl.BlockSpec((tr, sw), lambda k, i: (i, k))],
        out_shape=[jax.ShapeDtypeStruct((t_rows, width), F32), jax.ShapeDtypeStruct((t_rows, nb * sw), BF16)],
        scratch_shapes=[pltpu.VMEM((8, sw), F32), pltpu.VMEM((tr, sw), F32)],
        compiler_params=_params(("parallel", "arbitrary")),
    )(proj, bblk, cblk, lamblk, dvec)


def _s5_bwd(name, dy, proj, col0, states, bblk, cblk, lamblk, dvec, width, tr):
    t_rows = dy.shape[0]
    ni = t_rows // tr
    nb = width // LANES
    sw = bblk.shape[-1]
    hw = sw // 2
    hb16 = tr // 16

    def rmap(k, i):
        return (ni - 1 - i, k)

    def body(dy_ref, u_ref, s_ref, sprev_ref, b_ref, c_ref, lam_ref, d_ref,
             du_ref, db_ref, dc_ref, dlam_ref, dd_ref, carry_ref, pow_ref):
        i = pl.program_id(1)
        row = lax.broadcasted_iota(jnp.int32, (tr, hw), 0)
        row_w = lax.broadcasted_iota(jnp.int32, (tr, sw), 0)
        lam = lam_ref[...]
        lr, li = lam[:, :hw], -lam[:, hw:]

        @pl.when(i == 0)
        def _():
            carry_ref[...] = jnp.zeros_like(carry_ref)
            zr, zi = _cscan(jnp.broadcast_to(lr, (tr, hw)) * jnp.where(row == tr - 1, 1.0, 0.0),
                            jnp.broadcast_to(li, (tr, hw)) * jnp.where(row == tr - 1, 1.0, 0.0), lr, li, row, tr, True)
            pow_ref[:, :hw] = zr
            pow_ref[:, hw:] = zi

        dyv = dy_ref[...]
        dy16 = dyv.astype(BF16)
        u = u_ref[...]
        gd = lax.dot_general(dy16, c_ref[...], _DN['nt'], preferred_element_type=F32)
        gr, gi = _cscan(gd[:, :hw], gd[:, hw:], lr, li, row, tr, True)
        cr, ci = carry_ref[0:1, :hw], carry_ref[0:1, hw:]
        kr, ki = _cmul(pow_ref[:, :hw], pow_ref[:, hw:], cr, ci)
        gr, gi = gr + kr, gi + ki
        g = jnp.concatenate([gr, gi], axis=1)
        carry_ref[...] = jnp.broadcast_to(g[0:1, :], carry_ref.shape)
        g16 = g.astype(BF16)
        du_ref[...] = lax.dot_general(g16, b_ref[...], _DN['nt'], preferred_element_type=F32) + d_ref[...] * dyv
        s16 = s_ref[...]
        dbv = lax.dot_general(g16, u.astype(BF16), _DN['tn'], preferred_element_type=F32)
        dcv = lax.dot_general(s16, dy16, _DN['tn'], preferred_element_type=F32)
        s32 = s16.astype(F32)
        first = jnp.where(i < ni - 1, sprev_ref[15:16, :].astype(F32), 0.0)
        sp = jnp.where(row_w == 0, jnp.broadcast_to(first, (tr, sw)), pltpu.roll(s32, 1, 0))
        spr, spi = sp[:, :hw], sp[:, hw:]
        dlr = jnp.sum(gr * spr + gi * spi, axis=0, keepdims=True)
        dli = jnp.sum(gi * spr - gr * spi, axis=0, keepdims=True)
        dlam = jnp.concatenate([dlr, dli], axis=1)
        ddv = jnp.sum(dyv * u, axis=0, keepdims=True)

        @pl.when(i == 0)
        def _():
            db_ref[...] = dbv
            dc_ref[...] = dcv
            dlam_ref[...] = dlam
            dd_ref[...] = ddv

        @pl.when(i > 0)
        def _():
            db_ref[...] += dbv
            dc_ref[...] += dcv
            dlam_ref[...] += dlam
            dd_ref[...] += ddv

    return pl.pallas_call(
        body, name=name, grid=(nb, ni),
        in_specs=[pl.BlockSpec((tr, LANES), rmap),
                  pl.BlockSpec((tr, LANES), lambda k, i: (ni - 1 - i, col0 + k)),
                  pl.BlockSpec((tr, sw), rmap),
                  pl.BlockSpec((16, sw), lambda k, i: (jnp.maximum((ni - 1 - i) * hb16 - 1, 0), k)),
                  pl.BlockSpec((None, LANES, sw), lambda k, i: (k, 0, 0)),
                  pl.BlockSpec((None, sw, LANES), lambda k, i: (k, 0, 0)),
                  pl.BlockSpec((None, 1, sw), lambda k, i: (k, 0, 0)),
                  pl.BlockSpec((1, LANES), lambda k, i: (0, k))],
        out_specs=[pl.BlockSpec((tr, LANES), rmap),
                   pl.BlockSpec((None, sw, LANES), lambda k, i: (k, 0, 0)),
                   pl.BlockSpec((None, sw, LANES), lambda k, i: (k, 0, 0)),
                   pl.BlockSpec((None, 1, sw), lambda k, i: (k, 0, 0)),
                   pl.BlockSpec((1, LANES), lambda k, i: (0, k))],
        out_shape=[jax.ShapeDtypeStruct((t_rows, width), F32), jax.ShapeDtypeStruct((nb, sw, LANES), F32),
                   jax.ShapeDtypeStruct((nb, sw, LANES), F32), jax.ShapeDtypeStruct((nb, 1, sw), F32),
                   jax.ShapeDtypeStruct((1, width), F32)],
        scratch_shapes=[pltpu.VMEM((8, sw), F32), pltpu.VMEM((tr, sw), F32)],
        compiler_params=_params(("parallel", "arbitrary")),
    )(dy, proj, states, states, bblk, cblk, lamblk, dvec)


def _blockdiag(m):
    nb, g, p, q = m.shape
    eye = jnp.eye(g, dtype=m.dtype)
    return (m[:, :, :, None, :] * eye[None, :, None, :, None]).reshape(nb, g * p, g * q)


def _blockdiag_take(m, p, q):
    nb = m.shape[0]
    g = GROUPS_PER_BLOCK
    m = m.reshape(nb, g, p, g, q)
    return jnp.stack([m[:, k, :, k, :] for k in range(g)], axis=1)


def _adamw(name, w, g, m, v, tr):
    rows, cols = w.shape
    gcols = g.shape[1]
    c1 = 1.0 - ADAM_B1 ** ADAM_STEP
    c2 = 1.0 - ADAM_B2 ** ADAM_STEP

    def body(w_ref, g_ref, m_ref, v_ref, go_ref, d_ref, mo_ref, vo_ref):
        gv = g_ref[...] if gcols == cols else g_ref[:, :cols]
        mn = ADAM_B1 * m_ref[...] + (1.0 - ADAM_B1) * gv
        vn = ADAM_B2 * v_ref[...] + (1.0 - ADAM_B2) * (gv * gv)
        go_ref[...] = gv
        mo_ref[...] = mn
        vo_ref[...] = vn
        d_ref[...] = -ADAM_LR * ((mn / c1) / (jnp.sqrt(vn / c2) + ADAM_EPS) + ADAM_WD * w_ref[...])

    spec = pl.BlockSpec((tr, cols), lambda i: (i, 0))
    return pl.pallas_call(
        body, name=name, grid=(rows // tr,),
        in_specs=[spec, pl.BlockSpec((tr, gcols), lambda i: (i, 0)), spec, spec],
        out_specs=[spec] * 4, out_shape=[jax.ShapeDtypeStruct((rows, cols), F32)] * 4,
        compiler_params=_params(("parallel",)),
    )(w, g, m, v)


def _ffn_fwd(tag, h, norm, wg, wu, wd, dims):
    t_rows, d, fp, tm, tr, trn = dims['T'], dims['D'], dims['FP'], dims['TM'], dims['TR'], dims['TRN']
    ft = N_CHIPS * fp
    tn = _tile(fp, 1408)
    npb = fp // tn
    n16, = _rowmap(f"{tag}_norm", lambda x, g: _rms(x, g).astype(BF16), (1, t_rows // trn),
                   [_rows(h, trn), _full(norm)], [_o((t_rows, d), BF16, trn)])

    def up(nm, w):
        return _matmul(f"{tag}_{nm}", 'nn', n16, (tm, d // 2), lambda i, j, k: (i, k),
                       w, (None, d // 2, tn), lambda i, j, k: (j // npb, k, j % npb),
                       (t_rows // tm, ft // tn, 2), [((t_rows, ft), BF16, (tm, tn), lambda i, j: (i, j))],
                       lambda acc: acc)[0]

    gate, upv = up("gate", wg), up("up", wu)
    tc = _tile(ft, 1408)
    act, = _rowmap(f"{tag}_act", lambda g, u: _silu(g.astype(F32)) * u.astype(F32), (ft // tc, t_rows // tr),
                   [_rows(gate, tr, tc), _rows(upv, tr, tc)], [_o((t_rows, ft), BF16, tr, tc)])
    tnd = _tile(d, 1024)
    hn, = _matmul(f"{tag}_down", 'nn', act, (tm, fp), lambda i, j, k: (i, k),
                  wd, (None, fp, tnd), lambda i, j, k: (k, 0, j),
                  (t_rows // tm, d // tnd, N_CHIPS), [((t_rows, d), F32, (tm, tnd), lambda i, j: (i, j))],
                  lambda acc, hin: hin + 0.5 * acc, extras=[(h, (tm, tnd), lambda i, j: (i, j))])
    return hn, dict(h=h, n16=n16, gate=gate, up=upv, act=act)


def _ffn_bwd(tag, dh, dh16, saved, norm, wg, wu, wd, dims):
    t_rows, d, fp, tm, trn = dims['T'], dims['D'], dims['FP'], dims['TM'], dims['TRN']
    ft = N_CHIPS * fp
    tn = _tile(fp, 1408)
    npb = fp // tn
    tm2 = tm // 2
    tk = t_rows // 2

    def act_bwd(acc, g, u):
        da = 0.5 * acc
        g32, u32 = g.astype(F32), u.astype(F32)
        sg = _sigmoid(g32)
        return da * u32 * (sg * (1.0 + g32 * (1.0 - sg))), da * (g32 * sg)

    dgate, dup = _matmul(
        f"{tag}_dact", 'nt', dh16, (tm2, d // 2), lambda i, j, k: (i, k),
        wd, (None, tn, d // 2), lambda i, j, k: (j // npb, j % npb, k),
        (t_rows // tm2, ft // tn, 2),
        [((t_rows, ft), BF16, (tm2, tn), lambda i, j: (i, j))] * 2, act_bwd,
        extras=[(saved['gate'], (tm2, tn), lambda i, j: (i, j)), (saved['up'], (tm2, tn), lambda i, j: (i, j))])
    tnd = _tile(d, 1024)
    dwd, = _matmul(f"{tag}_dwd", 'tn', saved['act'], (tk, tn), lambda i, j, k: (k, i),
                   dh16, (tk, tnd), lambda i, j, k: (k, j),
                   (ft // tn, d // tnd, 2),
                   [((N_CHIPS, fp, d), BF16, (None, tn, tnd), lambda i, j: (i // npb, i % npb, j))],
                   lambda acc: 0.5 * acc)

    def dn_part(nm, dact, w, prev):
        extras = [] if prev is None else [(prev, (tm, tnd), lambda i, j: (i, j))]
        return _matmul(f"{tag}_{nm}", 'nt', dact, (tm, fp), lambda i, j, k: (i, k),
                       w, (None, tnd, fp), lambda i, j, k: (k, j, 0),
                       (t_rows // tm, d // tnd, N_CHIPS), [((t_rows, d), F32, (tm, tnd), lambda i, j: (i, j))],
                       (lambda acc: acc) if prev is None else (lambda acc, p: acc + p), extras=extras)[0]

    dn = dn_part("dn_up", dup, wu, dn_part("dn_gate", dgate, wg, None))

    def dw_up(nm, dact):
        return _matmul(f"{tag}_{nm}", 'tn', saved['n16'], (tk, tnd), lambda i, j, k: (k, i),
                       dact, (tk, tn), lambda i, j, k: (k, j),
                       (d // tnd, ft // tn, 2),
                       [((N_CHIPS, d, fp), BF16, (None, tnd, tn), lambda i, j: (j // npb, i, j % npb))],
                       lambda acc: acc)[0]

    dwg, dwu = dw_up("dwg", dgate), dw_up("dwu", dup)

    def norm_bwd(x, dnv, dhv, g):
        _, vjp = jax.vjp(_rms, x, g)
        dx, dg = vjp(dnv)
        res = dhv + dx
        return res, res.astype(BF16), dg

    dh_in, dh_in16, dnorm = _rowmap(
        f"{tag}_dnorm", norm_bwd, (1, t_rows // trn),
        [_rows(saved['h'], trn), _rows(dn, trn), _rows(dh, trn), _full(norm)],
        [_o((t_rows, d), F32, trn), _o((t_rows, d), BF16, trn)], [_acc((1, d))])
    return dh_in, dh_in16, dnorm, dwg, dwu, dwd


def kernel(x, meta_tokens, ffn1_norm, ffn1_w_gate, ffn1_w_up, ffn1_w_down, mix_norm, w_in, rg_conv_w, rg_conv_b, rg_w_a, rg_b_a, rg_w_x, rg_b_x, rg_lambda, s5_lambda_re, s5_lambda_im, s5_log_dt, s5_b_re, s5_b_im, s5_c_re, s5_c_im, s5_d, s5_glu_w, s5_glu_b, rg_out_norm, s5_out_norm, w_out, ffn2_norm, ffn2_w_gate, ffn2_w_up, ffn2_w_down, final_norm, loss_target, m_meta_tokens, m_ffn1_norm, m_ffn1_w_gate, m_ffn1_w_up, m_ffn1_w_down, m_mix_norm, m_w_in, m_rg_conv_w, m_rg_conv_b, m_rg_w_a, m_rg_b_a, m_rg_w_x, m_rg_b_x, m_rg_lambda, m_s5_lambda_re, m_s5_lambda_im, m_s5_log_dt, m_s5_b_re, m_s5_b_im, m_s5_c_re, m_s5_c_im, m_s5_d, m_s5_glu_w, m_s5_glu_b, m_rg_out_norm, m_s5_out_norm, m_w_out, m_ffn2_norm, m_ffn2_w_gate, m_ffn2_w_up, m_ffn2_w_down, m_final_norm, v_meta_tokens, v_ffn1_norm, v_ffn1_w_gate, v_ffn1_w_up, v_ffn1_w_down, v_mix_norm, v_w_in, v_rg_conv_w, v_rg_conv_b, v_rg_w_a, v_rg_b_a, v_rg_w_x, v_rg_b_x, v_rg_lambda, v_s5_lambda_re, v_s5_lambda_im, v_s5_log_dt, v_s5_b_re, v_s5_b_im, v_s5_c_re, v_s5_c_im, v_s5_d, v_s5_glu_w, v_s5_glu_b, v_rg_out_norm, v_s5_out_norm, v_w_out, v_ffn2_norm, v_ffn2_w_gate, v_ffn2_w_up, v_ffn2_w_down, v_final_norm):
    args = locals()
    w = {n: args[n] for n in WEIGHTS}
    mom = {n: args["m_" + n] for n in WEIGHTS}
    var = {n: args["v_" + n] for n in WEIGHTS}

    seq, d = x.shape[1], x.shape[2]
    f_shard = ffn1_w_gate.shape[2]
    fp = _round_up(f_shard, LANES)
    r = rg_conv_b.shape[1]
    s5w = s5_d.shape[1]
    hd = r // RG_HEADS
    groups = s5w // S5_GROUP
    nb = groups // GROUPS_PER_BLOCK
    t_rows = _round_up(N_META + seq, LANES)
    tm, tr, tk = t_rows // 4, t_rows // 8, t_rows // 2
    trn = _tile(t_rows, max(t_rows // 32, 16), 16)
    dims = dict(T=t_rows, D=d, FP=fp, TM=tm, TR=tr, TRN=trn)
    cx, cy, cc = lax.axis_index("x"), lax.axis_index("y"), lax.axis_index("c")
    chip = 2 * cx + cy
    pos = jnp.stack([chip, cc]).astype(jnp.int32)

    def gather(nm, buf):
        return _all_gather_chips(f"ag_{nm}", buf)

    def cast_gather(nm, local, rows_to, cols_to):
        return gather(nm, _cast_pad(f"cast_{nm}", local, rows_to, cols_to, pos))

    def gather_f32(nm, local):
        buf = lax.dynamic_update_slice(jnp.zeros((N_CHIPS,) + local.shape, F32), local[None], (chip, 0, 0))
        return gather(nm, buf)

    wfull = {}
    for nm in ('ffn1_w_gate', 'ffn1_w_up', 'ffn2_w_gate', 'ffn2_w_up'):
        wfull[nm] = cast_gather(nm, w[nm][0], d, fp)
    for nm in ('ffn1_w_down', 'ffn2_w_down'):
        wfull[nm] = cast_gather(nm, w[nm][0], fp, d)
    wfull['w_in'] = cast_gather('w_in', w_in[0], d, w_in.shape[2])
    wfull['w_out'] = cast_gather('w_out', w_out[0], w_out.shape[1], d).reshape(d, d)
    wfull['s5_glu_w'] = cast_gather('glu', s5_glu_w[0], s5_glu_w.shape[1], s5w).reshape(s5w, s5w)
    rgw_local = jnp.concatenate([rg_w_a[0], rg_w_x[0]], axis=0).reshape(2 * RG_HEADS * (hd // N_CHIPS), hd)
    rgw = cast_gather('rgw', rgw_local, rgw_local.shape[0], hd)
    rgw = rgw.reshape(N_CHIPS, 2, RG_HEADS, hd // N_CHIPS, hd).transpose(1, 2, 0, 3, 4).reshape(2, RG_HEADS, hd, hd)
    w_a16, w_x16 = rgw[0], rgw[1]
    meta_full = gather_f32('meta', meta_tokens).transpose(1, 0, 2).reshape(N_META, d)

    pad_rows = t_rows - N_META - seq
    h0 = jnp.concatenate([meta_full, x[0], jnp.zeros((pad_rows, d), F32)], axis=0)
    tgt = jnp.concatenate([jnp.zeros((N_META, d), F32), loss_target[0], jnp.zeros((pad_rows, d), F32)], axis=0)

    h1, sv1 = _ffn_fwd("ffn1", h0, ffn1_norm, wfull['ffn1_w_gate'], wfull['ffn1_w_up'], wfull['ffn1_w_down'], dims)

    n2, = _rowmap("mix_norm", lambda xv, g: _rms(xv, g).astype(BF16), (1, t_rows // trn),
                  [_rows(h1, trn), _full(mix_norm)], [_o((t_rows, d), BF16, trn)])
    pw = w_in.shape[2]
    tnp = _tile(pw, 1536)
    nppb = pw // tnp
    proj, = _matmul("in_proj", 'nn', n2, (tm, d // 2), lambda i, j, k: (i, k),
                    wfull['w_in'], (None, d // 2, tnp), lambda i, j, k: (j // nppb, k, j % nppb),
                    (t_rows // tm, N_CHIPS * nppb, 2), [((t_rows, 3 * r), F32, (tm, tnp), lambda i, j: (i, j))],
                    lambda acc: acc)

    conv_w2, b_a2, b_x2 = _gather_small_shards(rg_conv_w[0], rg_b_a[0], rg_b_x[0], gather_f32)
    xc = _conv_fwd("rg_conv", proj, conv_w2, rg_conv_b, r, tr)

    def head_cols(arr):
        return (arr, (tr, hd), lambda j, i: (i, j))

    def head_w(arr):
        return (arr, (None, hd, hd), lambda j, i: (j, 0, 0))

    def head_vec(arr):
        return (arr, (1, hd), lambda j, i: (0, j))

    def gates_fwd(xcv, wa, wx, ba, bx, lam):
        x16 = xcv.astype(BF16)
        pre_r = jnp.dot(x16, wa, preferred_element_type=F32) + ba
        pre_i = jnp.dot(x16, wx, preferred_element_type=F32) + bx
        return _rg_gate_math(xcv, pre_r, pre_i, _softplus_neg(lam))

    gate_params = [head_w(w_a16), head_w(w_x16), head_vec(b_a2), head_vec(b_x2), head_vec(rg_lambda)]
    head_out = ((t_rows, r), F32, (tr, hd), lambda j, i: (i, j))
    a_dec, bxv = _rowmap("rg_gates", gates_fwd, (RG_HEADS, t_rows // tr), [head_cols(xc)] + gate_params,
                         [head_out, head_out])
    h_rg = _linear_scan("rg_scan", a_dec, bxv, tr)

    expand = jnp.repeat(jnp.eye(S5_STATE, dtype=F32), S5_GROUP, axis=1)
    b_re2 = s5_b_re[0].reshape(groups, S5_STATE * S5_GROUP)
    b_im2 = s5_b_im[0].reshape(groups, S5_STATE * S5_GROUP)
    log_dt2 = s5_log_dt[0].reshape(groups, 1)
    lbr, lbi, bbr, bbi = _s5_disc("s5_disc", s5_lambda_re[0], s5_lambda_im[0], log_dt2, b_re2, b_im2, expand)

    def to_bblk(m):
        return _blockdiag(m.reshape(nb, GROUPS_PER_BLOCK, S5_STATE, S5_GROUP).transpose(0, 1, 3, 2))

    def to_cblk(m):
        return _blockdiag(m.reshape(nb, GROUPS_PER_BLOCK, S5_GROUP, S5_STATE).transpose(0, 1, 3, 2))

    bblk = jnp.concatenate([to_bblk(bbr), to_bblk(bbi)], axis=-1).astype(BF16)
    cblk = jnp.concatenate([to_cblk(s5_c_re[0]), -to_cblk(s5_c_im[0])], axis=-2).astype(BF16)
    hw = GROUPS_PER_BLOCK * S5_STATE
    lamblk = jnp.concatenate([lbr.reshape(nb, 1, hw), lbi.reshape(nb, 1, hw)], axis=-1)
    col0 = 2 * r // LANES
    y_pre, states = _s5_fwd("s5_fwd", proj, col0, bblk, cblk, lamblk, s5_d, s5w, tr)

    z16, = _rowmap("s5_gelu", lambda yv: _gelu(yv).astype(BF16), (1, t_rows // trn),
                   [_rows(y_pre, trn)], [_o((t_rows, s5w), BF16, trn)])
    tns = _tile(s5w, 1024)

    def glu_fwd(acc, yv, b):
        gl = acc + b
        return _gelu(yv) * _sigmoid(gl), gl

    y_s5, gl = _matmul("s5_glu", 'nn', z16, (tm, s5w), lambda i, j, k: (i, 0),
                       wfull['s5_glu_w'], (s5w, tns), lambda i, j, k: (0, j),
                       (t_rows // tm, s5w // tns, 1), [((t_rows, s5w), F32, (tm, tns), lambda i, j: (i, j))] * 2,
                       glu_fwd, extras=[(y_pre, (tm, tns), lambda i, j: (i, j)), (s5_glu_b, (1, tns), lambda i, j: (0, j))])

    def mix_out(hv, gv, yv, g1, g2):
        return jnp.concatenate([_rms(hv * _gelu(gv), g1), _rms(yv, g2)], axis=1)

    gate_in = (proj, (trn, r), lambda j, i: (i, 1))
    ycat16, = _rowmap("mix_out", lambda *a: mix_out(*a).astype(BF16), (1, t_rows // trn),
                      [_rows(h_rg, trn), gate_in, _rows(y_s5, trn), _full(rg_out_norm), _full(s5_out_norm)],
                      [_o((t_rows, r + s5w), BF16, trn)])
    tnd = _tile(d, 1024)
    h2, = _matmul("out_proj", 'nn', ycat16, (tm, (r + s5w) // 2), lambda i, j, k: (i, k),
                  wfull['w_out'], ((r + s5w) // 2, tnd), lambda i, j, k: (k, j),
                  (t_rows // tm, d // tnd, 2), [((t_rows, d), F32, (tm, tnd), lambda i, j: (i, j))],
                  lambda acc, hin: hin + acc, extras=[(h1, (tm, tnd), lambda i, j: (i, j))])

    h3, sv2 = _ffn_fwd("ffn2", h2, ffn2_norm, wfull['ffn2_w_gate'], wfull['ffn2_w_up'], wfull['ffn2_w_down'], dims)

    fnorm2 = final_norm.reshape(1, d)

    def head(xv, tv, g):
        i = pl.program_id(1)
        rowi = lax.broadcasted_iota(jnp.int32, (trn, 1), 0) + i * trn
        mask = jnp.where((rowi >= N_META) & (rowi < N_META + seq), 1.0, 0.0)
        out, vjp = jax.vjp(_rms, xv, g)
        err = (out - tv) * mask
        dx, dg = vjp(err * (1.0 / d))
        return dx, dx.astype(BF16), jnp.sum(err * err, axis=0, keepdims=True), dg

    dh3, dh3_16, loss_cols, d_final = _rowmap(
        "loss_head", head, (1, t_rows // trn), [_rows(h3, trn), _rows(tgt, trn), _full(fnorm2)],
        [_o((t_rows, d), F32, trn), _o((t_rows, d), BF16, trn)], [_acc((1, d)), _acc((1, d))])
    loss = lax.psum(0.5 * jnp.sum(loss_cols) / d, ("x", "y", "c"))

    small = {}
    big = {}
    dh2, dh2_16, small['ffn2_norm'], big['ffn2_w_gate'], big['ffn2_w_up'], big['ffn2_w_down'] = _ffn_bwd(
        "ffn2b", dh3, dh3_16, sv2, ffn2_norm, wfull['ffn2_w_gate'], wfull['ffn2_w_up'], wfull['ffn2_w_down'], dims)

    dycat, = _matmul("d_out_proj", 'nt', dh2_16, (tm, d // 2), lambda i, j, k: (i, k),
                     wfull['w_out'], (tnd, d // 2), lambda i, j, k: (j, k),
                     (t_rows // tm, (r + s5w) // tnd, 2), [((t_rows, r + s5w), F32, (tm, tnd), lambda i, j: (i, j))],
                     lambda acc: acc)
    wo_rows = (r + s5w) // N_CHIPS
    tno = _tile(wo_rows, 1024)
    npo = wo_rows // tno
    big['w_out'], = _matmul("dw_out", 'tn', ycat16, (tk, tno), lambda i, j, k: (k, i),
                            dh2_16, (tk, tnd), lambda i, j, k: (k, j),
                            ((r + s5w) // tno, d // tnd, 2),
                            [((N_CHIPS, wo_rows, d), BF16, (None, tno, tnd), lambda i, j: (i // npo, i % npo, j))],
                            lambda acc: acc)

    def mix_out_bwd(hv, gv, yv, dyc, g1, g2):
        _, vjp = jax.vjp(mix_out, hv, gv, yv, g1, g2)
        return vjp(dyc)

    dh_out, dgate_rg, dy_s5, small['rg_out_norm'], small['s5_out_norm'] = _rowmap(
        "mix_out_bwd", mix_out_bwd, (1, t_rows // trn),
        [_rows(h_rg, trn), gate_in, _rows(y_s5, trn), _rows(dycat, trn), _full(rg_out_norm), _full(s5_out_norm)],
        [_o((t_rows, r), F32, trn), _o((t_rows, r), F32, trn), _o((t_rows, s5w), F32, trn)],
        [_acc((1, r)), _acc((1, s5w))])

    def glu_bwd(dyv, yv, glv):
        zv = _gelu(yv)
        sg = _sigmoid(glv)
        dgl = dyv * zv * sg * (1.0 - sg)
        return dyv * sg, dgl.astype(BF16), jnp.sum(dgl, axis=0, keepdims=True)

    dz_dir, dgl16, small['s5_glu_b'] = _rowmap(
        "s5_glu_bwd", glu_bwd, (1, t_rows // trn), [_rows(dy_s5, trn), _rows(y_pre, trn), _rows(gl, trn)],
        [_o((t_rows, s5w), F32, trn), _o((t_rows, s5w), BF16, trn)], [_acc((1, s5w))])

    def dgelu(acc, dzd, yv):
        _, vjp = jax.vjp(_gelu, yv)
        return vjp(acc + dzd)[0]

    dy_pre, = _matmul("s5_dz", 'nt', dgl16, (tm, s5w), lambda i, j, k: (i, 0),
                      wfull['s5_glu_w'], (tns, s5w), lambda i, j, k: (j, 0),
                      (t_rows // tm, s5w // tns, 1), [((t_rows, s5w), F32, (tm, tns), lambda i, j: (i, j))],
                      dgelu, extras=[(dz_dir, (tm, tns), lambda i, j: (i, j)), (y_pre, (tm, tns), lambda i, j: (i, j))])
    gl_rows = s5w // N_CHIPS
    tng = _tile(gl_rows, 1024)
    npg = gl_rows // tng
    big['s5_glu_w'], = _matmul("dw_glu", 'tn', z16, (tk, tng), lambda i, j, k: (k, i),
                               dgl16, (tk, tns), lambda i, j, k: (k, j),
                               (s5w // tng, s5w // tns, 2),
                               [((N_CHIPS, gl_rows, s5w), BF16, (None, tng, tns), lambda i, j: (i // npg, i % npg, j))],
                               lambda acc: acc)
    du_s5, dbblk, dcblk, dlamblk, small['s5_d'] = _s5_bwd(
        "s5_bwd", dy_pre, proj, col0, states, bblk, cblk, lamblk, s5_d, s5w, tr)

    def from_bblk(m):
        return _blockdiag_take(m, S5_STATE, S5_GROUP).reshape(groups, S5_STATE * S5_GROUP)

    dbbr, dbbi = from_bblk(dbblk[:, :hw]), from_bblk(dbblk[:, hw:])
    dcs = lambda m: _blockdiag_take(m, S5_STATE, S5_GROUP).transpose(0, 1, 3, 2).reshape(1, groups, S5_GROUP, S5_STATE)
    small['s5_c_re'], small['s5_c_im'] = dcs(dcblk[:, :hw]), -dcs(dcblk[:, hw:])
    dlbr, dlbi = dlamblk[:, 0, :hw].reshape(groups, S5_STATE), dlamblk[:, 0, hw:].reshape(groups, S5_STATE)
    g_lre, g_lim, g_ldt, g_bre, g_bim = _s5_disc(
        "s5_disc_bwd", s5_lambda_re[0], s5_lambda_im[0], log_dt2, b_re2, b_im2, expand, cots=(dlbr, dlbi, dbbr, dbbi))
    small['s5_lambda_re'], small['s5_lambda_im'], small['s5_log_dt'] = g_lre[None], g_lim[None], g_ldt.reshape(1, groups)
    small['s5_b_re'] = g_bre.reshape(s5_b_re.shape)
    small['s5_b_im'] = g_bim.reshape(s5_b_im.shape)

    dh_rg = _linear_scan("rg_scan_bwd", a_dec, dh_out, tr, reverse=True, shift_a=True)

    hb = tr // 8

    def gates_bwd(xcv, dhv, hv, hprev, wa, wx, ba, bx, lam):
        i = pl.program_id(1)
        row = lax.broadcasted_iota(jnp.int32, (tr, hd), 0)
        first = jnp.where(i > 0, hprev[7:8, :], 0.0)
        h_prev = jnp.where(row == 0, jnp.broadcast_to(first, (tr, hd)), pltpu.roll(hv, 1, 0))
        x16 = xcv.astype(BF16)
        pre_r = jnp.dot(x16, wa, preferred_element_type=F32) + ba
        pre_i = jnp.dot(x16, wx, preferred_element_type=F32) + bx
        _, vjp = jax.vjp(_rg_gate_math, xcv, pre_r, pre_i, _softplus_neg(lam))
        dxc_, dpr, dpi, dsp = vjp((dhv * h_prev, dhv))
        dpr16, dpi16 = dpr.astype(BF16), dpi.astype(BF16)
        dxc_ = (dxc_ + lax.dot_general(dpr16, wa, _DN['nt'], preferred_element_type=F32)
                + lax.dot_general(dpi16, wx, _DN['nt'], preferred_element_type=F32))
        dwa = lax.dot_general(x16, dpr16, _DN['tn'], preferred_element_type=F32)
        dwx = lax.dot_general(x16, dpi16, _DN['tn'], preferred_element_type=F32)
        dlam = dsp * (-_sigmoid(-lam))
        return (dxc_, dwa, dwx, jnp.sum(dpr, axis=0, keepdims=True), jnp.sum(dpi, axis=0, keepdims=True), dlam)

    head_acc_w = ((RG_HEADS, hd, hd), F32, (None, hd, hd), lambda j, i: (j, 0, 0))
    head_acc_v = ((1, r), F32, (1, hd), lambda j, i: (0, j))
    h_halo = (h_rg, (8, hd), lambda j, i: (jnp.maximum(i * hb - 1, 0), j))
    dxc, d_wa, d_wx, d_ba, d_bx, d_lam = _rowmap(
        "rg_gates_bwd", gates_bwd, (RG_HEADS, t_rows // tr),
        [head_cols(xc), head_cols(dh_rg), head_cols(h_rg), h_halo] + gate_params,
        [head_out], [head_acc_w, head_acc_w, head_acc_v, head_acc_v, head_acc_v])
    small['rg_w_a'], small['rg_w_x'] = d_wa[None], d_wx[None]
    small['rg_b_a'], small['rg_b_x'] = d_ba.reshape(1, RG_HEADS, hd), d_bx.reshape(1, RG_HEADS, hd)
    small['rg_lambda'] = d_lam
    du_rg, d_convw, small['rg_conv_b'] = _conv_bwd("rg_conv_bwd", dxc, proj, conv_w2, r, tr)
    small['rg_conv_w'] = d_convw[None]

    dproj16, = _rowmap("dproj", lambda a, b, c: jnp.concatenate([a, b, c], axis=1).astype(BF16), (1, t_rows // trn),
                       [_rows(du_rg, trn), _rows(dgate_rg, trn), _rows(du_s5, trn)], [_o((t_rows, 3 * r), BF16, trn)])
    dn2, = _matmul("d_in_proj", 'nt', dproj16, (tm, pw), lambda i, j, k: (i, k),
                   wfull['w_in'], (None, tnd, pw), lambda i, j, k: (k, j, 0),
                   (t_rows // tm, d // tnd, N_CHIPS), [((t_rows, d), F32, (tm, tnd), lambda i, j: (i, j))],
                   lambda acc: acc)
    big['w_in'], = _matmul("dw_in", 'tn', n2, (tk, tnd), lambda i, j, k: (k, i),
                           dproj16, (tk, tnp), lambda i, j, k: (k, j),
                           (d // tnd, N_CHIPS * nppb, 2),
                           [((N_CHIPS, d, pw), BF16, (None, tnd, tnp), lambda i, j: (j // nppb, i, j % nppb))],
                           lambda acc: acc)

    def norm_bwd(xv, dnv, dhv, g):
        _, vjp = jax.vjp(_rms, xv, g)
        dx, dg = vjp(dnv)
        res = dhv + dx
        return res, res.astype(BF16), dg

    dh1, dh1_16, small['mix_norm'] = _rowmap(
        "mix_dnorm", norm_bwd, (1, t_rows // trn), [_rows(h1, trn), _rows(dn2, trn), _rows(dh2, trn), _full(mix_norm)],
        [_o((t_rows, d), F32, trn), _o((t_rows, d), BF16, trn)], [_acc((1, d))])

    dh0, _, small['ffn1_norm'], big['ffn1_w_gate'], big['ffn1_w_up'], big['ffn1_w_down'] = _ffn_bwd(
        "ffn1b", dh1, dh1_16, sv1, ffn1_norm, wfull['ffn1_w_gate'], wfull['ffn1_w_up'], wfull['ffn1_w_down'], dims)
    grad_x = dh0[N_META:N_META + seq][None]
    small['meta_tokens'] = dh0[:N_META]
    small['final_norm'] = d_final.reshape(d)

    halves = []
    for nm in BIG:
        land = _scatter_partials(f"rs_{nm}", big[nm])
        halves.append(_reduce_partials(f"red_{nm}", big[nm], land, pos))
    wholes = _sibling_exchange("exchange", halves)
    gbig = {nm: wh.reshape(wh.shape[0] * wh.shape[1], wh.shape[2]) for nm, wh in zip(BIG, wholes)}

    small_names = [n for n in WEIGHTS if n not in BIG]
    full_shape = {n: small[n].shape for n in small_names}
    flat = jnp.concatenate([small[n].reshape(-1) for n in small_names])
    unit = 8 * 8 * LANES
    total = _round_up(flat.shape[0], unit)
    flat = jnp.concatenate([flat, jnp.zeros((total - flat.shape[0],), F32)])
    red = _all_reduce_small("ar_small", flat.reshape(total // LANES, LANES)).reshape(-1)
    gsmall = {}
    off = 0
    for n in small_names:
        size = math.prod(full_shape[n])
        gsmall[n] = _own_shard(n, red[off:off + size].reshape(full_shape[n]), w[n].shape, chip)
        off += size

    out_g, out_d, out_m, out_v = {}, {}, {}, {}
    for nm in BIG:
        shp = w[nm].shape
        w2 = w[nm].reshape(shp[1], shp[2])
        trw = _tile(shp[1], 128 if shp[2] <= 4096 else 64, 8)
        res = _adamw(f"adamw_{nm}", w2, gbig[nm], mom[nm].reshape(w2.shape), var[nm].reshape(w2.shape), trw)
        out_g[nm], out_d[nm], out_m[nm], out_v[nm] = [a.reshape(shp) for a in res]

    def pack(tree):
        fl = jnp.concatenate([tree[n].reshape(-1) for n in small_names])
        tot = _round_up(fl.shape[0], 8 * LANES)
        return jnp.concatenate([fl, jnp.zeros((tot - fl.shape[0],), F32)]).reshape(tot // LANES, LANES)

    wp, gp, mp, vp = pack(w), pack(gsmall), pack(mom), pack(var)
    res = _adamw("adamw_small", wp, gp, mp, vp, _tile(wp.shape[0], 512, 8))
    off = 0
    for n in small_names:
        size = math.prod(w[n].shape)
        for dst, src in zip((out_g, out_d, out_m, out_v), res):
            dst[n] = src.reshape(-1)[off:off + size].reshape(w[n].shape)
        off += size

    return (loss, grad_x, *[out_g[n] for n in WEIGHTS], *[out_d[n] for n in WEIGHTS],
            *[out_m[n] for n in WEIGHTS], *[out_v[n] for n in WEIGHTS])


def _gather_small_shards(conv_w, b_a, b_x, gather):
    cw = conv_w.shape[1]
    part = b_a.shape[1]

    def rows8(a):
        a = jnp.concatenate([a, jnp.zeros((a.shape[0], cw - a.shape[1]), F32)], axis=1)
        return jnp.concatenate([a, jnp.zeros((8 - a.shape[0], cw), F32)], axis=0) if a.shape[0] < 8 else a

    local = jnp.concatenate([rows8(conv_w), rows8(b_a), rows8(b_x), jnp.zeros((8, cw), F32)], axis=0)
    full = gather("rg_small", local)
    conv_full = full[:, :CONV_WIDTH].transpose(1, 0, 2).reshape(CONV_WIDTH, N_CHIPS * cw)
    bias = lambda k: full[:, 8 * k:8 * k + RG_HEADS, :part].transpose(1, 0, 2).reshape(1, RG_HEADS * N_CHIPS * part)
    return conv_full, bias(1), bias(2)


def _own_shard(name, g, local_shape, chip):
    if tuple(g.shape) == tuple(local_shape):
        return g
    axis = [k for k, (a, b) in enumerate(zip(g.shape, local_shape)) if a != b][0]
    size = local_shape[axis]
    return lax.dynamic_slice_in_dim(g, chip * size, size, axis=axis)
```

```python
import functools
import math

import jax
import jax.numpy as jnp
from jax import lax
from jax.experimental import pallas as pl
from jax.experimental.pallas import tpu as pltpu

F32 = jnp.float32
BF16 = jnp.bfloat16
MESH = pl.DeviceIdType.MESH

N_META = 16
RG_HEADS = 8
CONV_WIDTH = 4
RG_C = 8.0
S5_GROUP = 16
S5_STATE = 64
GROUPS_PER_BLOCK = 8
EPS = 1e-6
N_CHIPS = 4
LANES = 128
VMEM_LIMIT = 56 * 1024 * 1024

ADAM_LR = 0.001
ADAM_B1 = 0.9
ADAM_B2 = 0.999
ADAM_EPS = 1e-08
ADAM_WD = 0.01
ADAM_STEP = 10

WEIGHTS = ['meta_tokens', 'ffn1_norm', 'ffn1_w_gate', 'ffn1_w_up', 'ffn1_w_down', 'mix_norm', 'w_in', 'rg_conv_w',
           'rg_conv_b', 'rg_w_a', 'rg_b_a', 'rg_w_x', 'rg_b_x', 'rg_lambda', 's5_lambda_re', 's5_lambda_im',
           's5_log_dt', 's5_b_re', 's5_b_im', 's5_c_re', 's5_c_im', 's5_d', 's5_glu_w', 's5_glu_b', 'rg_out_norm',
           's5_out_norm', 'w_out', 'ffn2_norm', 'ffn2_w_gate', 'ffn2_w_up', 'ffn2_w_down', 'final_norm']
BIG = ('ffn1_w_gate', 'ffn1_w_up', 'ffn1_w_down', 'w_in', 's5_glu_w', 'w_out', 'ffn2_w_gate', 'ffn2_w_up',
       'ffn2_w_down')

_DN = {'nn': (((1,), (0,)), ((), ())), 'nt': (((1,), (1,)), ((), ())), 'tn': (((0,), (0,)), ((), ()))}


def _round_up(n, m):
    return (n + m - 1) // m * m


def _tile(n, pref, unit=LANES):
    best = None
    for t in range(unit, min(n, pref) + 1, unit):
        if n % t == 0:
            best = t
    return best if best is not None else n


def _params(sem=None):
    return pltpu.CompilerParams(dimension_semantics=sem, vmem_limit_bytes=VMEM_LIMIT)


def _rms(x, g):
    return x * lax.rsqrt(jnp.mean(x * x, axis=-1, keepdims=True) + EPS) * g


def _sigmoid(x):
    return 1.0 / (1.0 + jnp.exp(-x))


def _gelu(x):
    return 0.5 * x * (1.0 + jnp.tanh(math.sqrt(2.0 / math.pi) * (x + 0.044715 * (x * x * x))))


def _silu(x):
    return x * _sigmoid(x)


def _expm1(x):
    series = x * (1.0 + x * (1.0 / 2) * (1.0 + x * (1.0 / 3) * (1.0 + x * (1.0 / 4) * (1.0 + x * (1.0 / 5) * (1.0 + x * (1.0 / 6))))))
    return jnp.where(jnp.abs(x) < 0.3, series, jnp.exp(x) - 1.0)


def _softplus_neg(lam):
    m = jnp.maximum(-lam, 0.0)
    e = jnp.exp(-jnp.abs(lam))
    w = 1.0 + e
    log1p = jnp.where(w == 1.0, e, jnp.log(w) * (e / jnp.where(w == 1.0, 1.0, w - 1.0)))
    return m + log1p


def _rg_gate_math(xc, pre_r, pre_i, sp):
    r = _sigmoid(pre_r)
    i = _sigmoid(pre_i)
    log_a = -RG_C * r * sp
    a = jnp.exp(log_a)
    mult = jnp.sqrt(-_expm1(2.0 * log_a))
    return a, mult * i * xc


class _Comm:
    def __init__(self, arrays, out_shapes, aliased, sems, start, finish):
        self.arrays, self.out_shapes, self.aliased, self.sems = arrays, out_shapes, aliased, sems
        self.start, self.finish = start, finish


def _matmul(name, mode, a, a_blk, a_map, b, b_blk, b_map, grid, outs, epilogue, extras=(), comms=()):
    ni, nj, nk = grid
    ne, no = len(extras), len(outs)
    sq = lambda blk: tuple(d for d in blk if d is not None)
    ab, bb = sq(a_blk), sq(b_blk)
    acc_shape = {'nn': (ab[0], bb[1]), 'nt': (ab[0], bb[0]), 'tn': (ab[1], bb[1])}[mode]
    n_cin = sum(len(cm.arrays) for cm in comms)
    n_cout = sum(len(cm.out_shapes) for cm in comms)
    n_acc = 1 if nk > 1 else 0

    def comm_refs(refs):
        cin = refs[2 + ne:2 + ne + n_cin]
        cout = refs[2 + ne + n_cin + no:2 + ne + n_cin + no + n_cout]
        csem = refs[2 + ne + n_cin + no + n_cout + n_acc:]
        for cm in comms:
            yield cm, cin[:len(cm.arrays)], cout[:len(cm.out_shapes)], csem[:len(cm.sems)]
            cin, cout, csem = cin[len(cm.arrays):], cout[len(cm.out_shapes):], csem[len(cm.sems):]

    def body(*refs):
        a_ref, b_ref = refs[0], refs[1]
        ex = refs[2:2 + ne]
        out = refs[2 + ne + n_cin:2 + ne + n_cin + no]
        if comms:
            @pl.when((pl.program_id(0) == 0) & (pl.program_id(1) == 0) & (pl.program_id(2) == 0))
            def _():
                for cm, cin, cout, csem in comm_refs(refs):
                    cm.start(cin, cout, csem)

        part = lax.dot_general(a_ref[...], b_ref[...], _DN[mode], preferred_element_type=F32)

        def finish(acc):
            res = epilogue(acc, *[e[...] for e in ex])
            if not isinstance(res, tuple):
                res = (res,)
            for o, r in zip(out, res):
                o[...] = r.astype(o.dtype)

        if nk == 1:
            finish(part)
        else:
            acc_ref = refs[2 + ne + n_cin + no + n_cout]
            k = pl.program_id(2)

            @pl.when(k == 0)
            def _():
                acc_ref[...] = part

            @pl.when(k > 0)
            def _():
                acc_ref[...] += part

            @pl.when(k == nk - 1)
            def _():
                finish(acc_ref[...])

        if comms:
            @pl.when((pl.program_id(0) == ni - 1) & (pl.program_id(1) == nj - 1) & (pl.program_id(2) == nk - 1))
            def _():
                for cm, cin, cout, csem in comm_refs(refs):
                    cm.finish(cin, cout, csem)

    hbm = pl.BlockSpec(memory_space=pltpu.HBM)
    in_specs = [pl.BlockSpec(a_blk, a_map), pl.BlockSpec(b_blk, b_map)]
    in_specs += [pl.BlockSpec(blk, functools.partial(lambda m, i, j, k: m(i, j), m)) for _, blk, m in extras]
    in_specs += [hbm] * n_cin
    out_specs = [pl.BlockSpec(blk, functools.partial(lambda m, i, j, k: m(i, j), m)) for _, _, blk, m in outs]
    out_specs += [hbm] * n_cout
    aliases, cin_at, cout_at = {}, 2 + ne, no
    for cm in comms:
        if cm.aliased:
            aliases.update({cin_at + k: cout_at + k for k in range(len(cm.arrays))})
        cin_at, cout_at = cin_at + len(cm.arrays), cout_at + len(cm.out_shapes)
    res = pl.pallas_call(
        body, name=name, grid=grid, in_specs=in_specs, out_specs=out_specs,
        out_shape=[jax.ShapeDtypeStruct(s, d) for s, d, _, _ in outs] + [s for cm in comms for s in cm.out_shapes],
        scratch_shapes=([pltpu.VMEM(acc_shape, F32)] if nk > 1 else [])
        + [pltpu.SemaphoreType.DMA((n,)) for cm in comms for n in cm.sems],
        input_output_aliases=aliases,
        compiler_params=_params(("arbitrary",) * 3 if comms else ("parallel", "parallel", "arbitrary")),
    )(a, b, *[e for e, _, _ in extras], *[arr for cm in comms for arr in cm.arrays])
    return res


def _rows(arr, tr, tc=None, col0=0):
    if tc is None:
        return (arr, (tr, arr.shape[1]), lambda j, i: (i, 0))
    return (arr, (tr, tc), lambda j, i: (i, col0 + j))


def _full(arr):
    nd = arr.ndim
    return (arr, arr.shape, lambda j, i: (0,) * nd)


def _cols(arr, tc):
    return (arr, (arr.shape[0], tc), lambda j, i: (0, j))


def _rowmap(name, fn, grid, ins, outs, accs=(), scratch=()):
    nj, ni = grid
    n_in, n_out, n_acc = len(ins), len(outs), len(accs)

    def body(*refs):
        vals = [r[...] for r in refs[:n_in]]
        o_refs = refs[n_in:n_in + n_out]
        a_refs = refs[n_in + n_out:n_in + n_out + n_acc]
        s_refs = refs[n_in + n_out + n_acc:]
        res = fn(*vals, *s_refs)
        if not isinstance(res, tuple):
            res = (res,)
        for o, r in zip(o_refs, res[:n_out]):
            o[...] = r.astype(o.dtype)
        i = pl.program_id(1)
        for a_ref, r in zip(a_refs, res[n_out:]):
            @pl.when(i == 0)
            def _(a_ref=a_ref, r=r):
                a_ref[...] = r.astype(a_ref.dtype)

            @pl.when(i > 0)
            def _(a_ref=a_ref, r=r):
                a_ref[...] += r.astype(a_ref.dtype)

    res = pl.pallas_call(
        body, name=name, grid=grid,
        in_specs=[pl.BlockSpec(blk, m) for _, blk, m in ins],
        out_specs=[pl.BlockSpec(blk, m) for _, _, blk, m in list(outs) + list(accs)],
        out_shape=[jax.ShapeDtypeStruct(s, d) for s, d, _, _ in list(outs) + list(accs)],
        scratch_shapes=list(scratch),
        compiler_params=_params(("parallel", "arbitrary")),
    )(*[a for a, _, _ in ins])
    return res


def _o(shape, dtype, tr, tc=None):
    if tc is None:
        return (shape, dtype, (tr, shape[1]), lambda j, i: (i, 0))
    return (shape, dtype, (tr, tc), lambda j, i: (i, j))


def _acc(shape, tc=None):
    if tc is None:
        nd = len(shape)
        return (shape, F32, shape, lambda j, i: (0,) * nd)
    return (shape, F32, (shape[0], tc), lambda j, i: (0, j))


def _position():
    x, y, c = lax.axis_index("x"), lax.axis_index("y"), lax.axis_index("c")
    return x, y, c


def _gather_comm(buf):
    _, rows, cols = buf.shape
    rh = rows // 2

    def plan(out_ref, sems):
        x, y, c = _position()
        chips = [(1 - x, y), (x, 1 - y), (1 - x, 1 - y)]

        def copy(pair, j, chip_index, half, to):
            return pltpu.make_async_remote_copy(
                src_ref=out_ref.at[chip_index, half], dst_ref=out_ref.at[chip_index, half],
                send_sem=sems[pair].at[j], recv_sem=sems[pair + 1].at[j], device_id=to, device_id_type=MESH)

        mine, other = pl.ds(c * rh, rh), pl.ds((1 - c) * rh, rh)
        ici_out = [copy(0, j, 2 * x + y, mine, (px, py, c)) for j, (px, py) in enumerate(chips)]
        ici_in = [copy(0, j, 2 * px + py, mine, (px, py, c)) for j, (px, py) in enumerate(chips)]
        d2d_out = [copy(2, j, 2 * px + py, mine, (x, y, 1 - c)) for j, (px, py) in enumerate(chips)]
        d2d_in = [copy(2, j, 2 * px + py, other, (x, y, 1 - c)) for j, (px, py) in enumerate(chips)]
        return ici_out, ici_in, d2d_out, d2d_in

    def start(ins, outs, sems):
        for cp in plan(outs[0], sems)[0]:
            cp.start()

    def finish(ins, outs, sems):
        ici_out, ici_in, d2d_out, d2d_in = plan(outs[0], sems)
        for arrived, fwd in zip(ici_in, d2d_out):
            arrived.wait_recv()
            fwd.start()
        for cp in d2d_in:
            cp.wait_recv()
        for cp in ici_out + d2d_out:
            cp.wait_send()

    return _Comm([buf], [jax.ShapeDtypeStruct(buf.shape, buf.dtype)], True, [3, 3, 3, 3], start, finish)


def _scatter_comm(p):
    _, rh, cols = p.shape

    def plan(p_ref, land_ref, sems):
        x, y, c = _position()
        chips = [(1 - x, y), (x, 1 - y), (1 - x, 1 - y)]

        def copy(j, chip_index, to):
            return pltpu.make_async_remote_copy(
                src_ref=p_ref.at[chip_index], dst_ref=land_ref.at[j],
                send_sem=sems[0].at[j], recv_sem=sems[1].at[j], device_id=to, device_id_type=MESH)

        out = [copy(j, 2 * px + py, (px, py, c)) for j, (px, py) in enumerate(chips)]
        arrive = [copy(j, 2 * x + y, (x, y, c)) for j in range(3)]
        return out, arrive

    def start(ins, outs, sems):
        for cp in plan(ins[0], outs[0], sems)[0]:
            cp.start()

    def finish(ins, outs, sems):
        out, arrive = plan(ins[0], outs[0], sems)
        for cp in arrive:
            cp.wait_recv()
        for cp in out:
            cp.wait_send()

    return _Comm([p], [jax.ShapeDtypeStruct((3, rh, cols), p.dtype)], False, [3, 3], start, finish)


def _run_comm(name, cm):
    n_in, n_out = len(cm.arrays), len(cm.out_shapes)

    def body(*refs):
        ins, outs, sems = refs[:n_in], refs[n_in:n_in + n_out], refs[n_in + n_out:]
        cm.start(ins, outs, sems)
        cm.finish(ins, outs, sems)

    hbm = pl.BlockSpec(memory_space=pltpu.HBM)
    return pl.pallas_call(
        body, name=name, in_specs=[hbm] * n_in, out_specs=[hbm] * n_out, out_shape=list(cm.out_shapes),
        input_output_aliases={k: k for k in range(n_in)} if cm.aliased else {},
        scratch_shapes=[pltpu.SemaphoreType.DMA((n,)) for n in cm.sems],
    )(*cm.arrays)


def _pair_swap(name, g):
    n, rows, cols = g.shape
    rh = rows // 2

    def body(g_ref, land_ref, send_sems, recv_sems):
        x, y, c = _position()

        def copy(k, half):
            return pltpu.make_async_remote_copy(
                src_ref=g_ref.at[k, pl.ds(half * rh, rh)], dst_ref=land_ref.at[k],
                send_sem=send_sems.at[k], recv_sem=recv_sems.at[k], device_id=(x, y, 1 - c), device_id_type=MESH)

        sends = [copy(k, 1 - c) for k in range(n)]
        for cp in sends:
            cp.start()
        for k in range(n):
            copy(k, c).wait_recv()
        for cp in sends:
            cp.wait_send()

    hbm = pl.BlockSpec(memory_space=pltpu.HBM)
    return pl.pallas_call(
        body, name=name, in_specs=[hbm], out_specs=hbm,
        out_shape=jax.ShapeDtypeStruct((n, rh, cols), g.dtype),
        scratch_shapes=[pltpu.SemaphoreType.DMA((n,)), pltpu.SemaphoreType.DMA((n,))],
    )(g)


def _pair_sum(name, g, land, pos):
    n, rows, cols = g.shape
    rh = rows // 2
    tr = _tile(rh, 256, 16)
    nt = rh // tr

    def body(pos_ref, g_ref, l_ref, o_ref):
        o_ref[...] = (g_ref[...].astype(F32) + l_ref[...].astype(F32)).astype(o_ref.dtype)

    grid_spec = pltpu.PrefetchScalarGridSpec(
        num_scalar_prefetch=1, grid=(n, nt),
        in_specs=[pl.BlockSpec((None, tr, cols), lambda k, i, p: (k, p[1] * nt + i, 0)),
                  pl.BlockSpec((None, tr, cols), lambda k, i, p: (k, i, 0))],
        out_specs=pl.BlockSpec((None, tr, cols), lambda k, i, p: (k, i, 0)))
    return pl.pallas_call(
        body, name=name, grid_spec=grid_spec, out_shape=jax.ShapeDtypeStruct((n, rh, cols), g.dtype),
        compiler_params=_params(("parallel", "parallel")),
    )(pos, g, land)


def _reduce_partials(name, p, land, pos):
    _, rh, cols = p.shape
    tr = _tile(rh, 256, 16)
    nt = rh // tr

    def body(pos_ref, p_ref, l_ref, o_ref):
        acc = p_ref[...].astype(F32)
        for k in range(3):
            acc = acc + l_ref[k].astype(F32)
        o_ref[...] = acc

    grid_spec = pltpu.PrefetchScalarGridSpec(
        num_scalar_prefetch=1, grid=(nt,),
        in_specs=[pl.BlockSpec((None, tr, cols), lambda i, p_: (p_[0], i, 0)),
                  pl.BlockSpec((3, tr, cols), lambda i, p_: (0, i, 0))],
        out_specs=pl.BlockSpec((None, tr, cols), lambda i, p_: (p_[1], i, 0)))
    return pl.pallas_call(
        body, name=name, grid_spec=grid_spec, out_shape=jax.ShapeDtypeStruct((2, rh, cols), F32),
        compiler_params=_params(("parallel",)),
    )(pos, p, land)


def _sibling_exchange(name, bufs):
    n = len(bufs)

    def body(*refs):
        outs = refs[n:2 * n]
        send_sems, recv_sems = refs[2 * n:]
        x, y, c = _position()
        sibling = (x, y, 1 - c)

        def copy(k, half):
            return pltpu.make_async_remote_copy(
                src_ref=outs[k].at[half], dst_ref=outs[k].at[half], send_sem=send_sems.at[k], recv_sem=recv_sems.at[k],
                device_id=sibling, device_id_type=MESH)

        sends = [copy(k, c) for k in range(n)]
        for cp in sends:
            cp.start()
        for k in range(n):
            copy(k, 1 - c).wait_recv()
        for cp in sends:
            cp.wait_send()

    hbm = pl.BlockSpec(memory_space=pltpu.HBM)
    return pl.pallas_call(
        body, name=name, in_specs=[hbm] * n, out_specs=[hbm] * n, input_output_aliases={k: k for k in range(n)},
        out_shape=[jax.ShapeDtypeStruct(b.shape, b.dtype) for b in bufs],
        scratch_shapes=[pltpu.SemaphoreType.DMA((n,)), pltpu.SemaphoreType.DMA((n,))],
    )(*bufs)


def _all_reduce_small(name, buf):
    rows = buf.shape[0]
    p = rows // 8

    def body(buf_ref, out_ref, land_ref, red_ref, s1, r1, s2, r2):
        x, y, c = _position()
        me = 4 * x + 2 * y + c

        def peer(r):
            px = 1 - x if (r >> 2) & 1 else x
            py = 1 - y if (r >> 1) & 1 else y
            pc = 1 - c if r & 1 else c
            return (px, py, pc), 4 * px + 2 * py + pc

        firsts = []
        for r in range(1, 8):
            to, d = peer(r)
            cp = pltpu.make_async_remote_copy(
                src_ref=buf_ref.at[pl.ds(d * p, p)], dst_ref=land_ref.at[r - 1],
                send_sem=s1.at[r - 1], recv_sem=r1.at[r - 1], device_id=to, device_id_type=MESH)
            cp.start()
            firsts.append(cp)
        acc = buf_ref[pl.ds(me * p, p), :]
        for r in range(1, 8):
            firsts[r - 1].wait_recv()
            acc = acc + land_ref[r - 1]
        red_ref[...] = acc
        out_ref[pl.ds(me * p, p), :] = acc
        seconds = []
        for r in range(1, 8):
            to, d = peer(r)
            cp = pltpu.make_async_remote_copy(
                src_ref=red_ref, dst_ref=out_ref.at[pl.ds(me * p, p)],
                send_sem=s2.at[r - 1], recv_sem=r2.at[r - 1], device_id=to, device_id_type=MESH)
            cp.start()
            seconds.append(cp)
        for r in range(1, 8):
            to, d = peer(r)
            pltpu.make_async_remote_copy(
                src_ref=red_ref, dst_ref=out_ref.at[pl.ds(d * p, p)],
                send_sem=s2.at[r - 1], recv_sem=r2.at[r - 1], device_id=to, device_id_type=MESH).wait_recv()
        for cp in firsts + seconds:
            cp.wait_send()

    vmem = pl.BlockSpec(memory_space=pltpu.VMEM)
    return pl.pallas_call(
        body, name=name, in_specs=[vmem], out_specs=vmem,
        out_shape=jax.ShapeDtypeStruct(buf.shape, F32),
        scratch_shapes=[pltpu.VMEM((7, p, LANES), F32), pltpu.VMEM((p, LANES), F32),
                        pltpu.SemaphoreType.DMA((7,)), pltpu.SemaphoreType.DMA((7,)),
                        pltpu.SemaphoreType.DMA((7,)), pltpu.SemaphoreType.DMA((7,))],
        compiler_params=pltpu.CompilerParams(vmem_limit_bytes=VMEM_LIMIT),
    )(buf)


def _cast_pad(name, w, rows_to, cols_to, pos):
    rows, cols = w.shape
    tr = _tile(math.gcd(rows, rows_to), 256, 16)
    assert rows % tr == 0 and rows_to % tr == 0, (rows, rows_to, tr)
    n_src = rows // tr

    def body(pos_ref, w_ref, o_ref):
        i = pl.program_id(0)
        if cols_to > cols:
            o_ref[:, cols:] = jnp.zeros((tr, cols_to - cols), BF16)

        @pl.when(i < n_src)
        def _():
            o_ref[:, :cols] = w_ref[...].astype(BF16)

        if rows_to > rows:
            @pl.when(i >= n_src)
            def _():
                o_ref[:, :cols] = jnp.zeros((tr, cols), BF16)

    grid_spec = pltpu.PrefetchScalarGridSpec(
        num_scalar_prefetch=1, grid=(rows_to // tr,),
        in_specs=[pl.BlockSpec((tr, cols), lambda i, p: (jnp.minimum(i, n_src - 1), 0))],
        out_specs=pl.BlockSpec((None, tr, cols_to), lambda i, p: (p[0], i, 0)))
    return pl.pallas_call(
        body, name=name, grid_spec=grid_spec,
        out_shape=jax.ShapeDtypeStruct((N_CHIPS, rows_to, cols_to), BF16),
        compiler_params=_params(("parallel",)),
    )(pos, w)


def _scan_rows(a, b, row, tr, reverse):
    sh = 1
    while sh < tr:
        if reverse:
            valid = row < tr - sh
            a_s = pltpu.roll(a, tr - sh, 0)
            b_s = pltpu.roll(b, tr - sh, 0)
        else:
            valid = row >= sh
            a_s = pltpu.roll(a, sh, 0)
            b_s = pltpu.roll(b, sh, 0)
        b = b + jnp.where(valid, a * b_s, 0.0)
        a = jnp.where(valid, a * a_s, a)
        sh *= 2
    return a, b


def _linear_scan(name, a, b, tr, reverse=False, shift_a=False):
    t_rows, cols = a.shape
    tc = _tile(cols, 512)
    ni, nj = t_rows // tr, cols // tc
    hb = tr // 8

    def rmap(j, i):
        return ((ni - 1 - i) if reverse else i, j)

    def halo_map(j, i):
        ri = ni - 1 - i
        return (jnp.minimum((ri + 1) * hb, t_rows // 8 - 1), j)

    def body(*refs):
        if shift_a:
            a_ref, halo_ref, b_ref, h_ref, carry_ref = refs
        else:
            a_ref, b_ref, h_ref, carry_ref = refs
        i = pl.program_id(1)
        row = lax.broadcasted_iota(jnp.int32, (tr, tc), 0)

        @pl.when(i == 0)
        def _():
            carry_ref[...] = jnp.zeros_like(carry_ref)

        av = a_ref[...]
        if shift_a:
            nxt = jnp.where(i > 0, halo_ref[0:1, :], 0.0)
            av = jnp.where(row == tr - 1, jnp.broadcast_to(nxt, (tr, tc)), pltpu.roll(av, tr - 1, 0))
        pa, hb_ = _scan_rows(av, b_ref[...], row, tr, reverse)
        h = hb_ + pa * carry_ref[0:1, :]
        h_ref[...] = h
        last = h[0:1, :] if reverse else h[tr - 1:tr, :]
        carry_ref[...] = jnp.broadcast_to(last, carry_ref.shape)

    in_specs = [pl.BlockSpec((tr, tc), rmap)]
    args = [a]
    if shift_a:
        in_specs.append(pl.BlockSpec((8, tc), halo_map))
        args.append(a)
    in_specs.append(pl.BlockSpec((tr, tc), rmap))
    args.append(b)
    return pl.pallas_call(
        body, name=name, grid=(nj, ni), in_specs=in_specs, out_specs=pl.BlockSpec((tr, tc), rmap),
        out_shape=jax.ShapeDtypeStruct((t_rows, cols), F32), scratch_shapes=[pltpu.VMEM((8, tc), F32)],
        compiler_params=_params(("parallel", "arbitrary")),
    )(*args)


def _conv_fwd(name, proj, conv_w, conv_b, width, tr):
    t_rows = proj.shape[0]
    ni = t_rows // tr
    hb = tr // 8
    tc = _tile(width, 512)

    def body(u_ref, halo_ref, w_ref, b_ref, o_ref, ext_ref):
        i = pl.program_id(1)
        ext_ref[0:8, :] = jnp.where(i > 0, halo_ref[...], 0.0)
        ext_ref[8:8 + tr, :] = u_ref[...]
        acc = b_ref[...] + w_ref[CONV_WIDTH - 1:CONV_WIDTH, :] * u_ref[...]
        for k in range(CONV_WIDTH - 1):
            acc = acc + w_ref[k:k + 1, :] * ext_ref[pl.ds(8 - (CONV_WIDTH - 1) + k, tr), :]
        o_ref[...] = acc

    return pl.pallas_call(
        body, name=name, grid=(width // tc, ni),
        in_specs=[pl.BlockSpec((tr, tc), lambda j, i: (i, j)),
                  pl.BlockSpec((8, tc), lambda j, i: (jnp.maximum(i * hb - 1, 0), j)),
                  pl.BlockSpec((CONV_WIDTH, tc), lambda j, i: (0, j)),
                  pl.BlockSpec((1, tc), lambda j, i: (0, j))],
        out_specs=pl.BlockSpec((tr, tc), lambda j, i: (i, j)),
        out_shape=jax.ShapeDtypeStruct((t_rows, width), F32),
        scratch_shapes=[pltpu.VMEM((tr + 8, tc), F32)],
        compiler_params=_params(("parallel", "parallel")),
    )(proj, proj, conv_w, conv_b)


def _conv_bwd(name, dxc, proj, conv_w, width, tr):
    t_rows = dxc.shape[0]
    ni = t_rows // tr
    hb = tr // 8
    tc = _tile(width, 512)

    def body(d_ref, dnext_ref, u_ref, uprev_ref, w_ref, du_ref, dw_ref, db_ref, dext_ref, uext_ref):
        i = pl.program_id(1)
        dext_ref[0:tr, :] = d_ref[...]
        dext_ref[tr:tr + 8, :] = jnp.where(i < ni - 1, dnext_ref[...], 0.0)
        uext_ref[0:8, :] = jnp.where(i > 0, uprev_ref[...], 0.0)
        uext_ref[8:8 + tr, :] = u_ref[...]
        d = d_ref[...]
        du = w_ref[CONV_WIDTH - 1:CONV_WIDTH, :] * d
        dws = []
        for k in range(CONV_WIDTH - 1):
            du = du + w_ref[k:k + 1, :] * dext_ref[pl.ds(CONV_WIDTH - 1 - k, tr), :]
            dws.append(jnp.sum(d * uext_ref[pl.ds(8 - (CONV_WIDTH - 1) + k, tr), :], axis=0, keepdims=True))
        dws.append(jnp.sum(d * u_ref[...], axis=0, keepdims=True))
        du_ref[...] = du
        dw = jnp.concatenate(dws, axis=0)
        db = jnp.sum(d, axis=0, keepdims=True)

        @pl.when(i == 0)
        def _():
            dw_ref[...] = dw
            db_ref[...] = db

        @pl.when(i > 0)
        def _():
            dw_ref[...] += dw
            db_ref[...] += db

    return pl.pallas_call(
        body, name=name, grid=(width // tc, ni),
        in_specs=[pl.BlockSpec((tr, tc), lambda j, i: (i, j)),
                  pl.BlockSpec((8, tc), lambda j, i: (jnp.minimum((i + 1) * hb, t_rows // 8 - 1), j)),
                  pl.BlockSpec((tr, tc), lambda j, i: (i, j)),
                  pl.BlockSpec((8, tc), lambda j, i: (jnp.maximum(i * hb - 1, 0), j)),
                  pl.BlockSpec((CONV_WIDTH, tc), lambda j, i: (0, j))],
        out_specs=[pl.BlockSpec((tr, tc), lambda j, i: (i, j)),
                   pl.BlockSpec((CONV_WIDTH, tc), lambda j, i: (0, j)),
                   pl.BlockSpec((1, tc), lambda j, i: (0, j))],
        out_shape=[jax.ShapeDtypeStruct((t_rows, width), F32), jax.ShapeDtypeStruct((CONV_WIDTH, width), F32),
                   jax.ShapeDtypeStruct((1, width), F32)],
        scratch_shapes=[pltpu.VMEM((tr + 8, tc), F32), pltpu.VMEM((tr + 8, tc), F32)],
        compiler_params=_params(("parallel", "arbitrary")),
    )(dxc, dxc, proj, proj, conv_w)


def _s5_disc_math(lam_re, lam_im, log_dt, b_re, b_im, expand):
    dt = jnp.exp(log_dt)
    zr, zi = lam_re * dt, lam_im * dt
    mag = jnp.exp(zr)
    lbr, lbi = mag * jnp.cos(zi), mag * jnp.sin(zi)
    ar, ai = lbr - 1.0, lbi
    den = lam_re * lam_re + lam_im * lam_im
    cr = (ar * lam_re + ai * lam_im) / den
    ci = (ai * lam_re - ar * lam_im) / den
    cre = jnp.dot(cr, expand, precision=lax.Precision.HIGHEST, preferred_element_type=F32)
    cie = jnp.dot(ci, expand, precision=lax.Precision.HIGHEST, preferred_element_type=F32)
    return lbr, lbi, cre * b_re - cie * b_im, cre * b_im + cie * b_re


def _s5_disc(name, lam_re, lam_im, log_dt, b_re, b_im, expand, cots=None):
    ins = [lam_re, lam_im, log_dt, b_re, b_im, expand]
    n_in = len(ins) + (len(cots) if cots else 0)

    def body(*refs):
        vals = [r[...] for r in refs[:6]]
        outs = refs[n_in:]
        if cots is None:
            res = _s5_disc_math(*vals)
        else:
            cv = tuple(r[...] for r in refs[6:n_in])
            _, vjp = jax.vjp(lambda a, b, c, d, e: _s5_disc_math(a, b, c, d, e, vals[5]), *vals[:5])
            res = vjp(cv)
        for o, r in zip(outs, res):
            o[...] = r

    if cots is None:
        shapes = [lam_re.shape, lam_re.shape, b_re.shape, b_re.shape]
    else:
        shapes = [lam_re.shape, lam_re.shape, log_dt.shape, b_re.shape, b_re.shape]
    vmem = pl.BlockSpec(memory_space=pltpu.VMEM)
    return pl.pallas_call(
        body, name=name, in_specs=[vmem] * n_in, out_specs=[vmem] * len(shapes),
        out_shape=[jax.ShapeDtypeStruct(s, F32) for s in shapes],
        compiler_params=pltpu.CompilerParams(vmem_limit_bytes=VMEM_LIMIT),
    )(*ins, *(cots or ()))


def _cmul(ar, ai, br, bi):
    return ar * br - ai * bi, ar * bi + ai * br


def _cscan(br, bi, lr, li, row, tr, reverse):
    pr, pi = lr, li
    sh = 1
    while sh < tr:
        if reverse:
            valid = row < tr - sh
            rs, is_ = pltpu.roll(br, tr - sh, 0), pltpu.roll(bi, tr - sh, 0)
        else:
            valid = row >= sh
            rs, is_ = pltpu.roll(br, sh, 0), pltpu.roll(bi, sh, 0)
        mr, mi = _cmul(pr, pi, rs, is_)
        br = br + jnp.where(valid, mr, 0.0)
        bi = bi + jnp.where(valid, mi, 0.0)
        pr, pi = _cmul(pr, pi, pr, pi)
        sh *= 2
    return br, bi


def _s5_fwd(name, proj, col0, bblk, cblk, lamblk, dvec, width, tr):
    t_rows = proj.shape[0]
    ni = t_rows // tr
    nb = width // LANES
    sw = bblk.shape[-1]
    hw = sw // 2

    def body(u_ref, b_ref, c_ref, lam_ref, d_ref, y_ref, s_ref, carry_ref, pow_ref):
        i = pl.program_id(1)
        row = lax.broadcasted_iota(jnp.int32, (tr, hw), 0)
        lam = lam_ref[...]
        lr, li = lam[:, :hw], lam[:, hw:]

        @pl.when(i == 0)
        def _():
            carry_ref[...] = jnp.zeros_like(carry_ref)
            zr, zi = _cscan(jnp.broadcast_to(lr, (tr, hw)) * jnp.where(row == 0, 1.0, 0.0),
                            jnp.broadcast_to(li, (tr, hw)) * jnp.where(row == 0, 1.0, 0.0), lr, li, row, tr, False)
            pow_ref[:, :hw] = zr
            pow_ref[:, hw:] = zi

        u = u_ref[...]
        bu = jnp.dot(u.astype(BF16), b_ref[...], preferred_element_type=F32)
        sr, si = _cscan(bu[:, :hw], bu[:, hw:], lr, li, row, tr, False)
        cr, ci = carry_ref[0:1, :hw], carry_ref[0:1, hw:]
        kr, ki = _cmul(pow_ref[:, :hw], pow_ref[:, hw:], cr, ci)
        sr, si = sr + kr, si + ki
        s = jnp.concatenate([sr, si], axis=1)
        carry_ref[...] = jnp.broadcast_to(s[tr - 1:tr, :], carry_ref.shape)
        s16 = s.astype(BF16)
        s_ref[...] = s16
        y_ref[...] = jnp.dot(s16, c_ref[...], preferred_element_type=F32) + d_ref[...] * u

    return pl.pallas_call(
        body, name=name, grid=(nb, ni),
        in_specs=[pl.BlockSpec((tr, LANES), lambda k, i: (i, col0 + k)),
                  pl.BlockSpec((None, LANES, sw), lambda k, i: (k, 0, 0)),
                  pl.BlockSpec((None, sw, LANES), lambda k, i: (k, 0, 0)),
                  pl.BlockSpec((None, 1, sw), lambda k, i: (k, 0, 0)),
                  pl.BlockSpec((1, LANES), lambda k, i: (0, k))],
        out_specs=[pl.BlockSpec((tr, LANES), lambda k, i: (i, k)),
                   pl.BlockSpec((tr, sw), lambda k, i: (i, k))],
        out_shape=[jax.ShapeDtypeStruct((t_rows, width), F32), jax.ShapeDtypeStruct((t_rows, nb * sw), BF16)],
        scratch_shapes=[pltpu.VMEM((8, sw), F32), pltpu.VMEM((tr, sw), F32)],
        compiler_params=_params(("parallel", "arbitrary")),
    )(proj, bblk, cblk, lamblk, dvec)


def _s5_bwd(name, dy, proj, col0, states, bblk, cblk, lamblk, dvec, width, tr):
    t_rows = dy.shape[0]
    ni = t_rows // tr
    nb = width // LANES
    sw = bblk.shape[-1]
    hw = sw // 2
    hb16 = tr // 16

    def rmap(k, i):
        return (ni - 1 - i, k)

    def body(dy_ref, u_ref, s_ref, sprev_ref, b_ref, c_ref, lam_ref, d_ref,
             du_ref, db_ref, dc_ref, dlam_ref, dd_ref, carry_ref, pow_ref):
        i = pl.program_id(1)
        row = lax.broadcasted_iota(jnp.int32, (tr, hw), 0)
        row_w = lax.broadcasted_iota(jnp.int32, (tr, sw), 0)
        lam = lam_ref[...]
        lr, li = lam[:, :hw], -lam[:, hw:]

        @pl.when(i == 0)
        def _():
            carry_ref[...] = jnp.zeros_like(carry_ref)
            zr, zi = _cscan(jnp.broadcast_to(lr, (tr, hw)) * jnp.where(row == tr - 1, 1.0, 0.0),
                            jnp.broadcast_to(li, (tr, hw)) * jnp.where(row == tr - 1, 1.0, 0.0), lr, li, row, tr, True)
            pow_ref[:, :hw] = zr
            pow_ref[:, hw:] = zi

        dyv = dy_ref[...]
        dy16 = dyv.astype(BF16)
        u = u_ref[...]
        gd = lax.dot_general(dy16, c_ref[...], _DN['nt'], preferred_element_type=F32)
        gr, gi = _cscan(gd[:, :hw], gd[:, hw:], lr, li, row, tr, True)
        cr, ci = carry_ref[0:1, :hw], carry_ref[0:1, hw:]
        kr, ki = _cmul(pow_ref[:, :hw], pow_ref[:, hw:], cr, ci)
        gr, gi = gr + kr, gi + ki
        g = jnp.concatenate([gr, gi], axis=1)
        carry_ref[...] = jnp.broadcast_to(g[0:1, :], carry_ref.shape)
        g16 = g.astype(BF16)
        du_ref[...] = lax.dot_general(g16, b_ref[...], _DN['nt'], preferred_element_type=F32) + d_ref[...] * dyv
        s16 = s_ref[...]
        dbv = lax.dot_general(g16, u.astype(BF16), _DN['tn'], preferred_element_type=F32)
        dcv = lax.dot_general(s16, dy16, _DN['tn'], preferred_element_type=F32)
        s32 = s16.astype(F32)
        first = jnp.where(i < ni - 1, sprev_ref[15:16, :].astype(F32), 0.0)
        sp = jnp.where(row_w == 0, jnp.broadcast_to(first, (tr, sw)), pltpu.roll(s32, 1, 0))
        spr, spi = sp[:, :hw], sp[:, hw:]
        dlr = jnp.sum(gr * spr + gi * spi, axis=0, keepdims=True)
        dli = jnp.sum(gi * spr - gr * spi, axis=0, keepdims=True)
        dlam = jnp.concatenate([dlr, dli], axis=1)
        ddv = jnp.sum(dyv * u, axis=0, keepdims=True)

        @pl.when(i == 0)
        def _():
            db_ref[...] = dbv
            dc_ref[...] = dcv
            dlam_ref[...] = dlam
            dd_ref[...] = ddv

        @pl.when(i > 0)
        def _():
            db_ref[...] += dbv
            dc_ref[...] += dcv
            dlam_ref[...] += dlam
            dd_ref[...] += ddv

    return pl.pallas_call(
        body, name=name, grid=(nb, ni),
        in_specs=[pl.BlockSpec((tr, LANES), rmap),
                  pl.BlockSpec((tr, LANES), lambda k, i: (ni - 1 - i, col0 + k)),
                  pl.BlockSpec((tr, sw), rmap),
                  pl.BlockSpec((16, sw), lambda k, i: (jnp.maximum((ni - 1 - i) * hb16 - 1, 0), k)),
                  pl.BlockSpec((None, LANES, sw), lambda k, i: (k, 0, 0)),
                  pl.BlockSpec((None, sw, LANES), lambda k, i: (k, 0, 0)),
                  pl.BlockSpec((None, 1, sw), lambda k, i: (k, 0, 0)),
                  pl.BlockSpec((1, LANES), lambda k, i: (0, k))],
        out_specs=[pl.BlockSpec((tr, LANES), rmap),
                   pl.BlockSpec((None, sw, LANES), lambda k, i: (k, 0, 0)),
                   pl.BlockSpec((None, sw, LANES), lambda k, i: (k, 0, 0)),
                   pl.BlockSpec((None, 1, sw), lambda k, i: (k, 0, 0)),
                   pl.BlockSpec((1, LANES), lambda k, i: (0, k))],
        out_shape=[jax.ShapeDtypeStruct((t_rows, width), F32), jax.ShapeDtypeStruct((nb, sw, LANES), F32),
                   jax.ShapeDtypeStruct((nb, sw, LANES), F32), jax.ShapeDtypeStruct((nb, 1, sw), F32),
                   jax.ShapeDtypeStruct((1, width), F32)],
        scratch_shapes=[pltpu.VMEM((8, sw), F32), pltpu.VMEM((tr, sw), F32)],
        compiler_params=_params(("parallel", "arbitrary")),
    )(dy, proj, states, states, bblk, cblk, lamblk, dvec)


def _blockdiag(m):
    nb, g, p, q = m.shape
    eye = jnp.eye(g, dtype=m.dtype)
    return (m[:, :, :, None, :] * eye[None, :, None, :, None]).reshape(nb, g * p, g * q)


def _blockdiag_take(m, p, q):
    nb = m.shape[0]
    g = GROUPS_PER_BLOCK
    m = m.reshape(nb, g, p, g, q)
    return jnp.stack([m[:, k, :, k, :] for k in range(g)], axis=1)


def _adamw(name, w, g, m, v, tr):
    rows, cols = w.shape
    gcols = g.shape[1]
    c1 = 1.0 - ADAM_B1 ** ADAM_STEP
    c2 = 1.0 - ADAM_B2 ** ADAM_STEP

    def body(w_ref, g_ref, m_ref, v_ref, go_ref, d_ref, mo_ref, vo_ref):
        gv = g_ref[...] if gcols == cols else g_ref[:, :cols]
        mn = ADAM_B1 * m_ref[...] + (1.0 - ADAM_B1) * gv
        vn = ADAM_B2 * v_ref[...] + (1.0 - ADAM_B2) * (gv * gv)
        go_ref[...] = gv
        mo_ref[...] = mn
        vo_ref[...] = vn
        d_ref[...] = -ADAM_LR * ((mn / c1) / (jnp.sqrt(vn / c2) + ADAM_EPS) + ADAM_WD * w_ref[...])

    spec = pl.BlockSpec((tr, cols), lambda i: (i, 0))
    return pl.pallas_call(
        body, name=name, grid=(rows // tr,),
        in_specs=[spec, pl.BlockSpec((tr, gcols), lambda i: (i, 0)), spec, spec],
        out_specs=[spec] * 4, out_shape=[jax.ShapeDtypeStruct((rows, cols), F32)] * 4,
        compiler_params=_params(("parallel",)),
    )(w, g, m, v)


def _ffn_fwd(tag, h, norm, wg, wu, wd, dims, host=None):
    t_rows, d, fp, tm, tr, trn = dims['T'], dims['D'], dims['FP'], dims['TM'], dims['TR'], dims['TRN']
    host = host or {}
    got = {}
    ft = N_CHIPS * fp
    tn = _tile(fp, 1408)
    npb = fp // tn
    n16, = _rowmap(f"{tag}_norm", lambda x, g: _rms(x, g).astype(BF16), (1, t_rows // trn),
                   [_rows(h, trn), _full(norm)], [_o((t_rows, d), BF16, trn)])

    def up(nm, w):
        res = _matmul(f"{tag}_{nm}", 'nn', n16, (tm, d // 2), lambda i, j, k: (i, k),
                      w, (None, d // 2, tn), lambda i, j, k: (j // npb, k, j % npb),
                      (t_rows // tm, ft // tn, 2), [((t_rows, ft), BF16, (tm, tn), lambda i, j: (i, j))],
                      lambda acc: acc, comms=host.get(nm, ()))
        got[nm] = res[1:]
        return res[0]

    gate = up("gate", wg)
    wu = got['gate'][wu] if isinstance(wu, int) else wu
    upv = up("up", wu)
    wd = got['up'][wd] if isinstance(wd, int) else wd
    tc = _tile(ft, 1408)
    act, = _rowmap(f"{tag}_act", lambda g, u: _silu(g.astype(F32)) * u.astype(F32), (ft // tc, t_rows // tr),
                   [_rows(gate, tr, tc), _rows(upv, tr, tc)], [_o((t_rows, ft), BF16, tr, tc)])
    tnd = _tile(d, 1024)
    res = _matmul(f"{tag}_down", 'nn', act, (tm, fp), lambda i, j, k: (i, k),
                  wd, (None, fp, tnd), lambda i, j, k: (k, 0, j),
                  (t_rows // tm, d // tnd, N_CHIPS), [((t_rows, d), F32, (tm, tnd), lambda i, j: (i, j))],
                  lambda acc, hin: hin + 0.5 * acc, extras=[(h, (tm, tnd), lambda i, j: (i, j))],
                  comms=host.get('down', ()))
    got['down'] = res[1:]
    return res[0], dict(h=h, n16=n16, gate=gate, up=upv, act=act, wu=wu, wd=wd), got


def _pair(nm, g, pos):
    return _pair_sum(f"pair_{nm}", g, _pair_swap(f"swap_{nm}", g), pos)


def _ffn_bwd(tag, names, dh, dh16, saved, norm, wg, dims, pos, host=None):
    t_rows, d, fp, tm, trn = dims['T'], dims['D'], dims['FP'], dims['TM'], dims['TRN']
    host = host or {}
    got = {}
    n_gate, n_up, n_down = names
    wu, wd = saved['wu'], saved['wd']
    ft = N_CHIPS * fp
    tn = _tile(fp, 1408)
    npb = fp // tn
    tm2 = tm // 2
    tk = t_rows // 2

    def act_bwd(acc, g, u):
        da = 0.5 * acc
        g32, u32 = g.astype(F32), u.astype(F32)
        sg = _sigmoid(g32)
        return da * u32 * (sg * (1.0 + g32 * (1.0 - sg))), da * (g32 * sg)

    res = _matmul(
        f"{tag}_dact", 'nt', dh16, (tm2, d // 2), lambda i, j, k: (i, k),
        wd, (None, tn, d // 2), lambda i, j, k: (j // npb, j % npb, k),
        (t_rows // tm2, ft // tn, 2),
        [((t_rows, ft), BF16, (tm2, tn), lambda i, j: (i, j))] * 2, act_bwd,
        extras=[(saved['gate'], (tm2, tn), lambda i, j: (i, j)), (saved['up'], (tm2, tn), lambda i, j: (i, j))],
        comms=host.get('dact', ()))
    dgate, dup, got['dact'] = res[0], res[1], res[2:]
    tnd = _tile(d, 1024)
    res = _matmul(f"{tag}_dwd", 'tn', saved['act'], (tk, tn), lambda i, j, k: (k, i),
                  dh16, (tk, tnd), lambda i, j, k: (k, j),
                  (ft // tn, d // tnd, 2),
                  [((N_CHIPS, fp, d), BF16, (None, tn, tnd), lambda i, j: (i // npb, i % npb, j))],
                  lambda acc: 0.5 * acc, comms=host.get('dwd', ()))
    dwd, got['dwd'] = res[0], res[1:]
    pair_d = _pair(n_down, dwd, pos)

    def dn_part(nm, dact, w, prev, comms=()):
        extras = [] if prev is None else [(prev, (tm, tnd), lambda i, j: (i, j))]
        return _matmul(f"{tag}_{nm}", 'nt', dact, (tm, fp), lambda i, j, k: (i, k),
                       w, (None, tnd, fp), lambda i, j, k: (k, j, 0),
                       (t_rows // tm, d // tnd, N_CHIPS), [((t_rows, d), F32, (tm, tnd), lambda i, j: (i, j))],
                       (lambda acc: acc) if prev is None else (lambda acc, p: acc + p), extras=extras, comms=comms)

    dn_g, land_d = dn_part("dn_gate", dgate, wg, None, [_scatter_comm(pair_d)])
    dn, = dn_part("dn_up", dup, wu, dn_g)

    def dw_up(nm, dact, comms=()):
        return _matmul(f"{tag}_{nm}", 'tn', saved['n16'], (tk, tnd), lambda i, j, k: (k, i),
                       dact, (tk, tn), lambda i, j, k: (k, j),
                       (d // tnd, ft // tn, 2),
                       [((N_CHIPS, d, fp), BF16, (None, tnd, tn), lambda i, j: (j // npb, i, j % npb))],
                       lambda acc: acc, comms=comms)

    dwg, = dw_up("dwg", dgate)
    pair_g = _pair(n_gate, dwg, pos)
    dwu, land_g = dw_up("dwu", dup, [_scatter_comm(pair_g)])
    pair_u = _pair(n_up, dwu, pos)

    def norm_bwd(x, dnv, dhv, g):
        _, vjp = jax.vjp(_rms, x, g)
        dx, dg = vjp(dnv)
        res = dhv + dx
        return res, res.astype(BF16), dg

    dh_in, dh_in16, dnorm = _rowmap(
        f"{tag}_dnorm", norm_bwd, (1, t_rows // trn),
        [_rows(saved['h'], trn), _rows(dn, trn), _rows(dh, trn), _full(norm)],
        [_o((t_rows, d), F32, trn), _o((t_rows, d), BF16, trn)], [_acc((1, d))])
    parts = {n_down: (pair_d, land_d), n_gate: (pair_g, land_g), n_up: (pair_u, None)}
    return dh_in, dh_in16, dnorm, parts, got


def kernel(x, meta_tokens, ffn1_norm, ffn1_w_gate, ffn1_w_up, ffn1_w_down, mix_norm, w_in, rg_conv_w, rg_conv_b, rg_w_a, rg_b_a, rg_w_x, rg_b_x, rg_lambda, s5_lambda_re, s5_lambda_im, s5_log_dt, s5_b_re, s5_b_im, s5_c_re, s5_c_im, s5_d, s5_glu_w, s5_glu_b, rg_out_norm, s5_out_norm, w_out, ffn2_norm, ffn2_w_gate, ffn2_w_up, ffn2_w_down, final_norm, loss_target, m_meta_tokens, m_ffn1_norm, m_ffn1_w_gate, m_ffn1_w_up, m_ffn1_w_down, m_mix_norm, m_w_in, m_rg_conv_w, m_rg_conv_b, m_rg_w_a, m_rg_b_a, m_rg_w_x, m_rg_b_x, m_rg_lambda, m_s5_lambda_re, m_s5_lambda_im, m_s5_log_dt, m_s5_b_re, m_s5_b_im, m_s5_c_re, m_s5_c_im, m_s5_d, m_s5_glu_w, m_s5_glu_b, m_rg_out_norm, m_s5_out_norm, m_w_out, m_ffn2_norm, m_ffn2_w_gate, m_ffn2_w_up, m_ffn2_w_down, m_final_norm, v_meta_tokens, v_ffn1_norm, v_ffn1_w_gate, v_ffn1_w_up, v_ffn1_w_down, v_mix_norm, v_w_in, v_rg_conv_w, v_rg_conv_b, v_rg_w_a, v_rg_b_a, v_rg_w_x, v_rg_b_x, v_rg_lambda, v_s5_lambda_re, v_s5_lambda_im, v_s5_log_dt, v_s5_b_re, v_s5_b_im, v_s5_c_re, v_s5_c_im, v_s5_d, v_s5_glu_w, v_s5_glu_b, v_rg_out_norm, v_s5_out_norm, v_w_out, v_ffn2_norm, v_ffn2_w_gate, v_ffn2_w_up, v_ffn2_w_down, v_final_norm):
    args = locals()
    w = {n: args[n] for n in WEIGHTS}
    mom = {n: args["m_" + n] for n in WEIGHTS}
    var = {n: args["v_" + n] for n in WEIGHTS}

    seq, d = x.shape[1], x.shape[2]
    f_shard = ffn1_w_gate.shape[2]
    fp = _round_up(f_shard, LANES)
    r = rg_conv_b.shape[1]
    s5w = s5_d.shape[1]
    hd = r // RG_HEADS
    groups = s5w // S5_GROUP
    nb = groups // GROUPS_PER_BLOCK
    t_rows = _round_up(N_META + seq, LANES)
    tm, tr, tk = t_rows // 4, t_rows // 8, t_rows // 2
    trn = _tile(t_rows, max(t_rows // 32, 16), 16)
    dims = dict(T=t_rows, D=d, FP=fp, TM=tm, TR=tr, TRN=trn)
    cx, cy, cc = lax.axis_index("x"), lax.axis_index("y"), lax.axis_index("c")
    chip = 2 * cx + cy
    pos = jnp.stack([chip, cc]).astype(jnp.int32)

    def gather(nm, buf):
        return _run_comm(f"ag_{nm}", _gather_comm(buf))[0]

    def gather_f32(nm, local):
        buf = lax.dynamic_update_slice(jnp.zeros((N_CHIPS,) + local.shape, F32), local[None], (chip, 0, 0))
        return gather(nm, buf)

    slot = {}
    for nm in ('ffn1_w_gate', 'ffn1_w_up', 'ffn2_w_gate', 'ffn2_w_up'):
        slot[nm] = _cast_pad(f"cast_{nm}", w[nm][0], d, fp, pos)
    for nm in ('ffn1_w_down', 'ffn2_w_down'):
        slot[nm] = _cast_pad(f"cast_{nm}", w[nm][0], fp, d, pos)
    slot['w_in'] = _cast_pad("cast_w_in", w_in[0], d, w_in.shape[2], pos)
    slot['w_out'] = _cast_pad("cast_w_out", w_out[0], w_out.shape[1], d, pos)
    slot['s5_glu_w'] = _cast_pad("cast_glu", s5_glu_w[0], s5_glu_w.shape[1], s5w, pos)
    rgw_local = jnp.concatenate([rg_w_a[0], rg_w_x[0]], axis=0).reshape(2 * RG_HEADS * (hd // N_CHIPS), hd)
    rgw = gather('rgw', _cast_pad("cast_rgw", rgw_local, rgw_local.shape[0], hd, pos))
    rgw = rgw.reshape(N_CHIPS, 2, RG_HEADS, hd // N_CHIPS, hd).transpose(1, 2, 0, 3, 4).reshape(2, RG_HEADS, hd, hd)
    w_a16, w_x16 = rgw[0], rgw[1]
    meta_full = gather_f32('meta', meta_tokens).transpose(1, 0, 2).reshape(N_META, d)
    conv_w2, b_a2, b_x2 = _gather_small_shards(rg_conv_w[0], rg_b_a[0], rg_b_x[0], gather_f32)
    wfull = {'ffn1_w_gate': gather('ffn1_w_gate', slot['ffn1_w_gate'])}

    pad_rows = t_rows - N_META - seq
    h0 = jnp.concatenate([meta_full, x[0], jnp.zeros((pad_rows, d), F32)], axis=0)
    tgt = jnp.concatenate([jnp.zeros((N_META, d), F32), loss_target[0], jnp.zeros((pad_rows, d), F32)], axis=0)

    h1, sv1, got = _ffn_fwd(
        "ffn1", h0, ffn1_norm, wfull['ffn1_w_gate'], 0, 0, dims,
        host={'gate': [_gather_comm(slot['ffn1_w_up'])], 'up': [_gather_comm(slot['ffn1_w_down'])],
              'down': [_gather_comm(slot['w_in']), _gather_comm(slot['s5_glu_w'])]})
    wfull['w_in'] = got['down'][0]
    wfull['s5_glu_w'] = got['down'][1].reshape(s5w, s5w)

    n2, = _rowmap("mix_norm", lambda xv, g: _rms(xv, g).astype(BF16), (1, t_rows // trn),
                  [_rows(h1, trn), _full(mix_norm)], [_o((t_rows, d), BF16, trn)])
    pw = w_in.shape[2]
    tnp = _tile(pw, 1536)
    nppb = pw // tnp
    proj, w_out_full = _matmul(
        "in_proj", 'nn', n2, (tm, d // 2), lambda i, j, k: (i, k),
        wfull['w_in'], (None, d // 2, tnp), lambda i, j, k: (j // nppb, k, j % nppb),
        (t_rows // tm, N_CHIPS * nppb, 2), [((t_rows, 3 * r), F32, (tm, tnp), lambda i, j: (i, j))],
        lambda acc: acc, comms=[_gather_comm(slot['w_out'])])
    wfull['w_out'] = w_out_full.reshape(d, d)

    xc = _conv_fwd("rg_conv", proj, conv_w2, rg_conv_b, r, tr)

    def head_cols(arr):
        return (arr, (tr, hd), lambda j, i: (i, j))

    def head_w(arr):
        return (arr, (None, hd, hd), lambda j, i: (j, 0, 0))

    def head_vec(arr):
        return (arr, (1, hd), lambda j, i: (0, j))

    def gates_fwd(xcv, wa, wx, ba, bx, lam):
        x16 = xcv.astype(BF16)
        pre_r = jnp.dot(x16, wa, preferred_element_type=F32) + ba
        pre_i = jnp.dot(x16, wx, preferred_element_type=F32) + bx
        return _rg_gate_math(xcv, pre_r, pre_i, _softplus_neg(lam))

    gate_params = [head_w(w_a16), head_w(w_x16), head_vec(b_a2), head_vec(b_x2), head_vec(rg_lambda)]
    head_out = ((t_rows, r), F32, (tr, hd), lambda j, i: (i, j))
    a_dec, bxv = _rowmap("rg_gates", gates_fwd, (RG_HEADS, t_rows // tr), [head_cols(xc)] + gate_params,
                         [head_out, head_out])
    h_rg = _linear_scan("rg_scan", a_dec, bxv, tr)

    expand = jnp.repeat(jnp.eye(S5_STATE, dtype=F32), S5_GROUP, axis=1)
    b_re2 = s5_b_re[0].reshape(groups, S5_STATE * S5_GROUP)
    b_im2 = s5_b_im[0].reshape(groups, S5_STATE * S5_GROUP)
    log_dt2 = s5_log_dt[0].reshape(groups, 1)
    lbr, lbi, bbr, bbi = _s5_disc("s5_disc", s5_lambda_re[0], s5_lambda_im[0], log_dt2, b_re2, b_im2, expand)

    def to_bblk(m):
        return _blockdiag(m.reshape(nb, GROUPS_PER_BLOCK, S5_STATE, S5_GROUP).transpose(0, 1, 3, 2))

    def to_cblk(m):
        return _blockdiag(m.reshape(nb, GROUPS_PER_BLOCK, S5_GROUP, S5_STATE).transpose(0, 1, 3, 2))

    bblk = jnp.concatenate([to_bblk(bbr), to_bblk(bbi)], axis=-1).astype(BF16)
    cblk = jnp.concatenate([to_cblk(s5_c_re[0]), -to_cblk(s5_c_im[0])], axis=-2).astype(BF16)
    hw = GROUPS_PER_BLOCK * S5_STATE
    lamblk = jnp.concatenate([lbr.reshape(nb, 1, hw), lbi.reshape(nb, 1, hw)], axis=-1)
    col0 = 2 * r // LANES
    y_pre, states = _s5_fwd("s5_fwd", proj, col0, bblk, cblk, lamblk, s5_d, s5w, tr)

    z16, = _rowmap("s5_gelu", lambda yv: _gelu(yv).astype(BF16), (1, t_rows // trn),
                   [_rows(y_pre, trn)], [_o((t_rows, s5w), BF16, trn)])
    tns = _tile(s5w, 1024)

    def glu_fwd(acc, yv, b):
        gl = acc + b
        return _gelu(yv) * _sigmoid(gl), gl

    y_s5, gl = _matmul("s5_glu", 'nn', z16, (tm, s5w), lambda i, j, k: (i, 0),
                       wfull['s5_glu_w'], (s5w, tns), lambda i, j, k: (0, j),
                       (t_rows // tm, s5w // tns, 1), [((t_rows, s5w), F32, (tm, tns), lambda i, j: (i, j))] * 2,
                       glu_fwd, extras=[(y_pre, (tm, tns), lambda i, j: (i, j)), (s5_glu_b, (1, tns), lambda i, j: (0, j))])

    def mix_out(hv, gv, yv, g1, g2):
        return jnp.concatenate([_rms(hv * _gelu(gv), g1), _rms(yv, g2)], axis=1)

    gate_in = (proj, (trn, r), lambda j, i: (i, 1))
    ycat16, = _rowmap("mix_out", lambda *a: mix_out(*a).astype(BF16), (1, t_rows // trn),
                      [_rows(h_rg, trn), gate_in, _rows(y_s5, trn), _full(rg_out_norm), _full(s5_out_norm)],
                      [_o((t_rows, r + s5w), BF16, trn)])
    tnd = _tile(d, 1024)
    h2, wfull['ffn2_w_gate'] = _matmul(
        "out_proj", 'nn', ycat16, (tm, (r + s5w) // 2), lambda i, j, k: (i, k),
        wfull['w_out'], ((r + s5w) // 2, tnd), lambda i, j, k: (k, j),
        (t_rows // tm, d // tnd, 2), [((t_rows, d), F32, (tm, tnd), lambda i, j: (i, j))],
        lambda acc, hin: hin + acc, extras=[(h1, (tm, tnd), lambda i, j: (i, j))],
        comms=[_gather_comm(slot['ffn2_w_gate'])])

    h3, sv2, _ = _ffn_fwd(
        "ffn2", h2, ffn2_norm, wfull['ffn2_w_gate'], 0, 0, dims,
        host={'gate': [_gather_comm(slot['ffn2_w_up'])], 'up': [_gather_comm(slot['ffn2_w_down'])]})

    fnorm2 = final_norm.reshape(1, d)

    def head(xv, tv, g):
        i = pl.program_id(1)
        rowi = lax.broadcasted_iota(jnp.int32, (trn, 1), 0) + i * trn
        mask = jnp.where((rowi >= N_META) & (rowi < N_META + seq), 1.0, 0.0)
        out, vjp = jax.vjp(_rms, xv, g)
        err = (out - tv) * mask
        dx, dg = vjp(err * (1.0 / d))
        return dx, dx.astype(BF16), jnp.sum(err * err, axis=0, keepdims=True), dg

    dh3, dh3_16, loss_cols, d_final = _rowmap(
        "loss_head", head, (1, t_rows // trn), [_rows(h3, trn), _rows(tgt, trn), _full(fnorm2)],
        [_o((t_rows, d), F32, trn), _o((t_rows, d), BF16, trn)], [_acc((1, d)), _acc((1, d))])
    loss = lax.psum(0.5 * jnp.sum(loss_cols) / d, ("x", "y", "c"))

    small = {}
    dh2, dh2_16, small['ffn2_norm'], parts, _ = _ffn_bwd(
        "ffn2b", ('ffn2_w_gate', 'ffn2_w_up', 'ffn2_w_down'), dh3, dh3_16, sv2, ffn2_norm, wfull['ffn2_w_gate'],
        dims, pos)

    dycat, = _matmul("d_out_proj", 'nt', dh2_16, (tm, d // 2), lambda i, j, k: (i, k),
                     wfull['w_out'], (tnd, d // 2), lambda i, j, k: (j, k),
                     (t_rows // tm, (r + s5w) // tnd, 2), [((t_rows, r + s5w), F32, (tm, tnd), lambda i, j: (i, j))],
                     lambda acc: acc)
    wo_rows = (r + s5w) // N_CHIPS
    tno = _tile(wo_rows, 1024)
    npo = wo_rows // tno
    dw_out, = _matmul("dw_out", 'tn', ycat16, (tk, tno), lambda i, j, k: (k, i),
                      dh2_16, (tk, tnd), lambda i, j, k: (k, j),
                      ((r + s5w) // tno, d // tnd, 2),
                      [((N_CHIPS, wo_rows, d), BF16, (None, tno, tnd), lambda i, j: (i // npo, i % npo, j))],
                      lambda acc: acc)
    pair_w_out = _pair('w_out', dw_out, pos)

    def mix_out_bwd(hv, gv, yv, dyc, g1, g2):
        _, vjp = jax.vjp(mix_out, hv, gv, yv, g1, g2)
        return vjp(dyc)

    dh_out, dgate_rg, dy_s5, small['rg_out_norm'], small['s5_out_norm'] = _rowmap(
        "mix_out_bwd", mix_out_bwd, (1, t_rows // trn),
        [_rows(h_rg, trn), gate_in, _rows(y_s5, trn), _rows(dycat, trn), _full(rg_out_norm), _full(s5_out_norm)],
        [_o((t_rows, r), F32, trn), _o((t_rows, r), F32, trn), _o((t_rows, s5w), F32, trn)],
        [_acc((1, r)), _acc((1, s5w))])

    def glu_bwd(dyv, yv, glv):
        zv = _gelu(yv)
        sg = _sigmoid(glv)
        dgl = dyv * zv * sg * (1.0 - sg)
        return dyv * sg, dgl.astype(BF16), jnp.sum(dgl, axis=0, keepdims=True)

    dz_dir, dgl16, small['s5_glu_b'] = _rowmap(
        "s5_glu_bwd", glu_bwd, (1, t_rows // trn), [_rows(dy_s5, trn), _rows(y_pre, trn), _rows(gl, trn)],
        [_o((t_rows, s5w), F32, trn), _o((t_rows, s5w), BF16, trn)], [_acc((1, s5w))])

    def dgelu(acc, dzd, yv):
        _, vjp = jax.vjp(_gelu, yv)
        return vjp(acc + dzd)[0]

    dy_pre, = _matmul("s5_dz", 'nt', dgl16, (tm, s5w), lambda i, j, k: (i, 0),
                      wfull['s5_glu_w'], (tns, s5w), lambda i, j, k: (j, 0),
                      (t_rows // tm, s5w // tns, 1), [((t_rows, s5w), F32, (tm, tns), lambda i, j: (i, j))],
                      dgelu, extras=[(dz_dir, (tm, tns), lambda i, j: (i, j)), (y_pre, (tm, tns), lambda i, j: (i, j))])
    gl_rows = s5w // N_CHIPS
    tng = _tile(gl_rows, 1024)
    npg = gl_rows // tng
    dw_glu, = _matmul("dw_glu", 'tn', z16, (tk, tng), lambda i, j, k: (k, i),
                      dgl16, (tk, tns), lambda i, j, k: (k, j),
                      (s5w // tng, s5w // tns, 2),
                      [((N_CHIPS, gl_rows, s5w), BF16, (None, tng, tns), lambda i, j: (i // npg, i % npg, j))],
                      lambda acc: acc)
    pair_glu = _pair('s5_glu_w', dw_glu, pos)
    du_s5, dbblk, dcblk, dlamblk, small['s5_d'] = _s5_bwd(
        "s5_bwd", dy_pre, proj, col0, states, bblk, cblk, lamblk, s5_d, s5w, tr)

    def from_bblk(m):
        return _blockdiag_take(m, S5_STATE, S5_GROUP).reshape(groups, S5_STATE * S5_GROUP)

    dbbr, dbbi = from_bblk(dbblk[:, :hw]), from_bblk(dbblk[:, hw:])
    dcs = lambda m: _blockdiag_take(m, S5_STATE, S5_GROUP).transpose(0, 1, 3, 2).reshape(1, groups, S5_GROUP, S5_STATE)
    small['s5_c_re'], small['s5_c_im'] = dcs(dcblk[:, :hw]), -dcs(dcblk[:, hw:])
    dlbr, dlbi = dlamblk[:, 0, :hw].reshape(groups, S5_STATE), dlamblk[:, 0, hw:].reshape(groups, S5_STATE)
    g_lre, g_lim, g_ldt, g_bre, g_bim = _s5_disc(
        "s5_disc_bwd", s5_lambda_re[0], s5_lambda_im[0], log_dt2, b_re2, b_im2, expand, cots=(dlbr, dlbi, dbbr, dbbi))
    small['s5_lambda_re'], small['s5_lambda_im'], small['s5_log_dt'] = g_lre[None], g_lim[None], g_ldt.reshape(1, groups)
    small['s5_b_re'] = g_bre.reshape(s5_b_re.shape)
    small['s5_b_im'] = g_bim.reshape(s5_b_im.shape)

    dh_rg = _linear_scan("rg_scan_bwd", a_dec, dh_out, tr, reverse=True, shift_a=True)

    hb = tr // 8

    def gates_bwd(xcv, dhv, hv, hprev, wa, wx, ba, bx, lam):
        i = pl.program_id(1)
        row = lax.broadcasted_iota(jnp.int32, (tr, hd), 0)
        first = jnp.where(i > 0, hprev[7:8, :], 0.0)
        h_prev = jnp.where(row == 0, jnp.broadcast_to(first, (tr, hd)), pltpu.roll(hv, 1, 0))
        x16 = xcv.astype(BF16)
        pre_r = jnp.dot(x16, wa, preferred_element_type=F32) + ba
        pre_i = jnp.dot(x16, wx, preferred_element_type=F32) + bx
        _, vjp = jax.vjp(_rg_gate_math, xcv, pre_r, pre_i, _softplus_neg(lam))
        dxc_, dpr, dpi, dsp = vjp((dhv * h_prev, dhv))
        dpr16, dpi16 = dpr.astype(BF16), dpi.astype(BF16)
        dxc_ = (dxc_ + lax.dot_general(dpr16, wa, _DN['nt'], preferred_element_type=F32)
                + lax.dot_general(dpi16, wx, _DN['nt'], preferred_element_type=F32))
        dwa = lax.dot_general(x16, dpr16, _DN['tn'], preferred_element_type=F32)
        dwx = lax.dot_general(x16, dpi16, _DN['tn'], preferred_element_type=F32)
        dlam = dsp * (-_sigmoid(-lam))
        return (dxc_, dwa, dwx, jnp.sum(dpr, axis=0, keepdims=True), jnp.sum(dpi, axis=0, keepdims=True), dlam)

    head_acc_w = ((RG_HEADS, hd, hd), F32, (None, hd, hd), lambda j, i: (j, 0, 0))
    head_acc_v = ((1, r), F32, (1, hd), lambda j, i: (0, j))
    h_halo = (h_rg, (8, hd), lambda j, i: (jnp.maximum(i * hb - 1, 0), j))
    dxc, d_wa, d_wx, d_ba, d_bx, d_lam = _rowmap(
        "rg_gates_bwd", gates_bwd, (RG_HEADS, t_rows // tr),
        [head_cols(xc), head_cols(dh_rg), head_cols(h_rg), h_halo] + gate_params,
        [head_out], [head_acc_w, head_acc_w, head_acc_v, head_acc_v, head_acc_v])
    small['rg_w_a'], small['rg_w_x'] = d_wa[None], d_wx[None]
    small['rg_b_a'], small['rg_b_x'] = d_ba.reshape(1, RG_HEADS, hd), d_bx.reshape(1, RG_HEADS, hd)
    small['rg_lambda'] = d_lam
    du_rg, d_convw, small['rg_conv_b'] = _conv_bwd("rg_conv_bwd", dxc, proj, conv_w2, r, tr)
    small['rg_conv_w'] = d_convw[None]

    dproj16, = _rowmap("dproj", lambda a, b, c: jnp.concatenate([a, b, c], axis=1).astype(BF16), (1, t_rows // trn),
                       [_rows(du_rg, trn), _rows(dgate_rg, trn), _rows(du_s5, trn)], [_o((t_rows, 3 * r), BF16, trn)])
    dn2, land_w_out = _matmul(
        "d_in_proj", 'nt', dproj16, (tm, pw), lambda i, j, k: (i, k),
        wfull['w_in'], (None, tnd, pw), lambda i, j, k: (k, j, 0),
        (t_rows // tm, d // tnd, N_CHIPS), [((t_rows, d), F32, (tm, tnd), lambda i, j: (i, j))],
        lambda acc: acc, comms=[_scatter_comm(pair_w_out)])
    parts['w_out'] = (pair_w_out, land_w_out)
    dw_in, = _matmul("dw_in", 'tn', n2, (tk, tnd), lambda i, j, k: (k, i),
                     dproj16, (tk, tnp), lambda i, j, k: (k, j),
                     (d // tnd, N_CHIPS * nppb, 2),
                     [((N_CHIPS, d, pw), BF16, (None, tnd, tnp), lambda i, j: (j // nppb, i, j % nppb))],
                     lambda acc: acc)
    pair_w_in = _pair('w_in', dw_in, pos)

    def norm_bwd(xv, dnv, dhv, g):
        _, vjp = jax.vjp(_rms, xv, g)
        dx, dg = vjp(dnv)
        res = dhv + dx
        return res, res.astype(BF16), dg

    dh1, dh1_16, small['mix_norm'] = _rowmap(
        "mix_dnorm", norm_bwd, (1, t_rows // trn), [_rows(h1, trn), _rows(dn2, trn), _rows(dh2, trn), _full(mix_norm)],
        [_o((t_rows, d), F32, trn), _o((t_rows, d), BF16, trn)], [_acc((1, d))])

    pair_u2 = parts['ffn2_w_up'][0]
    dh0, _, small['ffn1_norm'], parts1, got = _ffn_bwd(
        "ffn1b", ('ffn1_w_gate', 'ffn1_w_up', 'ffn1_w_down'), dh1, dh1_16, sv1, ffn1_norm, wfull['ffn1_w_gate'],
        dims, pos, host={'dact': [_scatter_comm(pair_u2)], 'dwd': [_scatter_comm(pair_glu), _scatter_comm(pair_w_in)]})
    parts.update(parts1)
    parts['ffn2_w_up'] = (pair_u2, got['dact'][0])
    parts['s5_glu_w'] = (pair_glu, got['dwd'][0])
    parts['w_in'] = (pair_w_in, got['dwd'][1])
    pair_u1 = parts['ffn1_w_up'][0]
    parts['ffn1_w_up'] = (pair_u1, _run_comm("rs_ffn1_w_up", _scatter_comm(pair_u1))[0])
    grad_x = dh0[N_META:N_META + seq][None]
    small['meta_tokens'] = dh0[:N_META]
    small['final_norm'] = d_final.reshape(d)

    halves = [_reduce_partials(f"red_{nm}", parts[nm][0], parts[nm][1], pos) for nm in BIG]
    wholes = _sibling_exchange("exchange", halves)
    gbig = {nm: wh.reshape(wh.shape[0] * wh.shape[1], wh.shape[2]) for nm, wh in zip(BIG, wholes)}

    small_names = [n for n in WEIGHTS if n not in BIG]
    full_shape = {n: small[n].shape for n in small_names}
    flat = jnp.concatenate([small[n].reshape(-1) for n in small_names])
    unit = 8 * 8 * LANES
    total = _round_up(flat.shape[0], unit)
    flat = jnp.concatenate([flat, jnp.zeros((total - flat.shape[0],), F32)])
    red = _all_reduce_small("ar_small", flat.reshape(total // LANES, LANES)).reshape(-1)
    gsmall = {}
    off = 0
    for n in small_names:
        size = math.prod(full_shape[n])
        gsmall[n] = _own_shard(n, red[off:off + size].reshape(full_shape[n]), w[n].shape, chip)
        off += size

    out_g, out_d, out_m, out_v = {}, {}, {}, {}
    for nm in BIG:
        shp = w[nm].shape
        w2 = w[nm].reshape(shp[1], shp[2])
        trw = _tile(shp[1], 128 if shp[2] <= 4096 else 64, 8)
        res = _adamw(f"adamw_{nm}", w2, gbig[nm], mom[nm].reshape(w2.shape), var[nm].reshape(w2.shape), trw)
        out_g[nm], out_d[nm], out_m[nm], out_v[nm] = [a.reshape(shp) for a in res]

    def pack(tree):
        fl = jnp.concatenate([tree[n].reshape(-1) for n in small_names])
        tot = _round_up(fl.shape[0], 8 * LANES)
        return jnp.concatenate([fl, jnp.zeros((tot - fl.shape[0],), F32)]).reshape(tot // LANES, LANES)

    wp, gp, mp, vp = pack(w), pack(gsmall), pack(mom), pack(var)
    res = _adamw("adamw_small", wp, gp, mp, vp, _tile(wp.shape[0], 512, 8))
    off = 0
    for n in small_names:
        size = math.prod(w[n].shape)
        for dst, src in zip((out_g, out_d, out_m, out_v), res):
            dst[n] = src.reshape(-1)[off:off + size].reshape(w[n].shape)
        off += size

    return (loss, grad_x, *[out_g[n] for n in WEIGHTS], *[out_d[n] for n in WEIGHTS],
            *[out_m[n] for n in WEIGHTS], *[out_v[n] for n in WEIGHTS])


def _gather_small_shards(conv_w, b_a, b_x, gather):
    cw = conv_w.shape[1]
    part = b_a.shape[1]

    def rows8(a):
        a = jnp.concatenate([a, jnp.zeros((a.shape[0], cw - a.shape[1]), F32)], axis=1)
        return jnp.concatenate([a, jnp.zeros((8 - a.shape[0], cw), F32)], axis=0) if a.shape[0] < 8 else a

    local = jnp.concatenate([rows8(conv_w), rows8(b_a), rows8(b_x), jnp.zeros((8, cw), F32)], axis=0)
    full = gather("rg_small", local)
    conv_full = full[:, :CONV_WIDTH].transpose(1, 0, 2).reshape(CONV_WIDTH, N_CHIPS * cw)
    bias = lambda k: full[:, 8 * k:8 * k + RG_HEADS, :part].transpose(1, 0, 2).reshape(1, RG_HEADS * N_CHIPS * part)
    return conv_full, bias(1), bias(2)


def _own_shard(name, g, local_shape, chip):
    if tuple(g.shape) == tuple(local_shape):
        return g
    axis = [k for k, (a, b) in enumerate(zip(g.shape, local_shape)) if a != b][0]
    size = local_shape[axis]
    return lax.dynamic_slice_in_dim(g, chip * size, size, axis=axis)
```

```python
import functools
import math

import jax
import jax.numpy as jnp
from jax import lax
from jax.experimental import pallas as pl
from jax.experimental.pallas import tpu as pltpu

F32 = jnp.float32
BF16 = jnp.bfloat16
MESH = pl.DeviceIdType.MESH

N_META = 16
RG_HEADS = 8
CONV_WIDTH = 4
RG_C = 8.0
S5_GROUP = 16
S5_STATE = 64
GROUPS_PER_BLOCK = 8
EPS = 1e-6
N_CHIPS = 4
LANES = 128
VMEM_LIMIT = 56 * 1024 * 1024

ADAM_LR = 0.001
ADAM_B1 = 0.9
ADAM_B2 = 0.999
ADAM_EPS = 1e-08
ADAM_WD = 0.01
ADAM_STEP = 10

WEIGHTS = ['meta_tokens', 'ffn1_norm', 'ffn1_w_gate', 'ffn1_w_up', 'ffn1_w_down', 'mix_norm', 'w_in', 'rg_conv_w',
           'rg_conv_b', 'rg_w_a', 'rg_b_a', 'rg_w_x', 'rg_b_x', 'rg_lambda', 's5_lambda_re', 's5_lambda_im',
           's5_log_dt', 's5_b_re', 's5_b_im', 's5_c_re', 's5_c_im', 's5_d', 's5_glu_w', 's5_glu_b', 'rg_out_norm',
           's5_out_norm', 'w_out', 'ffn2_norm', 'ffn2_w_gate', 'ffn2_w_up', 'ffn2_w_down', 'final_norm']
BIG = ('ffn1_w_gate', 'ffn1_w_up', 'ffn1_w_down', 'w_in', 's5_glu_w', 'w_out', 'ffn2_w_gate', 'ffn2_w_up',
       'ffn2_w_down')

_DN = {'nn': (((1,), (0,)), ((), ())), 'nt': (((1,), (1,)), ((), ())), 'tn': (((0,), (0,)), ((), ()))}


def _round_up(n, m):
    return (n + m - 1) // m * m


def _tile(n, pref, unit=LANES):
    best = None
    for t in range(unit, min(n, pref) + 1, unit):
        if n % t == 0:
            best = t
    return best if best is not None else n


def _params(sem=None):
    return pltpu.CompilerParams(dimension_semantics=sem, vmem_limit_bytes=VMEM_LIMIT)


def _rms(x, g):
    return x * lax.rsqrt(jnp.mean(x * x, axis=-1, keepdims=True) + EPS) * g


def _sigmoid(x):
    return 1.0 / (1.0 + jnp.exp(-x))


def _gelu(x):
    return 0.5 * x * (1.0 + jnp.tanh(math.sqrt(2.0 / math.pi) * (x + 0.044715 * (x * x * x))))


def _silu(x):
    return x * _sigmoid(x)


def _expm1(x):
    series = x * (1.0 + x * (1.0 / 2) * (1.0 + x * (1.0 / 3) * (1.0 + x * (1.0 / 4) * (1.0 + x * (1.0 / 5) * (1.0 + x * (1.0 / 6))))))
    return jnp.where(jnp.abs(x) < 0.3, series, jnp.exp(x) - 1.0)


def _softplus_neg(lam):
    m = jnp.maximum(-lam, 0.0)
    e = jnp.exp(-jnp.abs(lam))
    w = 1.0 + e
    log1p = jnp.where(w == 1.0, e, jnp.log(w) * (e / jnp.where(w == 1.0, 1.0, w - 1.0)))
    return m + log1p


def _rg_gate_math(xc, pre_r, pre_i, sp):
    r = _sigmoid(pre_r)
    i = _sigmoid(pre_i)
    log_a = -RG_C * r * sp
    a = jnp.exp(log_a)
    mult = jnp.sqrt(-_expm1(2.0 * log_a))
    return a, mult * i * xc


class _Comm:
    def __init__(self, arrays, out_shapes, aliased, sems, start, finish):
        self.arrays, self.out_shapes, self.aliased, self.sems = arrays, out_shapes, aliased, sems
        self.start, self.finish = start, finish


def _matmul(name, mode, a, a_blk, a_map, b, b_blk, b_map, grid, outs, epilogue, extras=(), comms=()):
    ni, nj, nk = grid
    ne, no = len(extras), len(outs)
    sq = lambda blk: tuple(d for d in blk if d is not None)
    ab, bb = sq(a_blk), sq(b_blk)
    acc_shape = {'nn': (ab[0], bb[1]), 'nt': (ab[0], bb[0]), 'tn': (ab[1], bb[1])}[mode]
    n_cin = sum(len(cm.arrays) for cm in comms)
    n_cout = sum(len(cm.out_shapes) for cm in comms)
    n_acc = 1 if nk > 1 else 0

    def comm_refs(refs):
        cin = refs[2 + ne:2 + ne + n_cin]
        cout = refs[2 + ne + n_cin + no:2 + ne + n_cin + no + n_cout]
        csem = refs[2 + ne + n_cin + no + n_cout + n_acc:]
        for cm in comms:
            yield cm, cin[:len(cm.arrays)], cout[:len(cm.out_shapes)], csem[:len(cm.sems)]
            cin, cout, csem = cin[len(cm.arrays):], cout[len(cm.out_shapes):], csem[len(cm.sems):]

    def body(*refs):
        a_ref, b_ref = refs[0], refs[1]
        ex = refs[2:2 + ne]
        out = refs[2 + ne + n_cin:2 + ne + n_cin + no]
        if comms:
            @pl.when((pl.program_id(0) == 0) & (pl.program_id(1) == 0) & (pl.program_id(2) == 0))
            def _():
                for cm, cin, cout, csem in comm_refs(refs):
                    cm.start(cin, cout, csem)

        part = lax.dot_general(a_ref[...], b_ref[...], _DN[mode], preferred_element_type=F32)

        def finish(acc):
            res = epilogue(acc, *[e[...] for e in ex])
            if not isinstance(res, tuple):
                res = (res,)
            for o, r in zip(out, res):
                o[...] = r.astype(o.dtype)

        if nk == 1:
            finish(part)
        else:
            acc_ref = refs[2 + ne + n_cin + no + n_cout]
            k = pl.program_id(2)

            @pl.when(k == 0)
            def _():
                acc_ref[...] = part

            @pl.when(k > 0)
            def _():
                acc_ref[...] += part

            @pl.when(k == nk - 1)
            def _():
                finish(acc_ref[...])

        if comms:
            @pl.when((pl.program_id(0) == ni - 1) & (pl.program_id(1) == nj - 1) & (pl.program_id(2) == nk - 1))
            def _():
                for cm, cin, cout, csem in comm_refs(refs):
                    cm.finish(cin, cout, csem)

    hbm = pl.BlockSpec(memory_space=pltpu.HBM)
    in_specs = [pl.BlockSpec(a_blk, a_map), pl.BlockSpec(b_blk, b_map)]
    in_specs += [pl.BlockSpec(blk, functools.partial(lambda m, i, j, k: m(i, j), m)) for _, blk, m in extras]
    in_specs += [hbm] * n_cin
    out_specs = [pl.BlockSpec(blk, functools.partial(lambda m, i, j, k: m(i, j), m)) for _, _, blk, m in outs]
    out_specs += [hbm] * n_cout
    aliases, cin_at, cout_at = {}, 2 + ne, no
    for cm in comms:
        if cm.aliased:
            aliases.update({cin_at + k: cout_at + k for k in range(len(cm.arrays))})
        cin_at, cout_at = cin_at + len(cm.arrays), cout_at + len(cm.out_shapes)
    res = pl.pallas_call(
        body, name=name, grid=grid, in_specs=in_specs, out_specs=out_specs,
        out_shape=[jax.ShapeDtypeStruct(s, d) for s, d, _, _ in outs] + [s for cm in comms for s in cm.out_shapes],
        scratch_shapes=([pltpu.VMEM(acc_shape, F32)] if nk > 1 else [])
        + [pltpu.SemaphoreType.DMA((n,)) for cm in comms for n in cm.sems],
        input_output_aliases=aliases,
        compiler_params=_params(("arbitrary",) * 3 if comms else ("parallel", "parallel", "arbitrary")),
    )(a, b, *[e for e, _, _ in extras], *[arr for cm in comms for arr in cm.arrays])
    return res


def _rows(arr, tr, tc=None, col0=0):
    if tc is None:
        return (arr, (tr, arr.shape[1]), lambda j, i: (i, 0))
    return (arr, (tr, tc), lambda j, i: (i, col0 + j))


def _full(arr):
    nd = arr.ndim
    return (arr, arr.shape, lambda j, i: (0,) * nd)


def _cols(arr, tc):
    return (arr, (arr.shape[0], tc), lambda j, i: (0, j))


def _rowmap(name, fn, grid, ins, outs, accs=(), scratch=()):
    nj, ni = grid
    n_in, n_out, n_acc = len(ins), len(outs), len(accs)

    def body(*refs):
        vals = [r[...] for r in refs[:n_in]]
        o_refs = refs[n_in:n_in + n_out]
        a_refs = refs[n_in + n_out:n_in + n_out + n_acc]
        s_refs = refs[n_in + n_out + n_acc:]
        res = fn(*vals, *s_refs)
        if not isinstance(res, tuple):
            res = (res,)
        for o, r in zip(o_refs, res[:n_out]):
            o[...] = r.astype(o.dtype)
        i = pl.program_id(1)
        for a_ref, r in zip(a_refs, res[n_out:]):
            @pl.when(i == 0)
            def _(a_ref=a_ref, r=r):
                a_ref[...] = r.astype(a_ref.dtype)

            @pl.when(i > 0)
            def _(a_ref=a_ref, r=r):
                a_ref[...] += r.astype(a_ref.dtype)

    res = pl.pallas_call(
        body, name=name, grid=grid,
        in_specs=[pl.BlockSpec(blk, m) for _, blk, m in ins],
        out_specs=[pl.BlockSpec(blk, m) for _, _, blk, m in list(outs) + list(accs)],
        out_shape=[jax.ShapeDtypeStruct(s, d) for s, d, _, _ in list(outs) + list(accs)],
        scratch_shapes=list(scratch),
        compiler_params=_params(("parallel", "arbitrary")),
    )(*[a for a, _, _ in ins])
    return res


def _o(shape, dtype, tr, tc=None):
    if tc is None:
        return (shape, dtype, (tr, shape[1]), lambda j, i: (i, 0))
    return (shape, dtype, (tr, tc), lambda j, i: (i, j))


def _acc(shape, tc=None):
    if tc is None:
        nd = len(shape)
        return (shape, F32, shape, lambda j, i: (0,) * nd)
    return (shape, F32, (shape[0], tc), lambda j, i: (0, j))


def _position():
    x, y, c = lax.axis_index("x"), lax.axis_index("y"), lax.axis_index("c")
    return x, y, c


def _gather_comm(buf):
    _, rows, cols = buf.shape
    rh = rows // 2

    def plan(out_ref, sems):
        x, y, c = _position()
        chips = [(1 - x, y), (x, 1 - y), (1 - x, 1 - y)]

        def copy(pair, j, chip_index, half, to):
            return pltpu.make_async_remote_copy(
                src_ref=out_ref.at[chip_index, half], dst_ref=out_ref.at[chip_index, half],
                send_sem=sems[pair].at[j], recv_sem=sems[pair + 1].at[j], device_id=to, device_id_type=MESH)

        mine, other = pl.ds(c * rh, rh), pl.ds((1 - c) * rh, rh)
        ici_out = [copy(0, j, 2 * x + y, mine, (px, py, c)) for j, (px, py) in enumerate(chips)]
        ici_in = [copy(0, j, 2 * px + py, mine, (px, py, c)) for j, (px, py) in enumerate(chips)]
        d2d_out = [copy(2, j, 2 * px + py, mine, (x, y, 1 - c)) for j, (px, py) in enumerate(chips)]
        d2d_in = [copy(2, j, 2 * px + py, other, (x, y, 1 - c)) for j, (px, py) in enumerate(chips)]
        return ici_out, ici_in, d2d_out, d2d_in

    def start(ins, outs, sems):
        for cp in plan(outs[0], sems)[0]:
            cp.start()

    def finish(ins, outs, sems):
        ici_out, ici_in, d2d_out, d2d_in = plan(outs[0], sems)
        for arrived, fwd in zip(ici_in, d2d_out):
            arrived.wait_recv()
            fwd.start()
        for cp in d2d_in:
            cp.wait_recv()
        for cp in ici_out + d2d_out:
            cp.wait_send()

    return _Comm([buf], [jax.ShapeDtypeStruct(buf.shape, buf.dtype)], True, [3, 3, 3, 3], start, finish)


def _scatter_comm(p):
    _, rh, cols = p.shape

    def plan(p_ref, land_ref, sems):
        x, y, c = _position()
        chips = [(1 - x, y), (x, 1 - y), (1 - x, 1 - y)]

        def copy(j, chip_index, to):
            return pltpu.make_async_remote_copy(
                src_ref=p_ref.at[chip_index], dst_ref=land_ref.at[j],
                send_sem=sems[0].at[j], recv_sem=sems[1].at[j], device_id=to, device_id_type=MESH)

        out = [copy(j, 2 * px + py, (px, py, c)) for j, (px, py) in enumerate(chips)]
        arrive = [copy(j, 2 * x + y, (x, y, c)) for j in range(3)]
        return out, arrive

    def start(ins, outs, sems):
        for cp in plan(ins[0], outs[0], sems)[0]:
            cp.start()

    def finish(ins, outs, sems):
        out, arrive = plan(ins[0], outs[0], sems)
        for cp in arrive:
            cp.wait_recv()
        for cp in out:
            cp.wait_send()

    return _Comm([p], [jax.ShapeDtypeStruct((3, rh, cols), p.dtype)], False, [3, 3], start, finish)


def _run_comm(name, cm):
    n_in, n_out = len(cm.arrays), len(cm.out_shapes)

    def body(*refs):
        ins, outs, sems = refs[:n_in], refs[n_in:n_in + n_out], refs[n_in + n_out:]
        cm.start(ins, outs, sems)
        cm.finish(ins, outs, sems)

    hbm = pl.BlockSpec(memory_space=pltpu.HBM)
    return pl.pallas_call(
        body, name=name, in_specs=[hbm] * n_in, out_specs=[hbm] * n_out, out_shape=list(cm.out_shapes),
        input_output_aliases={k: k for k in range(n_in)} if cm.aliased else {},
        scratch_shapes=[pltpu.SemaphoreType.DMA((n,)) for n in cm.sems],
    )(*cm.arrays)


def _swap_comm(g):
    n, rows, cols = g.shape
    rh = rows // 2

    def plan(g_ref, land_ref, sems):
        x, y, c = _position()

        def copy(k, half):
            return pltpu.make_async_remote_copy(
                src_ref=g_ref.at[k, pl.ds(half * rh, rh)], dst_ref=land_ref.at[k],
                send_sem=sems[0].at[k], recv_sem=sems[1].at[k], device_id=(x, y, 1 - c), device_id_type=MESH)

        return [copy(k, 1 - c) for k in range(n)], [copy(k, c) for k in range(n)]

    def start(ins, outs, sems):
        for cp in plan(ins[0], outs[0], sems)[0]:
            cp.start()

    def finish(ins, outs, sems):
        out, arrive = plan(ins[0], outs[0], sems)
        for cp in arrive:
            cp.wait_recv()
        for cp in out:
            cp.wait_send()

    return _Comm([g], [jax.ShapeDtypeStruct((n, rh, cols), g.dtype)], False, [n, n], start, finish)


def _pair_sum(name, g, land, pos):
    n, rows, cols = g.shape
    rh = rows // 2
    tr = _tile(rh, 256, 16)
    nt = rh // tr

    def body(pos_ref, g_ref, l_ref, o_ref):
        o_ref[...] = (g_ref[...].astype(F32) + l_ref[...].astype(F32)).astype(o_ref.dtype)

    grid_spec = pltpu.PrefetchScalarGridSpec(
        num_scalar_prefetch=1, grid=(n, nt),
        in_specs=[pl.BlockSpec((None, tr, cols), lambda k, i, p: (k, p[1] * nt + i, 0)),
                  pl.BlockSpec((None, tr, cols), lambda k, i, p: (k, i, 0))],
        out_specs=pl.BlockSpec((None, tr, cols), lambda k, i, p: (k, i, 0)))
    return pl.pallas_call(
        body, name=name, grid_spec=grid_spec, out_shape=jax.ShapeDtypeStruct((n, rh, cols), g.dtype),
        compiler_params=_params(("parallel", "parallel")),
    )(pos, g, land)


def _reduce_partials(name, p, land, pos):
    _, rh, cols = p.shape
    tr = _tile(rh, 256, 16)
    nt = rh // tr

    def body(pos_ref, p_ref, l_ref, o_ref):
        acc = p_ref[...].astype(F32)
        for k in range(3):
            acc = acc + l_ref[k].astype(F32)
        o_ref[...] = acc

    grid_spec = pltpu.PrefetchScalarGridSpec(
        num_scalar_prefetch=1, grid=(nt,),
        in_specs=[pl.BlockSpec((None, tr, cols), lambda i, p_: (p_[0], i, 0)),
                  pl.BlockSpec((3, tr, cols), lambda i, p_: (0, i, 0))],
        out_specs=pl.BlockSpec((None, tr, cols), lambda i, p_: (p_[1], i, 0)))
    return pl.pallas_call(
        body, name=name, grid_spec=grid_spec, out_shape=jax.ShapeDtypeStruct((2, rh, cols), F32),
        compiler_params=_params(("parallel",)),
    )(pos, p, land)


def _sibling_exchange(name, bufs):
    n = len(bufs)

    def body(*refs):
        outs = refs[n:2 * n]
        send_sems, recv_sems = refs[2 * n:]
        x, y, c = _position()
        sibling = (x, y, 1 - c)

        def copy(k, half):
            return pltpu.make_async_remote_copy(
                src_ref=outs[k].at[half], dst_ref=outs[k].at[half], send_sem=send_sems.at[k], recv_sem=recv_sems.at[k],
                device_id=sibling, device_id_type=MESH)

        sends = [copy(k, c) for k in range(n)]
        for cp in sends:
            cp.start()
        for k in range(n):
            copy(k, 1 - c).wait_recv()
        for cp in sends:
            cp.wait_send()

    hbm = pl.BlockSpec(memory_space=pltpu.HBM)
    return pl.pallas_call(
        body, name=name, in_specs=[hbm] * n, out_specs=[hbm] * n, input_output_aliases={k: k for k in range(n)},
        out_shape=[jax.ShapeDtypeStruct(b.shape, b.dtype) for b in bufs],
        scratch_shapes=[pltpu.SemaphoreType.DMA((n,)), pltpu.SemaphoreType.DMA((n,))],
    )(*bufs)


def _all_reduce_small(name, buf):
    rows = buf.shape[0]
    p = rows // 8

    def body(buf_ref, out_ref, land_ref, red_ref, s1, r1, s2, r2):
        x, y, c = _position()
        me = 4 * x + 2 * y + c

        def peer(r):
            px = 1 - x if (r >> 2) & 1 else x
            py = 1 - y if (r >> 1) & 1 else y
            pc = 1 - c if r & 1 else c
            return (px, py, pc), 4 * px + 2 * py + pc

        firsts = []
        for r in range(1, 8):
            to, d = peer(r)
            cp = pltpu.make_async_remote_copy(
                src_ref=buf_ref.at[pl.ds(d * p, p)], dst_ref=land_ref.at[r - 1],
                send_sem=s1.at[r - 1], recv_sem=r1.at[r - 1], device_id=to, device_id_type=MESH)
            cp.start()
            firsts.append(cp)
        acc = buf_ref[pl.ds(me * p, p), :]
        for r in range(1, 8):
            firsts[r - 1].wait_recv()
            acc = acc + land_ref[r - 1]
        red_ref[...] = acc
        out_ref[pl.ds(me * p, p), :] = acc
        seconds = []
        for r in range(1, 8):
            to, d = peer(r)
            cp = pltpu.make_async_remote_copy(
                src_ref=red_ref, dst_ref=out_ref.at[pl.ds(me * p, p)],
                send_sem=s2.at[r - 1], recv_sem=r2.at[r - 1], device_id=to, device_id_type=MESH)
            cp.start()
            seconds.append(cp)
        for r in range(1, 8):
            to, d = peer(r)
            pltpu.make_async_remote_copy(
                src_ref=red_ref, dst_ref=out_ref.at[pl.ds(d * p, p)],
                send_sem=s2.at[r - 1], recv_sem=r2.at[r - 1], device_id=to, device_id_type=MESH).wait_recv()
        for cp in firsts + seconds:
            cp.wait_send()

    vmem = pl.BlockSpec(memory_space=pltpu.VMEM)
    return pl.pallas_call(
        body, name=name, in_specs=[vmem], out_specs=vmem,
        out_shape=jax.ShapeDtypeStruct(buf.shape, F32),
        scratch_shapes=[pltpu.VMEM((7, p, LANES), F32), pltpu.VMEM((p, LANES), F32),
                        pltpu.SemaphoreType.DMA((7,)), pltpu.SemaphoreType.DMA((7,)),
                        pltpu.SemaphoreType.DMA((7,)), pltpu.SemaphoreType.DMA((7,))],
        compiler_params=pltpu.CompilerParams(vmem_limit_bytes=VMEM_LIMIT),
    )(buf)


def _cast_pad(name, w, rows_to, cols_to, pos):
    rows, cols = w.shape
    tr = _tile(math.gcd(rows, rows_to), 256, 16)
    assert rows % tr == 0 and rows_to % tr == 0, (rows, rows_to, tr)
    n_src = rows // tr

    def body(pos_ref, w_ref, o_ref):
        i = pl.program_id(0)
        if cols_to > cols:
            o_ref[:, cols:] = jnp.zeros((tr, cols_to - cols), BF16)

        @pl.when(i < n_src)
        def _():
            o_ref[:, :cols] = w_ref[...].astype(BF16)

        if rows_to > rows:
            @pl.when(i >= n_src)
            def _():
                o_ref[:, :cols] = jnp.zeros((tr, cols), BF16)

    grid_spec = pltpu.PrefetchScalarGridSpec(
        num_scalar_prefetch=1, grid=(rows_to // tr,),
        in_specs=[pl.BlockSpec((tr, cols), lambda i, p: (jnp.minimum(i, n_src - 1), 0))],
        out_specs=pl.BlockSpec((None, tr, cols_to), lambda i, p: (p[0], i, 0)))
    return pl.pallas_call(
        body, name=name, grid_spec=grid_spec,
        out_shape=jax.ShapeDtypeStruct((N_CHIPS, rows_to, cols_to), BF16),
        compiler_params=_params(("parallel",)),
    )(pos, w)


def _scan_rows(a, b, row, tr, reverse):
    sh = 1
    while sh < tr:
        if reverse:
            valid = row < tr - sh
            a_s = pltpu.roll(a, tr - sh, 0)
            b_s = pltpu.roll(b, tr - sh, 0)
        else:
            valid = row >= sh
            a_s = pltpu.roll(a, sh, 0)
            b_s = pltpu.roll(b, sh, 0)
        b = b + jnp.where(valid, a * b_s, 0.0)
        a = jnp.where(valid, a * a_s, a)
        sh *= 2
    return a, b


def _linear_scan(name, a, b, tr, reverse=False, shift_a=False):
    t_rows, cols = a.shape
    tc = _tile(cols, 512)
    ni, nj = t_rows // tr, cols // tc
    hb = tr // 8

    def rmap(j, i):
        return ((ni - 1 - i) if reverse else i, j)

    def halo_map(j, i):
        ri = ni - 1 - i
        return (jnp.minimum((ri + 1) * hb, t_rows // 8 - 1), j)

    def body(*refs):
        if shift_a:
            a_ref, halo_ref, b_ref, h_ref, carry_ref = refs
        else:
            a_ref, b_ref, h_ref, carry_ref = refs
        i = pl.program_id(1)
        row = lax.broadcasted_iota(jnp.int32, (tr, tc), 0)

        @pl.when(i == 0)
        def _():
            carry_ref[...] = jnp.zeros_like(carry_ref)

        av = a_ref[...]
        if shift_a:
            nxt = jnp.where(i > 0, halo_ref[0:1, :], 0.0)
            av = jnp.where(row == tr - 1, jnp.broadcast_to(nxt, (tr, tc)), pltpu.roll(av, tr - 1, 0))
        pa, hb_ = _scan_rows(av, b_ref[...], row, tr, reverse)
        h = hb_ + pa * carry_ref[0:1, :]
        h_ref[...] = h
        last = h[0:1, :] if reverse else h[tr - 1:tr, :]
        carry_ref[...] = jnp.broadcast_to(last, carry_ref.shape)

    in_specs = [pl.BlockSpec((tr, tc), rmap)]
    args = [a]
    if shift_a:
        in_specs.append(pl.BlockSpec((8, tc), halo_map))
        args.append(a)
    in_specs.append(pl.BlockSpec((tr, tc), rmap))
    args.append(b)
    return pl.pallas_call(
        body, name=name, grid=(nj, ni), in_specs=in_specs, out_specs=pl.BlockSpec((tr, tc), rmap),
        out_shape=jax.ShapeDtypeStruct((t_rows, cols), F32), scratch_shapes=[pltpu.VMEM((8, tc), F32)],
        compiler_params=_params(("parallel", "arbitrary")),
    )(*args)


def _conv_fwd(name, proj, conv_w, conv_b, width, tr):
    t_rows = proj.shape[0]
    ni = t_rows // tr
    hb = tr // 8
    tc = _tile(width, 512)

    def body(u_ref, halo_ref, w_ref, b_ref, o_ref, ext_ref):
        i = pl.program_id(1)
        ext_ref[0:8, :] = jnp.where(i > 0, halo_ref[...], 0.0)
        ext_ref[8:8 + tr, :] = u_ref[...]
        acc = b_ref[...] + w_ref[CONV_WIDTH - 1:CONV_WIDTH, :] * u_ref[...]
        for k in range(CONV_WIDTH - 1):
            acc = acc + w_ref[k:k + 1, :] * ext_ref[pl.ds(8 - (CONV_WIDTH - 1) + k, tr), :]
        o_ref[...] = acc

    return pl.pallas_call(
        body, name=name, grid=(width // tc, ni),
        in_specs=[pl.BlockSpec((tr, tc), lambda j, i: (i, j)),
                  pl.BlockSpec((8, tc), lambda j, i: (jnp.maximum(i * hb - 1, 0), j)),
                  pl.BlockSpec((CONV_WIDTH, tc), lambda j, i: (0, j)),
                  pl.BlockSpec((1, tc), lambda j, i: (0, j))],
        out_specs=pl.BlockSpec((tr, tc), lambda j, i: (i, j)),
        out_shape=jax.ShapeDtypeStruct((t_rows, width), F32),
        scratch_shapes=[pltpu.VMEM((tr + 8, tc), F32)],
        compiler_params=_params(("parallel", "parallel")),
    )(proj, proj, conv_w, conv_b)


def _conv_bwd(name, dxc, proj, conv_w, width, tr):
    t_rows = dxc.shape[0]
    ni = t_rows // tr
    hb = tr // 8
    tc = _tile(width, 512)

    def body(d_ref, dnext_ref, u_ref, uprev_ref, w_ref, du_ref, dw_ref, db_ref, dext_ref, uext_ref):
        i = pl.program_id(1)
        dext_ref[0:tr, :] = d_ref[...]
        dext_ref[tr:tr + 8, :] = jnp.where(i < ni - 1, dnext_ref[...], 0.0)
        uext_ref[0:8, :] = jnp.where(i > 0, uprev_ref[...], 0.0)
        uext_ref[8:8 + tr, :] = u_ref[...]
        d = d_ref[...]
        du = w_ref[CONV_WIDTH - 1:CONV_WIDTH, :] * d
        dws = []
        for k in range(CONV_WIDTH - 1):
            du = du + w_ref[k:k + 1, :] * dext_ref[pl.ds(CONV_WIDTH - 1 - k, tr), :]
            dws.append(jnp.sum(d * uext_ref[pl.ds(8 - (CONV_WIDTH - 1) + k, tr), :], axis=0, keepdims=True))
        dws.append(jnp.sum(d * u_ref[...], axis=0, keepdims=True))
        du_ref[...] = du
        dw = jnp.concatenate(dws, axis=0)
        db = jnp.sum(d, axis=0, keepdims=True)

        @pl.when(i == 0)
        def _():
            dw_ref[...] = dw
            db_ref[...] = db

        @pl.when(i > 0)
        def _():
            dw_ref[...] += dw
            db_ref[...] += db

    return pl.pallas_call(
        body, name=name, grid=(width // tc, ni),
        in_specs=[pl.BlockSpec((tr, tc), lambda j, i: (i, j)),
                  pl.BlockSpec((8, tc), lambda j, i: (jnp.minimum((i + 1) * hb, t_rows // 8 - 1), j)),
                  pl.BlockSpec((tr, tc), lambda j, i: (i, j)),
                  pl.BlockSpec((8, tc), lambda j, i: (jnp.maximum(i * hb - 1, 0), j)),
                  pl.BlockSpec((CONV_WIDTH, tc), lambda j, i: (0, j))],
        out_specs=[pl.BlockSpec((tr, tc), lambda j, i: (i, j)),
                   pl.BlockSpec((CONV_WIDTH, tc), lambda j, i: (0, j)),
                   pl.BlockSpec((1, tc), lambda j, i: (0, j))],
        out_shape=[jax.ShapeDtypeStruct((t_rows, width), F32), jax.ShapeDtypeStruct((CONV_WIDTH, width), F32),
                   jax.ShapeDtypeStruct((1, width), F32)],
        scratch_shapes=[pltpu.VMEM((tr + 8, tc), F32), pltpu.VMEM((tr + 8, tc), F32)],
        compiler_params=_params(("parallel", "arbitrary")),
    )(dxc, dxc, proj, proj, conv_w)


def _s5_disc_math(lam_re, lam_im, log_dt, b_re, b_im, expand):
    dt = jnp.exp(log_dt)
    zr, zi = lam_re * dt, lam_im * dt
    mag = jnp.exp(zr)
    lbr, lbi = mag * jnp.cos(zi), mag * jnp.sin(zi)
    ar, ai = lbr - 1.0, lbi
    den = lam_re * lam_re + lam_im * lam_im
    cr = (ar * lam_re + ai * lam_im) / den
    ci = (ai * lam_re - ar * lam_im) / den
    cre = jnp.dot(cr, expand, precision=lax.Precision.HIGHEST, preferred_element_type=F32)
    cie = jnp.dot(ci, expand, precision=lax.Precision.HIGHEST, preferred_element_type=F32)
    return lbr, lbi, cre * b_re - cie * b_im, cre * b_im + cie * b_re


def _s5_disc(name, lam_re, lam_im, log_dt, b_re, b_im, expand, cots=None):
    ins = [lam_re, lam_im, log_dt, b_re, b_im, expand]
    n_in = len(ins) + (len(cots) if cots else 0)

    def body(*refs):
        vals = [r[...] for r in refs[:6]]
        outs = refs[n_in:]
        if cots is None:
            res = _s5_disc_math(*vals)
        else:
            cv = tuple(r[...] for r in refs[6:n_in])
            _, vjp = jax.vjp(lambda a, b, c, d, e: _s5_disc_math(a, b, c, d, e, vals[5]), *vals[:5])
            res = vjp(cv)
        for o, r in zip(outs, res):
            o[...] = r

    if cots is None:
        shapes = [lam_re.shape, lam_re.shape, b_re.shape, b_re.shape]
    else:
        shapes = [lam_re.shape, lam_re.shape, log_dt.shape, b_re.shape, b_re.shape]
    vmem = pl.BlockSpec(memory_space=pltpu.VMEM)
    return pl.pallas_call(
        body, name=name, in_specs=[vmem] * n_in, out_specs=[vmem] * len(shapes),
        out_shape=[jax.ShapeDtypeStruct(s, F32) for s in shapes],
        compiler_params=pltpu.CompilerParams(vmem_limit_bytes=VMEM_LIMIT),
    )(*ins, *(cots or ()))


def _cmul(ar, ai, br, bi):
    return ar * br - ai * bi, ar * bi + ai * br


def _cscan(br, bi, lr, li, row, tr, reverse):
    pr, pi = lr, li
    sh = 1
    while sh < tr:
        if reverse:
            valid = row < tr - sh
            rs, is_ = pltpu.roll(br, tr - sh, 0), pltpu.roll(bi, tr - sh, 0)
        else:
            valid = row >= sh
            rs, is_ = pltpu.roll(br, sh, 0), pltpu.roll(bi, sh, 0)
        mr, mi = _cmul(pr, pi, rs, is_)
        br = br + jnp.where(valid, mr, 0.0)
        bi = bi + jnp.where(valid, mi, 0.0)
        pr, pi = _cmul(pr, pi, pr, pi)
        sh *= 2
    return br, bi


def _s5_fwd(name, proj, col0, bblk, cblk, lamblk, dvec, width, tr):
    t_rows = proj.shape[0]
    ni = t_rows // tr
    nb = width // LANES
    sw = bblk.shape[-1]
    hw = sw // 2

    def body(u_ref, b_ref, c_ref, lam_ref, d_ref, y_ref, s_ref, carry_ref, pow_ref):
        i = pl.program_id(1)
        row = lax.broadcasted_iota(jnp.int32, (tr, hw), 0)
        lam = lam_ref[...]
        lr, li = lam[:, :hw], lam[:, hw:]

        @pl.when(i == 0)
        def _():
            carry_ref[...] = jnp.zeros_like(carry_ref)
            zr, zi = _cscan(jnp.broadcast_to(lr, (tr, hw)) * jnp.where(row == 0, 1.0, 0.0),
                            jnp.broadcast_to(li, (tr, hw)) * jnp.where(row == 0, 1.0, 0.0), lr, li, row, tr, False)
            pow_ref[:, :hw] = zr
            pow_ref[:, hw:] = zi

        u = u_ref[...]
        bu = jnp.dot(u.astype(BF16), b_ref[...], preferred_element_type=F32)
        sr, si = _cscan(bu[:, :hw], bu[:, hw:], lr, li, row, tr, False)
        cr, ci = carry_ref[0:1, :hw], carry_ref[0:1, hw:]
        kr, ki = _cmul(pow_ref[:, :hw], pow_ref[:, hw:], cr, ci)
        sr, si = sr + kr, si + ki
        s = jnp.concatenate([sr, si], axis=1)
        carry_ref[...] = jnp.broadcast_to(s[tr - 1:tr, :], carry_ref.shape)
        s16 = s.astype(BF16)
        s_ref[...] = s16
        y_ref[...] = jnp.dot(s16, c_ref[...], preferred_element_type=F32) + d_ref[...] * u

    return pl.pallas_call(
        body, name=name, grid=(nb, ni),
        in_specs=[pl.BlockSpec((tr, LANES), lambda k, i: (i, col0 + k)),
                  pl.BlockSpec((None, LANES, sw), lambda k, i: (k, 0, 0)),
                  pl.BlockSpec((None, sw, LANES), lambda k, i: (k, 0, 0)),
                  pl.BlockSpec((None, 1, sw), lambda k, i: (k, 0, 0)),
                  pl.BlockSpec((1, LANES), lambda k, i: (0, k))],
        out_specs=[pl.BlockSpec((tr, LANES), lambda k, i: (i, k)),
                   pl.BlockSpec((tr, sw), lambda k, i: (i, k))],
        out_shape=[jax.ShapeDtypeStruct((t_rows, width), F32), jax.ShapeDtypeStruct((t_rows, nb * sw), BF16)],
        scratch_shapes=[pltpu.VMEM((8, sw), F32), pltpu.VMEM((tr, sw), F32)],
        compiler_params=_params(("parallel", "arbitrary")),
    )(proj, bblk, cblk, lamblk, dvec)


def _s5_bwd(name, dy, proj, col0, states, bblk, cblk, lamblk, dvec, width, tr):
    t_rows = dy.shape[0]
    ni = t_rows // tr
    nb = width // LANES
    sw = bblk.shape[-1]
    hw = sw // 2
    hb16 = tr // 16

    def rmap(k, i):
        return (ni - 1 - i, k)

    def body(dy_ref, u_ref, s_ref, sprev_ref, b_ref, c_ref, lam_ref, d_ref,
             du_ref, db_ref, dc_ref, dlam_ref, dd_ref, carry_ref, pow_ref):
        i = pl.program_id(1)
        row = lax.broadcasted_iota(jnp.int32, (tr, hw), 0)
        row_w = lax.broadcasted_iota(jnp.int32, (tr, sw), 0)
        lam = lam_ref[...]
        lr, li = lam[:, :hw], -lam[:, hw:]

        @pl.when(i == 0)
        def _():
            carry_ref[...] = jnp.zeros_like(carry_ref)
            zr, zi = _cscan(jnp.broadcast_to(lr, (tr, hw)) * jnp.where(row == tr - 1, 1.0, 0.0),
                            jnp.broadcast_to(li, (tr, hw)) * jnp.where(row == tr - 1, 1.0, 0.0), lr, li, row, tr, True)
            pow_ref[:, :hw] = zr
            pow_ref[:, hw:] = zi

        dyv = dy_ref[...]
        dy16 = dyv.astype(BF16)
        u = u_ref[...]
        gd = lax.dot_general(dy16, c_ref[...], _DN['nt'], preferred_element_type=F32)
        gr, gi = _cscan(gd[:, :hw], gd[:, hw:], lr, li, row, tr, True)
        cr, ci = carry_ref[0:1, :hw], carry_ref[0:1, hw:]
        kr, ki = _cmul(pow_ref[:, :hw], pow_ref[:, hw:], cr, ci)
        gr, gi = gr + kr, gi + ki
        g = jnp.concatenate([gr, gi], axis=1)
        carry_ref[...] = jnp.broadcast_to(g[0:1, :], carry_ref.shape)
        g16 = g.astype(BF16)
        du_ref[...] = lax.dot_general(g16, b_ref[...], _DN['nt'], preferred_element_type=F32) + d_ref[...] * dyv
        s16 = s_ref[...]
        dbv = lax.dot_general(g16, u.astype(BF16), _DN['tn'], preferred_element_type=F32)
        dcv = lax.dot_general(s16, dy16, _DN['tn'], preferred_element_type=F32)
        s32 = s16.astype(F32)
        first = jnp.where(i < ni - 1, sprev_ref[15:16, :].astype(F32), 0.0)
        sp = jnp.where(row_w == 0, jnp.broadcast_to(first, (tr, sw)), pltpu.roll(s32, 1, 0))
        spr, spi = sp[:, :hw], sp[:, hw:]
        dlr = jnp.sum(gr * spr + gi * spi, axis=0, keepdims=True)
        dli = jnp.sum(gi * spr - gr * spi, axis=0, keepdims=True)
        dlam = jnp.concatenate([dlr, dli], axis=1)
        ddv = jnp.sum(dyv * u, axis=0, keepdims=True)

        @pl.when(i == 0)
        def _():
            db_ref[...] = dbv
            dc_ref[...] = dcv
            dlam_ref[...] = dlam
            dd_ref[...] = ddv

        @pl.when(i > 0)
        def _():
            db_ref[...] += dbv
            dc_ref[...] += dcv
            dlam_ref[...] += dlam
            dd_ref[...] += ddv

    return pl.pallas_call(
        body, name=name, grid=(nb, ni),
        in_specs=[pl.BlockSpec((tr, LANES), rmap),
                  pl.BlockSpec((tr, LANES), lambda k, i: (ni - 1 - i, col0 + k)),
                  pl.BlockSpec((tr, sw), rmap),
                  pl.BlockSpec((16, sw), lambda k, i: (jnp.maximum((ni - 1 - i) * hb16 - 1, 0), k)),
                  pl.BlockSpec((None, LANES, sw), lambda k, i: (k, 0, 0)),
                  pl.BlockSpec((None, sw, LANES), lambda k, i: (k, 0, 0)),
                  pl.BlockSpec((None, 1, sw), lambda k, i: (k, 0, 0)),
                  pl.BlockSpec((1, LANES), lambda k, i: (0, k))],
        out_specs=[pl.BlockSpec((tr, LANES), rmap),
                   pl.BlockSpec((None, sw, LANES), lambda k, i: (k, 0, 0)),
                   pl.BlockSpec((None, sw, LANES), lambda k, i: (k, 0, 0)),
                   pl.BlockSpec((None, 1, sw), lambda k, i: (k, 0, 0)),
                   pl.BlockSpec((1, LANES), lambda k, i: (0, k))],
        out_shape=[jax.ShapeDtypeStruct((t_rows, width), F32), jax.ShapeDtypeStruct((nb, sw, LANES), F32),
                   jax.ShapeDtypeStruct((nb, sw, LANES), F32), jax.ShapeDtypeStruct((nb, 1, sw), F32),
                   jax.ShapeDtypeStruct((1, width), F32)],
        scratch_shapes=[pltpu.VMEM((8, sw), F32), pltpu.VMEM((tr, sw), F32)],
        compiler_params=_params(("parallel", "arbitrary")),
    )(dy, proj, states, states, bblk, cblk, lamblk, dvec)


def _blockdiag(m):
    nb, g, p, q = m.shape
    eye = jnp.eye(g, dtype=m.dtype)
    return (m[:, :, :, None, :] * eye[None, :, None, :, None]).reshape(nb, g * p, g * q)


def _blockdiag_take(m, p, q):
    nb = m.shape[0]
    g = GROUPS_PER_BLOCK
    m = m.reshape(nb, g, p, g, q)
    return jnp.stack([m[:, k, :, k, :] for k in range(g)], axis=1)


def _adamw(name, w, g, m, v, tr):
    rows, cols = w.shape
    gcols = g.shape[1]
    c1 = 1.0 - ADAM_B1 ** ADAM_STEP
    c2 = 1.0 - ADAM_B2 ** ADAM_STEP

    def body(w_ref, g_ref, m_ref, v_ref, go_ref, d_ref, mo_ref, vo_ref):
        gv = g_ref[...] if gcols == cols else g_ref[:, :cols]
        mn = ADAM_B1 * m_ref[...] + (1.0 - ADAM_B1) * gv
        vn = ADAM_B2 * v_ref[...] + (1.0 - ADAM_B2) * (gv * gv)
        go_ref[...] = gv
        mo_ref[...] = mn
        vo_ref[...] = vn
        d_ref[...] = -ADAM_LR * ((mn / c1) / (jnp.sqrt(vn / c2) + ADAM_EPS) + ADAM_WD * w_ref[...])

    spec = pl.BlockSpec((tr, cols), lambda i: (i, 0))
    return pl.pallas_call(
        body, name=name, grid=(rows // tr,),
        in_specs=[spec, pl.BlockSpec((tr, gcols), lambda i: (i, 0)), spec, spec],
        out_specs=[spec] * 4, out_shape=[jax.ShapeDtypeStruct((rows, cols), F32)] * 4,
        compiler_params=_params(("parallel",)),
    )(w, g, m, v)


def _ffn_fwd(tag, h, norm, wg, wu, wd, dims, host=None):
    t_rows, d, fp, tm, tr, trn = dims['T'], dims['D'], dims['FP'], dims['TM'], dims['TR'], dims['TRN']
    host = host or {}
    got = {}
    ft = N_CHIPS * fp
    tn = _tile(fp, 1408)
    npb = fp // tn
    n16, = _rowmap(f"{tag}_norm", lambda x, g: _rms(x, g).astype(BF16), (1, t_rows // trn),
                   [_rows(h, trn), _full(norm)], [_o((t_rows, d), BF16, trn)])

    def up(nm, w):
        res = _matmul(f"{tag}_{nm}", 'nt', n16, (tm, d // 2), lambda i, j, k: (i, k),
                      w, (None, tn, d // 2), lambda i, j, k: (j // npb, j % npb, k),
                      (t_rows // tm, ft // tn, 2), [((t_rows, ft), BF16, (tm, tn), lambda i, j: (i, j))],
                      lambda acc: acc, comms=host.get(nm, ()))
        got[nm] = res[1:]
        return res[0]

    gate = up("gate", wg)
    wu = got['gate'][wu] if isinstance(wu, int) else wu
    upv = up("up", wu)
    wd = got['up'][wd] if isinstance(wd, int) else wd
    tc = _tile(ft, 1408)
    act, = _rowmap(f"{tag}_act", lambda g, u: _silu(g.astype(F32)) * u.astype(F32), (ft // tc, t_rows // tr),
                   [_rows(gate, tr, tc), _rows(upv, tr, tc)], [_o((t_rows, ft), BF16, tr, tc)])
    tnd = _tile(d, 1024)
    res = _matmul(f"{tag}_down", 'nn', act, (tm, fp), lambda i, j, k: (i, k),
                  wd, (None, fp, tnd), lambda i, j, k: (k, 0, j),
                  (t_rows // tm, d // tnd, N_CHIPS), [((t_rows, d), F32, (tm, tnd), lambda i, j: (i, j))],
                  lambda acc, hin: hin + 0.5 * acc, extras=[(h, (tm, tnd), lambda i, j: (i, j))],
                  comms=host.get('down', ()))
    got['down'] = res[1:]
    return res[0], dict(h=h, n16=n16, gate=gate, up=upv, act=act, wu=wu, wd=wd), got


def _pair(nm, g, pos):
    return _pair_sum(f"pair_{nm}", g, _run_comm(f"swap_{nm}", _swap_comm(g))[0], pos)


def _ffn_bwd(tag, names, dh, dh16, saved, norm, wg, dims, pos, host=None):
    t_rows, d, fp, tm, trn = dims['T'], dims['D'], dims['FP'], dims['TM'], dims['TRN']
    host = host or {}
    got = {}
    n_gate, n_up, n_down = names
    wu, wd = saved['wu'], saved['wd']
    ft = N_CHIPS * fp
    tn = _tile(fp, 1408)
    npb = fp // tn
    tm2 = tm // 2
    tk = t_rows // 2

    def act_bwd(acc, g, u):
        da = 0.5 * acc
        g32, u32 = g.astype(F32), u.astype(F32)
        sg = _sigmoid(g32)
        return da * u32 * (sg * (1.0 + g32 * (1.0 - sg))), da * (g32 * sg)

    res = _matmul(
        f"{tag}_dact", 'nt', dh16, (tm2, d // 2), lambda i, j, k: (i, k),
        wd, (None, tn, d // 2), lambda i, j, k: (j // npb, j % npb, k),
        (t_rows // tm2, ft // tn, 2),
        [((t_rows, ft), BF16, (tm2, tn), lambda i, j: (i, j))] * 2, act_bwd,
        extras=[(saved['gate'], (tm2, tn), lambda i, j: (i, j)), (saved['up'], (tm2, tn), lambda i, j: (i, j))],
        comms=host.get('dact', ()))
    dgate, dup, got['dact'] = res[0], res[1], res[2:]
    tnd = _tile(d, 1024)
    res = _matmul(f"{tag}_dwd", 'tn', saved['act'], (tk, tn), lambda i, j, k: (k, i),
                  dh16, (tk, tnd), lambda i, j, k: (k, j),
                  (ft // tn, d // tnd, 2),
                  [((N_CHIPS, fp, d), BF16, (None, tn, tnd), lambda i, j: (i // npb, i % npb, j))],
                  lambda acc: 0.5 * acc, comms=host.get('dwd', ()))
    dwd, got['dwd'] = res[0], res[1:]

    def dw_up(nm, dact, comms):
        return _matmul(f"{tag}_{nm}", 'tn', dact, (tk, tn), lambda i, j, k: (k, i),
                       saved['n16'], (tk, tnd), lambda i, j, k: (k, j),
                       (ft // tn, d // tnd, 2),
                       [((N_CHIPS, fp, d), BF16, (None, tn, tnd), lambda i, j: (i // npb, i % npb, j))],
                       lambda acc: acc, comms=comms)

    def dn_part(nm, dact, w, prev, comms):
        extras = [] if prev is None else [(prev, (tm, tnd), lambda i, j: (i, j))]
        return _matmul(f"{tag}_{nm}", 'nn', dact, (tm, fp), lambda i, j, k: (i, k),
                       w, (None, fp, tnd), lambda i, j, k: (k, 0, j),
                       (t_rows // tm, d // tnd, N_CHIPS), [((t_rows, d), F32, (tm, tnd), lambda i, j: (i, j))],
                       (lambda acc: acc) if prev is None else (lambda acc, p: acc + p), extras=extras, comms=comms)

    dwg, swap_d = dw_up("dwg", dgate, [_swap_comm(dwd)])
    pair_d = _pair_sum(f"pair_{n_down}", dwd, swap_d, pos)
    dwu, land_d, swap_g = dw_up("dwu", dup, [_scatter_comm(pair_d), _swap_comm(dwg)])
    pair_g = _pair_sum(f"pair_{n_gate}", dwg, swap_g, pos)
    dn_g, land_g, swap_u = dn_part("dn_gate", dgate, wg, None, [_scatter_comm(pair_g), _swap_comm(dwu)])
    pair_u = _pair_sum(f"pair_{n_up}", dwu, swap_u, pos)
    dn, land_u = dn_part("dn_up", dup, wu, dn_g, [_scatter_comm(pair_u)])

    def norm_bwd(x, dnv, dhv, g):
        _, vjp = jax.vjp(_rms, x, g)
        dx, dg = vjp(dnv)
        res = dhv + dx
        return res, res.astype(BF16), dg

    dh_in, dh_in16, dnorm = _rowmap(
        f"{tag}_dnorm", norm_bwd, (1, t_rows // trn),
        [_rows(saved['h'], trn), _rows(dn, trn), _rows(dh, trn), _full(norm)],
        [_o((t_rows, d), F32, trn), _o((t_rows, d), BF16, trn)], [_acc((1, d))])
    parts = {n_down: (pair_d, land_d), n_gate: (pair_g, land_g), n_up: (pair_u, land_u)}
    return dh_in, dh_in16, dnorm, parts, got


def kernel(x, meta_tokens, ffn1_norm, ffn1_w_gate, ffn1_w_up, ffn1_w_down, mix_norm, w_in, rg_conv_w, rg_conv_b, rg_w_a, rg_b_a, rg_w_x, rg_b_x, rg_lambda, s5_lambda_re, s5_lambda_im, s5_log_dt, s5_b_re, s5_b_im, s5_c_re, s5_c_im, s5_d, s5_glu_w, s5_glu_b, rg_out_norm, s5_out_norm, w_out, ffn2_norm, ffn2_w_gate, ffn2_w_up, ffn2_w_down, final_norm, loss_target, m_meta_tokens, m_ffn1_norm, m_ffn1_w_gate, m_ffn1_w_up, m_ffn1_w_down, m_mix_norm, m_w_in, m_rg_conv_w, m_rg_conv_b, m_rg_w_a, m_rg_b_a, m_rg_w_x, m_rg_b_x, m_rg_lambda, m_s5_lambda_re, m_s5_lambda_im, m_s5_log_dt, m_s5_b_re, m_s5_b_im, m_s5_c_re, m_s5_c_im, m_s5_d, m_s5_glu_w, m_s5_glu_b, m_rg_out_norm, m_s5_out_norm, m_w_out, m_ffn2_norm, m_ffn2_w_gate, m_ffn2_w_up, m_ffn2_w_down, m_final_norm, v_meta_tokens, v_ffn1_norm, v_ffn1_w_gate, v_ffn1_w_up, v_ffn1_w_down, v_mix_norm, v_w_in, v_rg_conv_w, v_rg_conv_b, v_rg_w_a, v_rg_b_a, v_rg_w_x, v_rg_b_x, v_rg_lambda, v_s5_lambda_re, v_s5_lambda_im, v_s5_log_dt, v_s5_b_re, v_s5_b_im, v_s5_c_re, v_s5_c_im, v_s5_d, v_s5_glu_w, v_s5_glu_b, v_rg_out_norm, v_s5_out_norm, v_w_out, v_ffn2_norm, v_ffn2_w_gate, v_ffn2_w_up, v_ffn2_w_down, v_final_norm):
    args = locals()
    w = {n: args[n] for n in WEIGHTS}
    mom = {n: args["m_" + n] for n in WEIGHTS}
    var = {n: args["v_" + n] for n in WEIGHTS}

    seq, d = x.shape[1], x.shape[2]
    f_shard = ffn1_w_gate.shape[2]
    fp = _round_up(f_shard, LANES)
    r = rg_conv_b.shape[1]
    s5w = s5_d.shape[1]
    hd = r // RG_HEADS
    groups = s5w // S5_GROUP
    nb = groups // GROUPS_PER_BLOCK
    t_rows = _round_up(N_META + seq, LANES)
    tm, tr, tk = t_rows // 4, t_rows // 8, t_rows // 2
    trn = _tile(t_rows, max(t_rows // 32, 16), 16)
    dims = dict(T=t_rows, D=d, FP=fp, TM=tm, TR=tr, TRN=trn)
    cx, cy, cc = lax.axis_index("x"), lax.axis_index("y"), lax.axis_index("c")
    chip = 2 * cx + cy
    pos = jnp.stack([chip, cc]).astype(jnp.int32)

    def gather(nm, buf):
        return _run_comm(f"ag_{nm}", _gather_comm(buf))[0]

    def gather_f32(nm, local):
        buf = lax.dynamic_update_slice(jnp.zeros((N_CHIPS,) + local.shape, F32), local[None], (chip, 0, 0))
        return gather(nm, buf)

    slot = {}
    transposed = ('ffn1_w_gate', 'ffn1_w_up', 'ffn2_w_gate', 'ffn2_w_up')
    local2d = lambda tree, nm: jnp.swapaxes(tree[nm][0], 0, 1) if nm in transposed else tree[nm][0]
    for nm in transposed:
        slot[nm] = _cast_pad(f"cast_{nm}", local2d(w, nm), fp, d, pos)
    for nm in ('ffn1_w_down', 'ffn2_w_down'):
        slot[nm] = _cast_pad(f"cast_{nm}", w[nm][0], fp, d, pos)
    slot['w_in'] = _cast_pad("cast_w_in", w_in[0], d, w_in.shape[2], pos)
    slot['w_out'] = _cast_pad("cast_w_out", w_out[0], w_out.shape[1], d, pos)
    slot['s5_glu_w'] = _cast_pad("cast_glu", s5_glu_w[0], s5_glu_w.shape[1], s5w, pos)
    rgw_local = jnp.concatenate([rg_w_a[0], rg_w_x[0]], axis=0).reshape(2 * RG_HEADS * (hd // N_CHIPS), hd)
    rgw = gather('rgw', _cast_pad("cast_rgw", rgw_local, rgw_local.shape[0], hd, pos))
    rgw = rgw.reshape(N_CHIPS, 2, RG_HEADS, hd // N_CHIPS, hd).transpose(1, 2, 0, 3, 4).reshape(2, RG_HEADS, hd, hd)
    w_a16, w_x16 = rgw[0], rgw[1]
    meta_full = gather_f32('meta', meta_tokens).transpose(1, 0, 2).reshape(N_META, d)
    conv_w2, b_a2, b_x2 = _gather_small_shards(rg_conv_w[0], rg_b_a[0], rg_b_x[0], gather_f32)
    wfull = {'ffn1_w_gate': gather('ffn1_w_gate', slot['ffn1_w_gate'])}

    pad_rows = t_rows - N_META - seq
    h0 = jnp.concatenate([meta_full, x[0], jnp.zeros((pad_rows, d), F32)], axis=0)
    tgt = jnp.concatenate([jnp.zeros((N_META, d), F32), loss_target[0], jnp.zeros((pad_rows, d), F32)], axis=0)

    h1, sv1, got = _ffn_fwd(
        "ffn1", h0, ffn1_norm, wfull['ffn1_w_gate'], 0, 0, dims,
        host={'gate': [_gather_comm(slot['ffn1_w_up'])], 'up': [_gather_comm(slot['ffn1_w_down'])],
              'down': [_gather_comm(slot['w_in']), _gather_comm(slot['s5_glu_w'])]})
    wfull['w_in'] = got['down'][0]
    wfull['s5_glu_w'] = got['down'][1].reshape(s5w, s5w)

    n2, = _rowmap("mix_norm", lambda xv, g: _rms(xv, g).astype(BF16), (1, t_rows // trn),
                  [_rows(h1, trn), _full(mix_norm)], [_o((t_rows, d), BF16, trn)])
    pw = w_in.shape[2]
    tnp = _tile(pw, 1536)
    nppb = pw // tnp
    proj, w_out_full = _matmul(
        "in_proj", 'nn', n2, (tm, d // 2), lambda i, j, k: (i, k),
        wfull['w_in'], (None, d // 2, tnp), lambda i, j, k: (j // nppb, k, j % nppb),
        (t_rows // tm, N_CHIPS * nppb, 2), [((t_rows, 3 * r), F32, (tm, tnp), lambda i, j: (i, j))],
        lambda acc: acc, comms=[_gather_comm(slot['w_out'])])
    wfull['w_out'] = w_out_full.reshape(d, d)

    xc = _conv_fwd("rg_conv", proj, conv_w2, rg_conv_b, r, tr)

    def head_cols(arr):
        return (arr, (tr, hd), lambda j, i: (i, j))

    def head_w(arr):
        return (arr, (None, hd, hd), lambda j, i: (j, 0, 0))

    def head_vec(arr):
        return (arr, (1, hd), lambda j, i: (0, j))

    def gates_fwd(xcv, wa, wx, ba, bx, lam):
        x16 = xcv.astype(BF16)
        pre_r = jnp.dot(x16, wa, preferred_element_type=F32) + ba
        pre_i = jnp.dot(x16, wx, preferred_element_type=F32) + bx
        return _rg_gate_math(xcv, pre_r, pre_i, _softplus_neg(lam))

    gate_params = [head_w(w_a16), head_w(w_x16), head_vec(b_a2), head_vec(b_x2), head_vec(rg_lambda)]
    head_out = ((t_rows, r), F32, (tr, hd), lambda j, i: (i, j))
    a_dec, bxv = _rowmap("rg_gates", gates_fwd, (RG_HEADS, t_rows // tr), [head_cols(xc)] + gate_params,
                         [head_out, head_out])
    h_rg = _linear_scan("rg_scan", a_dec, bxv, tr)

    expand = jnp.repeat(jnp.eye(S5_STATE, dtype=F32), S5_GROUP, axis=1)
    b_re2 = s5_b_re[0].reshape(groups, S5_STATE * S5_GROUP)
    b_im2 = s5_b_im[0].reshape(groups, S5_STATE * S5_GROUP)
    log_dt2 = s5_log_dt[0].reshape(groups, 1)
    lbr, lbi, bbr, bbi = _s5_disc("s5_disc", s5_lambda_re[0], s5_lambda_im[0], log_dt2, b_re2, b_im2, expand)

    def to_bblk(m):
        return _blockdiag(m.reshape(nb, GROUPS_PER_BLOCK, S5_STATE, S5_GROUP).transpose(0, 1, 3, 2))

    def to_cblk(m):
        return _blockdiag(m.reshape(nb, GROUPS_PER_BLOCK, S5_GROUP, S5_STATE).transpose(0, 1, 3, 2))

    bblk = jnp.concatenate([to_bblk(bbr), to_bblk(bbi)], axis=-1).astype(BF16)
    cblk = jnp.concatenate([to_cblk(s5_c_re[0]), -to_cblk(s5_c_im[0])], axis=-2).astype(BF16)
    hw = GROUPS_PER_BLOCK * S5_STATE
    lamblk = jnp.concatenate([lbr.reshape(nb, 1, hw), lbi.reshape(nb, 1, hw)], axis=-1)
    col0 = 2 * r // LANES
    y_pre, states = _s5_fwd("s5_fwd", proj, col0, bblk, cblk, lamblk, s5_d, s5w, tr)

    z16, = _rowmap("s5_gelu", lambda yv: _gelu(yv).astype(BF16), (1, t_rows // trn),
                   [_rows(y_pre, trn)], [_o((t_rows, s5w), BF16, trn)])
    tns = _tile(s5w, 1024)

    def glu_fwd(acc, yv, b):
        gl = acc + b
        return _gelu(yv) * _sigmoid(gl), gl

    y_s5, gl = _matmul("s5_glu", 'nn', z16, (tm, s5w), lambda i, j, k: (i, 0),
                       wfull['s5_glu_w'], (s5w, tns), lambda i, j, k: (0, j),
                       (t_rows // tm, s5w // tns, 1), [((t_rows, s5w), F32, (tm, tns), lambda i, j: (i, j))] * 2,
                       glu_fwd, extras=[(y_pre, (tm, tns), lambda i, j: (i, j)), (s5_glu_b, (1, tns), lambda i, j: (0, j))])

    def mix_out(hv, gv, yv, g1, g2):
        return jnp.concatenate([_rms(hv * _gelu(gv), g1), _rms(yv, g2)], axis=1)

    gate_in = (proj, (trn, r), lambda j, i: (i, 1))
    ycat16, = _rowmap("mix_out", lambda *a: mix_out(*a).astype(BF16), (1, t_rows // trn),
                      [_rows(h_rg, trn), gate_in, _rows(y_s5, trn), _full(rg_out_norm), _full(s5_out_norm)],
                      [_o((t_rows, r + s5w), BF16, trn)])
    tnd = _tile(d, 1024)
    h2, wfull['ffn2_w_gate'] = _matmul(
        "out_proj", 'nn', ycat16, (tm, (r + s5w) // 2), lambda i, j, k: (i, k),
        wfull['w_out'], ((r + s5w) // 2, tnd), lambda i, j, k: (k, j),
        (t_rows // tm, d // tnd, 2), [((t_rows, d), F32, (tm, tnd), lambda i, j: (i, j))],
        lambda acc, hin: hin + acc, extras=[(h1, (tm, tnd), lambda i, j: (i, j))],
        comms=[_gather_comm(slot['ffn2_w_gate'])])

    h3, sv2, _ = _ffn_fwd(
        "ffn2", h2, ffn2_norm, wfull['ffn2_w_gate'], 0, 0, dims,
        host={'gate': [_gather_comm(slot['ffn2_w_up'])], 'up': [_gather_comm(slot['ffn2_w_down'])]})

    fnorm2 = final_norm.reshape(1, d)

    def head(xv, tv, g):
        i = pl.program_id(1)
        rowi = lax.broadcasted_iota(jnp.int32, (trn, 1), 0) + i * trn
        mask = jnp.where((rowi >= N_META) & (rowi < N_META + seq), 1.0, 0.0)
        out, vjp = jax.vjp(_rms, xv, g)
        err = (out - tv) * mask
        dx, dg = vjp(err * (1.0 / d))
        return dx, dx.astype(BF16), jnp.sum(err * err, axis=0, keepdims=True), dg

    dh3, dh3_16, loss_cols, d_final = _rowmap(
        "loss_head", head, (1, t_rows // trn), [_rows(h3, trn), _rows(tgt, trn), _full(fnorm2)],
        [_o((t_rows, d), F32, trn), _o((t_rows, d), BF16, trn)], [_acc((1, d)), _acc((1, d))])
    loss = lax.psum(0.5 * jnp.sum(loss_cols) / d, ("x", "y", "c"))

    small = {}
    dh2, dh2_16, small['ffn2_norm'], parts, _ = _ffn_bwd(
        "ffn2b", ('ffn2_w_gate', 'ffn2_w_up', 'ffn2_w_down'), dh3, dh3_16, sv2, ffn2_norm, wfull['ffn2_w_gate'],
        dims, pos)

    dycat, = _matmul("d_out_proj", 'nt', dh2_16, (tm, d // 2), lambda i, j, k: (i, k),
                     wfull['w_out'], (tnd, d // 2), lambda i, j, k: (j, k),
                     (t_rows // tm, (r + s5w) // tnd, 2), [((t_rows, r + s5w), F32, (tm, tnd), lambda i, j: (i, j))],
                     lambda acc: acc)
    wo_rows = (r + s5w) // N_CHIPS
    tno = _tile(wo_rows, 1024)
    npo = wo_rows // tno
    dw_out, = _matmul("dw_out", 'tn', ycat16, (tk, tno), lambda i, j, k: (k, i),
                      dh2_16, (tk, tnd), lambda i, j, k: (k, j),
                      ((r + s5w) // tno, d // tnd, 2),
                      [((N_CHIPS, wo_rows, d), BF16, (None, tno, tnd), lambda i, j: (i // npo, i % npo, j))],
                      lambda acc: acc)
    pair_w_out = _pair('w_out', dw_out, pos)

    def mix_out_bwd(hv, gv, yv, dyc, g1, g2):
        _, vjp = jax.vjp(mix_out, hv, gv, yv, g1, g2)
        return vjp(dyc)

    dh_out, dgate_rg, dy_s5, small['rg_out_norm'], small['s5_out_norm'] = _rowmap(
        "mix_out_bwd", mix_out_bwd, (1, t_rows // trn),
        [_rows(h_rg, trn), gate_in, _rows(y_s5, trn), _rows(dycat, trn), _full(rg_out_norm), _full(s5_out_norm)],
        [_o((t_rows, r), F32, trn), _o((t_rows, r), F32, trn), _o((t_rows, s5w), F32, trn)],
        [_acc((1, r)), _acc((1, s5w))])

    def glu_bwd(dyv, yv, glv):
        zv = _gelu(yv)
        sg = _sigmoid(glv)
        dgl = dyv * zv * sg * (1.0 - sg)
        return dyv * sg, dgl.astype(BF16), jnp.sum(dgl, axis=0, keepdims=True)

    dz_dir, dgl16, small['s5_glu_b'] = _rowmap(
        "s5_glu_bwd", glu_bwd, (1, t_rows // trn), [_rows(dy_s5, trn), _rows(y_pre, trn), _rows(gl, trn)],
        [_o((t_rows, s5w), F32, trn), _o((t_rows, s5w), BF16, trn)], [_acc((1, s5w))])

    def dgelu(acc, dzd, yv):
        _, vjp = jax.vjp(_gelu, yv)
        return vjp(acc + dzd)[0]

    dy_pre, = _matmul("s5_dz", 'nt', dgl16, (tm, s5w), lambda i, j, k: (i, 0),
                      wfull['s5_glu_w'], (tns, s5w), lambda i, j, k: (j, 0),
                      (t_rows // tm, s5w // tns, 1), [((t_rows, s5w), F32, (tm, tns), lambda i, j: (i, j))],
                      dgelu, extras=[(dz_dir, (tm, tns), lambda i, j: (i, j)), (y_pre, (tm, tns), lambda i, j: (i, j))])
    gl_rows = s5w // N_CHIPS
    tng = _tile(gl_rows, 1024)
    npg = gl_rows // tng
    dw_glu, = _matmul("dw_glu", 'tn', z16, (tk, tng), lambda i, j, k: (k, i),
                      dgl16, (tk, tns), lambda i, j, k: (k, j),
                      (s5w // tng, s5w // tns, 2),
                      [((N_CHIPS, gl_rows, s5w), BF16, (None, tng, tns), lambda i, j: (i // npg, i % npg, j))],
                      lambda acc: acc)
    pair_glu = _pair('s5_glu_w', dw_glu, pos)
    du_s5, dbblk, dcblk, dlamblk, small['s5_d'] = _s5_bwd(
        "s5_bwd", dy_pre, proj, col0, states, bblk, cblk, lamblk, s5_d, s5w, tr)

    def from_bblk(m):
        return _blockdiag_take(m, S5_STATE, S5_GROUP).reshape(groups, S5_STATE * S5_GROUP)

    dbbr, dbbi = from_bblk(dbblk[:, :hw]), from_bblk(dbblk[:, hw:])
    dcs = lambda m: _blockdiag_take(m, S5_STATE, S5_GROUP).transpose(0, 1, 3, 2).reshape(1, groups, S5_GROUP, S5_STATE)
    small['s5_c_re'], small['s5_c_im'] = dcs(dcblk[:, :hw]), -dcs(dcblk[:, hw:])
    dlbr, dlbi = dlamblk[:, 0, :hw].reshape(groups, S5_STATE), dlamblk[:, 0, hw:].reshape(groups, S5_STATE)
    g_lre, g_lim, g_ldt, g_bre, g_bim = _s5_disc(
        "s5_disc_bwd", s5_lambda_re[0], s5_lambda_im[0], log_dt2, b_re2, b_im2, expand, cots=(dlbr, dlbi, dbbr, dbbi))
    small['s5_lambda_re'], small['s5_lambda_im'], small['s5_log_dt'] = g_lre[None], g_lim[None], g_ldt.reshape(1, groups)
    small['s5_b_re'] = g_bre.reshape(s5_b_re.shape)
    small['s5_b_im'] = g_bim.reshape(s5_b_im.shape)

    dh_rg = _linear_scan("rg_scan_bwd", a_dec, dh_out, tr, reverse=True, shift_a=True)

    hb = tr // 8

    def gates_bwd(xcv, dhv, hv, hprev, wa, wx, ba, bx, lam):
        i = pl.program_id(1)
        row = lax.broadcasted_iota(jnp.int32, (tr, hd), 0)
        first = jnp.where(i > 0, hprev[7:8, :], 0.0)
        h_prev = jnp.where(row == 0, jnp.broadcast_to(first, (tr, hd)), pltpu.roll(hv, 1, 0))
        x16 = xcv.astype(BF16)
        pre_r = jnp.dot(x16, wa, preferred_element_type=F32) + ba
        pre_i = jnp.dot(x16, wx, preferred_element_type=F32) + bx
        _, vjp = jax.vjp(_rg_gate_math, xcv, pre_r, pre_i, _softplus_neg(lam))
        dxc_, dpr, dpi, dsp = vjp((dhv * h_prev, dhv))
        dpr16, dpi16 = dpr.astype(BF16), dpi.astype(BF16)
        dxc_ = (dxc_ + lax.dot_general(dpr16, wa, _DN['nt'], preferred_element_type=F32)
                + lax.dot_general(dpi16, wx, _DN['nt'], preferred_element_type=F32))
        dwa = lax.dot_general(x16, dpr16, _DN['tn'], preferred_element_type=F32)
        dwx = lax.dot_general(x16, dpi16, _DN['tn'], preferred_element_type=F32)
        dlam = dsp * (-_sigmoid(-lam))
        return (dxc_, dwa, dwx, jnp.sum(dpr, axis=0, keepdims=True), jnp.sum(dpi, axis=0, keepdims=True), dlam)

    head_acc_w = ((RG_HEADS, hd, hd), F32, (None, hd, hd), lambda j, i: (j, 0, 0))
    head_acc_v = ((1, r), F32, (1, hd), lambda j, i: (0, j))
    h_halo = (h_rg, (8, hd), lambda j, i: (jnp.maximum(i * hb - 1, 0), j))
    dxc, d_wa, d_wx, d_ba, d_bx, d_lam = _rowmap(
        "rg_gates_bwd", gates_bwd, (RG_HEADS, t_rows // tr),
        [head_cols(xc), head_cols(dh_rg), head_cols(h_rg), h_halo] + gate_params,
        [head_out], [head_acc_w, head_acc_w, head_acc_v, head_acc_v, head_acc_v])
    small['rg_w_a'], small['rg_w_x'] = d_wa[None], d_wx[None]
    small['rg_b_a'], small['rg_b_x'] = d_ba.reshape(1, RG_HEADS, hd), d_bx.reshape(1, RG_HEADS, hd)
    small['rg_lambda'] = d_lam
    du_rg, d_convw, small['rg_conv_b'] = _conv_bwd("rg_conv_bwd", dxc, proj, conv_w2, r, tr)
    small['rg_conv_w'] = d_convw[None]

    dproj16, = _rowmap("dproj", lambda a, b, c: jnp.concatenate([a, b, c], axis=1).astype(BF16), (1, t_rows // trn),
                       [_rows(du_rg, trn), _rows(dgate_rg, trn), _rows(du_s5, trn)], [_o((t_rows, 3 * r), BF16, trn)])
    dn2, land_w_out = _matmul(
        "d_in_proj", 'nt', dproj16, (tm, pw), lambda i, j, k: (i, k),
        wfull['w_in'], (None, tnd, pw), lambda i, j, k: (k, j, 0),
        (t_rows // tm, d // tnd, N_CHIPS), [((t_rows, d), F32, (tm, tnd), lambda i, j: (i, j))],
        lambda acc: acc, comms=[_scatter_comm(pair_w_out)])
    parts['w_out'] = (pair_w_out, land_w_out)
    dw_in, = _matmul("dw_in", 'tn', n2, (tk, tnd), lambda i, j, k: (k, i),
                     dproj16, (tk, tnp), lambda i, j, k: (k, j),
                     (d // tnd, N_CHIPS * nppb, 2),
                     [((N_CHIPS, d, pw), BF16, (None, tnd, tnp), lambda i, j: (j // nppb, i, j % nppb))],
                     lambda acc: acc)
    pair_w_in = _pair('w_in', dw_in, pos)

    def norm_bwd(xv, dnv, dhv, g):
        _, vjp = jax.vjp(_rms, xv, g)
        dx, dg = vjp(dnv)
        res = dhv + dx
        return res, res.astype(BF16), dg

    dh1, dh1_16, small['mix_norm'] = _rowmap(
        "mix_dnorm", norm_bwd, (1, t_rows // trn), [_rows(h1, trn), _rows(dn2, trn), _rows(dh2, trn), _full(mix_norm)],
        [_o((t_rows, d), F32, trn), _o((t_rows, d), BF16, trn)], [_acc((1, d))])

    dh0, _, small['ffn1_norm'], parts1, got = _ffn_bwd(
        "ffn1b", ('ffn1_w_gate', 'ffn1_w_up', 'ffn1_w_down'), dh1, dh1_16, sv1, ffn1_norm, wfull['ffn1_w_gate'],
        dims, pos, host={'dact': [_scatter_comm(pair_glu), _scatter_comm(pair_w_in)]})
    parts.update(parts1)
    parts['s5_glu_w'] = (pair_glu, got['dact'][0])
    parts['w_in'] = (pair_w_in, got['dact'][1])
    grad_x = dh0[N_META:N_META + seq][None]
    small['meta_tokens'] = dh0[:N_META]
    small['final_norm'] = d_final.reshape(d)

    halves = [_reduce_partials(f"red_{nm}", parts[nm][0], parts[nm][1], pos) for nm in BIG]
    wholes = _sibling_exchange("exchange", halves)
    gbig = {nm: wh.reshape(wh.shape[0] * wh.shape[1], wh.shape[2]) for nm, wh in zip(BIG, wholes)}

    small_names = [n for n in WEIGHTS if n not in BIG]
    full_shape = {n: small[n].shape for n in small_names}
    flat = jnp.concatenate([small[n].reshape(-1) for n in small_names])
    unit = 8 * 8 * LANES
    total = _round_up(flat.shape[0], unit)
    flat = jnp.concatenate([flat, jnp.zeros((total - flat.shape[0],), F32)])
    red = _all_reduce_small("ar_small", flat.reshape(total // LANES, LANES)).reshape(-1)
    gsmall = {}
    off = 0
    for n in small_names:
        size = math.prod(full_shape[n])
        gsmall[n] = _own_shard(n, red[off:off + size].reshape(full_shape[n]), w[n].shape, chip)
        off += size

    out_g, out_d, out_m, out_v = {}, {}, {}, {}
    for nm in BIG:
        w2 = local2d(w, nm)
        trw = _tile(w2.shape[0], 128 if w2.shape[1] < 4096 else 64, 8)
        res = _adamw(f"adamw_{nm}", w2, gbig[nm], local2d(mom, nm), local2d(var, nm), trw)
        back = (lambda a: jnp.swapaxes(a, 0, 1)[None]) if nm in transposed else (lambda a: a[None])
        out_g[nm], out_d[nm], out_m[nm], out_v[nm] = [back(a) for a in res]

    def pack(tree):
        fl = jnp.concatenate([tree[n].reshape(-1) for n in small_names])
        tot = _round_up(fl.shape[0], 8 * LANES)
        return jnp.concatenate([fl, jnp.zeros((tot - fl.shape[0],), F32)]).reshape(tot // LANES, LANES)

    wp, gp, mp, vp = pack(w), pack(gsmall), pack(mom), pack(var)
    res = _adamw("adamw_small", wp, gp, mp, vp, _tile(wp.shape[0], 512, 8))
    off = 0
    for n in small_names:
        size = math.prod(w[n].shape)
        for dst, src in zip((out_g, out_d, out_m, out_v), res):
            dst[n] = src.reshape(-1)[off:off + size].reshape(w[n].shape)
        off += size

    return (loss, grad_x, *[out_g[n] for n in WEIGHTS], *[out_d[n] for n in WEIGHTS],
            *[out_m[n] for n in WEIGHTS], *[out_v[n] for n in WEIGHTS])


def _gather_small_shards(conv_w, b_a, b_x, gather):
    cw = conv_w.shape[1]
    part = b_a.shape[1]

    def rows8(a):
        a = jnp.concatenate([a, jnp.zeros((a.shape[0], cw - a.shape[1]), F32)], axis=1)
        return jnp.concatenate([a, jnp.zeros((8 - a.shape[0], cw), F32)], axis=0) if a.shape[0] < 8 else a

    local = jnp.concatenate([rows8(conv_w), rows8(b_a), rows8(b_x), jnp.zeros((8, cw), F32)], axis=0)
    full = gather("rg_small", local)
    conv_full = full[:, :CONV_WIDTH].transpose(1, 0, 2).reshape(CONV_WIDTH, N_CHIPS * cw)
    bias = lambda k: full[:, 8 * k:8 * k + RG_HEADS, :part].transpose(1, 0, 2).reshape(1, RG_HEADS * N_CHIPS * part)
    return conv_full, bias(1), bias(2)


def _own_shard(name, g, local_shape, chip):
    if tuple(g.shape) == tuple(local_shape):
        return g
    axis = [k for k, (a, b) in enumerate(zip(g.shape, local_shape)) if a != b][0]
    size = local_shape[axis]
    return lax.dynamic_slice_in_dim(g, chip * size, size, axis=axis)
```

```python
import functools
import math

import jax
import jax.numpy as jnp
from jax import lax
from jax.experimental import pallas as pl
from jax.experimental.pallas import tpu as pltpu

F32 = jnp.float32
BF16 = jnp.bfloat16
MESH = pl.DeviceIdType.MESH

N_META = 16
RG_HEADS = 8
CONV_WIDTH = 4
RG_C = 8.0
S5_GROUP = 16
S5_STATE = 64
GROUPS_PER_BLOCK = 8
EPS = 1e-6
N_CHIPS = 4
LANES = 128
VMEM_LIMIT = 56 * 1024 * 1024

ADAM_LR = 0.001
ADAM_B1 = 0.9
ADAM_B2 = 0.999
ADAM_EPS = 1e-08
ADAM_WD = 0.01
ADAM_STEP = 10

WEIGHTS = ['meta_tokens', 'ffn1_norm', 'ffn1_w_gate', 'ffn1_w_up', 'ffn1_w_down', 'mix_norm', 'w_in', 'rg_conv_w',
           'rg_conv_b', 'rg_w_a', 'rg_b_a', 'rg_w_x', 'rg_b_x', 'rg_lambda', 's5_lambda_re', 's5_lambda_im',
           's5_log_dt', 's5_b_re', 's5_b_im', 's5_c_re', 's5_c_im', 's5_d', 's5_glu_w', 's5_glu_b', 'rg_out_norm',
           's5_out_norm', 'w_out', 'ffn2_norm', 'ffn2_w_gate', 'ffn2_w_up', 'ffn2_w_down', 'final_norm']
BIG = ('ffn1_w_gate', 'ffn1_w_up', 'ffn1_w_down', 'w_in', 's5_glu_w', 'w_out', 'ffn2_w_gate', 'ffn2_w_up',
       'ffn2_w_down')

_DN = {'nn': (((1,), (0,)), ((), ())), 'nt': (((1,), (1,)), ((), ())), 'tn': (((0,), (0,)), ((), ()))}


def _round_up(n, m):
    return (n + m - 1) // m * m


def _tile(n, pref, unit=LANES):
    best = None
    for t in range(unit, min(n, pref) + 1, unit):
        if n % t == 0:
            best = t
    return best if best is not None else n


def _params(sem=None):
    return pltpu.CompilerParams(dimension_semantics=sem, vmem_limit_bytes=VMEM_LIMIT)


def _rms(x, g):
    return x * lax.rsqrt(jnp.mean(x * x, axis=-1, keepdims=True) + EPS) * g


def _sigmoid(x):
    return 1.0 / (1.0 + jnp.exp(-x))


def _gelu(x):
    return 0.5 * x * (1.0 + jnp.tanh(math.sqrt(2.0 / math.pi) * (x + 0.044715 * (x * x * x))))


def _silu(x):
    return x * _sigmoid(x)


def _expm1(x):
    series = x * (1.0 + x * (1.0 / 2) * (1.0 + x * (1.0 / 3) * (1.0 + x * (1.0 / 4) * (1.0 + x * (1.0 / 5) * (1.0 + x * (1.0 / 6))))))
    return jnp.where(jnp.abs(x) < 0.3, series, jnp.exp(x) - 1.0)


def _softplus_neg(lam):
    m = jnp.maximum(-lam, 0.0)
    e = jnp.exp(-jnp.abs(lam))
    w = 1.0 + e
    log1p = jnp.where(w == 1.0, e, jnp.log(w) * (e / jnp.where(w == 1.0, 1.0, w - 1.0)))
    return m + log1p


def _rg_gate_math(xc, pre_r, pre_i, sp):
    r = _sigmoid(pre_r)
    i = _sigmoid(pre_i)
    log_a = -RG_C * r * sp
    a = jnp.exp(log_a)
    mult = jnp.sqrt(-_expm1(2.0 * log_a))
    return a, mult * i * xc


class _Comm:
    def __init__(self, arrays, out_shapes, aliased, sems, start, finish, mid=None):
        self.arrays, self.out_shapes, self.aliased, self.sems = arrays, out_shapes, aliased, sems
        self.start, self.finish, self.mid = start, finish, mid


def _matmul(name, mode, a, a_blk, a_map, b, b_blk, b_map, grid, outs, epilogue, extras=(), comms=()):
    ni, nj, nk = grid
    ne, no = len(extras), len(outs)
    sq = lambda blk: tuple(d for d in blk if d is not None)
    ab, bb = sq(a_blk), sq(b_blk)
    acc_shape = {'nn': (ab[0], bb[1]), 'nt': (ab[0], bb[0]), 'tn': (ab[1], bb[1])}[mode]
    n_cin = sum(len(cm.arrays) for cm in comms)
    n_cout = sum(len(cm.out_shapes) for cm in comms)
    n_acc = 1 if nk > 1 else 0

    def comm_refs(refs):
        cin = refs[2 + ne:2 + ne + n_cin]
        cout = refs[2 + ne + n_cin + no:2 + ne + n_cin + no + n_cout]
        csem = refs[2 + ne + n_cin + no + n_cout + n_acc:]
        for cm in comms:
            yield cm, cin[:len(cm.arrays)], cout[:len(cm.out_shapes)], csem[:len(cm.sems)]
            cin, cout, csem = cin[len(cm.arrays):], cout[len(cm.out_shapes):], csem[len(cm.sems):]

    def body(*refs):
        a_ref, b_ref = refs[0], refs[1]
        ex = refs[2:2 + ne]
        out = refs[2 + ne + n_cin:2 + ne + n_cin + no]
        if comms:
            @pl.when((pl.program_id(0) == 0) & (pl.program_id(1) == 0) & (pl.program_id(2) == 0))
            def _():
                for cm, cin, cout, csem in comm_refs(refs):
                    cm.start(cin, cout, csem)

            if any(cm.mid for cm in comms):
                step = (pl.program_id(0) * nj + pl.program_id(1)) * nk + pl.program_id(2)

                @pl.when(step == (3 * ni * nj * nk) // 5)
                def _():
                    for cm, cin, cout, csem in comm_refs(refs):
                        if cm.mid:
                            cm.mid(cin, cout, csem)

        part = lax.dot_general(a_ref[...], b_ref[...], _DN[mode], preferred_element_type=F32)

        def finish(acc):
            res = epilogue(acc, *[e[...] for e in ex])
            if not isinstance(res, tuple):
                res = (res,)
            for o, r in zip(out, res):
                o[...] = r.astype(o.dtype)

        if nk == 1:
            finish(part)
        else:
            acc_ref = refs[2 + ne + n_cin + no + n_cout]
            k = pl.program_id(2)

            @pl.when(k == 0)
            def _():
                acc_ref[...] = part

            @pl.when(k > 0)
            def _():
                acc_ref[...] += part

            @pl.when(k == nk - 1)
            def _():
                finish(acc_ref[...])

        if comms:
            @pl.when((pl.program_id(0) == ni - 1) & (pl.program_id(1) == nj - 1) & (pl.program_id(2) == nk - 1))
            def _():
                for cm, cin, cout, csem in comm_refs(refs):
                    cm.finish(cin, cout, csem)

    hbm = pl.BlockSpec(memory_space=pltpu.HBM)
    in_specs = [pl.BlockSpec(a_blk, a_map), pl.BlockSpec(b_blk, b_map)]
    in_specs += [pl.BlockSpec(blk, functools.partial(lambda m, i, j, k: m(i, j), m)) for _, blk, m in extras]
    in_specs += [hbm] * n_cin
    out_specs = [pl.BlockSpec(blk, functools.partial(lambda m, i, j, k: m(i, j), m)) for _, _, blk, m in outs]
    out_specs += [hbm] * n_cout
    aliases, cin_at, cout_at = {}, 2 + ne, no
    for cm in comms:
        if cm.aliased:
            aliases.update({cin_at + k: cout_at + k for k in range(len(cm.arrays))})
        cin_at, cout_at = cin_at + len(cm.arrays), cout_at + len(cm.out_shapes)
    res = pl.pallas_call(
        body, name=name, grid=grid, in_specs=in_specs, out_specs=out_specs,
        out_shape=[jax.ShapeDtypeStruct(s, d) for s, d, _, _ in outs] + [s for cm in comms for s in cm.out_shapes],
        scratch_shapes=([pltpu.VMEM(acc_shape, F32)] if nk > 1 else [])
        + [pltpu.SemaphoreType.DMA((n,)) for cm in comms for n in cm.sems],
        input_output_aliases=aliases,
        compiler_params=_params(("arbitrary",) * 3 if comms else ("parallel", "parallel", "arbitrary")),
    )(a, b, *[e for e, _, _ in extras], *[arr for cm in comms for arr in cm.arrays])
    return res


def _rows(arr, tr, tc=None, col0=0):
    if tc is None:
        return (arr, (tr, arr.shape[1]), lambda j, i: (i, 0))
    return (arr, (tr, tc), lambda j, i: (i, col0 + j))


def _full(arr):
    nd = arr.ndim
    return (arr, arr.shape, lambda j, i: (0,) * nd)


def _cols(arr, tc):
    return (arr, (arr.shape[0], tc), lambda j, i: (0, j))


def _rowmap(name, fn, grid, ins, outs, accs=(), scratch=()):
    nj, ni = grid
    n_in, n_out, n_acc = len(ins), len(outs), len(accs)

    def body(*refs):
        vals = [r[...] for r in refs[:n_in]]
        o_refs = refs[n_in:n_in + n_out]
        a_refs = refs[n_in + n_out:n_in + n_out + n_acc]
        s_refs = refs[n_in + n_out + n_acc:]
        res = fn(*vals, *s_refs)
        if not isinstance(res, tuple):
            res = (res,)
        for o, r in zip(o_refs, res[:n_out]):
            o[...] = r.astype(o.dtype)
        i = pl.program_id(1)
        for a_ref, r in zip(a_refs, res[n_out:]):
            @pl.when(i == 0)
            def _(a_ref=a_ref, r=r):
                a_ref[...] = r.astype(a_ref.dtype)

            @pl.when(i > 0)
            def _(a_ref=a_ref, r=r):
                a_ref[...] += r.astype(a_ref.dtype)

    res = pl.pallas_call(
        body, name=name, grid=grid,
        in_specs=[pl.BlockSpec(blk, m) for _, blk, m in ins],
        out_specs=[pl.BlockSpec(blk, m) for _, _, blk, m in list(outs) + list(accs)],
        out_shape=[jax.ShapeDtypeStruct(s, d) for s, d, _, _ in list(outs) + list(accs)],
        scratch_shapes=list(scratch),
        compiler_params=_params(("parallel", "arbitrary")),
    )(*[a for a, _, _ in ins])
    return res


def _o(shape, dtype, tr, tc=None):
    if tc is None:
        return (shape, dtype, (tr, shape[1]), lambda j, i: (i, 0))
    return (shape, dtype, (tr, tc), lambda j, i: (i, j))


def _acc(shape, tc=None):
    if tc is None:
        nd = len(shape)
        return (shape, F32, shape, lambda j, i: (0,) * nd)
    return (shape, F32, (shape[0], tc), lambda j, i: (0, j))


def _position():
    x, y, c = lax.axis_index("x"), lax.axis_index("y"), lax.axis_index("c")
    return x, y, c


def _relay_gather_comm(buf):
    _, rows, cols = buf.shape
    rh = rows // 2
    rq = rh // 2

    def plan(out_ref, sems):
        x, y, c = _position()
        own, cx, cy, cd = 2 * x + y, 2 * (1 - x) + y, 2 * x + (1 - y), 2 * (1 - x) + (1 - y)
        to_x, to_y, sibling = (1 - x, y, c), (x, 1 - y, c), (x, y, 1 - c)
        half, other = pl.ds(c * rh, rh), pl.ds((1 - c) * rh, rh)
        q0, q1 = pl.ds(c * rh, rq), pl.ds(c * rh + rq, rq)

        def copy(pair, k, chip_index, rows_, to):
            return pltpu.make_async_remote_copy(
                src_ref=out_ref.at[chip_index, rows_], dst_ref=out_ref.at[chip_index, rows_],
                send_sem=sems[pair].at[k], recv_sem=sems[pair + 1].at[k], device_id=to, device_id_type=MESH)

        return dict(
            own_x=copy(0, 0, own, half, to_x), own_y=copy(0, 1, own, half, to_y),
            in_x=copy(0, 0, cx, half, to_x), in_y=copy(0, 1, cy, half, to_y),
            fwd_xy=copy(0, 2, cx, q0, to_y), fwd_yx=copy(0, 3, cy, q1, to_x),
            in_d0=copy(0, 2, cd, q0, to_y), in_d1=copy(0, 3, cd, q1, to_x),
            d2d_out=[copy(2, j, ch, half, sibling) for j, ch in enumerate((cx, cy, cd))],
            d2d_in=[copy(2, j, ch, other, sibling) for j, ch in enumerate((cx, cy, cd))])

    def start(ins, outs, sems):
        p = plan(outs[0], sems)
        p['own_x'].start()
        p['own_y'].start()

    def mid(ins, outs, sems):
        p = plan(outs[0], sems)
        p['in_x'].wait_recv()
        p['fwd_xy'].start()
        p['d2d_out'][0].start()
        p['in_y'].wait_recv()
        p['fwd_yx'].start()
        p['d2d_out'][1].start()

    def finish(ins, outs, sems):
        p = plan(outs[0], sems)
        p['in_d0'].wait_recv()
        p['in_d1'].wait_recv()
        p['d2d_out'][2].start()
        for cp in p['d2d_in']:
            cp.wait_recv()
        for cp in [p['own_x'], p['own_y'], p['fwd_xy'], p['fwd_yx']] + p['d2d_out']:
            cp.wait_send()

    return _Comm([buf], [jax.ShapeDtypeStruct(buf.shape, buf.dtype)], True, [4, 4, 3, 3], start, finish, mid)


def _gather_comm(buf):
    _, rows, cols = buf.shape
    rh = rows // 2
    if buf.dtype == BF16 and rh % 32 == 0:
        return _relay_gather_comm(buf)

    def plan(out_ref, sems):
        x, y, c = _position()
        chips = [(1 - x, y), (x, 1 - y), (1 - x, 1 - y)]

        def copy(pair, j, chip_index, half, to):
            return pltpu.make_async_remote_copy(
                src_ref=out_ref.at[chip_index, half], dst_ref=out_ref.at[chip_index, half],
                send_sem=sems[pair].at[j], recv_sem=sems[pair + 1].at[j], device_id=to, device_id_type=MESH)

        mine, other = pl.ds(c * rh, rh), pl.ds((1 - c) * rh, rh)
        ici_out = [copy(0, j, 2 * x + y, mine, (px, py, c)) for j, (px, py) in enumerate(chips)]
        ici_in = [copy(0, j, 2 * px + py, mine, (px, py, c)) for j, (px, py) in enumerate(chips)]
        d2d_out = [copy(2, j, 2 * px + py, mine, (x, y, 1 - c)) for j, (px, py) in enumerate(chips)]
        d2d_in = [copy(2, j, 2 * px + py, other, (x, y, 1 - c)) for j, (px, py) in enumerate(chips)]
        return ici_out, ici_in, d2d_out, d2d_in

    def start(ins, outs, sems):
        for cp in plan(outs[0], sems)[0]:
            cp.start()

    def finish(ins, outs, sems):
        ici_out, ici_in, d2d_out, d2d_in = plan(outs[0], sems)
        for arrived, fwd in zip(ici_in, d2d_out):
            arrived.wait_recv()
            fwd.start()
        for cp in d2d_in:
            cp.wait_recv()
        for cp in ici_out + d2d_out:
            cp.wait_send()

    return _Comm([buf], [jax.ShapeDtypeStruct(buf.shape, buf.dtype)], True, [3, 3, 3, 3], start, finish)


def _scatter_comm(p):
    _, rh, cols = p.shape

    def plan(p_ref, land_ref, sems):
        x, y, c = _position()
        chips = [(1 - x, y), (x, 1 - y), (1 - x, 1 - y)]

        def copy(j, chip_index, to):
            return pltpu.make_async_remote_copy(
                src_ref=p_ref.at[chip_index], dst_ref=land_ref.at[j],
                send_sem=sems[0].at[j], recv_sem=sems[1].at[j], device_id=to, device_id_type=MESH)

        out = [copy(j, 2 * px + py, (px, py, c)) for j, (px, py) in enumerate(chips)]
        arrive = [copy(j, 2 * x + y, (x, y, c)) for j in range(3)]
        return out, arrive

    def start(ins, outs, sems):
        for cp in plan(ins[0], outs[0], sems)[0]:
            cp.start()

    def finish(ins, outs, sems):
        out, arrive = plan(ins[0], outs[0], sems)
        for cp in arrive:
            cp.wait_recv()
        for cp in out:
            cp.wait_send()

    return _Comm([p], [jax.ShapeDtypeStruct((3, rh, cols), p.dtype)], False, [3, 3], start, finish)


def _run_comm(name, cm):
    n_in, n_out = len(cm.arrays), len(cm.out_shapes)

    def body(*refs):
        ins, outs, sems = refs[:n_in], refs[n_in:n_in + n_out], refs[n_in + n_out:]
        cm.start(ins, outs, sems)
        if cm.mid:
            cm.mid(ins, outs, sems)
        cm.finish(ins, outs, sems)

    hbm = pl.BlockSpec(memory_space=pltpu.HBM)
    return pl.pallas_call(
        body, name=name, in_specs=[hbm] * n_in, out_specs=[hbm] * n_out, out_shape=list(cm.out_shapes),
        input_output_aliases={k: k for k in range(n_in)} if cm.aliased else {},
        scratch_shapes=[pltpu.SemaphoreType.DMA((n,)) for n in cm.sems],
    )(*cm.arrays)


def _swap_comm(g):
    n, rows, cols = g.shape
    rh = rows // 2

    def plan(g_ref, land_ref, sems):
        x, y, c = _position()

        def copy(k, half):
            return pltpu.make_async_remote_copy(
                src_ref=g_ref.at[k, pl.ds(half * rh, rh)], dst_ref=land_ref.at[k],
                send_sem=sems[0].at[k], recv_sem=sems[1].at[k], device_id=(x, y, 1 - c), device_id_type=MESH)

        return [copy(k, 1 - c) for k in range(n)], [copy(k, c) for k in range(n)]

    def start(ins, outs, sems):
        for cp in plan(ins[0], outs[0], sems)[0]:
            cp.start()

    def finish(ins, outs, sems):
        out, arrive = plan(ins[0], outs[0], sems)
        for cp in arrive:
            cp.wait_recv()
        for cp in out:
            cp.wait_send()

    return _Comm([g], [jax.ShapeDtypeStruct((n, rh, cols), g.dtype)], False, [n, n], start, finish)


def _pair_sum(name, g, land, pos):
    n, rows, cols = g.shape
    rh = rows // 2
    tr = _tile(rh, 256, 16)
    nt = rh // tr

    def body(pos_ref, g_ref, l_ref, o_ref):
        o_ref[...] = (g_ref[...].astype(F32) + l_ref[...].astype(F32)).astype(o_ref.dtype)

    grid_spec = pltpu.PrefetchScalarGridSpec(
        num_scalar_prefetch=1, grid=(n, nt),
        in_specs=[pl.BlockSpec((None, tr, cols), lambda k, i, p: (k, p[1] * nt + i, 0)),
                  pl.BlockSpec((None, tr, cols), lambda k, i, p: (k, i, 0))],
        out_specs=pl.BlockSpec((None, tr, cols), lambda k, i, p: (k, i, 0)))
    return pl.pallas_call(
        body, name=name, grid_spec=grid_spec, out_shape=jax.ShapeDtypeStruct((n, rh, cols), g.dtype),
        compiler_params=_params(("parallel", "parallel")),
    )(pos, g, land)


def _reduce_partials(name, p, land, pos):
    _, rh, cols = p.shape
    tr = _tile(rh, 256, 16)
    nt = rh // tr

    def body(pos_ref, p_ref, l_ref, o_ref):
        acc = p_ref[...].astype(F32)
        for k in range(3):
            acc = acc + l_ref[k].astype(F32)
        o_ref[...] = acc

    grid_spec = pltpu.PrefetchScalarGridSpec(
        num_scalar_prefetch=1, grid=(nt,),
        in_specs=[pl.BlockSpec((None, tr, cols), lambda i, p_: (p_[0], i, 0)),
                  pl.BlockSpec((3, tr, cols), lambda i, p_: (0, i, 0))],
        out_specs=pl.BlockSpec((None, tr, cols), lambda i, p_: (p_[1], i, 0)))
    return pl.pallas_call(
        body, name=name, grid_spec=grid_spec, out_shape=jax.ShapeDtypeStruct((2, rh, cols), F32),
        compiler_params=_params(("parallel",)),
    )(pos, p, land)


def _sibling_exchange(name, bufs):
    n = len(bufs)

    def body(*refs):
        outs = refs[n:2 * n]
        send_sems, recv_sems = refs[2 * n:]
        x, y, c = _position()
        sibling = (x, y, 1 - c)

        def copy(k, half):
            return pltpu.make_async_remote_copy(
                src_ref=outs[k].at[half], dst_ref=outs[k].at[half], send_sem=send_sems.at[k], recv_sem=recv_sems.at[k],
                device_id=sibling, device_id_type=MESH)

        sends = [copy(k, c) for k in range(n)]
        for cp in sends:
            cp.start()
        for k in range(n):
            copy(k, 1 - c).wait_recv()
        for cp in sends:
            cp.wait_send()

    hbm = pl.BlockSpec(memory_space=pltpu.HBM)
    return pl.pallas_call(
        body, name=name, in_specs=[hbm] * n, out_specs=[hbm] * n, input_output_aliases={k: k for k in range(n)},
        out_shape=[jax.ShapeDtypeStruct(b.shape, b.dtype) for b in bufs],
        scratch_shapes=[pltpu.SemaphoreType.DMA((n,)), pltpu.SemaphoreType.DMA((n,))],
    )(*bufs)


def _all_reduce_small(name, buf):
    rows = buf.shape[0]
    p = rows // 8

    def body(buf_ref, out_ref, land_ref, red_ref, s1, r1, s2, r2):
        x, y, c = _position()
        me = 4 * x + 2 * y + c

        def peer(r):
            px = 1 - x if (r >> 2) & 1 else x
            py = 1 - y if (r >> 1) & 1 else y
            pc = 1 - c if r & 1 else c
            return (px, py, pc), 4 * px + 2 * py + pc

        firsts = []
        for r in range(1, 8):
            to, d = peer(r)
            cp = pltpu.make_async_remote_copy(
                src_ref=buf_ref.at[pl.ds(d * p, p)], dst_ref=land_ref.at[r - 1],
                send_sem=s1.at[r - 1], recv_sem=r1.at[r - 1], device_id=to, device_id_type=MESH)
            cp.start()
            firsts.append(cp)
        acc = buf_ref[pl.ds(me * p, p), :]
        for r in range(1, 8):
            firsts[r - 1].wait_recv()
            acc = acc + land_ref[r - 1]
        red_ref[...] = acc
        out_ref[pl.ds(me * p, p), :] = acc
        seconds = []
        for r in range(1, 8):
            to, d = peer(r)
            cp = pltpu.make_async_remote_copy(
                src_ref=red_ref, dst_ref=out_ref.at[pl.ds(me * p, p)],
                send_sem=s2.at[r - 1], recv_sem=r2.at[r - 1], device_id=to, device_id_type=MESH)
            cp.start()
            seconds.append(cp)
        for r in range(1, 8):
            to, d = peer(r)
            pltpu.make_async_remote_copy(
                src_ref=red_ref, dst_ref=out_ref.at[pl.ds(d * p, p)],
                send_sem=s2.at[r - 1], recv_sem=r2.at[r - 1], device_id=to, device_id_type=MESH).wait_recv()
        for cp in firsts + seconds:
            cp.wait_send()

    vmem = pl.BlockSpec(memory_space=pltpu.VMEM)
    return pl.pallas_call(
        body, name=name, in_specs=[vmem], out_specs=vmem,
        out_shape=jax.ShapeDtypeStruct(buf.shape, F32),
        scratch_shapes=[pltpu.VMEM((7, p, LANES), F32), pltpu.VMEM((p, LANES), F32),
                        pltpu.SemaphoreType.DMA((7,)), pltpu.SemaphoreType.DMA((7,)),
                        pltpu.SemaphoreType.DMA((7,)), pltpu.SemaphoreType.DMA((7,))],
        compiler_params=pltpu.CompilerParams(vmem_limit_bytes=VMEM_LIMIT),
    )(buf)


def _cast_pad(name, w, rows_to, cols_to, pos):
    rows, cols = w.shape
    tr = _tile(math.gcd(rows, rows_to), 256, 16)
    assert rows % tr == 0 and rows_to % tr == 0, (rows, rows_to, tr)
    n_src = rows // tr

    def body(pos_ref, w_ref, o_ref):
        i = pl.program_id(0)
        if cols_to > cols:
            o_ref[:, cols:] = jnp.zeros((tr, cols_to - cols), BF16)

        @pl.when(i < n_src)
        def _():
            o_ref[:, :cols] = w_ref[...].astype(BF16)

        if rows_to > rows:
            @pl.when(i >= n_src)
            def _():
                o_ref[:, :cols] = jnp.zeros((tr, cols), BF16)

    grid_spec = pltpu.PrefetchScalarGridSpec(
        num_scalar_prefetch=1, grid=(rows_to // tr,),
        in_specs=[pl.BlockSpec((tr, cols), lambda i, p: (jnp.minimum(i, n_src - 1), 0))],
        out_specs=pl.BlockSpec((None, tr, cols_to), lambda i, p: (p[0], i, 0)))
    return pl.pallas_call(
        body, name=name, grid_spec=grid_spec,
        out_shape=jax.ShapeDtypeStruct((N_CHIPS, rows_to, cols_to), BF16),
        compiler_params=_params(("parallel",)),
    )(pos, w)


def _scan_rows(a, b, row, tr, reverse):
    sh = 1
    while sh < tr:
        if reverse:
            valid = row < tr - sh
            a_s = pltpu.roll(a, tr - sh, 0)
            b_s = pltpu.roll(b, tr - sh, 0)
        else:
            valid = row >= sh
            a_s = pltpu.roll(a, sh, 0)
            b_s = pltpu.roll(b, sh, 0)
        b = b + jnp.where(valid, a * b_s, 0.0)
        a = jnp.where(valid, a * a_s, a)
        sh *= 2
    return a, b


def _linear_scan(name, a, b, tr, reverse=False, shift_a=False):
    t_rows, cols = a.shape
    tc = _tile(cols, 512)
    ni, nj = t_rows // tr, cols // tc
    hb = tr // 8

    def rmap(j, i):
        return ((ni - 1 - i) if reverse else i, j)

    def halo_map(j, i):
        ri = ni - 1 - i
        return (jnp.minimum((ri + 1) * hb, t_rows // 8 - 1), j)

    def body(*refs):
        if shift_a:
            a_ref, halo_ref, b_ref, h_ref, carry_ref = refs
        else:
            a_ref, b_ref, h_ref, carry_ref = refs
        i = pl.program_id(1)
        row = lax.broadcasted_iota(jnp.int32, (tr, tc), 0)

        @pl.when(i == 0)
        def _():
            carry_ref[...] = jnp.zeros_like(carry_ref)

        av = a_ref[...]
        if shift_a:
            nxt = jnp.where(i > 0, halo_ref[0:1, :], 0.0)
            av = jnp.where(row == tr - 1, jnp.broadcast_to(nxt, (tr, tc)), pltpu.roll(av, tr - 1, 0))
        pa, hb_ = _scan_rows(av, b_ref[...], row, tr, reverse)
        h = hb_ + pa * carry_ref[0:1, :]
        h_ref[...] = h
        last = h[0:1, :] if reverse else h[tr - 1:tr, :]
        carry_ref[...] = jnp.broadcast_to(last, carry_ref.shape)

    in_specs = [pl.BlockSpec((tr, tc), rmap)]
    args = [a]
    if shift_a:
        in_specs.append(pl.BlockSpec((8, tc), halo_map))
        args.append(a)
    in_specs.append(pl.BlockSpec((tr, tc), rmap))
    args.append(b)
    return pl.pallas_call(
        body, name=name, grid=(nj, ni), in_specs=in_specs, out_specs=pl.BlockSpec((tr, tc), rmap),
        out_shape=jax.ShapeDtypeStruct((t_rows, cols), F32), scratch_shapes=[pltpu.VMEM((8, tc), F32)],
        compiler_params=_params(("parallel", "arbitrary")),
    )(*args)


def _conv_fwd(name, proj, conv_w, conv_b, width, tr):
    t_rows = proj.shape[0]
    ni = t_rows // tr
    hb = tr // 8
    tc = _tile(width, 512)

    def body(u_ref, halo_ref, w_ref, b_ref, o_ref, ext_ref):
        i = pl.program_id(1)
        ext_ref[0:8, :] = jnp.where(i > 0, halo_ref[...], 0.0)
        ext_ref[8:8 + tr, :] = u_ref[...]
        acc = b_ref[...] + w_ref[CONV_WIDTH - 1:CONV_WIDTH, :] * u_ref[...]
        for k in range(CONV_WIDTH - 1):
            acc = acc + w_ref[k:k + 1, :] * ext_ref[pl.ds(8 - (CONV_WIDTH - 1) + k, tr), :]
        o_ref[...] = acc

    return pl.pallas_call(
        body, name=name, grid=(width // tc, ni),
        in_specs=[pl.BlockSpec((tr, tc), lambda j, i: (i, j)),
                  pl.BlockSpec((8, tc), lambda j, i: (jnp.maximum(i * hb - 1, 0), j)),
                  pl.BlockSpec((CONV_WIDTH, tc), lambda j, i: (0, j)),
                  pl.BlockSpec((1, tc), lambda j, i: (0, j))],
        out_specs=pl.BlockSpec((tr, tc), lambda j, i: (i, j)),
        out_shape=jax.ShapeDtypeStruct((t_rows, width), F32),
        scratch_shapes=[pltpu.VMEM((tr + 8, tc), F32)],
        compiler_params=_params(("parallel", "parallel")),
    )(proj, proj, conv_w, conv_b)


def _conv_bwd(name, dxc, proj, conv_w, width, tr):
    t_rows = dxc.shape[0]
    ni = t_rows // tr
    hb = tr // 8
    tc = _tile(width, 512)

    def body(d_ref, dnext_ref, u_ref, uprev_ref, w_ref, du_ref, dw_ref, db_ref, dext_ref, uext_ref):
        i = pl.program_id(1)
        dext_ref[0:tr, :] = d_ref[...]
        dext_ref[tr:tr + 8, :] = jnp.where(i < ni - 1, dnext_ref[...], 0.0)
        uext_ref[0:8, :] = jnp.where(i > 0, uprev_ref[...], 0.0)
        uext_ref[8:8 + tr, :] = u_ref[...]
        d = d_ref[...]
        du = w_ref[CONV_WIDTH - 1:CONV_WIDTH, :] * d
        dws = []
        for k in range(CONV_WIDTH - 1):
            du = du + w_ref[k:k + 1, :] * dext_ref[pl.ds(CONV_WIDTH - 1 - k, tr), :]
            dws.append(jnp.sum(d * uext_ref[pl.ds(8 - (CONV_WIDTH - 1) + k, tr), :], axis=0, keepdims=True))
        dws.append(jnp.sum(d * u_ref[...], axis=0, keepdims=True))
        du_ref[...] = du
        dw = jnp.concatenate(dws, axis=0)
        db = jnp.sum(d, axis=0, keepdims=True)

        @pl.when(i == 0)
        def _():
            dw_ref[...] = dw
            db_ref[...] = db

        @pl.when(i > 0)
        def _():
            dw_ref[...] += dw
            db_ref[...] += db

    return pl.pallas_call(
        body, name=name, grid=(width // tc, ni),
        in_specs=[pl.BlockSpec((tr, tc), lambda j, i: (i, j)),
                  pl.BlockSpec((8, tc), lambda j, i: (jnp.minimum((i + 1) * hb, t_rows // 8 - 1), j)),
                  pl.BlockSpec((tr, tc), lambda j, i: (i, j)),
                  pl.BlockSpec((8, tc), lambda j, i: (jnp.maximum(i * hb - 1, 0), j)),
                  pl.BlockSpec((CONV_WIDTH, tc), lambda j, i: (0, j))],
        out_specs=[pl.BlockSpec((tr, tc), lambda j, i: (i, j)),
                   pl.BlockSpec((CONV_WIDTH, tc), lambda j, i: (0, j)),
                   pl.BlockSpec((1, tc), lambda j, i: (0, j))],
        out_shape=[jax.ShapeDtypeStruct((t_rows, width), F32), jax.ShapeDtypeStruct((CONV_WIDTH, width), F32),
                   jax.ShapeDtypeStruct((1, width), F32)],
        scratch_shapes=[pltpu.VMEM((tr + 8, tc), F32), pltpu.VMEM((tr + 8, tc), F32)],
        compiler_params=_params(("parallel", "arbitrary")),
    )(dxc, dxc, proj, proj, conv_w)


def _s5_disc_math(lam_re, lam_im, log_dt, b_re, b_im, expand):
    dt = jnp.exp(log_dt)
    zr, zi = lam_re * dt, lam_im * dt
    mag = jnp.exp(zr)
    lbr, lbi = mag * jnp.cos(zi), mag * jnp.sin(zi)
    ar, ai = lbr - 1.0, lbi
    den = lam_re * lam_re + lam_im * lam_im
    cr = (ar * lam_re + ai * lam_im) / den
    ci = (ai * lam_re - ar * lam_im) / den
    cre = jnp.dot(cr, expand, precision=lax.Precision.HIGHEST, preferred_element_type=F32)
    cie = jnp.dot(ci, expand, precision=lax.Precision.HIGHEST, preferred_element_type=F32)
    return lbr, lbi, cre * b_re - cie * b_im, cre * b_im + cie * b_re


def _s5_disc(name, lam_re, lam_im, log_dt, b_re, b_im, expand, cots=None):
    ins = [lam_re, lam_im, log_dt, b_re, b_im, expand]
    n_in = len(ins) + (len(cots) if cots else 0)

    def body(*refs):
        vals = [r[...] for r in refs[:6]]
        outs = refs[n_in:]
        if cots is None:
            res = _s5_disc_math(*vals)
        else:
            cv = tuple(r[...] for r in refs[6:n_in])
            _, vjp = jax.vjp(lambda a, b, c, d, e: _s5_disc_math(a, b, c, d, e, vals[5]), *vals[:5])
            res = vjp(cv)
        for o, r in zip(outs, res):
            o[...] = r

    if cots is None:
        shapes = [lam_re.shape, lam_re.shape, b_re.shape, b_re.shape]
    else:
        shapes = [lam_re.shape, lam_re.shape, log_dt.shape, b_re.shape, b_re.shape]
    vmem = pl.BlockSpec(memory_space=pltpu.VMEM)
    return pl.pallas_call(
        body, name=name, in_specs=[vmem] * n_in, out_specs=[vmem] * len(shapes),
        out_shape=[jax.ShapeDtypeStruct(s, F32) for s in shapes],
        compiler_params=pltpu.CompilerParams(vmem_limit_bytes=VMEM_LIMIT),
    )(*ins, *(cots or ()))


def _cmul(ar, ai, br, bi):
    return ar * br - ai * bi, ar * bi + ai * br


SCAN_BLOCK = 8


def _block_scan(br, bi, lr, li, row, rows, reverse):
    pr, pi = lr, li
    pos = row & (SCAN_BLOCK - 1)
    sh = 1
    while sh < SCAN_BLOCK:
        if reverse:
            valid = pos < SCAN_BLOCK - sh
            rs, is_ = pltpu.roll(br, rows - sh, 0), pltpu.roll(bi, rows - sh, 0)
        else:
            valid = pos >= sh
            rs, is_ = pltpu.roll(br, sh, 0), pltpu.roll(bi, sh, 0)
        mr, mi = _cmul(pr, pi, rs, is_)
        br = br + jnp.where(valid, mr, 0.0)
        bi = bi + jnp.where(valid, mi, 0.0)
        pr, pi = _cmul(pr, pi, pr, pi)
        sh *= 2
    return br, bi


def _block_powers(lr, li, hw, reverse):
    row = lax.broadcasted_iota(jnp.int32, (SCAN_BLOCK, hw), 0)
    d = jnp.where(row == (SCAN_BLOCK - 1 if reverse else 0), 1.0, 0.0)
    return _block_scan(jnp.broadcast_to(lr, (SCAN_BLOCK, hw)) * d, jnp.broadcast_to(li, (SCAN_BLOCK, hw)) * d,
                       lr, li, row, SCAN_BLOCK, reverse)


def _chain_blocks(s_ref, pow_ref, carry, rows, hw, reverse):
    nblk = rows // SCAN_BLOCK
    pr, pi = pow_ref[:, :hw], pow_ref[:, hw:]

    def step(n, c):
        b = (nblk - 1 - n) if reverse else n
        at = pl.ds(pl.multiple_of(b * SCAN_BLOCK, SCAN_BLOCK), SCAN_BLOCK)
        blk = s_ref[at, :]
        kr, ki = _cmul(pr, pi, c[0], c[1])
        sr, si = blk[:, :hw] + kr, blk[:, hw:] + ki
        s_ref[at, :] = jnp.concatenate([sr, si], axis=1)
        edge = slice(0, 1) if reverse else slice(SCAN_BLOCK - 1, SCAN_BLOCK)
        return sr[edge, :], si[edge, :]

    return lax.fori_loop(0, nblk, step, carry)


def _s5_fwd(name, proj, col0, bblk, cblk, lamblk, dvec, width, tr):
    t_rows = proj.shape[0]
    ni = t_rows // tr
    nb = width // LANES
    sw = bblk.shape[-1]
    hw = sw // 2

    def body(u_ref, b_ref, c_ref, lam_ref, d_ref, y_ref, s_ref, carry_ref, pow_ref, st_ref):
        i = pl.program_id(1)
        row = lax.broadcasted_iota(jnp.int32, (tr, hw), 0)
        lam = lam_ref[...]
        lr, li = lam[:, :hw], lam[:, hw:]

        @pl.when(i == 0)
        def _():
            carry_ref[...] = jnp.zeros_like(carry_ref)
            zr, zi = _block_powers(lr, li, hw, False)
            pow_ref[:, :hw] = zr
            pow_ref[:, hw:] = zi

        u = u_ref[...]
        bu = jnp.dot(u.astype(BF16), b_ref[...], preferred_element_type=F32)
        sr, si = _block_scan(bu[:, :hw], bu[:, hw:], lr, li, row, tr, False)
        st_ref[:, :hw] = sr
        st_ref[:, hw:] = si
        cr, ci = _chain_blocks(st_ref, pow_ref, (carry_ref[0:1, :hw], carry_ref[0:1, hw:]), tr, hw, False)
        carry_ref[...] = jnp.broadcast_to(jnp.concatenate([cr, ci], axis=1), carry_ref.shape)
        s16 = st_ref[...].astype(BF16)
        s_ref[...] = s16
        y_ref[...] = jnp.dot(s16, c_ref[...], preferred_element_type=F32) + d_ref[...] * u

    return pl.pallas_call(
        body, name=name, grid=(nb, ni),
        in_specs=[pl.BlockSpec((tr, LANES), lambda k, i: (i, col0 + k)),
                  pl.BlockSpec((None, LANES, sw), lambda k, i: (k, 0, 0)),
                  pl.BlockSpec((None, sw, LANES), lambda k, i: (k, 0, 0)),
                  pl.BlockSpec((None, 1, sw), lambda k, i: (k, 0, 0)),
                  pl.BlockSpec((1, LANES), lambda k, i: (0, k))],
        out_specs=[pl.BlockSpec((tr, LANES), lambda k, i: (i, k)),
                   pl.BlockSpec((tr, sw), lambda k, i: (i, k))],
        out_shape=[jax.ShapeDtypeStruct((t_rows, width), F32), jax.ShapeDtypeStruct((t_rows, nb * sw), BF16)],
        scratch_shapes=[pltpu.VMEM((8, sw), F32), pltpu.VMEM((SCAN_BLOCK, sw), F32), pltpu.VMEM((tr, sw), F32)],
        compiler_params=_params(("parallel", "arbitrary")),
    )(proj, bblk, cblk, lamblk, dvec)


def _s5_bwd(name, dy, proj, col0, states, bblk, cblk, lamblk, dvec, width, tr):
    t_rows = dy.shape[0]
    ni = t_rows // tr
    nb = width // LANES
    sw = bblk.shape[-1]
    hw = sw // 2
    hb16 = tr // 16

    def rmap(k, i):
        return (ni - 1 - i, k)

    def body(dy_ref, u_ref, s_ref, sprev_ref, b_ref, c_ref, lam_ref, d_ref,
             du_ref, db_ref, dc_ref, dlam_ref, dd_ref, carry_ref, pow_ref, gt_ref):
        i = pl.program_id(1)
        row = lax.broadcasted_iota(jnp.int32, (tr, hw), 0)
        row_w = lax.broadcasted_iota(jnp.int32, (tr, sw), 0)
        lam = lam_ref[...]
        lr, li = lam[:, :hw], -lam[:, hw:]

        @pl.when(i == 0)
        def _():
            carry_ref[...] = jnp.zeros_like(carry_ref)
            zr, zi = _block_powers(lr, li, hw, True)
            pow_ref[:, :hw] = zr
            pow_ref[:, hw:] = zi

        dyv = dy_ref[...]
        dy16 = dyv.astype(BF16)
        u = u_ref[...]
        gd = lax.dot_general(dy16, c_ref[...], _DN['nt'], preferred_element_type=F32)
        gr, gi = _block_scan(gd[:, :hw], gd[:, hw:], lr, li, row, tr, True)
        gt_ref[:, :hw] = gr
        gt_ref[:, hw:] = gi
        cr, ci = _chain_blocks(gt_ref, pow_ref, (carry_ref[0:1, :hw], carry_ref[0:1, hw:]), tr, hw, True)
        carry_ref[...] = jnp.broadcast_to(jnp.concatenate([cr, ci], axis=1), carry_ref.shape)
        g = gt_ref[...]
        gr, gi = g[:, :hw], g[:, hw:]
        g16 = g.astype(BF16)
        du_ref[...] = lax.dot_general(g16, b_ref[...], _DN['nt'], preferred_element_type=F32) + d_ref[...] * dyv
        s16 = s_ref[...]
        dbv = lax.dot_general(g16, u.astype(BF16), _DN['tn'], preferred_element_type=F32)
        dcv = lax.dot_general(s16, dy16, _DN['tn'], preferred_element_type=F32)
        s32 = s16.astype(F32)
        first = jnp.where(i < ni - 1, sprev_ref[15:16, :].astype(F32), 0.0)
        sp = jnp.where(row_w == 0, jnp.broadcast_to(first, (tr, sw)), pltpu.roll(s32, 1, 0))
        spr, spi = sp[:, :hw], sp[:, hw:]
        dlr = jnp.sum(gr * spr + gi * spi, axis=0, keepdims=True)
        dli = jnp.sum(gi * spr - gr * spi, axis=0, keepdims=True)
        dlam = jnp.concatenate([dlr, dli], axis=1)
        ddv = jnp.sum(dyv * u, axis=0, keepdims=True)

        @pl.when(i == 0)
        def _():
            db_ref[...] = dbv
            dc_ref[...] = dcv
            dlam_ref[...] = dlam
            dd_ref[...] = ddv

        @pl.when(i > 0)
        def _():
            db_ref[...] += dbv
            dc_ref[...] += dcv
            dlam_ref[...] += dlam
            dd_ref[...] += ddv

    return pl.pallas_call(
        body, name=name, grid=(nb, ni),
        in_specs=[pl.BlockSpec((tr, LANES), rmap),
                  pl.BlockSpec((tr, LANES), lambda k, i: (ni - 1 - i, col0 + k)),
                  pl.BlockSpec((tr, sw), rmap),
                  pl.BlockSpec((16, sw), lambda k, i: (jnp.maximum((ni - 1 - i) * hb16 - 1, 0), k)),
                  pl.BlockSpec((None, LANES, sw), lambda k, i: (k, 0, 0)),
                  pl.BlockSpec((None, sw, LANES), lambda k, i: (k, 0, 0)),
                  pl.BlockSpec((None, 1, sw), lambda k, i: (k, 0, 0)),
                  pl.BlockSpec((1, LANES), lambda k, i: (0, k))],
        out_specs=[pl.BlockSpec((tr, LANES), rmap),
                   pl.BlockSpec((None, sw, LANES), lambda k, i: (k, 0, 0)),
                   pl.BlockSpec((None, sw, LANES), lambda k, i: (k, 0, 0)),
                   pl.BlockSpec((None, 1, sw), lambda k, i: (k, 0, 0)),
                   pl.BlockSpec((1, LANES), lambda k, i: (0, k))],
        out_shape=[jax.ShapeDtypeStruct((t_rows, width), F32), jax.ShapeDtypeStruct((nb, sw, LANES), F32),
                   jax.ShapeDtypeStruct((nb, sw, LANES), F32), jax.ShapeDtypeStruct((nb, 1, sw), F32),
                   jax.ShapeDtypeStruct((1, width), F32)],
        scratch_shapes=[pltpu.VMEM((8, sw), F32), pltpu.VMEM((SCAN_BLOCK, sw), F32), pltpu.VMEM((tr, sw), F32)],
        compiler_params=_params(("parallel", "arbitrary")),
    )(dy, proj, states, states, bblk, cblk, lamblk, dvec)


def _blockdiag(m):
    nb, g, p, q = m.shape
    eye = jnp.eye(g, dtype=m.dtype)
    return (m[:, :, :, None, :] * eye[None, :, None, :, None]).reshape(nb, g * p, g * q)


def _blockdiag_take(m, p, q):
    nb = m.shape[0]
    g = GROUPS_PER_BLOCK
    m = m.reshape(nb, g, p, g, q)
    return jnp.stack([m[:, k, :, k, :] for k in range(g)], axis=1)


def _adamw(name, w, g, m, v, tr):
    rows, cols = w.shape
    gcols = g.shape[1]
    c1 = 1.0 - ADAM_B1 ** ADAM_STEP
    c2 = 1.0 - ADAM_B2 ** ADAM_STEP

    def body(w_ref, g_ref, m_ref, v_ref, go_ref, d_ref, mo_ref, vo_ref):
        gv = g_ref[...] if gcols == cols else g_ref[:, :cols]
        mn = ADAM_B1 * m_ref[...] + (1.0 - ADAM_B1) * gv
        vn = ADAM_B2 * v_ref[...] + (1.0 - ADAM_B2) * (gv * gv)
        go_ref[...] = gv
        mo_ref[...] = mn
        vo_ref[...] = vn
        d_ref[...] = -ADAM_LR * ((mn / c1) / (jnp.sqrt(vn / c2) + ADAM_EPS) + ADAM_WD * w_ref[...])

    spec = pl.BlockSpec((tr, cols), lambda i: (i, 0))
    return pl.pallas_call(
        body, name=name, grid=(rows // tr,),
        in_specs=[spec, pl.BlockSpec((tr, gcols), lambda i: (i, 0)), spec, spec],
        out_specs=[spec] * 4, out_shape=[jax.ShapeDtypeStruct((rows, cols), F32)] * 4,
        compiler_params=_params(("parallel",)),
    )(w, g, m, v)


def _ffn_fwd(tag, h, norm, wg, wu, wd, dims, host=None):
    t_rows, d, fp, tm, tr, trn = dims['T'], dims['D'], dims['FP'], dims['TM'], dims['TR'], dims['TRN']
    host = host or {}
    got = {}
    ft = N_CHIPS * fp
    tn = _tile(fp, 1408)
    npb = fp // tn
    n16, = _rowmap(f"{tag}_norm", lambda x, g: _rms(x, g).astype(BF16), (1, t_rows // trn),
                   [_rows(h, trn), _full(norm)], [_o((t_rows, d), BF16, trn)])

    def up(nm, w):
        res = _matmul(f"{tag}_{nm}", 'nt', n16, (tm, d // 2), lambda i, j, k: (i, k),
                      w, (None, tn, d // 2), lambda i, j, k: (j // npb, j % npb, k),
                      (t_rows // tm, ft // tn, 2), [((t_rows, ft), BF16, (tm, tn), lambda i, j: (i, j))],
                      lambda acc: acc, comms=host.get(nm, ()))
        got[nm] = res[1:]
        return res[0]

    gate = up("gate", wg)
    wu = got['gate'][wu] if isinstance(wu, int) else wu
    upv = up("up", wu)
    wd = got['up'][wd] if isinstance(wd, int) else wd
    tc = _tile(ft, 1408)
    act, = _rowmap(f"{tag}_act", lambda g, u: _silu(g.astype(F32)) * u.astype(F32), (ft // tc, t_rows // tr),
                   [_rows(gate, tr, tc), _rows(upv, tr, tc)], [_o((t_rows, ft), BF16, tr, tc)])
    tnd = _tile(d, 1024)
    res = _matmul(f"{tag}_down", 'nn', act, (tm, fp), lambda i, j, k: (i, k),
                  wd, (None, fp, tnd), lambda i, j, k: (k, 0, j),
                  (t_rows // tm, d // tnd, N_CHIPS), [((t_rows, d), F32, (tm, tnd), lambda i, j: (i, j))],
                  lambda acc, hin: hin + 0.5 * acc, extras=[(h, (tm, tnd), lambda i, j: (i, j))],
                  comms=host.get('down', ()))
    got['down'] = res[1:]
    return res[0], dict(h=h, n16=n16, gate=gate, up=upv, act=act, wu=wu, wd=wd), got


def _pair(nm, g, pos):
    return _pair_sum(f"pair_{nm}", g, _run_comm(f"swap_{nm}", _swap_comm(g))[0], pos)


def _ffn_bwd(tag, names, dh, dh16, saved, norm, wg, dims, pos, host=None):
    t_rows, d, fp, tm, trn = dims['T'], dims['D'], dims['FP'], dims['TM'], dims['TRN']
    host = host or {}
    got = {}
    n_gate, n_up, n_down = names
    wu, wd = saved['wu'], saved['wd']
    ft = N_CHIPS * fp
    tn = _tile(fp, 1408)
    npb = fp // tn
    tm2 = tm // 2
    tk = t_rows // 2

    def act_bwd(acc, g, u):
        da = 0.5 * acc
        g32, u32 = g.astype(F32), u.astype(F32)
        sg = _sigmoid(g32)
        return da * u32 * (sg * (1.0 + g32 * (1.0 - sg))), da * (g32 * sg)

    res = _matmul(
        f"{tag}_dact", 'nt', dh16, (tm2, d // 2), lambda i, j, k: (i, k),
        wd, (None, tn, d // 2), lambda i, j, k: (j // npb, j % npb, k),
        (t_rows // tm2, ft // tn, 2),
        [((t_rows, ft), BF16, (tm2, tn), lambda i, j: (i, j))] * 2, act_bwd,
        extras=[(saved['gate'], (tm2, tn), lambda i, j: (i, j)), (saved['up'], (tm2, tn), lambda i, j: (i, j))],
        comms=host.get('dact', ()))
    dgate, dup, got['dact'] = res[0], res[1], res[2:]
    tnd = _tile(d, 1024)
    res = _matmul(f"{tag}_dwd", 'tn', saved['act'], (tk, tn), lambda i, j, k: (k, i),
                  dh16, (tk, tnd), lambda i, j, k: (k, j),
                  (ft // tn, d // tnd, 2),
                  [((N_CHIPS, fp, d), BF16, (None, tn, tnd), lambda i, j: (i // npb, i % npb, j))],
                  lambda acc: 0.5 * acc, comms=host.get('dwd', ()))
    dwd, got['dwd'] = res[0], res[1:]

    def dw_up(nm, dact, comms):
        return _matmul(f"{tag}_{nm}", 'tn', dact, (tk, tn), lambda i, j, k: (k, i),
                       saved['n16'], (tk, tnd), lambda i, j, k: (k, j),
                       (ft // tn, d // tnd, 2),
                       [((N_CHIPS, fp, d), BF16, (None, tn, tnd), lambda i, j: (i // npb, i % npb, j))],
                       lambda acc: acc, comms=comms)

    def dn_part(nm, dact, w, prev, comms):
        extras = [] if prev is None else [(prev, (tm, tnd), lambda i, j: (i, j))]
        return _matmul(f"{tag}_{nm}", 'nn', dact, (tm, fp), lambda i, j, k: (i, k),
                       w, (None, fp, tnd), lambda i, j, k: (k, 0, j),
                       (t_rows // tm, d // tnd, N_CHIPS), [((t_rows, d), F32, (tm, tnd), lambda i, j: (i, j))],
                       (lambda acc: acc) if prev is None else (lambda acc, p: acc + p), extras=extras, comms=comms)

    dwg, swap_d = dw_up("dwg", dgate, [_swap_comm(dwd)])
    pair_d = _pair_sum(f"pair_{n_down}", dwd, swap_d, pos)
    dwu, land_d, swap_g = dw_up("dwu", dup, [_scatter_comm(pair_d), _swap_comm(dwg)])
    pair_g = _pair_sum(f"pair_{n_gate}", dwg, swap_g, pos)
    dn_g, land_g, swap_u = dn_part("dn_gate", dgate, wg, None, [_scatter_comm(pair_g), _swap_comm(dwu)])
    pair_u = _pair_sum(f"pair_{n_up}", dwu, swap_u, pos)
    dn, land_u = dn_part("dn_up", dup, wu, dn_g, [_scatter_comm(pair_u)])

    def norm_bwd(x, dnv, dhv, g):
        _, vjp = jax.vjp(_rms, x, g)
        dx, dg = vjp(dnv)
        res = dhv + dx
        return res, res.astype(BF16), dg

    dh_in, dh_in16, dnorm = _rowmap(
        f"{tag}_dnorm", norm_bwd, (1, t_rows // trn),
        [_rows(saved['h'], trn), _rows(dn, trn), _rows(dh, trn), _full(norm)],
        [_o((t_rows, d), F32, trn), _o((t_rows, d), BF16, trn)], [_acc((1, d))])
    parts = {n_down: (pair_d, land_d), n_gate: (pair_g, land_g), n_up: (pair_u, land_u)}
    return dh_in, dh_in16, dnorm, parts, got


def kernel(x, meta_tokens, ffn1_norm, ffn1_w_gate, ffn1_w_up, ffn1_w_down, mix_norm, w_in, rg_conv_w, rg_conv_b, rg_w_a, rg_b_a, rg_w_x, rg_b_x, rg_lambda, s5_lambda_re, s5_lambda_im, s5_log_dt, s5_b_re, s5_b_im, s5_c_re, s5_c_im, s5_d, s5_glu_w, s5_glu_b, rg_out_norm, s5_out_norm, w_out, ffn2_norm, ffn2_w_gate, ffn2_w_up, ffn2_w_down, final_norm, loss_target, m_meta_tokens, m_ffn1_norm, m_ffn1_w_gate, m_ffn1_w_up, m_ffn1_w_down, m_mix_norm, m_w_in, m_rg_conv_w, m_rg_conv_b, m_rg_w_a, m_rg_b_a, m_rg_w_x, m_rg_b_x, m_rg_lambda, m_s5_lambda_re, m_s5_lambda_im, m_s5_log_dt, m_s5_b_re, m_s5_b_im, m_s5_c_re, m_s5_c_im, m_s5_d, m_s5_glu_w, m_s5_glu_b, m_rg_out_norm, m_s5_out_norm, m_w_out, m_ffn2_norm, m_ffn2_w_gate, m_ffn2_w_up, m_ffn2_w_down, m_final_norm, v_meta_tokens, v_ffn1_norm, v_ffn1_w_gate, v_ffn1_w_up, v_ffn1_w_down, v_mix_norm, v_w_in, v_rg_conv_w, v_rg_conv_b, v_rg_w_a, v_rg_b_a, v_rg_w_x, v_rg_b_x, v_rg_lambda, v_s5_lambda_re, v_s5_lambda_im, v_s5_log_dt, v_s5_b_re, v_s5_b_im, v_s5_c_re, v_s5_c_im, v_s5_d, v_s5_glu_w, v_s5_glu_b, v_rg_out_norm, v_s5_out_norm, v_w_out, v_ffn2_norm, v_ffn2_w_gate, v_ffn2_w_up, v_ffn2_w_down, v_final_norm):
    args = locals()
    w = {n: args[n] for n in WEIGHTS}
    mom = {n: args["m_" + n] for n in WEIGHTS}
    var = {n: args["v_" + n] for n in WEIGHTS}

    seq, d = x.shape[1], x.shape[2]
    f_shard = ffn1_w_gate.shape[2]
    fp = _round_up(f_shard, LANES)
    r = rg_conv_b.shape[1]
    s5w = s5_d.shape[1]
    hd = r // RG_HEADS
    groups = s5w // S5_GROUP
    nb = groups // GROUPS_PER_BLOCK
    t_rows = _round_up(N_META + seq, LANES)
    tm, tr, tk = t_rows // 4, t_rows // 8, t_rows // 2
    trn = _tile(t_rows, max(t_rows // 32, 16), 16)
    dims = dict(T=t_rows, D=d, FP=fp, TM=tm, TR=tr, TRN=trn)
    cx, cy, cc = lax.axis_index("x"), lax.axis_index("y"), lax.axis_index("c")
    chip = 2 * cx + cy
    pos = jnp.stack([chip, cc]).astype(jnp.int32)

    def gather(nm, buf):
        return _run_comm(f"ag_{nm}", _gather_comm(buf))[0]

    def gather_f32(nm, local):
        buf = lax.dynamic_update_slice(jnp.zeros((N_CHIPS,) + local.shape, F32), local[None], (chip, 0, 0))
        return gather(nm, buf)

    slot = {}
    transposed = ('ffn1_w_gate', 'ffn1_w_up', 'ffn2_w_gate', 'ffn2_w_up')
    local2d = lambda tree, nm: jnp.swapaxes(tree[nm][0], 0, 1) if nm in transposed else tree[nm][0]
    for nm in transposed:
        slot[nm] = _cast_pad(f"cast_{nm}", local2d(w, nm), fp, d, pos)
    for nm in ('ffn1_w_down', 'ffn2_w_down'):
        slot[nm] = _cast_pad(f"cast_{nm}", w[nm][0], fp, d, pos)
    slot['w_in'] = _cast_pad("cast_w_in", w_in[0], d, w_in.shape[2], pos)
    slot['w_out'] = _cast_pad("cast_w_out", w_out[0], w_out.shape[1], d, pos)
    slot['s5_glu_w'] = _cast_pad("cast_glu", s5_glu_w[0], s5_glu_w.shape[1], s5w, pos)
    rgw_local = jnp.concatenate([rg_w_a[0], rg_w_x[0]], axis=0).reshape(2 * RG_HEADS * (hd // N_CHIPS), hd)
    rgw = gather('rgw', _cast_pad("cast_rgw", rgw_local, rgw_local.shape[0], hd, pos))
    rgw = rgw.reshape(N_CHIPS, 2, RG_HEADS, hd // N_CHIPS, hd).transpose(1, 2, 0, 3, 4).reshape(2, RG_HEADS, hd, hd)
    w_a16, w_x16 = rgw[0], rgw[1]
    meta_full = gather_f32('meta', meta_tokens).transpose(1, 0, 2).reshape(N_META, d)
    conv_w2, b_a2, b_x2 = _gather_small_shards(rg_conv_w[0], rg_b_a[0], rg_b_x[0], gather_f32)
    wfull = {'ffn1_w_gate': gather('ffn1_w_gate', slot['ffn1_w_gate'])}

    pad_rows = t_rows - N_META - seq
    h0 = jnp.concatenate([meta_full, x[0], jnp.zeros((pad_rows, d), F32)], axis=0)
    tgt = jnp.concatenate([jnp.zeros((N_META, d), F32), loss_target[0], jnp.zeros((pad_rows, d), F32)], axis=0)

    h1, sv1, got = _ffn_fwd(
        "ffn1", h0, ffn1_norm, wfull['ffn1_w_gate'], 0, 0, dims,
        host={'gate': [_gather_comm(slot['ffn1_w_up'])], 'up': [_gather_comm(slot['ffn1_w_down'])],
              'down': [_gather_comm(slot['w_in']), _gather_comm(slot['s5_glu_w'])]})
    wfull['w_in'] = got['down'][0]
    wfull['s5_glu_w'] = got['down'][1].reshape(s5w, s5w)

    n2, = _rowmap("mix_norm", lambda xv, g: _rms(xv, g).astype(BF16), (1, t_rows // trn),
                  [_rows(h1, trn), _full(mix_norm)], [_o((t_rows, d), BF16, trn)])
    pw = w_in.shape[2]
    tnp = _tile(pw, 1536)
    nppb = pw // tnp
    proj, w_out_full = _matmul(
        "in_proj", 'nn', n2, (tm, d // 2), lambda i, j, k: (i, k),
        wfull['w_in'], (None, d // 2, tnp), lambda i, j, k: (j // nppb, k, j % nppb),
        (t_rows // tm, N_CHIPS * nppb, 2), [((t_rows, 3 * r), F32, (tm, tnp), lambda i, j: (i, j))],
        lambda acc: acc, comms=[_gather_comm(slot['w_out'])])
    wfull['w_out'] = w_out_full.reshape(d, d)

    xc = _conv_fwd("rg_conv", proj, conv_w2, rg_conv_b, r, tr)

    def head_cols(arr):
        return (arr, (tr, hd), lambda j, i: (i, j))

    def head_w(arr):
        return (arr, (None, hd, hd), lambda j, i: (j, 0, 0))

    def head_vec(arr):
        return (arr, (1, hd), lambda j, i: (0, j))

    def gates_fwd(xcv, wa, wx, ba, bx, lam):
        x16 = xcv.astype(BF16)
        pre_r = jnp.dot(x16, wa, preferred_element_type=F32) + ba
        pre_i = jnp.dot(x16, wx, preferred_element_type=F32) + bx
        return _rg_gate_math(xcv, pre_r, pre_i, _softplus_neg(lam))

    gate_params = [head_w(w_a16), head_w(w_x16), head_vec(b_a2), head_vec(b_x2), head_vec(rg_lambda)]
    head_out = ((t_rows, r), F32, (tr, hd), lambda j, i: (i, j))
    a_dec, bxv = _rowmap("rg_gates", gates_fwd, (RG_HEADS, t_rows // tr), [head_cols(xc)] + gate_params,
                         [head_out, head_out])
    h_rg = _linear_scan("rg_scan", a_dec, bxv, tr)

    expand = jnp.repeat(jnp.eye(S5_STATE, dtype=F32), S5_GROUP, axis=1)
    b_re2 = s5_b_re[0].reshape(groups, S5_STATE * S5_GROUP)
    b_im2 = s5_b_im[0].reshape(groups, S5_STATE * S5_GROUP)
    log_dt2 = s5_log_dt[0].reshape(groups, 1)
    lbr, lbi, bbr, bbi = _s5_disc("s5_disc", s5_lambda_re[0], s5_lambda_im[0], log_dt2, b_re2, b_im2, expand)

    def to_bblk(m):
        return _blockdiag(m.reshape(nb, GROUPS_PER_BLOCK, S5_STATE, S5_GROUP).transpose(0, 1, 3, 2))

    def to_cblk(m):
        return _blockdiag(m.reshape(nb, GROUPS_PER_BLOCK, S5_GROUP, S5_STATE).transpose(0, 1, 3, 2))

    bblk = jnp.concatenate([to_bblk(bbr), to_bblk(bbi)], axis=-1).astype(BF16)
    cblk = jnp.concatenate([to_cblk(s5_c_re[0]), -to_cblk(s5_c_im[0])], axis=-2).astype(BF16)
    hw = GROUPS_PER_BLOCK * S5_STATE
    lamblk = jnp.concatenate([lbr.reshape(nb, 1, hw), lbi.reshape(nb, 1, hw)], axis=-1)
    col0 = 2 * r // LANES
    y_pre, states = _s5_fwd("s5_fwd", proj, col0, bblk, cblk, lamblk, s5_d, s5w, tr)

    z16, = _rowmap("s5_gelu", lambda yv: _gelu(yv).astype(BF16), (1, t_rows // trn),
                   [_rows(y_pre, trn)], [_o((t_rows, s5w), BF16, trn)])
    tns = _tile(s5w, 1024)

    def glu_fwd(acc, yv, b):
        gl = acc + b
        return _gelu(yv) * _sigmoid(gl), gl

    y_s5, gl = _matmul("s5_glu", 'nn', z16, (tm, s5w), lambda i, j, k: (i, 0),
                       wfull['s5_glu_w'], (s5w, tns), lambda i, j, k: (0, j),
                       (t_rows // tm, s5w // tns, 1), [((t_rows, s5w), F32, (tm, tns), lambda i, j: (i, j))] * 2,
                       glu_fwd, extras=[(y_pre, (tm, tns), lambda i, j: (i, j)), (s5_glu_b, (1, tns), lambda i, j: (0, j))])

    def mix_out(hv, gv, yv, g1, g2):
        return jnp.concatenate([_rms(hv * _gelu(gv), g1), _rms(yv, g2)], axis=1)

    gate_in = (proj, (trn, r), lambda j, i: (i, 1))
    ycat16, = _rowmap("mix_out", lambda *a: mix_out(*a).astype(BF16), (1, t_rows // trn),
                      [_rows(h_rg, trn), gate_in, _rows(y_s5, trn), _full(rg_out_norm), _full(s5_out_norm)],
                      [_o((t_rows, r + s5w), BF16, trn)])
    tnd = _tile(d, 1024)
    h2, wfull['ffn2_w_gate'] = _matmul(
        "out_proj", 'nn', ycat16, (tm, (r + s5w) // 2), lambda i, j, k: (i, k),
        wfull['w_out'], ((r + s5w) // 2, tnd), lambda i, j, k: (k, j),
        (t_rows // tm, d // tnd, 2), [((t_rows, d), F32, (tm, tnd), lambda i, j: (i, j))],
        lambda acc, hin: hin + acc, extras=[(h1, (tm, tnd), lambda i, j: (i, j))],
        comms=[_gather_comm(slot['ffn2_w_gate'])])

    h3, sv2, _ = _ffn_fwd(
        "ffn2", h2, ffn2_norm, wfull['ffn2_w_gate'], 0, 0, dims,
        host={'gate': [_gather_comm(slot['ffn2_w_up'])], 'up': [_gather_comm(slot['ffn2_w_down'])]})

    fnorm2 = final_norm.reshape(1, d)

    def head(xv, tv, g):
        i = pl.program_id(1)
        rowi = lax.broadcasted_iota(jnp.int32, (trn, 1), 0) + i * trn
        mask = jnp.where((rowi >= N_META) & (rowi < N_META + seq), 1.0, 0.0)
        out, vjp = jax.vjp(_rms, xv, g)
        err = (out - tv) * mask
        dx, dg = vjp(err * (1.0 / d))
        return dx, dx.astype(BF16), jnp.sum(err * err, axis=0, keepdims=True), dg

    dh3, dh3_16, loss_cols, d_final = _rowmap(
        "loss_head", head, (1, t_rows // trn), [_rows(h3, trn), _rows(tgt, trn), _full(fnorm2)],
        [_o((t_rows, d), F32, trn), _o((t_rows, d), BF16, trn)], [_acc((1, d)), _acc((1, d))])
    loss = lax.psum(0.5 * jnp.sum(loss_cols) / d, ("x", "y", "c"))

    small = {}
    dh2, dh2_16, small['ffn2_norm'], parts, _ = _ffn_bwd(
        "ffn2b", ('ffn2_w_gate', 'ffn2_w_up', 'ffn2_w_down'), dh3, dh3_16, sv2, ffn2_norm, wfull['ffn2_w_gate'],
        dims, pos)

    dycat, = _matmul("d_out_proj", 'nt', dh2_16, (tm, d // 2), lambda i, j, k: (i, k),
                     wfull['w_out'], (tnd, d // 2), lambda i, j, k: (j, k),
                     (t_rows // tm, (r + s5w) // tnd, 2), [((t_rows, r + s5w), F32, (tm, tnd), lambda i, j: (i, j))],
                     lambda acc: acc)
    wo_rows = (r + s5w) // N_CHIPS
    tno = _tile(wo_rows, 1024)
    npo = wo_rows // tno
    dw_out, = _matmul("dw_out", 'tn', ycat16, (tk, tno), lambda i, j, k: (k, i),
                      dh2_16, (tk, tnd), lambda i, j, k: (k, j),
                      ((r + s5w) // tno, d // tnd, 2),
                      [((N_CHIPS, wo_rows, d), BF16, (None, tno, tnd), lambda i, j: (i // npo, i % npo, j))],
                      lambda acc: acc)
    pair_w_out = _pair('w_out', dw_out, pos)

    def mix_out_bwd(hv, gv, yv, dyc, g1, g2):
        _, vjp = jax.vjp(mix_out, hv, gv, yv, g1, g2)
        return vjp(dyc)

    dh_out, dgate_rg, dy_s5, small['rg_out_norm'], small['s5_out_norm'] = _rowmap(
        "mix_out_bwd", mix_out_bwd, (1, t_rows // trn),
        [_rows(h_rg, trn), gate_in, _rows(y_s5, trn), _rows(dycat, trn), _full(rg_out_norm), _full(s5_out_norm)],
        [_o((t_rows, r), F32, trn), _o((t_rows, r), F32, trn), _o((t_rows, s5w), F32, trn)],
        [_acc((1, r)), _acc((1, s5w))])

    def glu_bwd(dyv, yv, glv):
        zv = _gelu(yv)
        sg = _sigmoid(glv)
        dgl = dyv * zv * sg * (1.0 - sg)
        return dyv * sg, dgl.astype(BF16), jnp.sum(dgl, axis=0, keepdims=True)

    dz_dir, dgl16, small['s5_glu_b'] = _rowmap(
        "s5_glu_bwd", glu_bwd, (1, t_rows // trn), [_rows(dy_s5, trn), _rows(y_pre, trn), _rows(gl, trn)],
        [_o((t_rows, s5w), F32, trn), _o((t_rows, s5w), BF16, trn)], [_acc((1, s5w))])

    def dgelu(acc, dzd, yv):
        _, vjp = jax.vjp(_gelu, yv)
        return vjp(acc + dzd)[0]

    dy_pre, = _matmul("s5_dz", 'nt', dgl16, (tm, s5w), lambda i, j, k: (i, 0),
                      wfull['s5_glu_w'], (tns, s5w), lambda i, j, k: (j, 0),
                      (t_rows // tm, s5w // tns, 1), [((t_rows, s5w), F32, (tm, tns), lambda i, j: (i, j))],
                      dgelu, extras=[(dz_dir, (tm, tns), lambda i, j: (i, j)), (y_pre, (tm, tns), lambda i, j: (i, j))])
    gl_rows = s5w // N_CHIPS
    tng = _tile(gl_rows, 1024)
    npg = gl_rows // tng
    dw_glu, = _matmul("dw_glu", 'tn', z16, (tk, tng), lambda i, j, k: (k, i),
                      dgl16, (tk, tns), lambda i, j, k: (k, j),
                      (s5w // tng, s5w // tns, 2),
                      [((N_CHIPS, gl_rows, s5w), BF16, (None, tng, tns), lambda i, j: (i // npg, i % npg, j))],
                      lambda acc: acc)
    pair_glu = _pair('s5_glu_w', dw_glu, pos)
    du_s5, dbblk, dcblk, dlamblk, small['s5_d'] = _s5_bwd(
        "s5_bwd", dy_pre, proj, col0, states, bblk, cblk, lamblk, s5_d, s5w, tr)

    def from_bblk(m):
        return _blockdiag_take(m, S5_STATE, S5_GROUP).reshape(groups, S5_STATE * S5_GROUP)

    dbbr, dbbi = from_bblk(dbblk[:, :hw]), from_bblk(dbblk[:, hw:])
    dcs = lambda m: _blockdiag_take(m, S5_STATE, S5_GROUP).transpose(0, 1, 3, 2).reshape(1, groups, S5_GROUP, S5_STATE)
    small['s5_c_re'], small['s5_c_im'] = dcs(dcblk[:, :hw]), -dcs(dcblk[:, hw:])
    dlbr, dlbi = dlamblk[:, 0, :hw].reshape(groups, S5_STATE), dlamblk[:, 0, hw:].reshape(groups, S5_STATE)
    g_lre, g_lim, g_ldt, g_bre, g_bim = _s5_disc(
        "s5_disc_bwd", s5_lambda_re[0], s5_lambda_im[0], log_dt2, b_re2, b_im2, expand, cots=(dlbr, dlbi, dbbr, dbbi))
    small['s5_lambda_re'], small['s5_lambda_im'], small['s5_log_dt'] = g_lre[None], g_lim[None], g_ldt.reshape(1, groups)
    small['s5_b_re'] = g_bre.reshape(s5_b_re.shape)
    small['s5_b_im'] = g_bim.reshape(s5_b_im.shape)

    dh_rg = _linear_scan("rg_scan_bwd", a_dec, dh_out, tr, reverse=True, shift_a=True)

    hb = tr // 8

    def gates_bwd(xcv, dhv, hv, hprev, wa, wx, ba, bx, lam):
        i = pl.program_id(1)
        row = lax.broadcasted_iota(jnp.int32, (tr, hd), 0)
        first = jnp.where(i > 0, hprev[7:8, :], 0.0)
        h_prev = jnp.where(row == 0, jnp.broadcast_to(first, (tr, hd)), pltpu.roll(hv, 1, 0))
        x16 = xcv.astype(BF16)
        pre_r = jnp.dot(x16, wa, preferred_element_type=F32) + ba
        pre_i = jnp.dot(x16, wx, preferred_element_type=F32) + bx
        _, vjp = jax.vjp(_rg_gate_math, xcv, pre_r, pre_i, _softplus_neg(lam))
        dxc_, dpr, dpi, dsp = vjp((dhv * h_prev, dhv))
        dpr16, dpi16 = dpr.astype(BF16), dpi.astype(BF16)
        dxc_ = (dxc_ + lax.dot_general(dpr16, wa, _DN['nt'], preferred_element_type=F32)
                + lax.dot_general(dpi16, wx, _DN['nt'], preferred_element_type=F32))
        dwa = lax.dot_general(x16, dpr16, _DN['tn'], preferred_element_type=F32)
        dwx = lax.dot_general(x16, dpi16, _DN['tn'], preferred_element_type=F32)
        dlam = dsp * (-_sigmoid(-lam))
        return (dxc_, dwa, dwx, jnp.sum(dpr, axis=0, keepdims=True), jnp.sum(dpi, axis=0, keepdims=True), dlam)

    head_acc_w = ((RG_HEADS, hd, hd), F32, (None, hd, hd), lambda j, i: (j, 0, 0))
    head_acc_v = ((1, r), F32, (1, hd), lambda j, i: (0, j))
    h_halo = (h_rg, (8, hd), lambda j, i: (jnp.maximum(i * hb - 1, 0), j))
    dxc, d_wa, d_wx, d_ba, d_bx, d_lam = _rowmap(
        "rg_gates_bwd", gates_bwd, (RG_HEADS, t_rows // tr),
        [head_cols(xc), head_cols(dh_rg), head_cols(h_rg), h_halo] + gate_params,
        [head_out], [head_acc_w, head_acc_w, head_acc_v, head_acc_v, head_acc_v])
    small['rg_w_a'], small['rg_w_x'] = d_wa[None], d_wx[None]
    small['rg_b_a'], small['rg_b_x'] = d_ba.reshape(1, RG_HEADS, hd), d_bx.reshape(1, RG_HEADS, hd)
    small['rg_lambda'] = d_lam
    du_rg, d_convw, small['rg_conv_b'] = _conv_bwd("rg_conv_bwd", dxc, proj, conv_w2, r, tr)
    small['rg_conv_w'] = d_convw[None]

    dproj16, = _rowmap("dproj", lambda a, b, c: jnp.concatenate([a, b, c], axis=1).astype(BF16), (1, t_rows // trn),
                       [_rows(du_rg, trn), _rows(dgate_rg, trn), _rows(du_s5, trn)], [_o((t_rows, 3 * r), BF16, trn)])
    dn2, land_w_out = _matmul(
        "d_in_proj", 'nt', dproj16, (tm, pw), lambda i, j, k: (i, k),
        wfull['w_in'], (None, tnd, pw), lambda i, j, k: (k, j, 0),
        (t_rows // tm, d // tnd, N_CHIPS), [((t_rows, d), F32, (tm, tnd), lambda i, j: (i, j))],
        lambda acc: acc, comms=[_scatter_comm(pair_w_out)])
    parts['w_out'] = (pair_w_out, land_w_out)
    dw_in, = _matmul("dw_in", 'tn', n2, (tk, tnd), lambda i, j, k: (k, i),
                     dproj16, (tk, tnp), lambda i, j, k: (k, j),
                     (d // tnd, N_CHIPS * nppb, 2),
                     [((N_CHIPS, d, pw), BF16, (None, tnd, tnp), lambda i, j: (j // nppb, i, j % nppb))],
                     lambda acc: acc)
    pair_w_in = _pair('w_in', dw_in, pos)

    def norm_bwd(xv, dnv, dhv, g):
        _, vjp = jax.vjp(_rms, xv, g)
        dx, dg = vjp(dnv)
        res = dhv + dx
        return res, res.astype(BF16), dg

    dh1, dh1_16, small['mix_norm'] = _rowmap(
        "mix_dnorm", norm_bwd, (1, t_rows // trn), [_rows(h1, trn), _rows(dn2, trn), _rows(dh2, trn), _full(mix_norm)],
        [_o((t_rows, d), F32, trn), _o((t_rows, d), BF16, trn)], [_acc((1, d))])

    dh0, _, small['ffn1_norm'], parts1, got = _ffn_bwd(
        "ffn1b", ('ffn1_w_gate', 'ffn1_w_up', 'ffn1_w_down'), dh1, dh1_16, sv1, ffn1_norm, wfull['ffn1_w_gate'],
        dims, pos, host={'dact': [_scatter_comm(pair_glu), _scatter_comm(pair_w_in)]})
    parts.update(parts1)
    parts['s5_glu_w'] = (pair_glu, got['dact'][0])
    parts['w_in'] = (pair_w_in, got['dact'][1])
    grad_x = dh0[N_META:N_META + seq][None]
    small['meta_tokens'] = dh0[:N_META]
    small['final_norm'] = d_final.reshape(d)

    halves = [_reduce_partials(f"red_{nm}", parts[nm][0], parts[nm][1], pos) for nm in BIG]
    wholes = _sibling_exchange("exchange", halves)
    gbig = {nm: wh.reshape(wh.shape[0] * wh.shape[1], wh.shape[2]) for nm, wh in zip(BIG, wholes)}

    small_names = [n for n in WEIGHTS if n not in BIG]
    full_shape = {n: small[n].shape for n in small_names}
    flat = jnp.concatenate([small[n].reshape(-1) for n in small_names])
    unit = 8 * 8 * LANES
    total = _round_up(flat.shape[0], unit)
    flat = jnp.concatenate([flat, jnp.zeros((total - flat.shape[0],), F32)])
    red = _all_reduce_small("ar_small", flat.reshape(total // LANES, LANES)).reshape(-1)
    gsmall = {}
    off = 0
    for n in small_names:
        size = math.prod(full_shape[n])
        gsmall[n] = _own_shard(n, red[off:off + size].reshape(full_shape[n]), w[n].shape, chip)
        off += size

    out_g, out_d, out_m, out_v = {}, {}, {}, {}
    for nm in BIG:
        w2 = local2d(w, nm)
        trw = _tile(w2.shape[0], 128 if w2.shape[1] < 4096 else 64, 8)
        res = _adamw(f"adamw_{nm}", w2, gbig[nm], local2d(mom, nm), local2d(var, nm), trw)
        back = (lambda a: jnp.swapaxes(a, 0, 1)[None]) if nm in transposed else (lambda a: a[None])
        out_g[nm], out_d[nm], out_m[nm], out_v[nm] = [back(a) for a in res]

    def pack(tree):
        fl = jnp.concatenate([tree[n].reshape(-1) for n in small_names])
        tot = _round_up(fl.shape[0], 8 * LANES)
        return jnp.concatenate([fl, jnp.zeros((tot - fl.shape[0],), F32)]).reshape(tot // LANES, LANES)

    wp, gp, mp, vp = pack(w), pack(gsmall), pack(mom), pack(var)
    res = _adamw("adamw_small", wp, gp, mp, vp, _tile(wp.shape[0], 512, 8))
    off = 0
    for n in small_names:
        size = math.prod(w[n].shape)
        for dst, src in zip((out_g, out_d, out_m, out_v), res):
            dst[n] = src.reshape(-1)[off:off + size].reshape(w[n].shape)
        off += size

    return (loss, grad_x, *[out_g[n] for n in WEIGHTS], *[out_d[n] for n in WEIGHTS],
            *[out_m[n] for n in WEIGHTS], *[out_v[n] for n in WEIGHTS])


def _gather_small_shards(conv_w, b_a, b_x, gather):
    cw = conv_w.shape[1]
    part = b_a.shape[1]

    def rows8(a):
        a = jnp.concatenate([a, jnp.zeros((a.shape[0], cw - a.shape[1]), F32)], axis=1)
        return jnp.concatenate([a, jnp.zeros((8 - a.shape[0], cw), F32)], axis=0) if a.shape[0] < 8 else a

    local = jnp.concatenate([rows8(conv_w), rows8(b_a), rows8(b_x), jnp.zeros((8, cw), F32)], axis=0)
    full = gather("rg_small", local)
    conv_full = full[:, :CONV_WIDTH].transpose(1, 0, 2).reshape(CONV_WIDTH, N_CHIPS * cw)
    bias = lambda k: full[:, 8 * k:8 * k + RG_HEADS, :part].transpose(1, 0, 2).reshape(1, RG_HEADS * N_CHIPS * part)
    return conv_full, bias(1), bias(2)


def _own_shard(name, g, local_shape, chip):
    if tuple(g.shape) == tuple(local_shape):
        return g
    axis = [k for k, (a, b) in enumerate(zip(g.shape, local_shape)) if a != b][0]
    size = local_shape[axis]
    return lax.dynamic_slice_in_dim(g, chip * size, size, axis=axis)
```

```python
import functools
import math

import jax
import jax.numpy as jnp
from jax import lax
from jax.experimental import pallas as pl
from jax.experimental.pallas import tpu as pltpu

F32 = jnp.float32
BF16 = jnp.bfloat16
MESH = pl.DeviceIdType.MESH

N_META = 16
RG_HEADS = 8
CONV_WIDTH = 4
RG_C = 8.0
S5_GROUP = 16
S5_STATE = 64
GROUPS_PER_BLOCK = 8
EPS = 1e-6
N_CHIPS = 4
LANES = 128
VMEM_LIMIT = 56 * 1024 * 1024

ADAM_LR = 0.001
ADAM_B1 = 0.9
ADAM_B2 = 0.999
ADAM_EPS = 1e-08
ADAM_WD = 0.01
ADAM_STEP = 10

WEIGHTS = ['meta_tokens', 'ffn1_norm', 'ffn1_w_gate', 'ffn1_w_up', 'ffn1_w_down', 'mix_norm', 'w_in', 'rg_conv_w',
           'rg_conv_b', 'rg_w_a', 'rg_b_a', 'rg_w_x', 'rg_b_x', 'rg_lambda', 's5_lambda_re', 's5_lambda_im',
           's5_log_dt', 's5_b_re', 's5_b_im', 's5_c_re', 's5_c_im', 's5_d', 's5_glu_w', 's5_glu_b', 'rg_out_norm',
           's5_out_norm', 'w_out', 'ffn2_norm', 'ffn2_w_gate', 'ffn2_w_up', 'ffn2_w_down', 'final_norm']
BIG = ('ffn1_w_gate', 'ffn1_w_up', 'ffn1_w_down', 'w_in', 's5_glu_w', 'w_out', 'ffn2_w_gate', 'ffn2_w_up',
       'ffn2_w_down')

_DN = {'nn': (((1,), (0,)), ((), ())), 'nt': (((1,), (1,)), ((), ())), 'tn': (((0,), (0,)), ((), ()))}


def _round_up(n, m):
    return (n + m - 1) // m * m


def _tile(n, pref, unit=LANES):
    best = None
    for t in range(unit, min(n, pref) + 1, unit):
        if n % t == 0:
            best = t
    return best if best is not None else n


def _params(sem=None):
    return pltpu.CompilerParams(dimension_semantics=sem, vmem_limit_bytes=VMEM_LIMIT)


def _rms(x, g):
    return x * lax.rsqrt(jnp.mean(x * x, axis=-1, keepdims=True) + EPS) * g


def _sigmoid(x):
    return 0.5 * (jnp.tanh(0.5 * x) + 1.0)


def _gelu(x):
    return 0.5 * x * (1.0 + jnp.tanh(math.sqrt(2.0 / math.pi) * (x + 0.044715 * (x * x * x))))


def _silu(x):
    return x * _sigmoid(x)


def _expm1(x):
    series = x * (1.0 + x * (1.0 / 2) * (1.0 + x * (1.0 / 3) * (1.0 + x * (1.0 / 4) * (1.0 + x * (1.0 / 5) * (1.0 + x * (1.0 / 6))))))
    return jnp.where(jnp.abs(x) < 0.3, series, jnp.exp(x) - 1.0)


def _softplus_neg(lam):
    m = jnp.maximum(-lam, 0.0)
    e = jnp.exp(-jnp.abs(lam))
    w = 1.0 + e
    log1p = jnp.where(w == 1.0, e, jnp.log(w) * (e / jnp.where(w == 1.0, 1.0, w - 1.0)))
    return m + log1p


def _rg_gate_math(xc, pre_r, pre_i, sp):
    r = 1.0 / (1.0 + jnp.exp(-pre_r))
    i = 1.0 / (1.0 + jnp.exp(-pre_i))
    log_a = -RG_C * r * sp
    a = jnp.exp(log_a)
    mult = jnp.sqrt(-_expm1(2.0 * log_a))
    return a, mult * i * xc


class _Comm:
    def __init__(self, arrays, out_shapes, aliased, sems, start, finish, mid=None):
        self.arrays, self.out_shapes, self.aliased, self.sems = arrays, out_shapes, aliased, sems
        self.start, self.finish, self.mid = start, finish, mid


def _matmul(name, mode, a, a_blk, a_map, b, b_blk, b_map, grid, outs, epilogue, extras=(), comms=()):
    ni, nj, nk = grid
    ne, no = len(extras), len(outs)
    sq = lambda blk: tuple(d for d in blk if d is not None)
    ab, bb = sq(a_blk), sq(b_blk)
    acc_shape = {'nn': (ab[0], bb[1]), 'nt': (ab[0], bb[0]), 'tn': (ab[1], bb[1])}[mode]
    n_cin = sum(len(cm.arrays) for cm in comms)
    n_cout = sum(len(cm.out_shapes) for cm in comms)
    n_acc = 1 if nk > 1 else 0

    def comm_refs(refs):
        cin = refs[2 + ne:2 + ne + n_cin]
        cout = refs[2 + ne + n_cin + no:2 + ne + n_cin + no + n_cout]
        csem = refs[2 + ne + n_cin + no + n_cout + n_acc:]
        for cm in comms:
            yield cm, cin[:len(cm.arrays)], cout[:len(cm.out_shapes)], csem[:len(cm.sems)]
            cin, cout, csem = cin[len(cm.arrays):], cout[len(cm.out_shapes):], csem[len(cm.sems):]

    def body(*refs):
        a_ref, b_ref = refs[0], refs[1]
        ex = refs[2:2 + ne]
        out = refs[2 + ne + n_cin:2 + ne + n_cin + no]
        if comms:
            @pl.when((pl.program_id(0) == 0) & (pl.program_id(1) == 0) & (pl.program_id(2) == 0))
            def _():
                for cm, cin, cout, csem in comm_refs(refs):
                    cm.start(cin, cout, csem)

            if any(cm.mid for cm in comms):
                step = (pl.program_id(0) * nj + pl.program_id(1)) * nk + pl.program_id(2)

                @pl.when(step == (3 * ni * nj * nk) // 5)
                def _():
                    for cm, cin, cout, csem in comm_refs(refs):
                        if cm.mid:
                            cm.mid(cin, cout, csem)

        part = lax.dot_general(a_ref[...], b_ref[...], _DN[mode], preferred_element_type=F32)

        def finish(acc):
            res = epilogue(acc, *[e[...] for e in ex])
            if not isinstance(res, tuple):
                res = (res,)
            for o, r in zip(out, res):
                o[...] = r.astype(o.dtype)

        if nk == 1:
            finish(part)
        else:
            acc_ref = refs[2 + ne + n_cin + no + n_cout]
            k = pl.program_id(2)

            @pl.when(k == 0)
            def _():
                acc_ref[...] = part

            @pl.when(k > 0)
            def _():
                acc_ref[...] += part

            @pl.when(k == nk - 1)
            def _():
                finish(acc_ref[...])

        if comms:
            @pl.when((pl.program_id(0) == ni - 1) & (pl.program_id(1) == nj - 1) & (pl.program_id(2) == nk - 1))
            def _():
                for cm, cin, cout, csem in comm_refs(refs):
                    cm.finish(cin, cout, csem)

    hbm = pl.BlockSpec(memory_space=pltpu.HBM)
    in_specs = [pl.BlockSpec(a_blk, a_map), pl.BlockSpec(b_blk, b_map)]
    in_specs += [pl.BlockSpec(blk, functools.partial(lambda m, i, j, k: m(i, j), m)) for _, blk, m in extras]
    in_specs += [hbm] * n_cin
    out_specs = [pl.BlockSpec(blk, functools.partial(lambda m, i, j, k: m(i, j), m)) for _, _, blk, m in outs]
    out_specs += [hbm] * n_cout
    aliases, cin_at, cout_at = {}, 2 + ne, no
    for cm in comms:
        if cm.aliased:
            aliases.update({cin_at + k: cout_at + k for k in range(len(cm.arrays))})
        cin_at, cout_at = cin_at + len(cm.arrays), cout_at + len(cm.out_shapes)
    res = pl.pallas_call(
        body, name=name, grid=grid, in_specs=in_specs, out_specs=out_specs,
        out_shape=[jax.ShapeDtypeStruct(s, d) for s, d, _, _ in outs] + [s for cm in comms for s in cm.out_shapes],
        scratch_shapes=([pltpu.VMEM(acc_shape, F32)] if nk > 1 else [])
        + [pltpu.SemaphoreType.DMA((n,)) for cm in comms for n in cm.sems],
        input_output_aliases=aliases,
        compiler_params=_params(("arbitrary",) * 3 if comms else ("parallel", "parallel", "arbitrary")),
    )(a, b, *[e for e, _, _ in extras], *[arr for cm in comms for arr in cm.arrays])
    return res


def _rows(arr, tr, tc=None, col0=0):
    if tc is None:
        return (arr, (tr, arr.shape[1]), lambda j, i: (i, 0))
    return (arr, (tr, tc), lambda j, i: (i, col0 + j))


def _full(arr):
    nd = arr.ndim
    return (arr, arr.shape, lambda j, i: (0,) * nd)


def _cols(arr, tc):
    return (arr, (arr.shape[0], tc), lambda j, i: (0, j))


def _rowmap(name, fn, grid, ins, outs, accs=(), scratch=()):
    nj, ni = grid
    n_in, n_out, n_acc = len(ins), len(outs), len(accs)

    def body(*refs):
        vals = [r[...] for r in refs[:n_in]]
        o_refs = refs[n_in:n_in + n_out]
        a_refs = refs[n_in + n_out:n_in + n_out + n_acc]
        s_refs = refs[n_in + n_out + n_acc:]
        res = fn(*vals, *s_refs)
        if not isinstance(res, tuple):
            res = (res,)
        for o, r in zip(o_refs, res[:n_out]):
            o[...] = r.astype(o.dtype)
        i = pl.program_id(1)
        for a_ref, r in zip(a_refs, res[n_out:]):
            @pl.when(i == 0)
            def _(a_ref=a_ref, r=r):
                a_ref[...] = r.astype(a_ref.dtype)

            @pl.when(i > 0)
            def _(a_ref=a_ref, r=r):
                a_ref[...] += r.astype(a_ref.dtype)

    res = pl.pallas_call(
        body, name=name, grid=grid,
        in_specs=[pl.BlockSpec(blk, m) for _, blk, m in ins],
        out_specs=[pl.BlockSpec(blk, m) for _, _, blk, m in list(outs) + list(accs)],
        out_shape=[jax.ShapeDtypeStruct(s, d) for s, d, _, _ in list(outs) + list(accs)],
        scratch_shapes=list(scratch),
        compiler_params=_params(("parallel", "arbitrary")),
    )(*[a for a, _, _ in ins])
    return res


def _o(shape, dtype, tr, tc=None):
    if tc is None:
        return (shape, dtype, (tr, shape[1]), lambda j, i: (i, 0))
    return (shape, dtype, (tr, tc), lambda j, i: (i, j))


def _acc(shape, tc=None):
    if tc is None:
        nd = len(shape)
        return (shape, F32, shape, lambda j, i: (0,) * nd)
    return (shape, F32, (shape[0], tc), lambda j, i: (0, j))


def _position():
    x, y, c = lax.axis_index("x"), lax.axis_index("y"), lax.axis_index("c")
    return x, y, c


def _relay_gather_comm(buf):
    _, rows, cols = buf.shape
    rh = rows // 2
    rq = rh // 2

    def plan(out_ref, sems):
        x, y, c = _position()
        own, cx, cy, cd = 2 * x + y, 2 * (1 - x) + y, 2 * x + (1 - y), 2 * (1 - x) + (1 - y)
        to_x, to_y, sibling = (1 - x, y, c), (x, 1 - y, c), (x, y, 1 - c)
        half, other = pl.ds(c * rh, rh), pl.ds((1 - c) * rh, rh)
        q0, q1 = pl.ds(c * rh, rq), pl.ds(c * rh + rq, rq)

        def copy(pair, k, chip_index, rows_, to):
            return pltpu.make_async_remote_copy(
                src_ref=out_ref.at[chip_index, rows_], dst_ref=out_ref.at[chip_index, rows_],
                send_sem=sems[pair].at[k], recv_sem=sems[pair + 1].at[k], device_id=to, device_id_type=MESH)

        return dict(
            own_x=copy(0, 0, own, half, to_x), own_y=copy(0, 1, own, half, to_y),
            in_x=copy(0, 0, cx, half, to_x), in_y=copy(0, 1, cy, half, to_y),
            fwd_xy=copy(0, 2, cx, q0, to_y), fwd_yx=copy(0, 3, cy, q1, to_x),
            in_d0=copy(0, 2, cd, q0, to_y), in_d1=copy(0, 3, cd, q1, to_x),
            d2d_out=[copy(2, j, ch, half, sibling) for j, ch in enumerate((cx, cy, cd))],
            d2d_in=[copy(2, j, ch, other, sibling) for j, ch in enumerate((cx, cy, cd))])

    def start(ins, outs, sems):
        p = plan(outs[0], sems)
        p['own_x'].start()
        p['own_y'].start()

    def mid(ins, outs, sems):
        p = plan(outs[0], sems)
        p['in_x'].wait_recv()
        p['fwd_xy'].start()
        p['d2d_out'][0].start()
        p['in_y'].wait_recv()
        p['fwd_yx'].start()
        p['d2d_out'][1].start()

    def finish(ins, outs, sems):
        p = plan(outs[0], sems)
        p['in_d0'].wait_recv()
        p['in_d1'].wait_recv()
        p['d2d_out'][2].start()
        for cp in p['d2d_in']:
            cp.wait_recv()
        for cp in [p['own_x'], p['own_y'], p['fwd_xy'], p['fwd_yx']] + p['d2d_out']:
            cp.wait_send()

    return _Comm([buf], [jax.ShapeDtypeStruct(buf.shape, buf.dtype)], True, [4, 4, 3, 3], start, finish, mid)


def _gather_comm(buf):
    _, rows, cols = buf.shape
    rh = rows // 2
    if buf.dtype == BF16 and rh % 32 == 0:
        return _relay_gather_comm(buf)

    def plan(out_ref, sems):
        x, y, c = _position()
        chips = [(1 - x, y), (x, 1 - y), (1 - x, 1 - y)]

        def copy(pair, j, chip_index, half, to):
            return pltpu.make_async_remote_copy(
                src_ref=out_ref.at[chip_index, half], dst_ref=out_ref.at[chip_index, half],
                send_sem=sems[pair].at[j], recv_sem=sems[pair + 1].at[j], device_id=to, device_id_type=MESH)

        mine, other = pl.ds(c * rh, rh), pl.ds((1 - c) * rh, rh)
        ici_out = [copy(0, j, 2 * x + y, mine, (px, py, c)) for j, (px, py) in enumerate(chips)]
        ici_in = [copy(0, j, 2 * px + py, mine, (px, py, c)) for j, (px, py) in enumerate(chips)]
        d2d_out = [copy(2, j, 2 * px + py, mine, (x, y, 1 - c)) for j, (px, py) in enumerate(chips)]
        d2d_in = [copy(2, j, 2 * px + py, other, (x, y, 1 - c)) for j, (px, py) in enumerate(chips)]
        return ici_out, ici_in, d2d_out, d2d_in

    def start(ins, outs, sems):
        for cp in plan(outs[0], sems)[0]:
            cp.start()

    def finish(ins, outs, sems):
        ici_out, ici_in, d2d_out, d2d_in = plan(outs[0], sems)
        for arrived, fwd in zip(ici_in, d2d_out):
            arrived.wait_recv()
            fwd.start()
        for cp in d2d_in:
            cp.wait_recv()
        for cp in ici_out + d2d_out:
            cp.wait_send()

    return _Comm([buf], [jax.ShapeDtypeStruct(buf.shape, buf.dtype)], True, [3, 3, 3, 3], start, finish)


def _scatter_comm(p):
    _, rh, cols = p.shape

    def plan(p_ref, land_ref, sems):
        x, y, c = _position()
        chips = [(1 - x, y), (x, 1 - y), (1 - x, 1 - y)]

        def copy(j, chip_index, to):
            return pltpu.make_async_remote_copy(
                src_ref=p_ref.at[chip_index], dst_ref=land_ref.at[j],
                send_sem=sems[0].at[j], recv_sem=sems[1].at[j], device_id=to, device_id_type=MESH)

        out = [copy(j, 2 * px + py, (px, py, c)) for j, (px, py) in enumerate(chips)]
        arrive = [copy(j, 2 * x + y, (x, y, c)) for j in range(3)]
        return out, arrive

    def start(ins, outs, sems):
        for cp in plan(ins[0], outs[0], sems)[0]:
            cp.start()

    def finish(ins, outs, sems):
        out, arrive = plan(ins[0], outs[0], sems)
        for cp in arrive:
            cp.wait_recv()
        for cp in out:
            cp.wait_send()

    return _Comm([p], [jax.ShapeDtypeStruct((3, rh, cols), p.dtype)], False, [3, 3], start, finish)


def _run_comm(name, cm):
    n_in, n_out = len(cm.arrays), len(cm.out_shapes)

    def body(*refs):
        ins, outs, sems = refs[:n_in], refs[n_in:n_in + n_out], refs[n_in + n_out:]
        cm.start(ins, outs, sems)
        if cm.mid:
            cm.mid(ins, outs, sems)
        cm.finish(ins, outs, sems)

    hbm = pl.BlockSpec(memory_space=pltpu.HBM)
    return pl.pallas_call(
        body, name=name, in_specs=[hbm] * n_in, out_specs=[hbm] * n_out, out_shape=list(cm.out_shapes),
        input_output_aliases={k: k for k in range(n_in)} if cm.aliased else {},
        scratch_shapes=[pltpu.SemaphoreType.DMA((n,)) for n in cm.sems],
    )(*cm.arrays)


def _swap_comm(g):
    n, rows, cols = g.shape
    rh = rows // 2

    def plan(g_ref, land_ref, sems):
        x, y, c = _position()

        def copy(k, half):
            return pltpu.make_async_remote_copy(
                src_ref=g_ref.at[k, pl.ds(half * rh, rh)], dst_ref=land_ref.at[k],
                send_sem=sems[0].at[k], recv_sem=sems[1].at[k], device_id=(x, y, 1 - c), device_id_type=MESH)

        return [copy(k, 1 - c) for k in range(n)], [copy(k, c) for k in range(n)]

    def start(ins, outs, sems):
        for cp in plan(ins[0], outs[0], sems)[0]:
            cp.start()

    def finish(ins, outs, sems):
        out, arrive = plan(ins[0], outs[0], sems)
        for cp in arrive:
            cp.wait_recv()
        for cp in out:
            cp.wait_send()

    return _Comm([g], [jax.ShapeDtypeStruct((n, rh, cols), g.dtype)], False, [n, n], start, finish)


def _pair_sum(name, g, land, pos):
    n, rows, cols = g.shape
    rh = rows // 2
    tr = _tile(rh, 256, 16)
    nt = rh // tr

    def body(pos_ref, g_ref, l_ref, o_ref):
        o_ref[...] = (g_ref[...].astype(F32) + l_ref[...].astype(F32)).astype(o_ref.dtype)

    grid_spec = pltpu.PrefetchScalarGridSpec(
        num_scalar_prefetch=1, grid=(n, nt),
        in_specs=[pl.BlockSpec((None, tr, cols), lambda k, i, p: (k, p[1] * nt + i, 0)),
                  pl.BlockSpec((None, tr, cols), lambda k, i, p: (k, i, 0))],
        out_specs=pl.BlockSpec((None, tr, cols), lambda k, i, p: (k, i, 0)))
    return pl.pallas_call(
        body, name=name, grid_spec=grid_spec, out_shape=jax.ShapeDtypeStruct((n, rh, cols), g.dtype),
        compiler_params=_params(("parallel", "parallel")),
    )(pos, g, land)


def _reduce_partials(name, p, land, pos):
    _, rh, cols = p.shape
    tr = _tile(rh, 256, 16)
    nt = rh // tr

    def body(pos_ref, p_ref, l_ref, o_ref):
        acc = p_ref[...].astype(F32)
        for k in range(3):
            acc = acc + l_ref[k].astype(F32)
        o_ref[...] = acc

    grid_spec = pltpu.PrefetchScalarGridSpec(
        num_scalar_prefetch=1, grid=(nt,),
        in_specs=[pl.BlockSpec((None, tr, cols), lambda i, p_: (p_[0], i, 0)),
                  pl.BlockSpec((3, tr, cols), lambda i, p_: (0, i, 0))],
        out_specs=pl.BlockSpec((None, tr, cols), lambda i, p_: (p_[1], i, 0)))
    return pl.pallas_call(
        body, name=name, grid_spec=grid_spec, out_shape=jax.ShapeDtypeStruct((2, rh, cols), F32),
        compiler_params=_params(("parallel",)),
    )(pos, p, land)


def _exchange_comm(buf):
    def plan(out_ref, sems):
        x, y, c = _position()

        def copy(half):
            return pltpu.make_async_remote_copy(
                src_ref=out_ref.at[half], dst_ref=out_ref.at[half], send_sem=sems[0].at[0], recv_sem=sems[1].at[0],
                device_id=(x, y, 1 - c), device_id_type=MESH)

        return copy(c), copy(1 - c)

    def start(ins, outs, sems):
        plan(outs[0], sems)[0].start()

    def finish(ins, outs, sems):
        out, arrive = plan(outs[0], sems)
        arrive.wait_recv()
        out.wait_send()

    return _Comm([buf], [jax.ShapeDtypeStruct(buf.shape, buf.dtype)], True, [1, 1], start, finish)


def _all_reduce_small(name, buf):
    rows = buf.shape[0]
    p = rows // 8

    def body(buf_ref, out_ref, land_ref, red_ref, s1, r1, s2, r2):
        x, y, c = _position()
        me = 4 * x + 2 * y + c

        def peer(r):
            px = 1 - x if (r >> 2) & 1 else x
            py = 1 - y if (r >> 1) & 1 else y
            pc = 1 - c if r & 1 else c
            return (px, py, pc), 4 * px + 2 * py + pc

        firsts = []
        for r in range(1, 8):
            to, d = peer(r)
            cp = pltpu.make_async_remote_copy(
                src_ref=buf_ref.at[pl.ds(d * p, p)], dst_ref=land_ref.at[r - 1],
                send_sem=s1.at[r - 1], recv_sem=r1.at[r - 1], device_id=to, device_id_type=MESH)
            cp.start()
            firsts.append(cp)
        acc = buf_ref[pl.ds(me * p, p), :]
        for r in range(1, 8):
            firsts[r - 1].wait_recv()
            acc = acc + land_ref[r - 1]
        red_ref[...] = acc
        out_ref[pl.ds(me * p, p), :] = acc
        seconds = []
        for r in range(1, 8):
            to, d = peer(r)
            cp = pltpu.make_async_remote_copy(
                src_ref=red_ref, dst_ref=out_ref.at[pl.ds(me * p, p)],
                send_sem=s2.at[r - 1], recv_sem=r2.at[r - 1], device_id=to, device_id_type=MESH)
            cp.start()
            seconds.append(cp)
        for r in range(1, 8):
            to, d = peer(r)
            pltpu.make_async_remote_copy(
                src_ref=red_ref, dst_ref=out_ref.at[pl.ds(d * p, p)],
                send_sem=s2.at[r - 1], recv_sem=r2.at[r - 1], device_id=to, device_id_type=MESH).wait_recv()
        for cp in firsts + seconds:
            cp.wait_send()

    vmem = pl.BlockSpec(memory_space=pltpu.VMEM)
    return pl.pallas_call(
        body, name=name, in_specs=[vmem], out_specs=vmem,
        out_shape=jax.ShapeDtypeStruct(buf.shape, F32),
        scratch_shapes=[pltpu.VMEM((7, p, LANES), F32), pltpu.VMEM((p, LANES), F32),
                        pltpu.SemaphoreType.DMA((7,)), pltpu.SemaphoreType.DMA((7,)),
                        pltpu.SemaphoreType.DMA((7,)), pltpu.SemaphoreType.DMA((7,))],
        compiler_params=pltpu.CompilerParams(vmem_limit_bytes=VMEM_LIMIT),
    )(buf)


def _cast_pad(name, w, rows_to, cols_to, pos):
    rows, cols = w.shape
    tr = _tile(math.gcd(rows, rows_to), 256, 16)
    assert rows % tr == 0 and rows_to % tr == 0, (rows, rows_to, tr)
    n_src = rows // tr

    def body(pos_ref, w_ref, o_ref):
        i = pl.program_id(0)
        if cols_to > cols:
            o_ref[:, cols:] = jnp.zeros((tr, cols_to - cols), BF16)

        @pl.when(i < n_src)
        def _():
            o_ref[:, :cols] = w_ref[...].astype(BF16)

        if rows_to > rows:
            @pl.when(i >= n_src)
            def _():
                o_ref[:, :cols] = jnp.zeros((tr, cols), BF16)

    grid_spec = pltpu.PrefetchScalarGridSpec(
        num_scalar_prefetch=1, grid=(rows_to // tr,),
        in_specs=[pl.BlockSpec((tr, cols), lambda i, p: (jnp.minimum(i, n_src - 1), 0))],
        out_specs=pl.BlockSpec((None, tr, cols_to), lambda i, p: (p[0], i, 0)))
    return pl.pallas_call(
        body, name=name, grid_spec=grid_spec,
        out_shape=jax.ShapeDtypeStruct((N_CHIPS, rows_to, cols_to), BF16),
        compiler_params=_params(("parallel",)),
    )(pos, w)


def _scan_rows(a, b, row, tr, reverse):
    sh = 1
    while sh < tr:
        if reverse:
            valid = row < tr - sh
            a_s = pltpu.roll(a, tr - sh, 0)
            b_s = pltpu.roll(b, tr - sh, 0)
        else:
            valid = row >= sh
            a_s = pltpu.roll(a, sh, 0)
            b_s = pltpu.roll(b, sh, 0)
        b = b + jnp.where(valid, a * b_s, 0.0)
        a = jnp.where(valid, a * a_s, a)
        sh *= 2
    return a, b


def _linear_scan(name, a, b, tr, reverse=False, shift_a=False):
    t_rows, cols = a.shape
    tc = _tile(cols, 512)
    ni, nj = t_rows // tr, cols // tc
    hb = tr // 8

    def rmap(j, i):
        return ((ni - 1 - i) if reverse else i, j)

    def halo_map(j, i):
        ri = ni - 1 - i
        return (jnp.minimum((ri + 1) * hb, t_rows // 8 - 1), j)

    def body(*refs):
        if shift_a:
            a_ref, halo_ref, b_ref, h_ref, carry_ref = refs
        else:
            a_ref, b_ref, h_ref, carry_ref = refs
        i = pl.program_id(1)
        row = lax.broadcasted_iota(jnp.int32, (tr, tc), 0)

        @pl.when(i == 0)
        def _():
            carry_ref[...] = jnp.zeros_like(carry_ref)

        av = a_ref[...]
        if shift_a:
            nxt = jnp.where(i > 0, halo_ref[0:1, :], 0.0)
            av = jnp.where(row == tr - 1, jnp.broadcast_to(nxt, (tr, tc)), pltpu.roll(av, tr - 1, 0))
        pa, hb_ = _scan_rows(av, b_ref[...], row, tr, reverse)
        h = hb_ + pa * carry_ref[0:1, :]
        h_ref[...] = h
        last = h[0:1, :] if reverse else h[tr - 1:tr, :]
        carry_ref[...] = jnp.broadcast_to(last, carry_ref.shape)

    in_specs = [pl.BlockSpec((tr, tc), rmap)]
    args = [a]
    if shift_a:
        in_specs.append(pl.BlockSpec((8, tc), halo_map))
        args.append(a)
    in_specs.append(pl.BlockSpec((tr, tc), rmap))
    args.append(b)
    return pl.pallas_call(
        body, name=name, grid=(nj, ni), in_specs=in_specs, out_specs=pl.BlockSpec((tr, tc), rmap),
        out_shape=jax.ShapeDtypeStruct((t_rows, cols), F32), scratch_shapes=[pltpu.VMEM((8, tc), F32)],
        compiler_params=_params(("parallel", "arbitrary")),
    )(*args)


def _conv_fwd(name, proj, conv_w, conv_b, width, tr):
    t_rows = proj.shape[0]
    ni = t_rows // tr
    hb = tr // 8
    tc = _tile(width, 512)

    def body(u_ref, halo_ref, w_ref, b_ref, o_ref, ext_ref):
        i = pl.program_id(1)
        ext_ref[0:8, :] = jnp.where(i > 0, halo_ref[...], 0.0)
        ext_ref[8:8 + tr, :] = u_ref[...]
        acc = b_ref[...] + w_ref[CONV_WIDTH - 1:CONV_WIDTH, :] * u_ref[...]
        for k in range(CONV_WIDTH - 1):
            acc = acc + w_ref[k:k + 1, :] * ext_ref[pl.ds(8 - (CONV_WIDTH - 1) + k, tr), :]
        o_ref[...] = acc

    return pl.pallas_call(
        body, name=name, grid=(width // tc, ni),
        in_specs=[pl.BlockSpec((tr, tc), lambda j, i: (i, j)),
                  pl.BlockSpec((8, tc), lambda j, i: (jnp.maximum(i * hb - 1, 0), j)),
                  pl.BlockSpec((CONV_WIDTH, tc), lambda j, i: (0, j)),
                  pl.BlockSpec((1, tc), lambda j, i: (0, j))],
        out_specs=pl.BlockSpec((tr, tc), lambda j, i: (i, j)),
        out_shape=jax.ShapeDtypeStruct((t_rows, width), F32),
        scratch_shapes=[pltpu.VMEM((tr + 8, tc), F32)],
        compiler_params=_params(("parallel", "parallel")),
    )(proj, proj, conv_w, conv_b)


def _conv_bwd(name, dxc, proj, conv_w, width, tr):
    t_rows = dxc.shape[0]
    ni = t_rows // tr
    hb = tr // 8
    tc = _tile(width, 512)

    def body(d_ref, dnext_ref, u_ref, uprev_ref, w_ref, du_ref, dw_ref, db_ref, dext_ref, uext_ref):
        i = pl.program_id(1)
        dext_ref[0:tr, :] = d_ref[...]
        dext_ref[tr:tr + 8, :] = jnp.where(i < ni - 1, dnext_ref[...], 0.0)
        uext_ref[0:8, :] = jnp.where(i > 0, uprev_ref[...], 0.0)
        uext_ref[8:8 + tr, :] = u_ref[...]
        d = d_ref[...]
        du = w_ref[CONV_WIDTH - 1:CONV_WIDTH, :] * d
        dws = []
        for k in range(CONV_WIDTH - 1):
            du = du + w_ref[k:k + 1, :] * dext_ref[pl.ds(CONV_WIDTH - 1 - k, tr), :]
            dws.append(jnp.sum(d * uext_ref[pl.ds(8 - (CONV_WIDTH - 1) + k, tr), :], axis=0, keepdims=True))
        dws.append(jnp.sum(d * u_ref[...], axis=0, keepdims=True))
        du_ref[...] = du
        dw = jnp.concatenate(dws, axis=0)
        db = jnp.sum(d, axis=0, keepdims=True)

        @pl.when(i == 0)
        def _():
            dw_ref[...] = dw
            db_ref[...] = db

        @pl.when(i > 0)
        def _():
            dw_ref[...] += dw
            db_ref[...] += db

    return pl.pallas_call(
        body, name=name, grid=(width // tc, ni),
        in_specs=[pl.BlockSpec((tr, tc), lambda j, i: (i, j)),
                  pl.BlockSpec((8, tc), lambda j, i: (jnp.minimum((i + 1) * hb, t_rows // 8 - 1), j)),
                  pl.BlockSpec((tr, tc), lambda j, i: (i, j)),
                  pl.BlockSpec((8, tc), lambda j, i: (jnp.maximum(i * hb - 1, 0), j)),
                  pl.BlockSpec((CONV_WIDTH, tc), lambda j, i: (0, j))],
        out_specs=[pl.BlockSpec((tr, tc), lambda j, i: (i, j)),
                   pl.BlockSpec((CONV_WIDTH, tc), lambda j, i: (0, j)),
                   pl.BlockSpec((1, tc), lambda j, i: (0, j))],
        out_shape=[jax.ShapeDtypeStruct((t_rows, width), F32), jax.ShapeDtypeStruct((CONV_WIDTH, width), F32),
                   jax.ShapeDtypeStruct((1, width), F32)],
        scratch_shapes=[pltpu.VMEM((tr + 8, tc), F32), pltpu.VMEM((tr + 8, tc), F32)],
        compiler_params=_params(("parallel", "arbitrary")),
    )(dxc, dxc, proj, proj, conv_w)


def _s5_disc_math(lam_re, lam_im, log_dt, b_re, b_im, expand):
    dt = jnp.exp(log_dt)
    zr, zi = lam_re * dt, lam_im * dt
    mag = jnp.exp(zr)
    lbr, lbi = mag * jnp.cos(zi), mag * jnp.sin(zi)
    ar, ai = lbr - 1.0, lbi
    den = lam_re * lam_re + lam_im * lam_im
    cr = (ar * lam_re + ai * lam_im) / den
    ci = (ai * lam_re - ar * lam_im) / den
    cre = jnp.dot(cr, expand, precision=lax.Precision.HIGHEST, preferred_element_type=F32)
    cie = jnp.dot(ci, expand, precision=lax.Precision.HIGHEST, preferred_element_type=F32)
    return lbr, lbi, cre * b_re - cie * b_im, cre * b_im + cie * b_re


def _s5_disc(name, lam_re, lam_im, log_dt, b_re, b_im, expand, cots=None):
    ins = [lam_re, lam_im, log_dt, b_re, b_im, expand]
    n_in = len(ins) + (len(cots) if cots else 0)

    def body(*refs):
        vals = [r[...] for r in refs[:6]]
        outs = refs[n_in:]
        if cots is None:
            res = _s5_disc_math(*vals)
        else:
            cv = tuple(r[...] for r in refs[6:n_in])
            _, vjp = jax.vjp(lambda a, b, c, d, e: _s5_disc_math(a, b, c, d, e, vals[5]), *vals[:5])
            res = vjp(cv)
        for o, r in zip(outs, res):
            o[...] = r

    if cots is None:
        shapes = [lam_re.shape, lam_re.shape, b_re.shape, b_re.shape]
    else:
        shapes = [lam_re.shape, lam_re.shape, log_dt.shape, b_re.shape, b_re.shape]
    vmem = pl.BlockSpec(memory_space=pltpu.VMEM)
    return pl.pallas_call(
        body, name=name, in_specs=[vmem] * n_in, out_specs=[vmem] * len(shapes),
        out_shape=[jax.ShapeDtypeStruct(s, F32) for s in shapes],
        compiler_params=pltpu.CompilerParams(vmem_limit_bytes=VMEM_LIMIT),
    )(*ins, *(cots or ()))


def _cmul(ar, ai, br, bi):
    return ar * br - ai * bi, ar * bi + ai * br


SCAN_BLOCK = 8


def _block_scan(br, bi, lr, li, row, rows, reverse):
    pr, pi = lr, li
    pos = row & (SCAN_BLOCK - 1)
    sh = 1
    while sh < SCAN_BLOCK:
        if reverse:
            valid = pos < SCAN_BLOCK - sh
            rs, is_ = pltpu.roll(br, rows - sh, 0), pltpu.roll(bi, rows - sh, 0)
        else:
            valid = pos >= sh
            rs, is_ = pltpu.roll(br, sh, 0), pltpu.roll(bi, sh, 0)
        mr, mi = _cmul(pr, pi, rs, is_)
        br = br + jnp.where(valid, mr, 0.0)
        bi = bi + jnp.where(valid, mi, 0.0)
        pr, pi = _cmul(pr, pi, pr, pi)
        sh *= 2
    return br, bi


def _block_powers(lr, li, hw, reverse):
    row = lax.broadcasted_iota(jnp.int32, (SCAN_BLOCK, hw), 0)
    d = jnp.where(row == (SCAN_BLOCK - 1 if reverse else 0), 1.0, 0.0)
    return _block_scan(jnp.broadcast_to(lr, (SCAN_BLOCK, hw)) * d, jnp.broadcast_to(li, (SCAN_BLOCK, hw)) * d,
                       lr, li, row, SCAN_BLOCK, reverse)


def _chain_blocks(s_ref, pow_ref, carry, rows, hw, reverse):
    nblk = rows // SCAN_BLOCK
    pr, pi = pow_ref[:, :hw], pow_ref[:, hw:]

    def step(n, c):
        b = (nblk - 1 - n) if reverse else n
        at = pl.ds(pl.multiple_of(b * SCAN_BLOCK, SCAN_BLOCK), SCAN_BLOCK)
        blk = s_ref[at, :]
        kr, ki = _cmul(pr, pi, c[0], c[1])
        sr, si = blk[:, :hw] + kr, blk[:, hw:] + ki
        s_ref[at, :] = jnp.concatenate([sr, si], axis=1)
        edge = slice(0, 1) if reverse else slice(SCAN_BLOCK - 1, SCAN_BLOCK)
        return sr[edge, :], si[edge, :]

    return lax.fori_loop(0, nblk, step, carry)


def _s5_fwd(name, proj, col0, bblk, cblk, lamblk, dvec, width, tr):
    t_rows = proj.shape[0]
    ni = t_rows // tr
    nb = width // LANES
    sw = bblk.shape[-1]
    hw = sw // 2

    def body(u_ref, b_ref, c_ref, lam_ref, d_ref, y_ref, s_ref, carry_ref, pow_ref, st_ref):
        i = pl.program_id(1)
        row = lax.broadcasted_iota(jnp.int32, (tr, hw), 0)
        lam = lam_ref[...]
        lr, li = lam[:, :hw], lam[:, hw:]

        @pl.when(i == 0)
        def _():
            carry_ref[...] = jnp.zeros_like(carry_ref)
            zr, zi = _block_powers(lr, li, hw, False)
            pow_ref[:, :hw] = zr
            pow_ref[:, hw:] = zi

        u = u_ref[...]
        bu = jnp.dot(u.astype(BF16), b_ref[...], preferred_element_type=F32)
        sr, si = _block_scan(bu[:, :hw], bu[:, hw:], lr, li, row, tr, False)
        st_ref[:, :hw] = sr
        st_ref[:, hw:] = si
        cr, ci = _chain_blocks(st_ref, pow_ref, (carry_ref[0:1, :hw], carry_ref[0:1, hw:]), tr, hw, False)
        carry_ref[...] = jnp.broadcast_to(jnp.concatenate([cr, ci], axis=1), carry_ref.shape)
        s16 = st_ref[...].astype(BF16)
        s_ref[...] = s16
        y_ref[...] = jnp.dot(s16, c_ref[...], preferred_element_type=F32) + d_ref[...] * u

    return pl.pallas_call(
        body, name=name, grid=(nb, ni),
        in_specs=[pl.BlockSpec((tr, LANES), lambda k, i: (i, col0 + k)),
                  pl.BlockSpec((None, LANES, sw), lambda k, i: (k, 0, 0)),
                  pl.BlockSpec((None, sw, LANES), lambda k, i: (k, 0, 0)),
                  pl.BlockSpec((None, 1, sw), lambda k, i: (k, 0, 0)),
                  pl.BlockSpec((1, LANES), lambda k, i: (0, k))],
        out_specs=[pl.BlockSpec((tr, LANES), lambda k, i: (i, k)),
                   pl.BlockSpec((tr, sw), lambda k, i: (i, k))],
        out_shape=[jax.ShapeDtypeStruct((t_rows, width), F32), jax.ShapeDtypeStruct((t_rows, nb * sw), BF16)],
        scratch_shapes=[pltpu.VMEM((8, sw), F32), pltpu.VMEM((SCAN_BLOCK, sw), F32), pltpu.VMEM((tr, sw), F32)],
        compiler_params=_params(("parallel", "arbitrary")),
    )(proj, bblk, cblk, lamblk, dvec)


def _s5_bwd(name, dy, proj, col0, states, bblk, cblk, lamblk, dvec, width, tr):
    t_rows = dy.shape[0]
    ni = t_rows // tr
    nb = width // LANES
    sw = bblk.shape[-1]
    hw = sw // 2
    hb16 = tr // 16

    def rmap(k, i):
        return (ni - 1 - i, k)

    def body(dy_ref, u_ref, s_ref, sprev_ref, b_ref, c_ref, lam_ref, d_ref,
             du_ref, db_ref, dc_ref, dlam_ref, dd_ref, carry_ref, pow_ref, gt_ref):
        i = pl.program_id(1)
        row = lax.broadcasted_iota(jnp.int32, (tr, hw), 0)
        row_w = lax.broadcasted_iota(jnp.int32, (tr, sw), 0)
        lam = lam_ref[...]
        lr, li = lam[:, :hw], -lam[:, hw:]

        @pl.when(i == 0)
        def _():
            carry_ref[...] = jnp.zeros_like(carry_ref)
            zr, zi = _block_powers(lr, li, hw, True)
            pow_ref[:, :hw] = zr
            pow_ref[:, hw:] = zi

        dyv = dy_ref[...]
        dy16 = dyv.astype(BF16)
        u = u_ref[...]
        gd = lax.dot_general(dy16, c_ref[...], _DN['nt'], preferred_element_type=F32)
        gr, gi = _block_scan(gd[:, :hw], gd[:, hw:], lr, li, row, tr, True)
        gt_ref[:, :hw] = gr
        gt_ref[:, hw:] = gi
        cr, ci = _chain_blocks(gt_ref, pow_ref, (carry_ref[0:1, :hw], carry_ref[0:1, hw:]), tr, hw, True)
        carry_ref[...] = jnp.broadcast_to(jnp.concatenate([cr, ci], axis=1), carry_ref.shape)
        g = gt_ref[...]
        gr, gi = g[:, :hw], g[:, hw:]
        g16 = g.astype(BF16)
        du_ref[...] = lax.dot_general(g16, b_ref[...], _DN['nt'], preferred_element_type=F32) + d_ref[...] * dyv
        s16 = s_ref[...]
        dbv = lax.dot_general(g16, u.astype(BF16), _DN['tn'], preferred_element_type=F32)
        dcv = lax.dot_general(s16, dy16, _DN['tn'], preferred_element_type=F32)
        s32 = s16.astype(F32)
        first = jnp.where(i < ni - 1, sprev_ref[15:16, :].astype(F32), 0.0)
        sp = jnp.where(row_w == 0, jnp.broadcast_to(first, (tr, sw)), pltpu.roll(s32, 1, 0))
        spr, spi = sp[:, :hw], sp[:, hw:]
        dlr = jnp.sum(gr * spr + gi * spi, axis=0, keepdims=True)
        dli = jnp.sum(gi * spr - gr * spi, axis=0, keepdims=True)
        dlam = jnp.concatenate([dlr, dli], axis=1)
        ddv = jnp.sum(dyv * u, axis=0, keepdims=True)

        @pl.when(i == 0)
        def _():
            db_ref[...] = dbv
            dc_ref[...] = dcv
            dlam_ref[...] = dlam
            dd_ref[...] = ddv

        @pl.when(i > 0)
        def _():
            db_ref[...] += dbv
            dc_ref[...] += dcv
            dlam_ref[...] += dlam
            dd_ref[...] += ddv

    return pl.pallas_call(
        body, name=name, grid=(nb, ni),
        in_specs=[pl.BlockSpec((tr, LANES), rmap),
                  pl.BlockSpec((tr, LANES), lambda k, i: (ni - 1 - i, col0 + k)),
                  pl.BlockSpec((tr, sw), rmap),
                  pl.BlockSpec((16, sw), lambda k, i: (jnp.maximum((ni - 1 - i) * hb16 - 1, 0), k)),
                  pl.BlockSpec((None, LANES, sw), lambda k, i: (k, 0, 0)),
                  pl.BlockSpec((None, sw, LANES), lambda k, i: (k, 0, 0)),
                  pl.BlockSpec((None, 1, sw), lambda k, i: (k, 0, 0)),
                  pl.BlockSpec((1, LANES), lambda k, i: (0, k))],
        out_specs=[pl.BlockSpec((tr, LANES), rmap),
                   pl.BlockSpec((None, sw, LANES), lambda k, i: (k, 0, 0)),
                   pl.BlockSpec((None, sw, LANES), lambda k, i: (k, 0, 0)),
                   pl.BlockSpec((None, 1, sw), lambda k, i: (k, 0, 0)),
                   pl.BlockSpec((1, LANES), lambda k, i: (0, k))],
        out_shape=[jax.ShapeDtypeStruct((t_rows, width), F32), jax.ShapeDtypeStruct((nb, sw, LANES), F32),
                   jax.ShapeDtypeStruct((nb, sw, LANES), F32), jax.ShapeDtypeStruct((nb, 1, sw), F32),
                   jax.ShapeDtypeStruct((1, width), F32)],
        scratch_shapes=[pltpu.VMEM((8, sw), F32), pltpu.VMEM((SCAN_BLOCK, sw), F32), pltpu.VMEM((tr, sw), F32)],
        compiler_params=_params(("parallel", "arbitrary")),
    )(dy, proj, states, states, bblk, cblk, lamblk, dvec)


def _blockdiag(m):
    nb, g, p, q = m.shape
    eye = jnp.eye(g, dtype=m.dtype)
    return (m[:, :, :, None, :] * eye[None, :, None, :, None]).reshape(nb, g * p, g * q)


def _blockdiag_take(m, p, q):
    nb = m.shape[0]
    g = GROUPS_PER_BLOCK
    m = m.reshape(nb, g, p, g, q)
    return jnp.stack([m[:, k, :, k, :] for k in range(g)], axis=1)


def _adamw(name, w, g, m, v, tr, comm=None):
    rows, cols = w.shape
    gcols = g.shape[1]
    steps = rows // tr
    c1 = 1.0 - ADAM_B1 ** ADAM_STEP
    c2 = 1.0 - ADAM_B2 ** ADAM_STEP
    n_cin = len(comm.arrays) if comm else 0
    n_cout = len(comm.out_shapes) if comm else 0

    def body(*refs):
        w_ref, g_ref, m_ref, v_ref = refs[:4]
        go_ref, d_ref, mo_ref, vo_ref = refs[4 + n_cin:8 + n_cin]
        comm_args = (refs[4:4 + n_cin], refs[8 + n_cin:8 + n_cin + n_cout], refs[8 + n_cin + n_cout:])
        if comm:
            @pl.when(pl.program_id(0) == 0)
            def _():
                comm.start(*comm_args)

        gv = g_ref[...] if gcols == cols else g_ref[:, :cols]
        mn = ADAM_B1 * m_ref[...] + (1.0 - ADAM_B1) * gv
        vn = ADAM_B2 * v_ref[...] + (1.0 - ADAM_B2) * (gv * gv)
        go_ref[...] = gv
        mo_ref[...] = mn
        vo_ref[...] = vn
        d_ref[...] = -ADAM_LR * ((mn / c1) / (jnp.sqrt(vn / c2) + ADAM_EPS) + ADAM_WD * w_ref[...])
        if comm:
            @pl.when(pl.program_id(0) == steps - 1)
            def _():
                comm.finish(*comm_args)

    spec = pl.BlockSpec((tr, cols), lambda i: (i, 0))
    hbm = pl.BlockSpec(memory_space=pltpu.HBM)
    return pl.pallas_call(
        body, name=name, grid=(steps,),
        in_specs=[spec, pl.BlockSpec((tr, gcols), lambda i: (i, 0)), spec, spec] + [hbm] * n_cin,
        out_specs=[spec] * 4 + [hbm] * n_cout,
        out_shape=[jax.ShapeDtypeStruct((rows, cols), F32)] * 4 + (list(comm.out_shapes) if comm else []),
        scratch_shapes=[pltpu.SemaphoreType.DMA((n,)) for n in comm.sems] if comm else [],
        input_output_aliases={4 + k: 4 + k for k in range(n_cin)} if comm and comm.aliased else {},
        compiler_params=_params(("arbitrary",) if comm else ("parallel",)),
    )(w, g, m, v, *(comm.arrays if comm else []))


def _ffn_fwd(tag, h, norm, wg, wu, wd, dims, host=None):
    t_rows, d, fp, tm, tr, trn = dims['T'], dims['D'], dims['FP'], dims['TM'], dims['TR'], dims['TRN']
    host = host or {}
    got = {}
    ft = N_CHIPS * fp
    tn = _tile(fp, 1408)
    npb = fp // tn
    n16, = _rowmap(f"{tag}_norm", lambda x, g: _rms(x, g).astype(BF16), (1, t_rows // trn),
                   [_rows(h, trn), _full(norm)], [_o((t_rows, d), BF16, trn)])

    def up(nm, w):
        res = _matmul(f"{tag}_{nm}", 'nt', n16, (tm // 2, d // 2), lambda i, j, k: (i, k),
                      w, (None, fp, d // 2), lambda i, j, k: (j, 0, k),
                      (2 * t_rows // tm, N_CHIPS, 2), [((t_rows, ft), BF16, (tm // 2, fp), lambda i, j: (i, j))],
                      lambda acc: acc, comms=host.get(nm, ()))
        got[nm] = res[1:]
        return res[0]

    gate = up("gate", wg)
    wu = got['gate'][wu] if isinstance(wu, int) else wu
    upv = up("up", wu)
    wd = got['up'][wd] if isinstance(wd, int) else wd
    tc = _tile(ft, 1408)
    act, = _rowmap(f"{tag}_act", lambda g, u: _silu(g.astype(F32)) * u.astype(F32), (ft // tc, t_rows // tr),
                   [_rows(gate, tr, tc), _rows(upv, tr, tc)], [_o((t_rows, ft), BF16, tr, tc)])
    tnd = _tile(d, 1024)
    res = _matmul(f"{tag}_down", 'nn', act, (tm, fp), lambda i, j, k: (i, k),
                  wd, (None, fp, tnd), lambda i, j, k: (k, 0, j),
                  (t_rows // tm, d // tnd, N_CHIPS), [((t_rows, d), F32, (tm, tnd), lambda i, j: (i, j))],
                  lambda acc, hin: hin + 0.5 * acc, extras=[(h, (tm, tnd), lambda i, j: (i, j))],
                  comms=host.get('down', ()))
    got['down'] = res[1:]
    return res[0], dict(h=h, n16=n16, gate=gate, up=upv, act=act, wu=wu, wd=wd), got


def _pair(nm, g, pos):
    return _pair_sum(f"pair_{nm}", g, _run_comm(f"swap_{nm}", _swap_comm(g))[0], pos)


def _ffn_bwd(tag, names, dh, dh16, saved, norm, wg, dims, pos, host=None):
    t_rows, d, fp, tm, trn = dims['T'], dims['D'], dims['FP'], dims['TM'], dims['TRN']
    host = host or {}
    got = {}
    n_gate, n_up, n_down = names
    wu, wd = saved['wu'], saved['wd']
    ft = N_CHIPS * fp
    tn = _tile(fp, 1408)
    npb = fp // tn
    tm2 = tm // 2
    tk = t_rows // 2

    def act_bwd(acc, g, u):
        da = 0.5 * acc
        g32, u32 = g.astype(F32), u.astype(F32)
        sg = _sigmoid(g32)
        return da * u32 * (sg * (1.0 + g32 * (1.0 - sg))), da * (g32 * sg)

    res = _matmul(
        f"{tag}_dact", 'nt', dh16, (tm2, d // 2), lambda i, j, k: (i, k),
        wd, (None, tn, d // 2), lambda i, j, k: (j // npb, j % npb, k),
        (t_rows // tm2, ft // tn, 2),
        [((t_rows, ft), BF16, (tm2, tn), lambda i, j: (i, j))] * 2, act_bwd,
        extras=[(saved['gate'], (tm2, tn), lambda i, j: (i, j)), (saved['up'], (tm2, tn), lambda i, j: (i, j))],
        comms=host.get('dact', ()))
    dgate, dup, got['dact'] = res[0], res[1], res[2:]
    tnd = _tile(d, 1024)
    tw = _tile(d, 512)
    res = _matmul(f"{tag}_dwd", 'tn', saved['act'], (t_rows, tn), lambda i, j, k: (0, i),
                  dh16, (t_rows, tw), lambda i, j, k: (0, j),
                  (ft // tn, d // tw, 1),
                  [((N_CHIPS, fp, d), BF16, (None, tn, tw), lambda i, j: (i // npb, i % npb, j))],
                  lambda acc: 0.5 * acc, comms=host.get('dwd', ()))
    dwd, got['dwd'] = res[0], res[1:]

    def dw_up(nm, dact, comms):
        return _matmul(f"{tag}_{nm}", 'tn', dact, (t_rows, tn), lambda i, j, k: (0, i),
                       saved['n16'], (t_rows, tw), lambda i, j, k: (0, j),
                       (ft // tn, d // tw, 1),
                       [((N_CHIPS, fp, d), BF16, (None, tn, tw), lambda i, j: (i // npb, i % npb, j))],
                       lambda acc: acc, comms=comms)

    def dn_part(nm, dact, w, prev, comms):
        extras = [] if prev is None else [(prev, (tm, tnd), lambda i, j: (i, j))]
        return _matmul(f"{tag}_{nm}", 'nn', dact, (tm, fp), lambda i, j, k: (i, k),
                       w, (None, fp, tnd), lambda i, j, k: (k, 0, j),
                       (t_rows // tm, d // tnd, N_CHIPS), [((t_rows, d), F32, (tm, tnd), lambda i, j: (i, j))],
                       (lambda acc: acc) if prev is None else (lambda acc, p: acc + p), extras=extras, comms=comms)

    dwg, swap_d = dw_up("dwg", dgate, [_swap_comm(dwd)])
    pair_d = _pair_sum(f"pair_{n_down}", dwd, swap_d, pos)
    dwu, land_d, swap_g = dw_up("dwu", dup, [_scatter_comm(pair_d), _swap_comm(dwg)])
    pair_g = _pair_sum(f"pair_{n_gate}", dwg, swap_g, pos)
    dn_g, land_g, swap_u = dn_part("dn_gate", dgate, wg, None, [_scatter_comm(pair_g), _swap_comm(dwu)])
    pair_u = _pair_sum(f"pair_{n_up}", dwu, swap_u, pos)
    dn, land_u = dn_part("dn_up", dup, wu, dn_g, [_scatter_comm(pair_u)])

    def norm_bwd(x, dnv, dhv, g):
        _, vjp = jax.vjp(_rms, x, g)
        dx, dg = vjp(dnv)
        res = dhv + dx
        return res, res.astype(BF16), dg

    dh_in, dh_in16, dnorm = _rowmap(
        f"{tag}_dnorm", norm_bwd, (1, t_rows // trn),
        [_rows(saved['h'], trn), _rows(dn, trn), _rows(dh, trn), _full(norm)],
        [_o((t_rows, d), F32, trn), _o((t_rows, d), BF16, trn)], [_acc((1, d))])
    parts = {n_down: (pair_d, land_d), n_gate: (pair_g, land_g), n_up: (pair_u, land_u)}
    return dh_in, dh_in16, dnorm, parts, got


def kernel(x, meta_tokens, ffn1_norm, ffn1_w_gate, ffn1_w_up, ffn1_w_down, mix_norm, w_in, rg_conv_w, rg_conv_b, rg_w_a, rg_b_a, rg_w_x, rg_b_x, rg_lambda, s5_lambda_re, s5_lambda_im, s5_log_dt, s5_b_re, s5_b_im, s5_c_re, s5_c_im, s5_d, s5_glu_w, s5_glu_b, rg_out_norm, s5_out_norm, w_out, ffn2_norm, ffn2_w_gate, ffn2_w_up, ffn2_w_down, final_norm, loss_target, m_meta_tokens, m_ffn1_norm, m_ffn1_w_gate, m_ffn1_w_up, m_ffn1_w_down, m_mix_norm, m_w_in, m_rg_conv_w, m_rg_conv_b, m_rg_w_a, m_rg_b_a, m_rg_w_x, m_rg_b_x, m_rg_lambda, m_s5_lambda_re, m_s5_lambda_im, m_s5_log_dt, m_s5_b_re, m_s5_b_im, m_s5_c_re, m_s5_c_im, m_s5_d, m_s5_glu_w, m_s5_glu_b, m_rg_out_norm, m_s5_out_norm, m_w_out, m_ffn2_norm, m_ffn2_w_gate, m_ffn2_w_up, m_ffn2_w_down, m_final_norm, v_meta_tokens, v_ffn1_norm, v_ffn1_w_gate, v_ffn1_w_up, v_ffn1_w_down, v_mix_norm, v_w_in, v_rg_conv_w, v_rg_conv_b, v_rg_w_a, v_rg_b_a, v_rg_w_x, v_rg_b_x, v_rg_lambda, v_s5_lambda_re, v_s5_lambda_im, v_s5_log_dt, v_s5_b_re, v_s5_b_im, v_s5_c_re, v_s5_c_im, v_s5_d, v_s5_glu_w, v_s5_glu_b, v_rg_out_norm, v_s5_out_norm, v_w_out, v_ffn2_norm, v_ffn2_w_gate, v_ffn2_w_up, v_ffn2_w_down, v_final_norm):
    args = locals()
    w = {n: args[n] for n in WEIGHTS}
    mom = {n: args["m_" + n] for n in WEIGHTS}
    var = {n: args["v_" + n] for n in WEIGHTS}

    seq, d = x.shape[1], x.shape[2]
    f_shard = ffn1_w_gate.shape[2]
    fp = _round_up(f_shard, LANES)
    r = rg_conv_b.shape[1]
    s5w = s5_d.shape[1]
    hd = r // RG_HEADS
    groups = s5w // S5_GROUP
    nb = groups // GROUPS_PER_BLOCK
    t_rows = _round_up(N_META + seq, LANES)
    tm, tr, tk = t_rows // 4, t_rows // 8, t_rows // 2
    trn = _tile(t_rows, max(t_rows // 32, 16), 16)
    dims = dict(T=t_rows, D=d, FP=fp, TM=tm, TR=tr, TRN=trn)
    cx, cy, cc = lax.axis_index("x"), lax.axis_index("y"), lax.axis_index("c")
    chip = 2 * cx + cy
    pos = jnp.stack([chip, cc]).astype(jnp.int32)

    def gather(nm, buf):
        return _run_comm(f"ag_{nm}", _gather_comm(buf))[0]

    def gather_f32(nm, local):
        buf = lax.dynamic_update_slice(jnp.zeros((N_CHIPS,) + local.shape, F32), local[None], (chip, 0, 0))
        return gather(nm, buf)

    slot = {}
    transposed = ('ffn1_w_gate', 'ffn1_w_up', 'ffn2_w_gate', 'ffn2_w_up')
    local2d = lambda tree, nm: jnp.swapaxes(tree[nm][0], 0, 1) if nm in transposed else tree[nm][0]
    for nm in transposed:
        slot[nm] = _cast_pad(f"cast_{nm}", local2d(w, nm), fp, d, pos)
    for nm in ('ffn1_w_down', 'ffn2_w_down'):
        slot[nm] = _cast_pad(f"cast_{nm}", w[nm][0], fp, d, pos)
    slot['w_in'] = _cast_pad("cast_w_in", w_in[0], d, w_in.shape[2], pos)
    slot['w_out'] = _cast_pad("cast_w_out", w_out[0], w_out.shape[1], d, pos)
    slot['s5_glu_w'] = _cast_pad("cast_glu", s5_glu_w[0], s5_glu_w.shape[1], s5w, pos)
    rgw_local = jnp.concatenate([rg_w_a[0], rg_w_x[0]], axis=0).reshape(2 * RG_HEADS * (hd // N_CHIPS), hd)
    rgw = gather('rgw', _cast_pad("cast_rgw", rgw_local, rgw_local.shape[0], hd, pos))
    rgw = rgw.reshape(N_CHIPS, 2, RG_HEADS, hd // N_CHIPS, hd).transpose(1, 2, 0, 3, 4).reshape(2, RG_HEADS, hd, hd)
    w_a16, w_x16 = rgw[0], rgw[1]
    meta_full = gather_f32('meta', meta_tokens).transpose(1, 0, 2).reshape(N_META, d)
    conv_w2, b_a2, b_x2 = _gather_small_shards(rg_conv_w[0], rg_b_a[0], rg_b_x[0], gather_f32)
    wfull = {'ffn1_w_gate': gather('ffn1_w_gate', slot['ffn1_w_gate'])}

    pad_rows = t_rows - N_META - seq
    h0 = jnp.concatenate([meta_full, x[0], jnp.zeros((pad_rows, d), F32)], axis=0)
    tgt = jnp.concatenate([jnp.zeros((N_META, d), F32), loss_target[0], jnp.zeros((pad_rows, d), F32)], axis=0)

    h1, sv1, got = _ffn_fwd(
        "ffn1", h0, ffn1_norm, wfull['ffn1_w_gate'], 0, 0, dims,
        host={'gate': [_gather_comm(slot['ffn1_w_up'])], 'up': [_gather_comm(slot['ffn1_w_down'])],
              'down': [_gather_comm(slot['w_in']), _gather_comm(slot['s5_glu_w'])]})
    wfull['w_in'] = got['down'][0]
    wfull['s5_glu_w'] = got['down'][1].reshape(s5w, s5w)

    n2, = _rowmap("mix_norm", lambda xv, g: _rms(xv, g).astype(BF16), (1, t_rows // trn),
                  [_rows(h1, trn), _full(mix_norm)], [_o((t_rows, d), BF16, trn)])
    pw = w_in.shape[2]
    tnp = _tile(pw, 1536)
    nppb = pw // tnp
    proj, w_out_full = _matmul(
        "in_proj", 'nn', n2, (tm, d // 2), lambda i, j, k: (i, k),
        wfull['w_in'], (None, d // 2, tnp), lambda i, j, k: (j // nppb, k, j % nppb),
        (t_rows // tm, N_CHIPS * nppb, 2), [((t_rows, 3 * r), F32, (tm, tnp), lambda i, j: (i, j))],
        lambda acc: acc, comms=[_gather_comm(slot['w_out'])])
    wfull['w_out'] = w_out_full.reshape(d, d)

    xc = _conv_fwd("rg_conv", proj, conv_w2, rg_conv_b, r, tr)

    def head_cols(arr):
        return (arr, (tr, hd), lambda j, i: (i, j))

    def head_w(arr):
        return (arr, (None, hd, hd), lambda j, i: (j, 0, 0))

    def head_vec(arr):
        return (arr, (1, hd), lambda j, i: (0, j))

    def gates_fwd(xcv, wa, wx, ba, bx, lam):
        x16 = xcv.astype(BF16)
        pre_r = jnp.dot(x16, wa, preferred_element_type=F32) + ba
        pre_i = jnp.dot(x16, wx, preferred_element_type=F32) + bx
        return _rg_gate_math(xcv, pre_r, pre_i, _softplus_neg(lam))

    gate_params = [head_w(w_a16), head_w(w_x16), head_vec(b_a2), head_vec(b_x2), head_vec(rg_lambda)]
    head_out = ((t_rows, r), F32, (tr, hd), lambda j, i: (i, j))
    a_dec, bxv = _rowmap("rg_gates", gates_fwd, (RG_HEADS, t_rows // tr), [head_cols(xc)] + gate_params,
                         [head_out, head_out])
    h_rg = _linear_scan("rg_scan", a_dec, bxv, tr)

    expand = jnp.repeat(jnp.eye(S5_STATE, dtype=F32), S5_GROUP, axis=1)
    b_re2 = s5_b_re[0].reshape(groups, S5_STATE * S5_GROUP)
    b_im2 = s5_b_im[0].reshape(groups, S5_STATE * S5_GROUP)
    log_dt2 = s5_log_dt[0].reshape(groups, 1)
    lbr, lbi, bbr, bbi = _s5_disc("s5_disc", s5_lambda_re[0], s5_lambda_im[0], log_dt2, b_re2, b_im2, expand)

    def to_bblk(m):
        return _blockdiag(m.reshape(nb, GROUPS_PER_BLOCK, S5_STATE, S5_GROUP).transpose(0, 1, 3, 2))

    def to_cblk(m):
        return _blockdiag(m.reshape(nb, GROUPS_PER_BLOCK, S5_GROUP, S5_STATE).transpose(0, 1, 3, 2))

    bblk = jnp.concatenate([to_bblk(bbr), to_bblk(bbi)], axis=-1).astype(BF16)
    cblk = jnp.concatenate([to_cblk(s5_c_re[0]), -to_cblk(s5_c_im[0])], axis=-2).astype(BF16)
    hw = GROUPS_PER_BLOCK * S5_STATE
    lamblk = jnp.concatenate([lbr.reshape(nb, 1, hw), lbi.reshape(nb, 1, hw)], axis=-1)
    col0 = 2 * r // LANES
    y_pre, states = _s5_fwd("s5_fwd", proj, col0, bblk, cblk, lamblk, s5_d, s5w, tr)

    z16, = _rowmap("s5_gelu", lambda yv: _gelu(yv).astype(BF16), (1, t_rows // trn),
                   [_rows(y_pre, trn)], [_o((t_rows, s5w), BF16, trn)])
    tns = _tile(s5w, 1024)

    def glu_fwd(acc, yv, b):
        gl = acc + b
        return _gelu(yv) * _sigmoid(gl), gl

    y_s5, gl = _matmul("s5_glu", 'nn', z16, (tm, s5w), lambda i, j, k: (i, 0),
                       wfull['s5_glu_w'], (s5w, tns), lambda i, j, k: (0, j),
                       (t_rows // tm, s5w // tns, 1), [((t_rows, s5w), F32, (tm, tns), lambda i, j: (i, j))] * 2,
                       glu_fwd, extras=[(y_pre, (tm, tns), lambda i, j: (i, j)), (s5_glu_b, (1, tns), lambda i, j: (0, j))])

    def mix_out(hv, gv, yv, g1, g2):
        return jnp.concatenate([_rms(hv * _gelu(gv), g1), _rms(yv, g2)], axis=1)

    gate_in = (proj, (trn, r), lambda j, i: (i, 1))
    ycat16, = _rowmap("mix_out", lambda *a: mix_out(*a).astype(BF16), (1, t_rows // trn),
                      [_rows(h_rg, trn), gate_in, _rows(y_s5, trn), _full(rg_out_norm), _full(s5_out_norm)],
                      [_o((t_rows, r + s5w), BF16, trn)])
    tnd = _tile(d, 1024)
    h2, wfull['ffn2_w_gate'] = _matmul(
        "out_proj", 'nn', ycat16, (tm, (r + s5w) // 2), lambda i, j, k: (i, k),
        wfull['w_out'], ((r + s5w) // 2, tnd), lambda i, j, k: (k, j),
        (t_rows // tm, d // tnd, 2), [((t_rows, d), F32, (tm, tnd), lambda i, j: (i, j))],
        lambda acc, hin: hin + acc, extras=[(h1, (tm, tnd), lambda i, j: (i, j))],
        comms=[_gather_comm(slot['ffn2_w_gate'])])

    h3, sv2, _ = _ffn_fwd(
        "ffn2", h2, ffn2_norm, wfull['ffn2_w_gate'], 0, 0, dims,
        host={'gate': [_gather_comm(slot['ffn2_w_up'])], 'up': [_gather_comm(slot['ffn2_w_down'])]})

    fnorm2 = final_norm.reshape(1, d)

    def head(xv, tv, g):
        i = pl.program_id(1)
        rowi = lax.broadcasted_iota(jnp.int32, (trn, 1), 0) + i * trn
        mask = jnp.where((rowi >= N_META) & (rowi < N_META + seq), 1.0, 0.0)
        out, vjp = jax.vjp(_rms, xv, g)
        err = (out - tv) * mask
        dx, dg = vjp(err * (1.0 / d))
        return dx, dx.astype(BF16), jnp.sum(err * err, axis=0, keepdims=True), dg

    dh3, dh3_16, loss_cols, d_final = _rowmap(
        "loss_head", head, (1, t_rows // trn), [_rows(h3, trn), _rows(tgt, trn), _full(fnorm2)],
        [_o((t_rows, d), F32, trn), _o((t_rows, d), BF16, trn)], [_acc((1, d)), _acc((1, d))])
    loss = lax.psum(0.5 * jnp.sum(loss_cols) / d, ("x", "y", "c"))

    small = {}
    dh2, dh2_16, small['ffn2_norm'], parts, _ = _ffn_bwd(
        "ffn2b", ('ffn2_w_gate', 'ffn2_w_up', 'ffn2_w_down'), dh3, dh3_16, sv2, ffn2_norm, wfull['ffn2_w_gate'],
        dims, pos)

    dycat, = _matmul("d_out_proj", 'nt', dh2_16, (tm, d // 2), lambda i, j, k: (i, k),
                     wfull['w_out'], (tnd, d // 2), lambda i, j, k: (j, k),
                     (t_rows // tm, (r + s5w) // tnd, 2), [((t_rows, r + s5w), F32, (tm, tnd), lambda i, j: (i, j))],
                     lambda acc: acc)
    wo_rows = (r + s5w) // N_CHIPS
    tno = _tile(wo_rows, 1024)
    npo = wo_rows // tno
    tw = _tile(d, 512)
    dw_out, = _matmul("dw_out", 'tn', ycat16, (t_rows, tno), lambda i, j, k: (0, i),
                      dh2_16, (t_rows, tw), lambda i, j, k: (0, j),
                      ((r + s5w) // tno, d // tw, 1),
                      [((N_CHIPS, wo_rows, d), BF16, (None, tno, tw), lambda i, j: (i // npo, i % npo, j))],
                      lambda acc: acc)
    pair_w_out = _pair('w_out', dw_out, pos)

    def mix_out_bwd(hv, gv, yv, dyc, g1, g2):
        _, vjp = jax.vjp(mix_out, hv, gv, yv, g1, g2)
        return vjp(dyc)

    dh_out, dgate_rg, dy_s5, small['rg_out_norm'], small['s5_out_norm'] = _rowmap(
        "mix_out_bwd", mix_out_bwd, (1, t_rows // trn),
        [_rows(h_rg, trn), gate_in, _rows(y_s5, trn), _rows(dycat, trn), _full(rg_out_norm), _full(s5_out_norm)],
        [_o((t_rows, r), F32, trn), _o((t_rows, r), F32, trn), _o((t_rows, s5w), F32, trn)],
        [_acc((1, r)), _acc((1, s5w))])

    def glu_bwd(dyv, yv, glv):
        zv = _gelu(yv)
        sg = _sigmoid(glv)
        dgl = dyv * zv * sg * (1.0 - sg)
        return dyv * sg, dgl.astype(BF16), jnp.sum(dgl, axis=0, keepdims=True)

    dz_dir, dgl16, small['s5_glu_b'] = _rowmap(
        "s5_glu_bwd", glu_bwd, (1, t_rows // trn), [_rows(dy_s5, trn), _rows(y_pre, trn), _rows(gl, trn)],
        [_o((t_rows, s5w), F32, trn), _o((t_rows, s5w), BF16, trn)], [_acc((1, s5w))])

    def dgelu(acc, dzd, yv):
        _, vjp = jax.vjp(_gelu, yv)
        return vjp(acc + dzd)[0]

    dy_pre, = _matmul("s5_dz", 'nt', dgl16, (tm, s5w), lambda i, j, k: (i, 0),
                      wfull['s5_glu_w'], (tns, s5w), lambda i, j, k: (j, 0),
                      (t_rows // tm, s5w // tns, 1), [((t_rows, s5w), F32, (tm, tns), lambda i, j: (i, j))],
                      dgelu, extras=[(dz_dir, (tm, tns), lambda i, j: (i, j)), (y_pre, (tm, tns), lambda i, j: (i, j))])
    gl_rows = s5w // N_CHIPS
    tng = _tile(gl_rows, 1024)
    npg = gl_rows // tng
    dw_glu, = _matmul("dw_glu", 'tn', z16, (t_rows, tng), lambda i, j, k: (0, i),
                      dgl16, (t_rows, tns), lambda i, j, k: (0, j),
                      (s5w // tng, s5w // tns, 1),
                      [((N_CHIPS, gl_rows, s5w), BF16, (None, tng, tns), lambda i, j: (i // npg, i % npg, j))],
                      lambda acc: acc)
    pair_glu = _pair('s5_glu_w', dw_glu, pos)
    du_s5, dbblk, dcblk, dlamblk, small['s5_d'] = _s5_bwd(
        "s5_bwd", dy_pre, proj, col0, states, bblk, cblk, lamblk, s5_d, s5w, tr)

    def from_bblk(m):
        return _blockdiag_take(m, S5_STATE, S5_GROUP).reshape(groups, S5_STATE * S5_GROUP)

    dbbr, dbbi = from_bblk(dbblk[:, :hw]), from_bblk(dbblk[:, hw:])
    dcs = lambda m: _blockdiag_take(m, S5_STATE, S5_GROUP).transpose(0, 1, 3, 2).reshape(1, groups, S5_GROUP, S5_STATE)
    small['s5_c_re'], small['s5_c_im'] = dcs(dcblk[:, :hw]), -dcs(dcblk[:, hw:])
    dlbr, dlbi = dlamblk[:, 0, :hw].reshape(groups, S5_STATE), dlamblk[:, 0, hw:].reshape(groups, S5_STATE)
    g_lre, g_lim, g_ldt, g_bre, g_bim = _s5_disc(
        "s5_disc_bwd", s5_lambda_re[0], s5_lambda_im[0], log_dt2, b_re2, b_im2, expand, cots=(dlbr, dlbi, dbbr, dbbi))
    small['s5_lambda_re'], small['s5_lambda_im'], small['s5_log_dt'] = g_lre[None], g_lim[None], g_ldt.reshape(1, groups)
    small['s5_b_re'] = g_bre.reshape(s5_b_re.shape)
    small['s5_b_im'] = g_bim.reshape(s5_b_im.shape)

    dh_rg = _linear_scan("rg_scan_bwd", a_dec, dh_out, tr, reverse=True, shift_a=True)

    hb = tr // 8

    def gates_bwd(xcv, dhv, hv, hprev, wa, wx, ba, bx, lam):
        i = pl.program_id(1)
        row = lax.broadcasted_iota(jnp.int32, (tr, hd), 0)
        first = jnp.where(i > 0, hprev[7:8, :], 0.0)
        h_prev = jnp.where(row == 0, jnp.broadcast_to(first, (tr, hd)), pltpu.roll(hv, 1, 0))
        x16 = xcv.astype(BF16)
        pre_r = jnp.dot(x16, wa, preferred_element_type=F32) + ba
        pre_i = jnp.dot(x16, wx, preferred_element_type=F32) + bx
        _, vjp = jax.vjp(_rg_gate_math, xcv, pre_r, pre_i, _softplus_neg(lam))
        dxc_, dpr, dpi, dsp = vjp((dhv * h_prev, dhv))
        dpr16, dpi16 = dpr.astype(BF16), dpi.astype(BF16)
        dxc_ = (dxc_ + lax.dot_general(dpr16, wa, _DN['nt'], preferred_element_type=F32)
                + lax.dot_general(dpi16, wx, _DN['nt'], preferred_element_type=F32))
        dwa = lax.dot_general(x16, dpr16, _DN['tn'], preferred_element_type=F32)
        dwx = lax.dot_general(x16, dpi16, _DN['tn'], preferred_element_type=F32)
        dlam = dsp * (-_sigmoid(-lam))
        return (dxc_, dwa, dwx, jnp.sum(dpr, axis=0, keepdims=True), jnp.sum(dpi, axis=0, keepdims=True), dlam)

    head_acc_w = ((RG_HEADS, hd, hd), F32, (None, hd, hd), lambda j, i: (j, 0, 0))
    head_acc_v = ((1, r), F32, (1, hd), lambda j, i: (0, j))
    h_halo = (h_rg, (8, hd), lambda j, i: (jnp.maximum(i * hb - 1, 0), j))
    dxc, d_wa, d_wx, d_ba, d_bx, d_lam = _rowmap(
        "rg_gates_bwd", gates_bwd, (RG_HEADS, t_rows // tr),
        [head_cols(xc), head_cols(dh_rg), head_cols(h_rg), h_halo] + gate_params,
        [head_out], [head_acc_w, head_acc_w, head_acc_v, head_acc_v, head_acc_v])
    small['rg_w_a'], small['rg_w_x'] = d_wa[None], d_wx[None]
    small['rg_b_a'], small['rg_b_x'] = d_ba.reshape(1, RG_HEADS, hd), d_bx.reshape(1, RG_HEADS, hd)
    small['rg_lambda'] = d_lam
    du_rg, d_convw, small['rg_conv_b'] = _conv_bwd("rg_conv_bwd", dxc, proj, conv_w2, r, tr)
    small['rg_conv_w'] = d_convw[None]

    dproj16, = _rowmap("dproj", lambda a, b, c: jnp.concatenate([a, b, c], axis=1).astype(BF16), (1, t_rows // trn),
                       [_rows(du_rg, trn), _rows(dgate_rg, trn), _rows(du_s5, trn)], [_o((t_rows, 3 * r), BF16, trn)])
    dn2, land_w_out = _matmul(
        "d_in_proj", 'nt', dproj16, (tm, pw), lambda i, j, k: (i, k),
        wfull['w_in'], (None, tnd, pw), lambda i, j, k: (k, j, 0),
        (t_rows // tm, d // tnd, N_CHIPS), [((t_rows, d), F32, (tm, tnd), lambda i, j: (i, j))],
        lambda acc: acc, comms=[_scatter_comm(pair_w_out)])
    parts['w_out'] = (pair_w_out, land_w_out)
    tq = _tile(pw, 768)
    nq = pw // tq
    dw_in, = _matmul("dw_in", 'tn', n2, (t_rows, tnd), lambda i, j, k: (0, i),
                     dproj16, (t_rows, tq), lambda i, j, k: (0, j),
                     (d // tnd, N_CHIPS * nq, 1),
                     [((N_CHIPS, d, pw), BF16, (None, tnd, tq), lambda i, j: (j // nq, i, j % nq))],
                     lambda acc: acc)
    pair_w_in = _pair('w_in', dw_in, pos)

    def norm_bwd(xv, dnv, dhv, g):
        _, vjp = jax.vjp(_rms, xv, g)
        dx, dg = vjp(dnv)
        res = dhv + dx
        return res, res.astype(BF16), dg

    dh1, dh1_16, small['mix_norm'] = _rowmap(
        "mix_dnorm", norm_bwd, (1, t_rows // trn), [_rows(h1, trn), _rows(dn2, trn), _rows(dh2, trn), _full(mix_norm)],
        [_o((t_rows, d), F32, trn), _o((t_rows, d), BF16, trn)], [_acc((1, d))])

    dh0, _, small['ffn1_norm'], parts1, got = _ffn_bwd(
        "ffn1b", ('ffn1_w_gate', 'ffn1_w_up', 'ffn1_w_down'), dh1, dh1_16, sv1, ffn1_norm, wfull['ffn1_w_gate'],
        dims, pos, host={'dact': [_scatter_comm(pair_glu), _scatter_comm(pair_w_in)]})
    parts.update(parts1)
    parts['s5_glu_w'] = (pair_glu, got['dact'][0])
    parts['w_in'] = (pair_w_in, got['dact'][1])
    grad_x = dh0[N_META:N_META + seq][None]
    small['meta_tokens'] = dh0[:N_META]
    small['final_norm'] = d_final.reshape(d)

    halves = {nm: _reduce_partials(f"red_{nm}", parts[nm][0], parts[nm][1], pos) for nm in BIG}

    small_names = [n for n in WEIGHTS if n not in BIG]
    full_shape = {n: small[n].shape for n in small_names}
    flat = jnp.concatenate([small[n].reshape(-1) for n in small_names])
    unit = 8 * 8 * LANES
    total = _round_up(flat.shape[0], unit)
    flat = jnp.concatenate([flat, jnp.zeros((total - flat.shape[0],), F32)])
    red = _all_reduce_small("ar_small", flat.reshape(total // LANES, LANES)).reshape(-1)
    gsmall = {}
    off = 0
    for n in small_names:
        size = math.prod(full_shape[n])
        gsmall[n] = _own_shard(n, red[off:off + size].reshape(full_shape[n]), w[n].shape, chip)
        off += size

    out_g, out_d, out_m, out_v = {}, {}, {}, {}
    whole = _run_comm(f"exchange_{BIG[0]}", _exchange_comm(halves[BIG[0]]))[0]
    for k, nm in enumerate(BIG):
        w2 = local2d(w, nm)
        trw = _tile(w2.shape[0], 128 if w2.shape[1] < 4096 else 64, 8)
        nxt = _exchange_comm(halves[BIG[k + 1]]) if k + 1 < len(BIG) else None
        res = _adamw(f"adamw_{nm}", w2, whole.reshape(-1, whole.shape[2]), local2d(mom, nm), local2d(var, nm), trw, nxt)
        back = (lambda a: jnp.swapaxes(a, 0, 1)[None]) if nm in transposed else (lambda a: a[None])
        out_g[nm], out_d[nm], out_m[nm], out_v[nm] = [back(a) for a in res[:4]]
        whole = res[4] if nxt else None

    def pack(tree):
        fl = jnp.concatenate([tree[n].reshape(-1) for n in small_names])
        tot = _round_up(fl.shape[0], 8 * LANES)
        return jnp.concatenate([fl, jnp.zeros((tot - fl.shape[0],), F32)]).reshape(tot // LANES, LANES)

    wp, gp, mp, vp = pack(w), pack(gsmall), pack(mom), pack(var)
    res = _adamw("adamw_small", wp, gp, mp, vp, _tile(wp.shape[0], 512, 8))
    off = 0
    for n in small_names:
        size = math.prod(w[n].shape)
        for dst, src in zip((out_g, out_d, out_m, out_v), res):
            dst[n] = src.reshape(-1)[off:off + size].reshape(w[n].shape)
        off += size

    return (loss, grad_x, *[out_g[n] for n in WEIGHTS], *[out_d[n] for n in WEIGHTS],
            *[out_m[n] for n in WEIGHTS], *[out_v[n] for n in WEIGHTS])


def _gather_small_shards(conv_w, b_a, b_x, gather):
    cw = conv_w.shape[1]
    part = b_a.shape[1]

    def rows8(a):
        a = jnp.concatenate([a, jnp.zeros((a.shape[0], cw - a.shape[1]), F32)], axis=1)
        return jnp.concatenate([a, jnp.zeros((8 - a.shape[0], cw), F32)], axis=0) if a.shape[0] < 8 else a

    local = jnp.concatenate([rows8(conv_w), rows8(b_a), rows8(b_x), jnp.zeros((8, cw), F32)], axis=0)
    full = gather("rg_small", local)
    conv_full = full[:, :CONV_WIDTH].transpose(1, 0, 2).reshape(CONV_WIDTH, N_CHIPS * cw)
    bias = lambda k: full[:, 8 * k:8 * k + RG_HEADS, :part].transpose(1, 0, 2).reshape(1, RG_HEADS * N_CHIPS * part)
    return conv_full, bias(1), bias(2)


def _own_shard(name, g, local_shape, chip):
    if tuple(g.shape) == tuple(local_shape):
        return g
    axis = [k for k, (a, b) in enumerate(zip(g.shape, local_shape)) if a != b][0]
    size = local_shape[axis]
    return lax.dynamic_slice_in_dim(g, chip * size, size, axis=axis)
```

```python
import functools
import math

import jax
import jax.numpy as jnp
from jax import lax
from jax.experimental import pallas as pl
from jax.experimental.pallas import tpu as pltpu

F32 = jnp.float32
BF16 = jnp.bfloat16
MESH = pl.DeviceIdType.MESH

N_META = 16
RG_HEADS = 8
CONV_WIDTH = 4
RG_C = 8.0
S5_GROUP = 16
S5_STATE = 64
GROUPS_PER_BLOCK = 8
EPS = 1e-6
N_CHIPS = 4
LANES = 128
VMEM_LIMIT = 56 * 1024 * 1024

ADAM_LR = 0.001
ADAM_B1 = 0.9
ADAM_B2 = 0.999
ADAM_EPS = 1e-08
ADAM_WD = 0.01
ADAM_STEP = 10

WEIGHTS = ['meta_tokens', 'ffn1_norm', 'ffn1_w_gate', 'ffn1_w_up', 'ffn1_w_down', 'mix_norm', 'w_in', 'rg_conv_w',
           'rg_conv_b', 'rg_w_a', 'rg_b_a', 'rg_w_x', 'rg_b_x', 'rg_lambda', 's5_lambda_re', 's5_lambda_im',
           's5_log_dt', 's5_b_re', 's5_b_im', 's5_c_re', 's5_c_im', 's5_d', 's5_glu_w', 's5_glu_b', 'rg_out_norm',
           's5_out_norm', 'w_out', 'ffn2_norm', 'ffn2_w_gate', 'ffn2_w_up', 'ffn2_w_down', 'final_norm']
BIG = ('ffn1_w_gate', 'ffn1_w_up', 'ffn1_w_down', 'w_in', 's5_glu_w', 'w_out', 'ffn2_w_gate', 'ffn2_w_up',
       'ffn2_w_down')

_DN = {'nn': (((1,), (0,)), ((), ())), 'nt': (((1,), (1,)), ((), ())), 'tn': (((0,), (0,)), ((), ()))}


def _round_up(n, m):
    return (n + m - 1) // m * m


def _tile(n, pref, unit=LANES):
    best = None
    for t in range(unit, min(n, pref) + 1, unit):
        if n % t == 0:
            best = t
    return best if best is not None else n


def _params(sem=None):
    return pltpu.CompilerParams(dimension_semantics=sem, vmem_limit_bytes=VMEM_LIMIT)


def _rms(x, g):
    return x * lax.rsqrt(jnp.mean(x * x, axis=-1, keepdims=True) + EPS) * g


def _sigmoid(x):
    return 0.5 * (jnp.tanh(0.5 * x) + 1.0)


def _gelu(x):
    return 0.5 * x * (1.0 + jnp.tanh(math.sqrt(2.0 / math.pi) * (x + 0.044715 * (x * x * x))))


def _silu(x):
    return x * _sigmoid(x)


def _expm1(x):
    series = x * (1.0 + x * (1.0 / 2) * (1.0 + x * (1.0 / 3) * (1.0 + x * (1.0 / 4) * (1.0 + x * (1.0 / 5) * (1.0 + x * (1.0 / 6))))))
    return jnp.where(jnp.abs(x) < 0.3, series, jnp.exp(x) - 1.0)


def _softplus_neg(lam):
    m = jnp.maximum(-lam, 0.0)
    e = jnp.exp(-jnp.abs(lam))
    w = 1.0 + e
    log1p = jnp.where(w == 1.0, e, jnp.log(w) * (e / jnp.where(w == 1.0, 1.0, w - 1.0)))
    return m + log1p


def _rg_gate_math(xc, pre_r, pre_i, sp):
    r = 1.0 / (1.0 + jnp.exp(-pre_r))
    i = 1.0 / (1.0 + jnp.exp(-pre_i))
    log_a = -RG_C * r * sp
    a = jnp.exp(log_a)
    mult = jnp.sqrt(-_expm1(2.0 * log_a))
    return a, mult * i * xc


class _Comm:
    def __init__(self, arrays, out_shapes, aliased, sems, start, finish, mid=None):
        self.arrays, self.out_shapes, self.aliased, self.sems = arrays, out_shapes, aliased, sems
        self.start, self.finish, self.mid = start, finish, mid


def _matmul(name, mode, a, a_blk, a_map, b, b_blk, b_map, grid, outs, epilogue, extras=(), comms=()):
    ni, nj, nk = grid
    ne, no = len(extras), len(outs)
    sq = lambda blk: tuple(d for d in blk if d is not None)
    ab, bb = sq(a_blk), sq(b_blk)
    acc_shape = {'nn': (ab[0], bb[1]), 'nt': (ab[0], bb[0]), 'tn': (ab[1], bb[1])}[mode]
    n_cin = sum(len(cm.arrays) for cm in comms)
    n_cout = sum(len(cm.out_shapes) for cm in comms)
    n_acc = 1 if nk > 1 else 0

    def comm_refs(refs):
        cin = refs[2 + ne:2 + ne + n_cin]
        cout = refs[2 + ne + n_cin + no:2 + ne + n_cin + no + n_cout]
        csem = refs[2 + ne + n_cin + no + n_cout + n_acc:]
        for cm in comms:
            yield cm, cin[:len(cm.arrays)], cout[:len(cm.out_shapes)], csem[:len(cm.sems)]
            cin, cout, csem = cin[len(cm.arrays):], cout[len(cm.out_shapes):], csem[len(cm.sems):]

    def body(*refs):
        a_ref, b_ref = refs[0], refs[1]
        ex = refs[2:2 + ne]
        out = refs[2 + ne + n_cin:2 + ne + n_cin + no]
        if comms:
            @pl.when((pl.program_id(0) == 0) & (pl.program_id(1) == 0) & (pl.program_id(2) == 0))
            def _():
                for cm, cin, cout, csem in comm_refs(refs):
                    cm.start(cin, cout, csem)

            if any(cm.mid for cm in comms):
                step = (pl.program_id(0) * nj + pl.program_id(1)) * nk + pl.program_id(2)

                @pl.when(step == (3 * ni * nj * nk) // 5)
                def _():
                    for cm, cin, cout, csem in comm_refs(refs):
                        if cm.mid:
                            cm.mid(cin, cout, csem)

        part = lax.dot_general(a_ref[...], b_ref[...], _DN[mode], preferred_element_type=F32)

        def finish(acc):
            res = epilogue(acc, *[e[...] for e in ex])
            if not isinstance(res, tuple):
                res = (res,)
            for o, r in zip(out, res):
                o[...] = r.astype(o.dtype)

        if nk == 1:
            finish(part)
        else:
            acc_ref = refs[2 + ne + n_cin + no + n_cout]
            k = pl.program_id(2)

            @pl.when(k == 0)
            def _():
                acc_ref[...] = part

            @pl.when(k > 0)
            def _():
                acc_ref[...] += part

            @pl.when(k == nk - 1)
            def _():
                finish(acc_ref[...])

        if comms:
            @pl.when((pl.program_id(0) == ni - 1) & (pl.program_id(1) == nj - 1) & (pl.program_id(2) == nk - 1))
            def _():
                for cm, cin, cout, csem in comm_refs(refs):
                    cm.finish(cin, cout, csem)

    hbm = pl.BlockSpec(memory_space=pltpu.HBM)
    in_specs = [pl.BlockSpec(a_blk, a_map), pl.BlockSpec(b_blk, b_map)]
    in_specs += [pl.BlockSpec(blk, functools.partial(lambda m, i, j, k: m(i, j), m)) for _, blk, m in extras]
    in_specs += [hbm] * n_cin
    out_specs = [pl.BlockSpec(blk, functools.partial(lambda m, i, j, k: m(i, j), m)) for _, _, blk, m in outs]
    out_specs += [hbm] * n_cout
    aliases, cin_at, cout_at = {}, 2 + ne, no
    for cm in comms:
        if cm.aliased:
            aliases.update({cin_at + k: cout_at + k for k in range(len(cm.arrays))})
        cin_at, cout_at = cin_at + len(cm.arrays), cout_at + len(cm.out_shapes)
    res = pl.pallas_call(
        body, name=name, grid=grid, in_specs=in_specs, out_specs=out_specs,
        out_shape=[jax.ShapeDtypeStruct(s, d) for s, d, _, _ in outs] + [s for cm in comms for s in cm.out_shapes],
        scratch_shapes=([pltpu.VMEM(acc_shape, F32)] if nk > 1 else [])
        + [pltpu.SemaphoreType.DMA((n,)) for cm in comms for n in cm.sems],
        input_output_aliases=aliases,
        compiler_params=_params(("arbitrary",) * 3 if comms else ("parallel", "parallel", "arbitrary")),
    )(a, b, *[e for e, _, _ in extras], *[arr for cm in comms for arr in cm.arrays])
    return res


def _rows(arr, tr, tc=None, col0=0):
    if tc is None:
        return (arr, (tr, arr.shape[1]), lambda j, i: (i, 0))
    return (arr, (tr, tc), lambda j, i: (i, col0 + j))


def _full(arr):
    nd = arr.ndim
    return (arr, arr.shape, lambda j, i: (0,) * nd)


def _cols(arr, tc):
    return (arr, (arr.shape[0], tc), lambda j, i: (0, j))


def _rowmap(name, fn, grid, ins, outs, accs=(), scratch=()):
    nj, ni = grid
    n_in, n_out, n_acc = len(ins), len(outs), len(accs)

    def body(*refs):
        vals = [r[...] for r in refs[:n_in]]
        o_refs = refs[n_in:n_in + n_out]
        a_refs = refs[n_in + n_out:n_in + n_out + n_acc]
        s_refs = refs[n_in + n_out + n_acc:]
        res = fn(*vals, *s_refs)
        if not isinstance(res, tuple):
            res = (res,)
        for o, r in zip(o_refs, res[:n_out]):
            o[...] = r.astype(o.dtype)
        i = pl.program_id(1)
        for a_ref, r in zip(a_refs, res[n_out:]):
            @pl.when(i == 0)
            def _(a_ref=a_ref, r=r):
                a_ref[...] = r.astype(a_ref.dtype)

            @pl.when(i > 0)
            def _(a_ref=a_ref, r=r):
                a_ref[...] += r.astype(a_ref.dtype)

    res = pl.pallas_call(
        body, name=name, grid=grid,
        in_specs=[pl.BlockSpec(blk, m) for _, blk, m in ins],
        out_specs=[pl.BlockSpec(blk, m) for _, _, blk, m in list(outs) + list(accs)],
        out_shape=[jax.ShapeDtypeStruct(s, d) for s, d, _, _ in list(outs) + list(accs)],
        scratch_shapes=list(scratch),
        compiler_params=_params(("parallel", "arbitrary")),
    )(*[a for a, _, _ in ins])
    return res


def _o(shape, dtype, tr, tc=None):
    if tc is None:
        return (shape, dtype, (tr, shape[1]), lambda j, i: (i, 0))
    return (shape, dtype, (tr, tc), lambda j, i: (i, j))


def _acc(shape, tc=None):
    if tc is None:
        nd = len(shape)
        return (shape, F32, shape, lambda j, i: (0,) * nd)
    return (shape, F32, (shape[0], tc), lambda j, i: (0, j))


def _position():
    x, y, c = lax.axis_index("x"), lax.axis_index("y"), lax.axis_index("c")
    return x, y, c


def _relay_gather_comm(buf):
    _, rows, cols = buf.shape
    rh = rows // 2
    rq = rh // 2

    def plan(out_ref, sems):
        x, y, c = _position()
        own, cx, cy, cd = 2 * x + y, 2 * (1 - x) + y, 2 * x + (1 - y), 2 * (1 - x) + (1 - y)
        to_x, to_y, sibling = (1 - x, y, c), (x, 1 - y, c), (x, y, 1 - c)
        half, other = pl.ds(c * rh, rh), pl.ds((1 - c) * rh, rh)
        q0, q1 = pl.ds(c * rh, rq), pl.ds(c * rh + rq, rq)

        def copy(pair, k, chip_index, rows_, to):
            return pltpu.make_async_remote_copy(
                src_ref=out_ref.at[chip_index, rows_], dst_ref=out_ref.at[chip_index, rows_],
                send_sem=sems[pair].at[k], recv_sem=sems[pair + 1].at[k], device_id=to, device_id_type=MESH)

        return dict(
            own_x=copy(0, 0, own, half, to_x), own_y=copy(0, 1, own, half, to_y),
            in_x=copy(0, 0, cx, half, to_x), in_y=copy(0, 1, cy, half, to_y),
            fwd_xy=copy(0, 2, cx, q0, to_y), fwd_yx=copy(0, 3, cy, q1, to_x),
            in_d0=copy(0, 2, cd, q0, to_y), in_d1=copy(0, 3, cd, q1, to_x),
            d2d_out=[copy(2, j, ch, half, sibling) for j, ch in enumerate((cx, cy, cd))],
            d2d_in=[copy(2, j, ch, other, sibling) for j, ch in enumerate((cx, cy, cd))])

    def start(ins, outs, sems):
        p = plan(outs[0], sems)
        p['own_x'].start()
        p['own_y'].start()

    def mid(ins, outs, sems):
        p = plan(outs[0], sems)
        p['in_x'].wait_recv()
        p['fwd_xy'].start()
        p['d2d_out'][0].start()
        p['in_y'].wait_recv()
        p['fwd_yx'].start()
        p['d2d_out'][1].start()

    def finish(ins, outs, sems):
        p = plan(outs[0], sems)
        p['in_d0'].wait_recv()
        p['in_d1'].wait_recv()
        p['d2d_out'][2].start()
        for cp in p['d2d_in']:
            cp.wait_recv()
        for cp in [p['own_x'], p['own_y'], p['fwd_xy'], p['fwd_yx']] + p['d2d_out']:
            cp.wait_send()

    return _Comm([buf], [jax.ShapeDtypeStruct(buf.shape, buf.dtype)], True, [4, 4, 3, 3], start, finish, mid)


def _gather_comm(buf):
    _, rows, cols = buf.shape
    rh = rows // 2
    if buf.dtype == BF16 and rh % 32 == 0:
        return _relay_gather_comm(buf)

    def plan(out_ref, sems):
        x, y, c = _position()
        chips = [(1 - x, y), (x, 1 - y), (1 - x, 1 - y)]

        def copy(pair, j, chip_index, half, to):
            return pltpu.make_async_remote_copy(
                src_ref=out_ref.at[chip_index, half], dst_ref=out_ref.at[chip_index, half],
                send_sem=sems[pair].at[j], recv_sem=sems[pair + 1].at[j], device_id=to, device_id_type=MESH)

        mine, other = pl.ds(c * rh, rh), pl.ds((1 - c) * rh, rh)
        ici_out = [copy(0, j, 2 * x + y, mine, (px, py, c)) for j, (px, py) in enumerate(chips)]
        ici_in = [copy(0, j, 2 * px + py, mine, (px, py, c)) for j, (px, py) in enumerate(chips)]
        d2d_out = [copy(2, j, 2 * px + py, mine, (x, y, 1 - c)) for j, (px, py) in enumerate(chips)]
        d2d_in = [copy(2, j, 2 * px + py, other, (x, y, 1 - c)) for j, (px, py) in enumerate(chips)]
        return ici_out, ici_in, d2d_out, d2d_in

    def start(ins, outs, sems):
        for cp in plan(outs[0], sems)[0]:
            cp.start()

    def finish(ins, outs, sems):
        ici_out, ici_in, d2d_out, d2d_in = plan(outs[0], sems)
        for arrived, fwd in zip(ici_in, d2d_out):
            arrived.wait_recv()
            fwd.start()
        for cp in d2d_in:
            cp.wait_recv()
        for cp in ici_out + d2d_out:
            cp.wait_send()

    return _Comm([buf], [jax.ShapeDtypeStruct(buf.shape, buf.dtype)], True, [3, 3, 3, 3], start, finish)


def _scatter_comm(p):
    _, rh, cols = p.shape

    def plan(p_ref, land_ref, sems):
        x, y, c = _position()
        chips = [(1 - x, y), (x, 1 - y), (1 - x, 1 - y)]

        def copy(j, chip_index, to):
            return pltpu.make_async_remote_copy(
                src_ref=p_ref.at[chip_index], dst_ref=land_ref.at[j],
                send_sem=sems[0].at[j], recv_sem=sems[1].at[j], device_id=to, device_id_type=MESH)

        out = [copy(j, 2 * px + py, (px, py, c)) for j, (px, py) in enumerate(chips)]
        arrive = [copy(j, 2 * x + y, (x, y, c)) for j in range(3)]
        return out, arrive

    def start(ins, outs, sems):
        for cp in plan(ins[0], outs[0], sems)[0]:
            cp.start()

    def finish(ins, outs, sems):
        out, arrive = plan(ins[0], outs[0], sems)
        for cp in arrive:
            cp.wait_recv()
        for cp in out:
            cp.wait_send()

    return _Comm([p], [jax.ShapeDtypeStruct((3, rh, cols), p.dtype)], False, [3, 3], start, finish)


def _run_comm(name, cm):
    n_in, n_out = len(cm.arrays), len(cm.out_shapes)

    def body(*refs):
        ins, outs, sems = refs[:n_in], refs[n_in:n_in + n_out], refs[n_in + n_out:]
        cm.start(ins, outs, sems)
        if cm.mid:
            cm.mid(ins, outs, sems)
        cm.finish(ins, outs, sems)

    hbm = pl.BlockSpec(memory_space=pltpu.HBM)
    return pl.pallas_call(
        body, name=name, in_specs=[hbm] * n_in, out_specs=[hbm] * n_out, out_shape=list(cm.out_shapes),
        input_output_aliases={k: k for k in range(n_in)} if cm.aliased else {},
        scratch_shapes=[pltpu.SemaphoreType.DMA((n,)) for n in cm.sems],
    )(*cm.arrays)


def _swap_comm(g):
    n, rows, cols = g.shape
    rh = rows // 2

    def plan(g_ref, land_ref, sems):
        x, y, c = _position()

        def copy(k, half):
            return pltpu.make_async_remote_copy(
                src_ref=g_ref.at[k, pl.ds(half * rh, rh)], dst_ref=land_ref.at[k],
                send_sem=sems[0].at[k], recv_sem=sems[1].at[k], device_id=(x, y, 1 - c), device_id_type=MESH)

        return [copy(k, 1 - c) for k in range(n)], [copy(k, c) for k in range(n)]

    def start(ins, outs, sems):
        for cp in plan(ins[0], outs[0], sems)[0]:
            cp.start()

    def finish(ins, outs, sems):
        out, arrive = plan(ins[0], outs[0], sems)
        for cp in arrive:
            cp.wait_recv()
        for cp in out:
            cp.wait_send()

    return _Comm([g], [jax.ShapeDtypeStruct((n, rh, cols), g.dtype)], False, [n, n], start, finish)


def _pair_sum(name, g, land, pos):
    n, rows, cols = g.shape
    rh = rows // 2
    tr = _tile(rh, 256, 16)
    nt = rh // tr

    def body(pos_ref, g_ref, l_ref, o_ref):
        o_ref[...] = (g_ref[...].astype(F32) + l_ref[...].astype(F32)).astype(o_ref.dtype)

    grid_spec = pltpu.PrefetchScalarGridSpec(
        num_scalar_prefetch=1, grid=(n, nt),
        in_specs=[pl.BlockSpec((None, tr, cols), lambda k, i, p: (k, p[1] * nt + i, 0)),
                  pl.BlockSpec((None, tr, cols), lambda k, i, p: (k, i, 0))],
        out_specs=pl.BlockSpec((None, tr, cols), lambda k, i, p: (k, i, 0)))
    return pl.pallas_call(
        body, name=name, grid_spec=grid_spec, out_shape=jax.ShapeDtypeStruct((n, rh, cols), g.dtype),
        compiler_params=_params(("parallel", "parallel")),
    )(pos, g, land)


def _reduce_partials(name, p, land, pos, comm=None):
    _, rh, cols = p.shape
    tr = _tile(rh, 256, 16)
    nt = rh // tr
    n_cin = len(comm.arrays) if comm else 0
    n_cout = len(comm.out_shapes) if comm else 0

    def body(*refs):
        p_ref, l_ref, o_ref = refs[1], refs[2], refs[3 + n_cin]
        comm_args = (refs[3:3 + n_cin], refs[4 + n_cin:4 + n_cin + n_cout], refs[4 + n_cin + n_cout:])
        if comm:
            @pl.when(pl.program_id(0) == 0)
            def _():
                comm.start(*comm_args)

        acc = p_ref[...].astype(F32)
        for k in range(3):
            acc = acc + l_ref[k].astype(F32)
        o_ref[...] = acc
        if comm:
            @pl.when(pl.program_id(0) == nt - 1)
            def _():
                comm.finish(*comm_args)

    hbm = pl.BlockSpec(memory_space=pltpu.HBM)
    grid_spec = pltpu.PrefetchScalarGridSpec(
        num_scalar_prefetch=1, grid=(nt,),
        in_specs=[pl.BlockSpec((None, tr, cols), lambda i, p_: (p_[0], i, 0)),
                  pl.BlockSpec((3, tr, cols), lambda i, p_: (0, i, 0))] + [hbm] * n_cin,
        out_specs=[pl.BlockSpec((None, tr, cols), lambda i, p_: (p_[1], i, 0))] + [hbm] * n_cout,
        scratch_shapes=[pltpu.SemaphoreType.DMA((n,)) for n in comm.sems] if comm else [])
    return pl.pallas_call(
        body, name=name, grid_spec=grid_spec,
        out_shape=[jax.ShapeDtypeStruct((2, rh, cols), F32)] + (list(comm.out_shapes) if comm else []),
        input_output_aliases={3 + k: 1 + k for k in range(n_cin)} if comm and comm.aliased else {},
        compiler_params=_params(("arbitrary",) if comm else ("parallel",)),
    )(pos, p, land, *(comm.arrays if comm else []))


def _exchange_comm(buf):
    def plan(out_ref, sems):
        x, y, c = _position()

        def copy(half):
            return pltpu.make_async_remote_copy(
                src_ref=out_ref.at[half], dst_ref=out_ref.at[half], send_sem=sems[0].at[0], recv_sem=sems[1].at[0],
                device_id=(x, y, 1 - c), device_id_type=MESH)

        return copy(c), copy(1 - c)

    def start(ins, outs, sems):
        plan(outs[0], sems)[0].start()

    def finish(ins, outs, sems):
        out, arrive = plan(outs[0], sems)
        arrive.wait_recv()
        out.wait_send()

    return _Comm([buf], [jax.ShapeDtypeStruct(buf.shape, buf.dtype)], True, [1, 1], start, finish)


def _all_reduce_small(name, buf):
    rows = buf.shape[0]
    p = rows // 8

    def body(buf_ref, out_ref, land_ref, red_ref, s1, r1, s2, r2):
        x, y, c = _position()
        me = 4 * x + 2 * y + c

        def peer(r):
            px = 1 - x if (r >> 2) & 1 else x
            py = 1 - y if (r >> 1) & 1 else y
            pc = 1 - c if r & 1 else c
            return (px, py, pc), 4 * px + 2 * py + pc

        firsts = []
        for r in range(1, 8):
            to, d = peer(r)
            cp = pltpu.make_async_remote_copy(
                src_ref=buf_ref.at[pl.ds(d * p, p)], dst_ref=land_ref.at[r - 1],
                send_sem=s1.at[r - 1], recv_sem=r1.at[r - 1], device_id=to, device_id_type=MESH)
            cp.start()
            firsts.append(cp)
        acc = buf_ref[pl.ds(me * p, p), :]
        for r in range(1, 8):
            firsts[r - 1].wait_recv()
            acc = acc + land_ref[r - 1]
        red_ref[...] = acc
        out_ref[pl.ds(me * p, p), :] = acc
        seconds = []
        for r in range(1, 8):
            to, d = peer(r)
            cp = pltpu.make_async_remote_copy(
                src_ref=red_ref, dst_ref=out_ref.at[pl.ds(me * p, p)],
                send_sem=s2.at[r - 1], recv_sem=r2.at[r - 1], device_id=to, device_id_type=MESH)
            cp.start()
            seconds.append(cp)
        for r in range(1, 8):
            to, d = peer(r)
            pltpu.make_async_remote_copy(
                src_ref=red_ref, dst_ref=out_ref.at[pl.ds(d * p, p)],
                send_sem=s2.at[r - 1], recv_sem=r2.at[r - 1], device_id=to, device_id_type=MESH).wait_recv()
        for cp in firsts + seconds:
            cp.wait_send()

    vmem = pl.BlockSpec(memory_space=pltpu.VMEM)
    return pl.pallas_call(
        body, name=name, in_specs=[vmem], out_specs=vmem,
        out_shape=jax.ShapeDtypeStruct(buf.shape, F32),
        scratch_shapes=[pltpu.VMEM((7, p, LANES), F32), pltpu.VMEM((p, LANES), F32),
                        pltpu.SemaphoreType.DMA((7,)), pltpu.SemaphoreType.DMA((7,)),
                        pltpu.SemaphoreType.DMA((7,)), pltpu.SemaphoreType.DMA((7,))],
        compiler_params=pltpu.CompilerParams(vmem_limit_bytes=VMEM_LIMIT),
    )(buf)


def _cast_pad(name, w, rows_to, cols_to, pos):
    rows, cols = w.shape
    tr = _tile(math.gcd(rows, rows_to), 256, 16)
    assert rows % tr == 0 and rows_to % tr == 0, (rows, rows_to, tr)
    n_src = rows // tr

    def body(pos_ref, w_ref, o_ref):
        i = pl.program_id(0)
        if cols_to > cols:
            o_ref[:, cols:] = jnp.zeros((tr, cols_to - cols), BF16)

        @pl.when(i < n_src)
        def _():
            o_ref[:, :cols] = w_ref[...].astype(BF16)

        if rows_to > rows:
            @pl.when(i >= n_src)
            def _():
                o_ref[:, :cols] = jnp.zeros((tr, cols), BF16)

    grid_spec = pltpu.PrefetchScalarGridSpec(
        num_scalar_prefetch=1, grid=(rows_to // tr,),
        in_specs=[pl.BlockSpec((tr, cols), lambda i, p: (jnp.minimum(i, n_src - 1), 0))],
        out_specs=pl.BlockSpec((None, tr, cols_to), lambda i, p: (p[0], i, 0)))
    return pl.pallas_call(
        body, name=name, grid_spec=grid_spec,
        out_shape=jax.ShapeDtypeStruct((N_CHIPS, rows_to, cols_to), BF16),
        compiler_params=_params(("parallel",)),
    )(pos, w)


def _scan_rows(a, b, row, tr, reverse):
    sh = 1
    while sh < tr:
        if reverse:
            valid = row < tr - sh
            a_s = pltpu.roll(a, tr - sh, 0)
            b_s = pltpu.roll(b, tr - sh, 0)
        else:
            valid = row >= sh
            a_s = pltpu.roll(a, sh, 0)
            b_s = pltpu.roll(b, sh, 0)
        b = b + jnp.where(valid, a * b_s, 0.0)
        a = jnp.where(valid, a * a_s, a)
        sh *= 2
    return a, b


def _linear_scan(name, a, b, tr, reverse=False, shift_a=False):
    t_rows, cols = a.shape
    tc = _tile(cols, 512)
    ni, nj = t_rows // tr, cols // tc
    hb = tr // 8

    def rmap(j, i):
        return ((ni - 1 - i) if reverse else i, j)

    def halo_map(j, i):
        ri = ni - 1 - i
        return (jnp.minimum((ri + 1) * hb, t_rows // 8 - 1), j)

    def body(*refs):
        if shift_a:
            a_ref, halo_ref, b_ref, h_ref, carry_ref = refs
        else:
            a_ref, b_ref, h_ref, carry_ref = refs
        i = pl.program_id(1)
        row = lax.broadcasted_iota(jnp.int32, (tr, tc), 0)

        @pl.when(i == 0)
        def _():
            carry_ref[...] = jnp.zeros_like(carry_ref)

        av = a_ref[...]
        if shift_a:
            nxt = jnp.where(i > 0, halo_ref[0:1, :], 0.0)
            av = jnp.where(row == tr - 1, jnp.broadcast_to(nxt, (tr, tc)), pltpu.roll(av, tr - 1, 0))
        pa, hb_ = _scan_rows(av, b_ref[...], row, tr, reverse)
        h = hb_ + pa * carry_ref[0:1, :]
        h_ref[...] = h
        last = h[0:1, :] if reverse else h[tr - 1:tr, :]
        carry_ref[...] = jnp.broadcast_to(last, carry_ref.shape)

    in_specs = [pl.BlockSpec((tr, tc), rmap)]
    args = [a]
    if shift_a:
        in_specs.append(pl.BlockSpec((8, tc), halo_map))
        args.append(a)
    in_specs.append(pl.BlockSpec((tr, tc), rmap))
    args.append(b)
    return pl.pallas_call(
        body, name=name, grid=(nj, ni), in_specs=in_specs, out_specs=pl.BlockSpec((tr, tc), rmap),
        out_shape=jax.ShapeDtypeStruct((t_rows, cols), F32), scratch_shapes=[pltpu.VMEM((8, tc), F32)],
        compiler_params=_params(("parallel", "arbitrary")),
    )(*args)


def _conv_fwd(name, proj, conv_w, conv_b, width, tr):
    t_rows = proj.shape[0]
    ni = t_rows // tr
    hb = tr // 8
    tc = _tile(width, 512)

    def body(u_ref, halo_ref, w_ref, b_ref, o_ref, ext_ref):
        i = pl.program_id(1)
        ext_ref[0:8, :] = jnp.where(i > 0, halo_ref[...], 0.0)
        ext_ref[8:8 + tr, :] = u_ref[...]
        acc = b_ref[...] + w_ref[CONV_WIDTH - 1:CONV_WIDTH, :] * u_ref[...]
        for k in range(CONV_WIDTH - 1):
            acc = acc + w_ref[k:k + 1, :] * ext_ref[pl.ds(8 - (CONV_WIDTH - 1) + k, tr), :]
        o_ref[...] = acc

    return pl.pallas_call(
        body, name=name, grid=(width // tc, ni),
        in_specs=[pl.BlockSpec((tr, tc), lambda j, i: (i, j)),
                  pl.BlockSpec((8, tc), lambda j, i: (jnp.maximum(i * hb - 1, 0), j)),
                  pl.BlockSpec((CONV_WIDTH, tc), lambda j, i: (0, j)),
                  pl.BlockSpec((1, tc), lambda j, i: (0, j))],
        out_specs=pl.BlockSpec((tr, tc), lambda j, i: (i, j)),
        out_shape=jax.ShapeDtypeStruct((t_rows, width), F32),
        scratch_shapes=[pltpu.VMEM((tr + 8, tc), F32)],
        compiler_params=_params(("parallel", "parallel")),
    )(proj, proj, conv_w, conv_b)


def _conv_bwd(name, dxc, proj, conv_w, width, tr):
    t_rows = dxc.shape[0]
    ni = t_rows // tr
    hb = tr // 8
    tc = _tile(width, 512)

    def body(d_ref, dnext_ref, u_ref, uprev_ref, w_ref, du_ref, dw_ref, db_ref, dext_ref, uext_ref):
        i = pl.program_id(1)
        dext_ref[0:tr, :] = d_ref[...]
        dext_ref[tr:tr + 8, :] = jnp.where(i < ni - 1, dnext_ref[...], 0.0)
        uext_ref[0:8, :] = jnp.where(i > 0, uprev_ref[...], 0.0)
        uext_ref[8:8 + tr, :] = u_ref[...]
        d = d_ref[...]
        du = w_ref[CONV_WIDTH - 1:CONV_WIDTH, :] * d
        dws = []
        for k in range(CONV_WIDTH - 1):
            du = du + w_ref[k:k + 1, :] * dext_ref[pl.ds(CONV_WIDTH - 1 - k, tr), :]
            dws.append(jnp.sum(d * uext_ref[pl.ds(8 - (CONV_WIDTH - 1) + k, tr), :], axis=0, keepdims=True))
        dws.append(jnp.sum(d * u_ref[...], axis=0, keepdims=True))
        du_ref[...] = du
        dw = jnp.concatenate(dws, axis=0)
        db = jnp.sum(d, axis=0, keepdims=True)

        @pl.when(i == 0)
        def _():
            dw_ref[...] = dw
            db_ref[...] = db

        @pl.when(i > 0)
        def _():
            dw_ref[...] += dw
            db_ref[...] += db

    return pl.pallas_call(
        body, name=name, grid=(width // tc, ni),
        in_specs=[pl.BlockSpec((tr, tc), lambda j, i: (i, j)),
                  pl.BlockSpec((8, tc), lambda j, i: (jnp.minimum((i + 1) * hb, t_rows // 8 - 1), j)),
                  pl.BlockSpec((tr, tc), lambda j, i: (i, j)),
                  pl.BlockSpec((8, tc), lambda j, i: (jnp.maximum(i * hb - 1, 0), j)),
                  pl.BlockSpec((CONV_WIDTH, tc), lambda j, i: (0, j))],
        out_specs=[pl.BlockSpec((tr, tc), lambda j, i: (i, j)),
                   pl.BlockSpec((CONV_WIDTH, tc), lambda j, i: (0, j)),
                   pl.BlockSpec((1, tc), lambda j, i: (0, j))],
        out_shape=[jax.ShapeDtypeStruct((t_rows, width), F32), jax.ShapeDtypeStruct((CONV_WIDTH, width), F32),
                   jax.ShapeDtypeStruct((1, width), F32)],
        scratch_shapes=[pltpu.VMEM((tr + 8, tc), F32), pltpu.VMEM((tr + 8, tc), F32)],
        compiler_params=_params(("parallel", "arbitrary")),
    )(dxc, dxc, proj, proj, conv_w)


def _s5_disc_math(lam_re, lam_im, log_dt, b_re, b_im, expand):
    dt = jnp.exp(log_dt)
    zr, zi = lam_re * dt, lam_im * dt
    mag = jnp.exp(zr)
    lbr, lbi = mag * jnp.cos(zi), mag * jnp.sin(zi)
    ar, ai = lbr - 1.0, lbi
    den = lam_re * lam_re + lam_im * lam_im
    cr = (ar * lam_re + ai * lam_im) / den
    ci = (ai * lam_re - ar * lam_im) / den
    cre = jnp.dot(cr, expand, precision=lax.Precision.HIGHEST, preferred_element_type=F32)
    cie = jnp.dot(ci, expand, precision=lax.Precision.HIGHEST, preferred_element_type=F32)
    return lbr, lbi, cre * b_re - cie * b_im, cre * b_im + cie * b_re


def _s5_disc(name, lam_re, lam_im, log_dt, b_re, b_im, expand, cots=None):
    ins = [lam_re, lam_im, log_dt, b_re, b_im, expand]
    n_in = len(ins) + (len(cots) if cots else 0)

    def body(*refs):
        vals = [r[...] for r in refs[:6]]
        outs = refs[n_in:]
        if cots is None:
            res = _s5_disc_math(*vals)
        else:
            cv = tuple(r[...] for r in refs[6:n_in])
            _, vjp = jax.vjp(lambda a, b, c, d, e: _s5_disc_math(a, b, c, d, e, vals[5]), *vals[:5])
            res = vjp(cv)
        for o, r in zip(outs, res):
            o[...] = r

    if cots is None:
        shapes = [lam_re.shape, lam_re.shape, b_re.shape, b_re.shape]
    else:
        shapes = [lam_re.shape, lam_re.shape, log_dt.shape, b_re.shape, b_re.shape]
    vmem = pl.BlockSpec(memory_space=pltpu.VMEM)
    return pl.pallas_call(
        body, name=name, in_specs=[vmem] * n_in, out_specs=[vmem] * len(shapes),
        out_shape=[jax.ShapeDtypeStruct(s, F32) for s in shapes],
        compiler_params=pltpu.CompilerParams(vmem_limit_bytes=VMEM_LIMIT),
    )(*ins, *(cots or ()))


def _cmul(ar, ai, br, bi):
    return ar * br - ai * bi, ar * bi + ai * br


SCAN_BLOCK = 8


def _block_scan(br, bi, lr, li, row, rows, reverse):
    pr, pi = lr, li
    pos = row & (SCAN_BLOCK - 1)
    sh = 1
    while sh < SCAN_BLOCK:
        if reverse:
            valid = pos < SCAN_BLOCK - sh
            rs, is_ = pltpu.roll(br, rows - sh, 0), pltpu.roll(bi, rows - sh, 0)
        else:
            valid = pos >= sh
            rs, is_ = pltpu.roll(br, sh, 0), pltpu.roll(bi, sh, 0)
        mr, mi = _cmul(pr, pi, rs, is_)
        br = br + jnp.where(valid, mr, 0.0)
        bi = bi + jnp.where(valid, mi, 0.0)
        pr, pi = _cmul(pr, pi, pr, pi)
        sh *= 2
    return br, bi


def _block_powers(lr, li, hw, reverse):
    row = lax.broadcasted_iota(jnp.int32, (SCAN_BLOCK, hw), 0)
    d = jnp.where(row == (SCAN_BLOCK - 1 if reverse else 0), 1.0, 0.0)
    return _block_scan(jnp.broadcast_to(lr, (SCAN_BLOCK, hw)) * d, jnp.broadcast_to(li, (SCAN_BLOCK, hw)) * d,
                       lr, li, row, SCAN_BLOCK, reverse)


def _chain_blocks(s_ref, pow_ref, carry, rows, hw, reverse):
    nblk = rows // SCAN_BLOCK
    pr, pi = pow_ref[:, :hw], pow_ref[:, hw:]

    def step(n, c):
        b = (nblk - 1 - n) if reverse else n
        at = pl.ds(pl.multiple_of(b * SCAN_BLOCK, SCAN_BLOCK), SCAN_BLOCK)
        blk = s_ref[at, :]
        kr, ki = _cmul(pr, pi, c[0], c[1])
        sr, si = blk[:, :hw] + kr, blk[:, hw:] + ki
        s_ref[at, :] = jnp.concatenate([sr, si], axis=1)
        edge = slice(0, 1) if reverse else slice(SCAN_BLOCK - 1, SCAN_BLOCK)
        return sr[edge, :], si[edge, :]

    return lax.fori_loop(0, nblk, step, carry)


def _s5_fwd(name, proj, col0, bblk, cblk, lamblk, dvec, width, tr):
    t_rows = proj.shape[0]
    ni = t_rows // tr
    nb = width // LANES
    sw = bblk.shape[-1]
    hw = sw // 2

    def body(u_ref, b_ref, c_ref, lam_ref, d_ref, y_ref, s_ref, carry_ref, pow_ref, st_ref):
        i = pl.program_id(1)
        row = lax.broadcasted_iota(jnp.int32, (tr, hw), 0)
        lam = lam_ref[...]
        lr, li = lam[:, :hw], lam[:, hw:]

        @pl.when(i == 0)
        def _():
            carry_ref[...] = jnp.zeros_like(carry_ref)
            zr, zi = _block_powers(lr, li, hw, False)
            pow_ref[:, :hw] = zr
            pow_ref[:, hw:] = zi

        u = u_ref[...]
        bu = jnp.dot(u.astype(BF16), b_ref[...], preferred_element_type=F32)
        sr, si = _block_scan(bu[:, :hw], bu[:, hw:], lr, li, row, tr, False)
        st_ref[:, :hw] = sr
        st_ref[:, hw:] = si
        cr, ci = _chain_blocks(st_ref, pow_ref, (carry_ref[0:1, :hw], carry_ref[0:1, hw:]), tr, hw, False)
        carry_ref[...] = jnp.broadcast_to(jnp.concatenate([cr, ci], axis=1), carry_ref.shape)
        s16 = st_ref[...].astype(BF16)
        s_ref[...] = s16
        y_ref[...] = jnp.dot(s16, c_ref[...], preferred_element_type=F32) + d_ref[...] * u

    return pl.pallas_call(
        body, name=name, grid=(nb, ni),
        in_specs=[pl.BlockSpec((tr, LANES), lambda k, i: (i, col0 + k)),
                  pl.BlockSpec((None, LANES, sw), lambda k, i: (k, 0, 0)),
                  pl.BlockSpec((None, sw, LANES), lambda k, i: (k, 0, 0)),
                  pl.BlockSpec((None, 1, sw), lambda k, i: (k, 0, 0)),
                  pl.BlockSpec((1, LANES), lambda k, i: (0, k))],
        out_specs=[pl.BlockSpec((tr, LANES), lambda k, i: (i, k)),
                   pl.BlockSpec((tr, sw), lambda k, i: (i, k))],
        out_shape=[jax.ShapeDtypeStruct((t_rows, width), F32), jax.ShapeDtypeStruct((t_rows, nb * sw), BF16)],
        scratch_shapes=[pltpu.VMEM((8, sw), F32), pltpu.VMEM((SCAN_BLOCK, sw), F32), pltpu.VMEM((tr, sw), F32)],
        compiler_params=_params(("parallel", "arbitrary")),
    )(proj, bblk, cblk, lamblk, dvec)


def _s5_bwd(name, dy, proj, col0, states, bblk, cblk, lamblk, dvec, width, tr):
    t_rows = dy.shape[0]
    ni = t_rows // tr
    nb = width // LANES
    sw = bblk.shape[-1]
    hw = sw // 2
    hb16 = tr // 16

    def rmap(k, i):
        return (ni - 1 - i, k)

    def body(dy_ref, u_ref, s_ref, sprev_ref, b_ref, c_ref, lam_ref, d_ref,
             du_ref, db_ref, dc_ref, dlam_ref, dd_ref, carry_ref, pow_ref, gt_ref):
        i = pl.program_id(1)
        row = lax.broadcasted_iota(jnp.int32, (tr, hw), 0)
        row_w = lax.broadcasted_iota(jnp.int32, (tr, sw), 0)
        lam = lam_ref[...]
        lr, li = lam[:, :hw], -lam[:, hw:]

        @pl.when(i == 0)
        def _():
            carry_ref[...] = jnp.zeros_like(carry_ref)
            zr, zi = _block_powers(lr, li, hw, True)
            pow_ref[:, :hw] = zr
            pow_ref[:, hw:] = zi

        dyv = dy_ref[...]
        dy16 = dyv.astype(BF16)
        u = u_ref[...]
        gd = lax.dot_general(dy16, c_ref[...], _DN['nt'], preferred_element_type=F32)
        gr, gi = _block_scan(gd[:, :hw], gd[:, hw:], lr, li, row, tr, True)
        gt_ref[:, :hw] = gr
        gt_ref[:, hw:] = gi
        cr, ci = _chain_blocks(gt_ref, pow_ref, (carry_ref[0:1, :hw], carry_ref[0:1, hw:]), tr, hw, True)
        carry_ref[...] = jnp.broadcast_to(jnp.concatenate([cr, ci], axis=1), carry_ref.shape)
        g = gt_ref[...]
        gr, gi = g[:, :hw], g[:, hw:]
        g16 = g.astype(BF16)
        du_ref[...] = lax.dot_general(g16, b_ref[...], _DN['nt'], preferred_element_type=F32) + d_ref[...] * dyv
        s16 = s_ref[...]
        dbv = lax.dot_general(g16, u.astype(BF16), _DN['tn'], preferred_element_type=F32)
        dcv = lax.dot_general(s16, dy16, _DN['tn'], preferred_element_type=F32)
        s32 = s16.astype(F32)
        first = jnp.where(i < ni - 1, sprev_ref[15:16, :].astype(F32), 0.0)
        sp = jnp.where(row_w == 0, jnp.broadcast_to(first, (tr, sw)), pltpu.roll(s32, 1, 0))
        spr, spi = sp[:, :hw], sp[:, hw:]
        dlr = jnp.sum(gr * spr + gi * spi, axis=0, keepdims=True)
        dli = jnp.sum(gi * spr - gr * spi, axis=0, keepdims=True)
        dlam = jnp.concatenate([dlr, dli], axis=1)
        ddv = jnp.sum(dyv * u, axis=0, keepdims=True)

        @pl.when(i == 0)
        def _():
            db_ref[...] = dbv
            dc_ref[...] = dcv
            dlam_ref[...] = dlam
            dd_ref[...] = ddv

        @pl.when(i > 0)
        def _():
            db_ref[...] += dbv
            dc_ref[...] += dcv
            dlam_ref[...] += dlam
            dd_ref[...] += ddv

    return pl.pallas_call(
        body, name=name, grid=(nb, ni),
        in_specs=[pl.BlockSpec((tr, LANES), rmap),
                  pl.BlockSpec((tr, LANES), lambda k, i: (ni - 1 - i, col0 + k)),
                  pl.BlockSpec((tr, sw), rmap),
                  pl.BlockSpec((16, sw), lambda k, i: (jnp.maximum((ni - 1 - i) * hb16 - 1, 0), k)),
                  pl.BlockSpec((None, LANES, sw), lambda k, i: (k, 0, 0)),
                  pl.BlockSpec((None, sw, LANES), lambda k, i: (k, 0, 0)),
                  pl.BlockSpec((None, 1, sw), lambda k, i: (k, 0, 0)),
                  pl.BlockSpec((1, LANES), lambda k, i: (0, k))],
        out_specs=[pl.BlockSpec((tr, LANES), rmap),
                   pl.BlockSpec((None, sw, LANES), lambda k, i: (k, 0, 0)),
                   pl.BlockSpec((None, sw, LANES), lambda k, i: (k, 0, 0)),
                   pl.BlockSpec((None, 1, sw), lambda k, i: (k, 0, 0)),
                   pl.BlockSpec((1, LANES), lambda k, i: (0, k))],
        out_shape=[jax.ShapeDtypeStruct((t_rows, width), F32), jax.ShapeDtypeStruct((nb, sw, LANES), F32),
                   jax.ShapeDtypeStruct((nb, sw, LANES), F32), jax.ShapeDtypeStruct((nb, 1, sw), F32),
                   jax.ShapeDtypeStruct((1, width), F32)],
        scratch_shapes=[pltpu.VMEM((8, sw), F32), pltpu.VMEM((SCAN_BLOCK, sw), F32), pltpu.VMEM((tr, sw), F32)],
        compiler_params=_params(("parallel", "arbitrary")),
    )(dy, proj, states, states, bblk, cblk, lamblk, dvec)


def _blockdiag(m):
    nb, g, p, q = m.shape
    eye = jnp.eye(g, dtype=m.dtype)
    return (m[:, :, :, None, :] * eye[None, :, None, :, None]).reshape(nb, g * p, g * q)


def _blockdiag_take(m, p, q):
    nb = m.shape[0]
    g = GROUPS_PER_BLOCK
    m = m.reshape(nb, g, p, g, q)
    return jnp.stack([m[:, k, :, k, :] for k in range(g)], axis=1)


def _adamw(name, w, g, m, v, tr):
    rows, cols = w.shape
    gcols = g.shape[1]
    c1 = 1.0 - ADAM_B1 ** ADAM_STEP
    c2 = 1.0 - ADAM_B2 ** ADAM_STEP

    def body(w_ref, g_ref, m_ref, v_ref, go_ref, d_ref, mo_ref, vo_ref):
        gv = g_ref[...] if gcols == cols else g_ref[:, :cols]
        mn = ADAM_B1 * m_ref[...] + (1.0 - ADAM_B1) * gv
        vn = ADAM_B2 * v_ref[...] + (1.0 - ADAM_B2) * (gv * gv)
        go_ref[...] = gv
        mo_ref[...] = mn
        vo_ref[...] = vn
        d_ref[...] = -ADAM_LR * ((mn / c1) / (jnp.sqrt(vn / c2) + ADAM_EPS) + ADAM_WD * w_ref[...])

    spec = pl.BlockSpec((tr, cols), lambda i: (i, 0))
    return pl.pallas_call(
        body, name=name, grid=(rows // tr,),
        in_specs=[spec, pl.BlockSpec((tr, gcols), lambda i: (i, 0)), spec, spec],
        out_specs=[spec] * 4, out_shape=[jax.ShapeDtypeStruct((rows, cols), F32)] * 4,
        compiler_params=_params(("parallel",)),
    )(w, g, m, v)


def _ffn_fwd(tag, h, norm, wg, wu, wd, dims, host=None):
    t_rows, d, fp, tm, tr, trn = dims['T'], dims['D'], dims['FP'], dims['TM'], dims['TR'], dims['TRN']
    host = host or {}
    got = {}
    ft = N_CHIPS * fp
    tn = _tile(fp, 1408)
    npb = fp // tn
    n16, = _rowmap(f"{tag}_norm", lambda x, g: _rms(x, g).astype(BF16), (1, t_rows // trn),
                   [_rows(h, trn), _full(norm)], [_o((t_rows, d), BF16, trn)])

    def up(nm, w):
        res = _matmul(f"{tag}_{nm}", 'nt', n16, (tm // 2, d // 2), lambda i, j, k: (i, k),
                      w, (None, fp, d // 2), lambda i, j, k: (j, 0, k),
                      (2 * t_rows // tm, N_CHIPS, 2), [((t_rows, ft), BF16, (tm // 2, fp), lambda i, j: (i, j))],
                      lambda acc: acc, comms=host.get(nm, ()))
        got[nm] = res[1:]
        return res[0]

    gate = up("gate", wg)
    wu = got['gate'][wu] if isinstance(wu, int) else wu
    upv = up("up", wu)
    wd = got['up'][wd] if isinstance(wd, int) else wd
    tc = _tile(ft, 1408)
    act, = _rowmap(f"{tag}_act", lambda g, u: _silu(g.astype(F32)) * u.astype(F32), (ft // tc, t_rows // tr),
                   [_rows(gate, tr, tc), _rows(upv, tr, tc)], [_o((t_rows, ft), BF16, tr, tc)])
    tnd = _tile(d, 1024)
    res = _matmul(f"{tag}_down", 'nn', act, (tm, fp), lambda i, j, k: (i, k),
                  wd, (None, fp, tnd), lambda i, j, k: (k, 0, j),
                  (t_rows // tm, d // tnd, N_CHIPS), [((t_rows, d), F32, (tm, tnd), lambda i, j: (i, j))],
                  lambda acc, hin: hin + 0.5 * acc, extras=[(h, (tm, tnd), lambda i, j: (i, j))],
                  comms=host.get('down', ()))
    got['down'] = res[1:]
    return res[0], dict(h=h, n16=n16, gate=gate, up=upv, act=act, wu=wu, wd=wd), got


def _pair(nm, g, pos):
    return _pair_sum(f"pair_{nm}", g, _run_comm(f"swap_{nm}", _swap_comm(g))[0], pos)


def _ffn_bwd(tag, names, dh, dh16, saved, norm, wg, dims, pos, host=None):
    t_rows, d, fp, tm, trn = dims['T'], dims['D'], dims['FP'], dims['TM'], dims['TRN']
    host = host or {}
    got = {}
    n_gate, n_up, n_down = names
    wu, wd = saved['wu'], saved['wd']
    ft = N_CHIPS * fp
    tn = _tile(fp, 1408)
    npb = fp // tn
    tm2 = tm // 2
    tk = t_rows // 2

    def act_bwd(acc, g, u):
        da = 0.5 * acc
        g32, u32 = g.astype(F32), u.astype(F32)
        sg = _sigmoid(g32)
        return da * u32 * (sg * (1.0 + g32 * (1.0 - sg))), da * (g32 * sg)

    res = _matmul(
        f"{tag}_dact", 'nt', dh16, (tm2, d // 2), lambda i, j, k: (i, k),
        wd, (None, tn, d // 2), lambda i, j, k: (j // npb, j % npb, k),
        (t_rows // tm2, ft // tn, 2),
        [((t_rows, ft), BF16, (tm2, tn), lambda i, j: (i, j))] * 2, act_bwd,
        extras=[(saved['gate'], (tm2, tn), lambda i, j: (i, j)), (saved['up'], (tm2, tn), lambda i, j: (i, j))],
        comms=host.get('dact', ()))
    dgate, dup, got['dact'] = res[0], res[1], res[2:]
    tnd = _tile(d, 1024)
    tw = _tile(d, 512)
    res = _matmul(f"{tag}_dwd", 'tn', saved['act'], (t_rows, tn), lambda i, j, k: (0, i),
                  dh16, (t_rows, tw), lambda i, j, k: (0, j),
                  (ft // tn, d // tw, 1),
                  [((N_CHIPS, fp, d), BF16, (None, tn, tw), lambda i, j: (i // npb, i % npb, j))],
                  lambda acc: 0.5 * acc, comms=host.get('dwd', ()))
    dwd, got['dwd'] = res[0], res[1:]

    def dw_up(nm, dact, comms):
        return _matmul(f"{tag}_{nm}", 'tn', dact, (t_rows, tn), lambda i, j, k: (0, i),
                       saved['n16'], (t_rows, tw), lambda i, j, k: (0, j),
                       (ft // tn, d // tw, 1),
                       [((N_CHIPS, fp, d), BF16, (None, tn, tw), lambda i, j: (i // npb, i % npb, j))],
                       lambda acc: acc, comms=comms)

    def dn_part(nm, dact, w, prev, comms):
        extras = [] if prev is None else [(prev, (tm, tnd), lambda i, j: (i, j))]
        return _matmul(f"{tag}_{nm}", 'nn', dact, (tm, fp), lambda i, j, k: (i, k),
                       w, (None, fp, tnd), lambda i, j, k: (k, 0, j),
                       (t_rows // tm, d // tnd, N_CHIPS), [((t_rows, d), F32, (tm, tnd), lambda i, j: (i, j))],
                       (lambda acc: acc) if prev is None else (lambda acc, p: acc + p), extras=extras, comms=comms)

    dwg, swap_d = dw_up("dwg", dgate, [_swap_comm(dwd)])
    pair_d = _pair_sum(f"pair_{n_down}", dwd, swap_d, pos)
    dwu, land_d, swap_g = dw_up("dwu", dup, [_scatter_comm(pair_d), _swap_comm(dwg)])
    pair_g = _pair_sum(f"pair_{n_gate}", dwg, swap_g, pos)
    dn_g, land_g, swap_u = dn_part("dn_gate", dgate, wg, None, [_scatter_comm(pair_g), _swap_comm(dwu)])
    pair_u = _pair_sum(f"pair_{n_up}", dwu, swap_u, pos)
    dn, land_u = dn_part("dn_up", dup, wu, dn_g, [_scatter_comm(pair_u)])

    def norm_bwd(x, dnv, dhv, g):
        _, vjp = jax.vjp(_rms, x, g)
        dx, dg = vjp(dnv)
        res = dhv + dx
        return res, res.astype(BF16), dg

    dh_in, dh_in16, dnorm = _rowmap(
        f"{tag}_dnorm", norm_bwd, (1, t_rows // trn),
        [_rows(saved['h'], trn), _rows(dn, trn), _rows(dh, trn), _full(norm)],
        [_o((t_rows, d), F32, trn), _o((t_rows, d), BF16, trn)], [_acc((1, d))])
    parts = {n_down: (pair_d, land_d), n_gate: (pair_g, land_g), n_up: (pair_u, land_u)}
    return dh_in, dh_in16, dnorm, parts, got


def kernel(x, meta_tokens, ffn1_norm, ffn1_w_gate, ffn1_w_up, ffn1_w_down, mix_norm, w_in, rg_conv_w, rg_conv_b, rg_w_a, rg_b_a, rg_w_x, rg_b_x, rg_lambda, s5_lambda_re, s5_lambda_im, s5_log_dt, s5_b_re, s5_b_im, s5_c_re, s5_c_im, s5_d, s5_glu_w, s5_glu_b, rg_out_norm, s5_out_norm, w_out, ffn2_norm, ffn2_w_gate, ffn2_w_up, ffn2_w_down, final_norm, loss_target, m_meta_tokens, m_ffn1_norm, m_ffn1_w_gate, m_ffn1_w_up, m_ffn1_w_down, m_mix_norm, m_w_in, m_rg_conv_w, m_rg_conv_b, m_rg_w_a, m_rg_b_a, m_rg_w_x, m_rg_b_x, m_rg_lambda, m_s5_lambda_re, m_s5_lambda_im, m_s5_log_dt, m_s5_b_re, m_s5_b_im, m_s5_c_re, m_s5_c_im, m_s5_d, m_s5_glu_w, m_s5_glu_b, m_rg_out_norm, m_s5_out_norm, m_w_out, m_ffn2_norm, m_ffn2_w_gate, m_ffn2_w_up, m_ffn2_w_down, m_final_norm, v_meta_tokens, v_ffn1_norm, v_ffn1_w_gate, v_ffn1_w_up, v_ffn1_w_down, v_mix_norm, v_w_in, v_rg_conv_w, v_rg_conv_b, v_rg_w_a, v_rg_b_a, v_rg_w_x, v_rg_b_x, v_rg_lambda, v_s5_lambda_re, v_s5_lambda_im, v_s5_log_dt, v_s5_b_re, v_s5_b_im, v_s5_c_re, v_s5_c_im, v_s5_d, v_s5_glu_w, v_s5_glu_b, v_rg_out_norm, v_s5_out_norm, v_w_out, v_ffn2_norm, v_ffn2_w_gate, v_ffn2_w_up, v_ffn2_w_down, v_final_norm):
    args = locals()
    w = {n: args[n] for n in WEIGHTS}
    mom = {n: args["m_" + n] for n in WEIGHTS}
    var = {n: args["v_" + n] for n in WEIGHTS}

    seq, d = x.shape[1], x.shape[2]
    f_shard = ffn1_w_gate.shape[2]
    fp = _round_up(f_shard, LANES)
    r = rg_conv_b.shape[1]
    s5w = s5_d.shape[1]
    hd = r // RG_HEADS
    groups = s5w // S5_GROUP
    nb = groups // GROUPS_PER_BLOCK
    t_rows = _round_up(N_META + seq, LANES)
    tm, tr, tk = t_rows // 4, t_rows // 8, t_rows // 2
    trn = _tile(t_rows, max(t_rows // 32, 16), 16)
    dims = dict(T=t_rows, D=d, FP=fp, TM=tm, TR=tr, TRN=trn)
    cx, cy, cc = lax.axis_index("x"), lax.axis_index("y"), lax.axis_index("c")
    chip = 2 * cx + cy
    pos = jnp.stack([chip, cc]).astype(jnp.int32)

    def gather(nm, buf):
        return _run_comm(f"ag_{nm}", _gather_comm(buf))[0]

    def gather_f32(nm, local):
        buf = lax.dynamic_update_slice(jnp.zeros((N_CHIPS,) + local.shape, F32), local[None], (chip, 0, 0))
        return gather(nm, buf)

    slot = {}
    transposed = ('ffn1_w_gate', 'ffn1_w_up', 'ffn2_w_gate', 'ffn2_w_up')
    local2d = lambda tree, nm: jnp.swapaxes(tree[nm][0], 0, 1) if nm in transposed else tree[nm][0]
    for nm in transposed:
        slot[nm] = _cast_pad(f"cast_{nm}", local2d(w, nm), fp, d, pos)
    for nm in ('ffn1_w_down', 'ffn2_w_down'):
        slot[nm] = _cast_pad(f"cast_{nm}", w[nm][0], fp, d, pos)
    slot['w_in'] = _cast_pad("cast_w_in", w_in[0], d, w_in.shape[2], pos)
    slot['w_out'] = _cast_pad("cast_w_out", w_out[0], w_out.shape[1], d, pos)
    slot['s5_glu_w'] = _cast_pad("cast_glu", s5_glu_w[0], s5_glu_w.shape[1], s5w, pos)
    rgw_local = jnp.concatenate([rg_w_a[0], rg_w_x[0]], axis=0).reshape(2 * RG_HEADS * (hd // N_CHIPS), hd)
    rgw = gather('rgw', _cast_pad("cast_rgw", rgw_local, rgw_local.shape[0], hd, pos))
    rgw = rgw.reshape(N_CHIPS, 2, RG_HEADS, hd // N_CHIPS, hd).transpose(1, 2, 0, 3, 4).reshape(2, RG_HEADS, hd, hd)
    w_a16, w_x16 = rgw[0], rgw[1]
    meta_full = gather_f32('meta', meta_tokens).transpose(1, 0, 2).reshape(N_META, d)
    conv_w2, b_a2, b_x2 = _gather_small_shards(rg_conv_w[0], rg_b_a[0], rg_b_x[0], gather_f32)
    wfull = {'ffn1_w_gate': gather('ffn1_w_gate', slot['ffn1_w_gate'])}

    pad_rows = t_rows - N_META - seq
    h0 = jnp.concatenate([meta_full, x[0], jnp.zeros((pad_rows, d), F32)], axis=0)
    tgt = jnp.concatenate([jnp.zeros((N_META, d), F32), loss_target[0], jnp.zeros((pad_rows, d), F32)], axis=0)

    h1, sv1, got = _ffn_fwd(
        "ffn1", h0, ffn1_norm, wfull['ffn1_w_gate'], 0, 0, dims,
        host={'gate': [_gather_comm(slot['ffn1_w_up'])], 'up': [_gather_comm(slot['ffn1_w_down'])],
              'down': [_gather_comm(slot['w_in']), _gather_comm(slot['s5_glu_w'])]})
    wfull['w_in'] = got['down'][0]
    wfull['s5_glu_w'] = got['down'][1].reshape(s5w, s5w)

    n2, = _rowmap("mix_norm", lambda xv, g: _rms(xv, g).astype(BF16), (1, t_rows // trn),
                  [_rows(h1, trn), _full(mix_norm)], [_o((t_rows, d), BF16, trn)])
    pw = w_in.shape[2]
    tnp = _tile(pw, 1536)
    nppb = pw // tnp
    proj, w_out_full = _matmul(
        "in_proj", 'nn', n2, (tm, d // 2), lambda i, j, k: (i, k),
        wfull['w_in'], (None, d // 2, tnp), lambda i, j, k: (j // nppb, k, j % nppb),
        (t_rows // tm, N_CHIPS * nppb, 2), [((t_rows, 3 * r), F32, (tm, tnp), lambda i, j: (i, j))],
        lambda acc: acc, comms=[_gather_comm(slot['w_out'])])
    wfull['w_out'] = w_out_full.reshape(d, d)

    xc = _conv_fwd("rg_conv", proj, conv_w2, rg_conv_b, r, tr)

    def head_cols(arr):
        return (arr, (tr, hd), lambda j, i: (i, j))

    def head_w(arr):
        return (arr, (None, hd, hd), lambda j, i: (j, 0, 0))

    def head_vec(arr):
        return (arr, (1, hd), lambda j, i: (0, j))

    def gates_fwd(xcv, wa, wx, ba, bx, lam):
        x16 = xcv.astype(BF16)
        pre_r = jnp.dot(x16, wa, preferred_element_type=F32) + ba
        pre_i = jnp.dot(x16, wx, preferred_element_type=F32) + bx
        return _rg_gate_math(xcv, pre_r, pre_i, _softplus_neg(lam))

    gate_params = [head_w(w_a16), head_w(w_x16), head_vec(b_a2), head_vec(b_x2), head_vec(rg_lambda)]
    head_out = ((t_rows, r), F32, (tr, hd), lambda j, i: (i, j))
    a_dec, bxv = _rowmap("rg_gates", gates_fwd, (RG_HEADS, t_rows // tr), [head_cols(xc)] + gate_params,
                         [head_out, head_out])
    h_rg = _linear_scan("rg_scan", a_dec, bxv, tr)

    expand = jnp.repeat(jnp.eye(S5_STATE, dtype=F32), S5_GROUP, axis=1)
    b_re2 = s5_b_re[0].reshape(groups, S5_STATE * S5_GROUP)
    b_im2 = s5_b_im[0].reshape(groups, S5_STATE * S5_GROUP)
    log_dt2 = s5_log_dt[0].reshape(groups, 1)
    lbr, lbi, bbr, bbi = _s5_disc("s5_disc", s5_lambda_re[0], s5_lambda_im[0], log_dt2, b_re2, b_im2, expand)

    def to_bblk(m):
        return _blockdiag(m.reshape(nb, GROUPS_PER_BLOCK, S5_STATE, S5_GROUP).transpose(0, 1, 3, 2))

    def to_cblk(m):
        return _blockdiag(m.reshape(nb, GROUPS_PER_BLOCK, S5_GROUP, S5_STATE).transpose(0, 1, 3, 2))

    bblk = jnp.concatenate([to_bblk(bbr), to_bblk(bbi)], axis=-1).astype(BF16)
    cblk = jnp.concatenate([to_cblk(s5_c_re[0]), -to_cblk(s5_c_im[0])], axis=-2).astype(BF16)
    hw = GROUPS_PER_BLOCK * S5_STATE
    lamblk = jnp.concatenate([lbr.reshape(nb, 1, hw), lbi.reshape(nb, 1, hw)], axis=-1)
    col0 = 2 * r // LANES
    y_pre, states = _s5_fwd("s5_fwd", proj, col0, bblk, cblk, lamblk, s5_d, s5w, tr)

    z16, = _rowmap("s5_gelu", lambda yv: _gelu(yv).astype(BF16), (1, t_rows // trn),
                   [_rows(y_pre, trn)], [_o((t_rows, s5w), BF16, trn)])
    tns = _tile(s5w, 1024)

    def glu_fwd(acc, yv, b):
        gl = acc + b
        return _gelu(yv) * _sigmoid(gl), gl

    y_s5, gl = _matmul("s5_glu", 'nn', z16, (tm, s5w), lambda i, j, k: (i, 0),
                       wfull['s5_glu_w'], (s5w, tns), lambda i, j, k: (0, j),
                       (t_rows // tm, s5w // tns, 1), [((t_rows, s5w), F32, (tm, tns), lambda i, j: (i, j))] * 2,
                       glu_fwd, extras=[(y_pre, (tm, tns), lambda i, j: (i, j)), (s5_glu_b, (1, tns), lambda i, j: (0, j))])

    def mix_out(hv, gv, yv, g1, g2):
        return jnp.concatenate([_rms(hv * _gelu(gv), g1), _rms(yv, g2)], axis=1)

    gate_in = (proj, (trn, r), lambda j, i: (i, 1))
    ycat16, = _rowmap("mix_out", lambda *a: mix_out(*a).astype(BF16), (1, t_rows // trn),
                      [_rows(h_rg, trn), gate_in, _rows(y_s5, trn), _full(rg_out_norm), _full(s5_out_norm)],
                      [_o((t_rows, r + s5w), BF16, trn)])
    tnd = _tile(d, 1024)
    h2, wfull['ffn2_w_gate'] = _matmul(
        "out_proj", 'nn', ycat16, (tm, (r + s5w) // 2), lambda i, j, k: (i, k),
        wfull['w_out'], ((r + s5w) // 2, tnd), lambda i, j, k: (k, j),
        (t_rows // tm, d // tnd, 2), [((t_rows, d), F32, (tm, tnd), lambda i, j: (i, j))],
        lambda acc, hin: hin + acc, extras=[(h1, (tm, tnd), lambda i, j: (i, j))],
        comms=[_gather_comm(slot['ffn2_w_gate'])])

    h3, sv2, _ = _ffn_fwd(
        "ffn2", h2, ffn2_norm, wfull['ffn2_w_gate'], 0, 0, dims,
        host={'gate': [_gather_comm(slot['ffn2_w_up'])], 'up': [_gather_comm(slot['ffn2_w_down'])]})

    fnorm2 = final_norm.reshape(1, d)

    def head(xv, tv, g):
        i = pl.program_id(1)
        rowi = lax.broadcasted_iota(jnp.int32, (trn, 1), 0) + i * trn
        mask = jnp.where((rowi >= N_META) & (rowi < N_META + seq), 1.0, 0.0)
        out, vjp = jax.vjp(_rms, xv, g)
        err = (out - tv) * mask
        dx, dg = vjp(err * (1.0 / d))
        return dx, dx.astype(BF16), jnp.sum(err * err, axis=0, keepdims=True), dg

    dh3, dh3_16, loss_cols, d_final = _rowmap(
        "loss_head", head, (1, t_rows // trn), [_rows(h3, trn), _rows(tgt, trn), _full(fnorm2)],
        [_o((t_rows, d), F32, trn), _o((t_rows, d), BF16, trn)], [_acc((1, d)), _acc((1, d))])
    loss = lax.psum(0.5 * jnp.sum(loss_cols) / d, ("x", "y", "c"))

    small = {}
    dh2, dh2_16, small['ffn2_norm'], parts, _ = _ffn_bwd(
        "ffn2b", ('ffn2_w_gate', 'ffn2_w_up', 'ffn2_w_down'), dh3, dh3_16, sv2, ffn2_norm, wfull['ffn2_w_gate'],
        dims, pos)

    dycat, = _matmul("d_out_proj", 'nt', dh2_16, (tm, d // 2), lambda i, j, k: (i, k),
                     wfull['w_out'], (tnd, d // 2), lambda i, j, k: (j, k),
                     (t_rows // tm, (r + s5w) // tnd, 2), [((t_rows, r + s5w), F32, (tm, tnd), lambda i, j: (i, j))],
                     lambda acc: acc)
    wo_rows = (r + s5w) // N_CHIPS
    tno = _tile(wo_rows, 1024)
    npo = wo_rows // tno
    tw = _tile(d, 512)
    dw_out, = _matmul("dw_out", 'tn', ycat16, (t_rows, tno), lambda i, j, k: (0, i),
                      dh2_16, (t_rows, tw), lambda i, j, k: (0, j),
                      ((r + s5w) // tno, d // tw, 1),
                      [((N_CHIPS, wo_rows, d), BF16, (None, tno, tw), lambda i, j: (i // npo, i % npo, j))],
                      lambda acc: acc)
    pair_w_out = _pair('w_out', dw_out, pos)

    def mix_out_bwd(hv, gv, yv, dyc, g1, g2):
        _, vjp = jax.vjp(mix_out, hv, gv, yv, g1, g2)
        return vjp(dyc)

    dh_out, dgate_rg, dy_s5, small['rg_out_norm'], small['s5_out_norm'] = _rowmap(
        "mix_out_bwd", mix_out_bwd, (1, t_rows // trn),
        [_rows(h_rg, trn), gate_in, _rows(y_s5, trn), _rows(dycat, trn), _full(rg_out_norm), _full(s5_out_norm)],
        [_o((t_rows, r), F32, trn), _o((t_rows, r), F32, trn), _o((t_rows, s5w), F32, trn)],
        [_acc((1, r)), _acc((1, s5w))])

    def glu_bwd(dyv, yv, glv):
        zv = _gelu(yv)
        sg = _sigmoid(glv)
        dgl = dyv * zv * sg * (1.0 - sg)
        return dyv * sg, dgl.astype(BF16), jnp.sum(dgl, axis=0, keepdims=True)

    dz_dir, dgl16, small['s5_glu_b'] = _rowmap(
        "s5_glu_bwd", glu_bwd, (1, t_rows // trn), [_rows(dy_s5, trn), _rows(y_pre, trn), _rows(gl, trn)],
        [_o((t_rows, s5w), F32, trn), _o((t_rows, s5w), BF16, trn)], [_acc((1, s5w))])

    def dgelu(acc, dzd, yv):
        _, vjp = jax.vjp(_gelu, yv)
        return vjp(acc + dzd)[0]

    dy_pre, = _matmul("s5_dz", 'nt', dgl16, (tm, s5w), lambda i, j, k: (i, 0),
                      wfull['s5_glu_w'], (tns, s5w), lambda i, j, k: (j, 0),
                      (t_rows // tm, s5w // tns, 1), [((t_rows, s5w), F32, (tm, tns), lambda i, j: (i, j))],
                      dgelu, extras=[(dz_dir, (tm, tns), lambda i, j: (i, j)), (y_pre, (tm, tns), lambda i, j: (i, j))])
    gl_rows = s5w // N_CHIPS
    tng = _tile(gl_rows, 1024)
    npg = gl_rows // tng
    dw_glu, = _matmul("dw_glu", 'tn', z16, (t_rows, tng), lambda i, j, k: (0, i),
                      dgl16, (t_rows, tns), lambda i, j, k: (0, j),
                      (s5w // tng, s5w // tns, 1),
                      [((N_CHIPS, gl_rows, s5w), BF16, (None, tng, tns), lambda i, j: (i // npg, i % npg, j))],
                      lambda acc: acc)
    pair_glu = _pair('s5_glu_w', dw_glu, pos)
    du_s5, dbblk, dcblk, dlamblk, small['s5_d'] = _s5_bwd(
        "s5_bwd", dy_pre, proj, col0, states, bblk, cblk, lamblk, s5_d, s5w, tr)

    def from_bblk(m):
        return _blockdiag_take(m, S5_STATE, S5_GROUP).reshape(groups, S5_STATE * S5_GROUP)

    dbbr, dbbi = from_bblk(dbblk[:, :hw]), from_bblk(dbblk[:, hw:])
    dcs = lambda m: _blockdiag_take(m, S5_STATE, S5_GROUP).transpose(0, 1, 3, 2).reshape(1, groups, S5_GROUP, S5_STATE)
    small['s5_c_re'], small['s5_c_im'] = dcs(dcblk[:, :hw]), -dcs(dcblk[:, hw:])
    dlbr, dlbi = dlamblk[:, 0, :hw].reshape(groups, S5_STATE), dlamblk[:, 0, hw:].reshape(groups, S5_STATE)
    g_lre, g_lim, g_ldt, g_bre, g_bim = _s5_disc(
        "s5_disc_bwd", s5_lambda_re[0], s5_lambda_im[0], log_dt2, b_re2, b_im2, expand, cots=(dlbr, dlbi, dbbr, dbbi))
    small['s5_lambda_re'], small['s5_lambda_im'], small['s5_log_dt'] = g_lre[None], g_lim[None], g_ldt.reshape(1, groups)
    small['s5_b_re'] = g_bre.reshape(s5_b_re.shape)
    small['s5_b_im'] = g_bim.reshape(s5_b_im.shape)

    dh_rg = _linear_scan("rg_scan_bwd", a_dec, dh_out, tr, reverse=True, shift_a=True)

    hb = tr // 8

    def gates_bwd(xcv, dhv, hv, hprev, wa, wx, ba, bx, lam):
        i = pl.program_id(1)
        row = lax.broadcasted_iota(jnp.int32, (tr, hd), 0)
        first = jnp.where(i > 0, hprev[7:8, :], 0.0)
        h_prev = jnp.where(row == 0, jnp.broadcast_to(first, (tr, hd)), pltpu.roll(hv, 1, 0))
        x16 = xcv.astype(BF16)
        pre_r = jnp.dot(x16, wa, preferred_element_type=F32) + ba
        pre_i = jnp.dot(x16, wx, preferred_element_type=F32) + bx
        _, vjp = jax.vjp(_rg_gate_math, xcv, pre_r, pre_i, _softplus_neg(lam))
        dxc_, dpr, dpi, dsp = vjp((dhv * h_prev, dhv))
        dpr16, dpi16 = dpr.astype(BF16), dpi.astype(BF16)
        dxc_ = (dxc_ + lax.dot_general(dpr16, wa, _DN['nt'], preferred_element_type=F32)
                + lax.dot_general(dpi16, wx, _DN['nt'], preferred_element_type=F32))
        dwa = lax.dot_general(x16, dpr16, _DN['tn'], preferred_element_type=F32)
        dwx = lax.dot_general(x16, dpi16, _DN['tn'], preferred_element_type=F32)
        dlam = dsp * (-_sigmoid(-lam))
        return (dxc_, dwa, dwx, jnp.sum(dpr, axis=0, keepdims=True), jnp.sum(dpi, axis=0, keepdims=True), dlam)

    head_acc_w = ((RG_HEADS, hd, hd), F32, (None, hd, hd), lambda j, i: (j, 0, 0))
    head_acc_v = ((1, r), F32, (1, hd), lambda j, i: (0, j))
    h_halo = (h_rg, (8, hd), lambda j, i: (jnp.maximum(i * hb - 1, 0), j))
    dxc, d_wa, d_wx, d_ba, d_bx, d_lam = _rowmap(
        "rg_gates_bwd", gates_bwd, (RG_HEADS, t_rows // tr),
        [head_cols(xc), head_cols(dh_rg), head_cols(h_rg), h_halo] + gate_params,
        [head_out], [head_acc_w, head_acc_w, head_acc_v, head_acc_v, head_acc_v])
    small['rg_w_a'], small['rg_w_x'] = d_wa[None], d_wx[None]
    small['rg_b_a'], small['rg_b_x'] = d_ba.reshape(1, RG_HEADS, hd), d_bx.reshape(1, RG_HEADS, hd)
    small['rg_lambda'] = d_lam
    du_rg, d_convw, small['rg_conv_b'] = _conv_bwd("rg_conv_bwd", dxc, proj, conv_w2, r, tr)
    small['rg_conv_w'] = d_convw[None]

    dproj16, = _rowmap("dproj", lambda a, b, c: jnp.concatenate([a, b, c], axis=1).astype(BF16), (1, t_rows // trn),
                       [_rows(du_rg, trn), _rows(dgate_rg, trn), _rows(du_s5, trn)], [_o((t_rows, 3 * r), BF16, trn)])
    dn2, land_w_out = _matmul(
        "d_in_proj", 'nt', dproj16, (tm, pw), lambda i, j, k: (i, k),
        wfull['w_in'], (None, tnd, pw), lambda i, j, k: (k, j, 0),
        (t_rows // tm, d // tnd, N_CHIPS), [((t_rows, d), F32, (tm, tnd), lambda i, j: (i, j))],
        lambda acc: acc, comms=[_scatter_comm(pair_w_out)])
    parts['w_out'] = (pair_w_out, land_w_out)
    tq = _tile(pw, 768)
    nq = pw // tq
    dw_in, = _matmul("dw_in", 'tn', n2, (t_rows, tnd), lambda i, j, k: (0, i),
                     dproj16, (t_rows, tq), lambda i, j, k: (0, j),
                     (d // tnd, N_CHIPS * nq, 1),
                     [((N_CHIPS, d, pw), BF16, (None, tnd, tq), lambda i, j: (j // nq, i, j % nq))],
                     lambda acc: acc)
    pair_w_in = _pair('w_in', dw_in, pos)

    def norm_bwd(xv, dnv, dhv, g):
        _, vjp = jax.vjp(_rms, xv, g)
        dx, dg = vjp(dnv)
        res = dhv + dx
        return res, res.astype(BF16), dg

    dh1, dh1_16, small['mix_norm'] = _rowmap(
        "mix_dnorm", norm_bwd, (1, t_rows // trn), [_rows(h1, trn), _rows(dn2, trn), _rows(dh2, trn), _full(mix_norm)],
        [_o((t_rows, d), F32, trn), _o((t_rows, d), BF16, trn)], [_acc((1, d))])

    dh0, _, small['ffn1_norm'], parts1, got = _ffn_bwd(
        "ffn1b", ('ffn1_w_gate', 'ffn1_w_up', 'ffn1_w_down'), dh1, dh1_16, sv1, ffn1_norm, wfull['ffn1_w_gate'],
        dims, pos, host={'dact': [_scatter_comm(pair_glu), _scatter_comm(pair_w_in)]})
    parts.update(parts1)
    parts['s5_glu_w'] = (pair_glu, got['dact'][0])
    parts['w_in'] = (pair_w_in, got['dact'][1])
    grad_x = dh0[N_META:N_META + seq][None]
    small['meta_tokens'] = dh0[:N_META]
    small['final_norm'] = d_final.reshape(d)

    gbig, prev, half = {}, None, None
    for nm in BIG:
        res = _reduce_partials(f"red_{nm}", parts[nm][0], parts[nm][1], pos, _exchange_comm(half) if prev else None)
        if prev:
            gbig[prev] = res[1]
        prev, half = nm, res[0]
    gbig[prev] = _run_comm(f"exchange_{prev}", _exchange_comm(half))[0]
    gbig = {nm: g2.reshape(-1, g2.shape[2]) for nm, g2 in gbig.items()}

    small_names = [n for n in WEIGHTS if n not in BIG]
    full_shape = {n: small[n].shape for n in small_names}
    flat = jnp.concatenate([small[n].reshape(-1) for n in small_names])
    unit = 8 * 8 * LANES
    total = _round_up(flat.shape[0], unit)
    flat = jnp.concatenate([flat, jnp.zeros((total - flat.shape[0],), F32)])
    red = _all_reduce_small("ar_small", flat.reshape(total // LANES, LANES)).reshape(-1)
    gsmall = {}
    off = 0
    for n in small_names:
        size = math.prod(full_shape[n])
        gsmall[n] = _own_shard(n, red[off:off + size].reshape(full_shape[n]), w[n].shape, chip)
        off += size

    out_g, out_d, out_m, out_v = {}, {}, {}, {}
    for nm in BIG:
        w2 = local2d(w, nm)
        trw = _tile(w2.shape[0], 128 if w2.shape[1] < 4096 else 64, 8)
        res = _adamw(f"adamw_{nm}", w2, gbig[nm], local2d(mom, nm), local2d(var, nm), trw)
        back = (lambda a: jnp.swapaxes(a, 0, 1)[None]) if nm in transposed else (lambda a: a[None])
        out_g[nm], out_d[nm], out_m[nm], out_v[nm] = [back(a) for a in res]

    def pack(tree):
        fl = jnp.concatenate([tree[n].reshape(-1) for n in small_names])
        tot = _round_up(fl.shape[0], 8 * LANES)
        return jnp.concatenate([fl, jnp.zeros((tot - fl.shape[0],), F32)]).reshape(tot // LANES, LANES)

    wp, gp, mp, vp = pack(w), pack(gsmall), pack(mom), pack(var)
    res = _adamw("adamw_small", wp, gp, mp, vp, _tile(wp.shape[0], 512, 8))
    off = 0
    for n in small_names:
        size = math.prod(w[n].shape)
        for dst, src in zip((out_g, out_d, out_m, out_v), res):
            dst[n] = src.reshape(-1)[off:off + size].reshape(w[n].shape)
        off += size

    return (loss, grad_x, *[out_g[n] for n in WEIGHTS], *[out_d[n] for n in WEIGHTS],
            *[out_m[n] for n in WEIGHTS], *[out_v[n] for n in WEIGHTS])


def _gather_small_shards(conv_w, b_a, b_x, gather):
    cw = conv_w.shape[1]
    part = b_a.shape[1]

    def rows8(a):
        a = jnp.concatenate([a, jnp.zeros((a.shape[0], cw - a.shape[1]), F32)], axis=1)
        return jnp.concatenate([a, jnp.zeros((8 - a.shape[0], cw), F32)], axis=0) if a.shape[0] < 8 else a

    local = jnp.concatenate([rows8(conv_w), rows8(b_a), rows8(b_x), jnp.zeros((8, cw), F32)], axis=0)
    full = gather("rg_small", local)
    conv_full = full[:, :CONV_WIDTH].transpose(1, 0, 2).reshape(CONV_WIDTH, N_CHIPS * cw)
    bias = lambda k: full[:, 8 * k:8 * k + RG_HEADS, :part].transpose(1, 0, 2).reshape(1, RG_HEADS * N_CHIPS * part)
    return conv_full, bias(1), bias(2)


def _own_shard(name, g, local_shape, chip):
    if tuple(g.shape) == tuple(local_shape):
        return g
    axis = [k for k, (a, b) in enumerate(zip(g.shape, local_shape)) if a != b][0]
    size = local_shape[axis]
    return lax.dynamic_slice_in_dim(g, chip * size, size, axis=axis)
```

```python
import functools
import math

import jax
import jax.numpy as jnp
from jax import lax
from jax.experimental import pallas as pl
from jax.experimental.pallas import tpu as pltpu

F32 = jnp.float32
BF16 = jnp.bfloat16
MESH = pl.DeviceIdType.MESH

N_META = 16
RG_HEADS = 8
CONV_WIDTH = 4
RG_C = 8.0
S5_GROUP = 16
S5_STATE = 64
GROUPS_PER_BLOCK = 8
EPS = 1e-6
N_CHIPS = 4
LANES = 128
VMEM_LIMIT = 56 * 1024 * 1024

ADAM_LR = 0.001
ADAM_B1 = 0.9
ADAM_B2 = 0.999
ADAM_EPS = 1e-08
ADAM_WD = 0.01
ADAM_STEP = 10

WEIGHTS = ['meta_tokens', 'ffn1_norm', 'ffn1_w_gate', 'ffn1_w_up', 'ffn1_w_down', 'mix_norm', 'w_in', 'rg_conv_w',
           'rg_conv_b', 'rg_w_a', 'rg_b_a', 'rg_w_x', 'rg_b_x', 'rg_lambda', 's5_lambda_re', 's5_lambda_im',
           's5_log_dt', 's5_b_re', 's5_b_im', 's5_c_re', 's5_c_im', 's5_d', 's5_glu_w', 's5_glu_b', 'rg_out_norm',
           's5_out_norm', 'w_out', 'ffn2_norm', 'ffn2_w_gate', 'ffn2_w_up', 'ffn2_w_down', 'final_norm']
BIG = ('ffn1_w_gate', 'ffn1_w_up', 'ffn1_w_down', 'w_in', 's5_glu_w', 'w_out', 'ffn2_w_gate', 'ffn2_w_up',
       'ffn2_w_down')

_DN = {'nn': (((1,), (0,)), ((), ())), 'nt': (((1,), (1,)), ((), ())), 'tn': (((0,), (0,)), ((), ()))}


def _round_up(n, m):
    return (n + m - 1) // m * m


def _tile(n, pref, unit=LANES):
    best = None
    for t in range(unit, min(n, pref) + 1, unit):
        if n % t == 0:
            best = t
    return best if best is not None else n


def _params(sem=None):
    return pltpu.CompilerParams(dimension_semantics=sem, vmem_limit_bytes=VMEM_LIMIT)


def _rms(x, g):
    return x * lax.rsqrt(jnp.mean(x * x, axis=-1, keepdims=True) + EPS) * g


def _sigmoid(x):
    return 0.5 * (jnp.tanh(0.5 * x) + 1.0)


def _gelu(x):
    return 0.5 * x * (1.0 + jnp.tanh(math.sqrt(2.0 / math.pi) * (x + 0.044715 * (x * x * x))))


def _silu(x):
    return x * _sigmoid(x)


def _expm1(x):
    series = x * (1.0 + x * (1.0 / 2) * (1.0 + x * (1.0 / 3) * (1.0 + x * (1.0 / 4) * (1.0 + x * (1.0 / 5) * (1.0 + x * (1.0 / 6))))))
    return jnp.where(jnp.abs(x) < 0.3, series, jnp.exp(x) - 1.0)


def _softplus_neg(lam):
    m = jnp.maximum(-lam, 0.0)
    e = jnp.exp(-jnp.abs(lam))
    w = 1.0 + e
    log1p = jnp.where(w == 1.0, e, jnp.log(w) * (e / jnp.where(w == 1.0, 1.0, w - 1.0)))
    return m + log1p


def _rg_gate_math(xc, pre_r, pre_i, sp):
    r = 1.0 / (1.0 + jnp.exp(-pre_r))
    i = 1.0 / (1.0 + jnp.exp(-pre_i))
    log_a = -RG_C * r * sp
    a = jnp.exp(log_a)
    mult = jnp.sqrt(-_expm1(2.0 * log_a))
    return a, mult * i * xc


class _Comm:
    def __init__(self, arrays, out_shapes, aliased, sems, start, finish, mid=None):
        self.arrays, self.out_shapes, self.aliased, self.sems = arrays, out_shapes, aliased, sems
        self.start, self.finish, self.mid = start, finish, mid


def _matmul(name, mode, a, a_blk, a_map, b, b_blk, b_map, grid, outs, epilogue, extras=(), comms=()):
    ni, nj, nk = grid
    ne, no = len(extras), len(outs)
    sq = lambda blk: tuple(d for d in blk if d is not None)
    ab, bb = sq(a_blk), sq(b_blk)
    acc_shape = {'nn': (ab[0], bb[1]), 'nt': (ab[0], bb[0]), 'tn': (ab[1], bb[1])}[mode]
    n_cin = sum(len(cm.arrays) for cm in comms)
    n_cout = sum(len(cm.out_shapes) for cm in comms)
    n_acc = 1 if nk > 1 else 0

    def comm_refs(refs):
        cin = refs[2 + ne:2 + ne + n_cin]
        cout = refs[2 + ne + n_cin + no:2 + ne + n_cin + no + n_cout]
        csem = refs[2 + ne + n_cin + no + n_cout + n_acc:]
        for cm in comms:
            yield cm, cin[:len(cm.arrays)], cout[:len(cm.out_shapes)], csem[:len(cm.sems)]
            cin, cout, csem = cin[len(cm.arrays):], cout[len(cm.out_shapes):], csem[len(cm.sems):]

    def body(*refs):
        a_ref, b_ref = refs[0], refs[1]
        ex = refs[2:2 + ne]
        out = refs[2 + ne + n_cin:2 + ne + n_cin + no]
        if comms:
            @pl.when((pl.program_id(0) == 0) & (pl.program_id(1) == 0) & (pl.program_id(2) == 0))
            def _():
                for cm, cin, cout, csem in comm_refs(refs):
                    cm.start(cin, cout, csem)

            if any(cm.mid for cm in comms):
                step = (pl.program_id(0) * nj + pl.program_id(1)) * nk + pl.program_id(2)

                @pl.when(step == (3 * ni * nj * nk) // 5)
                def _():
                    for cm, cin, cout, csem in comm_refs(refs):
                        if cm.mid:
                            cm.mid(cin, cout, csem)

        part = lax.dot_general(a_ref[...], b_ref[...], _DN[mode], preferred_element_type=F32)

        def finish(acc):
            res = epilogue(acc, *[e[...] for e in ex])
            if not isinstance(res, tuple):
                res = (res,)
            for o, r in zip(out, res):
                o[...] = r.astype(o.dtype)

        if nk == 1:
            finish(part)
        else:
            acc_ref = refs[2 + ne + n_cin + no + n_cout]
            k = pl.program_id(2)

            @pl.when(k == 0)
            def _():
                acc_ref[...] = part

            @pl.when(k > 0)
            def _():
                acc_ref[...] += part

            @pl.when(k == nk - 1)
            def _():
                finish(acc_ref[...])

        if comms:
            @pl.when((pl.program_id(0) == ni - 1) & (pl.program_id(1) == nj - 1) & (pl.program_id(2) == nk - 1))
            def _():
                for cm, cin, cout, csem in comm_refs(refs):
                    cm.finish(cin, cout, csem)

    hbm = pl.BlockSpec(memory_space=pltpu.HBM)
    in_specs = [pl.BlockSpec(a_blk, a_map), pl.BlockSpec(b_blk, b_map)]
    in_specs += [pl.BlockSpec(blk, functools.partial(lambda m, i, j, k: m(i, j), m)) for _, blk, m in extras]
    in_specs += [hbm] * n_cin
    out_specs = [pl.BlockSpec(blk, functools.partial(lambda m, i, j, k: m(i, j), m)) for _, _, blk, m in outs]
    out_specs += [hbm] * n_cout
    aliases, cin_at, cout_at = {}, 2 + ne, no
    for cm in comms:
        if cm.aliased:
            aliases.update({cin_at + k: cout_at + k for k in range(len(cm.arrays))})
        cin_at, cout_at = cin_at + len(cm.arrays), cout_at + len(cm.out_shapes)
    res = pl.pallas_call(
        body, name=name, grid=grid, in_specs=in_specs, out_specs=out_specs,
        out_shape=[jax.ShapeDtypeStruct(s, d) for s, d, _, _ in outs] + [s for cm in comms for s in cm.out_shapes],
        scratch_shapes=([pltpu.VMEM(acc_shape, F32)] if nk > 1 else [])
        + [pltpu.SemaphoreType.DMA((n,)) for cm in comms for n in cm.sems],
        input_output_aliases=aliases,
        compiler_params=_params(("arbitrary",) * 3 if comms else ("parallel", "parallel", "arbitrary")),
    )(a, b, *[e for e, _, _ in extras], *[arr for cm in comms for arr in cm.arrays])
    return res


def _rows(arr, tr, tc=None, col0=0):
    if tc is None:
        return (arr, (tr, arr.shape[1]), lambda j, i: (i, 0))
    return (arr, (tr, tc), lambda j, i: (i, col0 + j))


def _full(arr):
    nd = arr.ndim
    return (arr, arr.shape, lambda j, i: (0,) * nd)


def _cols(arr, tc):
    return (arr, (arr.shape[0], tc), lambda j, i: (0, j))


def _rowmap(name, fn, grid, ins, outs, accs=(), scratch=()):
    nj, ni = grid
    n_in, n_out, n_acc = len(ins), len(outs), len(accs)

    def body(*refs):
        vals = [r[...] for r in refs[:n_in]]
        o_refs = refs[n_in:n_in + n_out]
        a_refs = refs[n_in + n_out:n_in + n_out + n_acc]
        s_refs = refs[n_in + n_out + n_acc:]
        res = fn(*vals, *s_refs)
        if not isinstance(res, tuple):
            res = (res,)
        for o, r in zip(o_refs, res[:n_out]):
            o[...] = r.astype(o.dtype)
        i = pl.program_id(1)
        for a_ref, r in zip(a_refs, res[n_out:]):
            @pl.when(i == 0)
            def _(a_ref=a_ref, r=r):
                a_ref[...] = r.astype(a_ref.dtype)

            @pl.when(i > 0)
            def _(a_ref=a_ref, r=r):
                a_ref[...] += r.astype(a_ref.dtype)

    res = pl.pallas_call(
        body, name=name, grid=grid,
        in_specs=[pl.BlockSpec(blk, m) for _, blk, m in ins],
        out_specs=[pl.BlockSpec(blk, m) for _, _, blk, m in list(outs) + list(accs)],
        out_shape=[jax.ShapeDtypeStruct(s, d) for s, d, _, _ in list(outs) + list(accs)],
        scratch_shapes=list(scratch),
        compiler_params=_params(("parallel", "arbitrary")),
    )(*[a for a, _, _ in ins])
    return res


def _o(shape, dtype, tr, tc=None):
    if tc is None:
        return (shape, dtype, (tr, shape[1]), lambda j, i: (i, 0))
    return (shape, dtype, (tr, tc), lambda j, i: (i, j))


def _acc(shape, tc=None):
    if tc is None:
        nd = len(shape)
        return (shape, F32, shape, lambda j, i: (0,) * nd)
    return (shape, F32, (shape[0], tc), lambda j, i: (0, j))


def _position():
    x, y, c = lax.axis_index("x"), lax.axis_index("y"), lax.axis_index("c")
    return x, y, c


def _relay_gather_comm(buf):
    _, rows, cols = buf.shape
    rh = rows // 2
    rq = rh // 2

    def plan(out_ref, sems):
        x, y, c = _position()
        own, cx, cy, cd = 2 * x + y, 2 * (1 - x) + y, 2 * x + (1 - y), 2 * (1 - x) + (1 - y)
        to_x, to_y, sibling = (1 - x, y, c), (x, 1 - y, c), (x, y, 1 - c)
        half, other = pl.ds(c * rh, rh), pl.ds((1 - c) * rh, rh)
        q0, q1 = pl.ds(c * rh, rq), pl.ds(c * rh + rq, rq)

        def copy(pair, k, chip_index, rows_, to):
            return pltpu.make_async_remote_copy(
                src_ref=out_ref.at[chip_index, rows_], dst_ref=out_ref.at[chip_index, rows_],
                send_sem=sems[pair].at[k], recv_sem=sems[pair + 1].at[k], device_id=to, device_id_type=MESH)

        return dict(
            own_x=copy(0, 0, own, half, to_x), own_y=copy(0, 1, own, half, to_y),
            in_x=copy(0, 0, cx, half, to_x), in_y=copy(0, 1, cy, half, to_y),
            fwd_xy=copy(0, 2, cx, q0, to_y), fwd_yx=copy(0, 3, cy, q1, to_x),
            in_d0=copy(0, 2, cd, q0, to_y), in_d1=copy(0, 3, cd, q1, to_x),
            d2d_out=[copy(2, j, ch, half, sibling) for j, ch in enumerate((cx, cy, cd))],
            d2d_in=[copy(2, j, ch, other, sibling) for j, ch in enumerate((cx, cy, cd))])

    def start(ins, outs, sems):
        p = plan(outs[0], sems)
        p['own_x'].start()
        p['own_y'].start()

    def mid(ins, outs, sems):
        p = plan(outs[0], sems)
        p['in_x'].wait_recv()
        p['fwd_xy'].start()
        p['d2d_out'][0].start()
        p['in_y'].wait_recv()
        p['fwd_yx'].start()
        p['d2d_out'][1].start()

    def finish(ins, outs, sems):
        p = plan(outs[0], sems)
        p['in_d0'].wait_recv()
        p['in_d1'].wait_recv()
        p['d2d_out'][2].start()
        for cp in p['d2d_in']:
            cp.wait_recv()
        for cp in [p['own_x'], p['own_y'], p['fwd_xy'], p['fwd_yx']] + p['d2d_out']:
            cp.wait_send()

    return _Comm([buf], [jax.ShapeDtypeStruct(buf.shape, buf.dtype)], True, [4, 4, 3, 3], start, finish, mid)


def _gather_comm(buf):
    _, rows, cols = buf.shape
    rh = rows // 2
    if buf.dtype == BF16 and rh % 32 == 0:
        return _relay_gather_comm(buf)

    def plan(out_ref, sems):
        x, y, c = _position()
        chips = [(1 - x, y), (x, 1 - y), (1 - x, 1 - y)]

        def copy(pair, j, chip_index, half, to):
            return pltpu.make_async_remote_copy(
                src_ref=out_ref.at[chip_index, half], dst_ref=out_ref.at[chip_index, half],
                send_sem=sems[pair].at[j], recv_sem=sems[pair + 1].at[j], device_id=to, device_id_type=MESH)

        mine, other = pl.ds(c * rh, rh), pl.ds((1 - c) * rh, rh)
        ici_out = [copy(0, j, 2 * x + y, mine, (px, py, c)) for j, (px, py) in enumerate(chips)]
        ici_in = [copy(0, j, 2 * px + py, mine, (px, py, c)) for j, (px, py) in enumerate(chips)]
        d2d_out = [copy(2, j, 2 * px + py, mine, (x, y, 1 - c)) for j, (px, py) in enumerate(chips)]
        d2d_in = [copy(2, j, 2 * px + py, other, (x, y, 1 - c)) for j, (px, py) in enumerate(chips)]
        return ici_out, ici_in, d2d_out, d2d_in

    def start(ins, outs, sems):
        for cp in plan(outs[0], sems)[0]:
            cp.start()

    def finish(ins, outs, sems):
        ici_out, ici_in, d2d_out, d2d_in = plan(outs[0], sems)
        for arrived, fwd in zip(ici_in, d2d_out):
            arrived.wait_recv()
            fwd.start()
        for cp in d2d_in:
            cp.wait_recv()
        for cp in ici_out + d2d_out:
            cp.wait_send()

    return _Comm([buf], [jax.ShapeDtypeStruct(buf.shape, buf.dtype)], True, [3, 3, 3, 3], start, finish)


def _scatter_comm(p):
    _, rh, cols = p.shape

    def plan(p_ref, land_ref, sems):
        x, y, c = _position()
        chips = [(1 - x, y), (x, 1 - y), (1 - x, 1 - y)]

        def copy(j, chip_index, to):
            return pltpu.make_async_remote_copy(
                src_ref=p_ref.at[chip_index], dst_ref=land_ref.at[j],
                send_sem=sems[0].at[j], recv_sem=sems[1].at[j], device_id=to, device_id_type=MESH)

        out = [copy(j, 2 * px + py, (px, py, c)) for j, (px, py) in enumerate(chips)]
        arrive = [copy(j, 2 * x + y, (x, y, c)) for j in range(3)]
        return out, arrive

    def start(ins, outs, sems):
        for cp in plan(ins[0], outs[0], sems)[0]:
            cp.start()

    def finish(ins, outs, sems):
        out, arrive = plan(ins[0], outs[0], sems)
        for cp in arrive:
            cp.wait_recv()
        for cp in out:
            cp.wait_send()

    return _Comm([p], [jax.ShapeDtypeStruct((3, rh, cols), p.dtype)], False, [3, 3], start, finish)


def _run_comm(name, cm):
    n_in, n_out = len(cm.arrays), len(cm.out_shapes)

    def body(*refs):
        ins, outs, sems = refs[:n_in], refs[n_in:n_in + n_out], refs[n_in + n_out:]
        cm.start(ins, outs, sems)
        if cm.mid:
            cm.mid(ins, outs, sems)
        cm.finish(ins, outs, sems)

    hbm = pl.BlockSpec(memory_space=pltpu.HBM)
    return pl.pallas_call(
        body, name=name, in_specs=[hbm] * n_in, out_specs=[hbm] * n_out, out_shape=list(cm.out_shapes),
        input_output_aliases={k: k for k in range(n_in)} if cm.aliased else {},
        scratch_shapes=[pltpu.SemaphoreType.DMA((n,)) for n in cm.sems],
    )(*cm.arrays)


def _swap_comm(g):
    n, rows, cols = g.shape
    rh = rows // 2

    def plan(g_ref, land_ref, sems):
        x, y, c = _position()

        def copy(k, half):
            return pltpu.make_async_remote_copy(
                src_ref=g_ref.at[k, pl.ds(half * rh, rh)], dst_ref=land_ref.at[k],
                send_sem=sems[0].at[k], recv_sem=sems[1].at[k], device_id=(x, y, 1 - c), device_id_type=MESH)

        return [copy(k, 1 - c) for k in range(n)], [copy(k, c) for k in range(n)]

    def start(ins, outs, sems):
        for cp in plan(ins[0], outs[0], sems)[0]:
            cp.start()

    def finish(ins, outs, sems):
        out, arrive = plan(ins[0], outs[0], sems)
        for cp in arrive:
            cp.wait_recv()
        for cp in out:
            cp.wait_send()

    return _Comm([g], [jax.ShapeDtypeStruct((n, rh, cols), g.dtype)], False, [n, n], start, finish)


def _pair_sum(name, g, land, pos):
    n, rows, cols = g.shape
    rh = rows // 2
    tr = _tile(rh, 256, 16)
    nt = rh // tr

    def body(pos_ref, g_ref, l_ref, o_ref):
        o_ref[...] = (g_ref[...].astype(F32) + l_ref[...].astype(F32)).astype(o_ref.dtype)

    grid_spec = pltpu.PrefetchScalarGridSpec(
        num_scalar_prefetch=1, grid=(n, nt),
        in_specs=[pl.BlockSpec((None, tr, cols), lambda k, i, p: (k, p[1] * nt + i, 0)),
                  pl.BlockSpec((None, tr, cols), lambda k, i, p: (k, i, 0))],
        out_specs=pl.BlockSpec((None, tr, cols), lambda k, i, p: (k, i, 0)))
    return pl.pallas_call(
        body, name=name, grid_spec=grid_spec, out_shape=jax.ShapeDtypeStruct((n, rh, cols), g.dtype),
        compiler_params=_params(("parallel", "parallel")),
    )(pos, g, land)


def _reduce_partials(name, p, land, pos, comm=None):
    _, rh, cols = p.shape
    tr = _tile(rh, 256, 16)
    nt = rh // tr
    n_cin = len(comm.arrays) if comm else 0
    n_cout = len(comm.out_shapes) if comm else 0

    def body(*refs):
        p_ref, l_ref, o_ref = refs[1], refs[2], refs[3 + n_cin]
        comm_args = (refs[3:3 + n_cin], refs[4 + n_cin:4 + n_cin + n_cout], refs[4 + n_cin + n_cout:])
        if comm:
            @pl.when(pl.program_id(0) == 0)
            def _():
                comm.start(*comm_args)

        acc = p_ref[...].astype(F32)
        for k in range(3):
            acc = acc + l_ref[k].astype(F32)
        o_ref[...] = acc
        if comm:
            @pl.when(pl.program_id(0) == nt - 1)
            def _():
                comm.finish(*comm_args)

    hbm = pl.BlockSpec(memory_space=pltpu.HBM)
    grid_spec = pltpu.PrefetchScalarGridSpec(
        num_scalar_prefetch=1, grid=(nt,),
        in_specs=[pl.BlockSpec((None, tr, cols), lambda i, p_: (p_[0], i, 0)),
                  pl.BlockSpec((3, tr, cols), lambda i, p_: (0, i, 0))] + [hbm] * n_cin,
        out_specs=[pl.BlockSpec((None, tr, cols), lambda i, p_: (p_[1], i, 0))] + [hbm] * n_cout,
        scratch_shapes=[pltpu.SemaphoreType.DMA((n,)) for n in comm.sems] if comm else [])
    return pl.pallas_call(
        body, name=name, grid_spec=grid_spec,
        out_shape=[jax.ShapeDtypeStruct((2, rh, cols), F32)] + (list(comm.out_shapes) if comm else []),
        input_output_aliases={3 + k: 1 + k for k in range(n_cin)} if comm and comm.aliased else {},
        compiler_params=_params(("arbitrary",) if comm else ("parallel",)),
    )(pos, p, land, *(comm.arrays if comm else []))


def _exchange_comm(buf):
    def plan(out_ref, sems):
        x, y, c = _position()

        def copy(half):
            return pltpu.make_async_remote_copy(
                src_ref=out_ref.at[half], dst_ref=out_ref.at[half], send_sem=sems[0].at[0], recv_sem=sems[1].at[0],
                device_id=(x, y, 1 - c), device_id_type=MESH)

        return copy(c), copy(1 - c)

    def start(ins, outs, sems):
        plan(outs[0], sems)[0].start()

    def finish(ins, outs, sems):
        out, arrive = plan(outs[0], sems)
        arrive.wait_recv()
        out.wait_send()

    return _Comm([buf], [jax.ShapeDtypeStruct(buf.shape, buf.dtype)], True, [1, 1], start, finish)


def _all_reduce_small(name, buf):
    rows = buf.shape[0]
    p = rows // 8

    def body(buf_ref, out_ref, land_ref, red_ref, s1, r1, s2, r2):
        x, y, c = _position()
        me = 4 * x + 2 * y + c

        def peer(r):
            px = 1 - x if (r >> 2) & 1 else x
            py = 1 - y if (r >> 1) & 1 else y
            pc = 1 - c if r & 1 else c
            return (px, py, pc), 4 * px + 2 * py + pc

        firsts = []
        for r in range(1, 8):
            to, d = peer(r)
            cp = pltpu.make_async_remote_copy(
                src_ref=buf_ref.at[pl.ds(d * p, p)], dst_ref=land_ref.at[r - 1],
                send_sem=s1.at[r - 1], recv_sem=r1.at[r - 1], device_id=to, device_id_type=MESH)
            cp.start()
            firsts.append(cp)
        acc = buf_ref[pl.ds(me * p, p), :]
        for r in range(1, 8):
            firsts[r - 1].wait_recv()
            acc = acc + land_ref[r - 1]
        red_ref[...] = acc
        out_ref[pl.ds(me * p, p), :] = acc
        seconds = []
        for r in range(1, 8):
            to, d = peer(r)
            cp = pltpu.make_async_remote_copy(
                src_ref=red_ref, dst_ref=out_ref.at[pl.ds(me * p, p)],
                send_sem=s2.at[r - 1], recv_sem=r2.at[r - 1], device_id=to, device_id_type=MESH)
            cp.start()
            seconds.append(cp)
        for r in range(1, 8):
            to, d = peer(r)
            pltpu.make_async_remote_copy(
                src_ref=red_ref, dst_ref=out_ref.at[pl.ds(d * p, p)],
                send_sem=s2.at[r - 1], recv_sem=r2.at[r - 1], device_id=to, device_id_type=MESH).wait_recv()
        for cp in firsts + seconds:
            cp.wait_send()

    vmem = pl.BlockSpec(memory_space=pltpu.VMEM)
    return pl.pallas_call(
        body, name=name, in_specs=[vmem], out_specs=vmem,
        out_shape=jax.ShapeDtypeStruct(buf.shape, F32),
        scratch_shapes=[pltpu.VMEM((7, p, LANES), F32), pltpu.VMEM((p, LANES), F32),
                        pltpu.SemaphoreType.DMA((7,)), pltpu.SemaphoreType.DMA((7,)),
                        pltpu.SemaphoreType.DMA((7,)), pltpu.SemaphoreType.DMA((7,))],
        compiler_params=pltpu.CompilerParams(vmem_limit_bytes=VMEM_LIMIT),
    )(buf)


def _cast_pad(name, w, rows_to, cols_to, pos):
    rows, cols = w.shape
    tr = _tile(math.gcd(rows, rows_to), 256, 16)
    assert rows % tr == 0 and rows_to % tr == 0, (rows, rows_to, tr)
    n_src = rows // tr

    def body(pos_ref, w_ref, o_ref):
        i = pl.program_id(0)
        if cols_to > cols:
            o_ref[:, cols:] = jnp.zeros((tr, cols_to - cols), BF16)

        @pl.when(i < n_src)
        def _():
            o_ref[:, :cols] = w_ref[...].astype(BF16)

        if rows_to > rows:
            @pl.when(i >= n_src)
            def _():
                o_ref[:, :cols] = jnp.zeros((tr, cols), BF16)

    grid_spec = pltpu.PrefetchScalarGridSpec(
        num_scalar_prefetch=1, grid=(rows_to // tr,),
        in_specs=[pl.BlockSpec((tr, cols), lambda i, p: (jnp.minimum(i, n_src - 1), 0))],
        out_specs=pl.BlockSpec((None, tr, cols_to), lambda i, p: (p[0], i, 0)))
    return pl.pallas_call(
        body, name=name, grid_spec=grid_spec,
        out_shape=jax.ShapeDtypeStruct((N_CHIPS, rows_to, cols_to), BF16),
        compiler_params=_params(("parallel",)),
    )(pos, w)


def _scan_rows(a, b, row, tr, reverse):
    sh = 1
    while sh < tr:
        if reverse:
            valid = row < tr - sh
            a_s = pltpu.roll(a, tr - sh, 0)
            b_s = pltpu.roll(b, tr - sh, 0)
        else:
            valid = row >= sh
            a_s = pltpu.roll(a, sh, 0)
            b_s = pltpu.roll(b, sh, 0)
        b = b + jnp.where(valid, a * b_s, 0.0)
        a = jnp.where(valid, a * a_s, a)
        sh *= 2
    return a, b


def _linear_scan(name, a, b, tr, reverse=False, shift_a=False):
    t_rows, cols = a.shape
    tc = _tile(cols, 512)
    ni, nj = t_rows // tr, cols // tc
    hb = tr // 8

    def rmap(j, i):
        return ((ni - 1 - i) if reverse else i, j)

    def halo_map(j, i):
        ri = ni - 1 - i
        return (jnp.minimum((ri + 1) * hb, t_rows // 8 - 1), j)

    def body(*refs):
        if shift_a:
            a_ref, halo_ref, b_ref, h_ref, carry_ref = refs
        else:
            a_ref, b_ref, h_ref, carry_ref = refs
        i = pl.program_id(1)
        row = lax.broadcasted_iota(jnp.int32, (tr, tc), 0)

        @pl.when(i == 0)
        def _():
            carry_ref[...] = jnp.zeros_like(carry_ref)

        av = a_ref[...]
        if shift_a:
            nxt = jnp.where(i > 0, halo_ref[0:1, :], 0.0)
            av = jnp.where(row == tr - 1, jnp.broadcast_to(nxt, (tr, tc)), pltpu.roll(av, tr - 1, 0))
        pa, hb_ = _scan_rows(av, b_ref[...], row, tr, reverse)
        h = hb_ + pa * carry_ref[0:1, :]
        h_ref[...] = h
        last = h[0:1, :] if reverse else h[tr - 1:tr, :]
        carry_ref[...] = jnp.broadcast_to(last, carry_ref.shape)

    in_specs = [pl.BlockSpec((tr, tc), rmap)]
    args = [a]
    if shift_a:
        in_specs.append(pl.BlockSpec((8, tc), halo_map))
        args.append(a)
    in_specs.append(pl.BlockSpec((tr, tc), rmap))
    args.append(b)
    return pl.pallas_call(
        body, name=name, grid=(nj, ni), in_specs=in_specs, out_specs=pl.BlockSpec((tr, tc), rmap),
        out_shape=jax.ShapeDtypeStruct((t_rows, cols), F32), scratch_shapes=[pltpu.VMEM((8, tc), F32)],
        compiler_params=_params(("parallel", "arbitrary")),
    )(*args)


def _conv_fwd(name, proj, conv_w, conv_b, width, tr):
    t_rows = proj.shape[0]
    ni = t_rows // tr
    hb = tr // 8
    tc = _tile(width, 512)

    def body(u_ref, halo_ref, w_ref, b_ref, o_ref, ext_ref):
        i = pl.program_id(1)
        ext_ref[0:8, :] = jnp.where(i > 0, halo_ref[...], 0.0)
        ext_ref[8:8 + tr, :] = u_ref[...]
        acc = b_ref[...] + w_ref[CONV_WIDTH - 1:CONV_WIDTH, :] * u_ref[...]
        for k in range(CONV_WIDTH - 1):
            acc = acc + w_ref[k:k + 1, :] * ext_ref[pl.ds(8 - (CONV_WIDTH - 1) + k, tr), :]
        o_ref[...] = acc

    return pl.pallas_call(
        body, name=name, grid=(width // tc, ni),
        in_specs=[pl.BlockSpec((tr, tc), lambda j, i: (i, j)),
                  pl.BlockSpec((8, tc), lambda j, i: (jnp.maximum(i * hb - 1, 0), j)),
                  pl.BlockSpec((CONV_WIDTH, tc), lambda j, i: (0, j)),
                  pl.BlockSpec((1, tc), lambda j, i: (0, j))],
        out_specs=pl.BlockSpec((tr, tc), lambda j, i: (i, j)),
        out_shape=jax.ShapeDtypeStruct((t_rows, width), F32),
        scratch_shapes=[pltpu.VMEM((tr + 8, tc), F32)],
        compiler_params=_params(("parallel", "parallel")),
    )(proj, proj, conv_w, conv_b)


def _conv_bwd(name, dxc, proj, conv_w, width, tr):
    t_rows = dxc.shape[0]
    ni = t_rows // tr
    hb = tr // 8
    tc = _tile(width, 512)

    def body(d_ref, dnext_ref, u_ref, uprev_ref, w_ref, du_ref, dw_ref, db_ref, dext_ref, uext_ref):
        i = pl.program_id(1)
        dext_ref[0:tr, :] = d_ref[...]
        dext_ref[tr:tr + 8, :] = jnp.where(i < ni - 1, dnext_ref[...], 0.0)
        uext_ref[0:8, :] = jnp.where(i > 0, uprev_ref[...], 0.0)
        uext_ref[8:8 + tr, :] = u_ref[...]
        d = d_ref[...]
        du = w_ref[CONV_WIDTH - 1:CONV_WIDTH, :] * d
        dws = []
        for k in range(CONV_WIDTH - 1):
            du = du + w_ref[k:k + 1, :] * dext_ref[pl.ds(CONV_WIDTH - 1 - k, tr), :]
            dws.append(jnp.sum(d * uext_ref[pl.ds(8 - (CONV_WIDTH - 1) + k, tr), :], axis=0, keepdims=True))
        dws.append(jnp.sum(d * u_ref[...], axis=0, keepdims=True))
        du_ref[...] = du
        dw = jnp.concatenate(dws, axis=0)
        db = jnp.sum(d, axis=0, keepdims=True)

        @pl.when(i == 0)
        def _():
            dw_ref[...] = dw
            db_ref[...] = db

        @pl.when(i > 0)
        def _():
            dw_ref[...] += dw
            db_ref[...] += db

    return pl.pallas_call(
        body, name=name, grid=(width // tc, ni),
        in_specs=[pl.BlockSpec((tr, tc), lambda j, i: (i, j)),
                  pl.BlockSpec((8, tc), lambda j, i: (jnp.minimum((i + 1) * hb, t_rows // 8 - 1), j)),
                  pl.BlockSpec((tr, tc), lambda j, i: (i, j)),
                  pl.BlockSpec((8, tc), lambda j, i: (jnp.maximum(i * hb - 1, 0), j)),
                  pl.BlockSpec((CONV_WIDTH, tc), lambda j, i: (0, j))],
        out_specs=[pl.BlockSpec((tr, tc), lambda j, i: (i, j)),
                   pl.BlockSpec((CONV_WIDTH, tc), lambda j, i: (0, j)),
                   pl.BlockSpec((1, tc), lambda j, i: (0, j))],
        out_shape=[jax.ShapeDtypeStruct((t_rows, width), F32), jax.ShapeDtypeStruct((CONV_WIDTH, width), F32),
                   jax.ShapeDtypeStruct((1, width), F32)],
        scratch_shapes=[pltpu.VMEM((tr + 8, tc), F32), pltpu.VMEM((tr + 8, tc), F32)],
        compiler_params=_params(("parallel", "arbitrary")),
    )(dxc, dxc, proj, proj, conv_w)


def _s5_disc_math(lam_re, lam_im, log_dt, b_re, b_im, expand):
    dt = jnp.exp(log_dt)
    zr, zi = lam_re * dt, lam_im * dt
    mag = jnp.exp(zr)
    lbr, lbi = mag * jnp.cos(zi), mag * jnp.sin(zi)
    ar, ai = lbr - 1.0, lbi
    den = lam_re * lam_re + lam_im * lam_im
    cr = (ar * lam_re + ai * lam_im) / den
    ci = (ai * lam_re - ar * lam_im) / den
    cre = jnp.dot(cr, expand, precision=lax.Precision.HIGHEST, preferred_element_type=F32)
    cie = jnp.dot(ci, expand, precision=lax.Precision.HIGHEST, preferred_element_type=F32)
    return lbr, lbi, cre * b_re - cie * b_im, cre * b_im + cie * b_re


def _s5_disc(name, lam_re, lam_im, log_dt, b_re, b_im, expand, cots=None):
    ins = [lam_re, lam_im, log_dt, b_re, b_im, expand]
    n_in = len(ins) + (len(cots) if cots else 0)

    def body(*refs):
        vals = [r[...] for r in refs[:6]]
        outs = refs[n_in:]
        if cots is None:
            res = _s5_disc_math(*vals)
        else:
            cv = tuple(r[...] for r in refs[6:n_in])
            _, vjp = jax.vjp(lambda a, b, c, d, e: _s5_disc_math(a, b, c, d, e, vals[5]), *vals[:5])
            res = vjp(cv)
        for o, r in zip(outs, res):
            o[...] = r

    if cots is None:
        shapes = [lam_re.shape, lam_re.shape, b_re.shape, b_re.shape]
    else:
        shapes = [lam_re.shape, lam_re.shape, log_dt.shape, b_re.shape, b_re.shape]
    vmem = pl.BlockSpec(memory_space=pltpu.VMEM)
    return pl.pallas_call(
        body, name=name, in_specs=[vmem] * n_in, out_specs=[vmem] * len(shapes),
        out_shape=[jax.ShapeDtypeStruct(s, F32) for s in shapes],
        compiler_params=pltpu.CompilerParams(vmem_limit_bytes=VMEM_LIMIT),
    )(*ins, *(cots or ()))


def _cmul(ar, ai, br, bi):
    return ar * br - ai * bi, ar * bi + ai * br


SCAN_BLOCK = 8


def _block_scan(br, bi, lr, li, row, rows, reverse):
    pr, pi = lr, li
    pos = row & (SCAN_BLOCK - 1)
    sh = 1
    while sh < SCAN_BLOCK:
        if reverse:
            valid = pos < SCAN_BLOCK - sh
            rs, is_ = pltpu.roll(br, rows - sh, 0), pltpu.roll(bi, rows - sh, 0)
        else:
            valid = pos >= sh
            rs, is_ = pltpu.roll(br, sh, 0), pltpu.roll(bi, sh, 0)
        mr, mi = _cmul(pr, pi, rs, is_)
        br = br + jnp.where(valid, mr, 0.0)
        bi = bi + jnp.where(valid, mi, 0.0)
        pr, pi = _cmul(pr, pi, pr, pi)
        sh *= 2
    return br, bi


def _block_powers(lr, li, hw, reverse):
    row = lax.broadcasted_iota(jnp.int32, (SCAN_BLOCK, hw), 0)
    d = jnp.where(row == (SCAN_BLOCK - 1 if reverse else 0), 1.0, 0.0)
    return _block_scan(jnp.broadcast_to(lr, (SCAN_BLOCK, hw)) * d, jnp.broadcast_to(li, (SCAN_BLOCK, hw)) * d,
                       lr, li, row, SCAN_BLOCK, reverse)


def _chain_blocks(s_ref, pow_ref, carry, rows, hw, reverse):
    nblk = rows // SCAN_BLOCK
    pr, pi = pow_ref[:, :hw], pow_ref[:, hw:]

    def step(n, c):
        b = (nblk - 1 - n) if reverse else n
        at = pl.ds(pl.multiple_of(b * SCAN_BLOCK, SCAN_BLOCK), SCAN_BLOCK)
        blk = s_ref[at, :]
        kr, ki = _cmul(pr, pi, c[0], c[1])
        sr, si = blk[:, :hw] + kr, blk[:, hw:] + ki
        s_ref[at, :] = jnp.concatenate([sr, si], axis=1)
        edge = slice(0, 1) if reverse else slice(SCAN_BLOCK - 1, SCAN_BLOCK)
        return sr[edge, :], si[edge, :]

    return lax.fori_loop(0, nblk, step, carry)


def _s5_fwd(name, proj, col0, bblk, cblk, lamblk, dvec, width, tr):
    t_rows = proj.shape[0]
    ni = t_rows // tr
    nb = width // LANES
    sw = bblk.shape[-1]
    hw = sw // 2

    def body(u_ref, b_ref, c_ref, lam_ref, d_ref, y_ref, s_ref, carry_ref, pow_ref, st_ref):
        i = pl.program_id(1)
        row = lax.broadcasted_iota(jnp.int32, (tr, hw), 0)
        lam = lam_ref[...]
        lr, li = lam[:, :hw], lam[:, hw:]

        @pl.when(i == 0)
        def _():
            carry_ref[...] = jnp.zeros_like(carry_ref)
            zr, zi = _block_powers(lr, li, hw, False)
            pow_ref[:, :hw] = zr
            pow_ref[:, hw:] = zi

        u = u_ref[...]
        bu = jnp.dot(u.astype(BF16), b_ref[...], preferred_element_type=F32)
        sr, si = _block_scan(bu[:, :hw], bu[:, hw:], lr, li, row, tr, False)
        st_ref[:, :hw] = sr
        st_ref[:, hw:] = si
        cr, ci = _chain_blocks(st_ref, pow_ref, (carry_ref[0:1, :hw], carry_ref[0:1, hw:]), tr, hw, False)
        carry_ref[...] = jnp.broadcast_to(jnp.concatenate([cr, ci], axis=1), carry_ref.shape)
        s16 = st_ref[...].astype(BF16)
        s_ref[...] = s16
        y_ref[...] = jnp.dot(s16, c_ref[...], preferred_element_type=F32) + d_ref[...] * u

    return pl.pallas_call(
        body, name=name, grid=(nb, ni),
        in_specs=[pl.BlockSpec((tr, LANES), lambda k, i: (i, col0 + k)),
                  pl.BlockSpec((None, LANES, sw), lambda k, i: (k, 0, 0)),
                  pl.BlockSpec((None, sw, LANES), lambda k, i: (k, 0, 0)),
                  pl.BlockSpec((None, 1, sw), lambda k, i: (k, 0, 0)),
                  pl.BlockSpec((1, LANES), lambda k, i: (0, k))],
        out_specs=[pl.BlockSpec((tr, LANES), lambda k, i: (i, k)),
                   pl.BlockSpec((tr, sw), lambda k, i: (i, k))],
        out_shape=[jax.ShapeDtypeStruct((t_rows, width), F32), jax.ShapeDtypeStruct((t_rows, nb * sw), BF16)],
        scratch_shapes=[pltpu.VMEM((8, sw), F32), pltpu.VMEM((SCAN_BLOCK, sw), F32), pltpu.VMEM((tr, sw), F32)],
        compiler_params=_params(("parallel", "arbitrary")),
    )(proj, bblk, cblk, lamblk, dvec)


def _s5_bwd(name, dy, proj, col0, states, bblk, cblk, lamblk, dvec, width, tr):
    t_rows = dy.shape[0]
    ni = t_rows // tr
    nb = width // LANES
    sw = bblk.shape[-1]
    hw = sw // 2
    hb16 = tr // 16

    def rmap(k, i):
        return (ni - 1 - i, k)

    def body(dy_ref, u_ref, s_ref, sprev_ref, b_ref, c_ref, lam_ref, d_ref,
             du_ref, db_ref, dc_ref, dlam_ref, dd_ref, carry_ref, pow_ref, gt_ref):
        i = pl.program_id(1)
        row = lax.broadcasted_iota(jnp.int32, (tr, hw), 0)
        row_w = lax.broadcasted_iota(jnp.int32, (tr, sw), 0)
        lam = lam_ref[...]
        lr, li = lam[:, :hw], -lam[:, hw:]

        @pl.when(i == 0)
        def _():
            carry_ref[...] = jnp.zeros_like(carry_ref)
            zr, zi = _block_powers(lr, li, hw, True)
            pow_ref[:, :hw] = zr
            pow_ref[:, hw:] = zi

        dyv = dy_ref[...]
        dy16 = dyv.astype(BF16)
        u = u_ref[...]
        gd = lax.dot_general(dy16, c_ref[...], _DN['nt'], preferred_element_type=F32)
        gr, gi = _block_scan(gd[:, :hw], gd[:, hw:], lr, li, row, tr, True)
        gt_ref[:, :hw] = gr
        gt_ref[:, hw:] = gi
        cr, ci = _chain_blocks(gt_ref, pow_ref, (carry_ref[0:1, :hw], carry_ref[0:1, hw:]), tr, hw, True)
        carry_ref[...] = jnp.broadcast_to(jnp.concatenate([cr, ci], axis=1), carry_ref.shape)
        g = gt_ref[...]
        gr, gi = g[:, :hw], g[:, hw:]
        g16 = g.astype(BF16)
        du_ref[...] = lax.dot_general(g16, b_ref[...], _DN['nt'], preferred_element_type=F32) + d_ref[...] * dyv
        s16 = s_ref[...]
        dbv = lax.dot_general(g16, u.astype(BF16), _DN['tn'], preferred_element_type=F32)
        dcv = lax.dot_general(s16, dy16, _DN['tn'], preferred_element_type=F32)
        s32 = s16.astype(F32)
        first = jnp.where(i < ni - 1, sprev_ref[15:16, :].astype(F32), 0.0)
        sp = jnp.where(row_w == 0, jnp.broadcast_to(first, (tr, sw)), pltpu.roll(s32, 1, 0))
        spr, spi = sp[:, :hw], sp[:, hw:]
        dlr = jnp.sum(gr * spr + gi * spi, axis=0, keepdims=True)
        dli = jnp.sum(gi * spr - gr * spi, axis=0, keepdims=True)
        dlam = jnp.concatenate([dlr, dli], axis=1)
        ddv = jnp.sum(dyv * u, axis=0, keepdims=True)

        @pl.when(i == 0)
        def _():
            db_ref[...] = dbv
            dc_ref[...] = dcv
            dlam_ref[...] = dlam
            dd_ref[...] = ddv

        @pl.when(i > 0)
        def _():
            db_ref[...] += dbv
            dc_ref[...] += dcv
            dlam_ref[...] += dlam
            dd_ref[...] += ddv

    return pl.pallas_call(
        body, name=name, grid=(nb, ni),
        in_specs=[pl.BlockSpec((tr, LANES), rmap),
                  pl.BlockSpec((tr, LANES), lambda k, i: (ni - 1 - i, col0 + k)),
                  pl.BlockSpec((tr, sw), rmap),
                  pl.BlockSpec((16, sw), lambda k, i: (jnp.maximum((ni - 1 - i) * hb16 - 1, 0), k)),
                  pl.BlockSpec((None, LANES, sw), lambda k, i: (k, 0, 0)),
                  pl.BlockSpec((None, sw, LANES), lambda k, i: (k, 0, 0)),
                  pl.BlockSpec((None, 1, sw), lambda k, i: (k, 0, 0)),
                  pl.BlockSpec((1, LANES), lambda k, i: (0, k))],
        out_specs=[pl.BlockSpec((tr, LANES), rmap),
                   pl.BlockSpec((None, sw, LANES), lambda k, i: (k, 0, 0)),
                   pl.BlockSpec((None, sw, LANES), lambda k, i: (k, 0, 0)),
                   pl.BlockSpec((None, 1, sw), lambda k, i: (k, 0, 0)),
                   pl.BlockSpec((1, LANES), lambda k, i: (0, k))],
        out_shape=[jax.ShapeDtypeStruct((t_rows, width), F32), jax.ShapeDtypeStruct((nb, sw, LANES), F32),
                   jax.ShapeDtypeStruct((nb, sw, LANES), F32), jax.ShapeDtypeStruct((nb, 1, sw), F32),
                   jax.ShapeDtypeStruct((1, width), F32)],
        scratch_shapes=[pltpu.VMEM((8, sw), F32), pltpu.VMEM((SCAN_BLOCK, sw), F32), pltpu.VMEM((tr, sw), F32)],
        compiler_params=_params(("parallel", "arbitrary")),
    )(dy, proj, states, states, bblk, cblk, lamblk, dvec)


def _blockdiag(m):
    nb, g, p, q = m.shape
    eye = jnp.eye(g, dtype=m.dtype)
    return (m[:, :, :, None, :] * eye[None, :, None, :, None]).reshape(nb, g * p, g * q)


def _blockdiag_take(m, p, q):
    nb = m.shape[0]
    g = GROUPS_PER_BLOCK
    m = m.reshape(nb, g, p, g, q)
    return jnp.stack([m[:, k, :, k, :] for k in range(g)], axis=1)


def _adamw(name, w, g, m, v, tr):
    rows, cols = w.shape
    gcols = g.shape[1]
    c1 = 1.0 - ADAM_B1 ** ADAM_STEP
    c2 = 1.0 - ADAM_B2 ** ADAM_STEP

    def body(w_ref, g_ref, m_ref, v_ref, go_ref, d_ref, mo_ref, vo_ref):
        gv = g_ref[...] if gcols == cols else g_ref[:, :cols]
        mn = ADAM_B1 * m_ref[...] + (1.0 - ADAM_B1) * gv
        vn = ADAM_B2 * v_ref[...] + (1.0 - ADAM_B2) * (gv * gv)
        go_ref[...] = gv
        mo_ref[...] = mn
        vo_ref[...] = vn
        d_ref[...] = -ADAM_LR * ((mn / c1) / (jnp.sqrt(vn / c2) + ADAM_EPS) + ADAM_WD * w_ref[...])

    spec = pl.BlockSpec((tr, cols), lambda i: (i, 0))
    return pl.pallas_call(
        body, name=name, grid=(rows // tr,),
        in_specs=[spec, pl.BlockSpec((tr, gcols), lambda i: (i, 0)), spec, spec],
        out_specs=[spec] * 4, out_shape=[jax.ShapeDtypeStruct((rows, cols), F32)] * 4,
        compiler_params=_params(("parallel",)),
    )(w, g, m, v)


def _ffn_fwd(tag, h, norm, wg, wu, wd, dims, host=None):
    t_rows, d, fp, tm, tr, trn = dims['T'], dims['D'], dims['FP'], dims['TM'], dims['TR'], dims['TRN']
    host = host or {}
    got = {}
    ft = N_CHIPS * fp
    tn = _tile(fp, 1408)
    npb = fp // tn
    n16, = _rowmap(f"{tag}_norm", lambda x, g: _rms(x, g).astype(BF16), (1, t_rows // trn),
                   [_rows(h, trn), _full(norm)], [_o((t_rows, d), BF16, trn)])

    def up(nm, w):
        res = _matmul(f"{tag}_{nm}", 'nt', n16, (tm // 2, d // 2), lambda i, j, k: (i, k),
                      w, (None, fp, d // 2), lambda i, j, k: (j, 0, k),
                      (2 * t_rows // tm, N_CHIPS, 2), [((t_rows, ft), BF16, (tm // 2, fp), lambda i, j: (i, j))],
                      lambda acc: acc, comms=host.get(nm, ()))
        got[nm] = res[1:]
        return res[0]

    gate = up("gate", wg)
    wu = got['gate'][wu] if isinstance(wu, int) else wu
    upv = up("up", wu)
    wd = got['up'][wd] if isinstance(wd, int) else wd
    tc = _tile(ft, 1408)
    act, = _rowmap(f"{tag}_act", lambda g, u: _silu(g.astype(F32)) * u.astype(F32), (ft // tc, t_rows // tr),
                   [_rows(gate, tr, tc), _rows(upv, tr, tc)], [_o((t_rows, ft), BF16, tr, tc)])
    tnd = _tile(d, 1024)
    res = _matmul(f"{tag}_down", 'nn', act, (tm, fp), lambda i, j, k: (i, k),
                  wd, (None, fp, tnd), lambda i, j, k: (k, 0, j),
                  (t_rows // tm, d // tnd, N_CHIPS), [((t_rows, d), F32, (tm, tnd), lambda i, j: (i, j))],
                  lambda acc, hin: hin + 0.5 * acc, extras=[(h, (tm, tnd), lambda i, j: (i, j))],
                  comms=host.get('down', ()))
    got['down'] = res[1:]
    return res[0], dict(h=h, n16=n16, gate=gate, up=upv, act=act, wu=wu, wd=wd), got


def _pair(nm, g, pos):
    return _pair_sum(f"pair_{nm}", g, _run_comm(f"swap_{nm}", _swap_comm(g))[0], pos)


def _ffn_bwd(tag, names, dh, dh16, saved, norm, wg, dims, pos, host=None):
    t_rows, d, fp, tm, trn = dims['T'], dims['D'], dims['FP'], dims['TM'], dims['TRN']
    host = host or {}
    got = {}
    n_gate, n_up, n_down = names
    wu, wd = saved['wu'], saved['wd']
    ft = N_CHIPS * fp
    tn = _tile(fp, 1408)
    npb = fp // tn
    tm2 = tm // 2
    tk = t_rows // 2

    def act_bwd(acc, g, u):
        da = 0.5 * acc
        g32, u32 = g.astype(F32), u.astype(F32)
        sg = _sigmoid(g32)
        return da * u32 * (sg * (1.0 + g32 * (1.0 - sg))), da * (g32 * sg)

    res = _matmul(
        f"{tag}_dact", 'nt', dh16, (tm2, d // 2), lambda i, j, k: (i, k),
        wd, (None, tn, d // 2), lambda i, j, k: (j // npb, j % npb, k),
        (t_rows // tm2, ft // tn, 2),
        [((t_rows, ft), BF16, (tm2, tn), lambda i, j: (i, j))] * 2, act_bwd,
        extras=[(saved['gate'], (tm2, tn), lambda i, j: (i, j)), (saved['up'], (tm2, tn), lambda i, j: (i, j))],
        comms=host.get('dact', ()))
    dgate, dup, got['dact'] = res[0], res[1], res[2:]
    tnd = _tile(d, 1024)
    tw = _tile(d, 512)
    res = _matmul(f"{tag}_dwd", 'tn', saved['act'], (t_rows, tn), lambda i, j, k: (0, i),
                  dh16, (t_rows, tw), lambda i, j, k: (0, j),
                  (ft // tn, d // tw, 1),
                  [((N_CHIPS, fp, d), BF16, (None, tn, tw), lambda i, j: (i // npb, i % npb, j))],
                  lambda acc: 0.5 * acc, comms=host.get('dwd', ()))
    dwd, got['dwd'] = res[0], res[1:]

    def dw_up(nm, dact, comms):
        return _matmul(f"{tag}_{nm}", 'tn', dact, (t_rows, tn), lambda i, j, k: (0, i),
                       saved['n16'], (t_rows, tw), lambda i, j, k: (0, j),
                       (ft // tn, d // tw, 1),
                       [((N_CHIPS, fp, d), BF16, (None, tn, tw), lambda i, j: (i // npb, i % npb, j))],
                       lambda acc: acc, comms=comms)

    def dn_part(nm, dact, w, prev, comms):
        extras = [] if prev is None else [(prev, (tm, tnd), lambda i, j: (i, j))]
        return _matmul(f"{tag}_{nm}", 'nn', dact, (tm, fp), lambda i, j, k: (i, k),
                       w, (None, fp, tnd), lambda i, j, k: (k, 0, j),
                       (t_rows // tm, d // tnd, N_CHIPS), [((t_rows, d), F32, (tm, tnd), lambda i, j: (i, j))],
                       (lambda acc: acc) if prev is None else (lambda acc, p: acc + p), extras=extras, comms=comms)

    dwg, swap_d = dw_up("dwg", dgate, [_swap_comm(dwd)])
    pair_d = _pair_sum(f"pair_{n_down}", dwd, swap_d, pos)
    dwu, land_d, swap_g = dw_up("dwu", dup, [_scatter_comm(pair_d), _swap_comm(dwg)])
    pair_g = _pair_sum(f"pair_{n_gate}", dwg, swap_g, pos)
    dn_g, land_g, swap_u = dn_part("dn_gate", dgate, wg, None, [_scatter_comm(pair_g), _swap_comm(dwu)])
    pair_u = _pair_sum(f"pair_{n_up}", dwu, swap_u, pos)
    dn, land_u = dn_part("dn_up", dup, wu, dn_g, [_scatter_comm(pair_u)])

    def norm_bwd(x, dnv, dhv, g):
        _, vjp = jax.vjp(_rms, x, g)
        dx, dg = vjp(dnv)
        res = dhv + dx
        return res, res.astype(BF16), dg

    dh_in, dh_in16, dnorm = _rowmap(
        f"{tag}_dnorm", norm_bwd, (1, t_rows // trn),
        [_rows(saved['h'], trn), _rows(dn, trn), _rows(dh, trn), _full(norm)],
        [_o((t_rows, d), F32, trn), _o((t_rows, d), BF16, trn)], [_acc((1, d))])
    parts = {n_down: (pair_d, land_d), n_gate: (pair_g, land_g), n_up: (pair_u, land_u)}
    return dh_in, dh_in16, dnorm, parts, got


def kernel(x, meta_tokens, ffn1_norm, ffn1_w_gate, ffn1_w_up, ffn1_w_down, mix_norm, w_in, rg_conv_w, rg_conv_b, rg_w_a, rg_b_a, rg_w_x, rg_b_x, rg_lambda, s5_lambda_re, s5_lambda_im, s5_log_dt, s5_b_re, s5_b_im, s5_c_re, s5_c_im, s5_d, s5_glu_w, s5_glu_b, rg_out_norm, s5_out_norm, w_out, ffn2_norm, ffn2_w_gate, ffn2_w_up, ffn2_w_down, final_norm, loss_target, m_meta_tokens, m_ffn1_norm, m_ffn1_w_gate, m_ffn1_w_up, m_ffn1_w_down, m_mix_norm, m_w_in, m_rg_conv_w, m_rg_conv_b, m_rg_w_a, m_rg_b_a, m_rg_w_x, m_rg_b_x, m_rg_lambda, m_s5_lambda_re, m_s5_lambda_im, m_s5_log_dt, m_s5_b_re, m_s5_b_im, m_s5_c_re, m_s5_c_im, m_s5_d, m_s5_glu_w, m_s5_glu_b, m_rg_out_norm, m_s5_out_norm, m_w_out, m_ffn2_norm, m_ffn2_w_gate, m_ffn2_w_up, m_ffn2_w_down, m_final_norm, v_meta_tokens, v_ffn1_norm, v_ffn1_w_gate, v_ffn1_w_up, v_ffn1_w_down, v_mix_norm, v_w_in, v_rg_conv_w, v_rg_conv_b, v_rg_w_a, v_rg_b_a, v_rg_w_x, v_rg_b_x, v_rg_lambda, v_s5_lambda_re, v_s5_lambda_im, v_s5_log_dt, v_s5_b_re, v_s5_b_im, v_s5_c_re, v_s5_c_im, v_s5_d, v_s5_glu_w, v_s5_glu_b, v_rg_out_norm, v_s5_out_norm, v_w_out, v_ffn2_norm, v_ffn2_w_gate, v_ffn2_w_up, v_ffn2_w_down, v_final_norm):
    args = locals()
    w = {n: args[n] for n in WEIGHTS}
    mom = {n: args["m_" + n] for n in WEIGHTS}
    var = {n: args["v_" + n] for n in WEIGHTS}

    seq, d = x.shape[1], x.shape[2]
    f_shard = ffn1_w_gate.shape[2]
    fp = _round_up(f_shard, LANES)
    r = rg_conv_b.shape[1]
    s5w = s5_d.shape[1]
    hd = r // RG_HEADS
    groups = s5w // S5_GROUP
    nb = groups // GROUPS_PER_BLOCK
    t_rows = _round_up(N_META + seq, LANES)
    tm, tr, tk = t_rows // 4, t_rows // 8, t_rows // 2
    trn = _tile(t_rows, max(t_rows // 32, 16), 16)
    dims = dict(T=t_rows, D=d, FP=fp, TM=tm, TR=tr, TRN=trn)
    cx, cy, cc = lax.axis_index("x"), lax.axis_index("y"), lax.axis_index("c")
    chip = 2 * cx + cy
    pos = jnp.stack([chip, cc]).astype(jnp.int32)

    def gather(nm, buf):
        return _run_comm(f"ag_{nm}", _gather_comm(buf))[0]

    def gather_f32(nm, local):
        buf = lax.dynamic_update_slice(jnp.zeros((N_CHIPS,) + local.shape, F32), local[None], (chip, 0, 0))
        return gather(nm, buf)

    slot = {}
    transposed = ('ffn1_w_gate', 'ffn1_w_up', 'ffn2_w_gate', 'ffn2_w_up')
    local2d = lambda tree, nm: jnp.swapaxes(tree[nm][0], 0, 1) if nm in transposed else tree[nm][0]
    for nm in transposed:
        slot[nm] = _cast_pad(f"cast_{nm}", local2d(w, nm), fp, d, pos)
    for nm in ('ffn1_w_down', 'ffn2_w_down'):
        slot[nm] = _cast_pad(f"cast_{nm}", w[nm][0], fp, d, pos)
    slot['w_in'] = _cast_pad("cast_w_in", w_in[0], d, w_in.shape[2], pos)
    slot['w_out'] = _cast_pad("cast_w_out", w_out[0], w_out.shape[1], d, pos)
    slot['s5_glu_w'] = _cast_pad("cast_glu", s5_glu_w[0], s5_glu_w.shape[1], s5w, pos)
    rgw_local = jnp.concatenate([rg_w_a[0], rg_w_x[0]], axis=0).reshape(2 * RG_HEADS * (hd // N_CHIPS), hd)
    rgw = gather('rgw', _cast_pad("cast_rgw", rgw_local, rgw_local.shape[0], hd, pos))
    rgw = rgw.reshape(N_CHIPS, 2, RG_HEADS, hd // N_CHIPS, hd).transpose(1, 2, 0, 3, 4).reshape(2, RG_HEADS, hd, hd)
    w_a16, w_x16 = rgw[0], rgw[1]
    meta_full = gather_f32('meta', meta_tokens).transpose(1, 0, 2).reshape(N_META, d)
    conv_w2, b_a2, b_x2 = _gather_small_shards(rg_conv_w[0], rg_b_a[0], rg_b_x[0], gather_f32)
    wfull = {'ffn1_w_gate': gather('ffn1_w_gate', slot['ffn1_w_gate'])}

    pad_rows = t_rows - N_META - seq
    h0 = jnp.concatenate([meta_full, x[0], jnp.zeros((pad_rows, d), F32)], axis=0)
    tgt = jnp.concatenate([jnp.zeros((N_META, d), F32), loss_target[0], jnp.zeros((pad_rows, d), F32)], axis=0)

    h1, sv1, got = _ffn_fwd(
        "ffn1", h0, ffn1_norm, wfull['ffn1_w_gate'], 0, 0, dims,
        host={'gate': [_gather_comm(slot['ffn1_w_up']), _gather_comm(slot['w_in'])],
              'up': [_gather_comm(slot['ffn1_w_down']), _gather_comm(slot['s5_glu_w']), _gather_comm(slot['w_out'])],
              'down': [_gather_comm(slot['ffn2_w_gate'])]})
    wfull['w_in'] = got['gate'][1]
    wfull['s5_glu_w'] = got['up'][1].reshape(s5w, s5w)
    wfull['w_out'] = got['up'][2].reshape(d, d)
    wfull['ffn2_w_gate'] = got['down'][0]

    n2, = _rowmap("mix_norm", lambda xv, g: _rms(xv, g).astype(BF16), (1, t_rows // trn),
                  [_rows(h1, trn), _full(mix_norm)], [_o((t_rows, d), BF16, trn)])
    pw = w_in.shape[2]
    tnp = _tile(pw, 1536)
    nppb = pw // tnp
    proj, = _matmul(
        "in_proj", 'nn', n2, (tm, d // 2), lambda i, j, k: (i, k),
        wfull['w_in'], (None, d // 2, tnp), lambda i, j, k: (j // nppb, k, j % nppb),
        (t_rows // tm, N_CHIPS * nppb, 2), [((t_rows, 3 * r), F32, (tm, tnp), lambda i, j: (i, j))],
        lambda acc: acc)

    xc = _conv_fwd("rg_conv", proj, conv_w2, rg_conv_b, r, tr)

    def head_cols(arr):
        return (arr, (tr, hd), lambda j, i: (i, j))

    def head_w(arr):
        return (arr, (None, hd, hd), lambda j, i: (j, 0, 0))

    def head_vec(arr):
        return (arr, (1, hd), lambda j, i: (0, j))

    def gates_fwd(xcv, wa, wx, ba, bx, lam):
        x16 = xcv.astype(BF16)
        pre_r = jnp.dot(x16, wa, preferred_element_type=F32) + ba
        pre_i = jnp.dot(x16, wx, preferred_element_type=F32) + bx
        return _rg_gate_math(xcv, pre_r, pre_i, _softplus_neg(lam))

    gate_params = [head_w(w_a16), head_w(w_x16), head_vec(b_a2), head_vec(b_x2), head_vec(rg_lambda)]
    head_out = ((t_rows, r), F32, (tr, hd), lambda j, i: (i, j))
    a_dec, bxv = _rowmap("rg_gates", gates_fwd, (RG_HEADS, t_rows // tr), [head_cols(xc)] + gate_params,
                         [head_out, head_out])
    h_rg = _linear_scan("rg_scan", a_dec, bxv, tr)

    expand = jnp.repeat(jnp.eye(S5_STATE, dtype=F32), S5_GROUP, axis=1)
    b_re2 = s5_b_re[0].reshape(groups, S5_STATE * S5_GROUP)
    b_im2 = s5_b_im[0].reshape(groups, S5_STATE * S5_GROUP)
    log_dt2 = s5_log_dt[0].reshape(groups, 1)
    lbr, lbi, bbr, bbi = _s5_disc("s5_disc", s5_lambda_re[0], s5_lambda_im[0], log_dt2, b_re2, b_im2, expand)

    def to_bblk(m):
        return _blockdiag(m.reshape(nb, GROUPS_PER_BLOCK, S5_STATE, S5_GROUP).transpose(0, 1, 3, 2))

    def to_cblk(m):
        return _blockdiag(m.reshape(nb, GROUPS_PER_BLOCK, S5_GROUP, S5_STATE).transpose(0, 1, 3, 2))

    bblk = jnp.concatenate([to_bblk(bbr), to_bblk(bbi)], axis=-1).astype(BF16)
    cblk = jnp.concatenate([to_cblk(s5_c_re[0]), -to_cblk(s5_c_im[0])], axis=-2).astype(BF16)
    hw = GROUPS_PER_BLOCK * S5_STATE
    lamblk = jnp.concatenate([lbr.reshape(nb, 1, hw), lbi.reshape(nb, 1, hw)], axis=-1)
    col0 = 2 * r // LANES
    y_pre, states = _s5_fwd("s5_fwd", proj, col0, bblk, cblk, lamblk, s5_d, s5w, tr)

    z16, = _rowmap("s5_gelu", lambda yv: _gelu(yv).astype(BF16), (1, t_rows // trn),
                   [_rows(y_pre, trn)], [_o((t_rows, s5w), BF16, trn)])
    tns = _tile(s5w, 1024)

    def glu_fwd(acc, yv, b):
        gl = acc + b
        return _gelu(yv) * _sigmoid(gl), gl

    y_s5, gl = _matmul("s5_glu", 'nn', z16, (tm, s5w), lambda i, j, k: (i, 0),
                       wfull['s5_glu_w'], (s5w, tns), lambda i, j, k: (0, j),
                       (t_rows // tm, s5w // tns, 1), [((t_rows, s5w), F32, (tm, tns), lambda i, j: (i, j))] * 2,
                       glu_fwd, extras=[(y_pre, (tm, tns), lambda i, j: (i, j)), (s5_glu_b, (1, tns), lambda i, j: (0, j))])

    def mix_out(hv, gv, yv, g1, g2):
        return jnp.concatenate([_rms(hv * _gelu(gv), g1), _rms(yv, g2)], axis=1)

    gate_in = (proj, (trn, r), lambda j, i: (i, 1))
    ycat16, = _rowmap("mix_out", lambda *a: mix_out(*a).astype(BF16), (1, t_rows // trn),
                      [_rows(h_rg, trn), gate_in, _rows(y_s5, trn), _full(rg_out_norm), _full(s5_out_norm)],
                      [_o((t_rows, r + s5w), BF16, trn)])
    tnd = _tile(d, 1024)
    h2, = _matmul(
        "out_proj", 'nn', ycat16, (tm, (r + s5w) // 2), lambda i, j, k: (i, k),
        wfull['w_out'], ((r + s5w) // 2, tnd), lambda i, j, k: (k, j),
        (t_rows // tm, d // tnd, 2), [((t_rows, d), F32, (tm, tnd), lambda i, j: (i, j))],
        lambda acc, hin: hin + acc, extras=[(h1, (tm, tnd), lambda i, j: (i, j))])

    h3, sv2, _ = _ffn_fwd(
        "ffn2", h2, ffn2_norm, wfull['ffn2_w_gate'], 0, 0, dims,
        host={'gate': [_gather_comm(slot['ffn2_w_up'])], 'up': [_gather_comm(slot['ffn2_w_down'])]})

    fnorm2 = final_norm.reshape(1, d)

    def head(xv, tv, g):
        i = pl.program_id(1)
        rowi = lax.broadcasted_iota(jnp.int32, (trn, 1), 0) + i * trn
        mask = jnp.where((rowi >= N_META) & (rowi < N_META + seq), 1.0, 0.0)
        out, vjp = jax.vjp(_rms, xv, g)
        err = (out - tv) * mask
        dx, dg = vjp(err * (1.0 / d))
        return dx, dx.astype(BF16), jnp.sum(err * err, axis=0, keepdims=True), dg

    dh3, dh3_16, loss_cols, d_final = _rowmap(
        "loss_head", head, (1, t_rows // trn), [_rows(h3, trn), _rows(tgt, trn), _full(fnorm2)],
        [_o((t_rows, d), F32, trn), _o((t_rows, d), BF16, trn)], [_acc((1, d)), _acc((1, d))])
    loss = lax.psum(0.5 * jnp.sum(loss_cols) / d, ("x", "y", "c"))

    small = {}
    dh2, dh2_16, small['ffn2_norm'], parts, _ = _ffn_bwd(
        "ffn2b", ('ffn2_w_gate', 'ffn2_w_up', 'ffn2_w_down'), dh3, dh3_16, sv2, ffn2_norm, wfull['ffn2_w_gate'],
        dims, pos)

    dycat, = _matmul("d_out_proj", 'nt', dh2_16, (tm, d // 2), lambda i, j, k: (i, k),
                     wfull['w_out'], (tnd, d // 2), lambda i, j, k: (j, k),
                     (t_rows // tm, (r + s5w) // tnd, 2), [((t_rows, r + s5w), F32, (tm, tnd), lambda i, j: (i, j))],
                     lambda acc: acc)
    wo_rows = (r + s5w) // N_CHIPS
    tno = _tile(wo_rows, 1024)
    npo = wo_rows // tno
    tw = _tile(d, 512)
    dw_out, = _matmul("dw_out", 'tn', ycat16, (t_rows, tno), lambda i, j, k: (0, i),
                      dh2_16, (t_rows, tw), lambda i, j, k: (0, j),
                      ((r + s5w) // tno, d // tw, 1),
                      [((N_CHIPS, wo_rows, d), BF16, (None, tno, tw), lambda i, j: (i // npo, i % npo, j))],
                      lambda acc: acc)
    pair_w_out = _pair('w_out', dw_out, pos)

    def mix_out_bwd(hv, gv, yv, dyc, g1, g2):
        _, vjp = jax.vjp(mix_out, hv, gv, yv, g1, g2)
        return vjp(dyc)

    dh_out, dgate_rg, dy_s5, small['rg_out_norm'], small['s5_out_norm'] = _rowmap(
        "mix_out_bwd", mix_out_bwd, (1, t_rows // trn),
        [_rows(h_rg, trn), gate_in, _rows(y_s5, trn), _rows(dycat, trn), _full(rg_out_norm), _full(s5_out_norm)],
        [_o((t_rows, r), F32, trn), _o((t_rows, r), F32, trn), _o((t_rows, s5w), F32, trn)],
        [_acc((1, r)), _acc((1, s5w))])

    def glu_bwd(dyv, yv, glv):
        zv = _gelu(yv)
        sg = _sigmoid(glv)
        dgl = dyv * zv * sg * (1.0 - sg)
        return dyv * sg, dgl.astype(BF16), jnp.sum(dgl, axis=0, keepdims=True)

    dz_dir, dgl16, small['s5_glu_b'] = _rowmap(
        "s5_glu_bwd", glu_bwd, (1, t_rows // trn), [_rows(dy_s5, trn), _rows(y_pre, trn), _rows(gl, trn)],
        [_o((t_rows, s5w), F32, trn), _o((t_rows, s5w), BF16, trn)], [_acc((1, s5w))])

    def dgelu(acc, dzd, yv):
        _, vjp = jax.vjp(_gelu, yv)
        return vjp(acc + dzd)[0]

    dy_pre, = _matmul("s5_dz", 'nt', dgl16, (tm, s5w), lambda i, j, k: (i, 0),
                      wfull['s5_glu_w'], (tns, s5w), lambda i, j, k: (j, 0),
                      (t_rows // tm, s5w // tns, 1), [((t_rows, s5w), F32, (tm, tns), lambda i, j: (i, j))],
                      dgelu, extras=[(dz_dir, (tm, tns), lambda i, j: (i, j)), (y_pre, (tm, tns), lambda i, j: (i, j))])
    gl_rows = s5w // N_CHIPS
    tng = _tile(gl_rows, 1024)
    npg = gl_rows // tng
    dw_glu, = _matmul("dw_glu", 'tn', z16, (t_rows, tng), lambda i, j, k: (0, i),
                      dgl16, (t_rows, tns), lambda i, j, k: (0, j),
                      (s5w // tng, s5w // tns, 1),
                      [((N_CHIPS, gl_rows, s5w), BF16, (None, tng, tns), lambda i, j: (i // npg, i % npg, j))],
                      lambda acc: acc)
    pair_glu = _pair('s5_glu_w', dw_glu, pos)
    du_s5, dbblk, dcblk, dlamblk, small['s5_d'] = _s5_bwd(
        "s5_bwd", dy_pre, proj, col0, states, bblk, cblk, lamblk, s5_d, s5w, tr)

    def from_bblk(m):
        return _blockdiag_take(m, S5_STATE, S5_GROUP).reshape(groups, S5_STATE * S5_GROUP)

    dbbr, dbbi = from_bblk(dbblk[:, :hw]), from_bblk(dbblk[:, hw:])
    dcs = lambda m: _blockdiag_take(m, S5_STATE, S5_GROUP).transpose(0, 1, 3, 2).reshape(1, groups, S5_GROUP, S5_STATE)
    small['s5_c_re'], small['s5_c_im'] = dcs(dcblk[:, :hw]), -dcs(dcblk[:, hw:])
    dlbr, dlbi = dlamblk[:, 0, :hw].reshape(groups, S5_STATE), dlamblk[:, 0, hw:].reshape(groups, S5_STATE)
    g_lre, g_lim, g_ldt, g_bre, g_bim = _s5_disc(
        "s5_disc_bwd", s5_lambda_re[0], s5_lambda_im[0], log_dt2, b_re2, b_im2, expand, cots=(dlbr, dlbi, dbbr, dbbi))
    small['s5_lambda_re'], small['s5_lambda_im'], small['s5_log_dt'] = g_lre[None], g_lim[None], g_ldt.reshape(1, groups)
    small['s5_b_re'] = g_bre.reshape(s5_b_re.shape)
    small['s5_b_im'] = g_bim.reshape(s5_b_im.shape)

    dh_rg = _linear_scan("rg_scan_bwd", a_dec, dh_out, tr, reverse=True, shift_a=True)

    hb = tr // 8

    def gates_bwd(xcv, dhv, hv, hprev, wa, wx, ba, bx, lam):
        i = pl.program_id(1)
        row = lax.broadcasted_iota(jnp.int32, (tr, hd), 0)
        first = jnp.where(i > 0, hprev[7:8, :], 0.0)
        h_prev = jnp.where(row == 0, jnp.broadcast_to(first, (tr, hd)), pltpu.roll(hv, 1, 0))
        x16 = xcv.astype(BF16)
        pre_r = jnp.dot(x16, wa, preferred_element_type=F32) + ba
        pre_i = jnp.dot(x16, wx, preferred_element_type=F32) + bx
        _, vjp = jax.vjp(_rg_gate_math, xcv, pre_r, pre_i, _softplus_neg(lam))
        dxc_, dpr, dpi, dsp = vjp((dhv * h_prev, dhv))
        dpr16, dpi16 = dpr.astype(BF16), dpi.astype(BF16)
        dxc_ = (dxc_ + lax.dot_general(dpr16, wa, _DN['nt'], preferred_element_type=F32)
                + lax.dot_general(dpi16, wx, _DN['nt'], preferred_element_type=F32))
        dwa = lax.dot_general(x16, dpr16, _DN['tn'], preferred_element_type=F32)
        dwx = lax.dot_general(x16, dpi16, _DN['tn'], preferred_element_type=F32)
        dlam = dsp * (-_sigmoid(-lam))
        return (dxc_, dwa, dwx, jnp.sum(dpr, axis=0, keepdims=True), jnp.sum(dpi, axis=0, keepdims=True), dlam)

    head_acc_w = ((RG_HEADS, hd, hd), F32, (None, hd, hd), lambda j, i: (j, 0, 0))
    head_acc_v = ((1, r), F32, (1, hd), lambda j, i: (0, j))
    h_halo = (h_rg, (8, hd), lambda j, i: (jnp.maximum(i * hb - 1, 0), j))
    dxc, d_wa, d_wx, d_ba, d_bx, d_lam = _rowmap(
        "rg_gates_bwd", gates_bwd, (RG_HEADS, t_rows // tr),
        [head_cols(xc), head_cols(dh_rg), head_cols(h_rg), h_halo] + gate_params,
        [head_out], [head_acc_w, head_acc_w, head_acc_v, head_acc_v, head_acc_v])
    small['rg_w_a'], small['rg_w_x'] = d_wa[None], d_wx[None]
    small['rg_b_a'], small['rg_b_x'] = d_ba.reshape(1, RG_HEADS, hd), d_bx.reshape(1, RG_HEADS, hd)
    small['rg_lambda'] = d_lam
    du_rg, d_convw, small['rg_conv_b'] = _conv_bwd("rg_conv_bwd", dxc, proj, conv_w2, r, tr)
    small['rg_conv_w'] = d_convw[None]

    dproj16, = _rowmap("dproj", lambda a, b, c: jnp.concatenate([a, b, c], axis=1).astype(BF16), (1, t_rows // trn),
                       [_rows(du_rg, trn), _rows(dgate_rg, trn), _rows(du_s5, trn)], [_o((t_rows, 3 * r), BF16, trn)])
    dn2, land_w_out = _matmul(
        "d_in_proj", 'nt', dproj16, (tm, pw), lambda i, j, k: (i, k),
        wfull['w_in'], (None, tnd, pw), lambda i, j, k: (k, j, 0),
        (t_rows // tm, d // tnd, N_CHIPS), [((t_rows, d), F32, (tm, tnd), lambda i, j: (i, j))],
        lambda acc: acc, comms=[_scatter_comm(pair_w_out)])
    parts['w_out'] = (pair_w_out, land_w_out)
    tq = _tile(pw, 768)
    nq = pw // tq
    dw_in, = _matmul("dw_in", 'tn', n2, (t_rows, tnd), lambda i, j, k: (0, i),
                     dproj16, (t_rows, tq), lambda i, j, k: (0, j),
                     (d // tnd, N_CHIPS * nq, 1),
                     [((N_CHIPS, d, pw), BF16, (None, tnd, tq), lambda i, j: (j // nq, i, j % nq))],
                     lambda acc: acc)
    pair_w_in = _pair('w_in', dw_in, pos)

    def norm_bwd(xv, dnv, dhv, g):
        _, vjp = jax.vjp(_rms, xv, g)
        dx, dg = vjp(dnv)
        res = dhv + dx
        return res, res.astype(BF16), dg

    dh1, dh1_16, small['mix_norm'] = _rowmap(
        "mix_dnorm", norm_bwd, (1, t_rows // trn), [_rows(h1, trn), _rows(dn2, trn), _rows(dh2, trn), _full(mix_norm)],
        [_o((t_rows, d), F32, trn), _o((t_rows, d), BF16, trn)], [_acc((1, d))])

    dh0, _, small['ffn1_norm'], parts1, got = _ffn_bwd(
        "ffn1b", ('ffn1_w_gate', 'ffn1_w_up', 'ffn1_w_down'), dh1, dh1_16, sv1, ffn1_norm, wfull['ffn1_w_gate'],
        dims, pos, host={'dact': [_scatter_comm(pair_glu), _scatter_comm(pair_w_in)]})
    parts.update(parts1)
    parts['s5_glu_w'] = (pair_glu, got['dact'][0])
    parts['w_in'] = (pair_w_in, got['dact'][1])
    grad_x = dh0[N_META:N_META + seq][None]
    small['meta_tokens'] = dh0[:N_META]
    small['final_norm'] = d_final.reshape(d)

    gbig, prev, half = {}, None, None
    for nm in BIG:
        res = _reduce_partials(f"red_{nm}", parts[nm][0], parts[nm][1], pos, _exchange_comm(half) if prev else None)
        if prev:
            gbig[prev] = res[1]
        prev, half = nm, res[0]
    gbig[prev] = _run_comm(f"exchange_{prev}", _exchange_comm(half))[0]
    gbig = {nm: g2.reshape(-1, g2.shape[2]) for nm, g2 in gbig.items()}

    small_names = [n for n in WEIGHTS if n not in BIG]
    full_shape = {n: small[n].shape for n in small_names}
    flat = jnp.concatenate([small[n].reshape(-1) for n in small_names])
    unit = 8 * 8 * LANES
    total = _round_up(flat.shape[0], unit)
    flat = jnp.concatenate([flat, jnp.zeros((total - flat.shape[0],), F32)])
    red = _all_reduce_small("ar_small", flat.reshape(total // LANES, LANES)).reshape(-1)
    gsmall = {}
    off = 0
    for n in small_names:
        size = math.prod(full_shape[n])
        gsmall[n] = _own_shard(n, red[off:off + size].reshape(full_shape[n]), w[n].shape, chip)
        off += size

    out_g, out_d, out_m, out_v = {}, {}, {}, {}
    for nm in BIG:
        w2 = local2d(w, nm)
        trw = _tile(w2.shape[0], 128 if w2.shape[1] < 4096 else 64, 8)
        res = _adamw(f"adamw_{nm}", w2, gbig[nm], local2d(mom, nm), local2d(var, nm), trw)
        back = (lambda a: jnp.swapaxes(a, 0, 1)[None]) if nm in transposed else (lambda a: a[None])
        out_g[nm], out_d[nm], out_m[nm], out_v[nm] = [back(a) for a in res]

    def pack(tree):
        fl = jnp.concatenate([tree[n].reshape(-1) for n in small_names])
        tot = _round_up(fl.shape[0], 8 * LANES)
        return jnp.concatenate([fl, jnp.zeros((tot - fl.shape[0],), F32)]).reshape(tot // LANES, LANES)

    wp, gp, mp, vp = pack(w), pack(gsmall), pack(mom), pack(var)
    res = _adamw("adamw_small", wp, gp, mp, vp, _tile(wp.shape[0], 512, 8))
    off = 0
    for n in small_names:
        size = math.prod(w[n].shape)
        for dst, src in zip((out_g, out_d, out_m, out_v), res):
            dst[n] = src.reshape(-1)[off:off + size].reshape(w[n].shape)
        off += size

    return (loss, grad_x, *[out_g[n] for n in WEIGHTS], *[out_d[n] for n in WEIGHTS],
            *[out_m[n] for n in WEIGHTS], *[out_v[n] for n in WEIGHTS])


def _gather_small_shards(conv_w, b_a, b_x, gather):
    cw = conv_w.shape[1]
    part = b_a.shape[1]

    def rows8(a):
        a = jnp.concatenate([a, jnp.zeros((a.shape[0], cw - a.shape[1]), F32)], axis=1)
        return jnp.concatenate([a, jnp.zeros((8 - a.shape[0], cw), F32)], axis=0) if a.shape[0] < 8 else a

    local = jnp.concatenate([rows8(conv_w), rows8(b_a), rows8(b_x), jnp.zeros((8, cw), F32)], axis=0)
    full = gather("rg_small", local)
    conv_full = full[:, :CONV_WIDTH].transpose(1, 0, 2).reshape(CONV_WIDTH, N_CHIPS * cw)
    bias = lambda k: full[:, 8 * k:8 * k + RG_HEADS, :part].transpose(1, 0, 2).reshape(1, RG_HEADS * N_CHIPS * part)
    return conv_full, bias(1), bias(2)


def _own_shard(name, g, local_shape, chip):
    if tuple(g.shape) == tuple(local_shape):
        return g
    axis = [k for k, (a, b) in enumerate(zip(g.shape, local_shape)) if a != b][0]
    size = local_shape[axis]
    return lax.dynamic_slice_in_dim(g, chip * size, size, axis=axis)
```

```python
import functools
import math

import jax
import jax.numpy as jnp
from jax import lax
from jax.experimental import pallas as pl
from jax.experimental.pallas import tpu as pltpu

F32 = jnp.float32
BF16 = jnp.bfloat16
MESH = pl.DeviceIdType.MESH

N_META = 16
RG_HEADS = 8
CONV_WIDTH = 4
RG_C = 8.0
S5_GROUP = 16
S5_STATE = 64
GROUPS_PER_BLOCK = 8
EPS = 1e-6
N_CHIPS = 4
LANES = 128
VMEM_LIMIT = 56 * 1024 * 1024

ADAM_LR = 0.001
ADAM_B1 = 0.9
ADAM_B2 = 0.999
ADAM_EPS = 1e-08
ADAM_WD = 0.01
ADAM_STEP = 10

WEIGHTS = ['meta_tokens', 'ffn1_norm', 'ffn1_w_gate', 'ffn1_w_up', 'ffn1_w_down', 'mix_norm', 'w_in', 'rg_conv_w',
           'rg_conv_b', 'rg_w_a', 'rg_b_a', 'rg_w_x', 'rg_b_x', 'rg_lambda', 's5_lambda_re', 's5_lambda_im',
           's5_log_dt', 's5_b_re', 's5_b_im', 's5_c_re', 's5_c_im', 's5_d', 's5_glu_w', 's5_glu_b', 'rg_out_norm',
           's5_out_norm', 'w_out', 'ffn2_norm', 'ffn2_w_gate', 'ffn2_w_up', 'ffn2_w_down', 'final_norm']
BIG = ('ffn1_w_gate', 'ffn1_w_up', 'ffn1_w_down', 'w_in', 's5_glu_w', 'w_out', 'ffn2_w_gate', 'ffn2_w_up',
       'ffn2_w_down')

_DN = {'nn': (((1,), (0,)), ((), ())), 'nt': (((1,), (1,)), ((), ())), 'tn': (((0,), (0,)), ((), ()))}


def _round_up(n, m):
    return (n + m - 1) // m * m


def _tile(n, pref, unit=LANES):
    best = None
    for t in range(unit, min(n, pref) + 1, unit):
        if n % t == 0:
            best = t
    return best if best is not None else n


def _params(sem=None):
    return pltpu.CompilerParams(dimension_semantics=sem, vmem_limit_bytes=VMEM_LIMIT)


def _rms(x, g):
    return x * lax.rsqrt(jnp.mean(x * x, axis=-1, keepdims=True) + EPS) * g


def _sigmoid(x):
    return 0.5 * (jnp.tanh(0.5 * x) + 1.0)


def _gelu(x):
    return 0.5 * x * (1.0 + jnp.tanh(math.sqrt(2.0 / math.pi) * (x + 0.044715 * (x * x * x))))


def _silu(x):
    return x * _sigmoid(x)


def _expm1(x):
    series = x * (1.0 + x * (1.0 / 2) * (1.0 + x * (1.0 / 3) * (1.0 + x * (1.0 / 4) * (1.0 + x * (1.0 / 5) * (1.0 + x * (1.0 / 6))))))
    return jnp.where(jnp.abs(x) < 0.3, series, jnp.exp(x) - 1.0)


def _softplus_neg(lam):
    m = jnp.maximum(-lam, 0.0)
    e = jnp.exp(-jnp.abs(lam))
    w = 1.0 + e
    log1p = jnp.where(w == 1.0, e, jnp.log(w) * (e / jnp.where(w == 1.0, 1.0, w - 1.0)))
    return m + log1p


def _rg_gate_math(xc, pre_r, pre_i, sp):
    r = 1.0 / (1.0 + jnp.exp(-pre_r))
    i = 1.0 / (1.0 + jnp.exp(-pre_i))
    log_a = -RG_C * r * sp
    a = jnp.exp(log_a)
    mult = jnp.sqrt(-_expm1(2.0 * log_a))
    return a, mult * i * xc


class _Comm:
    def __init__(self, arrays, out_shapes, aliased, sems, start, finish, mid=None):
        self.arrays, self.out_shapes, self.aliased, self.sems = arrays, out_shapes, aliased, sems
        self.start, self.finish, self.mid = start, finish, mid


def _matmul(name, mode, a, a_blk, a_map, b, b_blk, b_map, grid, outs, epilogue, extras=(), comms=()):
    ni, nj, nk = grid
    ne, no = len(extras), len(outs)
    sq = lambda blk: tuple(d for d in blk if d is not None)
    ab, bb = sq(a_blk), sq(b_blk)
    acc_shape = {'nn': (ab[0], bb[1]), 'nt': (ab[0], bb[0]), 'tn': (ab[1], bb[1])}[mode]
    n_cin = sum(len(cm.arrays) for cm in comms)
    n_cout = sum(len(cm.out_shapes) for cm in comms)
    n_acc = 1 if nk > 1 else 0

    def comm_refs(refs):
        cin = refs[2 + ne:2 + ne + n_cin]
        cout = refs[2 + ne + n_cin + no:2 + ne + n_cin + no + n_cout]
        csem = refs[2 + ne + n_cin + no + n_cout + n_acc:]
        for cm in comms:
            yield cm, cin[:len(cm.arrays)], cout[:len(cm.out_shapes)], csem[:len(cm.sems)]
            cin, cout, csem = cin[len(cm.arrays):], cout[len(cm.out_shapes):], csem[len(cm.sems):]

    def body(*refs):
        a_ref, b_ref = refs[0], refs[1]
        ex = refs[2:2 + ne]
        out = refs[2 + ne + n_cin:2 + ne + n_cin + no]
        if comms:
            @pl.when((pl.program_id(0) == 0) & (pl.program_id(1) == 0) & (pl.program_id(2) == 0))
            def _():
                for cm, cin, cout, csem in comm_refs(refs):
                    cm.start(cin, cout, csem)

            if any(cm.mid for cm in comms):
                step = (pl.program_id(0) * nj + pl.program_id(1)) * nk + pl.program_id(2)

                @pl.when(step == (3 * ni * nj * nk) // 5)
                def _():
                    for cm, cin, cout, csem in comm_refs(refs):
                        if cm.mid:
                            cm.mid(cin, cout, csem)

        part = lax.dot_general(a_ref[...], b_ref[...], _DN[mode], preferred_element_type=F32)

        def finish(acc):
            res = epilogue(acc, *[e[...] for e in ex])
            if not isinstance(res, tuple):
                res = (res,)
            for o, r in zip(out, res):
                o[...] = r.astype(o.dtype)

        if nk == 1:
            finish(part)
        else:
            acc_ref = refs[2 + ne + n_cin + no + n_cout]
            k = pl.program_id(2)

            @pl.when(k == 0)
            def _():
                acc_ref[...] = part

            @pl.when(k > 0)
            def _():
                acc_ref[...] += part

            @pl.when(k == nk - 1)
            def _():
                finish(acc_ref[...])

        if comms:
            @pl.when((pl.program_id(0) == ni - 1) & (pl.program_id(1) == nj - 1) & (pl.program_id(2) == nk - 1))
            def _():
                for cm, cin, cout, csem in comm_refs(refs):
                    cm.finish(cin, cout, csem)

    hbm = pl.BlockSpec(memory_space=pltpu.HBM)
    in_specs = [pl.BlockSpec(a_blk, a_map), pl.BlockSpec(b_blk, b_map)]
    in_specs += [pl.BlockSpec(blk, functools.partial(lambda m, i, j, k: m(i, j), m)) for _, blk, m in extras]
    in_specs += [hbm] * n_cin
    out_specs = [pl.BlockSpec(blk, functools.partial(lambda m, i, j, k: m(i, j), m)) for _, _, blk, m in outs]
    out_specs += [hbm] * n_cout
    aliases, cin_at, cout_at = {}, 2 + ne, no
    for cm in comms:
        if cm.aliased:
            aliases.update({cin_at + k: cout_at + k for k in range(len(cm.arrays))})
        cin_at, cout_at = cin_at + len(cm.arrays), cout_at + len(cm.out_shapes)
    res = pl.pallas_call(
        body, name=name, grid=grid, in_specs=in_specs, out_specs=out_specs,
        out_shape=[jax.ShapeDtypeStruct(s, d) for s, d, _, _ in outs] + [s for cm in comms for s in cm.out_shapes],
        scratch_shapes=([pltpu.VMEM(acc_shape, F32)] if nk > 1 else [])
        + [pltpu.SemaphoreType.DMA((n,)) for cm in comms for n in cm.sems],
        input_output_aliases=aliases,
        compiler_params=_params(("arbitrary",) * 3 if comms else ("parallel", "parallel", "arbitrary")),
    )(a, b, *[e for e, _, _ in extras], *[arr for cm in comms for arr in cm.arrays])
    return res


def _rows(arr, tr, tc=None, col0=0):
    if tc is None:
        return (arr, (tr, arr.shape[1]), lambda j, i: (i, 0))
    return (arr, (tr, tc), lambda j, i: (i, col0 + j))


def _full(arr):
    nd = arr.ndim
    return (arr, arr.shape, lambda j, i: (0,) * nd)


def _cols(arr, tc):
    return (arr, (arr.shape[0], tc), lambda j, i: (0, j))


def _rowmap(name, fn, grid, ins, outs, accs=(), scratch=()):
    nj, ni = grid
    n_in, n_out, n_acc = len(ins), len(outs), len(accs)

    def body(*refs):
        vals = [r[...] for r in refs[:n_in]]
        o_refs = refs[n_in:n_in + n_out]
        a_refs = refs[n_in + n_out:n_in + n_out + n_acc]
        s_refs = refs[n_in + n_out + n_acc:]
        res = fn(*vals, *s_refs)
        if not isinstance(res, tuple):
            res = (res,)
        for o, r in zip(o_refs, res[:n_out]):
            o[...] = r.astype(o.dtype)
        i = pl.program_id(1)
        for a_ref, r in zip(a_refs, res[n_out:]):
            @pl.when(i == 0)
            def _(a_ref=a_ref, r=r):
                a_ref[...] = r.astype(a_ref.dtype)

            @pl.when(i > 0)
            def _(a_ref=a_ref, r=r):
                a_ref[...] += r.astype(a_ref.dtype)

    res = pl.pallas_call(
        body, name=name, grid=grid,
        in_specs=[pl.BlockSpec(blk, m) for _, blk, m in ins],
        out_specs=[pl.BlockSpec(blk, m) for _, _, blk, m in list(outs) + list(accs)],
        out_shape=[jax.ShapeDtypeStruct(s, d) for s, d, _, _ in list(outs) + list(accs)],
        scratch_shapes=list(scratch),
        compiler_params=_params(("parallel", "arbitrary")),
    )(*[a for a, _, _ in ins])
    return res


def _o(shape, dtype, tr, tc=None):
    if tc is None:
        return (shape, dtype, (tr, shape[1]), lambda j, i: (i, 0))
    return (shape, dtype, (tr, tc), lambda j, i: (i, j))


def _acc(shape, tc=None):
    if tc is None:
        nd = len(shape)
        return (shape, F32, shape, lambda j, i: (0,) * nd)
    return (shape, F32, (shape[0], tc), lambda j, i: (0, j))


def _position():
    x, y, c = lax.axis_index("x"), lax.axis_index("y"), lax.axis_index("c")
    return x, y, c


def _relay_gather_comm(buf):
    _, rows, cols = buf.shape
    rh = rows // 2
    rq = rh // 2

    def plan(out_ref, sems):
        x, y, c = _position()
        own, cx, cy, cd = 2 * x + y, 2 * (1 - x) + y, 2 * x + (1 - y), 2 * (1 - x) + (1 - y)
        to_x, to_y, sibling = (1 - x, y, c), (x, 1 - y, c), (x, y, 1 - c)
        half, other = pl.ds(c * rh, rh), pl.ds((1 - c) * rh, rh)
        q0, q1 = pl.ds(c * rh, rq), pl.ds(c * rh + rq, rq)

        def copy(pair, k, chip_index, rows_, to):
            return pltpu.make_async_remote_copy(
                src_ref=out_ref.at[chip_index, rows_], dst_ref=out_ref.at[chip_index, rows_],
                send_sem=sems[pair].at[k], recv_sem=sems[pair + 1].at[k], device_id=to, device_id_type=MESH)

        return dict(
            own_x=copy(0, 0, own, half, to_x), own_y=copy(0, 1, own, half, to_y),
            in_x=copy(0, 0, cx, half, to_x), in_y=copy(0, 1, cy, half, to_y),
            fwd_xy=copy(0, 2, cx, q0, to_y), fwd_yx=copy(0, 3, cy, q1, to_x),
            in_d0=copy(0, 2, cd, q0, to_y), in_d1=copy(0, 3, cd, q1, to_x),
            d2d_out=[copy(2, j, ch, half, sibling) for j, ch in enumerate((cx, cy, cd))],
            d2d_in=[copy(2, j, ch, other, sibling) for j, ch in enumerate((cx, cy, cd))])

    def start(ins, outs, sems):
        p = plan(outs[0], sems)
        p['own_x'].start()
        p['own_y'].start()

    def mid(ins, outs, sems):
        p = plan(outs[0], sems)
        p['in_x'].wait_recv()
        p['fwd_xy'].start()
        p['d2d_out'][0].start()
        p['in_y'].wait_recv()
        p['fwd_yx'].start()
        p['d2d_out'][1].start()

    def finish(ins, outs, sems):
        p = plan(outs[0], sems)
        p['in_d0'].wait_recv()
        p['in_d1'].wait_recv()
        p['d2d_out'][2].start()
        for cp in p['d2d_in']:
            cp.wait_recv()
        for cp in [p['own_x'], p['own_y'], p['fwd_xy'], p['fwd_yx']] + p['d2d_out']:
            cp.wait_send()

    return _Comm([buf], [jax.ShapeDtypeStruct(buf.shape, buf.dtype)], True, [4, 4, 3, 3], start, finish, mid)


def _gather_comm(buf):
    _, rows, cols = buf.shape
    rh = rows // 2
    if buf.dtype == BF16 and rh % 32 == 0:
        return _relay_gather_comm(buf)

    def plan(out_ref, sems):
        x, y, c = _position()
        chips = [(1 - x, y), (x, 1 - y), (1 - x, 1 - y)]

        def copy(pair, j, chip_index, half, to):
            return pltpu.make_async_remote_copy(
                src_ref=out_ref.at[chip_index, half], dst_ref=out_ref.at[chip_index, half],
                send_sem=sems[pair].at[j], recv_sem=sems[pair + 1].at[j], device_id=to, device_id_type=MESH)

        mine, other = pl.ds(c * rh, rh), pl.ds((1 - c) * rh, rh)
        ici_out = [copy(0, j, 2 * x + y, mine, (px, py, c)) for j, (px, py) in enumerate(chips)]
        ici_in = [copy(0, j, 2 * px + py, mine, (px, py, c)) for j, (px, py) in enumerate(chips)]
        d2d_out = [copy(2, j, 2 * px + py, mine, (x, y, 1 - c)) for j, (px, py) in enumerate(chips)]
        d2d_in = [copy(2, j, 2 * px + py, other, (x, y, 1 - c)) for j, (px, py) in enumerate(chips)]
        return ici_out, ici_in, d2d_out, d2d_in

    def start(ins, outs, sems):
        for cp in plan(outs[0], sems)[0]:
            cp.start()

    def finish(ins, outs, sems):
        ici_out, ici_in, d2d_out, d2d_in = plan(outs[0], sems)
        for arrived, fwd in zip(ici_in, d2d_out):
            arrived.wait_recv()
            fwd.start()
        for cp in d2d_in:
            cp.wait_recv()
        for cp in ici_out + d2d_out:
            cp.wait_send()

    return _Comm([buf], [jax.ShapeDtypeStruct(buf.shape, buf.dtype)], True, [3, 3, 3, 3], start, finish)


def _scatter_comm(p):
    _, rh, cols = p.shape

    def plan(p_ref, land_ref, sems):
        x, y, c = _position()
        chips = [(1 - x, y), (x, 1 - y), (1 - x, 1 - y)]

        def copy(j, chip_index, to):
            return pltpu.make_async_remote_copy(
                src_ref=p_ref.at[chip_index], dst_ref=land_ref.at[j],
                send_sem=sems[0].at[j], recv_sem=sems[1].at[j], device_id=to, device_id_type=MESH)

        out = [copy(j, 2 * px + py, (px, py, c)) for j, (px, py) in enumerate(chips)]
        arrive = [copy(j, 2 * x + y, (x, y, c)) for j in range(3)]
        return out, arrive

    def start(ins, outs, sems):
        for cp in plan(ins[0], outs[0], sems)[0]:
            cp.start()

    def finish(ins, outs, sems):
        out, arrive = plan(ins[0], outs[0], sems)
        for cp in arrive:
            cp.wait_recv()
        for cp in out:
            cp.wait_send()

    return _Comm([p], [jax.ShapeDtypeStruct((3, rh, cols), p.dtype)], False, [3, 3], start, finish)


def _run_comm(name, cm):
    n_in, n_out = len(cm.arrays), len(cm.out_shapes)

    def body(*refs):
        ins, outs, sems = refs[:n_in], refs[n_in:n_in + n_out], refs[n_in + n_out:]
        cm.start(ins, outs, sems)
        if cm.mid:
            cm.mid(ins, outs, sems)
        cm.finish(ins, outs, sems)

    hbm = pl.BlockSpec(memory_space=pltpu.HBM)
    return pl.pallas_call(
        body, name=name, in_specs=[hbm] * n_in, out_specs=[hbm] * n_out, out_shape=list(cm.out_shapes),
        input_output_aliases={k: k for k in range(n_in)} if cm.aliased else {},
        scratch_shapes=[pltpu.SemaphoreType.DMA((n,)) for n in cm.sems],
    )(*cm.arrays)


def _swap_comm(g):
    n, rows, cols = g.shape
    rh = rows // 2

    def plan(g_ref, land_ref, sems):
        x, y, c = _position()

        def copy(k, half):
            return pltpu.make_async_remote_copy(
                src_ref=g_ref.at[k, pl.ds(half * rh, rh)], dst_ref=land_ref.at[k],
                send_sem=sems[0].at[k], recv_sem=sems[1].at[k], device_id=(x, y, 1 - c), device_id_type=MESH)

        return [copy(k, 1 - c) for k in range(n)], [copy(k, c) for k in range(n)]

    def start(ins, outs, sems):
        for cp in plan(ins[0], outs[0], sems)[0]:
            cp.start()

    def finish(ins, outs, sems):
        out, arrive = plan(ins[0], outs[0], sems)
        for cp in arrive:
            cp.wait_recv()
        for cp in out:
            cp.wait_send()

    return _Comm([g], [jax.ShapeDtypeStruct((n, rh, cols), g.dtype)], False, [n, n], start, finish)


def _pair_sum(name, g, land, pos):
    n, rows, cols = g.shape
    rh = rows // 2
    tr = _tile(rh, 256, 16)
    nt = rh // tr

    def body(pos_ref, g_ref, l_ref, o_ref):
        o_ref[...] = (g_ref[...].astype(F32) + l_ref[...].astype(F32)).astype(o_ref.dtype)

    grid_spec = pltpu.PrefetchScalarGridSpec(
        num_scalar_prefetch=1, grid=(n, nt),
        in_specs=[pl.BlockSpec((None, tr, cols), lambda k, i, p: (k, p[1] * nt + i, 0)),
                  pl.BlockSpec((None, tr, cols), lambda k, i, p: (k, i, 0))],
        out_specs=pl.BlockSpec((None, tr, cols), lambda k, i, p: (k, i, 0)))
    return pl.pallas_call(
        body, name=name, grid_spec=grid_spec, out_shape=jax.ShapeDtypeStruct((n, rh, cols), g.dtype),
        compiler_params=_params(("parallel", "parallel")),
    )(pos, g, land)


def _reduce_partials(name, p, land, pos, comm=None):
    _, rh, cols = p.shape
    tr = _tile(rh, 256, 16)
    nt = rh // tr
    n_cin = len(comm.arrays) if comm else 0
    n_cout = len(comm.out_shapes) if comm else 0

    def body(*refs):
        p_ref, l_ref, o_ref = refs[1], refs[2], refs[3 + n_cin]
        comm_args = (refs[3:3 + n_cin], refs[4 + n_cin:4 + n_cin + n_cout], refs[4 + n_cin + n_cout:])
        if comm:
            @pl.when(pl.program_id(0) == 0)
            def _():
                comm.start(*comm_args)

        acc = p_ref[...].astype(F32)
        for k in range(3):
            acc = acc + l_ref[k].astype(F32)
        o_ref[...] = acc
        if comm:
            @pl.when(pl.program_id(0) == nt - 1)
            def _():
                comm.finish(*comm_args)

    hbm = pl.BlockSpec(memory_space=pltpu.HBM)
    grid_spec = pltpu.PrefetchScalarGridSpec(
        num_scalar_prefetch=1, grid=(nt,),
        in_specs=[pl.BlockSpec((None, tr, cols), lambda i, p_: (p_[0], i, 0)),
                  pl.BlockSpec((3, tr, cols), lambda i, p_: (0, i, 0))] + [hbm] * n_cin,
        out_specs=[pl.BlockSpec((None, tr, cols), lambda i, p_: (p_[1], i, 0))] + [hbm] * n_cout,
        scratch_shapes=[pltpu.SemaphoreType.DMA((n,)) for n in comm.sems] if comm else [])
    return pl.pallas_call(
        body, name=name, grid_spec=grid_spec,
        out_shape=[jax.ShapeDtypeStruct((2, rh, cols), F32)] + (list(comm.out_shapes) if comm else []),
        input_output_aliases={3 + k: 1 + k for k in range(n_cin)} if comm and comm.aliased else {},
        compiler_params=_params(("arbitrary",) if comm else ("parallel",)),
    )(pos, p, land, *(comm.arrays if comm else []))


def _exchange_comm(buf):
    def plan(out_ref, sems):
        x, y, c = _position()

        def copy(half):
            return pltpu.make_async_remote_copy(
                src_ref=out_ref.at[half], dst_ref=out_ref.at[half], send_sem=sems[0].at[0], recv_sem=sems[1].at[0],
                device_id=(x, y, 1 - c), device_id_type=MESH)

        return copy(c), copy(1 - c)

    def start(ins, outs, sems):
        plan(outs[0], sems)[0].start()

    def finish(ins, outs, sems):
        out, arrive = plan(outs[0], sems)
        arrive.wait_recv()
        out.wait_send()

    return _Comm([buf], [jax.ShapeDtypeStruct(buf.shape, buf.dtype)], True, [1, 1], start, finish)


def _all_reduce_small(name, buf):
    rows = buf.shape[0]
    p = rows // 8

    def body(buf_ref, out_ref, land_ref, red_ref, s1, r1, s2, r2):
        x, y, c = _position()
        me = 4 * x + 2 * y + c

        def peer(r):
            px = 1 - x if (r >> 2) & 1 else x
            py = 1 - y if (r >> 1) & 1 else y
            pc = 1 - c if r & 1 else c
            return (px, py, pc), 4 * px + 2 * py + pc

        firsts = []
        for r in range(1, 8):
            to, d = peer(r)
            cp = pltpu.make_async_remote_copy(
                src_ref=buf_ref.at[pl.ds(d * p, p)], dst_ref=land_ref.at[r - 1],
                send_sem=s1.at[r - 1], recv_sem=r1.at[r - 1], device_id=to, device_id_type=MESH)
            cp.start()
            firsts.append(cp)
        acc = buf_ref[pl.ds(me * p, p), :]
        for r in range(1, 8):
            firsts[r - 1].wait_recv()
            acc = acc + land_ref[r - 1]
        red_ref[...] = acc
        out_ref[pl.ds(me * p, p), :] = acc
        seconds = []
        for r in range(1, 8):
            to, d = peer(r)
            cp = pltpu.make_async_remote_copy(
                src_ref=red_ref, dst_ref=out_ref.at[pl.ds(me * p, p)],
                send_sem=s2.at[r - 1], recv_sem=r2.at[r - 1], device_id=to, device_id_type=MESH)
            cp.start()
            seconds.append(cp)
        for r in range(1, 8):
            to, d = peer(r)
            pltpu.make_async_remote_copy(
                src_ref=red_ref, dst_ref=out_ref.at[pl.ds(d * p, p)],
                send_sem=s2.at[r - 1], recv_sem=r2.at[r - 1], device_id=to, device_id_type=MESH).wait_recv()
        for cp in firsts + seconds:
            cp.wait_send()

    vmem = pl.BlockSpec(memory_space=pltpu.VMEM)
    return pl.pallas_call(
        body, name=name, in_specs=[vmem], out_specs=vmem,
        out_shape=jax.ShapeDtypeStruct(buf.shape, F32),
        scratch_shapes=[pltpu.VMEM((7, p, LANES), F32), pltpu.VMEM((p, LANES), F32),
                        pltpu.SemaphoreType.DMA((7,)), pltpu.SemaphoreType.DMA((7,)),
                        pltpu.SemaphoreType.DMA((7,)), pltpu.SemaphoreType.DMA((7,))],
        compiler_params=pltpu.CompilerParams(vmem_limit_bytes=VMEM_LIMIT),
    )(buf)


def _cast_pad(name, w, rows_to, cols_to, pos):
    rows, cols = w.shape
    tr = _tile(math.gcd(rows, rows_to), 256, 16)
    assert rows % tr == 0 and rows_to % tr == 0, (rows, rows_to, tr)
    n_src = rows // tr

    def body(pos_ref, w_ref, o_ref):
        i = pl.program_id(0)
        if cols_to > cols:
            o_ref[:, cols:] = jnp.zeros((tr, cols_to - cols), BF16)

        @pl.when(i < n_src)
        def _():
            o_ref[:, :cols] = w_ref[...].astype(BF16)

        if rows_to > rows:
            @pl.when(i >= n_src)
            def _():
                o_ref[:, :cols] = jnp.zeros((tr, cols), BF16)

    grid_spec = pltpu.PrefetchScalarGridSpec(
        num_scalar_prefetch=1, grid=(rows_to // tr,),
        in_specs=[pl.BlockSpec((tr, cols), lambda i, p: (jnp.minimum(i, n_src - 1), 0))],
        out_specs=pl.BlockSpec((None, tr, cols_to), lambda i, p: (p[0], i, 0)))
    return pl.pallas_call(
        body, name=name, grid_spec=grid_spec,
        out_shape=jax.ShapeDtypeStruct((N_CHIPS, rows_to, cols_to), BF16),
        compiler_params=_params(("parallel",)),
    )(pos, w)


def _scan_rows(a, b, row, tr, reverse):
    sh = 1
    while sh < tr:
        if reverse:
            valid = row < tr - sh
            a_s = pltpu.roll(a, tr - sh, 0)
            b_s = pltpu.roll(b, tr - sh, 0)
        else:
            valid = row >= sh
            a_s = pltpu.roll(a, sh, 0)
            b_s = pltpu.roll(b, sh, 0)
        b = b + jnp.where(valid, a * b_s, 0.0)
        a = jnp.where(valid, a * a_s, a)
        sh *= 2
    return a, b


def _linear_scan(name, a, b, tr, reverse=False, shift_a=False):
    t_rows, cols = a.shape
    tc = _tile(cols, 512)
    ni, nj = t_rows // tr, cols // tc
    hb = tr // 8

    def rmap(j, i):
        return ((ni - 1 - i) if reverse else i, j)

    def halo_map(j, i):
        ri = ni - 1 - i
        return (jnp.minimum((ri + 1) * hb, t_rows // 8 - 1), j)

    def body(*refs):
        if shift_a:
            a_ref, halo_ref, b_ref, h_ref, carry_ref = refs
        else:
            a_ref, b_ref, h_ref, carry_ref = refs
        i = pl.program_id(1)
        row = lax.broadcasted_iota(jnp.int32, (tr, tc), 0)

        @pl.when(i == 0)
        def _():
            carry_ref[...] = jnp.zeros_like(carry_ref)

        av = a_ref[...]
        if shift_a:
            nxt = jnp.where(i > 0, halo_ref[0:1, :], 0.0)
            av = jnp.where(row == tr - 1, jnp.broadcast_to(nxt, (tr, tc)), pltpu.roll(av, tr - 1, 0))
        pa, hb_ = _scan_rows(av, b_ref[...], row, tr, reverse)
        h = hb_ + pa * carry_ref[0:1, :]
        h_ref[...] = h
        last = h[0:1, :] if reverse else h[tr - 1:tr, :]
        carry_ref[...] = jnp.broadcast_to(last, carry_ref.shape)

    in_specs = [pl.BlockSpec((tr, tc), rmap)]
    args = [a]
    if shift_a:
        in_specs.append(pl.BlockSpec((8, tc), halo_map))
        args.append(a)
    in_specs.append(pl.BlockSpec((tr, tc), rmap))
    args.append(b)
    return pl.pallas_call(
        body, name=name, grid=(nj, ni), in_specs=in_specs, out_specs=pl.BlockSpec((tr, tc), rmap),
        out_shape=jax.ShapeDtypeStruct((t_rows, cols), F32), scratch_shapes=[pltpu.VMEM((8, tc), F32)],
        compiler_params=_params(("parallel", "arbitrary")),
    )(*args)


def _conv_fwd(name, proj, conv_w, conv_b, width, tr):
    t_rows = proj.shape[0]
    ni = t_rows // tr
    hb = tr // 8
    tc = _tile(width, 512)

    def body(u_ref, halo_ref, w_ref, b_ref, o_ref, ext_ref):
        i = pl.program_id(1)
        ext_ref[0:8, :] = jnp.where(i > 0, halo_ref[...], 0.0)
        ext_ref[8:8 + tr, :] = u_ref[...]
        acc = b_ref[...] + w_ref[CONV_WIDTH - 1:CONV_WIDTH, :] * u_ref[...]
        for k in range(CONV_WIDTH - 1):
            acc = acc + w_ref[k:k + 1, :] * ext_ref[pl.ds(8 - (CONV_WIDTH - 1) + k, tr), :]
        o_ref[...] = acc

    return pl.pallas_call(
        body, name=name, grid=(width // tc, ni),
        in_specs=[pl.BlockSpec((tr, tc), lambda j, i: (i, j)),
                  pl.BlockSpec((8, tc), lambda j, i: (jnp.maximum(i * hb - 1, 0), j)),
                  pl.BlockSpec((CONV_WIDTH, tc), lambda j, i: (0, j)),
                  pl.BlockSpec((1, tc), lambda j, i: (0, j))],
        out_specs=pl.BlockSpec((tr, tc), lambda j, i: (i, j)),
        out_shape=jax.ShapeDtypeStruct((t_rows, width), F32),
        scratch_shapes=[pltpu.VMEM((tr + 8, tc), F32)],
        compiler_params=_params(("parallel", "parallel")),
    )(proj, proj, conv_w, conv_b)


def _conv_bwd(name, dxc, proj, conv_w, width, tr):
    t_rows = dxc.shape[0]
    ni = t_rows // tr
    hb = tr // 8
    tc = _tile(width, 512)

    def body(d_ref, dnext_ref, u_ref, uprev_ref, w_ref, du_ref, dw_ref, db_ref, dext_ref, uext_ref):
        i = pl.program_id(1)
        dext_ref[0:tr, :] = d_ref[...]
        dext_ref[tr:tr + 8, :] = jnp.where(i < ni - 1, dnext_ref[...], 0.0)
        uext_ref[0:8, :] = jnp.where(i > 0, uprev_ref[...], 0.0)
        uext_ref[8:8 + tr, :] = u_ref[...]
        d = d_ref[...]
        du = w_ref[CONV_WIDTH - 1:CONV_WIDTH, :] * d
        dws = []
        for k in range(CONV_WIDTH - 1):
            du = du + w_ref[k:k + 1, :] * dext_ref[pl.ds(CONV_WIDTH - 1 - k, tr), :]
            dws.append(jnp.sum(d * uext_ref[pl.ds(8 - (CONV_WIDTH - 1) + k, tr), :], axis=0, keepdims=True))
        dws.append(jnp.sum(d * u_ref[...], axis=0, keepdims=True))
        du_ref[...] = du
        dw = jnp.concatenate(dws, axis=0)
        db = jnp.sum(d, axis=0, keepdims=True)

        @pl.when(i == 0)
        def _():
            dw_ref[...] = dw
            db_ref[...] = db

        @pl.when(i > 0)
        def _():
            dw_ref[...] += dw
            db_ref[...] += db

    return pl.pallas_call(
        body, name=name, grid=(width // tc, ni),
        in_specs=[pl.BlockSpec((tr, tc), lambda j, i: (i, j)),
                  pl.BlockSpec((8, tc), lambda j, i: (jnp.minimum((i + 1) * hb, t_rows // 8 - 1), j)),
                  pl.BlockSpec((tr, tc), lambda j, i: (i, j)),
                  pl.BlockSpec((8, tc), lambda j, i: (jnp.maximum(i * hb - 1, 0), j)),
                  pl.BlockSpec((CONV_WIDTH, tc), lambda j, i: (0, j))],
        out_specs=[pl.BlockSpec((tr, tc), lambda j, i: (i, j)),
                   pl.BlockSpec((CONV_WIDTH, tc), lambda j, i: (0, j)),
                   pl.BlockSpec((1, tc), lambda j, i: (0, j))],
        out_shape=[jax.ShapeDtypeStruct((t_rows, width), F32), jax.ShapeDtypeStruct((CONV_WIDTH, width), F32),
                   jax.ShapeDtypeStruct((1, width), F32)],
        scratch_shapes=[pltpu.VMEM((tr + 8, tc), F32), pltpu.VMEM((tr + 8, tc), F32)],
        compiler_params=_params(("parallel", "arbitrary")),
    )(dxc, dxc, proj, proj, conv_w)


def _s5_disc_math(lam_re, lam_im, log_dt, b_re, b_im, expand):
    dt = jnp.exp(log_dt)
    zr, zi = lam_re * dt, lam_im * dt
    mag = jnp.exp(zr)
    lbr, lbi = mag * jnp.cos(zi), mag * jnp.sin(zi)
    ar, ai = lbr - 1.0, lbi
    den = lam_re * lam_re + lam_im * lam_im
    cr = (ar * lam_re + ai * lam_im) / den
    ci = (ai * lam_re - ar * lam_im) / den
    cre = jnp.dot(cr, expand, precision=lax.Precision.HIGHEST, preferred_element_type=F32)
    cie = jnp.dot(ci, expand, precision=lax.Precision.HIGHEST, preferred_element_type=F32)
    return lbr, lbi, cre * b_re - cie * b_im, cre * b_im + cie * b_re


def _s5_disc(name, lam_re, lam_im, log_dt, b_re, b_im, expand, cots=None):
    ins = [lam_re, lam_im, log_dt, b_re, b_im, expand]
    n_in = len(ins) + (len(cots) if cots else 0)

    def body(*refs):
        vals = [r[...] for r in refs[:6]]
        outs = refs[n_in:]
        if cots is None:
            res = _s5_disc_math(*vals)
        else:
            cv = tuple(r[...] for r in refs[6:n_in])
            _, vjp = jax.vjp(lambda a, b, c, d, e: _s5_disc_math(a, b, c, d, e, vals[5]), *vals[:5])
            res = vjp(cv)
        for o, r in zip(outs, res):
            o[...] = r

    if cots is None:
        shapes = [lam_re.shape, lam_re.shape, b_re.shape, b_re.shape]
    else:
        shapes = [lam_re.shape, lam_re.shape, log_dt.shape, b_re.shape, b_re.shape]
    vmem = pl.BlockSpec(memory_space=pltpu.VMEM)
    return pl.pallas_call(
        body, name=name, in_specs=[vmem] * n_in, out_specs=[vmem] * len(shapes),
        out_shape=[jax.ShapeDtypeStruct(s, F32) for s in shapes],
        compiler_params=pltpu.CompilerParams(vmem_limit_bytes=VMEM_LIMIT),
    )(*ins, *(cots or ()))


def _cmul(ar, ai, br, bi):
    return ar * br - ai * bi, ar * bi + ai * br


SCAN_BLOCK = 8


def _block_scan(br, bi, lr, li, row, rows, reverse):
    pr, pi = lr, li
    pos = row & (SCAN_BLOCK - 1)
    sh = 1
    while sh < SCAN_BLOCK:
        if reverse:
            valid = pos < SCAN_BLOCK - sh
            rs, is_ = pltpu.roll(br, rows - sh, 0), pltpu.roll(bi, rows - sh, 0)
        else:
            valid = pos >= sh
            rs, is_ = pltpu.roll(br, sh, 0), pltpu.roll(bi, sh, 0)
        mr, mi = _cmul(pr, pi, rs, is_)
        br = br + jnp.where(valid, mr, 0.0)
        bi = bi + jnp.where(valid, mi, 0.0)
        pr, pi = _cmul(pr, pi, pr, pi)
        sh *= 2
    return br, bi


def _block_powers(lr, li, hw, reverse):
    row = lax.broadcasted_iota(jnp.int32, (SCAN_BLOCK, hw), 0)
    d = jnp.where(row == (SCAN_BLOCK - 1 if reverse else 0), 1.0, 0.0)
    return _block_scan(jnp.broadcast_to(lr, (SCAN_BLOCK, hw)) * d, jnp.broadcast_to(li, (SCAN_BLOCK, hw)) * d,
                       lr, li, row, SCAN_BLOCK, reverse)


def _chain_blocks(s_ref, pow_ref, carry, rows, hw, reverse):
    nblk = rows // SCAN_BLOCK
    pr, pi = pow_ref[:, :hw], pow_ref[:, hw:]

    def step(n, c):
        b = (nblk - 1 - n) if reverse else n
        at = pl.ds(pl.multiple_of(b * SCAN_BLOCK, SCAN_BLOCK), SCAN_BLOCK)
        blk = s_ref[at, :]
        kr, ki = _cmul(pr, pi, c[0], c[1])
        sr, si = blk[:, :hw] + kr, blk[:, hw:] + ki
        s_ref[at, :] = jnp.concatenate([sr, si], axis=1)
        edge = slice(0, 1) if reverse else slice(SCAN_BLOCK - 1, SCAN_BLOCK)
        return sr[edge, :], si[edge, :]

    return lax.fori_loop(0, nblk, step, carry)


def _s5_fwd(name, proj, col0, bblk, cblk, lamblk, dvec, width, tr):
    t_rows = proj.shape[0]
    ni = t_rows // tr
    nb = width // LANES
    sw = bblk.shape[-1]
    hw = sw // 2

    def body(u_ref, b_ref, c_ref, lam_ref, d_ref, y_ref, s_ref, carry_ref, pow_ref, st_ref):
        i = pl.program_id(1)
        row = lax.broadcasted_iota(jnp.int32, (tr, hw), 0)
        lam = lam_ref[...]
        lr, li = lam[:, :hw], lam[:, hw:]

        @pl.when(i == 0)
        def _():
            carry_ref[...] = jnp.zeros_like(carry_ref)
            zr, zi = _block_powers(lr, li, hw, False)
            pow_ref[:, :hw] = zr
            pow_ref[:, hw:] = zi

        u = u_ref[...]
        bu = jnp.dot(u.astype(BF16), b_ref[...], preferred_element_type=F32)
        sr, si = _block_scan(bu[:, :hw], bu[:, hw:], lr, li, row, tr, False)
        st_ref[:, :hw] = sr
        st_ref[:, hw:] = si
        cr, ci = _chain_blocks(st_ref, pow_ref, (carry_ref[0:1, :hw], carry_ref[0:1, hw:]), tr, hw, False)
        carry_ref[...] = jnp.broadcast_to(jnp.concatenate([cr, ci], axis=1), carry_ref.shape)
        s16 = st_ref[...].astype(BF16)
        s_ref[...] = s16
        y_ref[...] = jnp.dot(s16, c_ref[...], preferred_element_type=F32) + d_ref[...] * u

    return pl.pallas_call(
        body, name=name, grid=(nb, ni),
        in_specs=[pl.BlockSpec((tr, LANES), lambda k, i: (i, col0 + k)),
                  pl.BlockSpec((None, LANES, sw), lambda k, i: (k, 0, 0)),
                  pl.BlockSpec((None, sw, LANES), lambda k, i: (k, 0, 0)),
                  pl.BlockSpec((None, 1, sw), lambda k, i: (k, 0, 0)),
                  pl.BlockSpec((1, LANES), lambda k, i: (0, k))],
        out_specs=[pl.BlockSpec((tr, LANES), lambda k, i: (i, k)),
                   pl.BlockSpec((tr, sw), lambda k, i: (i, k))],
        out_shape=[jax.ShapeDtypeStruct((t_rows, width), F32), jax.ShapeDtypeStruct((t_rows, nb * sw), BF16)],
        scratch_shapes=[pltpu.VMEM((8, sw), F32), pltpu.VMEM((SCAN_BLOCK, sw), F32), pltpu.VMEM((tr, sw), F32)],
        compiler_params=_params(("parallel", "arbitrary")),
    )(proj, bblk, cblk, lamblk, dvec)


def _s5_bwd(name, dy, proj, col0, states, bblk, cblk, lamblk, dvec, width, tr):
    t_rows = dy.shape[0]
    ni = t_rows // tr
    nb = width // LANES
    sw = bblk.shape[-1]
    hw = sw // 2
    hb16 = tr // 16

    def rmap(k, i):
        return (ni - 1 - i, k)

    def body(dy_ref, u_ref, s_ref, sprev_ref, b_ref, c_ref, lam_ref, d_ref,
             du_ref, db_ref, dc_ref, dlam_ref, dd_ref, carry_ref, pow_ref, gt_ref):
        i = pl.program_id(1)
        row = lax.broadcasted_iota(jnp.int32, (tr, hw), 0)
        row_w = lax.broadcasted_iota(jnp.int32, (tr, sw), 0)
        lam = lam_ref[...]
        lr, li = lam[:, :hw], -lam[:, hw:]

        @pl.when(i == 0)
        def _():
            carry_ref[...] = jnp.zeros_like(carry_ref)
            zr, zi = _block_powers(lr, li, hw, True)
            pow_ref[:, :hw] = zr
            pow_ref[:, hw:] = zi

        dyv = dy_ref[...]
        dy16 = dyv.astype(BF16)
        u = u_ref[...]
        gd = lax.dot_general(dy16, c_ref[...], _DN['nt'], preferred_element_type=F32)
        gr, gi = _block_scan(gd[:, :hw], gd[:, hw:], lr, li, row, tr, True)
        gt_ref[:, :hw] = gr
        gt_ref[:, hw:] = gi
        cr, ci = _chain_blocks(gt_ref, pow_ref, (carry_ref[0:1, :hw], carry_ref[0:1, hw:]), tr, hw, True)
        carry_ref[...] = jnp.broadcast_to(jnp.concatenate([cr, ci], axis=1), carry_ref.shape)
        g = gt_ref[...]
        gr, gi = g[:, :hw], g[:, hw:]
        g16 = g.astype(BF16)
        du_ref[...] = lax.dot_general(g16, b_ref[...], _DN['nt'], preferred_element_type=F32) + d_ref[...] * dyv
        s16 = s_ref[...]
        dbv = lax.dot_general(g16, u.astype(BF16), _DN['tn'], preferred_element_type=F32)
        dcv = lax.dot_general(s16, dy16, _DN['tn'], preferred_element_type=F32)
        s32 = s16.astype(F32)
        first = jnp.where(i < ni - 1, sprev_ref[15:16, :].astype(F32), 0.0)
        sp = jnp.where(row_w == 0, jnp.broadcast_to(first, (tr, sw)), pltpu.roll(s32, 1, 0))
        spr, spi = sp[:, :hw], sp[:, hw:]
        dlr = jnp.sum(gr * spr + gi * spi, axis=0, keepdims=True)
        dli = jnp.sum(gi * spr - gr * spi, axis=0, keepdims=True)
        dlam = jnp.concatenate([dlr, dli], axis=1)
        ddv = jnp.sum(dyv * u, axis=0, keepdims=True)

        @pl.when(i == 0)
        def _():
            db_ref[...] = dbv
            dc_ref[...] = dcv
            dlam_ref[...] = dlam
            dd_ref[...] = ddv

        @pl.when(i > 0)
        def _():
            db_ref[...] += dbv
            dc_ref[...] += dcv
            dlam_ref[...] += dlam
            dd_ref[...] += ddv

    return pl.pallas_call(
        body, name=name, grid=(nb, ni),
        in_specs=[pl.BlockSpec((tr, LANES), rmap),
                  pl.BlockSpec((tr, LANES), lambda k, i: (ni - 1 - i, col0 + k)),
                  pl.BlockSpec((tr, sw), rmap),
                  pl.BlockSpec((16, sw), lambda k, i: (jnp.maximum((ni - 1 - i) * hb16 - 1, 0), k)),
                  pl.BlockSpec((None, LANES, sw), lambda k, i: (k, 0, 0)),
                  pl.BlockSpec((None, sw, LANES), lambda k, i: (k, 0, 0)),
                  pl.BlockSpec((None, 1, sw), lambda k, i: (k, 0, 0)),
                  pl.BlockSpec((1, LANES), lambda k, i: (0, k))],
        out_specs=[pl.BlockSpec((tr, LANES), rmap),
                   pl.BlockSpec((None, sw, LANES), lambda k, i: (k, 0, 0)),
                   pl.BlockSpec((None, sw, LANES), lambda k, i: (k, 0, 0)),
                   pl.BlockSpec((None, 1, sw), lambda k, i: (k, 0, 0)),
                   pl.BlockSpec((1, LANES), lambda k, i: (0, k))],
        out_shape=[jax.ShapeDtypeStruct((t_rows, width), F32), jax.ShapeDtypeStruct((nb, sw, LANES), F32),
                   jax.ShapeDtypeStruct((nb, sw, LANES), F32), jax.ShapeDtypeStruct((nb, 1, sw), F32),
                   jax.ShapeDtypeStruct((1, width), F32)],
        scratch_shapes=[pltpu.VMEM((8, sw), F32), pltpu.VMEM((SCAN_BLOCK, sw), F32), pltpu.VMEM((tr, sw), F32)],
        compiler_params=_params(("parallel", "arbitrary")),
    )(dy, proj, states, states, bblk, cblk, lamblk, dvec)


def _blockdiag(m):
    nb, g, p, q = m.shape
    eye = jnp.eye(g, dtype=m.dtype)
    return (m[:, :, :, None, :] * eye[None, :, None, :, None]).reshape(nb, g * p, g * q)


def _blockdiag_take(m, p, q):
    nb = m.shape[0]
    g = GROUPS_PER_BLOCK
    m = m.reshape(nb, g, p, g, q)
    return jnp.stack([m[:, k, :, k, :] for k in range(g)], axis=1)


def _adamw(name, w, g, m, v, tr):
    rows, cols = w.shape
    gcols = g.shape[1]
    c1 = 1.0 - ADAM_B1 ** ADAM_STEP
    c2 = 1.0 - ADAM_B2 ** ADAM_STEP

    def body(w_ref, g_ref, m_ref, v_ref, go_ref, d_ref, mo_ref, vo_ref):
        gv = g_ref[...] if gcols == cols else g_ref[:, :cols]
        mn = ADAM_B1 * m_ref[...] + (1.0 - ADAM_B1) * gv
        vn = ADAM_B2 * v_ref[...] + (1.0 - ADAM_B2) * (gv * gv)
        go_ref[...] = gv
        mo_ref[...] = mn
        vo_ref[...] = vn
        d_ref[...] = -ADAM_LR * ((mn / c1) / (jnp.sqrt(vn / c2) + ADAM_EPS) + ADAM_WD * w_ref[...])

    spec = pl.BlockSpec((tr, cols), lambda i: (i, 0))
    return pl.pallas_call(
        body, name=name, grid=(rows // tr,),
        in_specs=[spec, pl.BlockSpec((tr, gcols), lambda i: (i, 0)), spec, spec],
        out_specs=[spec] * 4, out_shape=[jax.ShapeDtypeStruct((rows, cols), F32)] * 4,
        compiler_params=_params(("parallel",)),
    )(w, g, m, v)


def _ffn_fwd(tag, h, norm, wg, wu, wd, dims, host=None):
    t_rows, d, fp, tm, tr, trn = dims['T'], dims['D'], dims['FP'], dims['TM'], dims['TR'], dims['TRN']
    host = host or {}
    got = {}
    ft = N_CHIPS * fp
    tn = _tile(fp, 1408)
    npb = fp // tn
    n16, = _rowmap(f"{tag}_norm", lambda x, g: _rms(x, g).astype(BF16), (1, t_rows // trn),
                   [_rows(h, trn), _full(norm)], [_o((t_rows, d), BF16, trn)])

    def up(nm, w):
        res = _matmul(f"{tag}_{nm}", 'nt', n16, (tm // 2, d // 2), lambda i, j, k: (i, k),
                      w, (None, fp, d // 2), lambda i, j, k: (j, 0, k),
                      (2 * t_rows // tm, N_CHIPS, 2), [((t_rows, ft), BF16, (tm // 2, fp), lambda i, j: (i, j))],
                      lambda acc: acc, comms=host.get(nm, ()))
        got[nm] = res[1:]
        return res[0]

    gate = up("gate", wg)
    wu = got['gate'][wu] if isinstance(wu, int) else wu
    upv = up("up", wu)
    wd = got['up'][wd] if isinstance(wd, int) else wd
    tc = _tile(ft, 1408)
    act, = _rowmap(f"{tag}_act", lambda g, u: _silu(g.astype(F32)) * u.astype(F32), (ft // tc, t_rows // tr),
                   [_rows(gate, tr, tc), _rows(upv, tr, tc)], [_o((t_rows, ft), BF16, tr, tc)])
    tnd = _tile(d, 1024)
    res = _matmul(f"{tag}_down", 'nn', act, (tm, fp), lambda i, j, k: (i, k),
                  wd, (None, fp, tnd), lambda i, j, k: (k, 0, j),
                  (t_rows // tm, d // tnd, N_CHIPS), [((t_rows, d), F32, (tm, tnd), lambda i, j: (i, j))],
                  lambda acc, hin: hin + 0.5 * acc, extras=[(h, (tm, tnd), lambda i, j: (i, j))],
                  comms=host.get('down', ()))
    got['down'] = res[1:]
    return res[0], dict(h=h, n16=n16, gate=gate, up=upv, act=act, wu=wu, wd=wd), got


def _pair(nm, g, pos):
    return _pair_sum(f"pair_{nm}", g, _run_comm(f"swap_{nm}", _swap_comm(g))[0], pos)


def _ffn_bwd(tag, names, dh, dh16, saved, norm, wg, dims, pos, host=None):
    t_rows, d, fp, tm, trn = dims['T'], dims['D'], dims['FP'], dims['TM'], dims['TRN']
    host = host or {}
    got = {}
    n_gate, n_up, n_down = names
    wu, wd = saved['wu'], saved['wd']
    ft = N_CHIPS * fp
    tn = _tile(fp, 1408)
    npb = fp // tn
    tm2 = tm // 2

    def act_bwd(acc, g, u):
        da = 0.5 * acc
        g32, u32 = g.astype(F32), u.astype(F32)
        sg = _sigmoid(g32)
        return da * u32 * (sg * (1.0 + g32 * (1.0 - sg))), da * (g32 * sg)

    res = _matmul(
        f"{tag}_dact", 'nt', dh16, (tm2, d // 2), lambda i, j, k: (i, k),
        wd, (None, tn, d // 2), lambda i, j, k: (j // npb, j % npb, k),
        (t_rows // tm2, ft // tn, 2),
        [((t_rows, ft), BF16, (tm2, tn), lambda i, j: (i, j))] * 2, act_bwd,
        extras=[(saved['gate'], (tm2, tn), lambda i, j: (i, j)), (saved['up'], (tm2, tn), lambda i, j: (i, j))],
        comms=host.get('dact', ()))
    dgate, dup, got['dact'] = res[0], res[1], res[2:]
    tnd = _tile(d, 1024)
    tw = _tile(d, 512)
    res = _matmul(f"{tag}_dwd", 'tn', saved['act'], (t_rows, tn), lambda i, j, k: (0, i),
                  dh16, (t_rows, tw), lambda i, j, k: (0, j),
                  (ft // tn, d // tw, 1),
                  [((N_CHIPS, fp, d), BF16, (None, tn, tw), lambda i, j: (i // npb, i % npb, j))],
                  lambda acc: 0.5 * acc, comms=host.get('dwd', ()))
    dwd, got['dwd'] = res[0], res[1:]

    def dw_up(nm, dact, comms):
        return _matmul(f"{tag}_{nm}", 'tn', dact, (t_rows, tn), lambda i, j, k: (0, i),
                       saved['n16'], (t_rows, tw), lambda i, j, k: (0, j),
                       (ft // tn, d // tw, 1),
                       [((N_CHIPS, fp, d), BF16, (None, tn, tw), lambda i, j: (i // npb, i % npb, j))],
                       lambda acc: acc, comms=comms)

    def dn_part(nm, dact, w, prev, comms):
        extras = [] if prev is None else [(prev, (tm, tnd), lambda i, j: (i, j))]
        return _matmul(f"{tag}_{nm}", 'nn', dact, (tm, fp), lambda i, j, k: (i, k),
                       w, (None, fp, tnd), lambda i, j, k: (k, 0, j),
                       (t_rows // tm, d // tnd, N_CHIPS), [((t_rows, d), F32, (tm, tnd), lambda i, j: (i, j))],
                       (lambda acc: acc) if prev is None else (lambda acc, p: acc + p), extras=extras, comms=comms)

    dwg, swap_d = dw_up("dwg", dgate, [_swap_comm(dwd)])
    pair_d = _pair_sum(f"pair_{n_down}", dwd, swap_d, pos)
    dwu, land_d, swap_g = dw_up("dwu", dup, [_scatter_comm(pair_d), _swap_comm(dwg)])
    pair_g = _pair_sum(f"pair_{n_gate}", dwg, swap_g, pos)
    dn_g, land_g, swap_u = dn_part("dn_gate", dgate, wg, None, [_scatter_comm(pair_g), _swap_comm(dwu)])
    pair_u = _pair_sum(f"pair_{n_up}", dwu, swap_u, pos)
    dn, land_u = dn_part("dn_up", dup, wu, dn_g, [_scatter_comm(pair_u)])

    def norm_bwd(x, dnv, dhv, g):
        _, vjp = jax.vjp(_rms, x, g)
        dx, dg = vjp(dnv)
        res = dhv + dx
        return res, res.astype(BF16), dg

    dh_in, dh_in16, dnorm = _rowmap(
        f"{tag}_dnorm", norm_bwd, (1, t_rows // trn),
        [_rows(saved['h'], trn), _rows(dn, trn), _rows(dh, trn), _full(norm)],
        [_o((t_rows, d), F32, trn), _o((t_rows, d), BF16, trn)], [_acc((1, d))])
    parts = {n_down: (pair_d, land_d), n_gate: (pair_g, land_g), n_up: (pair_u, land_u)}
    return dh_in, dh_in16, dnorm, parts, got


def kernel(x, meta_tokens, ffn1_norm, ffn1_w_gate, ffn1_w_up, ffn1_w_down, mix_norm, w_in, rg_conv_w, rg_conv_b, rg_w_a, rg_b_a, rg_w_x, rg_b_x, rg_lambda, s5_lambda_re, s5_lambda_im, s5_log_dt, s5_b_re, s5_b_im, s5_c_re, s5_c_im, s5_d, s5_glu_w, s5_glu_b, rg_out_norm, s5_out_norm, w_out, ffn2_norm, ffn2_w_gate, ffn2_w_up, ffn2_w_down, final_norm, loss_target, m_meta_tokens, m_ffn1_norm, m_ffn1_w_gate, m_ffn1_w_up, m_ffn1_w_down, m_mix_norm, m_w_in, m_rg_conv_w, m_rg_conv_b, m_rg_w_a, m_rg_b_a, m_rg_w_x, m_rg_b_x, m_rg_lambda, m_s5_lambda_re, m_s5_lambda_im, m_s5_log_dt, m_s5_b_re, m_s5_b_im, m_s5_c_re, m_s5_c_im, m_s5_d, m_s5_glu_w, m_s5_glu_b, m_rg_out_norm, m_s5_out_norm, m_w_out, m_ffn2_norm, m_ffn2_w_gate, m_ffn2_w_up, m_ffn2_w_down, m_final_norm, v_meta_tokens, v_ffn1_norm, v_ffn1_w_gate, v_ffn1_w_up, v_ffn1_w_down, v_mix_norm, v_w_in, v_rg_conv_w, v_rg_conv_b, v_rg_w_a, v_rg_b_a, v_rg_w_x, v_rg_b_x, v_rg_lambda, v_s5_lambda_re, v_s5_lambda_im, v_s5_log_dt, v_s5_b_re, v_s5_b_im, v_s5_c_re, v_s5_c_im, v_s5_d, v_s5_glu_w, v_s5_glu_b, v_rg_out_norm, v_s5_out_norm, v_w_out, v_ffn2_norm, v_ffn2_w_gate, v_ffn2_w_up, v_ffn2_w_down, v_final_norm):
    args = locals()
    w = {n: args[n] for n in WEIGHTS}
    mom = {n: args["m_" + n] for n in WEIGHTS}
    var = {n: args["v_" + n] for n in WEIGHTS}

    seq, d = x.shape[1], x.shape[2]
    f_shard = ffn1_w_gate.shape[2]
    fp = _round_up(f_shard, LANES)
    r = rg_conv_b.shape[1]
    s5w = s5_d.shape[1]
    hd = r // RG_HEADS
    groups = s5w // S5_GROUP
    nb = groups // GROUPS_PER_BLOCK
    t_rows = _round_up(N_META + seq, LANES)
    tm, tr = t_rows // 4, t_rows // 8
    trn = _tile(t_rows, max(t_rows // 32, 16), 16)
    dims = dict(T=t_rows, D=d, FP=fp, TM=tm, TR=tr, TRN=trn)
    cx, cy, cc = lax.axis_index("x"), lax.axis_index("y"), lax.axis_index("c")
    chip = 2 * cx + cy
    pos = jnp.stack([chip, cc]).astype(jnp.int32)

    def gather(nm, buf):
        return _run_comm(f"ag_{nm}", _gather_comm(buf))[0]

    def gather_f32(nm, local):
        buf = lax.dynamic_update_slice(jnp.zeros((N_CHIPS,) + local.shape, F32), local[None], (chip, 0, 0))
        return gather(nm, buf)

    slot = {}
    transposed = ('ffn1_w_gate', 'ffn1_w_up', 'ffn2_w_gate', 'ffn2_w_up')
    local2d = lambda tree, nm: jnp.swapaxes(tree[nm][0], 0, 1) if nm in transposed else tree[nm][0]
    for nm in transposed:
        slot[nm] = _cast_pad(f"cast_{nm}", local2d(w, nm), fp, d, pos)
    for nm in ('ffn1_w_down', 'ffn2_w_down'):
        slot[nm] = _cast_pad(f"cast_{nm}", w[nm][0], fp, d, pos)
    slot['w_in'] = _cast_pad("cast_w_in", w_in[0], d, w_in.shape[2], pos)
    slot['w_out'] = _cast_pad("cast_w_out", w_out[0], w_out.shape[1], d, pos)
    slot['s5_glu_w'] = _cast_pad("cast_glu", s5_glu_w[0], s5_glu_w.shape[1], s5w, pos)
    rgw_local = jnp.concatenate([rg_w_a[0], rg_w_x[0]], axis=0).reshape(2 * RG_HEADS * (hd // N_CHIPS), hd)
    rgw = gather('rgw', _cast_pad("cast_rgw", rgw_local, rgw_local.shape[0], hd, pos))
    rgw = rgw.reshape(N_CHIPS, 2, RG_HEADS, hd // N_CHIPS, hd).transpose(1, 2, 0, 3, 4).reshape(2, RG_HEADS, hd, hd)
    w_a16, w_x16 = rgw[0], rgw[1]
    meta_full = gather_f32('meta', meta_tokens).transpose(1, 0, 2).reshape(N_META, d)
    conv_w2, b_a2, b_x2 = _gather_small_shards(rg_conv_w[0], rg_b_a[0], rg_b_x[0], gather_f32)
    wfull = {'ffn1_w_gate': gather('ffn1_w_gate', slot['ffn1_w_gate'])}

    pad_rows = t_rows - N_META - seq
    h0 = jnp.concatenate([meta_full, x[0], jnp.zeros((pad_rows, d), F32)], axis=0)
    tgt = jnp.concatenate([jnp.zeros((N_META, d), F32), loss_target[0], jnp.zeros((pad_rows, d), F32)], axis=0)

    h1, sv1, got = _ffn_fwd(
        "ffn1", h0, ffn1_norm, wfull['ffn1_w_gate'], 0, 0, dims,
        host={'gate': [_gather_comm(slot['ffn1_w_up']), _gather_comm(slot['s5_glu_w'])],
              'up': [_gather_comm(slot['ffn1_w_down']), _gather_comm(slot['w_in'])],
              'down': [_gather_comm(slot['ffn2_w_gate']), _gather_comm(slot['w_out'])]})
    wfull['s5_glu_w'] = got['gate'][1].reshape(s5w, s5w)
    wfull['w_in'] = got['up'][1]
    wfull['ffn2_w_gate'] = got['down'][0]
    wfull['w_out'] = got['down'][1].reshape(d, d)

    n2, = _rowmap("mix_norm", lambda xv, g: _rms(xv, g).astype(BF16), (1, t_rows // trn),
                  [_rows(h1, trn), _full(mix_norm)], [_o((t_rows, d), BF16, trn)])
    pw = w_in.shape[2]
    tnp = _tile(pw, 1536)
    nppb = pw // tnp
    proj, = _matmul(
        "in_proj", 'nn', n2, (tm, d // 2), lambda i, j, k: (i, k),
        wfull['w_in'], (None, d // 2, tnp), lambda i, j, k: (j // nppb, k, j % nppb),
        (t_rows // tm, N_CHIPS * nppb, 2), [((t_rows, 3 * r), F32, (tm, tnp), lambda i, j: (i, j))],
        lambda acc: acc)

    xc = _conv_fwd("rg_conv", proj, conv_w2, rg_conv_b, r, tr)

    def head_cols(arr):
        return (arr, (tr, hd), lambda j, i: (i, j))

    def head_w(arr):
        return (arr, (None, hd, hd), lambda j, i: (j, 0, 0))

    def head_vec(arr):
        return (arr, (1, hd), lambda j, i: (0, j))

    def gates_fwd(xcv, wa, wx, ba, bx, lam):
        x16 = xcv.astype(BF16)
        pre_r = jnp.dot(x16, wa, preferred_element_type=F32) + ba
        pre_i = jnp.dot(x16, wx, preferred_element_type=F32) + bx
        return _rg_gate_math(xcv, pre_r, pre_i, _softplus_neg(lam))

    gate_params = [head_w(w_a16), head_w(w_x16), head_vec(b_a2), head_vec(b_x2), head_vec(rg_lambda)]
    head_out = ((t_rows, r), F32, (tr, hd), lambda j, i: (i, j))
    a_dec, bxv = _rowmap("rg_gates", gates_fwd, (RG_HEADS, t_rows // tr), [head_cols(xc)] + gate_params,
                         [head_out, head_out])
    h_rg = _linear_scan("rg_scan", a_dec, bxv, tr)

    expand = jnp.repeat(jnp.eye(S5_STATE, dtype=F32), S5_GROUP, axis=1)
    b_re2 = s5_b_re[0].reshape(groups, S5_STATE * S5_GROUP)
    b_im2 = s5_b_im[0].reshape(groups, S5_STATE * S5_GROUP)
    log_dt2 = s5_log_dt[0].reshape(groups, 1)
    lbr, lbi, bbr, bbi = _s5_disc("s5_disc", s5_lambda_re[0], s5_lambda_im[0], log_dt2, b_re2, b_im2, expand)

    def to_bblk(m):
        return _blockdiag(m.reshape(nb, GROUPS_PER_BLOCK, S5_STATE, S5_GROUP).transpose(0, 1, 3, 2))

    def to_cblk(m):
        return _blockdiag(m.reshape(nb, GROUPS_PER_BLOCK, S5_GROUP, S5_STATE).transpose(0, 1, 3, 2))

    bblk = jnp.concatenate([to_bblk(bbr), to_bblk(bbi)], axis=-1).astype(BF16)
    cblk = jnp.concatenate([to_cblk(s5_c_re[0]), -to_cblk(s5_c_im[0])], axis=-2).astype(BF16)
    hw = GROUPS_PER_BLOCK * S5_STATE
    lamblk = jnp.concatenate([lbr.reshape(nb, 1, hw), lbi.reshape(nb, 1, hw)], axis=-1)
    col0 = 2 * r // LANES
    y_pre, states = _s5_fwd("s5_fwd", proj, col0, bblk, cblk, lamblk, s5_d, s5w, tr)

    z16, = _rowmap("s5_gelu", lambda yv: _gelu(yv).astype(BF16), (1, t_rows // trn),
                   [_rows(y_pre, trn)], [_o((t_rows, s5w), BF16, trn)])
    tns = _tile(s5w, 1024)

    def glu_fwd(acc, yv, b):
        gl = acc + b
        return _gelu(yv) * _sigmoid(gl), gl

    y_s5, gl = _matmul("s5_glu", 'nn', z16, (tm, s5w), lambda i, j, k: (i, 0),
                       wfull['s5_glu_w'], (s5w, tns), lambda i, j, k: (0, j),
                       (t_rows // tm, s5w // tns, 1), [((t_rows, s5w), F32, (tm, tns), lambda i, j: (i, j))] * 2,
                       glu_fwd, extras=[(y_pre, (tm, tns), lambda i, j: (i, j)), (s5_glu_b, (1, tns), lambda i, j: (0, j))])

    def mix_out(hv, gv, yv, g1, g2):
        return jnp.concatenate([_rms(hv * _gelu(gv), g1), _rms(yv, g2)], axis=1)

    gate_in = (proj, (trn, r), lambda j, i: (i, 1))
    ycat16, = _rowmap("mix_out", lambda *a: mix_out(*a).astype(BF16), (1, t_rows // trn),
                      [_rows(h_rg, trn), gate_in, _rows(y_s5, trn), _full(rg_out_norm), _full(s5_out_norm)],
                      [_o((t_rows, r + s5w), BF16, trn)])
    tnd = _tile(d, 1024)
    h2, = _matmul(
        "out_proj", 'nn', ycat16, (tm, (r + s5w) // 2), lambda i, j, k: (i, k),
        wfull['w_out'], ((r + s5w) // 2, tnd), lambda i, j, k: (k, j),
        (t_rows // tm, d // tnd, 2), [((t_rows, d), F32, (tm, tnd), lambda i, j: (i, j))],
        lambda acc, hin: hin + acc, extras=[(h1, (tm, tnd), lambda i, j: (i, j))])

    h3, sv2, _ = _ffn_fwd(
        "ffn2", h2, ffn2_norm, wfull['ffn2_w_gate'], 0, 0, dims,
        host={'gate': [_gather_comm(slot['ffn2_w_up'])], 'up': [_gather_comm(slot['ffn2_w_down'])]})

    fnorm2 = final_norm.reshape(1, d)

    def head(xv, tv, g):
        i = pl.program_id(1)
        rowi = lax.broadcasted_iota(jnp.int32, (trn, 1), 0) + i * trn
        mask = jnp.where((rowi >= N_META) & (rowi < N_META + seq), 1.0, 0.0)
        out, vjp = jax.vjp(_rms, xv, g)
        err = (out - tv) * mask
        dx, dg = vjp(err * (1.0 / d))
        return dx, dx.astype(BF16), jnp.sum(err * err, axis=0, keepdims=True), dg

    dh3, dh3_16, loss_cols, d_final = _rowmap(
        "loss_head", head, (1, t_rows // trn), [_rows(h3, trn), _rows(tgt, trn), _full(fnorm2)],
        [_o((t_rows, d), F32, trn), _o((t_rows, d), BF16, trn)], [_acc((1, d)), _acc((1, d))])
    loss = lax.psum(0.5 * jnp.sum(loss_cols) / d, ("x", "y", "c"))

    small = {}
    dh2, dh2_16, small['ffn2_norm'], parts, _ = _ffn_bwd(
        "ffn2b", ('ffn2_w_gate', 'ffn2_w_up', 'ffn2_w_down'), dh3, dh3_16, sv2, ffn2_norm, wfull['ffn2_w_gate'],
        dims, pos)

    dycat, = _matmul("d_out_proj", 'nt', dh2_16, (tm, d // 2), lambda i, j, k: (i, k),
                     wfull['w_out'], (tnd, d // 2), lambda i, j, k: (j, k),
                     (t_rows // tm, (r + s5w) // tnd, 2), [((t_rows, r + s5w), F32, (tm, tnd), lambda i, j: (i, j))],
                     lambda acc: acc)
    wo_rows = (r + s5w) // N_CHIPS
    tno = _tile(wo_rows, 1024)
    npo = wo_rows // tno
    tw = _tile(d, 512)
    dw_out, = _matmul("dw_out", 'tn', ycat16, (t_rows, tno), lambda i, j, k: (0, i),
                      dh2_16, (t_rows, tw), lambda i, j, k: (0, j),
                      ((r + s5w) // tno, d // tw, 1),
                      [((N_CHIPS, wo_rows, d), BF16, (None, tno, tw), lambda i, j: (i // npo, i % npo, j))],
                      lambda acc: acc)
    pair_w_out = _pair('w_out', dw_out, pos)

    def mix_out_bwd(hv, gv, yv, dyc, g1, g2):
        _, vjp = jax.vjp(mix_out, hv, gv, yv, g1, g2)
        return vjp(dyc)

    dh_out, dgate_rg, dy_s5, small['rg_out_norm'], small['s5_out_norm'] = _rowmap(
        "mix_out_bwd", mix_out_bwd, (1, t_rows // trn),
        [_rows(h_rg, trn), gate_in, _rows(y_s5, trn), _rows(dycat, trn), _full(rg_out_norm), _full(s5_out_norm)],
        [_o((t_rows, r), F32, trn), _o((t_rows, r), F32, trn), _o((t_rows, s5w), F32, trn)],
        [_acc((1, r)), _acc((1, s5w))])

    def glu_bwd(dyv, yv, glv):
        zv = _gelu(yv)
        sg = _sigmoid(glv)
        dgl = dyv * zv * sg * (1.0 - sg)
        return dyv * sg, dgl.astype(BF16), jnp.sum(dgl, axis=0, keepdims=True)

    dz_dir, dgl16, small['s5_glu_b'] = _rowmap(
        "s5_glu_bwd", glu_bwd, (1, t_rows // trn), [_rows(dy_s5, trn), _rows(y_pre, trn), _rows(gl, trn)],
        [_o((t_rows, s5w), F32, trn), _o((t_rows, s5w), BF16, trn)], [_acc((1, s5w))])

    def dgelu(acc, dzd, yv):
        _, vjp = jax.vjp(_gelu, yv)
        return vjp(acc + dzd)[0]

    dy_pre, = _matmul("s5_dz", 'nt', dgl16, (tm, s5w), lambda i, j, k: (i, 0),
                      wfull['s5_glu_w'], (tns, s5w), lambda i, j, k: (j, 0),
                      (t_rows // tm, s5w // tns, 1), [((t_rows, s5w), F32, (tm, tns), lambda i, j: (i, j))],
                      dgelu, extras=[(dz_dir, (tm, tns), lambda i, j: (i, j)), (y_pre, (tm, tns), lambda i, j: (i, j))])
    gl_rows = s5w // N_CHIPS
    tng = _tile(gl_rows, 1024)
    npg = gl_rows // tng
    dw_glu, = _matmul("dw_glu", 'tn', z16, (t_rows, tng), lambda i, j, k: (0, i),
                      dgl16, (t_rows, tns), lambda i, j, k: (0, j),
                      (s5w // tng, s5w // tns, 1),
                      [((N_CHIPS, gl_rows, s5w), BF16, (None, tng, tns), lambda i, j: (i // npg, i % npg, j))],
                      lambda acc: acc)
    pair_glu = _pair('s5_glu_w', dw_glu, pos)
    du_s5, dbblk, dcblk, dlamblk, small['s5_d'] = _s5_bwd(
        "s5_bwd", dy_pre, proj, col0, states, bblk, cblk, lamblk, s5_d, s5w, tr)

    def from_bblk(m):
        return _blockdiag_take(m, S5_STATE, S5_GROUP).reshape(groups, S5_STATE * S5_GROUP)

    dbbr, dbbi = from_bblk(dbblk[:, :hw]), from_bblk(dbblk[:, hw:])
    dcs = lambda m: _blockdiag_take(m, S5_STATE, S5_GROUP).transpose(0, 1, 3, 2).reshape(1, groups, S5_GROUP, S5_STATE)
    small['s5_c_re'], small['s5_c_im'] = dcs(dcblk[:, :hw]), -dcs(dcblk[:, hw:])
    dlbr, dlbi = dlamblk[:, 0, :hw].reshape(groups, S5_STATE), dlamblk[:, 0, hw:].reshape(groups, S5_STATE)
    g_lre, g_lim, g_ldt, g_bre, g_bim = _s5_disc(
        "s5_disc_bwd", s5_lambda_re[0], s5_lambda_im[0], log_dt2, b_re2, b_im2, expand, cots=(dlbr, dlbi, dbbr, dbbi))
    small['s5_lambda_re'], small['s5_lambda_im'], small['s5_log_dt'] = g_lre[None], g_lim[None], g_ldt.reshape(1, groups)
    small['s5_b_re'] = g_bre.reshape(s5_b_re.shape)
    small['s5_b_im'] = g_bim.reshape(s5_b_im.shape)

    dh_rg = _linear_scan("rg_scan_bwd", a_dec, dh_out, tr, reverse=True, shift_a=True)

    hb = tr // 8

    def gates_bwd(xcv, dhv, hv, hprev, wa, wx, ba, bx, lam):
        i = pl.program_id(1)
        row = lax.broadcasted_iota(jnp.int32, (tr, hd), 0)
        first = jnp.where(i > 0, hprev[7:8, :], 0.0)
        h_prev = jnp.where(row == 0, jnp.broadcast_to(first, (tr, hd)), pltpu.roll(hv, 1, 0))
        x16 = xcv.astype(BF16)
        pre_r = jnp.dot(x16, wa, preferred_element_type=F32) + ba
        pre_i = jnp.dot(x16, wx, preferred_element_type=F32) + bx
        _, vjp = jax.vjp(_rg_gate_math, xcv, pre_r, pre_i, _softplus_neg(lam))
        dxc_, dpr, dpi, dsp = vjp((dhv * h_prev, dhv))
        dpr16, dpi16 = dpr.astype(BF16), dpi.astype(BF16)
        dxc_ = (dxc_ + lax.dot_general(dpr16, wa, _DN['nt'], preferred_element_type=F32)
                + lax.dot_general(dpi16, wx, _DN['nt'], preferred_element_type=F32))
        dwa = lax.dot_general(x16, dpr16, _DN['tn'], preferred_element_type=F32)
        dwx = lax.dot_general(x16, dpi16, _DN['tn'], preferred_element_type=F32)
        dlam = dsp * (-_sigmoid(-lam))
        return (dxc_, dwa, dwx, jnp.sum(dpr, axis=0, keepdims=True), jnp.sum(dpi, axis=0, keepdims=True), dlam)

    head_acc_w = ((RG_HEADS, hd, hd), F32, (None, hd, hd), lambda j, i: (j, 0, 0))
    head_acc_v = ((1, r), F32, (1, hd), lambda j, i: (0, j))
    h_halo = (h_rg, (8, hd), lambda j, i: (jnp.maximum(i * hb - 1, 0), j))
    dxc, d_wa, d_wx, d_ba, d_bx, d_lam = _rowmap(
        "rg_gates_bwd", gates_bwd, (RG_HEADS, t_rows // tr),
        [head_cols(xc), head_cols(dh_rg), head_cols(h_rg), h_halo] + gate_params,
        [head_out], [head_acc_w, head_acc_w, head_acc_v, head_acc_v, head_acc_v])
    small['rg_w_a'], small['rg_w_x'] = d_wa[None], d_wx[None]
    small['rg_b_a'], small['rg_b_x'] = d_ba.reshape(1, RG_HEADS, hd), d_bx.reshape(1, RG_HEADS, hd)
    small['rg_lambda'] = d_lam
    du_rg, d_convw, small['rg_conv_b'] = _conv_bwd("rg_conv_bwd", dxc, proj, conv_w2, r, tr)
    small['rg_conv_w'] = d_convw[None]

    dproj16, = _rowmap("dproj", lambda a, b, c: jnp.concatenate([a, b, c], axis=1).astype(BF16), (1, t_rows // trn),
                       [_rows(du_rg, trn), _rows(dgate_rg, trn), _rows(du_s5, trn)], [_o((t_rows, 3 * r), BF16, trn)])
    dn2, land_w_out = _matmul(
        "d_in_proj", 'nt', dproj16, (tm, pw), lambda i, j, k: (i, k),
        wfull['w_in'], (None, tnd, pw), lambda i, j, k: (k, j, 0),
        (t_rows // tm, d // tnd, N_CHIPS), [((t_rows, d), F32, (tm, tnd), lambda i, j: (i, j))],
        lambda acc: acc, comms=[_scatter_comm(pair_w_out)])
    parts['w_out'] = (pair_w_out, land_w_out)
    tq = _tile(pw, 768)
    nq = pw // tq
    dw_in, = _matmul("dw_in", 'tn', n2, (t_rows, tnd), lambda i, j, k: (0, i),
                     dproj16, (t_rows, tq), lambda i, j, k: (0, j),
                     (d // tnd, N_CHIPS * nq, 1),
                     [((N_CHIPS, d, pw), BF16, (None, tnd, tq), lambda i, j: (j // nq, i, j % nq))],
                     lambda acc: acc)
    pair_w_in = _pair('w_in', dw_in, pos)

    def norm_bwd(xv, dnv, dhv, g):
        _, vjp = jax.vjp(_rms, xv, g)
        dx, dg = vjp(dnv)
        res = dhv + dx
        return res, res.astype(BF16), dg

    dh1, dh1_16, small['mix_norm'] = _rowmap(
        "mix_dnorm", norm_bwd, (1, t_rows // trn), [_rows(h1, trn), _rows(dn2, trn), _rows(dh2, trn), _full(mix_norm)],
        [_o((t_rows, d), F32, trn), _o((t_rows, d), BF16, trn)], [_acc((1, d))])

    dh0, _, small['ffn1_norm'], parts1, got = _ffn_bwd(
        "ffn1b", ('ffn1_w_gate', 'ffn1_w_up', 'ffn1_w_down'), dh1, dh1_16, sv1, ffn1_norm, wfull['ffn1_w_gate'],
        dims, pos, host={'dact': [_scatter_comm(pair_glu), _scatter_comm(pair_w_in)]})
    parts.update(parts1)
    parts['s5_glu_w'] = (pair_glu, got['dact'][0])
    parts['w_in'] = (pair_w_in, got['dact'][1])
    grad_x = dh0[N_META:N_META + seq][None]
    small['meta_tokens'] = dh0[:N_META]
    small['final_norm'] = d_final.reshape(d)

    gbig, prev, half = {}, None, None
    for nm in BIG:
        res = _reduce_partials(f"red_{nm}", parts[nm][0], parts[nm][1], pos, _exchange_comm(half) if prev else None)
        if prev:
            gbig[prev] = res[1]
        prev, half = nm, res[0]
    gbig[prev] = _run_comm(f"exchange_{prev}", _exchange_comm(half))[0]
    gbig = {nm: g2.reshape(-1, g2.shape[2]) for nm, g2 in gbig.items()}

    small_names = [n for n in WEIGHTS if n not in BIG]
    full_shape = {n: small[n].shape for n in small_names}
    flat = jnp.concatenate([small[n].reshape(-1) for n in small_names])
    unit = 8 * 8 * LANES
    total = _round_up(flat.shape[0], unit)
    flat = jnp.concatenate([flat, jnp.zeros((total - flat.shape[0],), F32)])
    red = _all_reduce_small("ar_small", flat.reshape(total // LANES, LANES)).reshape(-1)
    gsmall = {}
    off = 0
    for n in small_names:
        size = math.prod(full_shape[n])
        gsmall[n] = _own_shard(n, red[off:off + size].reshape(full_shape[n]), w[n].shape, chip)
        off += size

    out_g, out_d, out_m, out_v = {}, {}, {}, {}
    for nm in BIG:
        w2 = local2d(w, nm)
        trw = _tile(w2.shape[0], 128 if w2.shape[1] < 4096 else 64, 8)
        res = _adamw(f"adamw_{nm}", w2, gbig[nm], local2d(mom, nm), local2d(var, nm), trw)
        back = (lambda a: jnp.swapaxes(a, 0, 1)[None]) if nm in transposed else (lambda a: a[None])
        out_g[nm], out_d[nm], out_m[nm], out_v[nm] = [back(a) for a in res]

    def pack(tree):
        fl = jnp.concatenate([tree[n].reshape(-1) for n in small_names])
        tot = _round_up(fl.shape[0], 8 * LANES)
        return jnp.concatenate([fl, jnp.zeros((tot - fl.shape[0],), F32)]).reshape(tot // LANES, LANES)

    wp, gp, mp, vp = pack(w), pack(gsmall), pack(mom), pack(var)
    res = _adamw("adamw_small", wp, gp, mp, vp, _tile(wp.shape[0], 512, 8))
    off = 0
    for n in small_names:
        size = math.prod(w[n].shape)
        for dst, src in zip((out_g, out_d, out_m, out_v), res):
            dst[n] = src.reshape(-1)[off:off + size].reshape(w[n].shape)
        off += size

    return (loss, grad_x, *[out_g[n] for n in WEIGHTS], *[out_d[n] for n in WEIGHTS],
            *[out_m[n] for n in WEIGHTS], *[out_v[n] for n in WEIGHTS])


def _gather_small_shards(conv_w, b_a, b_x, gather):
    cw = conv_w.shape[1]
    part = b_a.shape[1]

    def rows8(a):
        a = jnp.concatenate([a, jnp.zeros((a.shape[0], cw - a.shape[1]), F32)], axis=1)
        return jnp.concatenate([a, jnp.zeros((8 - a.shape[0], cw), F32)], axis=0) if a.shape[0] < 8 else a

    local = jnp.concatenate([rows8(conv_w), rows8(b_a), rows8(b_x), jnp.zeros((8, cw), F32)], axis=0)
    full = gather("rg_small", local)
    conv_full = full[:, :CONV_WIDTH].transpose(1, 0, 2).reshape(CONV_WIDTH, N_CHIPS * cw)
    bias = lambda k: full[:, 8 * k:8 * k + RG_HEADS, :part].transpose(1, 0, 2).reshape(1, RG_HEADS * N_CHIPS * part)
    return conv_full, bias(1), bias(2)


def _own_shard(name, g, local_shape, chip):
    if tuple(g.shape) == tuple(local_shape):
        return g
    axis = [k for k, (a, b) in enumerate(zip(g.shape, local_shape)) if a != b][0]
    size = local_shape[axis]
    return lax.dynamic_slice_in_dim(g, chip * size, size, axis=axis)
```

```python
import functools
import math

import jax
import jax.numpy as jnp
from jax import lax
from jax.experimental import pallas as pl
from jax.experimental.pallas import tpu as pltpu

F32 = jnp.float32
BF16 = jnp.bfloat16
MESH = pl.DeviceIdType.MESH

N_META = 16
RG_HEADS = 8
CONV_WIDTH = 4
RG_C = 8.0
S5_GROUP = 16
S5_STATE = 64
GROUPS_PER_BLOCK = 8
EPS = 1e-6
N_CHIPS = 4
LANES = 128
VMEM_LIMIT = 56 * 1024 * 1024

ADAM_LR = 0.001
ADAM_B1 = 0.9
ADAM_B2 = 0.999
ADAM_EPS = 1e-08
ADAM_WD = 0.01
ADAM_STEP = 10

WEIGHTS = ['meta_tokens', 'ffn1_norm', 'ffn1_w_gate', 'ffn1_w_up', 'ffn1_w_down', 'mix_norm', 'w_in', 'rg_conv_w',
           'rg_conv_b', 'rg_w_a', 'rg_b_a', 'rg_w_x', 'rg_b_x', 'rg_lambda', 's5_lambda_re', 's5_lambda_im',
           's5_log_dt', 's5_b_re', 's5_b_im', 's5_c_re', 's5_c_im', 's5_d', 's5_glu_w', 's5_glu_b', 'rg_out_norm',
           's5_out_norm', 'w_out', 'ffn2_norm', 'ffn2_w_gate', 'ffn2_w_up', 'ffn2_w_down', 'final_norm']
BIG = ('ffn1_w_gate', 'ffn1_w_up', 'ffn1_w_down', 'w_in', 's5_glu_w', 'w_out', 'ffn2_w_gate', 'ffn2_w_up',
       'ffn2_w_down')

_DN = {'nn': (((1,), (0,)), ((), ())), 'nt': (((1,), (1,)), ((), ())), 'tn': (((0,), (0,)), ((), ()))}


def _round_up(n, m):
    return (n + m - 1) // m * m


def _tile(n, pref, unit=LANES):
    best = None
    for t in range(unit, min(n, pref) + 1, unit):
        if n % t == 0:
            best = t
    return best if best is not None else n


MXU_WIDTH = 256


def _column_chunks(n):
    first = _round_up(n // 2, MXU_WIDTH)
    return [(0, first), (first, n - first)] if n > 2 * MXU_WIDTH and first < n else [(0, n)]


def _params(sem=None):
    return pltpu.CompilerParams(dimension_semantics=sem, vmem_limit_bytes=VMEM_LIMIT)


def _rms(x, g):
    return x * lax.rsqrt(jnp.mean(x * x, axis=-1, keepdims=True) + EPS) * g


def _sigmoid(x):
    return 0.5 * (jnp.tanh(0.5 * x) + 1.0)


def _gelu(x):
    return 0.5 * x * (1.0 + jnp.tanh(math.sqrt(2.0 / math.pi) * (x + 0.044715 * (x * x * x))))


def _silu(x):
    return x * _sigmoid(x)


def _expm1(x):
    series = x * (1.0 + x * (1.0 / 2) * (1.0 + x * (1.0 / 3) * (1.0 + x * (1.0 / 4) * (1.0 + x * (1.0 / 5) * (1.0 + x * (1.0 / 6))))))
    return jnp.where(jnp.abs(x) < 0.3, series, jnp.exp(x) - 1.0)


def _softplus_neg(lam):
    m = jnp.maximum(-lam, 0.0)
    e = jnp.exp(-jnp.abs(lam))
    w = 1.0 + e
    log1p = jnp.where(w == 1.0, e, jnp.log(w) * (e / jnp.where(w == 1.0, 1.0, w - 1.0)))
    return m + log1p


def _rg_gate_math(xc, pre_r, pre_i, sp):
    r = 1.0 / (1.0 + jnp.exp(-pre_r))
    i = 1.0 / (1.0 + jnp.exp(-pre_i))
    log_a = -RG_C * r * sp
    a = jnp.exp(log_a)
    mult = jnp.sqrt(-_expm1(2.0 * log_a))
    return a, mult * i * xc


class _Comm:
    def __init__(self, arrays, out_shapes, aliased, sems, start, finish, mid=None):
        self.arrays, self.out_shapes, self.aliased, self.sems = arrays, out_shapes, aliased, sems
        self.start, self.finish, self.mid = start, finish, mid


def _matmul(name, mode, a, a_blk, a_map, b, b_blk, b_map, grid, outs, epilogue, extras=(), comms=(), chunks=None):
    ni, nj, nk = grid
    ne, no = len(extras), len(outs)
    sq = lambda blk: tuple(d for d in blk if d is not None)
    ab, bb = sq(a_blk), sq(b_blk)
    acc_shape = {'nn': (ab[0], bb[1]), 'nt': (ab[0], bb[0]), 'tn': (ab[1], bb[1])}[mode]
    n_cin = sum(len(cm.arrays) for cm in comms)
    n_cout = sum(len(cm.out_shapes) for cm in comms)
    n_acc = 1 if nk > 1 else 0

    def comm_refs(refs):
        cin = refs[2 + ne:2 + ne + n_cin]
        cout = refs[2 + ne + n_cin + no:2 + ne + n_cin + no + n_cout]
        csem = refs[2 + ne + n_cin + no + n_cout + n_acc:]
        for cm in comms:
            yield cm, cin[:len(cm.arrays)], cout[:len(cm.out_shapes)], csem[:len(cm.sems)]
            cin, cout, csem = cin[len(cm.arrays):], cout[len(cm.out_shapes):], csem[len(cm.sems):]

    def body(*refs):
        a_ref, b_ref = refs[0], refs[1]
        ex = refs[2:2 + ne]
        out = refs[2 + ne + n_cin:2 + ne + n_cin + no]
        if comms:
            @pl.when((pl.program_id(0) == 0) & (pl.program_id(1) == 0) & (pl.program_id(2) == 0))
            def _():
                for cm, cin, cout, csem in comm_refs(refs):
                    cm.start(cin, cout, csem)

            if any(cm.mid for cm in comms):
                step = (pl.program_id(0) * nj + pl.program_id(1)) * nk + pl.program_id(2)

                @pl.when(step == (3 * ni * nj * nk) // 5)
                def _():
                    for cm, cin, cout, csem in comm_refs(refs):
                        if cm.mid:
                            cm.mid(cin, cout, csem)

        if chunks:
            assert mode == 'nt' and nk == 1
            av = a_ref[...]
            for c0, cw in chunks:
                acc = lax.dot_general(av, b_ref[c0:c0 + cw, :], _DN['nt'], preferred_element_type=F32)
                res = epilogue(acc, *[e[:, c0:c0 + cw] for e in ex])
                for o, r in zip(out, res if isinstance(res, tuple) else (res,)):
                    o[:, c0:c0 + cw] = r.astype(o.dtype)
            if comms:
                @pl.when((pl.program_id(0) == ni - 1) & (pl.program_id(1) == nj - 1))
                def _():
                    for cm, cin, cout, csem in comm_refs(refs):
                        cm.finish(cin, cout, csem)
            return

        part = lax.dot_general(a_ref[...], b_ref[...], _DN[mode], preferred_element_type=F32)

        def finish(acc):
            res = epilogue(acc, *[e[...] for e in ex])
            if not isinstance(res, tuple):
                res = (res,)
            for o, r in zip(out, res):
                o[...] = r.astype(o.dtype)

        if nk == 1:
            finish(part)
        else:
            acc_ref = refs[2 + ne + n_cin + no + n_cout]
            k = pl.program_id(2)

            @pl.when(k == 0)
            def _():
                acc_ref[...] = part

            @pl.when(k > 0)
            def _():
                acc_ref[...] += part

            @pl.when(k == nk - 1)
            def _():
                finish(acc_ref[...])

        if comms:
            @pl.when((pl.program_id(0) == ni - 1) & (pl.program_id(1) == nj - 1) & (pl.program_id(2) == nk - 1))
            def _():
                for cm, cin, cout, csem in comm_refs(refs):
                    cm.finish(cin, cout, csem)

    hbm = pl.BlockSpec(memory_space=pltpu.HBM)
    in_specs = [pl.BlockSpec(a_blk, a_map), pl.BlockSpec(b_blk, b_map)]
    in_specs += [pl.BlockSpec(blk, functools.partial(lambda m, i, j, k: m(i, j), m)) for _, blk, m in extras]
    in_specs += [hbm] * n_cin
    out_specs = [pl.BlockSpec(blk, functools.partial(lambda m, i, j, k: m(i, j), m)) for _, _, blk, m in outs]
    out_specs += [hbm] * n_cout
    aliases, cin_at, cout_at = {}, 2 + ne, no
    for cm in comms:
        if cm.aliased:
            aliases.update({cin_at + k: cout_at + k for k in range(len(cm.arrays))})
        cin_at, cout_at = cin_at + len(cm.arrays), cout_at + len(cm.out_shapes)
    res = pl.pallas_call(
        body, name=name, grid=grid, in_specs=in_specs, out_specs=out_specs,
        out_shape=[jax.ShapeDtypeStruct(s, d) for s, d, _, _ in outs] + [s for cm in comms for s in cm.out_shapes],
        scratch_shapes=([pltpu.VMEM(acc_shape, F32)] if nk > 1 else [])
        + [pltpu.SemaphoreType.DMA((n,)) for cm in comms for n in cm.sems],
        input_output_aliases=aliases,
        compiler_params=_params(("arbitrary",) * 3 if comms else ("parallel", "parallel", "arbitrary")),
    )(a, b, *[e for e, _, _ in extras], *[arr for cm in comms for arr in cm.arrays])
    return res


def _rows(arr, tr, tc=None, col0=0):
    if tc is None:
        return (arr, (tr, arr.shape[1]), lambda j, i: (i, 0))
    return (arr, (tr, tc), lambda j, i: (i, col0 + j))


def _full(arr):
    nd = arr.ndim
    return (arr, arr.shape, lambda j, i: (0,) * nd)


def _cols(arr, tc):
    return (arr, (arr.shape[0], tc), lambda j, i: (0, j))


def _rowmap(name, fn, grid, ins, outs, accs=(), scratch=()):
    nj, ni = grid
    n_in, n_out, n_acc = len(ins), len(outs), len(accs)

    def body(*refs):
        vals = [r[...] for r in refs[:n_in]]
        o_refs = refs[n_in:n_in + n_out]
        a_refs = refs[n_in + n_out:n_in + n_out + n_acc]
        s_refs = refs[n_in + n_out + n_acc:]
        res = fn(*vals, *s_refs)
        if not isinstance(res, tuple):
            res = (res,)
        for o, r in zip(o_refs, res[:n_out]):
            o[...] = r.astype(o.dtype)
        i = pl.program_id(1)
        for a_ref, r in zip(a_refs, res[n_out:]):
            @pl.when(i == 0)
            def _(a_ref=a_ref, r=r):
                a_ref[...] = r.astype(a_ref.dtype)

            @pl.when(i > 0)
            def _(a_ref=a_ref, r=r):
                a_ref[...] += r.astype(a_ref.dtype)

    res = pl.pallas_call(
        body, name=name, grid=grid,
        in_specs=[pl.BlockSpec(blk, m) for _, blk, m in ins],
        out_specs=[pl.BlockSpec(blk, m) for _, _, blk, m in list(outs) + list(accs)],
        out_shape=[jax.ShapeDtypeStruct(s, d) for s, d, _, _ in list(outs) + list(accs)],
        scratch_shapes=list(scratch),
        compiler_params=_params(("parallel", "arbitrary")),
    )(*[a for a, _, _ in ins])
    return res


def _o(shape, dtype, tr, tc=None):
    if tc is None:
        return (shape, dtype, (tr, shape[1]), lambda j, i: (i, 0))
    return (shape, dtype, (tr, tc), lambda j, i: (i, j))


def _acc(shape, tc=None):
    if tc is None:
        nd = len(shape)
        return (shape, F32, shape, lambda j, i: (0,) * nd)
    return (shape, F32, (shape[0], tc), lambda j, i: (0, j))


def _position():
    x, y, c = lax.axis_index("x"), lax.axis_index("y"), lax.axis_index("c")
    return x, y, c


def _relay_gather_comm(buf):
    _, rows, cols = buf.shape
    rh = rows // 2
    rq = rh // 2

    def plan(out_ref, sems):
        x, y, c = _position()
        own, cx, cy, cd = 2 * x + y, 2 * (1 - x) + y, 2 * x + (1 - y), 2 * (1 - x) + (1 - y)
        to_x, to_y, sibling = (1 - x, y, c), (x, 1 - y, c), (x, y, 1 - c)
        half, other = pl.ds(c * rh, rh), pl.ds((1 - c) * rh, rh)
        q0, q1 = pl.ds(c * rh, rq), pl.ds(c * rh + rq, rq)

        def copy(pair, k, chip_index, rows_, to):
            return pltpu.make_async_remote_copy(
                src_ref=out_ref.at[chip_index, rows_], dst_ref=out_ref.at[chip_index, rows_],
                send_sem=sems[pair].at[k], recv_sem=sems[pair + 1].at[k], device_id=to, device_id_type=MESH)

        return dict(
            own_x=copy(0, 0, own, half, to_x), own_y=copy(0, 1, own, half, to_y),
            in_x=copy(0, 0, cx, half, to_x), in_y=copy(0, 1, cy, half, to_y),
            fwd_xy=copy(0, 2, cx, q0, to_y), fwd_yx=copy(0, 3, cy, q1, to_x),
            in_d0=copy(0, 2, cd, q0, to_y), in_d1=copy(0, 3, cd, q1, to_x),
            d2d_out=[copy(2, j, ch, half, sibling) for j, ch in enumerate((cx, cy, cd))],
            d2d_in=[copy(2, j, ch, other, sibling) for j, ch in enumerate((cx, cy, cd))])

    def start(ins, outs, sems):
        p = plan(outs[0], sems)
        p['own_x'].start()
        p['own_y'].start()

    def mid(ins, outs, sems):
        p = plan(outs[0], sems)
        p['in_x'].wait_recv()
        p['fwd_xy'].start()
        p['d2d_out'][0].start()
        p['in_y'].wait_recv()
        p['fwd_yx'].start()
        p['d2d_out'][1].start()

    def finish(ins, outs, sems):
        p = plan(outs[0], sems)
        p['in_d0'].wait_recv()
        p['in_d1'].wait_recv()
        p['d2d_out'][2].start()
        for cp in p['d2d_in']:
            cp.wait_recv()
        for cp in [p['own_x'], p['own_y'], p['fwd_xy'], p['fwd_yx']] + p['d2d_out']:
            cp.wait_send()

    return _Comm([buf], [jax.ShapeDtypeStruct(buf.shape, buf.dtype)], True, [4, 4, 3, 3], start, finish, mid)


def _gather_comm(buf):
    _, rows, cols = buf.shape
    rh = rows // 2
    if buf.dtype == BF16 and rh % 32 == 0:
        return _relay_gather_comm(buf)

    def plan(out_ref, sems):
        x, y, c = _position()
        chips = [(1 - x, y), (x, 1 - y), (1 - x, 1 - y)]

        def copy(pair, j, chip_index, half, to):
            return pltpu.make_async_remote_copy(
                src_ref=out_ref.at[chip_index, half], dst_ref=out_ref.at[chip_index, half],
                send_sem=sems[pair].at[j], recv_sem=sems[pair + 1].at[j], device_id=to, device_id_type=MESH)

        mine, other = pl.ds(c * rh, rh), pl.ds((1 - c) * rh, rh)
        ici_out = [copy(0, j, 2 * x + y, mine, (px, py, c)) for j, (px, py) in enumerate(chips)]
        ici_in = [copy(0, j, 2 * px + py, mine, (px, py, c)) for j, (px, py) in enumerate(chips)]
        d2d_out = [copy(2, j, 2 * px + py, mine, (x, y, 1 - c)) for j, (px, py) in enumerate(chips)]
        d2d_in = [copy(2, j, 2 * px + py, other, (x, y, 1 - c)) for j, (px, py) in enumerate(chips)]
        return ici_out, ici_in, d2d_out, d2d_in

    def start(ins, outs, sems):
        for cp in plan(outs[0], sems)[0]:
            cp.start()

    def finish(ins, outs, sems):
        ici_out, ici_in, d2d_out, d2d_in = plan(outs[0], sems)
        for arrived, fwd in zip(ici_in, d2d_out):
            arrived.wait_recv()
            fwd.start()
        for cp in d2d_in:
            cp.wait_recv()
        for cp in ici_out + d2d_out:
            cp.wait_send()

    return _Comm([buf], [jax.ShapeDtypeStruct(buf.shape, buf.dtype)], True, [3, 3, 3, 3], start, finish)


def _scatter_comm(p):
    _, rh, cols = p.shape

    def plan(p_ref, land_ref, sems):
        x, y, c = _position()
        chips = [(1 - x, y), (x, 1 - y), (1 - x, 1 - y)]

        def copy(j, chip_index, to):
            return pltpu.make_async_remote_copy(
                src_ref=p_ref.at[chip_index], dst_ref=land_ref.at[j],
                send_sem=sems[0].at[j], recv_sem=sems[1].at[j], device_id=to, device_id_type=MESH)

        out = [copy(j, 2 * px + py, (px, py, c)) for j, (px, py) in enumerate(chips)]
        arrive = [copy(j, 2 * x + y, (x, y, c)) for j in range(3)]
        return out, arrive

    def start(ins, outs, sems):
        for cp in plan(ins[0], outs[0], sems)[0]:
            cp.start()

    def finish(ins, outs, sems):
        out, arrive = plan(ins[0], outs[0], sems)
        for cp in arrive:
            cp.wait_recv()
        for cp in out:
            cp.wait_send()

    return _Comm([p], [jax.ShapeDtypeStruct((3, rh, cols), p.dtype)], False, [3, 3], start, finish)


def _run_comm(name, cm):
    n_in, n_out = len(cm.arrays), len(cm.out_shapes)

    def body(*refs):
        ins, outs, sems = refs[:n_in], refs[n_in:n_in + n_out], refs[n_in + n_out:]
        cm.start(ins, outs, sems)
        if cm.mid:
            cm.mid(ins, outs, sems)
        cm.finish(ins, outs, sems)

    hbm = pl.BlockSpec(memory_space=pltpu.HBM)
    return pl.pallas_call(
        body, name=name, in_specs=[hbm] * n_in, out_specs=[hbm] * n_out, out_shape=list(cm.out_shapes),
        input_output_aliases={k: k for k in range(n_in)} if cm.aliased else {},
        scratch_shapes=[pltpu.SemaphoreType.DMA((n,)) for n in cm.sems],
    )(*cm.arrays)


def _swap_comm(g):
    n, rows, cols = g.shape
    rh = rows // 2

    def plan(g_ref, land_ref, sems):
        x, y, c = _position()

        def copy(k, half):
            return pltpu.make_async_remote_copy(
                src_ref=g_ref.at[k, pl.ds(half * rh, rh)], dst_ref=land_ref.at[k],
                send_sem=sems[0].at[k], recv_sem=sems[1].at[k], device_id=(x, y, 1 - c), device_id_type=MESH)

        return [copy(k, 1 - c) for k in range(n)], [copy(k, c) for k in range(n)]

    def start(ins, outs, sems):
        for cp in plan(ins[0], outs[0], sems)[0]:
            cp.start()

    def finish(ins, outs, sems):
        out, arrive = plan(ins[0], outs[0], sems)
        for cp in arrive:
            cp.wait_recv()
        for cp in out:
            cp.wait_send()

    return _Comm([g], [jax.ShapeDtypeStruct((n, rh, cols), g.dtype)], False, [n, n], start, finish)


def _pair_sum(name, g, land, pos):
    n, rows, cols = g.shape
    rh = rows // 2
    tr = _tile(rh, 256, 16)
    nt = rh // tr

    def body(pos_ref, g_ref, l_ref, o_ref):
        o_ref[...] = (g_ref[...].astype(F32) + l_ref[...].astype(F32)).astype(o_ref.dtype)

    grid_spec = pltpu.PrefetchScalarGridSpec(
        num_scalar_prefetch=1, grid=(n, nt),
        in_specs=[pl.BlockSpec((None, tr, cols), lambda k, i, p: (k, p[1] * nt + i, 0)),
                  pl.BlockSpec((None, tr, cols), lambda k, i, p: (k, i, 0))],
        out_specs=pl.BlockSpec((None, tr, cols), lambda k, i, p: (k, i, 0)))
    return pl.pallas_call(
        body, name=name, grid_spec=grid_spec, out_shape=jax.ShapeDtypeStruct((n, rh, cols), g.dtype),
        compiler_params=_params(("parallel", "parallel")),
    )(pos, g, land)


def _reduce_partials(name, p, land, pos, comm=None):
    _, rh, cols = p.shape
    tr = _tile(rh, 256, 16)
    nt = rh // tr
    n_cin = len(comm.arrays) if comm else 0
    n_cout = len(comm.out_shapes) if comm else 0

    def body(*refs):
        p_ref, l_ref, o_ref = refs[1], refs[2], refs[3 + n_cin]
        comm_args = (refs[3:3 + n_cin], refs[4 + n_cin:4 + n_cin + n_cout], refs[4 + n_cin + n_cout:])
        if comm:
            @pl.when(pl.program_id(0) == 0)
            def _():
                comm.start(*comm_args)

        acc = p_ref[...].astype(F32)
        for k in range(3):
            acc = acc + l_ref[k].astype(F32)
        o_ref[...] = acc
        if comm:
            @pl.when(pl.program_id(0) == nt - 1)
            def _():
                comm.finish(*comm_args)

    hbm = pl.BlockSpec(memory_space=pltpu.HBM)
    grid_spec = pltpu.PrefetchScalarGridSpec(
        num_scalar_prefetch=1, grid=(nt,),
        in_specs=[pl.BlockSpec((None, tr, cols), lambda i, p_: (p_[0], i, 0)),
                  pl.BlockSpec((3, tr, cols), lambda i, p_: (0, i, 0))] + [hbm] * n_cin,
        out_specs=[pl.BlockSpec((None, tr, cols), lambda i, p_: (p_[1], i, 0))] + [hbm] * n_cout,
        scratch_shapes=[pltpu.SemaphoreType.DMA((n,)) for n in comm.sems] if comm else [])
    return pl.pallas_call(
        body, name=name, grid_spec=grid_spec,
        out_shape=[jax.ShapeDtypeStruct((2, rh, cols), F32)] + (list(comm.out_shapes) if comm else []),
        input_output_aliases={3 + k: 1 + k for k in range(n_cin)} if comm and comm.aliased else {},
        compiler_params=_params(("arbitrary",) if comm else ("parallel",)),
    )(pos, p, land, *(comm.arrays if comm else []))


def _exchange_comm(buf):
    def plan(out_ref, sems):
        x, y, c = _position()

        def copy(half):
            return pltpu.make_async_remote_copy(
                src_ref=out_ref.at[half], dst_ref=out_ref.at[half], send_sem=sems[0].at[0], recv_sem=sems[1].at[0],
                device_id=(x, y, 1 - c), device_id_type=MESH)

        return copy(c), copy(1 - c)

    def start(ins, outs, sems):
        plan(outs[0], sems)[0].start()

    def finish(ins, outs, sems):
        out, arrive = plan(outs[0], sems)
        arrive.wait_recv()
        out.wait_send()

    return _Comm([buf], [jax.ShapeDtypeStruct(buf.shape, buf.dtype)], True, [1, 1], start, finish)


def _all_reduce_small(name, buf):
    rows = buf.shape[0]
    p = rows // 8

    def body(buf_ref, out_ref, land_ref, red_ref, s1, r1, s2, r2):
        x, y, c = _position()
        me = 4 * x + 2 * y + c

        def peer(r):
            px = 1 - x if (r >> 2) & 1 else x
            py = 1 - y if (r >> 1) & 1 else y
            pc = 1 - c if r & 1 else c
            return (px, py, pc), 4 * px + 2 * py + pc

        firsts = []
        for r in range(1, 8):
            to, d = peer(r)
            cp = pltpu.make_async_remote_copy(
                src_ref=buf_ref.at[pl.ds(d * p, p)], dst_ref=land_ref.at[r - 1],
                send_sem=s1.at[r - 1], recv_sem=r1.at[r - 1], device_id=to, device_id_type=MESH)
            cp.start()
            firsts.append(cp)
        acc = buf_ref[pl.ds(me * p, p), :]
        for r in range(1, 8):
            firsts[r - 1].wait_recv()
            acc = acc + land_ref[r - 1]
        red_ref[...] = acc
        out_ref[pl.ds(me * p, p), :] = acc
        seconds = []
        for r in range(1, 8):
            to, d = peer(r)
            cp = pltpu.make_async_remote_copy(
                src_ref=red_ref, dst_ref=out_ref.at[pl.ds(me * p, p)],
                send_sem=s2.at[r - 1], recv_sem=r2.at[r - 1], device_id=to, device_id_type=MESH)
            cp.start()
            seconds.append(cp)
        for r in range(1, 8):
            to, d = peer(r)
            pltpu.make_async_remote_copy(
                src_ref=red_ref, dst_ref=out_ref.at[pl.ds(d * p, p)],
                send_sem=s2.at[r - 1], recv_sem=r2.at[r - 1], device_id=to, device_id_type=MESH).wait_recv()
        for cp in firsts + seconds:
            cp.wait_send()

    vmem = pl.BlockSpec(memory_space=pltpu.VMEM)
    return pl.pallas_call(
        body, name=name, in_specs=[vmem], out_specs=vmem,
        out_shape=jax.ShapeDtypeStruct(buf.shape, F32),
        scratch_shapes=[pltpu.VMEM((7, p, LANES), F32), pltpu.VMEM((p, LANES), F32),
                        pltpu.SemaphoreType.DMA((7,)), pltpu.SemaphoreType.DMA((7,)),
                        pltpu.SemaphoreType.DMA((7,)), pltpu.SemaphoreType.DMA((7,))],
        compiler_params=pltpu.CompilerParams(vmem_limit_bytes=VMEM_LIMIT),
    )(buf)


def _cast_pad(name, w, rows_to, cols_to, pos):
    rows, cols = w.shape
    tr = _tile(math.gcd(rows, rows_to), 256, 16)
    assert rows % tr == 0 and rows_to % tr == 0, (rows, rows_to, tr)
    n_src = rows // tr

    def body(pos_ref, w_ref, o_ref):
        i = pl.program_id(0)
        if cols_to > cols:
            o_ref[:, cols:] = jnp.zeros((tr, cols_to - cols), BF16)

        @pl.when(i < n_src)
        def _():
            o_ref[:, :cols] = w_ref[...].astype(BF16)

        if rows_to > rows:
            @pl.when(i >= n_src)
            def _():
                o_ref[:, :cols] = jnp.zeros((tr, cols), BF16)

    grid_spec = pltpu.PrefetchScalarGridSpec(
        num_scalar_prefetch=1, grid=(rows_to // tr,),
        in_specs=[pl.BlockSpec((tr, cols), lambda i, p: (jnp.minimum(i, n_src - 1), 0))],
        out_specs=pl.BlockSpec((None, tr, cols_to), lambda i, p: (p[0], i, 0)))
    return pl.pallas_call(
        body, name=name, grid_spec=grid_spec,
        out_shape=jax.ShapeDtypeStruct((N_CHIPS, rows_to, cols_to), BF16),
        compiler_params=_params(("parallel",)),
    )(pos, w)


def _scan_rows(a, b, row, tr, reverse):
    sh = 1
    while sh < tr:
        if reverse:
            valid = row < tr - sh
            a_s = pltpu.roll(a, tr - sh, 0)
            b_s = pltpu.roll(b, tr - sh, 0)
        else:
            valid = row >= sh
            a_s = pltpu.roll(a, sh, 0)
            b_s = pltpu.roll(b, sh, 0)
        b = b + jnp.where(valid, a * b_s, 0.0)
        a = jnp.where(valid, a * a_s, a)
        sh *= 2
    return a, b


def _linear_scan(name, a, b, tr, reverse=False, shift_a=False):
    t_rows, cols = a.shape
    tc = _tile(cols, 512)
    ni, nj = t_rows // tr, cols // tc
    hb = tr // 8

    def rmap(j, i):
        return ((ni - 1 - i) if reverse else i, j)

    def halo_map(j, i):
        ri = ni - 1 - i
        return (jnp.minimum((ri + 1) * hb, t_rows // 8 - 1), j)

    def body(*refs):
        if shift_a:
            a_ref, halo_ref, b_ref, h_ref, carry_ref = refs
        else:
            a_ref, b_ref, h_ref, carry_ref = refs
        i = pl.program_id(1)
        row = lax.broadcasted_iota(jnp.int32, (tr, tc), 0)

        @pl.when(i == 0)
        def _():
            carry_ref[...] = jnp.zeros_like(carry_ref)

        av = a_ref[...]
        if shift_a:
            nxt = jnp.where(i > 0, halo_ref[0:1, :], 0.0)
            av = jnp.where(row == tr - 1, jnp.broadcast_to(nxt, (tr, tc)), pltpu.roll(av, tr - 1, 0))
        pa, hb_ = _scan_rows(av, b_ref[...], row, tr, reverse)
        h = hb_ + pa * carry_ref[0:1, :]
        h_ref[...] = h
        last = h[0:1, :] if reverse else h[tr - 1:tr, :]
        carry_ref[...] = jnp.broadcast_to(last, carry_ref.shape)

    in_specs = [pl.BlockSpec((tr, tc), rmap)]
    args = [a]
    if shift_a:
        in_specs.append(pl.BlockSpec((8, tc), halo_map))
        args.append(a)
    in_specs.append(pl.BlockSpec((tr, tc), rmap))
    args.append(b)
    return pl.pallas_call(
        body, name=name, grid=(nj, ni), in_specs=in_specs, out_specs=pl.BlockSpec((tr, tc), rmap),
        out_shape=jax.ShapeDtypeStruct((t_rows, cols), F32), scratch_shapes=[pltpu.VMEM((8, tc), F32)],
        compiler_params=_params(("parallel", "arbitrary")),
    )(*args)


def _conv_fwd(name, proj, conv_w, conv_b, width, tr):
    t_rows = proj.shape[0]
    ni = t_rows // tr
    hb = tr // 8
    tc = _tile(width, 512)

    def body(u_ref, halo_ref, w_ref, b_ref, o_ref, ext_ref):
        i = pl.program_id(1)
        ext_ref[0:8, :] = jnp.where(i > 0, halo_ref[...], 0.0)
        ext_ref[8:8 + tr, :] = u_ref[...]
        acc = b_ref[...] + w_ref[CONV_WIDTH - 1:CONV_WIDTH, :] * u_ref[...]
        for k in range(CONV_WIDTH - 1):
            acc = acc + w_ref[k:k + 1, :] * ext_ref[pl.ds(8 - (CONV_WIDTH - 1) + k, tr), :]
        o_ref[...] = acc

    return pl.pallas_call(
        body, name=name, grid=(width // tc, ni),
        in_specs=[pl.BlockSpec((tr, tc), lambda j, i: (i, j)),
                  pl.BlockSpec((8, tc), lambda j, i: (jnp.maximum(i * hb - 1, 0), j)),
                  pl.BlockSpec((CONV_WIDTH, tc), lambda j, i: (0, j)),
                  pl.BlockSpec((1, tc), lambda j, i: (0, j))],
        out_specs=pl.BlockSpec((tr, tc), lambda j, i: (i, j)),
        out_shape=jax.ShapeDtypeStruct((t_rows, width), F32),
        scratch_shapes=[pltpu.VMEM((tr + 8, tc), F32)],
        compiler_params=_params(("parallel", "parallel")),
    )(proj, proj, conv_w, conv_b)


def _conv_bwd(name, dxc, proj, conv_w, width, tr):
    t_rows = dxc.shape[0]
    ni = t_rows // tr
    hb = tr // 8
    tc = _tile(width, 512)

    def body(d_ref, dnext_ref, u_ref, uprev_ref, w_ref, du_ref, dw_ref, db_ref, dext_ref, uext_ref):
        i = pl.program_id(1)
        dext_ref[0:tr, :] = d_ref[...]
        dext_ref[tr:tr + 8, :] = jnp.where(i < ni - 1, dnext_ref[...], 0.0)
        uext_ref[0:8, :] = jnp.where(i > 0, uprev_ref[...], 0.0)
        uext_ref[8:8 + tr, :] = u_ref[...]
        d = d_ref[...]
        du = w_ref[CONV_WIDTH - 1:CONV_WIDTH, :] * d
        dws = []
        for k in range(CONV_WIDTH - 1):
            du = du + w_ref[k:k + 1, :] * dext_ref[pl.ds(CONV_WIDTH - 1 - k, tr), :]
            dws.append(jnp.sum(d * uext_ref[pl.ds(8 - (CONV_WIDTH - 1) + k, tr), :], axis=0, keepdims=True))
        dws.append(jnp.sum(d * u_ref[...], axis=0, keepdims=True))
        du_ref[...] = du
        dw = jnp.concatenate(dws, axis=0)
        db = jnp.sum(d, axis=0, keepdims=True)

        @pl.when(i == 0)
        def _():
            dw_ref[...] = dw
            db_ref[...] = db

        @pl.when(i > 0)
        def _():
            dw_ref[...] += dw
            db_ref[...] += db

    return pl.pallas_call(
        body, name=name, grid=(width // tc, ni),
        in_specs=[pl.BlockSpec((tr, tc), lambda j, i: (i, j)),
                  pl.BlockSpec((8, tc), lambda j, i: (jnp.minimum((i + 1) * hb, t_rows // 8 - 1), j)),
                  pl.BlockSpec((tr, tc), lambda j, i: (i, j)),
                  pl.BlockSpec((8, tc), lambda j, i: (jnp.maximum(i * hb - 1, 0), j)),
                  pl.BlockSpec((CONV_WIDTH, tc), lambda j, i: (0, j))],
        out_specs=[pl.BlockSpec((tr, tc), lambda j, i: (i, j)),
                   pl.BlockSpec((CONV_WIDTH, tc), lambda j, i: (0, j)),
                   pl.BlockSpec((1, tc), lambda j, i: (0, j))],
        out_shape=[jax.ShapeDtypeStruct((t_rows, width), F32), jax.ShapeDtypeStruct((CONV_WIDTH, width), F32),
                   jax.ShapeDtypeStruct((1, width), F32)],
        scratch_shapes=[pltpu.VMEM((tr + 8, tc), F32), pltpu.VMEM((tr + 8, tc), F32)],
        compiler_params=_params(("parallel", "arbitrary")),
    )(dxc, dxc, proj, proj, conv_w)


def _s5_disc_math(lam_re, lam_im, log_dt, b_re, b_im, expand):
    dt = jnp.exp(log_dt)
    zr, zi = lam_re * dt, lam_im * dt
    mag = jnp.exp(zr)
    lbr, lbi = mag * jnp.cos(zi), mag * jnp.sin(zi)
    ar, ai = lbr - 1.0, lbi
    den = lam_re * lam_re + lam_im * lam_im
    cr = (ar * lam_re + ai * lam_im) / den
    ci = (ai * lam_re - ar * lam_im) / den
    cre = jnp.dot(cr, expand, precision=lax.Precision.HIGHEST, preferred_element_type=F32)
    cie = jnp.dot(ci, expand, precision=lax.Precision.HIGHEST, preferred_element_type=F32)
    return lbr, lbi, cre * b_re - cie * b_im, cre * b_im + cie * b_re


def _s5_disc(name, lam_re, lam_im, log_dt, b_re, b_im, expand, cots=None):
    ins = [lam_re, lam_im, log_dt, b_re, b_im, expand]
    n_in = len(ins) + (len(cots) if cots else 0)

    def body(*refs):
        vals = [r[...] for r in refs[:6]]
        outs = refs[n_in:]
        if cots is None:
            res = _s5_disc_math(*vals)
        else:
            cv = tuple(r[...] for r in refs[6:n_in])
            _, vjp = jax.vjp(lambda a, b, c, d, e: _s5_disc_math(a, b, c, d, e, vals[5]), *vals[:5])
            res = vjp(cv)
        for o, r in zip(outs, res):
            o[...] = r

    if cots is None:
        shapes = [lam_re.shape, lam_re.shape, b_re.shape, b_re.shape]
    else:
        shapes = [lam_re.shape, lam_re.shape, log_dt.shape, b_re.shape, b_re.shape]
    vmem = pl.BlockSpec(memory_space=pltpu.VMEM)
    return pl.pallas_call(
        body, name=name, in_specs=[vmem] * n_in, out_specs=[vmem] * len(shapes),
        out_shape=[jax.ShapeDtypeStruct(s, F32) for s in shapes],
        compiler_params=pltpu.CompilerParams(vmem_limit_bytes=VMEM_LIMIT),
    )(*ins, *(cots or ()))


def _cmul(ar, ai, br, bi):
    return ar * br - ai * bi, ar * bi + ai * br


SCAN_BLOCK = 8


def _block_scan(br, bi, lr, li, row, rows, reverse):
    pr, pi = lr, li
    pos = row & (SCAN_BLOCK - 1)
    sh = 1
    while sh < SCAN_BLOCK:
        if reverse:
            valid = pos < SCAN_BLOCK - sh
            rs, is_ = pltpu.roll(br, rows - sh, 0), pltpu.roll(bi, rows - sh, 0)
        else:
            valid = pos >= sh
            rs, is_ = pltpu.roll(br, sh, 0), pltpu.roll(bi, sh, 0)
        mr, mi = _cmul(pr, pi, rs, is_)
        br = br + jnp.where(valid, mr, 0.0)
        bi = bi + jnp.where(valid, mi, 0.0)
        pr, pi = _cmul(pr, pi, pr, pi)
        sh *= 2
    return br, bi


def _block_powers(lr, li, hw, reverse):
    row = lax.broadcasted_iota(jnp.int32, (SCAN_BLOCK, hw), 0)
    d = jnp.where(row == (SCAN_BLOCK - 1 if reverse else 0), 1.0, 0.0)
    return _block_scan(jnp.broadcast_to(lr, (SCAN_BLOCK, hw)) * d, jnp.broadcast_to(li, (SCAN_BLOCK, hw)) * d,
                       lr, li, row, SCAN_BLOCK, reverse)


def _chain_blocks(s_ref, pow_ref, carry, rows, hw, reverse):
    nblk = rows // SCAN_BLOCK
    pr, pi = pow_ref[:, :hw], pow_ref[:, hw:]

    def step(n, c):
        b = (nblk - 1 - n) if reverse else n
        at = pl.ds(pl.multiple_of(b * SCAN_BLOCK, SCAN_BLOCK), SCAN_BLOCK)
        blk = s_ref[at, :]
        kr, ki = _cmul(pr, pi, c[0], c[1])
        sr, si = blk[:, :hw] + kr, blk[:, hw:] + ki
        s_ref[at, :] = jnp.concatenate([sr, si], axis=1)
        edge = slice(0, 1) if reverse else slice(SCAN_BLOCK - 1, SCAN_BLOCK)
        return sr[edge, :], si[edge, :]

    return lax.fori_loop(0, nblk, step, carry)


def _s5_fwd(name, proj, col0, bblk, cblk, lamblk, dvec, width, tr):
    t_rows = proj.shape[0]
    ni = t_rows // tr
    nb = width // LANES
    sw = bblk.shape[-1]
    hw = sw // 2

    def body(u_ref, b_ref, c_ref, lam_ref, d_ref, y_ref, s_ref, carry_ref, pow_ref, st_ref):
        i = pl.program_id(1)
        row = lax.broadcasted_iota(jnp.int32, (tr, hw), 0)
        lam = lam_ref[...]
        lr, li = lam[:, :hw], lam[:, hw:]

        @pl.when(i == 0)
        def _():
            carry_ref[...] = jnp.zeros_like(carry_ref)
            zr, zi = _block_powers(lr, li, hw, False)
            pow_ref[:, :hw] = zr
            pow_ref[:, hw:] = zi

        u = u_ref[...]
        bu = jnp.dot(u.astype(BF16), b_ref[...], preferred_element_type=F32)
        sr, si = _block_scan(bu[:, :hw], bu[:, hw:], lr, li, row, tr, False)
        st_ref[:, :hw] = sr
        st_ref[:, hw:] = si
        cr, ci = _chain_blocks(st_ref, pow_ref, (carry_ref[0:1, :hw], carry_ref[0:1, hw:]), tr, hw, False)
        carry_ref[...] = jnp.broadcast_to(jnp.concatenate([cr, ci], axis=1), carry_ref.shape)
        s16 = st_ref[...].astype(BF16)
        s_ref[...] = s16
        y_ref[...] = jnp.dot(s16, c_ref[...], preferred_element_type=F32) + d_ref[...] * u

    return pl.pallas_call(
        body, name=name, grid=(nb, ni),
        in_specs=[pl.BlockSpec((tr, LANES), lambda k, i: (i, col0 + k)),
                  pl.BlockSpec((None, LANES, sw), lambda k, i: (k, 0, 0)),
                  pl.BlockSpec((None, sw, LANES), lambda k, i: (k, 0, 0)),
                  pl.BlockSpec((None, 1, sw), lambda k, i: (k, 0, 0)),
                  pl.BlockSpec((1, LANES), lambda k, i: (0, k))],
        out_specs=[pl.BlockSpec((tr, LANES), lambda k, i: (i, k)),
                   pl.BlockSpec((tr, sw), lambda k, i: (i, k))],
        out_shape=[jax.ShapeDtypeStruct((t_rows, width), F32), jax.ShapeDtypeStruct((t_rows, nb * sw), BF16)],
        scratch_shapes=[pltpu.VMEM((8, sw), F32), pltpu.VMEM((SCAN_BLOCK, sw), F32), pltpu.VMEM((tr, sw), F32)],
        compiler_params=_params(("parallel", "arbitrary")),
    )(proj, bblk, cblk, lamblk, dvec)


def _s5_bwd(name, dy, proj, col0, states, bblk, cblk, lamblk, dvec, width, tr):
    t_rows = dy.shape[0]
    ni = t_rows // tr
    nb = width // LANES
    sw = bblk.shape[-1]
    hw = sw // 2
    hb16 = tr // 16

    def rmap(k, i):
        return (ni - 1 - i, k)

    def body(dy_ref, u_ref, s_ref, sprev_ref, b_ref, c_ref, lam_ref, d_ref,
             du_ref, db_ref, dc_ref, dlam_ref, dd_ref, carry_ref, pow_ref, gt_ref):
        i = pl.program_id(1)
        row = lax.broadcasted_iota(jnp.int32, (tr, hw), 0)
        row_w = lax.broadcasted_iota(jnp.int32, (tr, sw), 0)
        lam = lam_ref[...]
        lr, li = lam[:, :hw], -lam[:, hw:]

        @pl.when(i == 0)
        def _():
            carry_ref[...] = jnp.zeros_like(carry_ref)
            zr, zi = _block_powers(lr, li, hw, True)
            pow_ref[:, :hw] = zr
            pow_ref[:, hw:] = zi

        dyv = dy_ref[...]
        dy16 = dyv.astype(BF16)
        u = u_ref[...]
        gd = lax.dot_general(dy16, c_ref[...], _DN['nt'], preferred_element_type=F32)
        gr, gi = _block_scan(gd[:, :hw], gd[:, hw:], lr, li, row, tr, True)
        gt_ref[:, :hw] = gr
        gt_ref[:, hw:] = gi
        cr, ci = _chain_blocks(gt_ref, pow_ref, (carry_ref[0:1, :hw], carry_ref[0:1, hw:]), tr, hw, True)
        carry_ref[...] = jnp.broadcast_to(jnp.concatenate([cr, ci], axis=1), carry_ref.shape)
        g = gt_ref[...]
        gr, gi = g[:, :hw], g[:, hw:]
        g16 = g.astype(BF16)
        du_ref[...] = lax.dot_general(g16, b_ref[...], _DN['nt'], preferred_element_type=F32) + d_ref[...] * dyv
        s16 = s_ref[...]
        dbv = lax.dot_general(g16, u.astype(BF16), _DN['tn'], preferred_element_type=F32)
        dcv = lax.dot_general(s16, dy16, _DN['tn'], preferred_element_type=F32)
        s32 = s16.astype(F32)
        first = jnp.where(i < ni - 1, sprev_ref[15:16, :].astype(F32), 0.0)
        sp = jnp.where(row_w == 0, jnp.broadcast_to(first, (tr, sw)), pltpu.roll(s32, 1, 0))
        spr, spi = sp[:, :hw], sp[:, hw:]
        dlr = jnp.sum(gr * spr + gi * spi, axis=0, keepdims=True)
        dli = jnp.sum(gi * spr - gr * spi, axis=0, keepdims=True)
        dlam = jnp.concatenate([dlr, dli], axis=1)
        ddv = jnp.sum(dyv * u, axis=0, keepdims=True)

        @pl.when(i == 0)
        def _():
            db_ref[...] = dbv
            dc_ref[...] = dcv
            dlam_ref[...] = dlam
            dd_ref[...] = ddv

        @pl.when(i > 0)
        def _():
            db_ref[...] += dbv
            dc_ref[...] += dcv
            dlam_ref[...] += dlam
            dd_ref[...] += ddv

    return pl.pallas_call(
        body, name=name, grid=(nb, ni),
        in_specs=[pl.BlockSpec((tr, LANES), rmap),
                  pl.BlockSpec((tr, LANES), lambda k, i: (ni - 1 - i, col0 + k)),
                  pl.BlockSpec((tr, sw), rmap),
                  pl.BlockSpec((16, sw), lambda k, i: (jnp.maximum((ni - 1 - i) * hb16 - 1, 0), k)),
                  pl.BlockSpec((None, LANES, sw), lambda k, i: (k, 0, 0)),
                  pl.BlockSpec((None, sw, LANES), lambda k, i: (k, 0, 0)),
                  pl.BlockSpec((None, 1, sw), lambda k, i: (k, 0, 0)),
                  pl.BlockSpec((1, LANES), lambda k, i: (0, k))],
        out_specs=[pl.BlockSpec((tr, LANES), rmap),
                   pl.BlockSpec((None, sw, LANES), lambda k, i: (k, 0, 0)),
                   pl.BlockSpec((None, sw, LANES), lambda k, i: (k, 0, 0)),
                   pl.BlockSpec((None, 1, sw), lambda k, i: (k, 0, 0)),
                   pl.BlockSpec((1, LANES), lambda k, i: (0, k))],
        out_shape=[jax.ShapeDtypeStruct((t_rows, width), F32), jax.ShapeDtypeStruct((nb, sw, LANES), F32),
                   jax.ShapeDtypeStruct((nb, sw, LANES), F32), jax.ShapeDtypeStruct((nb, 1, sw), F32),
                   jax.ShapeDtypeStruct((1, width), F32)],
        scratch_shapes=[pltpu.VMEM((8, sw), F32), pltpu.VMEM((SCAN_BLOCK, sw), F32), pltpu.VMEM((tr, sw), F32)],
        compiler_params=_params(("parallel", "arbitrary")),
    )(dy, proj, states, states, bblk, cblk, lamblk, dvec)


def _blockdiag(m):
    nb, g, p, q = m.shape
    eye = jnp.eye(g, dtype=m.dtype)
    return (m[:, :, :, None, :] * eye[None, :, None, :, None]).reshape(nb, g * p, g * q)


def _blockdiag_take(m, p, q):
    nb = m.shape[0]
    g = GROUPS_PER_BLOCK
    m = m.reshape(nb, g, p, g, q)
    return jnp.stack([m[:, k, :, k, :] for k in range(g)], axis=1)


def _adamw(name, w, g, m, v, tr):
    rows, cols = w.shape
    gcols = g.shape[1]
    c1 = 1.0 - ADAM_B1 ** ADAM_STEP
    c2 = 1.0 - ADAM_B2 ** ADAM_STEP

    def body(w_ref, g_ref, m_ref, v_ref, go_ref, d_ref, mo_ref, vo_ref):
        gv = g_ref[...] if gcols == cols else g_ref[:, :cols]
        mn = ADAM_B1 * m_ref[...] + (1.0 - ADAM_B1) * gv
        vn = ADAM_B2 * v_ref[...] + (1.0 - ADAM_B2) * (gv * gv)
        go_ref[...] = gv
        mo_ref[...] = mn
        vo_ref[...] = vn
        d_ref[...] = -ADAM_LR * ((mn / c1) / (jnp.sqrt(vn / c2) + ADAM_EPS) + ADAM_WD * w_ref[...])

    spec = pl.BlockSpec((tr, cols), lambda i: (i, 0))
    return pl.pallas_call(
        body, name=name, grid=(rows // tr,),
        in_specs=[spec, pl.BlockSpec((tr, gcols), lambda i: (i, 0)), spec, spec],
        out_specs=[spec] * 4, out_shape=[jax.ShapeDtypeStruct((rows, cols), F32)] * 4,
        compiler_params=_params(("parallel",)),
    )(w, g, m, v)


def _ffn_fwd(tag, h, norm, wg, wu, wd, dims, host=None):
    t_rows, d, fp, tm, tr, trn = dims['T'], dims['D'], dims['FP'], dims['TM'], dims['TR'], dims['TRN']
    host = host or {}
    got = {}
    ft = N_CHIPS * fp
    tn = _tile(fp, 1408)
    npb = fp // tn
    n16, = _rowmap(f"{tag}_norm", lambda x, g: _rms(x, g).astype(BF16), (1, t_rows // trn),
                   [_rows(h, trn), _full(norm)], [_o((t_rows, d), BF16, trn)])

    def up(nm, w):
        res = _matmul(f"{tag}_{nm}", 'nt', n16, (tm // 2, d // 2), lambda i, j, k: (i, k),
                      w, (None, fp, d // 2), lambda i, j, k: (j, 0, k),
                      (2 * t_rows // tm, N_CHIPS, 2), [((t_rows, ft), BF16, (tm // 2, fp), lambda i, j: (i, j))],
                      lambda acc: acc, comms=host.get(nm, ()))
        got[nm] = res[1:]
        return res[0]

    gate = up("gate", wg)
    wu = got['gate'][wu] if isinstance(wu, int) else wu
    upv = up("up", wu)
    wd = got['up'][wd] if isinstance(wd, int) else wd
    tc = _tile(ft, 1408)
    act, = _rowmap(f"{tag}_act", lambda g, u: _silu(g.astype(F32)) * u.astype(F32), (ft // tc, t_rows // tr),
                   [_rows(gate, tr, tc), _rows(upv, tr, tc)], [_o((t_rows, ft), BF16, tr, tc)])
    tnd = _tile(d, 1024)
    res = _matmul(f"{tag}_down", 'nn', act, (tm, fp), lambda i, j, k: (i, k),
                  wd, (None, fp, tnd), lambda i, j, k: (k, 0, j),
                  (t_rows // tm, d // tnd, N_CHIPS), [((t_rows, d), F32, (tm, tnd), lambda i, j: (i, j))],
                  lambda acc, hin: hin + 0.5 * acc, extras=[(h, (tm, tnd), lambda i, j: (i, j))],
                  comms=host.get('down', ()))
    got['down'] = res[1:]
    return res[0], dict(h=h, n16=n16, gate=gate, up=upv, act=act, wu=wu, wd=wd), got


def _pair(nm, g, pos):
    return _pair_sum(f"pair_{nm}", g, _run_comm(f"swap_{nm}", _swap_comm(g))[0], pos)


def _ffn_bwd(tag, names, dh, dh16, saved, norm, wg, dims, pos, host=None):
    t_rows, d, fp, tm, trn = dims['T'], dims['D'], dims['FP'], dims['TM'], dims['TRN']
    host = host or {}
    got = {}
    n_gate, n_up, n_down = names
    wu, wd = saved['wu'], saved['wd']
    ft = N_CHIPS * fp
    tn = _tile(fp, 1408)
    npb = fp // tn
    tm2 = tm // 2

    def act_bwd(acc, g, u):
        da = 0.5 * acc
        g32, u32 = g.astype(F32), u.astype(F32)
        sg = _sigmoid(g32)
        return da * u32 * (sg * (1.0 + g32 * (1.0 - sg))), da * (g32 * sg)

    res = _matmul(
        f"{tag}_dact", 'nt', dh16, (tm2, d), lambda i, j, k: (i, 0),
        wd, (None, tn, d), lambda i, j, k: (j // npb, j % npb, 0),
        (t_rows // tm2, ft // tn, 1),
        [((t_rows, ft), BF16, (tm2, tn), lambda i, j: (i, j))] * 2, act_bwd,
        extras=[(saved['gate'], (tm2, tn), lambda i, j: (i, j)), (saved['up'], (tm2, tn), lambda i, j: (i, j))],
        comms=host.get('dact', ()), chunks=_column_chunks(tn))
    dgate, dup, got['dact'] = res[0], res[1], res[2:]
    tnd = _tile(d, 1024)
    tw = _tile(d, 512)
    res = _matmul(f"{tag}_dwd", 'tn', saved['act'], (t_rows, tn), lambda i, j, k: (0, i),
                  dh16, (t_rows, tw), lambda i, j, k: (0, j),
                  (ft // tn, d // tw, 1),
                  [((N_CHIPS, fp, d), BF16, (None, tn, tw), lambda i, j: (i // npb, i % npb, j))],
                  lambda acc: 0.5 * acc, comms=host.get('dwd', ()))
    dwd, got['dwd'] = res[0], res[1:]

    def dw_up(nm, dact, comms):
        return _matmul(f"{tag}_{nm}", 'tn', dact, (t_rows, tn), lambda i, j, k: (0, i),
                       saved['n16'], (t_rows, tw), lambda i, j, k: (0, j),
                       (ft // tn, d // tw, 1),
                       [((N_CHIPS, fp, d), BF16, (None, tn, tw), lambda i, j: (i // npb, i % npb, j))],
                       lambda acc: acc, comms=comms)

    def dn_part(nm, dact, w, prev, comms):
        extras = [] if prev is None else [(prev, (tm, tnd), lambda i, j: (i, j))]
        return _matmul(f"{tag}_{nm}", 'nn', dact, (tm, fp), lambda i, j, k: (i, k),
                       w, (None, fp, tnd), lambda i, j, k: (k, 0, j),
                       (t_rows // tm, d // tnd, N_CHIPS), [((t_rows, d), F32, (tm, tnd), lambda i, j: (i, j))],
                       (lambda acc: acc) if prev is None else (lambda acc, p: acc + p), extras=extras, comms=comms)

    dwg, swap_d = dw_up("dwg", dgate, [_swap_comm(dwd)])
    pair_d = _pair_sum(f"pair_{n_down}", dwd, swap_d, pos)
    dwu, land_d, swap_g = dw_up("dwu", dup, [_scatter_comm(pair_d), _swap_comm(dwg)])
    pair_g = _pair_sum(f"pair_{n_gate}", dwg, swap_g, pos)
    dn_g, land_g, swap_u = dn_part("dn_gate", dgate, wg, None, [_scatter_comm(pair_g), _swap_comm(dwu)])
    pair_u = _pair_sum(f"pair_{n_up}", dwu, swap_u, pos)
    dn, land_u = dn_part("dn_up", dup, wu, dn_g, [_scatter_comm(pair_u)])

    def norm_bwd(x, dnv, dhv, g):
        _, vjp = jax.vjp(_rms, x, g)
        dx, dg = vjp(dnv)
        res = dhv + dx
        return res, res.astype(BF16), dg

    dh_in, dh_in16, dnorm = _rowmap(
        f"{tag}_dnorm", norm_bwd, (1, t_rows // trn),
        [_rows(saved['h'], trn), _rows(dn, trn), _rows(dh, trn), _full(norm)],
        [_o((t_rows, d), F32, trn), _o((t_rows, d), BF16, trn)], [_acc((1, d))])
    parts = {n_down: (pair_d, land_d), n_gate: (pair_g, land_g), n_up: (pair_u, land_u)}
    return dh_in, dh_in16, dnorm, parts, got


def kernel(x, meta_tokens, ffn1_norm, ffn1_w_gate, ffn1_w_up, ffn1_w_down, mix_norm, w_in, rg_conv_w, rg_conv_b, rg_w_a, rg_b_a, rg_w_x, rg_b_x, rg_lambda, s5_lambda_re, s5_lambda_im, s5_log_dt, s5_b_re, s5_b_im, s5_c_re, s5_c_im, s5_d, s5_glu_w, s5_glu_b, rg_out_norm, s5_out_norm, w_out, ffn2_norm, ffn2_w_gate, ffn2_w_up, ffn2_w_down, final_norm, loss_target, m_meta_tokens, m_ffn1_norm, m_ffn1_w_gate, m_ffn1_w_up, m_ffn1_w_down, m_mix_norm, m_w_in, m_rg_conv_w, m_rg_conv_b, m_rg_w_a, m_rg_b_a, m_rg_w_x, m_rg_b_x, m_rg_lambda, m_s5_lambda_re, m_s5_lambda_im, m_s5_log_dt, m_s5_b_re, m_s5_b_im, m_s5_c_re, m_s5_c_im, m_s5_d, m_s5_glu_w, m_s5_glu_b, m_rg_out_norm, m_s5_out_norm, m_w_out, m_ffn2_norm, m_ffn2_w_gate, m_ffn2_w_up, m_ffn2_w_down, m_final_norm, v_meta_tokens, v_ffn1_norm, v_ffn1_w_gate, v_ffn1_w_up, v_ffn1_w_down, v_mix_norm, v_w_in, v_rg_conv_w, v_rg_conv_b, v_rg_w_a, v_rg_b_a, v_rg_w_x, v_rg_b_x, v_rg_lambda, v_s5_lambda_re, v_s5_lambda_im, v_s5_log_dt, v_s5_b_re, v_s5_b_im, v_s5_c_re, v_s5_c_im, v_s5_d, v_s5_glu_w, v_s5_glu_b, v_rg_out_norm, v_s5_out_norm, v_w_out, v_ffn2_norm, v_ffn2_w_gate, v_ffn2_w_up, v_ffn2_w_down, v_final_norm):
    args = locals()
    w = {n: args[n] for n in WEIGHTS}
    mom = {n: args["m_" + n] for n in WEIGHTS}
    var = {n: args["v_" + n] for n in WEIGHTS}

    seq, d = x.shape[1], x.shape[2]
    f_shard = ffn1_w_gate.shape[2]
    fp = _round_up(f_shard, LANES)
    r = rg_conv_b.shape[1]
    s5w = s5_d.shape[1]
    hd = r // RG_HEADS
    groups = s5w // S5_GROUP
    nb = groups // GROUPS_PER_BLOCK
    t_rows = _round_up(N_META + seq, LANES)
    tm, tr = t_rows // 4, t_rows // 8
    trn = _tile(t_rows, max(t_rows // 32, 16), 16)
    dims = dict(T=t_rows, D=d, FP=fp, TM=tm, TR=tr, TRN=trn)
    cx, cy, cc = lax.axis_index("x"), lax.axis_index("y"), lax.axis_index("c")
    chip = 2 * cx + cy
    pos = jnp.stack([chip, cc]).astype(jnp.int32)

    def gather(nm, buf):
        return _run_comm(f"ag_{nm}", _gather_comm(buf))[0]

    def gather_f32(nm, local):
        buf = lax.dynamic_update_slice(jnp.zeros((N_CHIPS,) + local.shape, F32), local[None], (chip, 0, 0))
        return gather(nm, buf)

    slot = {}
    transposed = ('ffn1_w_gate', 'ffn1_w_up', 'ffn2_w_gate', 'ffn2_w_up')
    local2d = lambda tree, nm: jnp.swapaxes(tree[nm][0], 0, 1) if nm in transposed else tree[nm][0]
    for nm in transposed:
        slot[nm] = _cast_pad(f"cast_{nm}", local2d(w, nm), fp, d, pos)
    for nm in ('ffn1_w_down', 'ffn2_w_down'):
        slot[nm] = _cast_pad(f"cast_{nm}", w[nm][0], fp, d, pos)
    slot['w_in'] = _cast_pad("cast_w_in", w_in[0], d, w_in.shape[2], pos)
    slot['w_out'] = _cast_pad("cast_w_out", w_out[0], w_out.shape[1], d, pos)
    slot['s5_glu_w'] = _cast_pad("cast_glu", s5_glu_w[0], s5_glu_w.shape[1], s5w, pos)
    rgw_local = jnp.concatenate([rg_w_a[0], rg_w_x[0]], axis=0).reshape(2 * RG_HEADS * (hd // N_CHIPS), hd)
    rgw = gather('rgw', _cast_pad("cast_rgw", rgw_local, rgw_local.shape[0], hd, pos))
    rgw = rgw.reshape(N_CHIPS, 2, RG_HEADS, hd // N_CHIPS, hd).transpose(1, 2, 0, 3, 4).reshape(2, RG_HEADS, hd, hd)
    w_a16, w_x16 = rgw[0], rgw[1]
    meta_full = gather_f32('meta', meta_tokens).transpose(1, 0, 2).reshape(N_META, d)
    conv_w2, b_a2, b_x2 = _gather_small_shards(rg_conv_w[0], rg_b_a[0], rg_b_x[0], gather_f32)
    wfull = {'ffn1_w_gate': gather('ffn1_w_gate', slot['ffn1_w_gate'])}

    pad_rows = t_rows - N_META - seq
    h0 = jnp.concatenate([meta_full, x[0], jnp.zeros((pad_rows, d), F32)], axis=0)
    tgt = jnp.concatenate([jnp.zeros((N_META, d), F32), loss_target[0], jnp.zeros((pad_rows, d), F32)], axis=0)

    h1, sv1, got = _ffn_fwd(
        "ffn1", h0, ffn1_norm, wfull['ffn1_w_gate'], 0, 0, dims,
        host={'gate': [_gather_comm(slot['ffn1_w_up']), _gather_comm(slot['s5_glu_w'])],
              'up': [_gather_comm(slot['ffn1_w_down']), _gather_comm(slot['w_in'])],
              'down': [_gather_comm(slot['ffn2_w_gate']), _gather_comm(slot['w_out'])]})
    wfull['s5_glu_w'] = got['gate'][1].reshape(s5w, s5w)
    wfull['w_in'] = got['up'][1]
    wfull['ffn2_w_gate'] = got['down'][0]
    wfull['w_out'] = got['down'][1].reshape(d, d)

    n2, = _rowmap("mix_norm", lambda xv, g: _rms(xv, g).astype(BF16), (1, t_rows // trn),
                  [_rows(h1, trn), _full(mix_norm)], [_o((t_rows, d), BF16, trn)])
    pw = w_in.shape[2]
    tnp = _tile(pw, 1536)
    nppb = pw // tnp
    proj, = _matmul(
        "in_proj", 'nn', n2, (tm, d // 2), lambda i, j, k: (i, k),
        wfull['w_in'], (None, d // 2, tnp), lambda i, j, k: (j // nppb, k, j % nppb),
        (t_rows // tm, N_CHIPS * nppb, 2), [((t_rows, 3 * r), F32, (tm, tnp), lambda i, j: (i, j))],
        lambda acc: acc)

    xc = _conv_fwd("rg_conv", proj, conv_w2, rg_conv_b, r, tr)

    def head_cols(arr):
        return (arr, (tr, hd), lambda j, i: (i, j))

    def head_w(arr):
        return (arr, (None, hd, hd), lambda j, i: (j, 0, 0))

    def head_vec(arr):
        return (arr, (1, hd), lambda j, i: (0, j))

    def gates_fwd(xcv, wa, wx, ba, bx, lam):
        x16 = xcv.astype(BF16)
        pre_r = jnp.dot(x16, wa, preferred_element_type=F32) + ba
        pre_i = jnp.dot(x16, wx, preferred_element_type=F32) + bx
        return _rg_gate_math(xcv, pre_r, pre_i, _softplus_neg(lam))

    gate_params = [head_w(w_a16), head_w(w_x16), head_vec(b_a2), head_vec(b_x2), head_vec(rg_lambda)]
    head_out = ((t_rows, r), F32, (tr, hd), lambda j, i: (i, j))
    a_dec, bxv = _rowmap("rg_gates", gates_fwd, (RG_HEADS, t_rows // tr), [head_cols(xc)] + gate_params,
                         [head_out, head_out])
    h_rg = _linear_scan("rg_scan", a_dec, bxv, tr)

    expand = jnp.repeat(jnp.eye(S5_STATE, dtype=F32), S5_GROUP, axis=1)
    b_re2 = s5_b_re[0].reshape(groups, S5_STATE * S5_GROUP)
    b_im2 = s5_b_im[0].reshape(groups, S5_STATE * S5_GROUP)
    log_dt2 = s5_log_dt[0].reshape(groups, 1)
    lbr, lbi, bbr, bbi = _s5_disc("s5_disc", s5_lambda_re[0], s5_lambda_im[0], log_dt2, b_re2, b_im2, expand)

    def to_bblk(m):
        return _blockdiag(m.reshape(nb, GROUPS_PER_BLOCK, S5_STATE, S5_GROUP).transpose(0, 1, 3, 2))

    def to_cblk(m):
        return _blockdiag(m.reshape(nb, GROUPS_PER_BLOCK, S5_GROUP, S5_STATE).transpose(0, 1, 3, 2))

    bblk = jnp.concatenate([to_bblk(bbr), to_bblk(bbi)], axis=-1).astype(BF16)
    cblk = jnp.concatenate([to_cblk(s5_c_re[0]), -to_cblk(s5_c_im[0])], axis=-2).astype(BF16)
    hw = GROUPS_PER_BLOCK * S5_STATE
    lamblk = jnp.concatenate([lbr.reshape(nb, 1, hw), lbi.reshape(nb, 1, hw)], axis=-1)
    col0 = 2 * r // LANES
    y_pre, states = _s5_fwd("s5_fwd", proj, col0, bblk, cblk, lamblk, s5_d, s5w, tr)

    z16, = _rowmap("s5_gelu", lambda yv: _gelu(yv).astype(BF16), (1, t_rows // trn),
                   [_rows(y_pre, trn)], [_o((t_rows, s5w), BF16, trn)])
    tns = _tile(s5w, 1024)

    def glu_fwd(acc, yv, b):
        gl = acc + b
        return _gelu(yv) * _sigmoid(gl), gl

    y_s5, gl = _matmul("s5_glu", 'nn', z16, (tm, s5w), lambda i, j, k: (i, 0),
                       wfull['s5_glu_w'], (s5w, tns), lambda i, j, k: (0, j),
                       (t_rows // tm, s5w // tns, 1), [((t_rows, s5w), F32, (tm, tns), lambda i, j: (i, j))] * 2,
                       glu_fwd, extras=[(y_pre, (tm, tns), lambda i, j: (i, j)), (s5_glu_b, (1, tns), lambda i, j: (0, j))])

    def mix_out(hv, gv, yv, g1, g2):
        return jnp.concatenate([_rms(hv * _gelu(gv), g1), _rms(yv, g2)], axis=1)

    gate_in = (proj, (trn, r), lambda j, i: (i, 1))
    ycat16, = _rowmap("mix_out", lambda *a: mix_out(*a).astype(BF16), (1, t_rows // trn),
                      [_rows(h_rg, trn), gate_in, _rows(y_s5, trn), _full(rg_out_norm), _full(s5_out_norm)],
                      [_o((t_rows, r + s5w), BF16, trn)])
    tnd = _tile(d, 1024)
    h2, = _matmul(
        "out_proj", 'nn', ycat16, (tm, (r + s5w) // 2), lambda i, j, k: (i, k),
        wfull['w_out'], ((r + s5w) // 2, tnd), lambda i, j, k: (k, j),
        (t_rows // tm, d // tnd, 2), [((t_rows, d), F32, (tm, tnd), lambda i, j: (i, j))],
        lambda acc, hin: hin + acc, extras=[(h1, (tm, tnd), lambda i, j: (i, j))])

    h3, sv2, _ = _ffn_fwd(
        "ffn2", h2, ffn2_norm, wfull['ffn2_w_gate'], 0, 0, dims,
        host={'gate': [_gather_comm(slot['ffn2_w_up'])], 'up': [_gather_comm(slot['ffn2_w_down'])]})

    fnorm2 = final_norm.reshape(1, d)

    def head(xv, tv, g):
        i = pl.program_id(1)
        rowi = lax.broadcasted_iota(jnp.int32, (trn, 1), 0) + i * trn
        mask = jnp.where((rowi >= N_META) & (rowi < N_META + seq), 1.0, 0.0)
        out, vjp = jax.vjp(_rms, xv, g)
        err = (out - tv) * mask
        dx, dg = vjp(err * (1.0 / d))
        return dx, dx.astype(BF16), jnp.sum(err * err, axis=0, keepdims=True), dg

    dh3, dh3_16, loss_cols, d_final = _rowmap(
        "loss_head", head, (1, t_rows // trn), [_rows(h3, trn), _rows(tgt, trn), _full(fnorm2)],
        [_o((t_rows, d), F32, trn), _o((t_rows, d), BF16, trn)], [_acc((1, d)), _acc((1, d))])
    loss = lax.psum(0.5 * jnp.sum(loss_cols) / d, ("x", "y", "c"))

    small = {}
    dh2, dh2_16, small['ffn2_norm'], parts, _ = _ffn_bwd(
        "ffn2b", ('ffn2_w_gate', 'ffn2_w_up', 'ffn2_w_down'), dh3, dh3_16, sv2, ffn2_norm, wfull['ffn2_w_gate'],
        dims, pos)

    dycat, = _matmul("d_out_proj", 'nt', dh2_16, (tm, d // 2), lambda i, j, k: (i, k),
                     wfull['w_out'], (tnd, d // 2), lambda i, j, k: (j, k),
                     (t_rows // tm, (r + s5w) // tnd, 2), [((t_rows, r + s5w), F32, (tm, tnd), lambda i, j: (i, j))],
                     lambda acc: acc)
    wo_rows = (r + s5w) // N_CHIPS
    tno = _tile(wo_rows, 1024)
    npo = wo_rows // tno
    tw = _tile(d, 512)
    dw_out, = _matmul("dw_out", 'tn', ycat16, (t_rows, tno), lambda i, j, k: (0, i),
                      dh2_16, (t_rows, tw), lambda i, j, k: (0, j),
                      ((r + s5w) // tno, d // tw, 1),
                      [((N_CHIPS, wo_rows, d), BF16, (None, tno, tw), lambda i, j: (i // npo, i % npo, j))],
                      lambda acc: acc)
    pair_w_out = _pair('w_out', dw_out, pos)

    def mix_out_bwd(hv, gv, yv, dyc, g1, g2):
        _, vjp = jax.vjp(mix_out, hv, gv, yv, g1, g2)
        return vjp(dyc)

    dh_out, dgate_rg, dy_s5, small['rg_out_norm'], small['s5_out_norm'] = _rowmap(
        "mix_out_bwd", mix_out_bwd, (1, t_rows // trn),
        [_rows(h_rg, trn), gate_in, _rows(y_s5, trn), _rows(dycat, trn), _full(rg_out_norm), _full(s5_out_norm)],
        [_o((t_rows, r), F32, trn), _o((t_rows, r), F32, trn), _o((t_rows, s5w), F32, trn)],
        [_acc((1, r)), _acc((1, s5w))])

    def glu_bwd(dyv, yv, glv):
        zv = _gelu(yv)
        sg = _sigmoid(glv)
        dgl = dyv * zv * sg * (1.0 - sg)
        return dyv * sg, dgl.astype(BF16), jnp.sum(dgl, axis=0, keepdims=True)

    dz_dir, dgl16, small['s5_glu_b'] = _rowmap(
        "s5_glu_bwd", glu_bwd, (1, t_rows // trn), [_rows(dy_s5, trn), _rows(y_pre, trn), _rows(gl, trn)],
        [_o((t_rows, s5w), F32, trn), _o((t_rows, s5w), BF16, trn)], [_acc((1, s5w))])

    def dgelu(acc, dzd, yv):
        _, vjp = jax.vjp(_gelu, yv)
        return vjp(acc + dzd)[0]

    dy_pre, = _matmul("s5_dz", 'nt', dgl16, (tm, s5w), lambda i, j, k: (i, 0),
                      wfull['s5_glu_w'], (tns, s5w), lambda i, j, k: (j, 0),
                      (t_rows // tm, s5w // tns, 1), [((t_rows, s5w), F32, (tm, tns), lambda i, j: (i, j))],
                      dgelu, extras=[(dz_dir, (tm, tns), lambda i, j: (i, j)), (y_pre, (tm, tns), lambda i, j: (i, j))])
    gl_rows = s5w // N_CHIPS
    tng = _tile(gl_rows, 1024)
    npg = gl_rows // tng
    dw_glu, = _matmul("dw_glu", 'tn', z16, (t_rows, tng), lambda i, j, k: (0, i),
                      dgl16, (t_rows, tns), lambda i, j, k: (0, j),
                      (s5w // tng, s5w // tns, 1),
                      [((N_CHIPS, gl_rows, s5w), BF16, (None, tng, tns), lambda i, j: (i // npg, i % npg, j))],
                      lambda acc: acc)
    pair_glu = _pair('s5_glu_w', dw_glu, pos)
    du_s5, dbblk, dcblk, dlamblk, small['s5_d'] = _s5_bwd(
        "s5_bwd", dy_pre, proj, col0, states, bblk, cblk, lamblk, s5_d, s5w, tr)

    def from_bblk(m):
        return _blockdiag_take(m, S5_STATE, S5_GROUP).reshape(groups, S5_STATE * S5_GROUP)

    dbbr, dbbi = from_bblk(dbblk[:, :hw]), from_bblk(dbblk[:, hw:])
    dcs = lambda m: _blockdiag_take(m, S5_STATE, S5_GROUP).transpose(0, 1, 3, 2).reshape(1, groups, S5_GROUP, S5_STATE)
    small['s5_c_re'], small['s5_c_im'] = dcs(dcblk[:, :hw]), -dcs(dcblk[:, hw:])
    dlbr, dlbi = dlamblk[:, 0, :hw].reshape(groups, S5_STATE), dlamblk[:, 0, hw:].reshape(groups, S5_STATE)
    g_lre, g_lim, g_ldt, g_bre, g_bim = _s5_disc(
        "s5_disc_bwd", s5_lambda_re[0], s5_lambda_im[0], log_dt2, b_re2, b_im2, expand, cots=(dlbr, dlbi, dbbr, dbbi))
    small['s5_lambda_re'], small['s5_lambda_im'], small['s5_log_dt'] = g_lre[None], g_lim[None], g_ldt.reshape(1, groups)
    small['s5_b_re'] = g_bre.reshape(s5_b_re.shape)
    small['s5_b_im'] = g_bim.reshape(s5_b_im.shape)

    dh_rg = _linear_scan("rg_scan_bwd", a_dec, dh_out, tr, reverse=True, shift_a=True)

    hb = tr // 8

    def gates_bwd(xcv, dhv, hv, hprev, wa, wx, ba, bx, lam):
        i = pl.program_id(1)
        row = lax.broadcasted_iota(jnp.int32, (tr, hd), 0)
        first = jnp.where(i > 0, hprev[7:8, :], 0.0)
        h_prev = jnp.where(row == 0, jnp.broadcast_to(first, (tr, hd)), pltpu.roll(hv, 1, 0))
        x16 = xcv.astype(BF16)
        pre_r = jnp.dot(x16, wa, preferred_element_type=F32) + ba
        pre_i = jnp.dot(x16, wx, preferred_element_type=F32) + bx
        _, vjp = jax.vjp(_rg_gate_math, xcv, pre_r, pre_i, _softplus_neg(lam))
        dxc_, dpr, dpi, dsp = vjp((dhv * h_prev, dhv))
        dpr16, dpi16 = dpr.astype(BF16), dpi.astype(BF16)
        dxc_ = (dxc_ + lax.dot_general(dpr16, wa, _DN['nt'], preferred_element_type=F32)
                + lax.dot_general(dpi16, wx, _DN['nt'], preferred_element_type=F32))
        dwa = lax.dot_general(x16, dpr16, _DN['tn'], preferred_element_type=F32)
        dwx = lax.dot_general(x16, dpi16, _DN['tn'], preferred_element_type=F32)
        dlam = dsp * (-_sigmoid(-lam))
        return (dxc_, dwa, dwx, jnp.sum(dpr, axis=0, keepdims=True), jnp.sum(dpi, axis=0, keepdims=True), dlam)

    head_acc_w = ((RG_HEADS, hd, hd), F32, (None, hd, hd), lambda j, i: (j, 0, 0))
    head_acc_v = ((1, r), F32, (1, hd), lambda j, i: (0, j))
    h_halo = (h_rg, (8, hd), lambda j, i: (jnp.maximum(i * hb - 1, 0), j))
    dxc, d_wa, d_wx, d_ba, d_bx, d_lam = _rowmap(
        "rg_gates_bwd", gates_bwd, (RG_HEADS, t_rows // tr),
        [head_cols(xc), head_cols(dh_rg), head_cols(h_rg), h_halo] + gate_params,
        [head_out], [head_acc_w, head_acc_w, head_acc_v, head_acc_v, head_acc_v])
    small['rg_w_a'], small['rg_w_x'] = d_wa[None], d_wx[None]
    small['rg_b_a'], small['rg_b_x'] = d_ba.reshape(1, RG_HEADS, hd), d_bx.reshape(1, RG_HEADS, hd)
    small['rg_lambda'] = d_lam
    du_rg, d_convw, small['rg_conv_b'] = _conv_bwd("rg_conv_bwd", dxc, proj, conv_w2, r, tr)
    small['rg_conv_w'] = d_convw[None]

    dproj16, = _rowmap("dproj", lambda a, b, c: jnp.concatenate([a, b, c], axis=1).astype(BF16), (1, t_rows // trn),
                       [_rows(du_rg, trn), _rows(dgate_rg, trn), _rows(du_s5, trn)], [_o((t_rows, 3 * r), BF16, trn)])
    dn2, land_w_out = _matmul(
        "d_in_proj", 'nt', dproj16, (tm, pw), lambda i, j, k: (i, k),
        wfull['w_in'], (None, tnd, pw), lambda i, j, k: (k, j, 0),
        (t_rows // tm, d // tnd, N_CHIPS), [((t_rows, d), F32, (tm, tnd), lambda i, j: (i, j))],
        lambda acc: acc, comms=[_scatter_comm(pair_w_out)])
    parts['w_out'] = (pair_w_out, land_w_out)
    tq = _tile(pw, 768)
    nq = pw // tq
    dw_in, = _matmul("dw_in", 'tn', n2, (t_rows, tnd), lambda i, j, k: (0, i),
                     dproj16, (t_rows, tq), lambda i, j, k: (0, j),
                     (d // tnd, N_CHIPS * nq, 1),
                     [((N_CHIPS, d, pw), BF16, (None, tnd, tq), lambda i, j: (j // nq, i, j % nq))],
                     lambda acc: acc)
    pair_w_in = _pair('w_in', dw_in, pos)

    def norm_bwd(xv, dnv, dhv, g):
        _, vjp = jax.vjp(_rms, xv, g)
        dx, dg = vjp(dnv)
        res = dhv + dx
        return res, res.astype(BF16), dg

    dh1, dh1_16, small['mix_norm'] = _rowmap(
        "mix_dnorm", norm_bwd, (1, t_rows // trn), [_rows(h1, trn), _rows(dn2, trn), _rows(dh2, trn), _full(mix_norm)],
        [_o((t_rows, d), F32, trn), _o((t_rows, d), BF16, trn)], [_acc((1, d))])

    dh0, _, small['ffn1_norm'], parts1, got = _ffn_bwd(
        "ffn1b", ('ffn1_w_gate', 'ffn1_w_up', 'ffn1_w_down'), dh1, dh1_16, sv1, ffn1_norm, wfull['ffn1_w_gate'],
        dims, pos, host={'dact': [_scatter_comm(pair_glu), _scatter_comm(pair_w_in)]})
    parts.update(parts1)
    parts['s5_glu_w'] = (pair_glu, got['dact'][0])
    parts['w_in'] = (pair_w_in, got['dact'][1])
    grad_x = dh0[N_META:N_META + seq][None]
    small['meta_tokens'] = dh0[:N_META]
    small['final_norm'] = d_final.reshape(d)

    gbig, prev, half = {}, None, None
    for nm in BIG:
        res = _reduce_partials(f"red_{nm}", parts[nm][0], parts[nm][1], pos, _exchange_comm(half) if prev else None)
        if prev:
            gbig[prev] = res[1]
        prev, half = nm, res[0]
    gbig[prev] = _run_comm(f"exchange_{prev}", _exchange_comm(half))[0]
    gbig = {nm: g2.reshape(-1, g2.shape[2]) for nm, g2 in gbig.items()}

    small_names = [n for n in WEIGHTS if n not in BIG]
    full_shape = {n: small[n].shape for n in small_names}
    flat = jnp.concatenate([small[n].reshape(-1) for n in small_names])
    unit = 8 * 8 * LANES
    total = _round_up(flat.shape[0], unit)
    flat = jnp.concatenate([flat, jnp.zeros((total - flat.shape[0],), F32)])
    red = _all_reduce_small("ar_small", flat.reshape(total // LANES, LANES)).reshape(-1)
    gsmall = {}
    off = 0
    for n in small_names:
        size = math.prod(full_shape[n])
        gsmall[n] = _own_shard(n, red[off:off + size].reshape(full_shape[n]), w[n].shape, chip)
        off += size

    out_g, out_d, out_m, out_v = {}, {}, {}, {}
    for nm in BIG:
        w2 = local2d(w, nm)
        trw = _tile(w2.shape[0], 128 if w2.shape[1] < 4096 else 64, 8)
        res = _adamw(f"adamw_{nm}", w2, gbig[nm], local2d(mom, nm), local2d(var, nm), trw)
        back = (lambda a: jnp.swapaxes(a, 0, 1)[None]) if nm in transposed else (lambda a: a[None])
        out_g[nm], out_d[nm], out_m[nm], out_v[nm] = [back(a) for a in res]

    def pack(tree):
        fl = jnp.concatenate([tree[n].reshape(-1) for n in small_names])
        tot = _round_up(fl.shape[0], 8 * LANES)
        return jnp.concatenate([fl, jnp.zeros((tot - fl.shape[0],), F32)]).reshape(tot // LANES, LANES)

    wp, gp, mp, vp = pack(w), pack(gsmall), pack(mom), pack(var)
    res = _adamw("adamw_small", wp, gp, mp, vp, _tile(wp.shape[0], 512, 8))
    off = 0
    for n in small_names:
        size = math.prod(w[n].shape)
        for dst, src in zip((out_g, out_d, out_m, out_v), res):
            dst[n] = src.reshape(-1)[off:off + size].reshape(w[n].shape)
        off += size

    return (loss, grad_x, *[out_g[n] for n in WEIGHTS], *[out_d[n] for n in WEIGHTS],
            *[out_m[n] for n in WEIGHTS], *[out_v[n] for n in WEIGHTS])


def _gather_small_shards(conv_w, b_a, b_x, gather):
    cw = conv_w.shape[1]
    part = b_a.shape[1]

    def rows8(a):
        a = jnp.concatenate([a, jnp.zeros((a.shape[0], cw - a.shape[1]), F32)], axis=1)
        return jnp.concatenate([a, jnp.zeros((8 - a.shape[0], cw), F32)], axis=0) if a.shape[0] < 8 else a

    local = jnp.concatenate([rows8(conv_w), rows8(b_a), rows8(b_x), jnp.zeros((8, cw), F32)], axis=0)
    full = gather("rg_small", local)
    conv_full = full[:, :CONV_WIDTH].transpose(1, 0, 2).reshape(CONV_WIDTH, N_CHIPS * cw)
    bias = lambda k: full[:, 8 * k:8 * k + RG_HEADS, :part].transpose(1, 0, 2).reshape(1, RG_HEADS * N_CHIPS * part)
    return conv_full, bias(1), bias(2)


def _own_shard(name, g, local_shape, chip):
    if tuple(g.shape) == tuple(local_shape):
        return g
    axis = [k for k, (a, b) in enumerate(zip(g.shape, local_shape)) if a != b][0]
    size = local_shape[axis]
    return lax.dynamic_slice_in_dim(g, chip * size, size, axis=axis)
```

```python
import functools
import math

import jax
import jax.numpy as jnp
from jax import lax
from jax.experimental import pallas as pl
from jax.experimental.pallas import tpu as pltpu

F32 = jnp.float32
BF16 = jnp.bfloat16
MESH = pl.DeviceIdType.MESH

N_META = 16
RG_HEADS = 8
CONV_WIDTH = 4
RG_C = 8.0
S5_GROUP = 16
S5_STATE = 64
GROUPS_PER_BLOCK = 8
EPS = 1e-6
N_CHIPS = 4
LANES = 128
VMEM_LIMIT = 56 * 1024 * 1024

ADAM_LR = 0.001
ADAM_B1 = 0.9
ADAM_B2 = 0.999
ADAM_EPS = 1e-08
ADAM_WD = 0.01
ADAM_STEP = 10

WEIGHTS = ['meta_tokens', 'ffn1_norm', 'ffn1_w_gate', 'ffn1_w_up', 'ffn1_w_down', 'mix_norm', 'w_in', 'rg_conv_w',
           'rg_conv_b', 'rg_w_a', 'rg_b_a', 'rg_w_x', 'rg_b_x', 'rg_lambda', 's5_lambda_re', 's5_lambda_im',
           's5_log_dt', 's5_b_re', 's5_b_im', 's5_c_re', 's5_c_im', 's5_d', 's5_glu_w', 's5_glu_b', 'rg_out_norm',
           's5_out_norm', 'w_out', 'ffn2_norm', 'ffn2_w_gate', 'ffn2_w_up', 'ffn2_w_down', 'final_norm']
BIG = ('ffn1_w_gate', 'ffn1_w_up', 'ffn1_w_down', 'w_in', 's5_glu_w', 'w_out', 'ffn2_w_gate', 'ffn2_w_up',
       'ffn2_w_down')

_DN = {'nn': (((1,), (0,)), ((), ())), 'nt': (((1,), (1,)), ((), ())), 'tn': (((0,), (0,)), ((), ()))}


def _round_up(n, m):
    return (n + m - 1) // m * m


def _tile(n, pref, unit=LANES):
    best = None
    for t in range(unit, min(n, pref) + 1, unit):
        if n % t == 0:
            best = t
    return best if best is not None else n


MXU_WIDTH = 256


def _column_chunks(n):
    first = _round_up(n // 2, MXU_WIDTH)
    return [(0, first), (first, n - first)] if n > 2 * MXU_WIDTH and first < n else [(0, n)]


def _params(sem=None):
    return pltpu.CompilerParams(dimension_semantics=sem, vmem_limit_bytes=VMEM_LIMIT)


def _rms(x, g):
    return x * lax.rsqrt(jnp.mean(x * x, axis=-1, keepdims=True) + EPS) * g


def _sigmoid(x):
    return 0.5 * (jnp.tanh(0.5 * x) + 1.0)


def _gelu(x):
    return 0.5 * x * (1.0 + jnp.tanh(math.sqrt(2.0 / math.pi) * (x + 0.044715 * (x * x * x))))


def _silu(x):
    return x * _sigmoid(x)


def _expm1(x):
    series = x * (1.0 + x * (1.0 / 2) * (1.0 + x * (1.0 / 3) * (1.0 + x * (1.0 / 4) * (1.0 + x * (1.0 / 5) * (1.0 + x * (1.0 / 6))))))
    return jnp.where(jnp.abs(x) < 0.3, series, jnp.exp(x) - 1.0)


def _softplus_neg(lam):
    m = jnp.maximum(-lam, 0.0)
    e = jnp.exp(-jnp.abs(lam))
    w = 1.0 + e
    log1p = jnp.where(w == 1.0, e, jnp.log(w) * (e / jnp.where(w == 1.0, 1.0, w - 1.0)))
    return m + log1p


def _rg_gate_math(xc, pre_r, pre_i, sp):
    r = 1.0 / (1.0 + jnp.exp(-pre_r))
    i = 1.0 / (1.0 + jnp.exp(-pre_i))
    log_a = -RG_C * r * sp
    a = jnp.exp(log_a)
    mult = jnp.sqrt(-_expm1(2.0 * log_a))
    return a, mult * i * xc


class _Comm:
    def __init__(self, arrays, out_shapes, aliased, sems, start, finish, mid=None):
        self.arrays, self.out_shapes, self.aliased, self.sems = arrays, out_shapes, aliased, sems
        self.start, self.finish, self.mid = start, finish, mid


def _matmul(name, mode, a, a_blk, a_map, b, b_blk, b_map, grid, outs, epilogue, extras=(), comms=(), chunks=None):
    ni, nj, nk = grid
    ne, no = len(extras), len(outs)
    sq = lambda blk: tuple(d for d in blk if d is not None)
    ab, bb = sq(a_blk), sq(b_blk)
    acc_shape = {'nn': (ab[0], bb[1]), 'nt': (ab[0], bb[0]), 'tn': (ab[1], bb[1])}[mode]
    n_cin = sum(len(cm.arrays) for cm in comms)
    n_cout = sum(len(cm.out_shapes) for cm in comms)
    n_acc = 1 if nk > 1 else 0

    def comm_refs(refs):
        cin = refs[2 + ne:2 + ne + n_cin]
        cout = refs[2 + ne + n_cin + no:2 + ne + n_cin + no + n_cout]
        csem = refs[2 + ne + n_cin + no + n_cout + n_acc:]
        for cm in comms:
            yield cm, cin[:len(cm.arrays)], cout[:len(cm.out_shapes)], csem[:len(cm.sems)]
            cin, cout, csem = cin[len(cm.arrays):], cout[len(cm.out_shapes):], csem[len(cm.sems):]

    def body(*refs):
        a_ref, b_ref = refs[0], refs[1]
        ex = refs[2:2 + ne]
        out = refs[2 + ne + n_cin:2 + ne + n_cin + no]
        if comms:
            @pl.when((pl.program_id(0) == 0) & (pl.program_id(1) == 0) & (pl.program_id(2) == 0))
            def _():
                for cm, cin, cout, csem in comm_refs(refs):
                    cm.start(cin, cout, csem)

            if any(cm.mid for cm in comms):
                step = (pl.program_id(0) * nj + pl.program_id(1)) * nk + pl.program_id(2)

                @pl.when(step == (3 * ni * nj * nk) // 5)
                def _():
                    for cm, cin, cout, csem in comm_refs(refs):
                        if cm.mid:
                            cm.mid(cin, cout, csem)

        if chunks:
            assert mode == 'nt' and nk == 1
            av = a_ref[...]
            for c0, cw in chunks:
                acc = lax.dot_general(av, b_ref[c0:c0 + cw, :], _DN['nt'], preferred_element_type=F32)
                res = epilogue(acc, *[e[:, c0:c0 + cw] for e in ex])
                for o, r in zip(out, res if isinstance(res, tuple) else (res,)):
                    o[:, c0:c0 + cw] = r.astype(o.dtype)
            if comms:
                @pl.when((pl.program_id(0) == ni - 1) & (pl.program_id(1) == nj - 1))
                def _():
                    for cm, cin, cout, csem in comm_refs(refs):
                        cm.finish(cin, cout, csem)
            return

        part = lax.dot_general(a_ref[...], b_ref[...], _DN[mode], preferred_element_type=F32)

        def finish(acc):
            res = epilogue(acc, *[e[...] for e in ex])
            if not isinstance(res, tuple):
                res = (res,)
            for o, r in zip(out, res):
                o[...] = r.astype(o.dtype)

        if nk == 1:
            finish(part)
        else:
            acc_ref = refs[2 + ne + n_cin + no + n_cout]
            k = pl.program_id(2)

            @pl.when(k == 0)
            def _():
                acc_ref[...] = part

            @pl.when(k > 0)
            def _():
                acc_ref[...] += part

            @pl.when(k == nk - 1)
            def _():
                finish(acc_ref[...])

        if comms:
            @pl.when((pl.program_id(0) == ni - 1) & (pl.program_id(1) == nj - 1) & (pl.program_id(2) == nk - 1))
            def _():
                for cm, cin, cout, csem in comm_refs(refs):
                    cm.finish(cin, cout, csem)

    hbm = pl.BlockSpec(memory_space=pltpu.HBM)
    in_specs = [pl.BlockSpec(a_blk, a_map), pl.BlockSpec(b_blk, b_map)]
    in_specs += [pl.BlockSpec(blk, functools.partial(lambda m, i, j, k: m(i, j), m)) for _, blk, m in extras]
    in_specs += [hbm] * n_cin
    out_specs = [pl.BlockSpec(blk, functools.partial(lambda m, i, j, k: m(i, j), m)) for _, _, blk, m in outs]
    out_specs += [hbm] * n_cout
    aliases, cin_at, cout_at = {}, 2 + ne, no
    for cm in comms:
        if cm.aliased:
            aliases.update({cin_at + k: cout_at + k for k in range(len(cm.arrays))})
        cin_at, cout_at = cin_at + len(cm.arrays), cout_at + len(cm.out_shapes)
    res = pl.pallas_call(
        body, name=name, grid=grid, in_specs=in_specs, out_specs=out_specs,
        out_shape=[jax.ShapeDtypeStruct(s, d) for s, d, _, _ in outs] + [s for cm in comms for s in cm.out_shapes],
        scratch_shapes=([pltpu.VMEM(acc_shape, F32)] if nk > 1 else [])
        + [pltpu.SemaphoreType.DMA((n,)) for cm in comms for n in cm.sems],
        input_output_aliases=aliases,
        compiler_params=_params(("arbitrary",) * 3 if comms else ("parallel", "parallel", "arbitrary")),
    )(a, b, *[e for e, _, _ in extras], *[arr for cm in comms for arr in cm.arrays])
    return res


def _rows(arr, tr, tc=None, col0=0):
    if tc is None:
        return (arr, (tr, arr.shape[1]), lambda j, i: (i, 0))
    return (arr, (tr, tc), lambda j, i: (i, col0 + j))


def _full(arr):
    nd = arr.ndim
    return (arr, arr.shape, lambda j, i: (0,) * nd)


def _cols(arr, tc):
    return (arr, (arr.shape[0], tc), lambda j, i: (0, j))


def _rowmap(name, fn, grid, ins, outs, accs=(), scratch=()):
    nj, ni = grid
    n_in, n_out, n_acc = len(ins), len(outs), len(accs)

    def body(*refs):
        vals = [r[...] for r in refs[:n_in]]
        o_refs = refs[n_in:n_in + n_out]
        a_refs = refs[n_in + n_out:n_in + n_out + n_acc]
        s_refs = refs[n_in + n_out + n_acc:]
        res = fn(*vals, *s_refs)
        if not isinstance(res, tuple):
            res = (res,)
        for o, r in zip(o_refs, res[:n_out]):
            o[...] = r.astype(o.dtype)
        i = pl.program_id(1)
        for a_ref, r in zip(a_refs, res[n_out:]):
            @pl.when(i == 0)
            def _(a_ref=a_ref, r=r):
                a_ref[...] = r.astype(a_ref.dtype)

            @pl.when(i > 0)
            def _(a_ref=a_ref, r=r):
                a_ref[...] += r.astype(a_ref.dtype)

    res = pl.pallas_call(
        body, name=name, grid=grid,
        in_specs=[pl.BlockSpec(blk, m) for _, blk, m in ins],
        out_specs=[pl.BlockSpec(blk, m) for _, _, blk, m in list(outs) + list(accs)],
        out_shape=[jax.ShapeDtypeStruct(s, d) for s, d, _, _ in list(outs) + list(accs)],
        scratch_shapes=list(scratch),
        compiler_params=_params(("parallel", "arbitrary")),
    )(*[a for a, _, _ in ins])
    return res


def _o(shape, dtype, tr, tc=None):
    if tc is None:
        return (shape, dtype, (tr, shape[1]), lambda j, i: (i, 0))
    return (shape, dtype, (tr, tc), lambda j, i: (i, j))


def _acc(shape, tc=None):
    if tc is None:
        nd = len(shape)
        return (shape, F32, shape, lambda j, i: (0,) * nd)
    return (shape, F32, (shape[0], tc), lambda j, i: (0, j))


def _position():
    x, y, c = lax.axis_index("x"), lax.axis_index("y"), lax.axis_index("c")
    return x, y, c


def _relay_gather_comm(buf):
    _, rows, cols = buf.shape
    rh = rows // 2
    rq = rh // 2

    def plan(out_ref, sems):
        x, y, c = _position()
        own, cx, cy, cd = 2 * x + y, 2 * (1 - x) + y, 2 * x + (1 - y), 2 * (1 - x) + (1 - y)
        to_x, to_y, sibling = (1 - x, y, c), (x, 1 - y, c), (x, y, 1 - c)
        half, other = pl.ds(c * rh, rh), pl.ds((1 - c) * rh, rh)
        q0, q1 = pl.ds(c * rh, rq), pl.ds(c * rh + rq, rq)

        def copy(pair, k, chip_index, rows_, to):
            return pltpu.make_async_remote_copy(
                src_ref=out_ref.at[chip_index, rows_], dst_ref=out_ref.at[chip_index, rows_],
                send_sem=sems[pair].at[k], recv_sem=sems[pair + 1].at[k], device_id=to, device_id_type=MESH)

        return dict(
            own_x=copy(0, 0, own, half, to_x), own_y=copy(0, 1, own, half, to_y),
            in_x=copy(0, 0, cx, half, to_x), in_y=copy(0, 1, cy, half, to_y),
            fwd_xy=copy(0, 2, cx, q0, to_y), fwd_yx=copy(0, 3, cy, q1, to_x),
            in_d0=copy(0, 2, cd, q0, to_y), in_d1=copy(0, 3, cd, q1, to_x),
            d2d_out=[copy(2, j, ch, half, sibling) for j, ch in enumerate((cx, cy, cd))],
            d2d_in=[copy(2, j, ch, other, sibling) for j, ch in enumerate((cx, cy, cd))])

    def start(ins, outs, sems):
        p = plan(outs[0], sems)
        p['own_x'].start()
        p['own_y'].start()

    def mid(ins, outs, sems):
        p = plan(outs[0], sems)
        p['in_x'].wait_recv()
        p['fwd_xy'].start()
        p['d2d_out'][0].start()
        p['in_y'].wait_recv()
        p['fwd_yx'].start()
        p['d2d_out'][1].start()

    def finish(ins, outs, sems):
        p = plan(outs[0], sems)
        p['in_d0'].wait_recv()
        p['in_d1'].wait_recv()
        p['d2d_out'][2].start()
        for cp in p['d2d_in']:
            cp.wait_recv()
        for cp in [p['own_x'], p['own_y'], p['fwd_xy'], p['fwd_yx']] + p['d2d_out']:
            cp.wait_send()

    return _Comm([buf], [jax.ShapeDtypeStruct(buf.shape, buf.dtype)], True, [4, 4, 3, 3], start, finish, mid)


def _gather_comm(buf):
    _, rows, cols = buf.shape
    rh = rows // 2
    if buf.dtype == BF16 and rh % 32 == 0:
        return _relay_gather_comm(buf)

    def plan(out_ref, sems):
        x, y, c = _position()
        chips = [(1 - x, y), (x, 1 - y), (1 - x, 1 - y)]

        def copy(pair, j, chip_index, half, to):
            return pltpu.make_async_remote_copy(
                src_ref=out_ref.at[chip_index, half], dst_ref=out_ref.at[chip_index, half],
                send_sem=sems[pair].at[j], recv_sem=sems[pair + 1].at[j], device_id=to, device_id_type=MESH)

        mine, other = pl.ds(c * rh, rh), pl.ds((1 - c) * rh, rh)
        ici_out = [copy(0, j, 2 * x + y, mine, (px, py, c)) for j, (px, py) in enumerate(chips)]
        ici_in = [copy(0, j, 2 * px + py, mine, (px, py, c)) for j, (px, py) in enumerate(chips)]
        d2d_out = [copy(2, j, 2 * px + py, mine, (x, y, 1 - c)) for j, (px, py) in enumerate(chips)]
        d2d_in = [copy(2, j, 2 * px + py, other, (x, y, 1 - c)) for j, (px, py) in enumerate(chips)]
        return ici_out, ici_in, d2d_out, d2d_in

    def start(ins, outs, sems):
        for cp in plan(outs[0], sems)[0]:
            cp.start()

    def finish(ins, outs, sems):
        ici_out, ici_in, d2d_out, d2d_in = plan(outs[0], sems)
        for arrived, fwd in zip(ici_in, d2d_out):
            arrived.wait_recv()
            fwd.start()
        for cp in d2d_in:
            cp.wait_recv()
        for cp in ici_out + d2d_out:
            cp.wait_send()

    return _Comm([buf], [jax.ShapeDtypeStruct(buf.shape, buf.dtype)], True, [3, 3, 3, 3], start, finish)


def _scatter_comm(p):
    _, rh, cols = p.shape

    def plan(p_ref, land_ref, sems):
        x, y, c = _position()
        chips = [(1 - x, y), (x, 1 - y), (1 - x, 1 - y)]

        def copy(j, chip_index, to):
            return pltpu.make_async_remote_copy(
                src_ref=p_ref.at[chip_index], dst_ref=land_ref.at[j],
                send_sem=sems[0].at[j], recv_sem=sems[1].at[j], device_id=to, device_id_type=MESH)

        out = [copy(j, 2 * px + py, (px, py, c)) for j, (px, py) in enumerate(chips)]
        arrive = [copy(j, 2 * x + y, (x, y, c)) for j in range(3)]
        return out, arrive

    def start(ins, outs, sems):
        for cp in plan(ins[0], outs[0], sems)[0]:
            cp.start()

    def finish(ins, outs, sems):
        out, arrive = plan(ins[0], outs[0], sems)
        for cp in arrive:
            cp.wait_recv()
        for cp in out:
            cp.wait_send()

    return _Comm([p], [jax.ShapeDtypeStruct((3, rh, cols), p.dtype)], False, [3, 3], start, finish)


def _run_comm(name, cm):
    n_in, n_out = len(cm.arrays), len(cm.out_shapes)

    def body(*refs):
        ins, outs, sems = refs[:n_in], refs[n_in:n_in + n_out], refs[n_in + n_out:]
        cm.start(ins, outs, sems)
        if cm.mid:
            cm.mid(ins, outs, sems)
        cm.finish(ins, outs, sems)

    hbm = pl.BlockSpec(memory_space=pltpu.HBM)
    return pl.pallas_call(
        body, name=name, in_specs=[hbm] * n_in, out_specs=[hbm] * n_out, out_shape=list(cm.out_shapes),
        input_output_aliases={k: k for k in range(n_in)} if cm.aliased else {},
        scratch_shapes=[pltpu.SemaphoreType.DMA((n,)) for n in cm.sems],
    )(*cm.arrays)


def _swap_comm(g):
    n, rows, cols = g.shape
    rh = rows // 2

    def plan(g_ref, land_ref, sems):
        x, y, c = _position()

        def copy(k, half):
            return pltpu.make_async_remote_copy(
                src_ref=g_ref.at[k, pl.ds(half * rh, rh)], dst_ref=land_ref.at[k],
                send_sem=sems[0].at[k], recv_sem=sems[1].at[k], device_id=(x, y, 1 - c), device_id_type=MESH)

        return [copy(k, 1 - c) for k in range(n)], [copy(k, c) for k in range(n)]

    def start(ins, outs, sems):
        for cp in plan(ins[0], outs[0], sems)[0]:
            cp.start()

    def finish(ins, outs, sems):
        out, arrive = plan(ins[0], outs[0], sems)
        for cp in arrive:
            cp.wait_recv()
        for cp in out:
            cp.wait_send()

    return _Comm([g], [jax.ShapeDtypeStruct((n, rh, cols), g.dtype)], False, [n, n], start, finish)


def _pair_sum(name, g, land, pos):
    n, rows, cols = g.shape
    rh = rows // 2
    tr = _tile(rh, 256, 16)
    nt = rh // tr

    def body(pos_ref, g_ref, l_ref, o_ref):
        o_ref[...] = (g_ref[...].astype(F32) + l_ref[...].astype(F32)).astype(o_ref.dtype)

    grid_spec = pltpu.PrefetchScalarGridSpec(
        num_scalar_prefetch=1, grid=(n, nt),
        in_specs=[pl.BlockSpec((None, tr, cols), lambda k, i, p: (k, p[1] * nt + i, 0)),
                  pl.BlockSpec((None, tr, cols), lambda k, i, p: (k, i, 0))],
        out_specs=pl.BlockSpec((None, tr, cols), lambda k, i, p: (k, i, 0)))
    return pl.pallas_call(
        body, name=name, grid_spec=grid_spec, out_shape=jax.ShapeDtypeStruct((n, rh, cols), g.dtype),
        compiler_params=_params(("parallel", "parallel")),
    )(pos, g, land)


def _reduce_partials(name, p, land, pos, comm=None):
    _, rh, cols = p.shape
    tr = _tile(rh, 256, 16)
    nt = rh // tr
    n_cin = len(comm.arrays) if comm else 0
    n_cout = len(comm.out_shapes) if comm else 0

    def body(*refs):
        p_ref, l_ref, o_ref = refs[1], refs[2], refs[3 + n_cin]
        comm_args = (refs[3:3 + n_cin], refs[4 + n_cin:4 + n_cin + n_cout], refs[4 + n_cin + n_cout:])
        if comm:
            @pl.when(pl.program_id(0) == 0)
            def _():
                comm.start(*comm_args)

        acc = p_ref[...].astype(F32)
        for k in range(3):
            acc = acc + l_ref[k].astype(F32)
        o_ref[...] = acc
        if comm:
            @pl.when(pl.program_id(0) == nt - 1)
            def _():
                comm.finish(*comm_args)

    hbm = pl.BlockSpec(memory_space=pltpu.HBM)
    grid_spec = pltpu.PrefetchScalarGridSpec(
        num_scalar_prefetch=1, grid=(nt,),
        in_specs=[pl.BlockSpec((None, tr, cols), lambda i, p_: (p_[0], i, 0)),
                  pl.BlockSpec((3, tr, cols), lambda i, p_: (0, i, 0))] + [hbm] * n_cin,
        out_specs=[pl.BlockSpec((None, tr, cols), lambda i, p_: (p_[1], i, 0))] + [hbm] * n_cout,
        scratch_shapes=[pltpu.SemaphoreType.DMA((n,)) for n in comm.sems] if comm else [])
    return pl.pallas_call(
        body, name=name, grid_spec=grid_spec,
        out_shape=[jax.ShapeDtypeStruct((2, rh, cols), F32)] + (list(comm.out_shapes) if comm else []),
        input_output_aliases={3 + k: 1 + k for k in range(n_cin)} if comm and comm.aliased else {},
        compiler_params=_params(("arbitrary",) if comm else ("parallel",)),
    )(pos, p, land, *(comm.arrays if comm else []))


def _exchange_comm(buf):
    def plan(out_ref, sems):
        x, y, c = _position()

        def copy(half):
            return pltpu.make_async_remote_copy(
                src_ref=out_ref.at[half], dst_ref=out_ref.at[half], send_sem=sems[0].at[0], recv_sem=sems[1].at[0],
                device_id=(x, y, 1 - c), device_id_type=MESH)

        return copy(c), copy(1 - c)

    def start(ins, outs, sems):
        plan(outs[0], sems)[0].start()

    def finish(ins, outs, sems):
        out, arrive = plan(outs[0], sems)
        arrive.wait_recv()
        out.wait_send()

    return _Comm([buf], [jax.ShapeDtypeStruct(buf.shape, buf.dtype)], True, [1, 1], start, finish)


def _all_reduce_small(name, buf):
    rows = buf.shape[0]
    p = rows // 8

    def body(buf_ref, out_ref, land_ref, red_ref, s1, r1, s2, r2):
        x, y, c = _position()
        me = 4 * x + 2 * y + c

        def peer(r):
            px = 1 - x if (r >> 2) & 1 else x
            py = 1 - y if (r >> 1) & 1 else y
            pc = 1 - c if r & 1 else c
            return (px, py, pc), 4 * px + 2 * py + pc

        firsts = []
        for r in range(1, 8):
            to, d = peer(r)
            cp = pltpu.make_async_remote_copy(
                src_ref=buf_ref.at[pl.ds(d * p, p)], dst_ref=land_ref.at[r - 1],
                send_sem=s1.at[r - 1], recv_sem=r1.at[r - 1], device_id=to, device_id_type=MESH)
            cp.start()
            firsts.append(cp)
        acc = buf_ref[pl.ds(me * p, p), :]
        for r in range(1, 8):
            firsts[r - 1].wait_recv()
            acc = acc + land_ref[r - 1]
        red_ref[...] = acc
        out_ref[pl.ds(me * p, p), :] = acc
        seconds = []
        for r in range(1, 8):
            to, d = peer(r)
            cp = pltpu.make_async_remote_copy(
                src_ref=red_ref, dst_ref=out_ref.at[pl.ds(me * p, p)],
                send_sem=s2.at[r - 1], recv_sem=r2.at[r - 1], device_id=to, device_id_type=MESH)
            cp.start()
            seconds.append(cp)
        for r in range(1, 8):
            to, d = peer(r)
            pltpu.make_async_remote_copy(
                src_ref=red_ref, dst_ref=out_ref.at[pl.ds(d * p, p)],
                send_sem=s2.at[r - 1], recv_sem=r2.at[r - 1], device_id=to, device_id_type=MESH).wait_recv()
        for cp in firsts + seconds:
            cp.wait_send()

    vmem = pl.BlockSpec(memory_space=pltpu.VMEM)
    return pl.pallas_call(
        body, name=name, in_specs=[vmem], out_specs=vmem,
        out_shape=jax.ShapeDtypeStruct(buf.shape, F32),
        scratch_shapes=[pltpu.VMEM((7, p, LANES), F32), pltpu.VMEM((p, LANES), F32),
                        pltpu.SemaphoreType.DMA((7,)), pltpu.SemaphoreType.DMA((7,)),
                        pltpu.SemaphoreType.DMA((7,)), pltpu.SemaphoreType.DMA((7,))],
        compiler_params=pltpu.CompilerParams(vmem_limit_bytes=VMEM_LIMIT),
    )(buf)


def _cast_pad(name, w, rows_to, cols_to, pos):
    rows, cols = w.shape
    tr = _tile(math.gcd(rows, rows_to), 256, 16)
    assert rows % tr == 0 and rows_to % tr == 0, (rows, rows_to, tr)
    n_src = rows // tr

    def body(pos_ref, w_ref, o_ref):
        i = pl.program_id(0)
        if cols_to > cols:
            o_ref[:, cols:] = jnp.zeros((tr, cols_to - cols), BF16)

        @pl.when(i < n_src)
        def _():
            o_ref[:, :cols] = w_ref[...].astype(BF16)

        if rows_to > rows:
            @pl.when(i >= n_src)
            def _():
                o_ref[:, :cols] = jnp.zeros((tr, cols), BF16)

    grid_spec = pltpu.PrefetchScalarGridSpec(
        num_scalar_prefetch=1, grid=(rows_to // tr,),
        in_specs=[pl.BlockSpec((tr, cols), lambda i, p: (jnp.minimum(i, n_src - 1), 0))],
        out_specs=pl.BlockSpec((None, tr, cols_to), lambda i, p: (p[0], i, 0)))
    return pl.pallas_call(
        body, name=name, grid_spec=grid_spec,
        out_shape=jax.ShapeDtypeStruct((N_CHIPS, rows_to, cols_to), BF16),
        compiler_params=_params(("parallel",)),
    )(pos, w)


def _scan_rows(a, b, row, tr, reverse):
    sh = 1
    while sh < tr:
        if reverse:
            valid = row < tr - sh
            a_s = pltpu.roll(a, tr - sh, 0)
            b_s = pltpu.roll(b, tr - sh, 0)
        else:
            valid = row >= sh
            a_s = pltpu.roll(a, sh, 0)
            b_s = pltpu.roll(b, sh, 0)
        b = b + jnp.where(valid, a * b_s, 0.0)
        a = jnp.where(valid, a * a_s, a)
        sh *= 2
    return a, b


def _linear_scan(name, a, b, tr, reverse=False, shift_a=False):
    t_rows, cols = a.shape
    tc = _tile(cols, 512)
    ni, nj = t_rows // tr, cols // tc
    hb = tr // 8

    def rmap(j, i):
        return ((ni - 1 - i) if reverse else i, j)

    def halo_map(j, i):
        ri = ni - 1 - i
        return (jnp.minimum((ri + 1) * hb, t_rows // 8 - 1), j)

    def body(*refs):
        if shift_a:
            a_ref, halo_ref, b_ref, h_ref, carry_ref = refs
        else:
            a_ref, b_ref, h_ref, carry_ref = refs
        i = pl.program_id(1)
        row = lax.broadcasted_iota(jnp.int32, (tr, tc), 0)

        @pl.when(i == 0)
        def _():
            carry_ref[...] = jnp.zeros_like(carry_ref)

        av = a_ref[...]
        if shift_a:
            nxt = jnp.where(i > 0, halo_ref[0:1, :], 0.0)
            av = jnp.where(row == tr - 1, jnp.broadcast_to(nxt, (tr, tc)), pltpu.roll(av, tr - 1, 0))
        pa, hb_ = _scan_rows(av, b_ref[...], row, tr, reverse)
        h = hb_ + pa * carry_ref[0:1, :]
        h_ref[...] = h
        last = h[0:1, :] if reverse else h[tr - 1:tr, :]
        carry_ref[...] = jnp.broadcast_to(last, carry_ref.shape)

    in_specs = [pl.BlockSpec((tr, tc), rmap)]
    args = [a]
    if shift_a:
        in_specs.append(pl.BlockSpec((8, tc), halo_map))
        args.append(a)
    in_specs.append(pl.BlockSpec((tr, tc), rmap))
    args.append(b)
    return pl.pallas_call(
        body, name=name, grid=(nj, ni), in_specs=in_specs, out_specs=pl.BlockSpec((tr, tc), rmap),
        out_shape=jax.ShapeDtypeStruct((t_rows, cols), F32), scratch_shapes=[pltpu.VMEM((8, tc), F32)],
        compiler_params=_params(("parallel", "arbitrary")),
    )(*args)


def _conv_fwd(name, proj, conv_w, conv_b, width, tr):
    t_rows = proj.shape[0]
    ni = t_rows // tr
    hb = tr // 8
    tc = _tile(width, 512)

    def body(u_ref, halo_ref, w_ref, b_ref, o_ref, ext_ref):
        i = pl.program_id(1)
        ext_ref[0:8, :] = jnp.where(i > 0, halo_ref[...], 0.0)
        ext_ref[8:8 + tr, :] = u_ref[...]
        acc = b_ref[...] + w_ref[CONV_WIDTH - 1:CONV_WIDTH, :] * u_ref[...]
        for k in range(CONV_WIDTH - 1):
            acc = acc + w_ref[k:k + 1, :] * ext_ref[pl.ds(8 - (CONV_WIDTH - 1) + k, tr), :]
        o_ref[...] = acc

    return pl.pallas_call(
        body, name=name, grid=(width // tc, ni),
        in_specs=[pl.BlockSpec((tr, tc), lambda j, i: (i, j)),
                  pl.BlockSpec((8, tc), lambda j, i: (jnp.maximum(i * hb - 1, 0), j)),
                  pl.BlockSpec((CONV_WIDTH, tc), lambda j, i: (0, j)),
                  pl.BlockSpec((1, tc), lambda j, i: (0, j))],
        out_specs=pl.BlockSpec((tr, tc), lambda j, i: (i, j)),
        out_shape=jax.ShapeDtypeStruct((t_rows, width), F32),
        scratch_shapes=[pltpu.VMEM((tr + 8, tc), F32)],
        compiler_params=_params(("parallel", "parallel")),
    )(proj, proj, conv_w, conv_b)


def _conv_bwd(name, dxc, proj, conv_w, width, tr):
    t_rows = dxc.shape[0]
    ni = t_rows // tr
    hb = tr // 8
    tc = _tile(width, 512)

    def body(d_ref, dnext_ref, u_ref, uprev_ref, w_ref, du_ref, dw_ref, db_ref, dext_ref, uext_ref):
        i = pl.program_id(1)
        dext_ref[0:tr, :] = d_ref[...]
        dext_ref[tr:tr + 8, :] = jnp.where(i < ni - 1, dnext_ref[...], 0.0)
        uext_ref[0:8, :] = jnp.where(i > 0, uprev_ref[...], 0.0)
        uext_ref[8:8 + tr, :] = u_ref[...]
        d = d_ref[...]
        du = w_ref[CONV_WIDTH - 1:CONV_WIDTH, :] * d
        dws = []
        for k in range(CONV_WIDTH - 1):
            du = du + w_ref[k:k + 1, :] * dext_ref[pl.ds(CONV_WIDTH - 1 - k, tr), :]
            dws.append(jnp.sum(d * uext_ref[pl.ds(8 - (CONV_WIDTH - 1) + k, tr), :], axis=0, keepdims=True))
        dws.append(jnp.sum(d * u_ref[...], axis=0, keepdims=True))
        du_ref[...] = du
        dw = jnp.concatenate(dws, axis=0)
        db = jnp.sum(d, axis=0, keepdims=True)

        @pl.when(i == 0)
        def _():
            dw_ref[...] = dw
            db_ref[...] = db

        @pl.when(i > 0)
        def _():
            dw_ref[...] += dw
            db_ref[...] += db

    return pl.pallas_call(
        body, name=name, grid=(width // tc, ni),
        in_specs=[pl.BlockSpec((tr, tc), lambda j, i: (i, j)),
                  pl.BlockSpec((8, tc), lambda j, i: (jnp.minimum((i + 1) * hb, t_rows // 8 - 1), j)),
                  pl.BlockSpec((tr, tc), lambda j, i: (i, j)),
                  pl.BlockSpec((8, tc), lambda j, i: (jnp.maximum(i * hb - 1, 0), j)),
                  pl.BlockSpec((CONV_WIDTH, tc), lambda j, i: (0, j))],
        out_specs=[pl.BlockSpec((tr, tc), lambda j, i: (i, j)),
                   pl.BlockSpec((CONV_WIDTH, tc), lambda j, i: (0, j)),
                   pl.BlockSpec((1, tc), lambda j, i: (0, j))],
        out_shape=[jax.ShapeDtypeStruct((t_rows, width), F32), jax.ShapeDtypeStruct((CONV_WIDTH, width), F32),
                   jax.ShapeDtypeStruct((1, width), F32)],
        scratch_shapes=[pltpu.VMEM((tr + 8, tc), F32), pltpu.VMEM((tr + 8, tc), F32)],
        compiler_params=_params(("parallel", "arbitrary")),
    )(dxc, dxc, proj, proj, conv_w)


def _s5_disc_math(lam_re, lam_im, log_dt, b_re, b_im, expand):
    dt = jnp.exp(log_dt)
    zr, zi = lam_re * dt, lam_im * dt
    mag = jnp.exp(zr)
    lbr, lbi = mag * jnp.cos(zi), mag * jnp.sin(zi)
    ar, ai = lbr - 1.0, lbi
    den = lam_re * lam_re + lam_im * lam_im
    cr = (ar * lam_re + ai * lam_im) / den
    ci = (ai * lam_re - ar * lam_im) / den
    cre = jnp.dot(cr, expand, precision=lax.Precision.HIGHEST, preferred_element_type=F32)
    cie = jnp.dot(ci, expand, precision=lax.Precision.HIGHEST, preferred_element_type=F32)
    return lbr, lbi, cre * b_re - cie * b_im, cre * b_im + cie * b_re


def _s5_disc(name, lam_re, lam_im, log_dt, b_re, b_im, expand, cots=None):
    ins = [lam_re, lam_im, log_dt, b_re, b_im, expand]
    n_in = len(ins) + (len(cots) if cots else 0)

    def body(*refs):
        vals = [r[...] for r in refs[:6]]
        outs = refs[n_in:]
        if cots is None:
            res = _s5_disc_math(*vals)
        else:
            cv = tuple(r[...] for r in refs[6:n_in])
            _, vjp = jax.vjp(lambda a, b, c, d, e: _s5_disc_math(a, b, c, d, e, vals[5]), *vals[:5])
            res = vjp(cv)
        for o, r in zip(outs, res):
            o[...] = r

    if cots is None:
        shapes = [lam_re.shape, lam_re.shape, b_re.shape, b_re.shape]
    else:
        shapes = [lam_re.shape, lam_re.shape, log_dt.shape, b_re.shape, b_re.shape]
    vmem = pl.BlockSpec(memory_space=pltpu.VMEM)
    return pl.pallas_call(
        body, name=name, in_specs=[vmem] * n_in, out_specs=[vmem] * len(shapes),
        out_shape=[jax.ShapeDtypeStruct(s, F32) for s in shapes],
        compiler_params=pltpu.CompilerParams(vmem_limit_bytes=VMEM_LIMIT),
    )(*ins, *(cots or ()))


def _cmul(ar, ai, br, bi):
    return ar * br - ai * bi, ar * bi + ai * br


SCAN_BLOCK = 8


def _block_scan(br, bi, lr, li, row, rows, reverse):
    pr, pi = lr, li
    pos = row & (SCAN_BLOCK - 1)
    sh = 1
    while sh < SCAN_BLOCK:
        if reverse:
            valid = pos < SCAN_BLOCK - sh
            rs, is_ = pltpu.roll(br, rows - sh, 0), pltpu.roll(bi, rows - sh, 0)
        else:
            valid = pos >= sh
            rs, is_ = pltpu.roll(br, sh, 0), pltpu.roll(bi, sh, 0)
        mr, mi = _cmul(pr, pi, rs, is_)
        br = br + jnp.where(valid, mr, 0.0)
        bi = bi + jnp.where(valid, mi, 0.0)
        pr, pi = _cmul(pr, pi, pr, pi)
        sh *= 2
    return br, bi


def _block_powers(lr, li, hw, reverse):
    row = lax.broadcasted_iota(jnp.int32, (SCAN_BLOCK, hw), 0)
    d = jnp.where(row == (SCAN_BLOCK - 1 if reverse else 0), 1.0, 0.0)
    return _block_scan(jnp.broadcast_to(lr, (SCAN_BLOCK, hw)) * d, jnp.broadcast_to(li, (SCAN_BLOCK, hw)) * d,
                       lr, li, row, SCAN_BLOCK, reverse)


def _chain_blocks(s_ref, pow_ref, carry, rows, hw, reverse):
    nblk = rows // SCAN_BLOCK
    pr, pi = pow_ref[:, :hw], pow_ref[:, hw:]

    def step(n, c):
        b = (nblk - 1 - n) if reverse else n
        at = pl.ds(pl.multiple_of(b * SCAN_BLOCK, SCAN_BLOCK), SCAN_BLOCK)
        blk = s_ref[at, :]
        kr, ki = _cmul(pr, pi, c[0], c[1])
        sr, si = blk[:, :hw] + kr, blk[:, hw:] + ki
        s_ref[at, :] = jnp.concatenate([sr, si], axis=1)
        edge = slice(0, 1) if reverse else slice(SCAN_BLOCK - 1, SCAN_BLOCK)
        return sr[edge, :], si[edge, :]

    return lax.fori_loop(0, nblk, step, carry)


def _s5_fwd(name, proj, col0, bblk, cblk, lamblk, dvec, width, tr):
    t_rows = proj.shape[0]
    ni = t_rows // tr
    nb = width // LANES
    sw = bblk.shape[-1]
    hw = sw // 2

    def body(u_ref, b_ref, c_ref, lam_ref, d_ref, y_ref, s_ref, carry_ref, pow_ref, st_ref):
        i = pl.program_id(1)
        row = lax.broadcasted_iota(jnp.int32, (tr, hw), 0)
        lam = lam_ref[...]
        lr, li = lam[:, :hw], lam[:, hw:]

        @pl.when(i == 0)
        def _():
            carry_ref[...] = jnp.zeros_like(carry_ref)
            zr, zi = _block_powers(lr, li, hw, False)
            pow_ref[:, :hw] = zr
            pow_ref[:, hw:] = zi

        u = u_ref[...]
        bu = jnp.dot(u.astype(BF16), b_ref[...], preferred_element_type=F32)
        sr, si = _block_scan(bu[:, :hw], bu[:, hw:], lr, li, row, tr, False)
        st_ref[:, :hw] = sr
        st_ref[:, hw:] = si
        cr, ci = _chain_blocks(st_ref, pow_ref, (carry_ref[0:1, :hw], carry_ref[0:1, hw:]), tr, hw, False)
        carry_ref[...] = jnp.broadcast_to(jnp.concatenate([cr, ci], axis=1), carry_ref.shape)
        s16 = st_ref[...].astype(BF16)
        s_ref[...] = s16
        y_ref[...] = jnp.dot(s16, c_ref[...], preferred_element_type=F32) + d_ref[...] * u

    return pl.pallas_call(
        body, name=name, grid=(nb, ni),
        in_specs=[pl.BlockSpec((tr, LANES), lambda k, i: (i, col0 + k)),
                  pl.BlockSpec((None, LANES, sw), lambda k, i: (k, 0, 0)),
                  pl.BlockSpec((None, sw, LANES), lambda k, i: (k, 0, 0)),
                  pl.BlockSpec((None, 1, sw), lambda k, i: (k, 0, 0)),
                  pl.BlockSpec((1, LANES), lambda k, i: (0, k))],
        out_specs=[pl.BlockSpec((tr, LANES), lambda k, i: (i, k)),
                   pl.BlockSpec((tr, sw), lambda k, i: (i, k))],
        out_shape=[jax.ShapeDtypeStruct((t_rows, width), F32), jax.ShapeDtypeStruct((t_rows, nb * sw), BF16)],
        scratch_shapes=[pltpu.VMEM((8, sw), F32), pltpu.VMEM((SCAN_BLOCK, sw), F32), pltpu.VMEM((tr, sw), F32)],
        compiler_params=_params(("parallel", "arbitrary")),
    )(proj, bblk, cblk, lamblk, dvec)


def _s5_bwd(name, dy, proj, col0, states, bblk, cblk, lamblk, dvec, width, tr):
    t_rows = dy.shape[0]
    ni = t_rows // tr
    nb = width // LANES
    sw = bblk.shape[-1]
    hw = sw // 2
    hb16 = tr // 16

    def rmap(k, i):
        return (ni - 1 - i, k)

    def body(dy_ref, u_ref, s_ref, sprev_ref, b_ref, c_ref, lam_ref, d_ref,
             du_ref, db_ref, dc_ref, dlam_ref, dd_ref, carry_ref, pow_ref, gt_ref):
        i = pl.program_id(1)
        row = lax.broadcasted_iota(jnp.int32, (tr, hw), 0)
        row_w = lax.broadcasted_iota(jnp.int32, (tr, sw), 0)
        lam = lam_ref[...]
        lr, li = lam[:, :hw], -lam[:, hw:]

        @pl.when(i == 0)
        def _():
            carry_ref[...] = jnp.zeros_like(carry_ref)
            zr, zi = _block_powers(lr, li, hw, True)
            pow_ref[:, :hw] = zr
            pow_ref[:, hw:] = zi

        dyv = dy_ref[...]
        dy16 = dyv.astype(BF16)
        u = u_ref[...]
        gd = lax.dot_general(dy16, c_ref[...], _DN['nt'], preferred_element_type=F32)
        gr, gi = _block_scan(gd[:, :hw], gd[:, hw:], lr, li, row, tr, True)
        gt_ref[:, :hw] = gr
        gt_ref[:, hw:] = gi
        cr, ci = _chain_blocks(gt_ref, pow_ref, (carry_ref[0:1, :hw], carry_ref[0:1, hw:]), tr, hw, True)
        carry_ref[...] = jnp.broadcast_to(jnp.concatenate([cr, ci], axis=1), carry_ref.shape)
        g = gt_ref[...]
        gr, gi = g[:, :hw], g[:, hw:]
        g16 = g.astype(BF16)
        du_ref[...] = lax.dot_general(g16, b_ref[...], _DN['nt'], preferred_element_type=F32) + d_ref[...] * dyv
        s16 = s_ref[...]
        dbv = lax.dot_general(g16, u.astype(BF16), _DN['tn'], preferred_element_type=F32)
        dcv = lax.dot_general(s16, dy16, _DN['tn'], preferred_element_type=F32)
        s32 = s16.astype(F32)
        first = jnp.where(i < ni - 1, sprev_ref[15:16, :].astype(F32), 0.0)
        sp = jnp.where(row_w == 0, jnp.broadcast_to(first, (tr, sw)), pltpu.roll(s32, 1, 0))
        spr, spi = sp[:, :hw], sp[:, hw:]
        dlr = jnp.sum(gr * spr + gi * spi, axis=0, keepdims=True)
        dli = jnp.sum(gi * spr - gr * spi, axis=0, keepdims=True)
        dlam = jnp.concatenate([dlr, dli], axis=1)
        ddv = jnp.sum(dyv * u, axis=0, keepdims=True)

        @pl.when(i == 0)
        def _():
            db_ref[...] = dbv
            dc_ref[...] = dcv
            dlam_ref[...] = dlam
            dd_ref[...] = ddv

        @pl.when(i > 0)
        def _():
            db_ref[...] += dbv
            dc_ref[...] += dcv
            dlam_ref[...] += dlam
            dd_ref[...] += ddv

    return pl.pallas_call(
        body, name=name, grid=(nb, ni),
        in_specs=[pl.BlockSpec((tr, LANES), rmap),
                  pl.BlockSpec((tr, LANES), lambda k, i: (ni - 1 - i, col0 + k)),
                  pl.BlockSpec((tr, sw), rmap),
                  pl.BlockSpec((16, sw), lambda k, i: (jnp.maximum((ni - 1 - i) * hb16 - 1, 0), k)),
                  pl.BlockSpec((None, LANES, sw), lambda k, i: (k, 0, 0)),
                  pl.BlockSpec((None, sw, LANES), lambda k, i: (k, 0, 0)),
                  pl.BlockSpec((None, 1, sw), lambda k, i: (k, 0, 0)),
                  pl.BlockSpec((1, LANES), lambda k, i: (0, k))],
        out_specs=[pl.BlockSpec((tr, LANES), rmap),
                   pl.BlockSpec((None, sw, LANES), lambda k, i: (k, 0, 0)),
                   pl.BlockSpec((None, sw, LANES), lambda k, i: (k, 0, 0)),
                   pl.BlockSpec((None, 1, sw), lambda k, i: (k, 0, 0)),
                   pl.BlockSpec((1, LANES), lambda k, i: (0, k))],
        out_shape=[jax.ShapeDtypeStruct((t_rows, width), F32), jax.ShapeDtypeStruct((nb, sw, LANES), F32),
                   jax.ShapeDtypeStruct((nb, sw, LANES), F32), jax.ShapeDtypeStruct((nb, 1, sw), F32),
                   jax.ShapeDtypeStruct((1, width), F32)],
        scratch_shapes=[pltpu.VMEM((8, sw), F32), pltpu.VMEM((SCAN_BLOCK, sw), F32), pltpu.VMEM((tr, sw), F32)],
        compiler_params=_params(("parallel", "arbitrary")),
    )(dy, proj, states, states, bblk, cblk, lamblk, dvec)


def _blockdiag(m):
    nb, g, p, q = m.shape
    eye = jnp.eye(g, dtype=m.dtype)
    return (m[:, :, :, None, :] * eye[None, :, None, :, None]).reshape(nb, g * p, g * q)


def _blockdiag_take(m, p, q):
    nb = m.shape[0]
    g = GROUPS_PER_BLOCK
    m = m.reshape(nb, g, p, g, q)
    return jnp.stack([m[:, k, :, k, :] for k in range(g)], axis=1)


def _adamw(name, w, g, m, v, tr):
    rows, cols = w.shape
    gcols = g.shape[1]
    c1 = 1.0 - ADAM_B1 ** ADAM_STEP
    c2 = 1.0 - ADAM_B2 ** ADAM_STEP

    def body(w_ref, g_ref, m_ref, v_ref, go_ref, d_ref, mo_ref, vo_ref):
        gv = g_ref[...] if gcols == cols else g_ref[:, :cols]
        mn = ADAM_B1 * m_ref[...] + (1.0 - ADAM_B1) * gv
        vn = ADAM_B2 * v_ref[...] + (1.0 - ADAM_B2) * (gv * gv)
        go_ref[...] = gv
        mo_ref[...] = mn
        vo_ref[...] = vn
        d_ref[...] = -ADAM_LR * ((mn / c1) / (jnp.sqrt(vn / c2) + ADAM_EPS) + ADAM_WD * w_ref[...])

    spec = pl.BlockSpec((tr, cols), lambda i: (i, 0))
    return pl.pallas_call(
        body, name=name, grid=(rows // tr,),
        in_specs=[spec, pl.BlockSpec((tr, gcols), lambda i: (i, 0)), spec, spec],
        out_specs=[spec] * 4, out_shape=[jax.ShapeDtypeStruct((rows, cols), F32)] * 4,
        compiler_params=_params(("parallel",)),
    )(w, g, m, v)


def _ffn_fwd(tag, h, norm, wg, wu, wd, dims, host=None):
    t_rows, d, fp, tm, tr, trn = dims['T'], dims['D'], dims['FP'], dims['TM'], dims['TR'], dims['TRN']
    host = host or {}
    got = {}
    ft = N_CHIPS * fp
    tn = _tile(fp, 1408)
    npb = fp // tn
    n16, = _rowmap(f"{tag}_norm", lambda x, g: _rms(x, g).astype(BF16), (1, t_rows // trn),
                   [_rows(h, trn), _full(norm)], [_o((t_rows, d), BF16, trn)])

    def up(nm, w):
        res = _matmul(f"{tag}_{nm}", 'nt', n16, (tm // 2, d), lambda i, j, k: (i, 0),
                      w, (None, tn, d), lambda i, j, k: (j // npb, j % npb, 0),
                      (2 * t_rows // tm, ft // tn, 1), [((t_rows, ft), BF16, (tm // 2, tn), lambda i, j: (i, j))],
                      lambda acc: acc, comms=host.get(nm, ()), chunks=_column_chunks(tn))
        got[nm] = res[1:]
        return res[0]

    gate = up("gate", wg)
    wu = got['gate'][wu] if isinstance(wu, int) else wu
    upv = up("up", wu)
    wd = got['up'][wd] if isinstance(wd, int) else wd
    tc = _tile(ft, 1408)
    act, = _rowmap(f"{tag}_act", lambda g, u: _silu(g.astype(F32)) * u.astype(F32), (ft // tc, t_rows // tr),
                   [_rows(gate, tr, tc), _rows(upv, tr, tc)], [_o((t_rows, ft), BF16, tr, tc)])
    tnd = _tile(d, 1024)
    res = _matmul(f"{tag}_down", 'nn', act, (tm, fp), lambda i, j, k: (i, k),
                  wd, (None, fp, tnd), lambda i, j, k: (k, 0, j),
                  (t_rows // tm, d // tnd, N_CHIPS), [((t_rows, d), F32, (tm, tnd), lambda i, j: (i, j))],
                  lambda acc, hin: hin + 0.5 * acc, extras=[(h, (tm, tnd), lambda i, j: (i, j))],
                  comms=host.get('down', ()))
    got['down'] = res[1:]
    return res[0], dict(h=h, n16=n16, gate=gate, up=upv, act=act, wu=wu, wd=wd), got


def _pair(nm, g, pos):
    return _pair_sum(f"pair_{nm}", g, _run_comm(f"swap_{nm}", _swap_comm(g))[0], pos)


def _ffn_bwd(tag, names, dh, dh16, saved, norm, wg, dims, pos, host=None):
    t_rows, d, fp, tm, trn = dims['T'], dims['D'], dims['FP'], dims['TM'], dims['TRN']
    host = host or {}
    got = {}
    n_gate, n_up, n_down = names
    wu, wd = saved['wu'], saved['wd']
    ft = N_CHIPS * fp
    tn = _tile(fp, 1408)
    npb = fp // tn
    tm2 = tm // 2

    def act_bwd(acc, g, u):
        da = 0.5 * acc
        g32, u32 = g.astype(F32), u.astype(F32)
        sg = _sigmoid(g32)
        return da * u32 * (sg * (1.0 + g32 * (1.0 - sg))), da * (g32 * sg)

    res = _matmul(
        f"{tag}_dact", 'nt', dh16, (tm2, d), lambda i, j, k: (i, 0),
        wd, (None, tn, d), lambda i, j, k: (j // npb, j % npb, 0),
        (t_rows // tm2, ft // tn, 1),
        [((t_rows, ft), BF16, (tm2, tn), lambda i, j: (i, j))] * 2, act_bwd,
        extras=[(saved['gate'], (tm2, tn), lambda i, j: (i, j)), (saved['up'], (tm2, tn), lambda i, j: (i, j))],
        comms=host.get('dact', ()), chunks=_column_chunks(tn))
    dgate, dup, got['dact'] = res[0], res[1], res[2:]
    tnd = _tile(d, 1024)
    tw = _tile(d, 512)
    res = _matmul(f"{tag}_dwd", 'tn', saved['act'], (t_rows, tn), lambda i, j, k: (0, i),
                  dh16, (t_rows, tw), lambda i, j, k: (0, j),
                  (ft // tn, d // tw, 1),
                  [((N_CHIPS, fp, d), BF16, (None, tn, tw), lambda i, j: (i // npb, i % npb, j))],
                  lambda acc: 0.5 * acc, comms=host.get('dwd', ()))
    dwd, got['dwd'] = res[0], res[1:]

    def dw_up(nm, dact, comms):
        return _matmul(f"{tag}_{nm}", 'tn', dact, (t_rows, tn), lambda i, j, k: (0, i),
                       saved['n16'], (t_rows, tw), lambda i, j, k: (0, j),
                       (ft // tn, d // tw, 1),
                       [((N_CHIPS, fp, d), BF16, (None, tn, tw), lambda i, j: (i // npb, i % npb, j))],
                       lambda acc: acc, comms=comms)

    def dn_part(nm, dact, w, prev, comms):
        extras = [] if prev is None else [(prev, (tm, tnd), lambda i, j: (i, j))]
        return _matmul(f"{tag}_{nm}", 'nn', dact, (tm, fp), lambda i, j, k: (i, k),
                       w, (None, fp, tnd), lambda i, j, k: (k, 0, j),
                       (t_rows // tm, d // tnd, N_CHIPS), [((t_rows, d), F32, (tm, tnd), lambda i, j: (i, j))],
                       (lambda acc: acc) if prev is None else (lambda acc, p: acc + p), extras=extras, comms=comms)

    dwg, swap_d = dw_up("dwg", dgate, [_swap_comm(dwd)])
    pair_d = _pair_sum(f"pair_{n_down}", dwd, swap_d, pos)
    dwu, land_d, swap_g = dw_up("dwu", dup, [_scatter_comm(pair_d), _swap_comm(dwg)])
    pair_g = _pair_sum(f"pair_{n_gate}", dwg, swap_g, pos)
    dn_g, land_g, swap_u = dn_part("dn_gate", dgate, wg, None, [_scatter_comm(pair_g), _swap_comm(dwu)])
    pair_u = _pair_sum(f"pair_{n_up}", dwu, swap_u, pos)
    dn, land_u = dn_part("dn_up", dup, wu, dn_g, [_scatter_comm(pair_u)])

    def norm_bwd(x, dnv, dhv, g):
        _, vjp = jax.vjp(_rms, x, g)
        dx, dg = vjp(dnv)
        res = dhv + dx
        return res, res.astype(BF16), dg

    dh_in, dh_in16, dnorm = _rowmap(
        f"{tag}_dnorm", norm_bwd, (1, t_rows // trn),
        [_rows(saved['h'], trn), _rows(dn, trn), _rows(dh, trn), _full(norm)],
        [_o((t_rows, d), F32, trn), _o((t_rows, d), BF16, trn)], [_acc((1, d))])
    parts = {n_down: (pair_d, land_d), n_gate: (pair_g, land_g), n_up: (pair_u, land_u)}
    return dh_in, dh_in16, dnorm, parts, got


def kernel(x, meta_tokens, ffn1_norm, ffn1_w_gate, ffn1_w_up, ffn1_w_down, mix_norm, w_in, rg_conv_w, rg_conv_b, rg_w_a, rg_b_a, rg_w_x, rg_b_x, rg_lambda, s5_lambda_re, s5_lambda_im, s5_log_dt, s5_b_re, s5_b_im, s5_c_re, s5_c_im, s5_d, s5_glu_w, s5_glu_b, rg_out_norm, s5_out_norm, w_out, ffn2_norm, ffn2_w_gate, ffn2_w_up, ffn2_w_down, final_norm, loss_target, m_meta_tokens, m_ffn1_norm, m_ffn1_w_gate, m_ffn1_w_up, m_ffn1_w_down, m_mix_norm, m_w_in, m_rg_conv_w, m_rg_conv_b, m_rg_w_a, m_rg_b_a, m_rg_w_x, m_rg_b_x, m_rg_lambda, m_s5_lambda_re, m_s5_lambda_im, m_s5_log_dt, m_s5_b_re, m_s5_b_im, m_s5_c_re, m_s5_c_im, m_s5_d, m_s5_glu_w, m_s5_glu_b, m_rg_out_norm, m_s5_out_norm, m_w_out, m_ffn2_norm, m_ffn2_w_gate, m_ffn2_w_up, m_ffn2_w_down, m_final_norm, v_meta_tokens, v_ffn1_norm, v_ffn1_w_gate, v_ffn1_w_up, v_ffn1_w_down, v_mix_norm, v_w_in, v_rg_conv_w, v_rg_conv_b, v_rg_w_a, v_rg_b_a, v_rg_w_x, v_rg_b_x, v_rg_lambda, v_s5_lambda_re, v_s5_lambda_im, v_s5_log_dt, v_s5_b_re, v_s5_b_im, v_s5_c_re, v_s5_c_im, v_s5_d, v_s5_glu_w, v_s5_glu_b, v_rg_out_norm, v_s5_out_norm, v_w_out, v_ffn2_norm, v_ffn2_w_gate, v_ffn2_w_up, v_ffn2_w_down, v_final_norm):
    args = locals()
    w = {n: args[n] for n in WEIGHTS}
    mom = {n: args["m_" + n] for n in WEIGHTS}
    var = {n: args["v_" + n] for n in WEIGHTS}

    seq, d = x.shape[1], x.shape[2]
    f_shard = ffn1_w_gate.shape[2]
    fp = _round_up(f_shard, LANES)
    r = rg_conv_b.shape[1]
    s5w = s5_d.shape[1]
    hd = r // RG_HEADS
    groups = s5w // S5_GROUP
    nb = groups // GROUPS_PER_BLOCK
    t_rows = _round_up(N_META + seq, LANES)
    tm, tr = t_rows // 4, t_rows // 8
    trn = _tile(t_rows, max(t_rows // 32, 16), 16)
    dims = dict(T=t_rows, D=d, FP=fp, TM=tm, TR=tr, TRN=trn)
    cx, cy, cc = lax.axis_index("x"), lax.axis_index("y"), lax.axis_index("c")
    chip = 2 * cx + cy
    pos = jnp.stack([chip, cc]).astype(jnp.int32)

    def gather(nm, buf):
        return _run_comm(f"ag_{nm}", _gather_comm(buf))[0]

    def gather_f32(nm, local):
        buf = lax.dynamic_update_slice(jnp.zeros((N_CHIPS,) + local.shape, F32), local[None], (chip, 0, 0))
        return gather(nm, buf)

    slot = {}
    transposed = ('ffn1_w_gate', 'ffn1_w_up', 'ffn2_w_gate', 'ffn2_w_up')
    local2d = lambda tree, nm: jnp.swapaxes(tree[nm][0], 0, 1) if nm in transposed else tree[nm][0]
    for nm in transposed:
        slot[nm] = _cast_pad(f"cast_{nm}", local2d(w, nm), fp, d, pos)
    for nm in ('ffn1_w_down', 'ffn2_w_down'):
        slot[nm] = _cast_pad(f"cast_{nm}", w[nm][0], fp, d, pos)
    slot['w_in'] = _cast_pad("cast_w_in", w_in[0], d, w_in.shape[2], pos)
    slot['w_out'] = _cast_pad("cast_w_out", w_out[0], w_out.shape[1], d, pos)
    slot['s5_glu_w'] = _cast_pad("cast_glu", s5_glu_w[0], s5_glu_w.shape[1], s5w, pos)
    rgw_local = jnp.concatenate([rg_w_a[0], rg_w_x[0]], axis=0).reshape(2 * RG_HEADS * (hd // N_CHIPS), hd)
    rgw = gather('rgw', _cast_pad("cast_rgw", rgw_local, rgw_local.shape[0], hd, pos))
    rgw = rgw.reshape(N_CHIPS, 2, RG_HEADS, hd // N_CHIPS, hd).transpose(1, 2, 0, 3, 4).reshape(2, RG_HEADS, hd, hd)
    w_a16, w_x16 = rgw[0], rgw[1]
    meta_full = gather_f32('meta', meta_tokens).transpose(1, 0, 2).reshape(N_META, d)
    conv_w2, b_a2, b_x2 = _gather_small_shards(rg_conv_w[0], rg_b_a[0], rg_b_x[0], gather_f32)
    wfull = {'ffn1_w_gate': gather('ffn1_w_gate', slot['ffn1_w_gate'])}

    pad_rows = t_rows - N_META - seq
    h0 = jnp.concatenate([meta_full, x[0], jnp.zeros((pad_rows, d), F32)], axis=0)
    tgt = jnp.concatenate([jnp.zeros((N_META, d), F32), loss_target[0], jnp.zeros((pad_rows, d), F32)], axis=0)

    h1, sv1, got = _ffn_fwd(
        "ffn1", h0, ffn1_norm, wfull['ffn1_w_gate'], 0, 0, dims,
        host={'gate': [_gather_comm(slot['ffn1_w_up']), _gather_comm(slot['s5_glu_w'])],
              'up': [_gather_comm(slot['ffn1_w_down']), _gather_comm(slot['w_in'])],
              'down': [_gather_comm(slot['ffn2_w_gate']), _gather_comm(slot['w_out'])]})
    wfull['s5_glu_w'] = got['gate'][1].reshape(s5w, s5w)
    wfull['w_in'] = got['up'][1]
    wfull['ffn2_w_gate'] = got['down'][0]
    wfull['w_out'] = got['down'][1].reshape(d, d)

    n2, = _rowmap("mix_norm", lambda xv, g: _rms(xv, g).astype(BF16), (1, t_rows // trn),
                  [_rows(h1, trn), _full(mix_norm)], [_o((t_rows, d), BF16, trn)])
    pw = w_in.shape[2]
    tnp = _tile(pw, 1536)
    nppb = pw // tnp
    proj, = _matmul(
        "in_proj", 'nn', n2, (tm, d // 2), lambda i, j, k: (i, k),
        wfull['w_in'], (None, d // 2, tnp), lambda i, j, k: (j // nppb, k, j % nppb),
        (t_rows // tm, N_CHIPS * nppb, 2), [((t_rows, 3 * r), F32, (tm, tnp), lambda i, j: (i, j))],
        lambda acc: acc)

    xc = _conv_fwd("rg_conv", proj, conv_w2, rg_conv_b, r, tr)

    def head_cols(arr):
        return (arr, (tr, hd), lambda j, i: (i, j))

    def head_w(arr):
        return (arr, (None, hd, hd), lambda j, i: (j, 0, 0))

    def head_vec(arr):
        return (arr, (1, hd), lambda j, i: (0, j))

    def gates_fwd(xcv, wa, wx, ba, bx, lam):
        x16 = xcv.astype(BF16)
        pre_r = jnp.dot(x16, wa, preferred_element_type=F32) + ba
        pre_i = jnp.dot(x16, wx, preferred_element_type=F32) + bx
        return _rg_gate_math(xcv, pre_r, pre_i, _softplus_neg(lam))

    gate_params = [head_w(w_a16), head_w(w_x16), head_vec(b_a2), head_vec(b_x2), head_vec(rg_lambda)]
    head_out = ((t_rows, r), F32, (tr, hd), lambda j, i: (i, j))
    a_dec, bxv = _rowmap("rg_gates", gates_fwd, (RG_HEADS, t_rows // tr), [head_cols(xc)] + gate_params,
                         [head_out, head_out])
    h_rg = _linear_scan("rg_scan", a_dec, bxv, tr)

    expand = jnp.repeat(jnp.eye(S5_STATE, dtype=F32), S5_GROUP, axis=1)
    b_re2 = s5_b_re[0].reshape(groups, S5_STATE * S5_GROUP)
    b_im2 = s5_b_im[0].reshape(groups, S5_STATE * S5_GROUP)
    log_dt2 = s5_log_dt[0].reshape(groups, 1)
    lbr, lbi, bbr, bbi = _s5_disc("s5_disc", s5_lambda_re[0], s5_lambda_im[0], log_dt2, b_re2, b_im2, expand)

    def to_bblk(m):
        return _blockdiag(m.reshape(nb, GROUPS_PER_BLOCK, S5_STATE, S5_GROUP).transpose(0, 1, 3, 2))

    def to_cblk(m):
        return _blockdiag(m.reshape(nb, GROUPS_PER_BLOCK, S5_GROUP, S5_STATE).transpose(0, 1, 3, 2))

    bblk = jnp.concatenate([to_bblk(bbr), to_bblk(bbi)], axis=-1).astype(BF16)
    cblk = jnp.concatenate([to_cblk(s5_c_re[0]), -to_cblk(s5_c_im[0])], axis=-2).astype(BF16)
    hw = GROUPS_PER_BLOCK * S5_STATE
    lamblk = jnp.concatenate([lbr.reshape(nb, 1, hw), lbi.reshape(nb, 1, hw)], axis=-1)
    col0 = 2 * r // LANES
    y_pre, states = _s5_fwd("s5_fwd", proj, col0, bblk, cblk, lamblk, s5_d, s5w, tr)

    z16, = _rowmap("s5_gelu", lambda yv: _gelu(yv).astype(BF16), (1, t_rows // trn),
                   [_rows(y_pre, trn)], [_o((t_rows, s5w), BF16, trn)])
    tns = _tile(s5w, 1024)

    def glu_fwd(acc, yv, b):
        gl = acc + b
        return _gelu(yv) * _sigmoid(gl), gl

    y_s5, gl = _matmul("s5_glu", 'nn', z16, (tm, s5w), lambda i, j, k: (i, 0),
                       wfull['s5_glu_w'], (s5w, tns), lambda i, j, k: (0, j),
                       (t_rows // tm, s5w // tns, 1), [((t_rows, s5w), F32, (tm, tns), lambda i, j: (i, j))] * 2,
                       glu_fwd, extras=[(y_pre, (tm, tns), lambda i, j: (i, j)), (s5_glu_b, (1, tns), lambda i, j: (0, j))])

    def mix_out(hv, gv, yv, g1, g2):
        return jnp.concatenate([_rms(hv * _gelu(gv), g1), _rms(yv, g2)], axis=1)

    gate_in = (proj, (trn, r), lambda j, i: (i, 1))
    ycat16, = _rowmap("mix_out", lambda *a: mix_out(*a).astype(BF16), (1, t_rows // trn),
                      [_rows(h_rg, trn), gate_in, _rows(y_s5, trn), _full(rg_out_norm), _full(s5_out_norm)],
                      [_o((t_rows, r + s5w), BF16, trn)])
    tnd = _tile(d, 1024)
    h2, = _matmul(
        "out_proj", 'nn', ycat16, (tm, (r + s5w) // 2), lambda i, j, k: (i, k),
        wfull['w_out'], ((r + s5w) // 2, tnd), lambda i, j, k: (k, j),
        (t_rows // tm, d // tnd, 2), [((t_rows, d), F32, (tm, tnd), lambda i, j: (i, j))],
        lambda acc, hin: hin + acc, extras=[(h1, (tm, tnd), lambda i, j: (i, j))])

    h3, sv2, _ = _ffn_fwd(
        "ffn2", h2, ffn2_norm, wfull['ffn2_w_gate'], 0, 0, dims,
        host={'gate': [_gather_comm(slot['ffn2_w_up'])], 'up': [_gather_comm(slot['ffn2_w_down'])]})

    fnorm2 = final_norm.reshape(1, d)

    def head(xv, tv, g):
        i = pl.program_id(1)
        rowi = lax.broadcasted_iota(jnp.int32, (trn, 1), 0) + i * trn
        mask = jnp.where((rowi >= N_META) & (rowi < N_META + seq), 1.0, 0.0)
        out, vjp = jax.vjp(_rms, xv, g)
        err = (out - tv) * mask
        dx, dg = vjp(err * (1.0 / d))
        return dx, dx.astype(BF16), jnp.sum(err * err, axis=0, keepdims=True), dg

    dh3, dh3_16, loss_cols, d_final = _rowmap(
        "loss_head", head, (1, t_rows // trn), [_rows(h3, trn), _rows(tgt, trn), _full(fnorm2)],
        [_o((t_rows, d), F32, trn), _o((t_rows, d), BF16, trn)], [_acc((1, d)), _acc((1, d))])
    loss = lax.psum(0.5 * jnp.sum(loss_cols) / d, ("x", "y", "c"))

    small = {}
    dh2, dh2_16, small['ffn2_norm'], parts, _ = _ffn_bwd(
        "ffn2b", ('ffn2_w_gate', 'ffn2_w_up', 'ffn2_w_down'), dh3, dh3_16, sv2, ffn2_norm, wfull['ffn2_w_gate'],
        dims, pos)

    dycat, = _matmul("d_out_proj", 'nt', dh2_16, (tm, d // 2), lambda i, j, k: (i, k),
                     wfull['w_out'], (tnd, d // 2), lambda i, j, k: (j, k),
                     (t_rows // tm, (r + s5w) // tnd, 2), [((t_rows, r + s5w), F32, (tm, tnd), lambda i, j: (i, j))],
                     lambda acc: acc)
    wo_rows = (r + s5w) // N_CHIPS
    tno = _tile(wo_rows, 1024)
    npo = wo_rows // tno
    tw = _tile(d, 512)
    dw_out, = _matmul("dw_out", 'tn', ycat16, (t_rows, tno), lambda i, j, k: (0, i),
                      dh2_16, (t_rows, tw), lambda i, j, k: (0, j),
                      ((r + s5w) // tno, d // tw, 1),
                      [((N_CHIPS, wo_rows, d), BF16, (None, tno, tw), lambda i, j: (i // npo, i % npo, j))],
                      lambda acc: acc)
    pair_w_out = _pair('w_out', dw_out, pos)

    def mix_out_bwd(hv, gv, yv, dyc, g1, g2):
        _, vjp = jax.vjp(mix_out, hv, gv, yv, g1, g2)
        return vjp(dyc)

    dh_out, dgate_rg, dy_s5, small['rg_out_norm'], small['s5_out_norm'] = _rowmap(
        "mix_out_bwd", mix_out_bwd, (1, t_rows // trn),
        [_rows(h_rg, trn), gate_in, _rows(y_s5, trn), _rows(dycat, trn), _full(rg_out_norm), _full(s5_out_norm)],
        [_o((t_rows, r), F32, trn), _o((t_rows, r), F32, trn), _o((t_rows, s5w), F32, trn)],
        [_acc((1, r)), _acc((1, s5w))])

    def glu_bwd(dyv, yv, glv):
        zv = _gelu(yv)
        sg = _sigmoid(glv)
        dgl = dyv * zv * sg * (1.0 - sg)
        return dyv * sg, dgl.astype(BF16), jnp.sum(dgl, axis=0, keepdims=True)

    dz_dir, dgl16, small['s5_glu_b'] = _rowmap(
        "s5_glu_bwd", glu_bwd, (1, t_rows // trn), [_rows(dy_s5, trn), _rows(y_pre, trn), _rows(gl, trn)],
        [_o((t_rows, s5w), F32, trn), _o((t_rows, s5w), BF16, trn)], [_acc((1, s5w))])

    def dgelu(acc, dzd, yv):
        _, vjp = jax.vjp(_gelu, yv)
        return vjp(acc + dzd)[0]

    dy_pre, = _matmul("s5_dz", 'nt', dgl16, (tm, s5w), lambda i, j, k: (i, 0),
                      wfull['s5_glu_w'], (tns, s5w), lambda i, j, k: (j, 0),
                      (t_rows // tm, s5w // tns, 1), [((t_rows, s5w), F32, (tm, tns), lambda i, j: (i, j))],
                      dgelu, extras=[(dz_dir, (tm, tns), lambda i, j: (i, j)), (y_pre, (tm, tns), lambda i, j: (i, j))])
    gl_rows = s5w // N_CHIPS
    tng = _tile(gl_rows, 1024)
    npg = gl_rows // tng
    dw_glu, = _matmul("dw_glu", 'tn', z16, (t_rows, tng), lambda i, j, k: (0, i),
                      dgl16, (t_rows, tns), lambda i, j, k: (0, j),
                      (s5w // tng, s5w // tns, 1),
                      [((N_CHIPS, gl_rows, s5w), BF16, (None, tng, tns), lambda i, j: (i // npg, i % npg, j))],
                      lambda acc: acc)
    pair_glu = _pair('s5_glu_w', dw_glu, pos)
    du_s5, dbblk, dcblk, dlamblk, small['s5_d'] = _s5_bwd(
        "s5_bwd", dy_pre, proj, col0, states, bblk, cblk, lamblk, s5_d, s5w, tr)

    def from_bblk(m):
        return _blockdiag_take(m, S5_STATE, S5_GROUP).reshape(groups, S5_STATE * S5_GROUP)

    dbbr, dbbi = from_bblk(dbblk[:, :hw]), from_bblk(dbblk[:, hw:])
    dcs = lambda m: _blockdiag_take(m, S5_STATE, S5_GROUP).transpose(0, 1, 3, 2).reshape(1, groups, S5_GROUP, S5_STATE)
    small['s5_c_re'], small['s5_c_im'] = dcs(dcblk[:, :hw]), -dcs(dcblk[:, hw:])
    dlbr, dlbi = dlamblk[:, 0, :hw].reshape(groups, S5_STATE), dlamblk[:, 0, hw:].reshape(groups, S5_STATE)
    g_lre, g_lim, g_ldt, g_bre, g_bim = _s5_disc(
        "s5_disc_bwd", s5_lambda_re[0], s5_lambda_im[0], log_dt2, b_re2, b_im2, expand, cots=(dlbr, dlbi, dbbr, dbbi))
    small['s5_lambda_re'], small['s5_lambda_im'], small['s5_log_dt'] = g_lre[None], g_lim[None], g_ldt.reshape(1, groups)
    small['s5_b_re'] = g_bre.reshape(s5_b_re.shape)
    small['s5_b_im'] = g_bim.reshape(s5_b_im.shape)

    dh_rg = _linear_scan("rg_scan_bwd", a_dec, dh_out, tr, reverse=True, shift_a=True)

    hb = tr // 8

    def gates_bwd(xcv, dhv, hv, hprev, wa, wx, ba, bx, lam):
        i = pl.program_id(1)
        row = lax.broadcasted_iota(jnp.int32, (tr, hd), 0)
        first = jnp.where(i > 0, hprev[7:8, :], 0.0)
        h_prev = jnp.where(row == 0, jnp.broadcast_to(first, (tr, hd)), pltpu.roll(hv, 1, 0))
        x16 = xcv.astype(BF16)
        pre_r = jnp.dot(x16, wa, preferred_element_type=F32) + ba
        pre_i = jnp.dot(x16, wx, preferred_element_type=F32) + bx
        _, vjp = jax.vjp(_rg_gate_math, xcv, pre_r, pre_i, _softplus_neg(lam))
        dxc_, dpr, dpi, dsp = vjp((dhv * h_prev, dhv))
        dpr16, dpi16 = dpr.astype(BF16), dpi.astype(BF16)
        dxc_ = (dxc_ + lax.dot_general(dpr16, wa, _DN['nt'], preferred_element_type=F32)
                + lax.dot_general(dpi16, wx, _DN['nt'], preferred_element_type=F32))
        dwa = lax.dot_general(x16, dpr16, _DN['tn'], preferred_element_type=F32)
        dwx = lax.dot_general(x16, dpi16, _DN['tn'], preferred_element_type=F32)
        dlam = dsp * (-_sigmoid(-lam))
        return (dxc_, dwa, dwx, jnp.sum(dpr, axis=0, keepdims=True), jnp.sum(dpi, axis=0, keepdims=True), dlam)

    head_acc_w = ((RG_HEADS, hd, hd), F32, (None, hd, hd), lambda j, i: (j, 0, 0))
    head_acc_v = ((1, r), F32, (1, hd), lambda j, i: (0, j))
    h_halo = (h_rg, (8, hd), lambda j, i: (jnp.maximum(i * hb - 1, 0), j))
    dxc, d_wa, d_wx, d_ba, d_bx, d_lam = _rowmap(
        "rg_gates_bwd", gates_bwd, (RG_HEADS, t_rows // tr),
        [head_cols(xc), head_cols(dh_rg), head_cols(h_rg), h_halo] + gate_params,
        [head_out], [head_acc_w, head_acc_w, head_acc_v, head_acc_v, head_acc_v])
    small['rg_w_a'], small['rg_w_x'] = d_wa[None], d_wx[None]
    small['rg_b_a'], small['rg_b_x'] = d_ba.reshape(1, RG_HEADS, hd), d_bx.reshape(1, RG_HEADS, hd)
    small['rg_lambda'] = d_lam
    du_rg, d_convw, small['rg_conv_b'] = _conv_bwd("rg_conv_bwd", dxc, proj, conv_w2, r, tr)
    small['rg_conv_w'] = d_convw[None]

    dproj16, = _rowmap("dproj", lambda a, b, c: jnp.concatenate([a, b, c], axis=1).astype(BF16), (1, t_rows // trn),
                       [_rows(du_rg, trn), _rows(dgate_rg, trn), _rows(du_s5, trn)], [_o((t_rows, 3 * r), BF16, trn)])
    dn2, land_w_out = _matmul(
        "d_in_proj", 'nt', dproj16, (tm, pw), lambda i, j, k: (i, k),
        wfull['w_in'], (None, tnd, pw), lambda i, j, k: (k, j, 0),
        (t_rows // tm, d // tnd, N_CHIPS), [((t_rows, d), F32, (tm, tnd), lambda i, j: (i, j))],
        lambda acc: acc, comms=[_scatter_comm(pair_w_out)])
    parts['w_out'] = (pair_w_out, land_w_out)
    tq = _tile(pw, 768)
    nq = pw // tq
    dw_in, = _matmul("dw_in", 'tn', n2, (t_rows, tnd), lambda i, j, k: (0, i),
                     dproj16, (t_rows, tq), lambda i, j, k: (0, j),
                     (d // tnd, N_CHIPS * nq, 1),
                     [((N_CHIPS, d, pw), BF16, (None, tnd, tq), lambda i, j: (j // nq, i, j % nq))],
                     lambda acc: acc)
    pair_w_in = _pair('w_in', dw_in, pos)

    def norm_bwd(xv, dnv, dhv, g):
        _, vjp = jax.vjp(_rms, xv, g)
        dx, dg = vjp(dnv)
        res = dhv + dx
        return res, res.astype(BF16), dg

    dh1, dh1_16, small['mix_norm'] = _rowmap(
        "mix_dnorm", norm_bwd, (1, t_rows // trn), [_rows(h1, trn), _rows(dn2, trn), _rows(dh2, trn), _full(mix_norm)],
        [_o((t_rows, d), F32, trn), _o((t_rows, d), BF16, trn)], [_acc((1, d))])

    dh0, _, small['ffn1_norm'], parts1, got = _ffn_bwd(
        "ffn1b", ('ffn1_w_gate', 'ffn1_w_up', 'ffn1_w_down'), dh1, dh1_16, sv1, ffn1_norm, wfull['ffn1_w_gate'],
        dims, pos, host={'dact': [_scatter_comm(pair_glu), _scatter_comm(pair_w_in)]})
    parts.update(parts1)
    parts['s5_glu_w'] = (pair_glu, got['dact'][0])
    parts['w_in'] = (pair_w_in, got['dact'][1])
    grad_x = dh0[N_META:N_META + seq][None]
    small['meta_tokens'] = dh0[:N_META]
    small['final_norm'] = d_final.reshape(d)

    gbig, prev, half = {}, None, None
    for nm in BIG:
        res = _reduce_partials(f"red_{nm}", parts[nm][0], parts[nm][1], pos, _exchange_comm(half) if prev else None)
        if prev:
            gbig[prev] = res[1]
        prev, half = nm, res[0]
    gbig[prev] = _run_comm(f"exchange_{prev}", _exchange_comm(half))[0]
    gbig = {nm: g2.reshape(-1, g2.shape[2]) for nm, g2 in gbig.items()}

    small_names = [n for n in WEIGHTS if n not in BIG]
    full_shape = {n: small[n].shape for n in small_names}
    flat = jnp.concatenate([small[n].reshape(-1) for n in small_names])
    unit = 8 * 8 * LANES
    total = _round_up(flat.shape[0], unit)
    flat = jnp.concatenate([flat, jnp.zeros((total - flat.shape[0],), F32)])
    red = _all_reduce_small("ar_small", flat.reshape(total // LANES, LANES)).reshape(-1)
    gsmall = {}
    off = 0
    for n in small_names:
        size = math.prod(full_shape[n])
        gsmall[n] = _own_shard(n, red[off:off + size].reshape(full_shape[n]), w[n].shape, chip)
        off += size

    out_g, out_d, out_m, out_v = {}, {}, {}, {}
    for nm in BIG:
        w2 = local2d(w, nm)
        trw = _tile(w2.shape[0], 128 if w2.shape[1] < 4096 else 64, 8)
        res = _adamw(f"adamw_{nm}", w2, gbig[nm], local2d(mom, nm), local2d(var, nm), trw)
        back = (lambda a: jnp.swapaxes(a, 0, 1)[None]) if nm in transposed else (lambda a: a[None])
        out_g[nm], out_d[nm], out_m[nm], out_v[nm] = [back(a) for a in res]

    def pack(tree):
        fl = jnp.concatenate([tree[n].reshape(-1) for n in small_names])
        tot = _round_up(fl.shape[0], 8 * LANES)
        return jnp.concatenate([fl, jnp.zeros((tot - fl.shape[0],), F32)]).reshape(tot // LANES, LANES)

    wp, gp, mp, vp = pack(w), pack(gsmall), pack(mom), pack(var)
    res = _adamw("adamw_small", wp, gp, mp, vp, _tile(wp.shape[0], 512, 8))
    off = 0
    for n in small_names:
        size = math.prod(w[n].shape)
        for dst, src in zip((out_g, out_d, out_m, out_v), res):
            dst[n] = src.reshape(-1)[off:off + size].reshape(w[n].shape)
        off += size

    return (loss, grad_x, *[out_g[n] for n in WEIGHTS], *[out_d[n] for n in WEIGHTS],
            *[out_m[n] for n in WEIGHTS], *[out_v[n] for n in WEIGHTS])


def _gather_small_shards(conv_w, b_a, b_x, gather):
    cw = conv_w.shape[1]
    part = b_a.shape[1]

    def rows8(a):
        a = jnp.concatenate([a, jnp.zeros((a.shape[0], cw - a.shape[1]), F32)], axis=1)
        return jnp.concatenate([a, jnp.zeros((8 - a.shape[0], cw), F32)], axis=0) if a.shape[0] < 8 else a

    local = jnp.concatenate([rows8(conv_w), rows8(b_a), rows8(b_x), jnp.zeros((8, cw), F32)], axis=0)
    full = gather("rg_small", local)
    conv_full = full[:, :CONV_WIDTH].transpose(1, 0, 2).reshape(CONV_WIDTH, N_CHIPS * cw)
    bias = lambda k: full[:, 8 * k:8 * k + RG_HEADS, :part].transpose(1, 0, 2).reshape(1, RG_HEADS * N_CHIPS * part)
    return conv_full, bias(1), bias(2)


def _own_shard(name, g, local_shape, chip):
    if tuple(g.shape) == tuple(local_shape):
        return g
    axis = [k for k, (a, b) in enumerate(zip(g.shape, local_shape)) if a != b][0]
    size = local_shape[axis]
    return lax.dynamic_slice_in_dim(g, chip * size, size, axis=axis)
```
